```python
import jax, jax.numpy as jnp
from jax import lax
import numpy as np

D_MODEL = 2048
BATCH = 8
SEQ = 4096
DEPTH = 4

N_A = DEPTH // 2
N_B = DEPTH - N_A
POOL_WINDOWS = (2, 4, 8, 16)
N_POOL_GROUPS = len(POOL_WINDOWS)
POOL_GROUP_DIM = D_MODEL // N_POOL_GROUPS
QK_NOPE_DIM = 128
QK_ROPE_DIM = 64
V_HEAD_DIM = 128
N_HEADS = D_MODEL // V_HEAD_DIM
KV_LORA_RANK = D_MODEL // 4
Q_LORA_RANK = ((1536 * D_MODEL // 7168 + 127) // 128) * 128
Q_HEAD_DIM = QK_NOPE_DIM + QK_ROPE_DIM
SM_SCALE = Q_HEAD_DIM ** -0.5
ROPE_THETA = 10000.0
Q_BLOCK = 128
FFN_DIM = ((8 * D_MODEL // 3 + 255) // 256) * 256
N_MOD = 6
EPS = 1e-6

kernel_name = "yoco_pool_mla_adaln_trunk"


def rmsnorm(x, g):
    x32 = x.astype(jnp.float32)
    y = x32 * lax.rsqrt(jnp.mean(x32 * x32, axis=-1, keepdims=True) + EPS)
    return (y * g.astype(jnp.float32)).astype(x.dtype)


def modulate(h, shift, scale):
    return h * (1.0 + scale[:, None, :]) + shift[:, None, :]


def rope_tables(positions):
    inv_freq = 1.0 / (ROPE_THETA ** (jnp.arange(0, QK_ROPE_DIM, 2, dtype=jnp.float32) / QK_ROPE_DIM))
    ang = positions.astype(jnp.float32)[..., None] * inv_freq
    return jnp.cos(ang), jnp.sin(ang)


def apply_rope(t, cos, sin):
    half = t.shape[-1] // 2
    t1, t2 = t[..., :half], t[..., half:]
    return jnp.concatenate([t1 * cos - t2 * sin, t2 * cos + t1 * sin], axis=-1).astype(t.dtype)


def pool_mixer(h, w_grp, scale):
    B, S, D = h.shape
    hf = h.astype(jnp.float32)
    csum = jnp.cumsum(hf, axis=1)
    t = jnp.arange(S)
    outs = []
    for g, w in enumerate(POOL_WINDOWS):
        lo, hi = g * POOL_GROUP_DIM, (g + 1) * POOL_GROUP_DIM
        cs = csum[..., lo:hi]
        lag = jnp.pad(cs, ((0, 0), (w, 0), (0, 0)))[:, :S]
        cnt = jnp.minimum(t + 1, w).astype(jnp.float32)[None, :, None]
        outs.append((cs - lag) / cnt - hf[..., lo:hi])
    d = jnp.stack(outs, axis=2).astype(h.dtype)
    y = jnp.einsum('bsgc,gcd->bsgd', d, w_grp).reshape(B, S, D)
    return y * scale


def swiglu(h, w_gate, w_up, w_down):
    return (jax.nn.silu(h @ w_gate) * (h @ w_up)) @ w_down


def shared_kv(h, w_dkv, kv_norm, w_uk, w_uv, w_kr, cos, sin):
    B, S, _ = h.shape
    ckv = rmsnorm(h @ w_dkv, kv_norm)
    k_nope = (ckv @ w_uk).reshape(B, S, N_HEADS, QK_NOPE_DIM)
    v = (ckv @ w_uv).reshape(B, S, N_HEADS, V_HEAD_DIM)
    k_rope = apply_rope(h @ w_kr, cos, sin)
    return k_nope, k_rope, v


def causal_mla_attention(q_nope, q_rope, k_nope, k_rope, v):
    B, S, H, _ = q_nope.shape
    nblk = S // Q_BLOCK
    key_idx = jnp.arange(S)

    def to_blocks(t):
        return jnp.moveaxis(t.reshape((B, nblk, Q_BLOCK) + t.shape[2:]), 1, 0)

    def one_block(args):
        qn, qr, blk = args
        s = (jnp.einsum('bqhd,bkhd->bhqk', qn, k_nope, preferred_element_type=jnp.float32)
             + jnp.einsum('bqhr,bkr->bhqk', qr, k_rope, preferred_element_type=jnp.float32))
        q_idx = blk * Q_BLOCK + jnp.arange(Q_BLOCK)
        mask = key_idx[None, :] <= q_idx[:, None]
        p = jax.nn.softmax(jnp.where(mask, s * SM_SCALE, -jnp.inf), axis=-1)
        return jnp.einsum('bhqk,bkhd->bqhd', p.astype(v.dtype), v)

    o = lax.map(one_block, (to_blocks(q_nope), to_blocks(q_rope), jnp.arange(nblk)))
    return jnp.moveaxis(o, 0, 1).reshape(B, S, H * V_HEAD_DIM)


def mla_mixer(h, kv, w_dq, q_norm, w_uq, w_o, cos, sin):
    B, S, _ = h.shape
    cq = rmsnorm(h @ w_dq, q_norm)
    q = (cq @ w_uq).reshape(B, S, N_HEADS, Q_HEAD_DIM)
    q_nope = q[..., :QK_NOPE_DIM]
    q_rope = apply_rope(q[..., QK_NOPE_DIM:], cos[:, :, None, :], sin[:, :, None, :])
    k_nope, k_rope, v = kv
    o = causal_mla_attention(q_nope, q_rope, k_nope, k_rope, v)
    return o @ w_o


def _fwd_setup_inputs(seed: int = 0) -> dict:
    key = jax.random.key(seed)
    ks = jax.random.split(key, 32)
    f32 = jnp.float32

    def nrm(k, shape, std):
        return jax.random.normal(k, shape, f32) * std

    def gain(k, shape):
        return 1.0 + 0.02 * jax.random.normal(k, shape, f32)

    D, F, H = D_MODEL, FFN_DIM, N_HEADS
    positions = (jax.random.randint(ks[2], (BATCH, 1), 0, 1024, dtype=jnp.int32)
                 + jnp.arange(SEQ, dtype=jnp.int32)[None, :])
    return {
        "x": nrm(ks[0], (BATCH, SEQ, D), 1.0),
        "c": nrm(ks[1], (BATCH, D), 1.0),
        "positions": positions,
        "mod_w": nrm(ks[3], (DEPTH, D, N_MOD * D), 0.5 * D ** -0.5),
        "mod_b": nrm(ks[4], (DEPTH, N_MOD * D), 0.02),
        "norm_mix": gain(ks[5], (DEPTH, D)),
        "norm_ffn": gain(ks[6], (DEPTH, D)),
        "pool_w": nrm(ks[7], (N_A, N_POOL_GROUPS, POOL_GROUP_DIM, POOL_GROUP_DIM), POOL_GROUP_DIM ** -0.5),
        "pool_scale": gain(ks[8], (N_A, D)),
        "kv_mod_w": nrm(ks[9], (D, 2 * D), 0.5 * D ** -0.5),
        "kv_mod_b": nrm(ks[10], (2 * D,), 0.02),
        "kv_in_norm": gain(ks[11], (D,)),
        "w_dkv": nrm(ks[12], (D, KV_LORA_RANK), D ** -0.5),
        "kv_norm": gain(ks[13], (KV_LORA_RANK,)),
        "w_uk": nrm(ks[14], (KV_LORA_RANK, H * QK_NOPE_DIM), KV_LORA_RANK ** -0.5),
        "w_uv": nrm(ks[15], (KV_LORA_RANK, H * V_HEAD_DIM), KV_LORA_RANK ** -0.5),
        "w_kr": nrm(ks[16], (D, QK_ROPE_DIM), D ** -0.5),
        "w_dq": nrm(ks[17], (N_B, D, Q_LORA_RANK), D ** -0.5),
        "q_norm": gain(ks[18], (N_B, Q_LORA_RANK)),
        "w_uq": nrm(ks[19], (N_B, Q_LORA_RANK, H * Q_HEAD_DIM), Q_LORA_RANK ** -0.5),
        "w_o": nrm(ks[20], (N_B, H * V_HEAD_DIM, D), (H * V_HEAD_DIM) ** -0.5),
        "ffn_gate": nrm(ks[21], (DEPTH, D, F), D ** -0.5),
        "ffn_up": nrm(ks[22], (DEPTH, D, F), D ** -0.5),
        "ffn_down": nrm(ks[23], (DEPTH, F, D), F ** -0.5),
        "final_norm": gain(ks[24], (D,)),
    }


def _fwd_reference(x, c, positions, mod_w, mod_b, norm_mix, norm_ffn, pool_w, pool_scale,
              kv_mod_w, kv_mod_b, kv_in_norm, w_dkv, kv_norm, w_uk, w_uv, w_kr,
              w_dq, q_norm, w_uq, w_o, ffn_gate, ffn_up, ffn_down, final_norm):
    cos, sin = rope_tables(positions)
    sc = jax.nn.silu(c)
    kv = None
    for i in range(DEPTH):
        shift_m, scale_m, gate_m, shift_f, scale_f, gate_f = jnp.split(sc @ mod_w[i] + mod_b[i], N_MOD, axis=-1)
        if i == N_A:
            kv_shift, kv_scale = jnp.split(sc @ kv_mod_w + kv_mod_b, 2, axis=-1)
            h_kv = modulate(rmsnorm(x, kv_in_norm), kv_shift, kv_scale)
            kv = shared_kv(h_kv, w_dkv, kv_norm, w_uk, w_uv, w_kr, cos, sin)
        h = modulate(rmsnorm(x, norm_mix[i]), shift_m, scale_m)
        if i < N_A:
            y = pool_mixer(h, pool_w[i], pool_scale[i])
        else:
            j = i - N_A
            y = mla_mixer(h, kv, w_dq[j], q_norm[j], w_uq[j], w_o[j], cos, sin)
        x = x + gate_m[:, None, :] * y
        h = modulate(rmsnorm(x, norm_ffn[i]), shift_f, scale_f)
        x = x + gate_f[:, None, :] * swiglu(h, ffn_gate[i], ffn_up[i], ffn_down[i])
    return rmsnorm(x, final_norm)


import jax as _jax
import jax.numpy as _jnp

TWIN_FORMAT = 'train_step'
FWD_PARAMS = ['x', 'c', 'positions', 'mod_w', 'mod_b', 'norm_mix', 'norm_ffn', 'pool_w', 'pool_scale', 'kv_mod_w', 'kv_mod_b', 'kv_in_norm', 'w_dkv', 'kv_norm', 'w_uk', 'w_uv', 'w_kr', 'w_dq', 'q_norm', 'w_uq', 'w_o', 'ffn_gate', 'ffn_up', 'ffn_down', 'final_norm']
TWIN_WEIGHTS = ['mod_w', 'mod_b', 'norm_mix', 'norm_ffn', 'pool_w', 'pool_scale', 'kv_mod_w', 'kv_mod_b', 'kv_in_norm', 'w_dkv', 'kv_norm', 'w_uk', 'w_uv', 'w_kr', 'w_dq', 'q_norm', 'w_uq', 'w_o', 'ffn_gate', 'ffn_up', 'ffn_down', 'final_norm']
TWIN_DIFF_INPUT = 'x'
TWIN_INPUTS = ['x', 'c', 'positions', 'mod_w', 'mod_b', 'norm_mix', 'norm_ffn', 'pool_w', 'pool_scale', 'kv_mod_w', 'kv_mod_b', 'kv_in_norm', 'w_dkv', 'kv_norm', 'w_uk', 'w_uv', 'w_kr', 'w_dq', 'q_norm', 'w_uq', 'w_o', 'ffn_gate', 'ffn_up', 'ffn_down', 'final_norm', 'loss_target', 'm_mod_w', 'm_mod_b', 'm_norm_mix', 'm_norm_ffn', 'm_pool_w', 'm_pool_scale', 'm_kv_mod_w', 'm_kv_mod_b', 'm_kv_in_norm', 'm_w_dkv', 'm_kv_norm', 'm_w_uk', 'm_w_uv', 'm_w_kr', 'm_w_dq', 'm_q_norm', 'm_w_uq', 'm_w_o', 'm_ffn_gate', 'm_ffn_up', 'm_ffn_down', 'm_final_norm', 'v_mod_w', 'v_mod_b', 'v_norm_mix', 'v_norm_ffn', 'v_pool_w', 'v_pool_scale', 'v_kv_mod_w', 'v_kv_mod_b', 'v_kv_in_norm', 'v_w_dkv', 'v_kv_norm', 'v_w_uk', 'v_w_uv', 'v_w_kr', 'v_w_dq', 'v_q_norm', 'v_w_uq', 'v_w_o', 'v_ffn_gate', 'v_ffn_up', 'v_ffn_down', 'v_final_norm']
TWIN_OUTPUTS = ['loss', 'grad_x', 'grad_mod_w', 'grad_mod_b', 'grad_norm_mix', 'grad_norm_ffn', 'grad_pool_w', 'grad_pool_scale', 'grad_kv_mod_w', 'grad_kv_mod_b', 'grad_kv_in_norm', 'grad_w_dkv', 'grad_kv_norm', 'grad_w_uk', 'grad_w_uv', 'grad_w_kr', 'grad_w_dq', 'grad_q_norm', 'grad_w_uq', 'grad_w_o', 'grad_ffn_gate', 'grad_ffn_up', 'grad_ffn_down', 'grad_final_norm', 'delta_mod_w', 'delta_mod_b', 'delta_norm_mix', 'delta_norm_ffn', 'delta_pool_w', 'delta_pool_scale', 'delta_kv_mod_w', 'delta_kv_mod_b', 'delta_kv_in_norm', 'delta_w_dkv', 'delta_kv_norm', 'delta_w_uk', 'delta_w_uv', 'delta_w_kr', 'delta_w_dq', 'delta_q_norm', 'delta_w_uq', 'delta_w_o', 'delta_ffn_gate', 'delta_ffn_up', 'delta_ffn_down', 'delta_final_norm', 'new_m_mod_w', 'new_m_mod_b', 'new_m_norm_mix', 'new_m_norm_ffn', 'new_m_pool_w', 'new_m_pool_scale', 'new_m_kv_mod_w', 'new_m_kv_mod_b', 'new_m_kv_in_norm', 'new_m_w_dkv', 'new_m_kv_norm', 'new_m_w_uk', 'new_m_w_uv', 'new_m_w_kr', 'new_m_w_dq', 'new_m_q_norm', 'new_m_w_uq', 'new_m_w_o', 'new_m_ffn_gate', 'new_m_ffn_up', 'new_m_ffn_down', 'new_m_final_norm', 'new_v_mod_w', 'new_v_mod_b', 'new_v_norm_mix', 'new_v_norm_ffn', 'new_v_pool_w', 'new_v_pool_scale', 'new_v_kv_mod_w', 'new_v_kv_mod_b', 'new_v_kv_in_norm', 'new_v_w_dkv', 'new_v_kv_norm', 'new_v_w_uk', 'new_v_w_uv', 'new_v_w_kr', 'new_v_w_dq', 'new_v_q_norm', 'new_v_w_uq', 'new_v_w_o', 'new_v_ffn_gate', 'new_v_ffn_up', 'new_v_ffn_down', 'new_v_final_norm']
TWIN_LEAF_KINDS = {'loss': 'loss', 'grad_x': 'grad_x', 'grad_mod_w': 'grad_w', 'grad_mod_b': 'grad_w', 'grad_norm_mix': 'grad_w', 'grad_norm_ffn': 'grad_w', 'grad_pool_w': 'grad_w', 'grad_pool_scale': 'grad_w', 'grad_kv_mod_w': 'grad_w', 'grad_kv_mod_b': 'grad_w', 'grad_kv_in_norm': 'grad_w', 'grad_w_dkv': 'grad_w', 'grad_kv_norm': 'grad_w', 'grad_w_uk': 'grad_w', 'grad_w_uv': 'grad_w', 'grad_w_kr': 'grad_w', 'grad_w_dq': 'grad_w', 'grad_q_norm': 'grad_w', 'grad_w_uq': 'grad_w', 'grad_w_o': 'grad_w', 'grad_ffn_gate': 'grad_w', 'grad_ffn_up': 'grad_w', 'grad_ffn_down': 'grad_w', 'grad_final_norm': 'grad_w', 'delta_mod_w': 'delta_w', 'delta_mod_b': 'delta_w', 'delta_norm_mix': 'delta_w', 'delta_norm_ffn': 'delta_w', 'delta_pool_w': 'delta_w', 'delta_pool_scale': 'delta_w', 'delta_kv_mod_w': 'delta_w', 'delta_kv_mod_b': 'delta_w', 'delta_kv_in_norm': 'delta_w', 'delta_w_dkv': 'delta_w', 'delta_kv_norm': 'delta_w', 'delta_w_uk': 'delta_w', 'delta_w_uv': 'delta_w', 'delta_w_kr': 'delta_w', 'delta_w_dq': 'delta_w', 'delta_q_norm': 'delta_w', 'delta_w_uq': 'delta_w', 'delta_w_o': 'delta_w', 'delta_ffn_gate': 'delta_w', 'delta_ffn_up': 'delta_w', 'delta_ffn_down': 'delta_w', 'delta_final_norm': 'delta_w', 'new_m_mod_w': 'new_m', 'new_m_mod_b': 'new_m', 'new_m_norm_mix': 'new_m', 'new_m_norm_ffn': 'new_m', 'new_m_pool_w': 'new_m', 'new_m_pool_scale': 'new_m', 'new_m_kv_mod_w': 'new_m', 'new_m_kv_mod_b': 'new_m', 'new_m_kv_in_norm': 'new_m', 'new_m_w_dkv': 'new_m', 'new_m_kv_norm': 'new_m', 'new_m_w_uk': 'new_m', 'new_m_w_uv': 'new_m', 'new_m_w_kr': 'new_m', 'new_m_w_dq': 'new_m', 'new_m_q_norm': 'new_m', 'new_m_w_uq': 'new_m', 'new_m_w_o': 'new_m', 'new_m_ffn_gate': 'new_m', 'new_m_ffn_up': 'new_m', 'new_m_ffn_down': 'new_m', 'new_m_final_norm': 'new_m', 'new_v_mod_w': 'new_v', 'new_v_mod_b': 'new_v', 'new_v_norm_mix': 'new_v', 'new_v_norm_ffn': 'new_v', 'new_v_pool_w': 'new_v', 'new_v_pool_scale': 'new_v', 'new_v_kv_mod_w': 'new_v', 'new_v_kv_mod_b': 'new_v', 'new_v_kv_in_norm': 'new_v', 'new_v_w_dkv': 'new_v', 'new_v_kv_norm': 'new_v', 'new_v_w_uk': 'new_v', 'new_v_w_uv': 'new_v', 'new_v_w_kr': 'new_v', 'new_v_w_dq': 'new_v', 'new_v_q_norm': 'new_v', 'new_v_w_uq': 'new_v', 'new_v_w_o': 'new_v', 'new_v_ffn_gate': 'new_v', 'new_v_ffn_up': 'new_v', 'new_v_ffn_down': 'new_v', 'new_v_final_norm': 'new_v'}


def _forward(args):
    return _fwd_reference(*[args[k] for k in FWD_PARAMS])


def _output_shape():
    def fwd():
        inp = _fwd_setup_inputs(0)
        return _fwd_reference(*[inp[k] for k in FWD_PARAMS])
    out = _jax.eval_shape(fwd)
    return out.shape, out.dtype

N_MICROBATCH = 1
ADAM_LR = 0.001
ADAM_B1 = 0.9
ADAM_B2 = 0.999
ADAM_EPS = 1e-08
ADAM_WD = 0.01
ADAM_STEP = 10
PER_EXAMPLE_BATCH_AXIS = {'x': 0, 'c': 0, 'positions': 0, 'loss_target': 0}
SHARED_INPUTS = []
_WEIGHT_DTYPES = {'mod_w': _jnp.float32, 'mod_b': _jnp.float32, 'norm_mix': _jnp.float32, 'norm_ffn': _jnp.float32, 'pool_w': _jnp.float32, 'pool_scale': _jnp.float32, 'kv_mod_w': _jnp.float32, 'kv_mod_b': _jnp.float32, 'kv_in_norm': _jnp.float32, 'w_dkv': _jnp.float32, 'kv_norm': _jnp.float32, 'w_uk': _jnp.float32, 'w_uv': _jnp.float32, 'w_kr': _jnp.float32, 'w_dq': _jnp.float32, 'q_norm': _jnp.float32, 'w_uq': _jnp.float32, 'w_o': _jnp.float32, 'ffn_gate': _jnp.float32, 'ffn_up': _jnp.float32, 'ffn_down': _jnp.float32, 'final_norm': _jnp.float32}
MOMENT_SCALE = {'mod_w': 2.308896e-02, 'mod_b': 3.961331e-02, 'norm_mix': 1.762435e-02, 'norm_ffn': 2.380908e-02, 'pool_w': 2.437571e-02, 'pool_scale': 3.757989e-02, 'kv_mod_w': 1.385696e-02, 'kv_mod_b': 2.359420e-02, 'kv_in_norm': 9.084717e-03, 'w_dkv': 2.469962e-02, 'kv_norm': 2.496228e-02, 'w_uk': 4.620434e-03, 'w_uv': 1.159753e-02, 'w_kr': 1.956160e-02, 'w_dq': 8.147575e-03, 'q_norm': 7.726468e-03, 'w_uq': 3.317051e-03, 'w_o': 8.248959e-03, 'ffn_gate': 1.085127e-02, 'ffn_up': 1.050750e-02, 'ffn_down': 1.742062e-02, 'final_norm': 1.604191e+01}


def _to_microbatches(a, axis):
    t = _jnp.moveaxis(a, axis, 0)
    t = t.reshape((N_MICROBATCH, t.shape[0] // N_MICROBATCH) + t.shape[1:])
    return _jnp.moveaxis(t, 1, axis + 1)


def setup_inputs(seed: int = 0) -> dict:
    inp = _fwd_setup_inputs(seed)
    key = _jax.random.fold_in(_jax.random.key(seed), 7919)
    shape, _ = _output_shape()
    out = dict(inp)
    out["loss_target"] = _jax.random.normal(_jax.random.fold_in(key, 0), shape, _jnp.float32)
    for i, name in enumerate(TWIN_WEIGHTS):
        w = inp[name].astype(_jnp.float32)
        if MOMENT_SCALE is None:
            s = _jnp.sqrt(_jnp.mean(_jnp.square(w)) + 1e-30)
        else:
            s = MOMENT_SCALE[name]
        km, kv = _jax.random.split(_jax.random.fold_in(key, i + 1))
        out[name] = w
        out["m_" + name] = s * _jax.random.normal(km, w.shape, _jnp.float32)
        out["v_" + name] = (s * s) * _jax.random.uniform(kv, w.shape, _jnp.float32, 0.5, 1.5)
    if N_MICROBATCH > 1:
        for name, axis in PER_EXAMPLE_BATCH_AXIS.items():
            out[name] = _to_microbatches(out[name], axis)
    return {'x': out['x'], 'c': out['c'], 'positions': out['positions'], 'mod_w': out['mod_w'], 'mod_b': out['mod_b'], 'norm_mix': out['norm_mix'], 'norm_ffn': out['norm_ffn'], 'pool_w': out['pool_w'], 'pool_scale': out['pool_scale'], 'kv_mod_w': out['kv_mod_w'], 'kv_mod_b': out['kv_mod_b'], 'kv_in_norm': out['kv_in_norm'], 'w_dkv': out['w_dkv'], 'kv_norm': out['kv_norm'], 'w_uk': out['w_uk'], 'w_uv': out['w_uv'], 'w_kr': out['w_kr'], 'w_dq': out['w_dq'], 'q_norm': out['q_norm'], 'w_uq': out['w_uq'], 'w_o': out['w_o'], 'ffn_gate': out['ffn_gate'], 'ffn_up': out['ffn_up'], 'ffn_down': out['ffn_down'], 'final_norm': out['final_norm'], 'loss_target': out['loss_target'], 'm_mod_w': out['m_mod_w'], 'm_mod_b': out['m_mod_b'], 'm_norm_mix': out['m_norm_mix'], 'm_norm_ffn': out['m_norm_ffn'], 'm_pool_w': out['m_pool_w'], 'm_pool_scale': out['m_pool_scale'], 'm_kv_mod_w': out['m_kv_mod_w'], 'm_kv_mod_b': out['m_kv_mod_b'], 'm_kv_in_norm': out['m_kv_in_norm'], 'm_w_dkv': out['m_w_dkv'], 'm_kv_norm': out['m_kv_norm'], 'm_w_uk': out['m_w_uk'], 'm_w_uv': out['m_w_uv'], 'm_w_kr': out['m_w_kr'], 'm_w_dq': out['m_w_dq'], 'm_q_norm': out['m_q_norm'], 'm_w_uq': out['m_w_uq'], 'm_w_o': out['m_w_o'], 'm_ffn_gate': out['m_ffn_gate'], 'm_ffn_up': out['m_ffn_up'], 'm_ffn_down': out['m_ffn_down'], 'm_final_norm': out['m_final_norm'], 'v_mod_w': out['v_mod_w'], 'v_mod_b': out['v_mod_b'], 'v_norm_mix': out['v_norm_mix'], 'v_norm_ffn': out['v_norm_ffn'], 'v_pool_w': out['v_pool_w'], 'v_pool_scale': out['v_pool_scale'], 'v_kv_mod_w': out['v_kv_mod_w'], 'v_kv_mod_b': out['v_kv_mod_b'], 'v_kv_in_norm': out['v_kv_in_norm'], 'v_w_dkv': out['v_w_dkv'], 'v_kv_norm': out['v_kv_norm'], 'v_w_uk': out['v_w_uk'], 'v_w_uv': out['v_w_uv'], 'v_w_kr': out['v_w_kr'], 'v_w_dq': out['v_w_dq'], 'v_q_norm': out['v_q_norm'], 'v_w_uq': out['v_w_uq'], 'v_w_o': out['v_w_o'], 'v_ffn_gate': out['v_ffn_gate'], 'v_ffn_up': out['v_ffn_up'], 'v_ffn_down': out['v_ffn_down'], 'v_final_norm': out['v_final_norm']}


def _loss(weights, diff, rest, loss_target):
    with _jax.named_scope("forward"):
        args = {**rest, TWIN_DIFF_INPUT: diff, **{k: w.astype(_WEIGHT_DTYPES[k]) for k, w in weights.items()}}
        y = _forward(args)
    with _jax.named_scope("loss_head"):
        err = _jnp.square(y.astype(_jnp.float32) - loss_target)
        return 0.5 * _jnp.sum(_jnp.mean(err, axis=-1)) if err.ndim else 0.5 * err


def _adamw(w, g, m, v):
    m = ADAM_B1 * m + (1.0 - ADAM_B1) * g
    v = ADAM_B2 * v + (1.0 - ADAM_B2) * _jnp.square(g)
    m_hat = m / (1.0 - ADAM_B1 ** ADAM_STEP)
    v_hat = v / (1.0 - ADAM_B2 ** ADAM_STEP)
    delta = -ADAM_LR * (m_hat / (_jnp.sqrt(v_hat) + ADAM_EPS) + ADAM_WD * w)
    return delta, m, v


def reference(x, c, positions, mod_w, mod_b, norm_mix, norm_ffn, pool_w, pool_scale, kv_mod_w, kv_mod_b, kv_in_norm, w_dkv, kv_norm, w_uk, w_uv, w_kr, w_dq, q_norm, w_uq, w_o, ffn_gate, ffn_up, ffn_down, final_norm, loss_target, m_mod_w, m_mod_b, m_norm_mix, m_norm_ffn, m_pool_w, m_pool_scale, m_kv_mod_w, m_kv_mod_b, m_kv_in_norm, m_w_dkv, m_kv_norm, m_w_uk, m_w_uv, m_w_kr, m_w_dq, m_q_norm, m_w_uq, m_w_o, m_ffn_gate, m_ffn_up, m_ffn_down, m_final_norm, v_mod_w, v_mod_b, v_norm_mix, v_norm_ffn, v_pool_w, v_pool_scale, v_kv_mod_w, v_kv_mod_b, v_kv_in_norm, v_w_dkv, v_kv_norm, v_w_uk, v_w_uv, v_w_kr, v_w_dq, v_q_norm, v_w_uq, v_w_o, v_ffn_gate, v_ffn_up, v_ffn_down, v_final_norm):
    given = dict(x=x, c=c, positions=positions, mod_w=mod_w, mod_b=mod_b, norm_mix=norm_mix, norm_ffn=norm_ffn, pool_w=pool_w, pool_scale=pool_scale, kv_mod_w=kv_mod_w, kv_mod_b=kv_mod_b, kv_in_norm=kv_in_norm, w_dkv=w_dkv, kv_norm=kv_norm, w_uk=w_uk, w_uv=w_uv, w_kr=w_kr, w_dq=w_dq, q_norm=q_norm, w_uq=w_uq, w_o=w_o, ffn_gate=ffn_gate, ffn_up=ffn_up, ffn_down=ffn_down, final_norm=final_norm, loss_target=loss_target, m_mod_w=m_mod_w, m_mod_b=m_mod_b, m_norm_mix=m_norm_mix, m_norm_ffn=m_norm_ffn, m_pool_w=m_pool_w, m_pool_scale=m_pool_scale, m_kv_mod_w=m_kv_mod_w, m_kv_mod_b=m_kv_mod_b, m_kv_in_norm=m_kv_in_norm, m_w_dkv=m_w_dkv, m_kv_norm=m_kv_norm, m_w_uk=m_w_uk, m_w_uv=m_w_uv, m_w_kr=m_w_kr, m_w_dq=m_w_dq, m_q_norm=m_q_norm, m_w_uq=m_w_uq, m_w_o=m_w_o, m_ffn_gate=m_ffn_gate, m_ffn_up=m_ffn_up, m_ffn_down=m_ffn_down, m_final_norm=m_final_norm, v_mod_w=v_mod_w, v_mod_b=v_mod_b, v_norm_mix=v_norm_mix, v_norm_ffn=v_norm_ffn, v_pool_w=v_pool_w, v_pool_scale=v_pool_scale, v_kv_mod_w=v_kv_mod_w, v_kv_mod_b=v_kv_mod_b, v_kv_in_norm=v_kv_in_norm, v_w_dkv=v_w_dkv, v_kv_norm=v_kv_norm, v_w_uk=v_w_uk, v_w_uv=v_w_uv, v_w_kr=v_w_kr, v_w_dq=v_w_dq, v_q_norm=v_q_norm, v_w_uq=v_w_uq, v_w_o=v_w_o, v_ffn_gate=v_ffn_gate, v_ffn_up=v_ffn_up, v_ffn_down=v_ffn_down, v_final_norm=v_final_norm)
    weights = {n: given[n] for n in TWIN_WEIGHTS}
    shared = {n: given[n] for n in SHARED_INPUTS}
    per_example = {n: given[n] for n in ['x', 'c', 'positions']}
    grad_fn = _jax.value_and_grad(_loss, argnums=(0, 1))

    def one_microbatch(ex, loss_target):
        ex = dict(ex)
        diff = ex.pop(TWIN_DIFF_INPUT)
        return grad_fn(weights, diff, {**shared, **ex}, loss_target)

    if N_MICROBATCH == 1:
        loss, (grad_w, grad_x) = one_microbatch(per_example, given["loss_target"])
    else:
        def body(carry, xs):
            loss_sum, grad_sum = carry
            l_k, (gw_k, gx_k) = one_microbatch(xs[0], xs[1])
            with _jax.named_scope("update"):
                return (loss_sum + l_k, _jax.tree.map(_jnp.add, grad_sum, gw_k)), gx_k

        init = (_jnp.zeros((), _jnp.float32), _jax.tree.map(_jnp.zeros_like, weights))
        (loss, grad_w), grad_x = _jax.lax.scan(body, init, (per_example, given["loss_target"]))
    with _jax.named_scope("update"):
        delta_w, new_m, new_v = {}, {}, {}
        for n in TWIN_WEIGHTS:
            delta_w[n], new_m[n], new_v[n] = _adamw(weights[n], grad_w[n], given["m_" + n], given["v_" + n])
    return (loss, grad_x, *[grad_w[n] for n in TWIN_WEIGHTS], *[delta_w[n] for n in TWIN_WEIGHTS],
            *[new_m[n] for n in TWIN_WEIGHTS], *[new_v[n] for n in TWIN_WEIGHTS])
```

```python
import functools

import jax
import jax.numpy as jnp
from jax import lax
from jax.experimental import pallas as pl
from jax.experimental.pallas import tpu as pltpu

F32 = jnp.float32
BF16 = jnp.bfloat16
MESH = pl.DeviceIdType.MESH
ANY = pl.BlockSpec(memory_space=pl.ANY)

NORM_EPS = 1e-6
POOL_WINDOWS = (2, 4, 8, 16)
NOPE_DIM = 128
ROPE_DIM = 64
V_DIM = 128
HEAD_PAD = 256
SM_SCALE = (NOPE_DIM + ROPE_DIM) ** -0.5
ROPE_THETA = 10000.0
N_MOD = 6
ADAM_LR, ADAM_B1, ADAM_B2, ADAM_EPS, ADAM_WD, ADAM_STEP = 0.001, 0.9, 0.999, 1e-08, 0.01, 10
N_CHIPS = 4
N_DEV = 8
LANES = 128
HALO = 128
VMEM_LIMIT = 48 * 1024 * 1024


def _tile(dim, pref, align):
    if dim <= pref:
        return dim
    t = (pref // align) * align
    while t >= align:
        if dim % t == 0:
            return t
        t -= align
    return dim


def _params(*sem):
    return pltpu.CompilerParams(dimension_semantics=sem, vmem_limit_bytes=VMEM_LIMIT)


def _mm(a, b, *, name, ta=False, tb=False, out_dtype=F32, bias=None, add=None, b_idx=None, out_stack=None, tm=1024, tn=1024, tk=None):
    m, k = (a.shape[1], a.shape[0]) if ta else a.shape
    b2 = b.shape if b_idx is None else b.shape[1:]
    kb, n = (b2[1], b2[0]) if tb else b2
    assert k == kb, (a.shape, b.shape, ta, tb)
    tm = _tile(m, tm, LANES)
    tn = _tile(n, tn, LANES)
    if tk is None:
        tk = k if k <= 2048 else 512
    tk = _tile(k, tk, LANES)
    nk = k // tk
    dims = (((0 if ta else 1,), (1 if tb else 0,)), ((), ()))

    def body(*refs):
        refs = list(refs)
        a_ref, b_ref = refs.pop(0), refs.pop(0)
        bias_ref = refs.pop(0) if bias is not None else None
        add_ref = refs.pop(0) if add is not None else None
        if out_stack is not None and out_stack[2] is not None:
            refs.pop(0)
        o_ref = refs.pop(0)
        part = lax.dot_general(a_ref[...].astype(BF16), b_ref[...].astype(BF16), dims, preferred_element_type=F32)

        def finish(acc):
            if bias_ref is not None:
                acc = acc + bias_ref[...]
            if add_ref is not None:
                acc = acc + add_ref[...]
            o_ref[...] = acc.astype(o_ref.dtype)

        if nk == 1:
            finish(part)
        else:
            acc_ref = refs[-1]
            step = pl.program_id(2)

            @pl.when(step == 0)
            def _():
                acc_ref[...] = part

            @pl.when(step > 0)
            def _():
                acc_ref[...] += part

            @pl.when(step == nk - 1)
            def _():
                finish(acc_ref[...])

    a_spec = pl.BlockSpec((tk, tm), lambda i, j, s: (s, i)) if ta else pl.BlockSpec((tm, tk), lambda i, j, s: (i, s))
    if b_idx is None:
        b_spec = pl.BlockSpec((tn, tk), lambda i, j, s: (j, s)) if tb else pl.BlockSpec((tk, tn), lambda i, j, s: (s, j))
    elif tb:
        b_spec = pl.BlockSpec((None, tn, tk), lambda i, j, s: (b_idx, j, s))
    else:
        b_spec = pl.BlockSpec((None, tk, tn), lambda i, j, s: (b_idx, s, j))
    in_specs = [a_spec, b_spec]
    args = [a, b]
    if bias is not None:
        in_specs.append(pl.BlockSpec((1, tn), lambda i, j, s: (0, j)))
        args.append(bias)
    if add is not None:
        in_specs.append(pl.BlockSpec((tm, tn), lambda i, j, s: (i, j)))
        args.append(add)
    aliases = {}
    if out_stack is None:
        out_spec = pl.BlockSpec((tm, tn), lambda i, j, s: (i, j))
        out_shape = jax.ShapeDtypeStruct((m, n), out_dtype)
    else:
        n_stack, idx, prev = out_stack
        out_spec = pl.BlockSpec((None, tm, tn), lambda i, j, s: (idx, i, j))
        out_shape = jax.ShapeDtypeStruct((n_stack, m, n), out_dtype)
        if prev is not None:
            aliases = {len(args): 0}
            in_specs.append(ANY)
            args.append(prev)
    return pl.pallas_call(
        body,
        name=name,
        grid=(m // tm, n // tn, nk),
        in_specs=in_specs,
        out_specs=out_spec,
        out_shape=out_shape,
        input_output_aliases=aliases,
        scratch_shapes=[pltpu.VMEM((tm, tn), F32)] if nk > 1 else [],
        compiler_params=_params("parallel", "parallel", "arbitrary"),
    )(*args)


def _tp_fwd(sc16, w, bias, *, name):
    nl, d, n = w.shape
    tn = _tile(n, 512, LANES)

    def body(sc_ref, w_ref, b_ref, o_ref):
        o_ref[0] = jnp.dot(sc_ref[...].astype(BF16), w_ref[0].astype(BF16), preferred_element_type=F32) + b_ref[0]

    return pl.pallas_call(
        body,
        name=name,
        grid=(nl, n // tn),
        in_specs=[
            pl.BlockSpec((16, d), lambda l, j: (0, 0)),
            pl.BlockSpec((1, d, tn), lambda l, j: (l, 0, j)),
            pl.BlockSpec((1, 1, tn), lambda l, j: (l, 0, j)),
        ],
        out_specs=pl.BlockSpec((1, 16, tn), lambda l, j: (l, 0, j)),
        out_shape=jax.ShapeDtypeStruct((nl, 16, n), F32),
        compiler_params=_params("parallel", "parallel"),
    )(sc16, w, bias)


def _gmm(a, w, *, name, mode, out_dtype):
    s = a.shape[0]
    g = len(POOL_WINDOWS)
    c = a.shape[1] // g
    tr = _tile(s, 1024, LANES)
    n_row = s // tr

    if mode == "tn":

        def body(a_ref, b_ref, o_ref, acc_ref):
            part = lax.dot_general(a_ref[...].astype(BF16), b_ref[...].astype(BF16), (((0,), (0,)), ((), ())), preferred_element_type=F32)

            @pl.when(pl.program_id(1) == 0)
            def _():
                acc_ref[...] = part

            @pl.when(pl.program_id(1) > 0)
            def _():
                acc_ref[...] += part

            @pl.when(pl.program_id(1) == n_row - 1)
            def _():
                o_ref[0] = acc_ref[...].astype(o_ref.dtype)

        return pl.pallas_call(
            body,
            name=name,
            grid=(g, n_row),
            in_specs=[pl.BlockSpec((tr, c), lambda gi, i: (i, gi)), pl.BlockSpec((tr, c), lambda gi, i: (i, gi))],
            out_specs=pl.BlockSpec((1, c, c), lambda gi, i: (gi, 0, 0)),
            out_shape=jax.ShapeDtypeStruct((g, c, c), out_dtype),
            scratch_shapes=[pltpu.VMEM((c, c), F32)],
            compiler_params=_params("parallel", "arbitrary"),
        )(a, w)

    dims = (((1,), (0 if mode == "nn" else 1,)), ((), ()))

    def body(a_ref, w_ref, o_ref):
        o_ref[...] = lax.dot_general(a_ref[...].astype(BF16), w_ref[0].astype(BF16), dims, preferred_element_type=F32).astype(o_ref.dtype)

    return pl.pallas_call(
        body,
        name=name,
        grid=(g, n_row),
        in_specs=[pl.BlockSpec((tr, c), lambda gi, i: (i, gi)), pl.BlockSpec((1, c, c), lambda gi, i: (gi, 0, 0))],
        out_specs=pl.BlockSpec((tr, c), lambda gi, i: (i, gi)),
        out_shape=jax.ShapeDtypeStruct((s, g * c), out_dtype),
        compiler_params=_params("parallel", "parallel"),
    )(a, w)


def _row_tile(s, d):
    return _tile(s, max(8, (1 << 19) // d), 8)


def _norm_fwd(x, g, *, name, scale=None, shift=None, y=None, gate=None, out_dtype=BF16):
    s, d = x.shape
    tr = _row_tile(s, d)
    has_res, has_mod = y is not None, scale is not None

    def body(*refs):
        refs = list(refs)
        x_ref = refs.pop(0)
        xv = x_ref[...]
        if has_res:
            y_ref, gate_ref = refs.pop(0), refs.pop(0)
            xv = xv + gate_ref[...] * y_ref[...]
        g_ref = refs.pop(0)
        if has_mod:
            scale_ref, shift_ref = refs.pop(0), refs.pop(0)
        if has_res:
            refs.pop(0)[...] = xv
        h = xv * lax.rsqrt(jnp.mean(xv * xv, axis=-1, keepdims=True) + NORM_EPS)
        h = h * g_ref[...]
        if has_mod:
            h = h * (1.0 + scale_ref[...]) + shift_ref[...]
        refs.pop(0)[...] = h.astype(out_dtype)

    row = pl.BlockSpec((tr, d), lambda i: (i, 0))
    vec = pl.BlockSpec((1, d), lambda i: (0, 0))
    args, in_specs = [x], [row]
    if has_res:
        args += [y, gate]
        in_specs += [row, vec]
    args.append(g)
    in_specs.append(vec)
    if has_mod:
        args += [scale, shift]
        in_specs += [vec, vec]
    out_shape, out_specs = [], []
    if has_res:
        out_shape.append(jax.ShapeDtypeStruct((s, d), F32))
        out_specs.append(row)
    out_shape.append(jax.ShapeDtypeStruct((s, d), out_dtype))
    out_specs.append(row)
    res = pl.pallas_call(
        body, name=name, grid=(s // tr,), in_specs=in_specs, out_specs=out_specs, out_shape=out_shape, compiler_params=_params("parallel")
    )(*args)
    return (res[0], res[1]) if has_res else res[0]


def _norm_bwd(x, g, dh, *, name, scale=None, resid=None):
    s, d = x.shape
    tr = _row_tile(s, d)
    has_mod, has_res = scale is not None, resid is not None

    def body(*refs):
        refs = list(refs)
        x_ref, g_ref, dh_ref = refs.pop(0), refs.pop(0), refs.pop(0)
        scale_ref = refs.pop(0) if has_mod else None
        resid_ref = refs.pop(0) if has_res else None
        dx_ref, sums_ref = refs
        xv = x_ref[...]
        r = lax.rsqrt(jnp.mean(xv * xv, axis=-1, keepdims=True) + NORM_EPS)
        xhat = xv * r
        dh32 = dh_ref[...].astype(F32)
        gv = g_ref[...]
        dn = dh32 * (1.0 + scale_ref[...]) if has_mod else dh32
        dxhat = dn * gv
        dx = r * (dxhat - xhat * jnp.mean(dxhat * xhat, axis=-1, keepdims=True))
        if has_res:
            dx = dx + resid_ref[...]
        dx_ref[...] = dx

        @pl.when(pl.program_id(0) == 0)
        def _():
            sums_ref[...] = jnp.zeros_like(sums_ref)

        sums_ref[0:1, :] += jnp.sum(dh32, axis=0, keepdims=True)
        sums_ref[1:2, :] += jnp.sum(dh32 * (xhat * gv), axis=0, keepdims=True)
        sums_ref[2:3, :] += jnp.sum(dn * xhat, axis=0, keepdims=True)

    row = pl.BlockSpec((tr, d), lambda i: (i, 0))
    vec = pl.BlockSpec((1, d), lambda i: (0, 0))
    args, in_specs = [x, g, dh], [row, vec, row]
    if has_mod:
        args.append(scale)
        in_specs.append(vec)
    if has_res:
        args.append(resid)
        in_specs.append(row)
    return pl.pallas_call(
        body,
        name=name,
        grid=(s // tr,),
        in_specs=in_specs,
        out_specs=[row, pl.BlockSpec((8, d), lambda i: (0, 0))],
        out_shape=[jax.ShapeDtypeStruct((s, d), F32), jax.ShapeDtypeStruct((8, d), F32)],
        compiler_params=_params("arbitrary"),
    )(*args)


def _gate_bwd(dx, y, gate, *, name):
    s, d = dx.shape
    tr = _row_tile(s, d)

    def body(dx_ref, y_ref, gate_ref, dy_ref, sums_ref):
        dxv = dx_ref[...]
        dy_ref[...] = (dxv * gate_ref[...]).astype(dy_ref.dtype)

        @pl.when(pl.program_id(0) == 0)
        def _():
            sums_ref[...] = jnp.zeros_like(sums_ref)

        sums_ref[0:1, :] += jnp.sum(dxv * y_ref[...], axis=0, keepdims=True)

    row = pl.BlockSpec((tr, d), lambda i: (i, 0))
    return pl.pallas_call(
        body,
        name=name,
        grid=(s // tr,),
        in_specs=[row, row, pl.BlockSpec((1, d), lambda i: (0, 0))],
        out_specs=[row, pl.BlockSpec((8, d), lambda i: (0, 0))],
        out_shape=[jax.ShapeDtypeStruct((s, d), BF16), jax.ShapeDtypeStruct((8, d), F32)],
        compiler_params=_params("arbitrary"),
    )(dx, y, gate)


def _elementwise(fn, args, out_dtypes, *, name):
    s, d = args[0].shape
    tc = d if d <= 2048 else _tile(d, 1024, LANES)
    tr = _tile(s, max(8, (1 << 18) // tc), 8)
    n_in = len(args)

    def body(*refs):
        outs = fn(*[r[...] for r in refs[:n_in]])
        for o_ref, o in zip(refs[n_in:], outs):
            o_ref[...] = o.astype(o_ref.dtype)

    spec = pl.BlockSpec((tr, tc), lambda i, j: (i, j))
    return pl.pallas_call(
        body,
        name=name,
        grid=(s // tr, d // tc),
        in_specs=[spec] * n_in,
        out_specs=[spec] * len(out_dtypes),
        out_shape=[jax.ShapeDtypeStruct((s, d), dt) for dt in out_dtypes],
        compiler_params=_params("parallel", "parallel"),
    )(*args)


def _swiglu_fwd(a, b):
    return (a * jax.nn.sigmoid(a) * b,)


def _swiglu_bwd(dz, a, b):
    sig = jax.nn.sigmoid(a)
    silu = a * sig
    return dz * b * (sig * (1.0 + a * (1.0 - sig))), dz * silu


def _silu(v):
    return (v * jax.nn.sigmoid(v),)


def _split3(v):
    hi = v.astype(BF16)
    r1 = v - hi.astype(F32)
    mid = r1.astype(BF16)
    lo = (r1 - mid.astype(F32)).astype(BF16)
    return hi, mid, lo


def _band_dot(band, v):
    return sum(jnp.dot(band, part, preferred_element_type=F32) for part in _split3(v))


def _pool(h, *, name, transpose, out_dtype):
    s, d = h.shape
    c = d // len(POOL_WINDOWS)
    tr = _tile(s, 256, HALO)
    per = tr // HALO
    n_halo = s // HALO

    def body(h_ref, halo_ref, o_ref):
        i = pl.program_id(0)
        out_row = i * tr + lax.broadcasted_iota(jnp.int32, (tr, tr + HALO), 0)
        col = lax.broadcasted_iota(jnp.int32, (tr, tr + HALO), 1)
        if transpose:
            ext = jnp.concatenate([h_ref[...], halo_ref[...]], axis=0)
            src_row = i * tr + col
            ext_row = i * tr + lax.broadcasted_iota(jnp.int32, (tr + HALO, 1), 0)
        else:
            ext = jnp.concatenate([halo_ref[...], h_ref[...]], axis=0)
            src_row = i * tr + col - HALO
            own_row = i * tr + lax.broadcasted_iota(jnp.int32, (tr, 1), 0)
        for gi, w in enumerate(POOL_WINDOWS):
            cols = slice(gi * c, (gi + 1) * c)
            if transpose:
                band = (src_row >= out_row) & (src_row < out_row + w) & (src_row < s)
                scaled = ext[:, cols] / jnp.minimum(ext_row + 1, w).astype(F32)
                res = _band_dot(band.astype(BF16), scaled) - h_ref[:, cols]
            else:
                band = (src_row <= out_row) & (src_row > out_row - w) & (src_row >= 0)
                res = _band_dot(band.astype(BF16), ext[:, cols]) / jnp.minimum(own_row + 1, w).astype(F32) - h_ref[:, cols]
            o_ref[:, cols] = res.astype(o_ref.dtype)

    if transpose:
        halo_map = lambda i: (jnp.minimum((i + 1) * per, n_halo - 1), 0)
    else:
        halo_map = lambda i: (jnp.maximum(i * per - 1, 0), 0)
    return pl.pallas_call(
        body,
        name=name,
        grid=(s // tr,),
        in_specs=[pl.BlockSpec((tr, d), lambda i: (i, 0)), pl.BlockSpec((HALO, d), halo_map)],
        out_specs=pl.BlockSpec((tr, d), lambda i: (i, 0)),
        out_shape=jax.ShapeDtypeStruct((s, d), out_dtype),
        compiler_params=_params("parallel"),
    )(h, h)


def _rotate(v, cos, sin):
    lane = lax.broadcasted_iota(jnp.int32, v.shape, 1)
    swapped = jnp.where(lane % ROPE_DIM < ROPE_DIM // 2, pltpu.roll(v, LANES - ROPE_DIM // 2, 1), pltpu.roll(v, ROPE_DIM // 2, 1))
    return v * cos + swapped * sin


def _rope_heads(x, cos, sin, *, name, out_dtype):
    s, n = x.shape
    tr = _tile(s, 512, 8)

    def body(x_ref, cos_ref, sin_ref, o_ref):
        j = pl.program_id(1)

        @pl.when(j % 2 == 0)
        def _():
            o_ref[...] = x_ref[...].astype(o_ref.dtype)

        @pl.when(j % 2 == 1)
        def _():
            o_ref[...] = _rotate(x_ref[...].astype(F32), cos_ref[...], sin_ref[...]).astype(o_ref.dtype)

    blk = pl.BlockSpec((tr, LANES), lambda i, j: (i, j))
    tab = pl.BlockSpec((tr, LANES), lambda i, j: (i, 0))
    return pl.pallas_call(
        body,
        name=name,
        grid=(s // tr, n // LANES),
        in_specs=[blk, tab, tab],
        out_specs=blk,
        out_shape=jax.ShapeDtypeStruct((s, n), out_dtype),
        compiler_params=_params("parallel", "parallel"),
    )(x, cos, sin)


def _build_keys(kv, kr_pre, cos, sin, *, name):
    s, n = kv.shape
    tr = _tile(s, 512, 8)

    def body(kv_ref, kr_ref, cos_ref, sin_ref, o_ref):
        j = pl.program_id(1)

        @pl.when(j % 2 == 0)
        def _():
            o_ref[...] = kv_ref[...]

        @pl.when(j % 2 == 1)
        def _():
            o_ref[...] = _rotate(kr_ref[...], cos_ref[...], sin_ref[...]).astype(o_ref.dtype)

    tab = pl.BlockSpec((tr, LANES), lambda i, j: (i, 0))
    return pl.pallas_call(
        body,
        name=name,
        grid=(s // tr, n // LANES),
        in_specs=[pl.BlockSpec((tr, LANES), lambda i, j: (i, (j // 2) * 2)), tab, tab, tab],
        out_specs=pl.BlockSpec((tr, LANES), lambda i, j: (i, j)),
        out_shape=jax.ShapeDtypeStruct((s, n), BF16),
        compiler_params=_params("parallel", "parallel"),
    )(kv, kr_pre, cos, sin)


def _keys_bwd(dk_a, dk_b, dv_a, dv_b, cos, sin_neg, *, name):
    s, n = dk_a.shape
    heads = n // HEAD_PAD
    tr = _tile(s, 512, 8)

    def body(dka_ref, dkb_ref, dva_ref, dvb_ref, cos_ref, sin_ref, dkv_ref, dkr_ref):
        hd = pl.program_id(1)
        dk = dka_ref[...] + dkb_ref[...]
        dkv_ref[:, :NOPE_DIM] = dk[:, :NOPE_DIM].astype(dkv_ref.dtype)
        dkv_ref[:, NOPE_DIM:] = (dva_ref[...] + dvb_ref[...]).astype(dkv_ref.dtype)

        @pl.when(hd == 0)
        def _():
            dkr_ref[...] = dk[:, NOPE_DIM:]

        @pl.when(hd > 0)
        def _():
            dkr_ref[...] += dk[:, NOPE_DIM:]

        @pl.when(hd == heads - 1)
        def _():
            dkr_ref[...] = _rotate(dkr_ref[...], cos_ref[...], sin_ref[...])

    dk_blk = pl.BlockSpec((tr, HEAD_PAD), lambda i, hd: (i, hd))
    dv_blk = pl.BlockSpec((tr, V_DIM), lambda i, hd: (i, hd))
    tab = pl.BlockSpec((tr, LANES), lambda i, hd: (i, 0))
    return pl.pallas_call(
        body,
        name=name,
        grid=(s // tr, heads),
        in_specs=[dk_blk, dk_blk, dv_blk, dv_blk, tab, tab],
        out_specs=[dk_blk, tab],
        out_shape=[jax.ShapeDtypeStruct((s, n), BF16), jax.ShapeDtypeStruct((s, LANES), F32)],
        compiler_params=_params("parallel", "arbitrary"),
    )(dk_a, dk_b, dv_a, dv_b, cos, sin_neg)


def _attn_tiles(s):
    t = _tile(s, 512, LANES)
    return t, t


def _causal_mask(q0, k0, tq, tk, transposed=False):
    if transposed:
        kpos = k0 + lax.broadcasted_iota(jnp.int32, (tk, tq), 0)
        qpos = q0 + lax.broadcasted_iota(jnp.int32, (tk, tq), 1)
    else:
        qpos = q0 + lax.broadcasted_iota(jnp.int32, (tq, tk), 0)
        kpos = k0 + lax.broadcasted_iota(jnp.int32, (tq, tk), 1)
    return kpos <= qpos


def _attn_fwd(q, keys, kv, *, name):
    s = q.shape[0]
    heads = q.shape[1] // HEAD_PAD
    tq, tk = _attn_tiles(s)
    nq, nk = s // tq, s // tk
    nt = (((1,), (1,)), ((), ()))

    def body(q_ref, k_ref, v_ref, o_ref, lse_ref, m_ref, l_ref, acc_ref):
        qi, ki = pl.program_id(1), pl.program_id(2)

        @pl.when(ki == 0)
        def _():
            m_ref[...] = jnp.full_like(m_ref, -jnp.inf)
            l_ref[...] = jnp.zeros_like(l_ref)
            acc_ref[...] = jnp.zeros_like(acc_ref)

        @pl.when(ki * tk <= qi * tq + tq - 1)
        def _():
            sc = lax.dot_general(q_ref[...], k_ref[...], nt, preferred_element_type=F32) * SM_SCALE
            sc = jnp.where(_causal_mask(qi * tq, ki * tk, tq, tk), sc, -jnp.inf)
            m_new = jnp.maximum(m_ref[...], jnp.max(sc, axis=1, keepdims=True))
            alpha = jnp.exp(m_ref[...] - m_new)
            p = jnp.exp(sc - m_new)
            l_ref[...] = alpha * l_ref[...] + jnp.sum(p, axis=1, keepdims=True)
            acc_ref[...] = alpha * acc_ref[...] + jnp.dot(p.astype(BF16), v_ref[...], preferred_element_type=F32)
            m_ref[...] = m_new

        @pl.when(ki == nk - 1)
        def _():
            o_ref[...] = (acc_ref[...] / l_ref[...]).astype(o_ref.dtype)
            lse_ref[0] = m_ref[...] + jnp.log(l_ref[...])

    def last_k(qi, ki):
        return jnp.minimum(ki, (qi * tq + tq - 1) // tk)

    return pl.pallas_call(
        body,
        name=name,
        grid=(heads, nq, nk),
        in_specs=[
            pl.BlockSpec((tq, HEAD_PAD), lambda hd, qi, ki: (qi, hd)),
            pl.BlockSpec((tk, HEAD_PAD), lambda hd, qi, ki: (last_k(qi, ki), hd)),
            pl.BlockSpec((tk, V_DIM), lambda hd, qi, ki: (last_k(qi, ki), 2 * hd + 1)),
        ],
        out_specs=[
            pl.BlockSpec((tq, V_DIM), lambda hd, qi, ki: (qi, hd)),
            pl.BlockSpec((1, tq, 1), lambda hd, qi, ki: (hd, qi, 0)),
        ],
        out_shape=[jax.ShapeDtypeStruct((s, heads * V_DIM), BF16), jax.ShapeDtypeStruct((heads, s, 1), F32)],
        scratch_shapes=[pltpu.VMEM((tq, 1), F32), pltpu.VMEM((tq, 1), F32), pltpu.VMEM((tq, V_DIM), F32)],
        compiler_params=_params("parallel", "parallel", "arbitrary"),
    )(q, keys, kv)


def _attn_bwd_q(q, keys, kv, o, do, lse, *, name):
    s = q.shape[0]
    heads = q.shape[1] // HEAD_PAD
    tq, tk = _attn_tiles(s)
    nq, nk = s // tq, s // tk
    nt = (((1,), (1,)), ((), ()))

    def body(q_ref, k_ref, v_ref, o_ref, do_ref, lse_ref, dq_ref, delta_ref, acc_ref, dl_ref):
        qi, ki = pl.program_id(1), pl.program_id(2)

        @pl.when(ki == 0)
        def _():
            acc_ref[...] = jnp.zeros_like(acc_ref)
            dl = jnp.sum(do_ref[...].astype(F32) * o_ref[...].astype(F32), axis=1, keepdims=True)
            dl_ref[...] = dl
            delta_ref[0] = dl

        @pl.when(ki * tk <= qi * tq + tq - 1)
        def _():
            sc = lax.dot_general(q_ref[...], k_ref[...], nt, preferred_element_type=F32) * SM_SCALE
            p = jnp.where(_causal_mask(qi * tq, ki * tk, tq, tk), jnp.exp(sc - lse_ref[0]), 0.0)
            dp = lax.dot_general(do_ref[...], v_ref[...], nt, preferred_element_type=F32)
            ds = p * (dp - dl_ref[...]) * SM_SCALE
            acc_ref[...] += jnp.dot(ds.astype(BF16), k_ref[...], preferred_element_type=F32)

        @pl.when(ki == nk - 1)
        def _():
            dq_ref[...] = acc_ref[...]

    def last_k(qi, ki):
        return jnp.minimum(ki, (qi * tq + tq - 1) // tk)

    o_blk = pl.BlockSpec((tq, V_DIM), lambda hd, qi, ki: (qi, hd))
    col = pl.BlockSpec((1, tq, 1), lambda hd, qi, ki: (hd, qi, 0))
    return pl.pallas_call(
        body,
        name=name,
        grid=(heads, nq, nk),
        in_specs=[
            pl.BlockSpec((tq, HEAD_PAD), lambda hd, qi, ki: (qi, hd)),
            pl.BlockSpec((tk, HEAD_PAD), lambda hd, qi, ki: (last_k(qi, ki), hd)),
            pl.BlockSpec((tk, V_DIM), lambda hd, qi, ki: (last_k(qi, ki), 2 * hd + 1)),
            o_blk,
            o_blk,
            col,
        ],
        out_specs=[pl.BlockSpec((tq, HEAD_PAD), lambda hd, qi, ki: (qi, hd)), col],
        out_shape=[jax.ShapeDtypeStruct((s, heads * HEAD_PAD), F32), jax.ShapeDtypeStruct((heads, s, 1), F32)],
        scratch_shapes=[pltpu.VMEM((tq, HEAD_PAD), F32), pltpu.VMEM((tq, 1), F32)],
        compiler_params=_params("parallel", "parallel", "arbitrary"),
    )(q, keys, kv, o, do, lse)


def _attn_bwd_kv(q, keys, kv, do, lse_row, delta_row, *, name):
    s = q.shape[0]
    heads = q.shape[1] // HEAD_PAD
    tq, tk = _attn_tiles(s)
    nq, nk = s // tq, s // tk
    nt = (((1,), (1,)), ((), ()))

    def body(q_ref, k_ref, v_ref, do_ref, lse_ref, delta_ref, dk_ref, dv_ref, dk_acc, dv_acc):
        ki, qi = pl.program_id(1), pl.program_id(2)

        @pl.when(qi == 0)
        def _():
            dk_acc[...] = jnp.zeros_like(dk_acc)
            dv_acc[...] = jnp.zeros_like(dv_acc)

        @pl.when(qi * tq + tq - 1 >= ki * tk)
        def _():
            sc_t = lax.dot_general(k_ref[...], q_ref[...], nt, preferred_element_type=F32) * SM_SCALE
            p_t = jnp.where(_causal_mask(qi * tq, ki * tk, tq, tk, transposed=True), jnp.exp(sc_t - lse_ref[0]), 0.0)
            dv_acc[...] += jnp.dot(p_t.astype(BF16), do_ref[...], preferred_element_type=F32)
            dp_t = lax.dot_general(v_ref[...], do_ref[...], nt, preferred_element_type=F32)
            ds_t = p_t * (dp_t - delta_ref[0]) * SM_SCALE
            dk_acc[...] += jnp.dot(ds_t.astype(BF16), q_ref[...], preferred_element_type=F32)

        @pl.when(qi == nq - 1)
        def _():
            dk_ref[...] = dk_acc[...]
            dv_ref[...] = dv_acc[...]

    def first_q(ki, qi):
        return jnp.maximum(qi, (ki * tk) // tq)

    row = pl.BlockSpec((1, 1, tq), lambda hd, ki, qi: (hd, 0, first_q(ki, qi)))
    return pl.pallas_call(
        body,
        name=name,
        grid=(heads, nk, nq),
        in_specs=[
            pl.BlockSpec((tq, HEAD_PAD), lambda hd, ki, qi: (first_q(ki, qi), hd)),
            pl.BlockSpec((tk, HEAD_PAD), lambda hd, ki, qi: (ki, hd)),
            pl.BlockSpec((tk, V_DIM), lambda hd, ki, qi: (ki, 2 * hd + 1)),
            pl.BlockSpec((tq, V_DIM), lambda hd, ki, qi: (first_q(ki, qi), hd)),
            row,
            row,
        ],
        out_specs=[
            pl.BlockSpec((tk, HEAD_PAD), lambda hd, ki, qi: (ki, hd)),
            pl.BlockSpec((tk, V_DIM), lambda hd, ki, qi: (ki, hd)),
        ],
        out_shape=[jax.ShapeDtypeStruct((s, heads * HEAD_PAD), F32), jax.ShapeDtypeStruct((s, heads * V_DIM), F32)],
        scratch_shapes=[pltpu.VMEM((tk, HEAD_PAD), F32), pltpu.VMEM((tk, V_DIM), F32)],
        compiler_params=_params("parallel", "parallel", "arbitrary"),
    )(q, keys, kv, do, lse_row, delta_row)


def _loss_bwd(x, y, gate, g, target, *, name):
    s, d = x.shape
    tr = _row_tile(s, d)

    def body(x_ref, y_ref, gate_ref, g_ref, t_ref, dx_ref, stats_ref, loss_ref):
        xv = x_ref[...] + gate_ref[...] * y_ref[...]
        r = lax.rsqrt(jnp.mean(xv * xv, axis=-1, keepdims=True) + NORM_EPS)
        xhat = xv * r
        gv = g_ref[...]
        err = xhat * gv - t_ref[...]
        dy = err / d
        dxhat = dy * gv
        dx_ref[...] = r * (dxhat - xhat * jnp.mean(dxhat * xhat, axis=-1, keepdims=True))

        @pl.when(pl.program_id(0) == 0)
        def _():
            stats_ref[...] = jnp.zeros_like(stats_ref)
            loss_ref[...] = jnp.zeros_like(loss_ref)

        stats_ref[0:1, :] += jnp.sum(dy * xhat, axis=0, keepdims=True)
        loss_ref[...] += 0.5 * jnp.sum(jnp.mean(err * err, axis=-1, keepdims=True))

    row = pl.BlockSpec((tr, d), lambda i: (i, 0))
    vec = pl.BlockSpec((1, d), lambda i: (0, 0))
    return pl.pallas_call(
        body,
        name=name,
        grid=(s // tr,),
        in_specs=[row, row, vec, vec, row],
        out_specs=[row, pl.BlockSpec((8, d), lambda i: (0, 0)), pl.BlockSpec((8, LANES), lambda i: (0, 0))],
        out_shape=[jax.ShapeDtypeStruct((s, d), F32), jax.ShapeDtypeStruct((8, d), F32), jax.ShapeDtypeStruct((8, LANES), F32)],
        compiler_params=_params("arbitrary"),
    )(x, y, gate, g, target)


def _adam_math(w, g, m, v):
    new_m = ADAM_B1 * m + (1.0 - ADAM_B1) * g
    new_v = ADAM_B2 * v + (1.0 - ADAM_B2) * (g * g)
    m_hat = new_m / (1.0 - ADAM_B1**ADAM_STEP)
    v_hat = new_v / (1.0 - ADAM_B2**ADAM_STEP)
    return -ADAM_LR * (m_hat / (jnp.sqrt(v_hat) + ADAM_EPS) + ADAM_WD * w), new_m, new_v


def _adamw(w, g, m, v, *, name):
    rows, cols = w.shape
    tr = _tile(rows, max(8, (1 << 18) // cols), 8)

    def body(w_ref, g_ref, m_ref, v_ref, go_ref, d_ref, mo_ref, vo_ref):
        gv = g_ref[...]
        go_ref[...] = gv
        d_ref[...], mo_ref[...], vo_ref[...] = _adam_math(w_ref[...], gv, m_ref[...], v_ref[...])

    spec = pl.BlockSpec((tr, cols), lambda i: (i, 0))
    return pl.pallas_call(
        body,
        name=name,
        grid=(rows // tr,),
        in_specs=[spec] * 4,
        out_specs=[spec] * 4,
        out_shape=[jax.ShapeDtypeStruct((rows, cols), F32)] * 4,
        compiler_params=_params("parallel"),
    )(w, g, m, v)


def _tp_adamw(sc16, dm, w, m, v, *, name):
    nl, d, n = w.shape
    tm = _tile(d, 512, LANES)
    tn = _tile(n, 1024, LANES)

    def body(sc_ref, dm_ref, w_ref, m_ref, v_ref, go_ref, d_ref, mo_ref, vo_ref):
        gv = lax.dot_general(sc_ref[...].astype(BF16), dm_ref[0].astype(BF16), (((0,), (0,)), ((), ())), preferred_element_type=F32)
        go_ref[0] = gv
        d_ref[0], mo_ref[0], vo_ref[0] = _adam_math(w_ref[0], gv, m_ref[0], v_ref[0])

    blk = pl.BlockSpec((1, tm, tn), lambda l, i, j: (l, i, j))
    return pl.pallas_call(
        body,
        name=name,
        grid=(nl, d // tm, n // tn),
        in_specs=[pl.BlockSpec((16, tm), lambda l, i, j: (0, i)), pl.BlockSpec((1, 16, tn), lambda l, i, j: (l, 0, j)), blk, blk, blk],
        out_specs=[blk] * 4,
        out_shape=[jax.ShapeDtypeStruct((nl, d, n), F32)] * 4,
        compiler_params=_params("parallel", "parallel", "parallel"),
    )(sc16, dm, w, m, v)


def _sum_devices(x, *, name):
    def body(x_ref, o_ref):
        acc = x_ref[0]
        for k in range(1, N_DEV):
            acc = acc + x_ref[k]
        o_ref[...] = acc

    return pl.pallas_call(body, name=name, out_shape=jax.ShapeDtypeStruct(x.shape[1:], F32))(x)


def _sum_chips(slots, *, name):
    _, n, c = slots.shape
    tr = _tile(n, max(8, (1 << 18) // c), 8)

    def body(s0, s1, s2, s3, o_ref):
        o_ref[...] = ((s0[0].astype(F32) + s1[0].astype(F32)) + s2[0].astype(F32)) + s3[0].astype(F32)

    return pl.pallas_call(
        body,
        name=name,
        grid=(n // tr,),
        in_specs=[pl.BlockSpec((1, tr, c), functools.partial(lambda k, i: (k, i, 0), k)) for k in range(N_CHIPS)],
        out_specs=pl.BlockSpec((tr, c), lambda i: (i, 0)),
        out_shape=jax.ShapeDtypeStruct((n, c), F32),
        compiler_params=_params("parallel"),
    )(slots, slots, slots, slots)


def _place():
    mx, my, mc = lax.axis_index("x"), lax.axis_index("y"), lax.axis_index("c")
    chips = [(1 - mx, my), (mx, 1 - my), (1 - mx, 1 - my)]
    return mx, my, mc, chips


def _remote(src, dst, send_sem, recv_sem, device):
    return pltpu.make_async_remote_copy(src_ref=src, dst_ref=dst, send_sem=send_sem, recv_sem=recv_sem, device_id=device, device_id_type=MESH)


def _allgather8(x, *, name):
    def body(x_ref, out_ref, send_sems, recv_sems, local_sem):
        mx, my, mc, chips = _place()
        me, sibling = (mx, my, mc), (mx, my, 1 - mc)

        def slot(px, py, pc):
            return out_ref.at[4 * px + 2 * py + pc]

        def copy(k, block, to, src=None):
            return _remote(slot(*block) if src is None else src, slot(*block), send_sems.at[k], recv_sems.at[k], to)

        mine = pltpu.make_async_copy(x_ref, slot(*me), local_sem)
        mine.start()
        first = [copy(0, me, sibling, src=x_ref)] + [copy(1 + j, me, (*chip, mc), src=x_ref) for j, chip in enumerate(chips)]
        for cp in first:
            cp.start()
        passed = [copy(4 + j, (*chip, mc), sibling) for j, chip in enumerate(chips)]
        for j, chip in enumerate(chips):
            copy(1 + j, (*chip, mc), me).wait_recv()
            passed[j].start()
        copy(0, sibling, me).wait_recv()
        for j, chip in enumerate(chips):
            copy(4 + j, (*chip, 1 - mc), me).wait_recv()
        for cp in first + passed:
            cp.wait_send()
        mine.wait()

    return pl.pallas_call(
        body,
        name=name,
        out_shape=jax.ShapeDtypeStruct((N_DEV,) + x.shape, x.dtype),
        in_specs=[pl.BlockSpec(memory_space=pltpu.VMEM)],
        out_specs=pl.BlockSpec(memory_space=pltpu.VMEM),
        scratch_shapes=[pltpu.SemaphoreType.DMA((7,)), pltpu.SemaphoreType.DMA((7,)), pltpu.SemaphoreType.DMA],
    )(x)


class _Geom:
    def __init__(self, shape3, axis):
        self.shape3, self.axis = shape3, axis
        nl, r, c = shape3
        self.rs, self.cs = (r // N_CHIPS, c) if axis == 1 else (r, c // N_CHIPS)
        self.hl, self.hr = (nl // 2, self.rs) if nl > 1 else (1, self.rs // 2)
        self.shard = (nl, self.rs, self.cs)
        self.half = (self.hl, self.hr, self.cs)

    def in_full(self, ref, chip, core):
        nl = self.shape3[0]
        l0 = core * self.hl if nl > 1 else 0
        r0 = (chip * self.rs if self.axis == 1 else 0) + (0 if nl > 1 else core * self.hr)
        c0 = chip * self.cs if self.axis == 2 else 0
        return ref.at[pl.ds(l0, self.hl), pl.ds(r0, self.hr), pl.ds(c0, self.cs)]

    def in_shard(self, ref, core):
        if self.shape3[0] > 1:
            return ref.at[pl.ds(core * self.hl, self.hl), pl.ds(0, self.hr), pl.ds(0, self.cs)]
        return ref.at[pl.ds(0, 1), pl.ds(core * self.hr, self.hr), pl.ds(0, self.cs)]


def _gather_weights(shards, geoms, *, name):
    n = len(shards)

    def body(*refs):
        shard_refs, full_refs = refs[:n], refs[n : 2 * n]
        ici_send, ici_recv, d2d_send, d2d_recv, local_sems = refs[2 * n :]
        mx, my, mc, chips = _place()
        me = 2 * mx + my
        local, sent = [], []
        for w, geom in enumerate(geoms):
            for half in range(2):
                cp = pltpu.make_async_copy(geom.in_shard(shard_refs[w], half), geom.in_full(full_refs[w], me, half), local_sems.at[2 * w + half])
                cp.start()
                local.append(cp)
            for k, chip in enumerate(chips):
                cp = _remote(geom.in_shard(shard_refs[w], mc), geom.in_full(full_refs[w], me, mc), ici_send.at[3 * w + k], ici_recv.at[3 * w + k], (*chip, mc))
                cp.start()
                sent.append(cp)
        passed = []
        for w, geom in enumerate(geoms):
            for k, (px, py) in enumerate(chips):
                landed = geom.in_full(full_refs[w], 2 * px + py, mc)
                _remote(landed, landed, ici_send.at[3 * w + k], ici_recv.at[3 * w + k], (px, py, mc)).wait_recv()
                cp = _remote(landed, landed, d2d_send.at[3 * w + k], d2d_recv.at[3 * w + k], (mx, my, 1 - mc))
                cp.start()
                passed.append(cp)
        for w, geom in enumerate(geoms):
            for k, (px, py) in enumerate(chips):
                other = geom.in_full(full_refs[w], 2 * px + py, 1 - mc)
                _remote(other, other, d2d_send.at[3 * w + k], d2d_recv.at[3 * w + k], (mx, my, 1 - mc)).wait_recv()
        for cp in sent + passed:
            cp.wait_send()
        for cp in local:
            cp.wait()

    return pl.pallas_call(
        body,
        name=name,
        in_specs=[ANY] * n,
        out_specs=[ANY] * n,
        out_shape=[jax.ShapeDtypeStruct(g.shape3, s.dtype) for g, s in zip(geoms, shards)],
        scratch_shapes=[pltpu.SemaphoreType.DMA((3 * n,))] * 4 + [pltpu.SemaphoreType.DMA((2 * n,))],
    )(*shards)


def _reduce_pair(grads, geoms, *, name):
    n = len(grads)

    def body(*refs):
        grad_refs, own_refs, theirs_refs = refs[:n], refs[n : 2 * n], refs[2 * n : 3 * n]
        send_sems, recv_sems, local_sems = refs[3 * n :]
        mx, my, mc, _ = _place()
        sibling = (mx, my, 1 - mc)
        local, remote = [], []
        for w, geom in enumerate(geoms):
            for chip in range(N_CHIPS):
                cp = pltpu.make_async_copy(geom.in_full(grad_refs[w], chip, mc), own_refs[w].at[chip], local_sems.at[4 * w + chip])
                cp.start()
                local.append(cp)
                cp = _remote(geom.in_full(grad_refs[w], chip, 1 - mc), theirs_refs[w].at[chip], send_sems.at[4 * w + chip], recv_sems.at[4 * w + chip], sibling)
                cp.start()
                remote.append(cp)
        for cp in remote:
            cp.wait_recv()
        for cp in remote:
            cp.wait_send()
        for cp in local:
            cp.wait()

    half = [jax.ShapeDtypeStruct((N_CHIPS,) + g.half, x.dtype) for g, x in zip(geoms, grads)]
    res = pl.pallas_call(
        body,
        name=name,
        in_specs=[ANY] * n,
        out_specs=[ANY] * (2 * n),
        out_shape=half + half,
        scratch_shapes=[pltpu.SemaphoreType.DMA((4 * n,))] * 3,
    )(*grads)
    return res[:n], res[n:]


def _reduce_chips(parts, *, name):
    n = len(parts)

    def body(*refs):
        part_refs, slot_refs = refs[:n], refs[n : 2 * n]
        send_sems, recv_sems, local_sems = refs[2 * n :]
        mx, my, mc, chips = _place()
        me = 2 * mx + my
        local, remote = [], []
        for w in range(n):
            cp = pltpu.make_async_copy(part_refs[w].at[me], slot_refs[w].at[me], local_sems.at[w])
            cp.start()
            local.append(cp)
            for k, (px, py) in enumerate(chips):
                cp = _remote(part_refs[w].at[2 * px + py], slot_refs[w].at[me], send_sems.at[3 * w + k], recv_sems.at[3 * w + k], (px, py, mc))
                cp.start()
                remote.append(cp)
        for w in range(n):
            for k, (px, py) in enumerate(chips):
                got = slot_refs[w].at[2 * px + py]
                _remote(got, got, send_sems.at[3 * w + k], recv_sems.at[3 * w + k], (px, py, mc)).wait_recv()
        for cp in remote:
            cp.wait_send()
        for cp in local:
            cp.wait()

    return pl.pallas_call(
        body,
        name=name,
        in_specs=[ANY] * n,
        out_specs=[ANY] * n,
        out_shape=[jax.ShapeDtypeStruct(p.shape, p.dtype) for p in parts],
        scratch_shapes=[pltpu.SemaphoreType.DMA((3 * n,))] * 2 + [pltpu.SemaphoreType.DMA((n,))],
    )(*parts)


def _join_pair(halves, *, name):
    n = len(halves)

    def body(*refs):
        half_refs, both_refs = refs[:n], refs[n : 2 * n]
        send_sems, recv_sems, local_sems = refs[2 * n :]
        mx, my, mc, _ = _place()
        local, remote = [], []
        for w in range(n):
            cp = pltpu.make_async_copy(half_refs[w], both_refs[w].at[mc], local_sems.at[w])
            cp.start()
            local.append(cp)
            cp = _remote(half_refs[w], both_refs[w].at[mc], send_sems.at[w], recv_sems.at[w], (mx, my, 1 - mc))
            cp.start()
            remote.append(cp)
        for w in range(n):
            got = both_refs[w].at[1 - mc]
            _remote(got, got, send_sems.at[w], recv_sems.at[w], (mx, my, 1 - mc)).wait_recv()
        for cp in remote:
            cp.wait_send()
        for cp in local:
            cp.wait()

    return pl.pallas_call(
        body,
        name=name,
        in_specs=[ANY] * n,
        out_specs=[ANY] * n,
        out_shape=[jax.ShapeDtypeStruct((2,) + h.shape, h.dtype) for h in halves],
        scratch_shapes=[pltpu.SemaphoreType.DMA((n,))] * 3,
    )(*halves)


WEIGHTS = ("mod_w", "mod_b", "norm_mix", "norm_ffn", "pool_w", "pool_scale", "kv_mod_w", "kv_mod_b", "kv_in_norm", "w_dkv", "kv_norm",
           "w_uk", "w_uv", "w_kr", "w_dq", "q_norm", "w_uq", "w_o", "ffn_gate", "ffn_up", "ffn_down", "final_norm")
SMALL = ("mod_b", "kv_mod_b", "norm_mix", "norm_ffn", "kv_in_norm", "kv_norm", "q_norm", "final_norm")


def _rows(v):
    return v.reshape(-1, LANES)


def _pad_rows(a):
    return jnp.pad(a, ((0, (-a.shape[0]) % 8), (0, 0)))


def _vec(v):
    return v.reshape(1, -1)


def _step(x, c, positions, target, wts, mom, var):
    _, s, d = x.shape
    depth, n_a, n_b = wts["mod_w"].shape[0], wts["pool_w"].shape[0], wts["w_dq"].shape[0]
    assert n_b == 2 and n_a + n_b == depth
    heads = d // V_DIM
    kvr, qr = wts["w_dkv"].shape[1], wts["w_dq"].shape[2]
    ffn = wts["ffn_gate"].shape[2] * N_CHIPS
    pool_c = d // len(POOL_WINDOWS)
    nmod, nkv = N_MOD * d, 2 * d
    mx, my, mc = lax.axis_index("x"), lax.axis_index("y"), lax.axis_index("c")
    chip, dev = 2 * mx + my, 4 * mx + 2 * my + mc
    xs, tgt = x[0], target[0]

    inv_freq = 1.0 / (ROPE_THETA ** (jnp.arange(0, ROPE_DIM, 2, dtype=F32) / ROPE_DIM))
    ang = positions[0].astype(F32)[:, None] * inv_freq
    cos, sin, zero = jnp.cos(ang), jnp.sin(ang), jnp.zeros((s, LANES - ROPE_DIM), F32)
    cos_t = jnp.concatenate([cos, cos, zero], axis=1)
    sin_fwd = jnp.concatenate([-sin, sin, zero], axis=1)
    sin_bwd = jnp.concatenate([sin, -sin, zero], axis=1)

    c_rows, ps_rows = d // LANES, n_a * (d // N_CHIPS) // LANES
    cond = _allgather8(_pad_rows(jnp.concatenate([_rows(c), _rows(wts["pool_scale"])])), name="gather_cond")
    c_all = cond[:, :c_rows].reshape(N_DEV, d)
    pool_scale = cond[0::2, c_rows : c_rows + ps_rows].reshape(N_CHIPS, n_a, d // N_CHIPS).transpose(1, 0, 2).reshape(n_a, d)
    sc16 = _elementwise(_silu, [jnp.pad(c_all, ((0, 16 - N_DEV), (0, 0)))], [F32], name="silu_cond")[0]

    mod_bias = lax.dynamic_slice_in_dim(wts["mod_b"], chip * (nmod // N_CHIPS), nmod // N_CHIPS, axis=1)[:, None, :]
    kv_bias = lax.dynamic_slice_in_dim(wts["kv_mod_b"], chip * (nkv // N_CHIPS), nkv // N_CHIPS).reshape(1, 1, -1)
    mod_part = _tp_fwd(sc16, wts["mod_w"], mod_bias, name="mod_fwd")
    kv_part = _tp_fwd(sc16, wts["kv_mod_w"][None], kv_bias, name="kv_mod_fwd")
    part = jnp.concatenate([mod_part[i, :N_DEV] for i in range(depth)] + [kv_part[0, :N_DEV]], axis=1)
    ncol = part.shape[1]
    gathered = _allgather8(_pad_rows(_rows(part)), name="gather_mods")
    gathered = gathered[0::2, : N_DEV * ncol // LANES].reshape(N_CHIPS, N_DEV, ncol)
    mine = lax.dynamic_index_in_dim(gathered, dev, axis=1, keepdims=False)
    per = nmod // N_CHIPS
    mods = [mine[:, i * per : (i + 1) * per].reshape(N_MOD, 1, d) for i in range(depth)]
    kv_shift, kv_scale = mine[:, depth * per :].reshape(2, 1, d)

    gathered_names = ("pool_w", "w_dkv", "w_uk", "w_uv", "w_kr", "w_dq", "w_uq", "w_o", "ffn_gate", "ffn_up", "ffn_down")
    geoms = {
        "pool_w": _Geom((n_a * len(POOL_WINDOWS), pool_c, pool_c), 1),
        "w_dkv": _Geom((1, d, kvr), 1),
        "w_uk": _Geom((1, kvr, heads * NOPE_DIM), 2),
        "w_uv": _Geom((1, kvr, heads * V_DIM), 2),
        "w_kr": _Geom((1, d, ROPE_DIM), 1),
        "w_dq": _Geom((n_b, d, qr), 1),
        "w_uq": _Geom((n_b, qr, heads * (NOPE_DIM + ROPE_DIM)), 2),
        "w_o": _Geom((n_b, d, d), 1),
        "ffn_gate": _Geom((depth, d, ffn), 2),
        "ffn_up": _Geom((depth, d, ffn), 2),
        "ffn_down": _Geom((depth, ffn, d), 1),
    }
    geom_list = [geoms[n] for n in gathered_names]
    shards = [wts[n].astype(BF16).reshape(geoms[n].shard) for n in gathered_names]
    full = dict(zip(gathered_names, _gather_weights(shards, geom_list, name="gather_weights")))

    pool_w = full["pool_w"].reshape(n_a, len(POOL_WINDOWS), pool_c, pool_c)
    w_uq = full["w_uq"].reshape(n_b, qr, heads, NOPE_DIM + ROPE_DIM)
    w_q = jnp.pad(w_uq, ((0, 0), (0, 0), (0, 0), (0, HEAD_PAD - NOPE_DIM - ROPE_DIM))).reshape(n_b, qr, heads * HEAD_PAD)
    w_ukv = jnp.stack([full["w_uk"].reshape(kvr, heads, NOPE_DIM), full["w_uv"].reshape(kvr, heads, V_DIM)], axis=2).reshape(kvr, heads * HEAD_PAD)
    w_dkvkr = jnp.concatenate([full["w_dkv"][0], full["w_kr"][0], jnp.zeros((d, LANES - ROPE_DIM), BF16)], axis=1)
    w_dq, w_o = full["w_dq"], full["w_o"]

    norm_mix, norm_ffn = wts["norm_mix"], wts["norm_ffn"]
    saved = []
    cur, pending = xs, None
    kv_side = None
    for i in range(depth):
        shift_m, scale_m, gate_m, shift_f, scale_f, gate_f = mods[i]
        h1_dtype = F32 if i < n_a else BF16
        if pending is None:
            x0 = cur
            h1 = _norm_fwd(x0, _vec(norm_mix[i]), scale=scale_m, shift=shift_m, out_dtype=h1_dtype, name="norm_mix_first")
        else:
            x0, h1 = _norm_fwd(cur, _vec(norm_mix[i]), scale=scale_m, shift=shift_m, y=pending[0], gate=pending[1], out_dtype=h1_dtype, name="norm_mix")
        lay = {"x0": x0, "h1": h1}
        if i == n_a:
            h_kv = _norm_fwd(x0, _vec(wts["kv_in_norm"]), scale=kv_scale, shift=kv_shift, name="norm_kv_in")
            pre = _mm(h_kv, w_dkvkr, name="kv_down", tn=kvr + LANES)
            ckv_pre, kr_pre = pre[:, :kvr], pre[:, kvr:]
            ckv = _norm_fwd(ckv_pre, _vec(wts["kv_norm"]), name="norm_kv")
            kv = _mm(ckv, w_ukv, out_dtype=BF16, name="kv_up")
            keys = _build_keys(kv, kr_pre, cos_t, sin_fwd, name="build_keys")
            kv_side = {"h_kv": h_kv, "ckv_pre": ckv_pre, "ckv": ckv, "kv": kv, "keys": keys, "x0": x0}
        if i < n_a:
            pooled = _pool(h1, transpose=False, out_dtype=BF16, name="pool_fwd")
            y_pre = _gmm(pooled, pool_w[i], mode="nn", out_dtype=F32, name="pool_mix")
            gate_eff = gate_m * _vec(pool_scale[i])
            lay.update(pooled=pooled)
        else:
            l = i - n_a
            cq_pre = _mm(h1, w_dq[l], name="q_down")
            cq = _norm_fwd(cq_pre, _vec(wts["q_norm"][l]), name="norm_q")
            q = _rope_heads(_mm(cq, w_q[l], name="q_up"), cos_t, sin_fwd, out_dtype=BF16, name="rope_q")
            o, lse = _attn_fwd(q, kv_side["keys"], kv_side["kv"], name="attn_fwd")
            y_pre = _mm(o, w_o[l], name="attn_out")
            gate_eff = gate_m
            lay.update(cq_pre=cq_pre, cq=cq, q=q, o=o, lse=lse)
        x1, h2 = _norm_fwd(x0, _vec(norm_ffn[i]), scale=scale_f, shift=shift_f, y=y_pre, gate=gate_eff, name="norm_ffn")
        a = _mm(h2, full["ffn_gate"], b_idx=i, name="ffn_gate")
        b = _mm(h2, full["ffn_up"], b_idx=i, name="ffn_up")
        z = _elementwise(_swiglu_fwd, [a, b], [BF16], name="swiglu")[0]
        f = _mm(z, full["ffn_down"], b_idx=i, name="ffn_down")
        lay.update(y_pre=y_pre, gate_eff=gate_eff, x1=x1, h2=h2, a=a, b=b, z=z, f=f)
        saved.append(lay)
        cur, pending = x1, (f, gate_f)

    dx, final_stats, loss_tile = _loss_bwd(cur, pending[0], pending[1], _vec(wts["final_norm"]), tgt, name="loss")
    loss = lax.psum(loss_tile[0, 0], ("x", "y", "c"))

    grad_full = {}
    g_gate = g_up = g_down = None
    g_wo, g_wq, g_wdq = [None] * n_b, [None] * n_b, [None] * n_b
    g_pool = [None] * n_a
    dmods = [None] * depth
    g_norm_mix, g_norm_ffn, g_q_norm, g_pool_scale = [None] * depth, [None] * depth, [None] * n_b, [None] * n_a
    dk_layers, dv_layers = [None] * n_b, [None] * n_b
    for i in reversed(range(depth)):
        lay = saved[i]
        shift_m, scale_m, gate_m, shift_f, scale_f, gate_f = mods[i]
        df, sums_gf = _gate_bwd(dx, lay["f"], gate_f, name="gate_bwd")
        dz = _mm(df, full["ffn_down"], tb=True, b_idx=i, name="ffn_down_dx")
        g_down = _mm(lay["z"], df, ta=True, out_dtype=BF16, out_stack=(depth, i, g_down), name="ffn_down_dw")
        da, db = _elementwise(_swiglu_bwd, [dz, lay["a"], lay["b"]], [BF16, BF16], name="swiglu_bwd")
        dh2 = _mm(da, full["ffn_gate"], tb=True, b_idx=i, name="ffn_gate_dx")
        dh2 = _mm(db, full["ffn_up"], tb=True, b_idx=i, add=dh2, name="ffn_up_dx")
        g_gate = _mm(lay["h2"], da, ta=True, out_dtype=BF16, out_stack=(depth, i, g_gate), name="ffn_gate_dw")
        g_up = _mm(lay["h2"], db, ta=True, out_dtype=BF16, out_stack=(depth, i, g_up), name="ffn_up_dw")
        dx1, sums_f = _norm_bwd(lay["x1"], _vec(norm_ffn[i]), dh2, scale=scale_f, resid=dx, name="norm_ffn_bwd")
        dyp, sums_gm = _gate_bwd(dx1, lay["y_pre"], lay["gate_eff"], name="gate_bwd")
        if i < n_a:
            g_pool[i] = _gmm(lay["pooled"], dyp, mode="tn", out_dtype=BF16, name="pool_mix_dw")
            dd = _gmm(dyp, pool_w[i], mode="nt", out_dtype=F32, name="pool_mix_dx")
            dh1 = _pool(dd, transpose=True, out_dtype=F32, name="pool_bwd")
            dgate_m = sums_gm[0] * pool_scale[i]
            g_pool_scale[i] = sums_gm[0] * gate_m[0]
        else:
            l = i - n_a
            do = _mm(dyp, w_o[l], tb=True, out_dtype=BF16, name="attn_out_dx")
            g_wo[l] = _mm(lay["o"], dyp, ta=True, out_dtype=BF16, name="attn_out_dw")
            dq, delta = _attn_bwd_q(lay["q"], kv_side["keys"], kv_side["kv"], lay["o"], do, lay["lse"], name="attn_bwd_q")
            dk_layers[l], dv_layers[l] = _attn_bwd_kv(
                lay["q"], kv_side["keys"], kv_side["kv"], do, lay["lse"].reshape(heads, 1, s), delta.reshape(heads, 1, s), name="attn_bwd_kv"
            )
            dq_pre = _rope_heads(dq, cos_t, sin_bwd, out_dtype=BF16, name="rope_q_bwd")
            dcq = _mm(dq_pre, w_q[l], tb=True, name="q_up_dx")
            g_wq[l] = _mm(lay["cq"], dq_pre, ta=True, out_dtype=BF16, name="q_up_dw")
            dcq_pre, sums_q = _norm_bwd(lay["cq_pre"], _vec(wts["q_norm"][l]), dcq, name="norm_q_bwd")
            g_q_norm[l] = sums_q[2]
            dh1 = _mm(dcq_pre, w_dq[l], tb=True, name="q_down_dx")
            g_wdq[l] = _mm(lay["h1"], dcq_pre, ta=True, out_dtype=BF16, name="q_down_dw")
            dgate_m = sums_gm[0]
        dx, sums_m = _norm_bwd(lay["x0"], _vec(norm_mix[i]), dh1, scale=scale_m, resid=dx1, name="norm_mix_bwd")
        if i == n_a:
            dkv, dkr_pre = _keys_bwd(dk_layers[0], dk_layers[1], dv_layers[0], dv_layers[1], cos_t, sin_bwd, name="keys_bwd")
            dckv = _mm(dkv, w_ukv, tb=True, name="kv_up_dx")
            g_ukv = _mm(kv_side["ckv"], dkv, ta=True, out_dtype=BF16, name="kv_up_dw")
            dckv_pre, sums_kvn = _norm_bwd(kv_side["ckv_pre"], _vec(wts["kv_norm"]), dckv, name="norm_kv_bwd")
            dpre = jnp.concatenate([dckv_pre, dkr_pre], axis=1)
            dh_kv = _mm(dpre, w_dkvkr, tb=True, name="kv_down_dx")
            g_dkvkr = _mm(kv_side["h_kv"], dpre, ta=True, out_dtype=BF16, name="kv_down_dw", tn=kvr + LANES)
            dx, sums_kv = _norm_bwd(lay["x0"], _vec(wts["kv_in_norm"]), dh_kv, scale=kv_scale, resid=dx, name="norm_kv_in_bwd")
        dmods[i] = jnp.concatenate([sums_m[0], sums_m[1], dgate_m, sums_f[0], sums_f[1], sums_gf[0]])
        g_norm_mix[i], g_norm_ffn[i] = sums_m[2], sums_f[2]
    grad_x = dx[None]

    g_ukv = g_ukv.reshape(kvr, heads, 2, NOPE_DIM)
    grad_full = {
        "pool_w": jnp.stack(g_pool).reshape(geoms["pool_w"].shape3),
        "w_dkv": g_dkvkr[None, :, :kvr],
        "w_uk": g_ukv[:, :, 0].reshape(1, kvr, heads * NOPE_DIM),
        "w_uv": g_ukv[:, :, 1].reshape(1, kvr, heads * V_DIM),
        "w_kr": g_dkvkr[None, :, kvr : kvr + ROPE_DIM],
        "w_dq": jnp.stack(g_wdq),
        "w_uq": jnp.stack(g_wq).reshape(n_b, qr, heads, HEAD_PAD)[..., : NOPE_DIM + ROPE_DIM].reshape(geoms["w_uq"].shape3),
        "w_o": jnp.stack(g_wo),
        "ffn_gate": g_gate,
        "ffn_up": g_up,
        "ffn_down": g_down,
    }

    small_grads = {
        "mod_b": jnp.concatenate(dmods),
        "kv_mod_b": jnp.concatenate([sums_kv[0], sums_kv[1]]),
        "norm_mix": jnp.concatenate(g_norm_mix),
        "norm_ffn": jnp.concatenate(g_norm_ffn),
        "kv_in_norm": sums_kv[2],
        "kv_norm": sums_kvn[2],
        "q_norm": jnp.concatenate(g_q_norm),
        "final_norm": final_stats[0],
    }
    packed = jnp.concatenate([small_grads[n] for n in SMALL] + g_pool_scale)
    small_rows = sum(wts[n].size for n in SMALL) // LANES
    every = _allgather8(_pad_rows(_rows(packed)), name="gather_small_grads")
    summed = _sum_devices(every, name="sum_small_grads")

    mod_rows = depth * nmod // LANES
    dm_all = every[:, :mod_rows].reshape(N_DEV, depth, nmod)
    dm = lax.dynamic_slice_in_dim(dm_all, chip * per, per, axis=2).transpose(1, 0, 2)
    dm = jnp.pad(dm, ((0, 0), (0, 16 - N_DEV), (0, 0)))
    dkvm_all = every[:, mod_rows : mod_rows + nkv // LANES].reshape(N_DEV, nkv)
    dkvm = jnp.pad(lax.dynamic_slice_in_dim(dkvm_all, chip * (nkv // N_CHIPS), nkv // N_CHIPS, axis=1), ((0, 16 - N_DEV), (0, 0)))[None]
    results = {}
    results["mod_w"] = _tp_adamw(sc16, dm, wts["mod_w"], mom["mod_w"], var["mod_w"], name="mod_w_update")
    results["kv_mod_w"] = [
        r[0] for r in _tp_adamw(sc16, dkvm, wts["kv_mod_w"][None], mom["kv_mod_w"][None], var["kv_mod_w"][None], name="kv_mod_w_update")
    ]

    ps_grad = lax.dynamic_slice_in_dim(summed[small_rows : small_rows + n_a * d // LANES].reshape(n_a, d), chip * (d // N_CHIPS), d // N_CHIPS, axis=1)
    small_names = SMALL + ("pool_scale",)

    def pack_small(tree):
        return _pad_rows(jnp.concatenate([_rows(tree[n]) for n in small_names]))

    g_small = _pad_rows(jnp.concatenate([summed[:small_rows], _rows(ps_grad)]))
    small_out = _adamw(pack_small(wts), g_small, pack_small(mom), pack_small(var), name="small_update")
    row = 0
    for n in small_names:
        nrow = wts[n].size // LANES
        results[n] = [r[row : row + nrow].reshape(wts[n].shape) for r in small_out]
        row += nrow

    own, theirs = _reduce_pair([grad_full[n] for n in gathered_names], geom_list, name="reduce_pair")
    pair_sum = []
    for n, mine_half, their_half in zip(gathered_names, own, theirs):
        cs = mine_half.shape[-1]
        added = _elementwise(lambda u, v: (u.astype(F32) + v.astype(F32),), [mine_half.reshape(-1, cs), their_half.reshape(-1, cs)], [BF16], name="pair_add_" + n)[0]
        pair_sum.append(added.reshape(mine_half.shape))
    slots = _reduce_chips(pair_sum, name="reduce_chips")
    halves = []
    for n, sl in zip(gathered_names, slots):
        geom = geoms[n]
        total = _sum_chips(sl.reshape(N_CHIPS, geom.hl * geom.hr, geom.cs), name="chip_sum_" + n)
        halves.append(total.reshape(geom.half))
    joined = _join_pair(halves, name="join_pair")
    for n, both in zip(gathered_names, joined):
        cs = geoms[n].cs
        out = _adamw(wts[n].reshape(-1, cs), both.reshape(-1, cs), mom[n].reshape(-1, cs), var[n].reshape(-1, cs), name="update_" + n)
        results[n] = [r.reshape(wts[n].shape) for r in out]

    outs = [loss, grad_x]
    for k in range(4):
        outs += [results[n][k] for n in WEIGHTS]
    return tuple(outs)


def kernel(x, c, positions, mod_w, mod_b, norm_mix, norm_ffn, pool_w, pool_scale, kv_mod_w, kv_mod_b, kv_in_norm, w_dkv, kv_norm, w_uk, w_uv, w_kr, w_dq, q_norm, w_uq, w_o, ffn_gate, ffn_up, ffn_down, final_norm, loss_target, m_mod_w, m_mod_b, m_norm_mix, m_norm_ffn, m_pool_w, m_pool_scale, m_kv_mod_w, m_kv_mod_b, m_kv_in_norm, m_w_dkv, m_kv_norm, m_w_uk, m_w_uv, m_w_kr, m_w_dq, m_q_norm, m_w_uq, m_w_o, m_ffn_gate, m_ffn_up, m_ffn_down, m_final_norm, v_mod_w, v_mod_b, v_norm_mix, v_norm_ffn, v_pool_w, v_pool_scale, v_kv_mod_w, v_kv_mod_b, v_kv_in_norm, v_w_dkv, v_kv_norm, v_w_uk, v_w_uv, v_w_kr, v_w_dq, v_q_norm, v_w_uq, v_w_o, v_ffn_gate, v_ffn_up, v_ffn_down, v_final_norm):
    given = dict(locals())
    wts = {n: given[n] for n in WEIGHTS}
    mom = {n: given["m_" + n] for n in WEIGHTS}
    var = {n: given["v_" + n] for n in WEIGHTS}
    return _step(x, c, positions, loss_target, wts, mom, var)
```

```python
import functools

import jax
import jax.numpy as jnp
from jax import lax
from jax.experimental import pallas as pl
from jax.experimental.pallas import tpu as pltpu

F32 = jnp.float32
BF16 = jnp.bfloat16
MESH = pl.DeviceIdType.MESH
ANY = pl.BlockSpec(memory_space=pl.ANY)

NORM_EPS = 1e-6
POOL_WINDOWS = (2, 4, 8, 16)
NOPE_DIM = 128
ROPE_DIM = 64
V_DIM = 128
HEAD_PAD = 256
SM_SCALE = (NOPE_DIM + ROPE_DIM) ** -0.5
ROPE_THETA = 10000.0
N_MOD = 6
ADAM_LR, ADAM_B1, ADAM_B2, ADAM_EPS, ADAM_WD, ADAM_STEP = 0.001, 0.9, 0.999, 1e-08, 0.01, 10
N_CHIPS = 4
N_DEV = 8
LANES = 128
HALO = 128
VMEM_LIMIT = 48 * 1024 * 1024


def _tile(dim, pref, align):
    if dim <= pref:
        return dim
    t = (pref // align) * align
    while t >= align:
        if dim % t == 0:
            return t
        t -= align
    return dim


def _params(*sem):
    return pltpu.CompilerParams(dimension_semantics=sem, vmem_limit_bytes=VMEM_LIMIT)


def _mm(a, b, *, name, ta=False, tb=False, out_dtype=F32, b_idx=None, out_stack=None, tm=1024, tn=1024, tk=None):
    m, k = (a.shape[1], a.shape[0]) if ta else a.shape
    b2 = b.shape if b_idx is None else b.shape[1:]
    kb, n = (b2[1], b2[0]) if tb else b2
    assert k == kb, (a.shape, b.shape, ta, tb)
    tm = _tile(m, tm, LANES)
    tn = _tile(n, tn, LANES)
    tk = _tile(k, 2048 if tk is None else tk, LANES)
    nk = k // tk
    dims = (((0 if ta else 1,), (1 if tb else 0,)), ((), ()))

    def body(*refs):
        refs = list(refs)
        a_ref, b_ref = refs.pop(0), refs.pop(0)
        if out_stack is not None and out_stack[2] is not None:
            refs.pop(0)
        o_ref = refs.pop(0)
        part = lax.dot_general(a_ref[...].astype(BF16), b_ref[...].astype(BF16), dims, preferred_element_type=F32)

        def finish(acc):
            o_ref[...] = acc.astype(o_ref.dtype)

        if nk == 1:
            finish(part)
        else:
            acc_ref = refs[-1]
            step = pl.program_id(2)

            @pl.when(step == 0)
            def _():
                acc_ref[...] = part

            @pl.when(step > 0)
            def _():
                acc_ref[...] += part

            @pl.when(step == nk - 1)
            def _():
                finish(acc_ref[...])

    a_spec = pl.BlockSpec((tk, tm), lambda i, j, s: (s, i)) if ta else pl.BlockSpec((tm, tk), lambda i, j, s: (i, s))
    if b_idx is None:
        b_spec = pl.BlockSpec((tn, tk), lambda i, j, s: (j, s)) if tb else pl.BlockSpec((tk, tn), lambda i, j, s: (s, j))
    elif tb:
        b_spec = pl.BlockSpec((None, tn, tk), lambda i, j, s: (b_idx, j, s))
    else:
        b_spec = pl.BlockSpec((None, tk, tn), lambda i, j, s: (b_idx, s, j))
    in_specs = [a_spec, b_spec]
    args = [a, b]
    aliases = {}
    if out_stack is None:
        out_spec = pl.BlockSpec((tm, tn), lambda i, j, s: (i, j))
        out_shape = jax.ShapeDtypeStruct((m, n), out_dtype)
    else:
        n_stack, idx, prev = out_stack
        out_spec = pl.BlockSpec((None, tm, tn), lambda i, j, s: (idx, i, j))
        out_shape = jax.ShapeDtypeStruct((n_stack, m, n), out_dtype)
        if prev is not None:
            aliases = {len(args): 0}
            in_specs.append(ANY)
            args.append(prev)
    return pl.pallas_call(
        body,
        name=name,
        grid=(m // tm, n // tn, nk),
        in_specs=in_specs,
        out_specs=out_spec,
        out_shape=out_shape,
        input_output_aliases=aliases,
        scratch_shapes=[pltpu.VMEM((tm, tn), F32)] if nk > 1 else [],
        compiler_params=_params("parallel", "parallel", "arbitrary"),
    )(*args)


def _tp_fwd(sc16, w, bias, *, name):
    nl, d, n = w.shape
    tn = _tile(n, 512, LANES)

    def body(sc_ref, w_ref, b_ref, o_ref):
        o_ref[0] = jnp.dot(sc_ref[...].astype(BF16), w_ref[0].astype(BF16), preferred_element_type=F32) + b_ref[0]

    return pl.pallas_call(
        body,
        name=name,
        grid=(nl, n // tn),
        in_specs=[
            pl.BlockSpec((16, d), lambda l, j: (0, 0)),
            pl.BlockSpec((1, d, tn), lambda l, j: (l, 0, j)),
            pl.BlockSpec((1, 1, tn), lambda l, j: (l, 0, j)),
        ],
        out_specs=pl.BlockSpec((1, 16, tn), lambda l, j: (l, 0, j)),
        out_shape=jax.ShapeDtypeStruct((nl, 16, n), F32),
        compiler_params=_params("parallel", "parallel"),
    )(sc16, w, bias)


_NT = (((1,), (1,)), ((), ()))
_TN = (((0,), (0,)), ((), ()))


def _silu_parts(a):
    sig = jax.nn.sigmoid(a)
    return a * sig, sig * (1.0 + a * (1.0 - sig))


def _ffn_in(h, w_gate, w_up, layer, *, name):
    s, d = h.shape
    f = w_gate.shape[2]
    tm, tn = _tile(s, 1024, LANES), _tile(f, 512, LANES)

    def body(h_ref, g_ref, u_ref, a_ref, b_ref, z_ref):
        hv = h_ref[...]
        a = jnp.dot(hv, g_ref[...], preferred_element_type=F32)
        b = jnp.dot(hv, u_ref[...], preferred_element_type=F32)
        a_ref[...] = a.astype(a_ref.dtype)
        b_ref[...] = b.astype(b_ref.dtype)
        z_ref[...] = (_silu_parts(a)[0] * b).astype(z_ref.dtype)

    w_spec = pl.BlockSpec((None, d, tn), lambda i, j: (layer, 0, j))
    out = pl.BlockSpec((tm, tn), lambda i, j: (i, j))
    return pl.pallas_call(
        body,
        name=name,
        grid=(s // tm, f // tn),
        in_specs=[pl.BlockSpec((tm, d), lambda i, j: (i, 0)), w_spec, w_spec],
        out_specs=[out] * 3,
        out_shape=[jax.ShapeDtypeStruct((s, f), BF16)] * 3,
        compiler_params=_params("parallel", "parallel"),
    )(h, w_gate, w_up)


def _ffn_down_bwd(df, w_down, layer, a, b, *, name):
    s, d = df.shape
    f = w_down.shape[1]
    tm, tn = _tile(s, 1024, LANES), _tile(f, 512, LANES)

    def body(df_ref, w_ref, a_ref, b_ref, da_ref, db_ref):
        dz = lax.dot_general(df_ref[...], w_ref[...], _NT, preferred_element_type=F32)
        silu, dsilu = _silu_parts(a_ref[...].astype(F32))
        da_ref[...] = (dz * b_ref[...].astype(F32) * dsilu).astype(da_ref.dtype)
        db_ref[...] = (dz * silu).astype(db_ref.dtype)

    blk = pl.BlockSpec((tm, tn), lambda i, j: (i, j))
    return pl.pallas_call(
        body,
        name=name,
        grid=(s // tm, f // tn),
        in_specs=[pl.BlockSpec((tm, d), lambda i, j: (i, 0)), pl.BlockSpec((None, tn, d), lambda i, j: (layer, j, 0)), blk, blk],
        out_specs=[blk, blk],
        out_shape=[jax.ShapeDtypeStruct((s, f), BF16)] * 2,
        compiler_params=_params("parallel", "parallel"),
    )(df, w_down, a, b)


def _ffn_in_dx(da, db, w_gate, w_up, layer, *, name):
    s, f = da.shape
    d = w_gate.shape[1]
    tm, tn, tk = _tile(s, 1024, LANES), _tile(d, 1024, LANES), _tile(f, 2048, LANES)
    nk = f // tk

    def body(da_ref, db_ref, g_ref, u_ref, o_ref, acc_ref):
        part = lax.dot_general(da_ref[...], g_ref[...], _NT, preferred_element_type=F32)
        part = part + lax.dot_general(db_ref[...], u_ref[...], _NT, preferred_element_type=F32)
        step = pl.program_id(2)

        @pl.when(step == 0)
        def _():
            acc_ref[...] = part

        @pl.when(step > 0)
        def _():
            acc_ref[...] += part

        @pl.when(step == nk - 1)
        def _():
            o_ref[...] = acc_ref[...]

    x_spec = pl.BlockSpec((tm, tk), lambda i, j, k: (i, k))
    w_spec = pl.BlockSpec((None, tn, tk), lambda i, j, k: (layer, j, k))
    return pl.pallas_call(
        body,
        name=name,
        grid=(s // tm, d // tn, nk),
        in_specs=[x_spec, x_spec, w_spec, w_spec],
        out_specs=pl.BlockSpec((tm, tn), lambda i, j, k: (i, j)),
        out_shape=jax.ShapeDtypeStruct((s, d), F32),
        scratch_shapes=[pltpu.VMEM((tm, tn), F32)],
        compiler_params=_params("parallel", "parallel", "arbitrary"),
    )(da, db, w_gate, w_up)


def _ffn_in_dw(h, da, db, layer, n_layers, prev, *, name):
    s, d = h.shape
    f = da.shape[1]
    tm, tn, tk = _tile(d, 1024, LANES), _tile(f, 512, LANES), _tile(s, 2048, LANES)
    nk = s // tk

    def body(*refs):
        h_ref, da_ref, db_ref = refs[:3]
        g_ref, u_ref, g_acc, u_acc = refs[-4:]
        hv = h_ref[...]
        pg = lax.dot_general(hv, da_ref[...], _TN, preferred_element_type=F32)
        pu = lax.dot_general(hv, db_ref[...], _TN, preferred_element_type=F32)
        step = pl.program_id(2)

        @pl.when(step == 0)
        def _():
            g_acc[...] = pg
            u_acc[...] = pu

        @pl.when(step > 0)
        def _():
            g_acc[...] += pg
            u_acc[...] += pu

        @pl.when(step == nk - 1)
        def _():
            g_ref[...] = g_acc[...].astype(g_ref.dtype)
            u_ref[...] = u_acc[...].astype(u_ref.dtype)

    y_spec = pl.BlockSpec((tk, tn), lambda i, j, k: (k, j))
    out = pl.BlockSpec((None, tm, tn), lambda i, j, k: (layer, i, j))
    args, in_specs, aliases = [h, da, db], [pl.BlockSpec((tk, tm), lambda i, j, k: (k, i)), y_spec, y_spec], {}
    if prev is not None:
        args += list(prev)
        in_specs += [ANY, ANY]
        aliases = {3: 0, 4: 1}
    return pl.pallas_call(
        body,
        name=name,
        grid=(d // tm, f // tn, nk),
        in_specs=in_specs,
        out_specs=[out, out],
        out_shape=[jax.ShapeDtypeStruct((n_layers, d, f), BF16)] * 2,
        input_output_aliases=aliases,
        scratch_shapes=[pltpu.VMEM((tm, tn), F32)] * 2,
        compiler_params=_params("parallel", "parallel", "arbitrary"),
    )(*args)


def _gmm(a, w, *, name, mode, out_dtype):
    s = a.shape[0]
    g = len(POOL_WINDOWS)
    c = a.shape[1] // g
    tr = _tile(s, 1024, LANES)
    n_row = s // tr

    if mode == "tn":

        def body(a_ref, b_ref, o_ref, acc_ref):
            part = lax.dot_general(a_ref[...].astype(BF16), b_ref[...].astype(BF16), (((0,), (0,)), ((), ())), preferred_element_type=F32)

            @pl.when(pl.program_id(1) == 0)
            def _():
                acc_ref[...] = part

            @pl.when(pl.program_id(1) > 0)
            def _():
                acc_ref[...] += part

            @pl.when(pl.program_id(1) == n_row - 1)
            def _():
                o_ref[0] = acc_ref[...].astype(o_ref.dtype)

        return pl.pallas_call(
            body,
            name=name,
            grid=(g, n_row),
            in_specs=[pl.BlockSpec((tr, c), lambda gi, i: (i, gi)), pl.BlockSpec((tr, c), lambda gi, i: (i, gi))],
            out_specs=pl.BlockSpec((1, c, c), lambda gi, i: (gi, 0, 0)),
            out_shape=jax.ShapeDtypeStruct((g, c, c), out_dtype),
            scratch_shapes=[pltpu.VMEM((c, c), F32)],
            compiler_params=_params("parallel", "arbitrary"),
        )(a, w)

    dims = (((1,), (0 if mode == "nn" else 1,)), ((), ()))

    def body(a_ref, w_ref, o_ref):
        o_ref[...] = lax.dot_general(a_ref[...].astype(BF16), w_ref[0].astype(BF16), dims, preferred_element_type=F32).astype(o_ref.dtype)

    return pl.pallas_call(
        body,
        name=name,
        grid=(g, n_row),
        in_specs=[pl.BlockSpec((tr, c), lambda gi, i: (i, gi)), pl.BlockSpec((1, c, c), lambda gi, i: (gi, 0, 0))],
        out_specs=pl.BlockSpec((tr, c), lambda gi, i: (i, gi)),
        out_shape=jax.ShapeDtypeStruct((s, g * c), out_dtype),
        compiler_params=_params("parallel", "parallel"),
    )(a, w)


def _row_tile(s, d):
    return _tile(s, max(8, (1 << 19) // d), 8)


def _norm_fwd(x, g, *, name, scale=None, shift=None, y=None, gate=None, out_dtype=BF16):
    s, d = x.shape
    tr = _row_tile(s, d)
    has_res, has_mod = y is not None, scale is not None

    def body(*refs):
        refs = list(refs)
        x_ref = refs.pop(0)
        xv = x_ref[...]
        if has_res:
            y_ref, gate_ref = refs.pop(0), refs.pop(0)
            xv = xv + gate_ref[...] * y_ref[...]
        g_ref = refs.pop(0)
        if has_mod:
            scale_ref, shift_ref = refs.pop(0), refs.pop(0)
        if has_res:
            refs.pop(0)[...] = xv
        h = xv * lax.rsqrt(jnp.mean(xv * xv, axis=-1, keepdims=True) + NORM_EPS)
        h = h * g_ref[...]
        if has_mod:
            h = h * (1.0 + scale_ref[...]) + shift_ref[...]
        refs.pop(0)[...] = h.astype(out_dtype)

    row = pl.BlockSpec((tr, d), lambda i: (i, 0))
    vec = pl.BlockSpec((1, d), lambda i: (0, 0))
    args, in_specs = [x], [row]
    if has_res:
        args += [y, gate]
        in_specs += [row, vec]
    args.append(g)
    in_specs.append(vec)
    if has_mod:
        args += [scale, shift]
        in_specs += [vec, vec]
    out_shape, out_specs = [], []
    if has_res:
        out_shape.append(jax.ShapeDtypeStruct((s, d), F32))
        out_specs.append(row)
    out_shape.append(jax.ShapeDtypeStruct((s, d), out_dtype))
    out_specs.append(row)
    res = pl.pallas_call(
        body, name=name, grid=(s // tr,), in_specs=in_specs, out_specs=out_specs, out_shape=out_shape, compiler_params=_params("parallel")
    )(*args)
    return (res[0], res[1]) if has_res else res[0]


def _norm_bwd(x, g, dh, *, name, scale=None, resid=None):
    s, d = x.shape
    tr = _row_tile(s, d)
    has_mod, has_res = scale is not None, resid is not None

    def body(*refs):
        refs = list(refs)
        x_ref, g_ref, dh_ref = refs.pop(0), refs.pop(0), refs.pop(0)
        scale_ref = refs.pop(0) if has_mod else None
        resid_ref = refs.pop(0) if has_res else None
        dx_ref, sums_ref = refs
        xv = x_ref[...]
        r = lax.rsqrt(jnp.mean(xv * xv, axis=-1, keepdims=True) + NORM_EPS)
        xhat = xv * r
        dh32 = dh_ref[...].astype(F32)
        gv = g_ref[...]
        dn = dh32 * (1.0 + scale_ref[...]) if has_mod else dh32
        dxhat = dn * gv
        dx = r * (dxhat - xhat * jnp.mean(dxhat * xhat, axis=-1, keepdims=True))
        if has_res:
            dx = dx + resid_ref[...]
        dx_ref[...] = dx

        @pl.when(pl.program_id(0) == 0)
        def _():
            sums_ref[...] = jnp.zeros_like(sums_ref)

        sums_ref[0:1, :] += jnp.sum(dh32, axis=0, keepdims=True)
        sums_ref[1:2, :] += jnp.sum(dh32 * (xhat * gv), axis=0, keepdims=True)
        sums_ref[2:3, :] += jnp.sum(dn * xhat, axis=0, keepdims=True)

    row = pl.BlockSpec((tr, d), lambda i: (i, 0))
    vec = pl.BlockSpec((1, d), lambda i: (0, 0))
    args, in_specs = [x, g, dh], [row, vec, row]
    if has_mod:
        args.append(scale)
        in_specs.append(vec)
    if has_res:
        args.append(resid)
        in_specs.append(row)
    return pl.pallas_call(
        body,
        name=name,
        grid=(s // tr,),
        in_specs=in_specs,
        out_specs=[row, pl.BlockSpec((8, d), lambda i: (0, 0))],
        out_shape=[jax.ShapeDtypeStruct((s, d), F32), jax.ShapeDtypeStruct((8, d), F32)],
        compiler_params=_params("arbitrary"),
    )(*args)


def _gate_bwd(dx, y, gate, *, name):
    s, d = dx.shape
    tr = _row_tile(s, d)

    def body(dx_ref, y_ref, gate_ref, dy_ref, sums_ref):
        dxv = dx_ref[...]
        dy_ref[...] = (dxv * gate_ref[...]).astype(dy_ref.dtype)

        @pl.when(pl.program_id(0) == 0)
        def _():
            sums_ref[...] = jnp.zeros_like(sums_ref)

        sums_ref[0:1, :] += jnp.sum(dxv * y_ref[...], axis=0, keepdims=True)

    row = pl.BlockSpec((tr, d), lambda i: (i, 0))
    return pl.pallas_call(
        body,
        name=name,
        grid=(s // tr,),
        in_specs=[row, row, pl.BlockSpec((1, d), lambda i: (0, 0))],
        out_specs=[row, pl.BlockSpec((8, d), lambda i: (0, 0))],
        out_shape=[jax.ShapeDtypeStruct((s, d), BF16), jax.ShapeDtypeStruct((8, d), F32)],
        compiler_params=_params("arbitrary"),
    )(dx, y, gate)


def _elementwise(fn, args, out_dtypes, *, name):
    s, d = args[0].shape
    tc = d if d <= 2048 else _tile(d, 1024, LANES)
    tr = _tile(s, max(8, (1 << 18) // tc), 8)
    n_in = len(args)

    def body(*refs):
        outs = fn(*[r[...] for r in refs[:n_in]])
        for o_ref, o in zip(refs[n_in:], outs):
            o_ref[...] = o.astype(o_ref.dtype)

    spec = pl.BlockSpec((tr, tc), lambda i, j: (i, j))
    return pl.pallas_call(
        body,
        name=name,
        grid=(s // tr, d // tc),
        in_specs=[spec] * n_in,
        out_specs=[spec] * len(out_dtypes),
        out_shape=[jax.ShapeDtypeStruct((s, d), dt) for dt in out_dtypes],
        compiler_params=_params("parallel", "parallel"),
    )(*args)


def _silu(v):
    return (v * jax.nn.sigmoid(v),)


def _split3(v):
    hi = v.astype(BF16)
    r1 = v - hi.astype(F32)
    mid = r1.astype(BF16)
    lo = (r1 - mid.astype(F32)).astype(BF16)
    return hi, mid, lo


def _band_dot(band, v):
    return sum(jnp.dot(band, part, preferred_element_type=F32) for part in _split3(v))


def _pool(h, *, name, transpose, out_dtype):
    s, d = h.shape
    c = d // len(POOL_WINDOWS)
    tr = _tile(s, 256, HALO)
    per = tr // HALO
    n_halo = s // HALO

    def body(h_ref, halo_ref, o_ref):
        i = pl.program_id(0)
        out_row = i * tr + lax.broadcasted_iota(jnp.int32, (tr, tr + HALO), 0)
        col = lax.broadcasted_iota(jnp.int32, (tr, tr + HALO), 1)
        if transpose:
            ext = jnp.concatenate([h_ref[...], halo_ref[...]], axis=0)
            src_row = i * tr + col
            ext_row = i * tr + lax.broadcasted_iota(jnp.int32, (tr + HALO, 1), 0)
        else:
            ext = jnp.concatenate([halo_ref[...], h_ref[...]], axis=0)
            src_row = i * tr + col - HALO
            own_row = i * tr + lax.broadcasted_iota(jnp.int32, (tr, 1), 0)
        for gi, w in enumerate(POOL_WINDOWS):
            cols = slice(gi * c, (gi + 1) * c)
            if transpose:
                band = (src_row >= out_row) & (src_row < out_row + w) & (src_row < s)
                scaled = ext[:, cols] / jnp.minimum(ext_row + 1, w).astype(F32)
                res = _band_dot(band.astype(BF16), scaled) - h_ref[:, cols]
            else:
                band = (src_row <= out_row) & (src_row > out_row - w) & (src_row >= 0)
                res = _band_dot(band.astype(BF16), ext[:, cols]) / jnp.minimum(own_row + 1, w).astype(F32) - h_ref[:, cols]
            o_ref[:, cols] = res.astype(o_ref.dtype)

    if transpose:
        halo_map = lambda i: (jnp.minimum((i + 1) * per, n_halo - 1), 0)
    else:
        halo_map = lambda i: (jnp.maximum(i * per - 1, 0), 0)
    return pl.pallas_call(
        body,
        name=name,
        grid=(s // tr,),
        in_specs=[pl.BlockSpec((tr, d), lambda i: (i, 0)), pl.BlockSpec((HALO, d), halo_map)],
        out_specs=pl.BlockSpec((tr, d), lambda i: (i, 0)),
        out_shape=jax.ShapeDtypeStruct((s, d), out_dtype),
        compiler_params=_params("parallel"),
    )(h, h)


def _rotate(v, cos, sin):
    lane = lax.broadcasted_iota(jnp.int32, v.shape, 1)
    swapped = jnp.where(lane % ROPE_DIM < ROPE_DIM // 2, pltpu.roll(v, LANES - ROPE_DIM // 2, 1), pltpu.roll(v, ROPE_DIM // 2, 1))
    return v * cos + swapped * sin


def _rope_heads(x, cos, sin, *, name, out_dtype):
    s, n = x.shape
    tr = _tile(s, 512, 8)

    def body(x_ref, cos_ref, sin_ref, o_ref):
        j = pl.program_id(1)

        @pl.when(j % 2 == 0)
        def _():
            o_ref[...] = x_ref[...].astype(o_ref.dtype)

        @pl.when(j % 2 == 1)
        def _():
            o_ref[...] = _rotate(x_ref[...].astype(F32), cos_ref[...], sin_ref[...]).astype(o_ref.dtype)

    blk = pl.BlockSpec((tr, LANES), lambda i, j: (i, j))
    tab = pl.BlockSpec((tr, LANES), lambda i, j: (i, 0))
    return pl.pallas_call(
        body,
        name=name,
        grid=(s // tr, n // LANES),
        in_specs=[blk, tab, tab],
        out_specs=blk,
        out_shape=jax.ShapeDtypeStruct((s, n), out_dtype),
        compiler_params=_params("parallel", "parallel"),
    )(x, cos, sin)


def _build_keys(kv, kr_pre, cos, sin, *, name):
    s, n = kv.shape
    tr = _tile(s, 512, 8)

    def body(kv_ref, kr_ref, cos_ref, sin_ref, o_ref):
        j = pl.program_id(1)

        @pl.when(j % 2 == 0)
        def _():
            o_ref[...] = kv_ref[...]

        @pl.when(j % 2 == 1)
        def _():
            o_ref[...] = _rotate(kr_ref[...], cos_ref[...], sin_ref[...]).astype(o_ref.dtype)

    tab = pl.BlockSpec((tr, LANES), lambda i, j: (i, 0))
    return pl.pallas_call(
        body,
        name=name,
        grid=(s // tr, n // LANES),
        in_specs=[pl.BlockSpec((tr, LANES), lambda i, j: (i, (j // 2) * 2)), tab, tab, tab],
        out_specs=pl.BlockSpec((tr, LANES), lambda i, j: (i, j)),
        out_shape=jax.ShapeDtypeStruct((s, n), BF16),
        compiler_params=_params("parallel", "parallel"),
    )(kv, kr_pre, cos, sin)


def _keys_bwd(dk_a, dk_b, dv_a, dv_b, cos, sin_neg, *, name):
    s, n = dk_a.shape
    heads = n // HEAD_PAD
    tr = _tile(s, 512, 8)

    def body(dka_ref, dkb_ref, dva_ref, dvb_ref, cos_ref, sin_ref, dkv_ref, dkr_ref):
        hd = pl.program_id(1)
        dk = dka_ref[...] + dkb_ref[...]
        dkv_ref[:, :NOPE_DIM] = dk[:, :NOPE_DIM].astype(dkv_ref.dtype)
        dkv_ref[:, NOPE_DIM:] = (dva_ref[...] + dvb_ref[...]).astype(dkv_ref.dtype)

        @pl.when(hd == 0)
        def _():
            dkr_ref[...] = dk[:, NOPE_DIM:]

        @pl.when(hd > 0)
        def _():
            dkr_ref[...] += dk[:, NOPE_DIM:]

        @pl.when(hd == heads - 1)
        def _():
            dkr_ref[...] = _rotate(dkr_ref[...], cos_ref[...], sin_ref[...])

    dk_blk = pl.BlockSpec((tr, HEAD_PAD), lambda i, hd: (i, hd))
    dv_blk = pl.BlockSpec((tr, V_DIM), lambda i, hd: (i, hd))
    tab = pl.BlockSpec((tr, LANES), lambda i, hd: (i, 0))
    return pl.pallas_call(
        body,
        name=name,
        grid=(s // tr, heads),
        in_specs=[dk_blk, dk_blk, dv_blk, dv_blk, tab, tab],
        out_specs=[dk_blk, tab],
        out_shape=[jax.ShapeDtypeStruct((s, n), BF16), jax.ShapeDtypeStruct((s, LANES), F32)],
        compiler_params=_params("parallel", "arbitrary"),
    )(dk_a, dk_b, dv_a, dv_b, cos, sin_neg)


def _attn_tile(s):
    return _tile(s, 512, LANES)


def _causal_mask(t, transposed=False):
    rows = lax.broadcasted_iota(jnp.int32, (t, t), 0)
    cols = lax.broadcasted_iota(jnp.int32, (t, t), 1)
    return rows <= cols if transposed else cols <= rows


def _attn_fwd(q, keys, kv, *, name):
    s = q.shape[0]
    heads = q.shape[1] // HEAD_PAD
    t = _attn_tile(s)

    def body(q_ref, k_ref, v_ref, o_ref, lse_ref, m_ref, l_ref, acc_ref):
        qi = pl.program_id(1)
        qv = q_ref[...]
        m_ref[...] = jnp.full_like(m_ref, -jnp.inf)
        l_ref[...] = jnp.zeros_like(l_ref)
        acc_ref[...] = jnp.zeros_like(acc_ref)

        def block(kb, diagonal):
            rows = pl.ds(pl.multiple_of(kb * t, t), t)
            sc = lax.dot_general(qv, k_ref[rows, :], _NT, preferred_element_type=F32) * SM_SCALE
            if diagonal:
                sc = jnp.where(_causal_mask(t), sc, -jnp.inf)
            m_new = jnp.maximum(m_ref[...], jnp.max(sc, axis=1, keepdims=True))
            alpha = jnp.exp(m_ref[...] - m_new)
            p = jnp.exp(sc - m_new)
            l_ref[...] = alpha * l_ref[...] + jnp.sum(p, axis=1, keepdims=True)
            acc_ref[...] = alpha * acc_ref[...] + jnp.dot(p.astype(BF16), v_ref[rows, :], preferred_element_type=F32)
            m_ref[...] = m_new

        def earlier(kb, carry):
            block(kb, False)
            return carry

        lax.fori_loop(0, qi, earlier, 0)
        block(qi, True)
        o_ref[...] = (acc_ref[...] / l_ref[...]).astype(o_ref.dtype)
        lse_ref[0] = m_ref[...] + jnp.log(l_ref[...])

    return pl.pallas_call(
        body,
        name=name,
        grid=(heads, s // t),
        in_specs=[
            pl.BlockSpec((t, HEAD_PAD), lambda hd, qi: (qi, hd)),
            pl.BlockSpec((s, HEAD_PAD), lambda hd, qi: (0, hd)),
            pl.BlockSpec((s, V_DIM), lambda hd, qi: (0, 2 * hd + 1)),
        ],
        out_specs=[pl.BlockSpec((t, V_DIM), lambda hd, qi: (qi, hd)), pl.BlockSpec((1, t, 1), lambda hd, qi: (hd, qi, 0))],
        out_shape=[jax.ShapeDtypeStruct((s, heads * V_DIM), BF16), jax.ShapeDtypeStruct((heads, s, 1), F32)],
        scratch_shapes=[pltpu.VMEM((t, 1), F32), pltpu.VMEM((t, 1), F32), pltpu.VMEM((t, V_DIM), F32)],
        compiler_params=_params("parallel", "parallel"),
    )(q, keys, kv)


def _attn_bwd_q(q, keys, kv, o, do, lse, *, name):
    s = q.shape[0]
    heads = q.shape[1] // HEAD_PAD
    t = _attn_tile(s)

    def body(q_ref, k_ref, v_ref, o_ref, do_ref, lse_ref, dq_ref, delta_ref, acc_ref):
        qi = pl.program_id(1)
        qv, dov, lse = q_ref[...], do_ref[...], lse_ref[0]
        delta = jnp.sum(dov.astype(F32) * o_ref[...].astype(F32), axis=1, keepdims=True)
        delta_ref[0] = delta
        acc_ref[...] = jnp.zeros_like(acc_ref)

        def block(kb, diagonal):
            rows = pl.ds(pl.multiple_of(kb * t, t), t)
            kv_ = k_ref[rows, :]
            sc = lax.dot_general(qv, kv_, _NT, preferred_element_type=F32) * SM_SCALE
            p = jnp.exp(sc - lse)
            if diagonal:
                p = jnp.where(_causal_mask(t), p, 0.0)
            dp = lax.dot_general(dov, v_ref[rows, :], _NT, preferred_element_type=F32)
            ds = p * (dp - delta) * SM_SCALE
            acc_ref[...] += jnp.dot(ds.astype(BF16), kv_, preferred_element_type=F32)

        def earlier(kb, carry):
            block(kb, False)
            return carry

        lax.fori_loop(0, qi, earlier, 0)
        block(qi, True)
        dq_ref[...] = acc_ref[...]

    o_blk = pl.BlockSpec((t, V_DIM), lambda hd, qi: (qi, hd))
    col = pl.BlockSpec((1, t, 1), lambda hd, qi: (hd, qi, 0))
    return pl.pallas_call(
        body,
        name=name,
        grid=(heads, s // t),
        in_specs=[
            pl.BlockSpec((t, HEAD_PAD), lambda hd, qi: (qi, hd)),
            pl.BlockSpec((s, HEAD_PAD), lambda hd, qi: (0, hd)),
            pl.BlockSpec((s, V_DIM), lambda hd, qi: (0, 2 * hd + 1)),
            o_blk,
            o_blk,
            col,
        ],
        out_specs=[pl.BlockSpec((t, HEAD_PAD), lambda hd, qi: (qi, hd)), col],
        out_shape=[jax.ShapeDtypeStruct((s, heads * HEAD_PAD), F32), jax.ShapeDtypeStruct((heads, s, 1), F32)],
        scratch_shapes=[pltpu.VMEM((t, HEAD_PAD), F32)],
        compiler_params=_params("parallel", "parallel"),
    )(q, keys, kv, o, do, lse)


def _attn_bwd_kv(q, keys, kv, do, lse_row, delta_row, *, name):
    s = q.shape[0]
    heads = q.shape[1] // HEAD_PAD
    t = _attn_tile(s)
    nq = s // t

    def body(q_ref, k_ref, v_ref, do_ref, lse_ref, delta_ref, dk_ref, dv_ref, dk_acc, dv_acc):
        ki = pl.program_id(1)
        kv_, vv = k_ref[...], v_ref[...]
        dk_acc[...] = jnp.zeros_like(dk_acc)
        dv_acc[...] = jnp.zeros_like(dv_acc)

        def block(qb, diagonal):
            rows = pl.ds(pl.multiple_of(qb * t, t), t)
            qv, dov = q_ref[rows, :], do_ref[rows, :]
            sc_t = lax.dot_general(kv_, qv, _NT, preferred_element_type=F32) * SM_SCALE
            p_t = jnp.exp(sc_t - lse_ref[0, qb])
            if diagonal:
                p_t = jnp.where(_causal_mask(t, transposed=True), p_t, 0.0)
            dv_acc[...] += jnp.dot(p_t.astype(BF16), dov, preferred_element_type=F32)
            dp_t = lax.dot_general(vv, dov, _NT, preferred_element_type=F32)
            ds_t = p_t * (dp_t - delta_ref[0, qb]) * SM_SCALE
            dk_acc[...] += jnp.dot(ds_t.astype(BF16), qv, preferred_element_type=F32)

        def later(qb, carry):
            block(qb, False)
            return carry

        block(ki, True)
        lax.fori_loop(ki + 1, nq, later, 0)
        dk_ref[...] = dk_acc[...]
        dv_ref[...] = dv_acc[...]

    row = pl.BlockSpec((1, nq, 1, t), lambda hd, ki: (hd, 0, 0, 0))
    return pl.pallas_call(
        body,
        name=name,
        grid=(heads, nq),
        in_specs=[
            pl.BlockSpec((s, HEAD_PAD), lambda hd, ki: (0, hd)),
            pl.BlockSpec((t, HEAD_PAD), lambda hd, ki: (ki, hd)),
            pl.BlockSpec((t, V_DIM), lambda hd, ki: (ki, 2 * hd + 1)),
            pl.BlockSpec((s, V_DIM), lambda hd, ki: (0, hd)),
            row,
            row,
        ],
        out_specs=[pl.BlockSpec((t, HEAD_PAD), lambda hd, ki: (ki, hd)), pl.BlockSpec((t, V_DIM), lambda hd, ki: (ki, hd))],
        out_shape=[jax.ShapeDtypeStruct((s, heads * HEAD_PAD), F32), jax.ShapeDtypeStruct((s, heads * V_DIM), F32)],
        scratch_shapes=[pltpu.VMEM((t, HEAD_PAD), F32), pltpu.VMEM((t, V_DIM), F32)],
        compiler_params=_params("parallel", "parallel"),
    )(q, keys, kv, do, lse_row, delta_row)


def _loss_bwd(x, y, gate, g, target, *, name):
    s, d = x.shape
    tr = _row_tile(s, d)

    def body(x_ref, y_ref, gate_ref, g_ref, t_ref, dx_ref, stats_ref, loss_ref):
        xv = x_ref[...] + gate_ref[...] * y_ref[...]
        r = lax.rsqrt(jnp.mean(xv * xv, axis=-1, keepdims=True) + NORM_EPS)
        xhat = xv * r
        gv = g_ref[...]
        err = xhat * gv - t_ref[...]
        dy = err / d
        dxhat = dy * gv
        dx_ref[...] = r * (dxhat - xhat * jnp.mean(dxhat * xhat, axis=-1, keepdims=True))

        @pl.when(pl.program_id(0) == 0)
        def _():
            stats_ref[...] = jnp.zeros_like(stats_ref)
            loss_ref[...] = jnp.zeros_like(loss_ref)

        stats_ref[0:1, :] += jnp.sum(dy * xhat, axis=0, keepdims=True)
        loss_ref[...] += 0.5 * jnp.sum(jnp.mean(err * err, axis=-1, keepdims=True))

    row = pl.BlockSpec((tr, d), lambda i: (i, 0))
    vec = pl.BlockSpec((1, d), lambda i: (0, 0))
    return pl.pallas_call(
        body,
        name=name,
        grid=(s // tr,),
        in_specs=[row, row, vec, vec, row],
        out_specs=[row, pl.BlockSpec((8, d), lambda i: (0, 0)), pl.BlockSpec((8, LANES), lambda i: (0, 0))],
        out_shape=[jax.ShapeDtypeStruct((s, d), F32), jax.ShapeDtypeStruct((8, d), F32), jax.ShapeDtypeStruct((8, LANES), F32)],
        compiler_params=_params("arbitrary"),
    )(x, y, gate, g, target)


def _adam_math(w, g, m, v):
    new_m = ADAM_B1 * m + (1.0 - ADAM_B1) * g
    new_v = ADAM_B2 * v + (1.0 - ADAM_B2) * (g * g)
    m_hat = new_m / (1.0 - ADAM_B1**ADAM_STEP)
    v_hat = new_v / (1.0 - ADAM_B2**ADAM_STEP)
    return -ADAM_LR * (m_hat / (jnp.sqrt(v_hat) + ADAM_EPS) + ADAM_WD * w), new_m, new_v


def _adamw(w, g, m, v, *, name):
    rows, cols = w.shape
    tr = _tile(rows, max(8, (1 << 18) // cols), 8)

    def body(w_ref, g_ref, m_ref, v_ref, go_ref, d_ref, mo_ref, vo_ref):
        gv = g_ref[...]
        go_ref[...] = gv
        d_ref[...], mo_ref[...], vo_ref[...] = _adam_math(w_ref[...], gv, m_ref[...], v_ref[...])

    spec = pl.BlockSpec((tr, cols), lambda i: (i, 0))
    return pl.pallas_call(
        body,
        name=name,
        grid=(rows // tr,),
        in_specs=[spec] * 4,
        out_specs=[spec] * 4,
        out_shape=[jax.ShapeDtypeStruct((rows, cols), F32)] * 4,
        compiler_params=_params("parallel"),
    )(w, g, m, v)


def _tp_adamw(sc16, dm, w, m, v, *, name):
    nl, d, n = w.shape
    tm = _tile(d, 512, LANES)
    tn = _tile(n, 1024, LANES)

    def body(sc_ref, dm_ref, w_ref, m_ref, v_ref, go_ref, d_ref, mo_ref, vo_ref):
        gv = lax.dot_general(sc_ref[...].astype(BF16), dm_ref[0].astype(BF16), (((0,), (0,)), ((), ())), preferred_element_type=F32)
        go_ref[0] = gv
        d_ref[0], mo_ref[0], vo_ref[0] = _adam_math(w_ref[0], gv, m_ref[0], v_ref[0])

    blk = pl.BlockSpec((1, tm, tn), lambda l, i, j: (l, i, j))
    return pl.pallas_call(
        body,
        name=name,
        grid=(nl, d // tm, n // tn),
        in_specs=[pl.BlockSpec((16, tm), lambda l, i, j: (0, i)), pl.BlockSpec((1, 16, tn), lambda l, i, j: (l, 0, j)), blk, blk, blk],
        out_specs=[blk] * 4,
        out_shape=[jax.ShapeDtypeStruct((nl, d, n), F32)] * 4,
        compiler_params=_params("parallel", "parallel", "parallel"),
    )(sc16, dm, w, m, v)


def _sum_devices(x, *, name):
    def body(x_ref, o_ref):
        acc = x_ref[0]
        for k in range(1, N_DEV):
            acc = acc + x_ref[k]
        o_ref[...] = acc

    return pl.pallas_call(body, name=name, out_shape=jax.ShapeDtypeStruct(x.shape[1:], F32))(x)


def _place():
    mx, my, mc = lax.axis_index("x"), lax.axis_index("y"), lax.axis_index("c")
    chips = [(1 - mx, my), (mx, 1 - my), (1 - mx, 1 - my)]
    return mx, my, mc, chips


def _remote(src, dst, send_sem, recv_sem, device):
    return pltpu.make_async_remote_copy(src_ref=src, dst_ref=dst, send_sem=send_sem, recv_sem=recv_sem, device_id=device, device_id_type=MESH)


def _allgather8(x, *, name):
    def body(x_ref, out_ref, send_sems, recv_sems, local_sem):
        mx, my, mc, chips = _place()
        me, sibling = (mx, my, mc), (mx, my, 1 - mc)

        def slot(px, py, pc):
            return out_ref.at[4 * px + 2 * py + pc]

        def copy(k, block, to, src=None):
            return _remote(slot(*block) if src is None else src, slot(*block), send_sems.at[k], recv_sems.at[k], to)

        mine = pltpu.make_async_copy(x_ref, slot(*me), local_sem)
        mine.start()
        first = [copy(0, me, sibling, src=x_ref)] + [copy(1 + j, me, (*chip, mc), src=x_ref) for j, chip in enumerate(chips)]
        for cp in first:
            cp.start()
        passed = [copy(4 + j, (*chip, mc), sibling) for j, chip in enumerate(chips)]
        for j, chip in enumerate(chips):
            copy(1 + j, (*chip, mc), me).wait_recv()
            passed[j].start()
        copy(0, sibling, me).wait_recv()
        for j, chip in enumerate(chips):
            copy(4 + j, (*chip, 1 - mc), me).wait_recv()
        for cp in first + passed:
            cp.wait_send()
        mine.wait()

    return pl.pallas_call(
        body,
        name=name,
        out_shape=jax.ShapeDtypeStruct((N_DEV,) + x.shape, x.dtype),
        in_specs=[pl.BlockSpec(memory_space=pltpu.VMEM)],
        out_specs=pl.BlockSpec(memory_space=pltpu.VMEM),
        scratch_shapes=[pltpu.SemaphoreType.DMA((7,)), pltpu.SemaphoreType.DMA((7,)), pltpu.SemaphoreType.DMA],
    )(x)


class _Geom:
    def __init__(self, shape3, axis):
        self.shape3, self.axis = shape3, axis
        nl, r, c = shape3
        self.rs, self.cs = (r // N_CHIPS, c) if axis == 1 else (r, c // N_CHIPS)
        self.hl, self.hr = (nl // 2, self.rs) if nl > 1 else (1, self.rs // 2)
        self.shard = (nl, self.rs, self.cs)
        self.half = (self.hl, self.hr, self.cs)

    def in_full(self, ref, chip, core):
        nl = self.shape3[0]
        l0 = core * self.hl if nl > 1 else 0
        r0 = (chip * self.rs if self.axis == 1 else 0) + (0 if nl > 1 else core * self.hr)
        c0 = chip * self.cs if self.axis == 2 else 0
        return ref.at[pl.ds(l0, self.hl), pl.ds(r0, self.hr), pl.ds(c0, self.cs)]


def _place_shard(shard, geom, chip_arr, *, name):
    nl, rs, cs = geom.shard
    tr = _tile(rs, max(16, (1 << 18) // cs), 16)
    per = rs // tr

    def body(chip_ref, x_ref, o_ref):
        o_ref[...] = x_ref[...].astype(o_ref.dtype)

    def out_map(l, i, chip_ref):
        return (l, chip_ref[0] * per + i, 0) if geom.axis == 1 else (l, i, chip_ref[0])

    return pl.pallas_call(
        body,
        name=name,
        grid_spec=pltpu.PrefetchScalarGridSpec(
            num_scalar_prefetch=1,
            grid=(nl, per),
            in_specs=[pl.BlockSpec((1, tr, cs), lambda l, i, chip_ref: (l, i, 0))],
            out_specs=pl.BlockSpec((1, tr, cs), out_map),
        ),
        out_shape=jax.ShapeDtypeStruct(geom.shape3, BF16),
        compiler_params=_params("parallel", "parallel"),
    )(chip_arr, shard)


def _gather_weights(fulls, geoms, *, name):
    n = len(fulls)

    def body(*refs):
        given_refs, full_refs = refs[:n], refs[n : 2 * n]
        ici_send, ici_recv, d2d_send, d2d_recv = refs[2 * n :]
        mx, my, mc, chips = _place()
        me = 2 * mx + my
        sent, passed = [], []
        for w, geom in enumerate(geoms):
            for k, chip in enumerate(chips):
                cp = _remote(geom.in_full(given_refs[w], me, mc), geom.in_full(full_refs[w], me, mc), ici_send.at[3 * w + k], ici_recv.at[3 * w + k], (*chip, mc))
                cp.start()
                sent.append(cp)
        for w, geom in enumerate(geoms):
            for k, (px, py) in enumerate(chips):
                landed = geom.in_full(full_refs[w], 2 * px + py, mc)
                _remote(landed, landed, ici_send.at[3 * w + k], ici_recv.at[3 * w + k], (px, py, mc)).wait_recv()
                cp = _remote(landed, landed, d2d_send.at[3 * w + k], d2d_recv.at[3 * w + k], (mx, my, 1 - mc))
                cp.start()
                passed.append(cp)
        for w, geom in enumerate(geoms):
            for k, (px, py) in enumerate(chips):
                other = geom.in_full(full_refs[w], 2 * px + py, 1 - mc)
                _remote(other, other, d2d_send.at[3 * w + k], d2d_recv.at[3 * w + k], (mx, my, 1 - mc)).wait_recv()
        for cp in sent + passed:
            cp.wait_send()

    return pl.pallas_call(
        body,
        name=name,
        in_specs=[ANY] * n,
        out_specs=[ANY] * n,
        out_shape=[jax.ShapeDtypeStruct(f.shape, f.dtype) for f in fulls],
        input_output_aliases={w: w for w in range(n)},
        scratch_shapes=[pltpu.SemaphoreType.DMA((3 * n,))] * 4,
    )(*fulls)


def _reduce_pair(grads, geoms, *, name):
    n = len(grads)

    def body(*refs):
        grad_refs, theirs_refs = refs[:n], refs[n : 2 * n]
        send_sems, recv_sems = refs[2 * n :]
        mx, my, mc, _ = _place()
        sibling = (mx, my, 1 - mc)
        remote = []
        for w, geom in enumerate(geoms):
            for chip in range(N_CHIPS):
                cp = _remote(geom.in_full(grad_refs[w], chip, 1 - mc), theirs_refs[w].at[chip], send_sems.at[4 * w + chip], recv_sems.at[4 * w + chip], sibling)
                cp.start()
                remote.append(cp)
        for cp in remote:
            cp.wait_recv()
        for cp in remote:
            cp.wait_send()

    return pl.pallas_call(
        body,
        name=name,
        in_specs=[ANY] * n,
        out_specs=[ANY] * n,
        out_shape=[jax.ShapeDtypeStruct((N_CHIPS,) + g.half, x.dtype) for g, x in zip(geoms, grads)],
        scratch_shapes=[pltpu.SemaphoreType.DMA((4 * n,))] * 2,
    )(*grads)


def _half_tile(geom):
    return _tile(geom.hr, max(16, (1 << 18) // geom.cs), 16)


def _pair_add(grad, theirs, geom, core_arr, *, name):
    hl, hr, cs = geom.half
    tr = _half_tile(geom)
    stacked = geom.shape3[0] > 1

    def grad_map(chip, l, i, core_ref):
        layer = core_ref[0] * hl + l if stacked else 0
        row = (chip * (geom.rs // tr) if geom.axis == 1 else 0) + (0 if stacked else core_ref[0] * (hr // tr)) + i
        return layer, row, (chip if geom.axis == 2 else 0)

    def body(core_ref, g_ref, t_ref, o_ref):
        o_ref[0] = (g_ref[...].astype(F32) + t_ref[0].astype(F32)).astype(o_ref.dtype)

    blk = pl.BlockSpec((1, 1, tr, cs), lambda chip, l, i, core_ref: (chip, l, i, 0))
    return pl.pallas_call(
        body,
        name=name,
        grid_spec=pltpu.PrefetchScalarGridSpec(
            num_scalar_prefetch=1, grid=(N_CHIPS, hl, hr // tr), in_specs=[pl.BlockSpec((1, tr, cs), grad_map), blk], out_specs=blk
        ),
        out_shape=jax.ShapeDtypeStruct(theirs.shape, BF16),
        compiler_params=_params("parallel", "parallel", "parallel"),
    )(core_arr, grad, theirs)


def _reduce_chips(parts, *, name):
    n = len(parts)

    def body(*refs):
        part_refs, slot_refs = refs[:n], refs[n : 2 * n]
        send_sems, recv_sems = refs[2 * n :]
        mx, my, mc, chips = _place()
        remote = []
        for w in range(n):
            for k, (px, py) in enumerate(chips):
                cp = _remote(part_refs[w].at[2 * px + py], slot_refs[w].at[k], send_sems.at[3 * w + k], recv_sems.at[3 * w + k], (px, py, mc))
                cp.start()
                remote.append(cp)
        for cp in remote:
            cp.wait_recv()
        for cp in remote:
            cp.wait_send()

    return pl.pallas_call(
        body,
        name=name,
        in_specs=[ANY] * n,
        out_specs=[ANY] * n,
        out_shape=[jax.ShapeDtypeStruct((N_CHIPS - 1,) + p.shape[1:], p.dtype) for p in parts],
        scratch_shapes=[pltpu.SemaphoreType.DMA((3 * n,))] * 2,
    )(*parts)


def _chip_sum(part, slots, geom, place_arr, *, name):
    hl, hr, cs = geom.half
    tr = _half_tile(geom)

    def body(place_ref, own_ref, s0_ref, s1_ref, s2_ref, o_ref):
        o_ref[...] = ((own_ref[...].astype(F32) + s0_ref[...].astype(F32)) + s1_ref[...].astype(F32)) + s2_ref[...].astype(F32)

    def slot(k):
        return pl.BlockSpec((1, 1, tr, cs), lambda l, i, place_ref: (k, l, i, 0))

    return pl.pallas_call(
        body,
        name=name,
        grid_spec=pltpu.PrefetchScalarGridSpec(
            num_scalar_prefetch=1,
            grid=(hl, hr // tr),
            in_specs=[pl.BlockSpec((1, 1, tr, cs), lambda l, i, place_ref: (place_ref[0], l, i, 0)), slot(0), slot(1), slot(2)],
            out_specs=pl.BlockSpec((1, 1, tr, cs), lambda l, i, place_ref: (place_ref[1], l, i, 0)),
        ),
        out_shape=jax.ShapeDtypeStruct((2,) + geom.half, F32),
        compiler_params=_params("parallel", "parallel"),
    )(place_arr, part, slots, slots, slots)


def _join_pair(boths, *, name):
    n = len(boths)

    def body(*refs):
        given_refs, both_refs = refs[:n], refs[n : 2 * n]
        send_sems, recv_sems = refs[2 * n :]
        mx, my, mc, _ = _place()
        remote = []
        for w in range(n):
            cp = _remote(given_refs[w].at[mc], both_refs[w].at[mc], send_sems.at[w], recv_sems.at[w], (mx, my, 1 - mc))
            cp.start()
            remote.append(cp)
        for w in range(n):
            got = both_refs[w].at[1 - mc]
            _remote(got, got, send_sems.at[w], recv_sems.at[w], (mx, my, 1 - mc)).wait_recv()
        for cp in remote:
            cp.wait_send()

    return pl.pallas_call(
        body,
        name=name,
        in_specs=[ANY] * n,
        out_specs=[ANY] * n,
        out_shape=[jax.ShapeDtypeStruct(b.shape, b.dtype) for b in boths],
        input_output_aliases={w: w for w in range(n)},
        scratch_shapes=[pltpu.SemaphoreType.DMA((n,))] * 2,
    )(*boths)


WEIGHTS = ("mod_w", "mod_b", "norm_mix", "norm_ffn", "pool_w", "pool_scale", "kv_mod_w", "kv_mod_b", "kv_in_norm", "w_dkv", "kv_norm",
           "w_uk", "w_uv", "w_kr", "w_dq", "q_norm", "w_uq", "w_o", "ffn_gate", "ffn_up", "ffn_down", "final_norm")
SMALL = ("mod_b", "kv_mod_b", "norm_mix", "norm_ffn", "kv_in_norm", "kv_norm", "q_norm", "final_norm")


def _rows(v):
    return v.reshape(-1, LANES)


def _pad_rows(a):
    return jnp.pad(a, ((0, (-a.shape[0]) % 8), (0, 0)))


def _vec(v):
    return v.reshape(1, -1)


def _step(x, c, positions, target, wts, mom, var):
    _, s, d = x.shape
    depth, n_a, n_b = wts["mod_w"].shape[0], wts["pool_w"].shape[0], wts["w_dq"].shape[0]
    assert n_b == 2 and n_a + n_b == depth
    heads = d // V_DIM
    kvr, qr = wts["w_dkv"].shape[1], wts["w_dq"].shape[2]
    ffn = wts["ffn_gate"].shape[2] * N_CHIPS
    pool_c = d // len(POOL_WINDOWS)
    nmod, nkv = N_MOD * d, 2 * d
    mx, my, mc = lax.axis_index("x"), lax.axis_index("y"), lax.axis_index("c")
    chip, dev = 2 * mx + my, 4 * mx + 2 * my + mc
    xs, tgt = x[0], target[0]

    inv_freq = 1.0 / (ROPE_THETA ** (jnp.arange(0, ROPE_DIM, 2, dtype=F32) / ROPE_DIM))
    ang = positions[0].astype(F32)[:, None] * inv_freq
    cos, sin, zero = jnp.cos(ang), jnp.sin(ang), jnp.zeros((s, LANES - ROPE_DIM), F32)
    cos_t = jnp.concatenate([cos, cos, zero], axis=1)
    sin_fwd = jnp.concatenate([-sin, sin, zero], axis=1)
    sin_bwd = jnp.concatenate([sin, -sin, zero], axis=1)

    c_rows, ps_rows = d // LANES, n_a * (d // N_CHIPS) // LANES
    cond = _allgather8(_pad_rows(jnp.concatenate([_rows(c), _rows(wts["pool_scale"])])), name="gather_cond")
    c_all = cond[:, :c_rows].reshape(N_DEV, d)
    pool_scale = cond[0::2, c_rows : c_rows + ps_rows].reshape(N_CHIPS, n_a, d // N_CHIPS).transpose(1, 0, 2).reshape(n_a, d)
    sc16 = _elementwise(_silu, [jnp.pad(c_all, ((0, 16 - N_DEV), (0, 0)))], [F32], name="silu_cond")[0]

    mod_bias = lax.dynamic_slice_in_dim(wts["mod_b"], chip * (nmod // N_CHIPS), nmod // N_CHIPS, axis=1)[:, None, :]
    kv_bias = lax.dynamic_slice_in_dim(wts["kv_mod_b"], chip * (nkv // N_CHIPS), nkv // N_CHIPS).reshape(1, 1, -1)
    mod_part = _tp_fwd(sc16, wts["mod_w"], mod_bias, name="mod_fwd")
    kv_part = _tp_fwd(sc16, wts["kv_mod_w"][None], kv_bias, name="kv_mod_fwd")
    part = jnp.concatenate([mod_part[i, :N_DEV] for i in range(depth)] + [kv_part[0, :N_DEV]], axis=1)
    ncol = part.shape[1]
    gathered = _allgather8(_pad_rows(_rows(part)), name="gather_mods")
    gathered = gathered[0::2, : N_DEV * ncol // LANES].reshape(N_CHIPS, N_DEV, ncol)
    mine = lax.dynamic_index_in_dim(gathered, dev, axis=1, keepdims=False)
    per = nmod // N_CHIPS
    mods = [mine[:, i * per : (i + 1) * per].reshape(N_MOD, 1, d) for i in range(depth)]
    kv_shift, kv_scale = mine[:, depth * per :].reshape(2, 1, d)

    gathered_names = ("pool_w", "w_dkv", "w_uk", "w_uv", "w_kr", "w_dq", "w_uq", "w_o", "ffn_gate", "ffn_up", "ffn_down")
    geoms = {
        "pool_w": _Geom((n_a * len(POOL_WINDOWS), pool_c, pool_c), 1),
        "w_dkv": _Geom((1, d, kvr), 1),
        "w_uk": _Geom((1, kvr, heads * NOPE_DIM), 2),
        "w_uv": _Geom((1, kvr, heads * V_DIM), 2),
        "w_kr": _Geom((1, d, ROPE_DIM), 1),
        "w_dq": _Geom((n_b, d, qr), 1),
        "w_uq": _Geom((n_b, qr, heads * (NOPE_DIM + ROPE_DIM)), 2),
        "w_o": _Geom((n_b, d, d), 1),
        "ffn_gate": _Geom((depth, d, ffn), 2),
        "ffn_up": _Geom((depth, d, ffn), 2),
        "ffn_down": _Geom((depth, ffn, d), 1),
    }
    geom_list = [geoms[n] for n in gathered_names]
    chip_arr, core_arr, place_arr = chip.reshape(1), mc.reshape(1), jnp.stack([chip, mc])
    placed = [_place_shard(wts[n].reshape(geoms[n].shard), geoms[n], chip_arr, name="place_" + n) for n in gathered_names]
    full = dict(zip(gathered_names, _gather_weights(placed, geom_list, name="gather_weights")))

    pool_w = full["pool_w"].reshape(n_a, len(POOL_WINDOWS), pool_c, pool_c)
    w_uq = full["w_uq"].reshape(n_b, qr, heads, NOPE_DIM + ROPE_DIM)
    w_q = jnp.pad(w_uq, ((0, 0), (0, 0), (0, 0), (0, HEAD_PAD - NOPE_DIM - ROPE_DIM))).reshape(n_b, qr, heads * HEAD_PAD)
    w_ukv = jnp.stack([full["w_uk"].reshape(kvr, heads, NOPE_DIM), full["w_uv"].reshape(kvr, heads, V_DIM)], axis=2).reshape(kvr, heads * HEAD_PAD)
    w_dkvkr = jnp.concatenate([full["w_dkv"][0], full["w_kr"][0], jnp.zeros((d, LANES - ROPE_DIM), BF16)], axis=1)
    w_dq, w_o = full["w_dq"], full["w_o"]

    norm_mix, norm_ffn = wts["norm_mix"], wts["norm_ffn"]
    saved = []
    cur, pending = xs, None
    kv_side = None
    for i in range(depth):
        shift_m, scale_m, gate_m, shift_f, scale_f, gate_f = mods[i]
        h1_dtype = F32 if i < n_a else BF16
        if pending is None:
            x0 = cur
            h1 = _norm_fwd(x0, _vec(norm_mix[i]), scale=scale_m, shift=shift_m, out_dtype=h1_dtype, name="norm_mix_first")
        else:
            x0, h1 = _norm_fwd(cur, _vec(norm_mix[i]), scale=scale_m, shift=shift_m, y=pending[0], gate=pending[1], out_dtype=h1_dtype, name="norm_mix")
        lay = {"x0": x0, "h1": h1}
        if i == n_a:
            h_kv = _norm_fwd(x0, _vec(wts["kv_in_norm"]), scale=kv_scale, shift=kv_shift, name="norm_kv_in")
            pre = _mm(h_kv, w_dkvkr, name="kv_down", tn=kvr + LANES)
            ckv_pre, kr_pre = pre[:, :kvr], pre[:, kvr:]
            ckv = _norm_fwd(ckv_pre, _vec(wts["kv_norm"]), name="norm_kv")
            kv = _mm(ckv, w_ukv, out_dtype=BF16, name="kv_up")
            keys = _build_keys(kv, kr_pre, cos_t, sin_fwd, name="build_keys")
            kv_side = {"h_kv": h_kv, "ckv_pre": ckv_pre, "ckv": ckv, "kv": kv, "keys": keys, "x0": x0}
        if i < n_a:
            pooled = _pool(h1, transpose=False, out_dtype=BF16, name="pool_fwd")
            y_pre = _gmm(pooled, pool_w[i], mode="nn", out_dtype=F32, name="pool_mix")
            gate_eff = gate_m * _vec(pool_scale[i])
            lay.update(pooled=pooled)
        else:
            l = i - n_a
            cq_pre = _mm(h1, w_dq[l], name="q_down")
            cq = _norm_fwd(cq_pre, _vec(wts["q_norm"][l]), name="norm_q")
            q = _rope_heads(_mm(cq, w_q[l], name="q_up"), cos_t, sin_fwd, out_dtype=BF16, name="rope_q")
            o, lse = _attn_fwd(q, kv_side["keys"], kv_side["kv"], name="attn_fwd")
            y_pre = _mm(o, w_o[l], name="attn_out")
            gate_eff = gate_m
            lay.update(cq_pre=cq_pre, cq=cq, q=q, o=o, lse=lse)
        x1, h2 = _norm_fwd(x0, _vec(norm_ffn[i]), scale=scale_f, shift=shift_f, y=y_pre, gate=gate_eff, name="norm_ffn")
        a, b, z = _ffn_in(h2, full["ffn_gate"], full["ffn_up"], i, name="ffn_in")
        f = _mm(z, full["ffn_down"], b_idx=i, name="ffn_down")
        lay.update(y_pre=y_pre, gate_eff=gate_eff, x1=x1, h2=h2, a=a, b=b, z=z, f=f)
        saved.append(lay)
        cur, pending = x1, (f, gate_f)

    dx, final_stats, loss_tile = _loss_bwd(cur, pending[0], pending[1], _vec(wts["final_norm"]), tgt, name="loss")
    loss = lax.psum(loss_tile[0, 0], ("x", "y", "c"))

    grad_full = {}
    g_gate = g_up = g_down = None
    g_wo, g_wq, g_wdq = [None] * n_b, [None] * n_b, [None] * n_b
    g_pool = [None] * n_a
    dmods = [None] * depth
    g_norm_mix, g_norm_ffn, g_q_norm, g_pool_scale = [None] * depth, [None] * depth, [None] * n_b, [None] * n_a
    dk_layers, dv_layers = [None] * n_b, [None] * n_b
    for i in reversed(range(depth)):
        lay = saved[i]
        shift_m, scale_m, gate_m, shift_f, scale_f, gate_f = mods[i]
        df, sums_gf = _gate_bwd(dx, lay["f"], gate_f, name="gate_bwd")
        da, db = _ffn_down_bwd(df, full["ffn_down"], i, lay["a"], lay["b"], name="ffn_down_bwd")
        g_down = _mm(lay["z"], df, ta=True, out_dtype=BF16, out_stack=(depth, i, g_down), name="ffn_down_dw")
        dh2 = _ffn_in_dx(da, db, full["ffn_gate"], full["ffn_up"], i, name="ffn_in_dx")
        g_gate, g_up = _ffn_in_dw(lay["h2"], da, db, i, depth, None if g_gate is None else (g_gate, g_up), name="ffn_in_dw")
        dx1, sums_f = _norm_bwd(lay["x1"], _vec(norm_ffn[i]), dh2, scale=scale_f, resid=dx, name="norm_ffn_bwd")
        dyp, sums_gm = _gate_bwd(dx1, lay["y_pre"], lay["gate_eff"], name="gate_bwd")
        if i < n_a:
            g_pool[i] = _gmm(lay["pooled"], dyp, mode="tn", out_dtype=BF16, name="pool_mix_dw")
            dd = _gmm(dyp, pool_w[i], mode="nt", out_dtype=F32, name="pool_mix_dx")
            dh1 = _pool(dd, transpose=True, out_dtype=F32, name="pool_bwd")
            dgate_m = sums_gm[0] * pool_scale[i]
            g_pool_scale[i] = sums_gm[0] * gate_m[0]
        else:
            l = i - n_a
            do = _mm(dyp, w_o[l], tb=True, out_dtype=BF16, name="attn_out_dx")
            g_wo[l] = _mm(lay["o"], dyp, ta=True, out_dtype=BF16, name="attn_out_dw")
            dq, delta = _attn_bwd_q(lay["q"], kv_side["keys"], kv_side["kv"], lay["o"], do, lay["lse"], name="attn_bwd_q")
            dk_layers[l], dv_layers[l] = _attn_bwd_kv(
                lay["q"], kv_side["keys"], kv_side["kv"], do, lay["lse"].reshape(heads, -1, 1, _attn_tile(s)), delta.reshape(heads, -1, 1, _attn_tile(s)), name="attn_bwd_kv"
            )
            dq_pre = _rope_heads(dq, cos_t, sin_bwd, out_dtype=BF16, name="rope_q_bwd")
            dcq = _mm(dq_pre, w_q[l], tb=True, name="q_up_dx")
            g_wq[l] = _mm(lay["cq"], dq_pre, ta=True, out_dtype=BF16, name="q_up_dw")
            dcq_pre, sums_q = _norm_bwd(lay["cq_pre"], _vec(wts["q_norm"][l]), dcq, name="norm_q_bwd")
            g_q_norm[l] = sums_q[2]
            dh1 = _mm(dcq_pre, w_dq[l], tb=True, name="q_down_dx")
            g_wdq[l] = _mm(lay["h1"], dcq_pre, ta=True, out_dtype=BF16, name="q_down_dw")
            dgate_m = sums_gm[0]
        dx, sums_m = _norm_bwd(lay["x0"], _vec(norm_mix[i]), dh1, scale=scale_m, resid=dx1, name="norm_mix_bwd")
        if i == n_a:
            dkv, dkr_pre = _keys_bwd(dk_layers[0], dk_layers[1], dv_layers[0], dv_layers[1], cos_t, sin_bwd, name="keys_bwd")
            dckv = _mm(dkv, w_ukv, tb=True, name="kv_up_dx")
            g_ukv = _mm(kv_side["ckv"], dkv, ta=True, out_dtype=BF16, name="kv_up_dw")
            dckv_pre, sums_kvn = _norm_bwd(kv_side["ckv_pre"], _vec(wts["kv_norm"]), dckv, name="norm_kv_bwd")
            dpre = jnp.concatenate([dckv_pre, dkr_pre], axis=1)
            dh_kv = _mm(dpre, w_dkvkr, tb=True, name="kv_down_dx")
            g_dkvkr = _mm(kv_side["h_kv"], dpre, ta=True, out_dtype=BF16, name="kv_down_dw", tn=kvr + LANES)
            dx, sums_kv = _norm_bwd(lay["x0"], _vec(wts["kv_in_norm"]), dh_kv, scale=kv_scale, resid=dx, name="norm_kv_in_bwd")
        dmods[i] = jnp.concatenate([sums_m[0], sums_m[1], dgate_m, sums_f[0], sums_f[1], sums_gf[0]])
        g_norm_mix[i], g_norm_ffn[i] = sums_m[2], sums_f[2]
    grad_x = dx[None]

    g_ukv = g_ukv.reshape(kvr, heads, 2, NOPE_DIM)
    grad_full = {
        "pool_w": jnp.stack(g_pool).reshape(geoms["pool_w"].shape3),
        "w_dkv": g_dkvkr[None, :, :kvr],
        "w_uk": g_ukv[:, :, 0].reshape(1, kvr, heads * NOPE_DIM),
        "w_uv": g_ukv[:, :, 1].reshape(1, kvr, heads * V_DIM),
        "w_kr": g_dkvkr[None, :, kvr : kvr + ROPE_DIM],
        "w_dq": jnp.stack(g_wdq),
        "w_uq": jnp.stack(g_wq).reshape(n_b, qr, heads, HEAD_PAD)[..., : NOPE_DIM + ROPE_DIM].reshape(geoms["w_uq"].shape3),
        "w_o": jnp.stack(g_wo),
        "ffn_gate": g_gate,
        "ffn_up": g_up,
        "ffn_down": g_down,
    }

    small_grads = {
        "mod_b": jnp.concatenate(dmods),
        "kv_mod_b": jnp.concatenate([sums_kv[0], sums_kv[1]]),
        "norm_mix": jnp.concatenate(g_norm_mix),
        "norm_ffn": jnp.concatenate(g_norm_ffn),
        "kv_in_norm": sums_kv[2],
        "kv_norm": sums_kvn[2],
        "q_norm": jnp.concatenate(g_q_norm),
        "final_norm": final_stats[0],
    }
    packed = jnp.concatenate([small_grads[n] for n in SMALL] + g_pool_scale)
    small_rows = sum(wts[n].size for n in SMALL) // LANES
    every = _allgather8(_pad_rows(_rows(packed)), name="gather_small_grads")
    summed = _sum_devices(every, name="sum_small_grads")

    mod_rows = depth * nmod // LANES
    dm_all = every[:, :mod_rows].reshape(N_DEV, depth, nmod)
    dm = lax.dynamic_slice_in_dim(dm_all, chip * per, per, axis=2).transpose(1, 0, 2)
    dm = jnp.pad(dm, ((0, 0), (0, 16 - N_DEV), (0, 0)))
    dkvm_all = every[:, mod_rows : mod_rows + nkv // LANES].reshape(N_DEV, nkv)
    dkvm = jnp.pad(lax.dynamic_slice_in_dim(dkvm_all, chip * (nkv // N_CHIPS), nkv // N_CHIPS, axis=1), ((0, 16 - N_DEV), (0, 0)))[None]
    results = {}
    results["mod_w"] = _tp_adamw(sc16, dm, wts["mod_w"], mom["mod_w"], var["mod_w"], name="mod_w_update")
    results["kv_mod_w"] = [
        r[0] for r in _tp_adamw(sc16, dkvm, wts["kv_mod_w"][None], mom["kv_mod_w"][None], var["kv_mod_w"][None], name="kv_mod_w_update")
    ]

    ps_grad = lax.dynamic_slice_in_dim(summed[small_rows : small_rows + n_a * d // LANES].reshape(n_a, d), chip * (d // N_CHIPS), d // N_CHIPS, axis=1)
    small_names = SMALL + ("pool_scale",)

    def pack_small(tree):
        return _pad_rows(jnp.concatenate([_rows(tree[n]) for n in small_names]))

    g_small = _pad_rows(jnp.concatenate([summed[:small_rows], _rows(ps_grad)]))
    small_out = _adamw(pack_small(wts), g_small, pack_small(mom), pack_small(var), name="small_update")
    row = 0
    for n in small_names:
        nrow = wts[n].size // LANES
        results[n] = [r[row : row + nrow].reshape(wts[n].shape) for r in small_out]
        row += nrow

    theirs = _reduce_pair([grad_full[n] for n in gathered_names], geom_list, name="reduce_pair")
    pair_sum = [_pair_add(grad_full[n], th, geoms[n], core_arr, name="pair_add_" + n) for n, th in zip(gathered_names, theirs)]
    slots = _reduce_chips(pair_sum, name="reduce_chips")
    boths = [_chip_sum(ps, sl, geoms[n], place_arr, name="chip_sum_" + n) for n, ps, sl in zip(gathered_names, pair_sum, slots)]
    joined = _join_pair(boths, name="join_pair")
    for n, both in zip(gathered_names, joined):
        cs = geoms[n].cs
        out = _adamw(wts[n].reshape(-1, cs), both.reshape(-1, cs), mom[n].reshape(-1, cs), var[n].reshape(-1, cs), name="update_" + n)
        results[n] = [r.reshape(wts[n].shape) for r in out]

    outs = [loss, grad_x]
    for k in range(4):
        outs += [results[n][k] for n in WEIGHTS]
    return tuple(outs)


def kernel(x, c, positions, mod_w, mod_b, norm_mix, norm_ffn, pool_w, pool_scale, kv_mod_w, kv_mod_b, kv_in_norm, w_dkv, kv_norm, w_uk, w_uv, w_kr, w_dq, q_norm, w_uq, w_o, ffn_gate, ffn_up, ffn_down, final_norm, loss_target, m_mod_w, m_mod_b, m_norm_mix, m_norm_ffn, m_pool_w, m_pool_scale, m_kv_mod_w, m_kv_mod_b, m_kv_in_norm, m_w_dkv, m_kv_norm, m_w_uk, m_w_uv, m_w_kr, m_w_dq, m_q_norm, m_w_uq, m_w_o, m_ffn_gate, m_ffn_up, m_ffn_down, m_final_norm, v_mod_w, v_mod_b, v_norm_mix, v_norm_ffn, v_pool_w, v_pool_scale, v_kv_mod_w, v_kv_mod_b, v_kv_in_norm, v_w_dkv, v_kv_norm, v_w_uk, v_w_uv, v_w_kr, v_w_dq, v_q_norm, v_w_uq, v_w_o, v_ffn_gate, v_ffn_up, v_ffn_down, v_final_norm):
    given = dict(locals())
    wts = {n: given[n] for n in WEIGHTS}
    mom = {n: given["m_" + n] for n in WEIGHTS}
    var = {n: given["v_" + n] for n in WEIGHTS}
    return _step(x, c, positions, loss_target, wts, mom, var)
```

```python
import functools

import jax
import jax.numpy as jnp
from jax import lax
from jax.experimental import pallas as pl
from jax.experimental.pallas import tpu as pltpu

F32 = jnp.float32
BF16 = jnp.bfloat16
MESH = pl.DeviceIdType.MESH
ANY = pl.BlockSpec(memory_space=pl.ANY)

NORM_EPS = 1e-6
POOL_WINDOWS = (2, 4, 8, 16)
NOPE_DIM = 128
ROPE_DIM = 64
V_DIM = 128
HEAD_PAD = 256
SM_SCALE = (NOPE_DIM + ROPE_DIM) ** -0.5
ROPE_THETA = 10000.0
N_MOD = 6
ADAM_LR, ADAM_B1, ADAM_B2, ADAM_EPS, ADAM_WD, ADAM_STEP = 0.001, 0.9, 0.999, 1e-08, 0.01, 10
N_CHIPS = 4
N_DEV = 8
LANES = 128
HALO = 128
VMEM_LIMIT = 48 * 1024 * 1024


def _tile(dim, pref, align):
    if dim <= pref:
        return dim
    t = (pref // align) * align
    while t >= align:
        if dim % t == 0:
            return t
        t -= align
    return dim


def _params(*sem):
    return pltpu.CompilerParams(dimension_semantics=sem, vmem_limit_bytes=VMEM_LIMIT)


class _Hosted:
    def __init__(self, args, out_shapes, aliases, sem_shapes, start, finish):
        self.args, self.out_shapes, self.aliases, self.sem_shapes = list(args), list(out_shapes), dict(aliases), list(sem_shapes)
        self.start, self.finish = start, finish


def _call(body, *, name, grid, in_specs, out_specs, out_shape, args, sem, scratch_shapes=(), hosted=None):
    n_in, n_out, n_scr = len(args), len(out_shape), len(scratch_shapes)
    if hosted is None:
        outs = pl.pallas_call(
            body, name=name, grid=grid, in_specs=list(in_specs), out_specs=list(out_specs), out_shape=list(out_shape),
            scratch_shapes=list(scratch_shapes), compiler_params=_params(*sem),
        )(*args)
        return list(outs), []
    n_hin, n_hout = len(hosted.args), len(hosted.out_shapes)

    def carrying(*refs):
        own_in, their_in = refs[:n_in], refs[n_in : n_in + n_hin]
        refs = refs[n_in + n_hin :]
        own_out, their_out = refs[:n_out], refs[n_out : n_out + n_hout]
        refs = refs[n_out + n_hout :]
        own_scratch, sems = refs[:n_scr], refs[n_scr:]
        ids = [pl.program_id(axis) for axis in range(len(grid))]
        first = functools.reduce(jnp.logical_and, [i == 0 for i in ids])
        last = functools.reduce(jnp.logical_and, [i == size - 1 for i, size in zip(ids, grid)])

        @pl.when(first)
        def _():
            hosted.start(their_in, their_out, sems)

        body(*own_in, *own_out, *own_scratch)

        @pl.when(last)
        def _():
            hosted.finish(their_in, their_out, sems)

    outs = pl.pallas_call(
        carrying,
        name=name,
        grid=grid,
        in_specs=list(in_specs) + [ANY] * n_hin,
        out_specs=list(out_specs) + [ANY] * n_hout,
        out_shape=list(out_shape) + hosted.out_shapes,
        input_output_aliases={n_in + i: n_out + o for i, o in hosted.aliases.items()},
        scratch_shapes=list(scratch_shapes) + hosted.sem_shapes,
        compiler_params=_params(*["arbitrary"] * len(grid)),
    )(*args, *hosted.args)
    return list(outs[:n_out]), list(outs[n_out:])


def _run(plan, *, name):
    n_in, n_out = len(plan.args), len(plan.out_shapes)

    def body(*refs):
        ins, outs, sems = refs[:n_in], refs[n_in : n_in + n_out], refs[n_in + n_out :]
        plan.start(ins, outs, sems)
        plan.finish(ins, outs, sems)

    return pl.pallas_call(
        body, name=name, in_specs=[ANY] * n_in, out_specs=[ANY] * n_out, out_shape=plan.out_shapes,
        input_output_aliases=plan.aliases, scratch_shapes=plan.sem_shapes,
    )(*plan.args)


def _mm(a, b, *, name, ta=False, tb=False, out_dtype=F32, b_idx=None, hosted=None, tm=1024, tn=1024, tk=None):
    m, k = (a.shape[1], a.shape[0]) if ta else a.shape
    b2 = b.shape if b_idx is None else b.shape[1:]
    kb, n = (b2[1], b2[0]) if tb else b2
    assert k == kb, (a.shape, b.shape, ta, tb)
    tm = _tile(m, tm, LANES)
    tn = _tile(n, tn, LANES)
    tk = _tile(k, 2048 if tk is None else tk, LANES)
    nk = k // tk
    dims = (((0 if ta else 1,), (1 if tb else 0,)), ((), ()))

    def body(a_ref, b_ref, o_ref, *acc):
        part = lax.dot_general(a_ref[...].astype(BF16), b_ref[...].astype(BF16), dims, preferred_element_type=F32)
        if nk == 1:
            o_ref[...] = part.astype(o_ref.dtype)
        else:
            acc_ref = acc[0]
            step = pl.program_id(2)

            @pl.when(step == 0)
            def _():
                acc_ref[...] = part

            @pl.when(step > 0)
            def _():
                acc_ref[...] += part

            @pl.when(step == nk - 1)
            def _():
                o_ref[...] = acc_ref[...].astype(o_ref.dtype)

    a_spec = pl.BlockSpec((tk, tm), lambda i, j, s: (s, i)) if ta else pl.BlockSpec((tm, tk), lambda i, j, s: (i, s))
    if b_idx is None:
        b_spec = pl.BlockSpec((tn, tk), lambda i, j, s: (j, s)) if tb else pl.BlockSpec((tk, tn), lambda i, j, s: (s, j))
    elif tb:
        b_spec = pl.BlockSpec((None, tn, tk), lambda i, j, s: (b_idx, j, s))
    else:
        b_spec = pl.BlockSpec((None, tk, tn), lambda i, j, s: (b_idx, s, j))
    outs, carried = _call(
        body,
        name=name,
        grid=(m // tm, n // tn, nk),
        in_specs=[a_spec, b_spec],
        out_specs=[pl.BlockSpec((tm, tn), lambda i, j, s: (i, j))],
        out_shape=[jax.ShapeDtypeStruct((m, n), out_dtype)],
        args=[a, b],
        sem=("parallel", "parallel", "arbitrary"),
        scratch_shapes=[pltpu.VMEM((tm, tn), F32)] if nk > 1 else [],
        hosted=hosted,
    )
    return outs[0] if hosted is None else (outs[0], carried)


def _tp_fwd(sc16, w, bias, *, name):
    nl, d, n = w.shape
    tn = _tile(n, 512, LANES)

    def body(sc_ref, w_ref, b_ref, o_ref):
        o_ref[0] = jnp.dot(sc_ref[...].astype(BF16), w_ref[0].astype(BF16), preferred_element_type=F32) + b_ref[0]

    return pl.pallas_call(
        body,
        name=name,
        grid=(nl, n // tn),
        in_specs=[
            pl.BlockSpec((16, d), lambda l, j: (0, 0)),
            pl.BlockSpec((1, d, tn), lambda l, j: (l, 0, j)),
            pl.BlockSpec((1, 1, tn), lambda l, j: (l, 0, j)),
        ],
        out_specs=pl.BlockSpec((1, 16, tn), lambda l, j: (l, 0, j)),
        out_shape=jax.ShapeDtypeStruct((nl, 16, n), F32),
        compiler_params=_params("parallel", "parallel"),
    )(sc16, w, bias)


_NT = (((1,), (1,)), ((), ()))
_TN = (((0,), (0,)), ((), ()))


def _silu_parts(a):
    sig = jax.nn.sigmoid(a)
    return a * sig, sig * (1.0 + a * (1.0 - sig))


def _ffn_in(h, w_gate, w_up, layer, *, name, hosted=None):
    s, d = h.shape
    f = w_gate.shape[2]
    tm, tn = _tile(s, 1024, LANES), _tile(f, 512, LANES)

    def body(h_ref, g_ref, u_ref, a_ref, b_ref, z_ref):
        hv = h_ref[...]
        a = jnp.dot(hv, g_ref[...], preferred_element_type=F32)
        b = jnp.dot(hv, u_ref[...], preferred_element_type=F32)
        a_ref[...] = a.astype(a_ref.dtype)
        b_ref[...] = b.astype(b_ref.dtype)
        z_ref[...] = (_silu_parts(a)[0] * b).astype(z_ref.dtype)

    w_spec = pl.BlockSpec((None, d, tn), lambda i, j: (layer, 0, j))
    out = pl.BlockSpec((tm, tn), lambda i, j: (i, j))
    return _call(
        body,
        name=name,
        grid=(s // tm, f // tn),
        in_specs=[pl.BlockSpec((tm, d), lambda i, j: (i, 0)), w_spec, w_spec],
        out_specs=[out] * 3,
        out_shape=[jax.ShapeDtypeStruct((s, f), BF16)] * 3,
        args=[h, w_gate, w_up],
        sem=("parallel", "parallel"),
        hosted=hosted,
    )


def _ffn_down_bwd(df, w_down, layer, a, b, *, name, hosted=None):
    s, d = df.shape
    f = w_down.shape[1]
    tm, tn = _tile(s, 1024, LANES), _tile(f, 512, LANES)

    def body(df_ref, w_ref, a_ref, b_ref, da_ref, db_ref):
        dz = lax.dot_general(df_ref[...], w_ref[...], _NT, preferred_element_type=F32)
        silu, dsilu = _silu_parts(a_ref[...].astype(F32))
        da_ref[...] = (dz * b_ref[...].astype(F32) * dsilu).astype(da_ref.dtype)
        db_ref[...] = (dz * silu).astype(db_ref.dtype)

    blk = pl.BlockSpec((tm, tn), lambda i, j: (i, j))
    return _call(
        body,
        name=name,
        grid=(s // tm, f // tn),
        in_specs=[pl.BlockSpec((tm, d), lambda i, j: (i, 0)), pl.BlockSpec((None, tn, d), lambda i, j: (layer, j, 0)), blk, blk],
        out_specs=[blk, blk],
        out_shape=[jax.ShapeDtypeStruct((s, f), BF16)] * 2,
        args=[df, w_down, a, b],
        sem=("parallel", "parallel"),
        hosted=hosted,
    )


def _ffn_in_dx(da, db, w_gate, w_up, layer, *, name, hosted=None):
    s, f = da.shape
    d = w_gate.shape[1]
    tm, tn, tk = _tile(s, 1024, LANES), _tile(d, 1024, LANES), _tile(f, 2048, LANES)
    nk = f // tk

    def body(da_ref, db_ref, g_ref, u_ref, o_ref, acc_ref):
        part = lax.dot_general(da_ref[...], g_ref[...], _NT, preferred_element_type=F32)
        part = part + lax.dot_general(db_ref[...], u_ref[...], _NT, preferred_element_type=F32)
        step = pl.program_id(2)

        @pl.when(step == 0)
        def _():
            acc_ref[...] = part

        @pl.when(step > 0)
        def _():
            acc_ref[...] += part

        @pl.when(step == nk - 1)
        def _():
            o_ref[...] = acc_ref[...]

    x_spec = pl.BlockSpec((tm, tk), lambda i, j, k: (i, k))
    w_spec = pl.BlockSpec((None, tn, tk), lambda i, j, k: (layer, j, k))
    return _call(
        body,
        name=name,
        grid=(s // tm, d // tn, nk),
        in_specs=[x_spec, x_spec, w_spec, w_spec],
        out_specs=[pl.BlockSpec((tm, tn), lambda i, j, k: (i, j))],
        out_shape=[jax.ShapeDtypeStruct((s, d), F32)],
        args=[da, db, w_gate, w_up],
        sem=("parallel", "parallel", "arbitrary"),
        scratch_shapes=[pltpu.VMEM((tm, tn), F32)],
        hosted=hosted,
    )


def _ffn_in_dw(h, da, db, *, name, hosted=None):
    s, d = h.shape
    f = da.shape[1]
    tm, tn, tk = _tile(d, 1024, LANES), _tile(f, 512, LANES), _tile(s, 2048, LANES)
    nk = s // tk

    def body(h_ref, da_ref, db_ref, g_ref, u_ref, g_acc, u_acc):
        hv = h_ref[...]
        pg = lax.dot_general(hv, da_ref[...], _TN, preferred_element_type=F32)
        pu = lax.dot_general(hv, db_ref[...], _TN, preferred_element_type=F32)
        step = pl.program_id(2)

        @pl.when(step == 0)
        def _():
            g_acc[...] = pg
            u_acc[...] = pu

        @pl.when(step > 0)
        def _():
            g_acc[...] += pg
            u_acc[...] += pu

        @pl.when(step == nk - 1)
        def _():
            g_ref[...] = g_acc[...].astype(g_ref.dtype)
            u_ref[...] = u_acc[...].astype(u_ref.dtype)

    y_spec = pl.BlockSpec((tk, tn), lambda i, j, k: (k, j))
    out = pl.BlockSpec((tm, tn), lambda i, j, k: (i, j))
    return _call(
        body,
        name=name,
        grid=(d // tm, f // tn, nk),
        in_specs=[pl.BlockSpec((tk, tm), lambda i, j, k: (k, i)), y_spec, y_spec],
        out_specs=[out, out],
        out_shape=[jax.ShapeDtypeStruct((d, f), BF16)] * 2,
        args=[h, da, db],
        sem=("parallel", "parallel", "arbitrary"),
        scratch_shapes=[pltpu.VMEM((tm, tn), F32)] * 2,
        hosted=hosted,
    )


def _gmm(a, w, *, name, mode, out_dtype):
    s = a.shape[0]
    g = len(POOL_WINDOWS)
    c = a.shape[1] // g
    tr = _tile(s, 1024, LANES)
    n_row = s // tr

    if mode == "tn":

        def body(a_ref, b_ref, o_ref, acc_ref):
            part = lax.dot_general(a_ref[...].astype(BF16), b_ref[...].astype(BF16), (((0,), (0,)), ((), ())), preferred_element_type=F32)

            @pl.when(pl.program_id(1) == 0)
            def _():
                acc_ref[...] = part

            @pl.when(pl.program_id(1) > 0)
            def _():
                acc_ref[...] += part

            @pl.when(pl.program_id(1) == n_row - 1)
            def _():
                o_ref[0] = acc_ref[...].astype(o_ref.dtype)

        return pl.pallas_call(
            body,
            name=name,
            grid=(g, n_row),
            in_specs=[pl.BlockSpec((tr, c), lambda gi, i: (i, gi)), pl.BlockSpec((tr, c), lambda gi, i: (i, gi))],
            out_specs=pl.BlockSpec((1, c, c), lambda gi, i: (gi, 0, 0)),
            out_shape=jax.ShapeDtypeStruct((g, c, c), out_dtype),
            scratch_shapes=[pltpu.VMEM((c, c), F32)],
            compiler_params=_params("parallel", "arbitrary"),
        )(a, w)

    dims = (((1,), (0 if mode == "nn" else 1,)), ((), ()))

    def body(a_ref, w_ref, o_ref):
        o_ref[...] = lax.dot_general(a_ref[...].astype(BF16), w_ref[0].astype(BF16), dims, preferred_element_type=F32).astype(o_ref.dtype)

    return pl.pallas_call(
        body,
        name=name,
        grid=(g, n_row),
        in_specs=[pl.BlockSpec((tr, c), lambda gi, i: (i, gi)), pl.BlockSpec((1, c, c), lambda gi, i: (gi, 0, 0))],
        out_specs=pl.BlockSpec((tr, c), lambda gi, i: (i, gi)),
        out_shape=jax.ShapeDtypeStruct((s, g * c), out_dtype),
        compiler_params=_params("parallel", "parallel"),
    )(a, w)


def _row_tile(s, d):
    return _tile(s, max(8, (1 << 19) // d), 8)


def _norm_fwd(x, g, *, name, scale=None, shift=None, y=None, gate=None, out_dtype=BF16):
    s, d = x.shape
    tr = _row_tile(s, d)
    has_res, has_mod = y is not None, scale is not None

    def body(*refs):
        refs = list(refs)
        x_ref = refs.pop(0)
        xv = x_ref[...]
        if has_res:
            y_ref, gate_ref = refs.pop(0), refs.pop(0)
            xv = xv + gate_ref[...] * y_ref[...]
        g_ref = refs.pop(0)
        if has_mod:
            scale_ref, shift_ref = refs.pop(0), refs.pop(0)
        if has_res:
            refs.pop(0)[...] = xv
        h = xv * lax.rsqrt(jnp.mean(xv * xv, axis=-1, keepdims=True) + NORM_EPS)
        h = h * g_ref[...]
        if has_mod:
            h = h * (1.0 + scale_ref[...]) + shift_ref[...]
        refs.pop(0)[...] = h.astype(out_dtype)

    row = pl.BlockSpec((tr, d), lambda i: (i, 0))
    vec = pl.BlockSpec((1, d), lambda i: (0, 0))
    args, in_specs = [x], [row]
    if has_res:
        args += [y, gate]
        in_specs += [row, vec]
    args.append(g)
    in_specs.append(vec)
    if has_mod:
        args += [scale, shift]
        in_specs += [vec, vec]
    out_shape, out_specs = [], []
    if has_res:
        out_shape.append(jax.ShapeDtypeStruct((s, d), F32))
        out_specs.append(row)
    out_shape.append(jax.ShapeDtypeStruct((s, d), out_dtype))
    out_specs.append(row)
    res = pl.pallas_call(
        body, name=name, grid=(s // tr,), in_specs=in_specs, out_specs=out_specs, out_shape=out_shape, compiler_params=_params("parallel")
    )(*args)
    return (res[0], res[1]) if has_res else res[0]


def _norm_bwd(x, g, dh, *, name, scale=None, resid=None):
    s, d = x.shape
    tr = _row_tile(s, d)
    has_mod, has_res = scale is not None, resid is not None

    def body(*refs):
        refs = list(refs)
        x_ref, g_ref, dh_ref = refs.pop(0), refs.pop(0), refs.pop(0)
        scale_ref = refs.pop(0) if has_mod else None
        resid_ref = refs.pop(0) if has_res else None
        dx_ref, sums_ref = refs
        xv = x_ref[...]
        r = lax.rsqrt(jnp.mean(xv * xv, axis=-1, keepdims=True) + NORM_EPS)
        xhat = xv * r
        dh32 = dh_ref[...].astype(F32)
        gv = g_ref[...]
        dn = dh32 * (1.0 + scale_ref[...]) if has_mod else dh32
        dxhat = dn * gv
        dx = r * (dxhat - xhat * jnp.mean(dxhat * xhat, axis=-1, keepdims=True))
        if has_res:
            dx = dx + resid_ref[...]
        dx_ref[...] = dx

        @pl.when(pl.program_id(0) == 0)
        def _():
            sums_ref[...] = jnp.zeros_like(sums_ref)

        sums_ref[0:1, :] += jnp.sum(dh32, axis=0, keepdims=True)
        sums_ref[1:2, :] += jnp.sum(dh32 * (xhat * gv), axis=0, keepdims=True)
        sums_ref[2:3, :] += jnp.sum(dn * xhat, axis=0, keepdims=True)

    row = pl.BlockSpec((tr, d), lambda i: (i, 0))
    vec = pl.BlockSpec((1, d), lambda i: (0, 0))
    args, in_specs = [x, g, dh], [row, vec, row]
    if has_mod:
        args.append(scale)
        in_specs.append(vec)
    if has_res:
        args.append(resid)
        in_specs.append(row)
    return pl.pallas_call(
        body,
        name=name,
        grid=(s // tr,),
        in_specs=in_specs,
        out_specs=[row, pl.BlockSpec((8, d), lambda i: (0, 0))],
        out_shape=[jax.ShapeDtypeStruct((s, d), F32), jax.ShapeDtypeStruct((8, d), F32)],
        compiler_params=_params("arbitrary"),
    )(*args)


def _gate_bwd(dx, y, gate, *, name):
    s, d = dx.shape
    tr = _row_tile(s, d)

    def body(dx_ref, y_ref, gate_ref, dy_ref, sums_ref):
        dxv = dx_ref[...]
        dy_ref[...] = (dxv * gate_ref[...]).astype(dy_ref.dtype)

        @pl.when(pl.program_id(0) == 0)
        def _():
            sums_ref[...] = jnp.zeros_like(sums_ref)

        sums_ref[0:1, :] += jnp.sum(dxv * y_ref[...], axis=0, keepdims=True)

    row = pl.BlockSpec((tr, d), lambda i: (i, 0))
    return pl.pallas_call(
        body,
        name=name,
        grid=(s // tr,),
        in_specs=[row, row, pl.BlockSpec((1, d), lambda i: (0, 0))],
        out_specs=[row, pl.BlockSpec((8, d), lambda i: (0, 0))],
        out_shape=[jax.ShapeDtypeStruct((s, d), BF16), jax.ShapeDtypeStruct((8, d), F32)],
        compiler_params=_params("arbitrary"),
    )(dx, y, gate)


def _elementwise(fn, args, out_dtypes, *, name):
    s, d = args[0].shape
    tc = d if d <= 2048 else _tile(d, 1024, LANES)
    tr = _tile(s, max(8, (1 << 18) // tc), 8)
    n_in = len(args)

    def body(*refs):
        outs = fn(*[r[...] for r in refs[:n_in]])
        for o_ref, o in zip(refs[n_in:], outs):
            o_ref[...] = o.astype(o_ref.dtype)

    spec = pl.BlockSpec((tr, tc), lambda i, j: (i, j))
    return pl.pallas_call(
        body,
        name=name,
        grid=(s // tr, d // tc),
        in_specs=[spec] * n_in,
        out_specs=[spec] * len(out_dtypes),
        out_shape=[jax.ShapeDtypeStruct((s, d), dt) for dt in out_dtypes],
        compiler_params=_params("parallel", "parallel"),
    )(*args)


def _silu(v):
    return (v * jax.nn.sigmoid(v),)


def _split3(v):
    hi = v.astype(BF16)
    r1 = v - hi.astype(F32)
    mid = r1.astype(BF16)
    lo = (r1 - mid.astype(F32)).astype(BF16)
    return hi, mid, lo


def _band_dot(band, v):
    return sum(jnp.dot(band, part, preferred_element_type=F32) for part in _split3(v))


def _pool(h, *, name, transpose, out_dtype):
    s, d = h.shape
    c = d // len(POOL_WINDOWS)
    tr = _tile(s, 256, HALO)
    per = tr // HALO
    n_halo = s // HALO

    def body(h_ref, halo_ref, o_ref):
        i = pl.program_id(0)
        out_row = i * tr + lax.broadcasted_iota(jnp.int32, (tr, tr + HALO), 0)
        col = lax.broadcasted_iota(jnp.int32, (tr, tr + HALO), 1)
        if transpose:
            ext = jnp.concatenate([h_ref[...], halo_ref[...]], axis=0)
            src_row = i * tr + col
            ext_row = i * tr + lax.broadcasted_iota(jnp.int32, (tr + HALO, 1), 0)
        else:
            ext = jnp.concatenate([halo_ref[...], h_ref[...]], axis=0)
            src_row = i * tr + col - HALO
            own_row = i * tr + lax.broadcasted_iota(jnp.int32, (tr, 1), 0)
        for gi, w in enumerate(POOL_WINDOWS):
            cols = slice(gi * c, (gi + 1) * c)
            if transpose:
                band = (src_row >= out_row) & (src_row < out_row + w) & (src_row < s)
                scaled = ext[:, cols] / jnp.minimum(ext_row + 1, w).astype(F32)
                res = _band_dot(band.astype(BF16), scaled) - h_ref[:, cols]
            else:
                band = (src_row <= out_row) & (src_row > out_row - w) & (src_row >= 0)
                res = _band_dot(band.astype(BF16), ext[:, cols]) / jnp.minimum(own_row + 1, w).astype(F32) - h_ref[:, cols]
            o_ref[:, cols] = res.astype(o_ref.dtype)

    if transpose:
        halo_map = lambda i: (jnp.minimum((i + 1) * per, n_halo - 1), 0)
    else:
        halo_map = lambda i: (jnp.maximum(i * per - 1, 0), 0)
    return pl.pallas_call(
        body,
        name=name,
        grid=(s // tr,),
        in_specs=[pl.BlockSpec((tr, d), lambda i: (i, 0)), pl.BlockSpec((HALO, d), halo_map)],
        out_specs=pl.BlockSpec((tr, d), lambda i: (i, 0)),
        out_shape=jax.ShapeDtypeStruct((s, d), out_dtype),
        compiler_params=_params("parallel"),
    )(h, h)


def _rotate(v, cos, sin):
    lane = lax.broadcasted_iota(jnp.int32, v.shape, 1)
    swapped = jnp.where(lane % ROPE_DIM < ROPE_DIM // 2, pltpu.roll(v, LANES - ROPE_DIM // 2, 1), pltpu.roll(v, ROPE_DIM // 2, 1))
    return v * cos + swapped * sin


def _rope_heads(x, cos, sin, *, name, out_dtype):
    s, n = x.shape
    tr = _tile(s, max(16, (1 << 18) // n), 16)

    def body(x_ref, cos_ref, sin_ref, o_ref):
        cos_v, sin_v = cos_ref[...], sin_ref[...]
        for j in range(n // LANES):
            lanes = slice(j * LANES, (j + 1) * LANES)
            if j % 2 == 0:
                o_ref[:, lanes] = x_ref[:, lanes].astype(o_ref.dtype)
            else:
                o_ref[:, lanes] = _rotate(x_ref[:, lanes].astype(F32), cos_v, sin_v).astype(o_ref.dtype)

    blk = pl.BlockSpec((tr, n), lambda i: (i, 0))
    tab = pl.BlockSpec((tr, LANES), lambda i: (i, 0))
    return pl.pallas_call(
        body,
        name=name,
        grid=(s // tr,),
        in_specs=[blk, tab, tab],
        out_specs=blk,
        out_shape=jax.ShapeDtypeStruct((s, n), out_dtype),
        compiler_params=_params("parallel"),
    )(x, cos, sin)


def _build_keys(kv, kr_pre, cos, sin, *, name):
    s, n = kv.shape
    tr = _tile(s, max(16, (1 << 18) // n), 16)

    def body(kv_ref, kr_ref, cos_ref, sin_ref, o_ref):
        rope = _rotate(kr_ref[...], cos_ref[...], sin_ref[...]).astype(o_ref.dtype)
        for j in range(n // LANES):
            lanes = slice(j * LANES, (j + 1) * LANES)
            o_ref[:, lanes] = kv_ref[:, lanes] if j % 2 == 0 else rope

    blk = pl.BlockSpec((tr, n), lambda i: (i, 0))
    tab = pl.BlockSpec((tr, LANES), lambda i: (i, 0))
    return pl.pallas_call(
        body,
        name=name,
        grid=(s // tr,),
        in_specs=[blk, tab, tab, tab],
        out_specs=blk,
        out_shape=jax.ShapeDtypeStruct((s, n), BF16),
        compiler_params=_params("parallel"),
    )(kv, kr_pre, cos, sin)


def _keys_bwd(dk_a, dk_b, dv_a, dv_b, cos, sin_neg, *, name):
    s, n = dk_a.shape
    heads = n // HEAD_PAD
    tr = _tile(s, 512, 8)

    def body(dka_ref, dkb_ref, dva_ref, dvb_ref, cos_ref, sin_ref, dkv_ref, dkr_ref):
        hd = pl.program_id(1)
        dk = dka_ref[...] + dkb_ref[...]
        dkv_ref[:, :NOPE_DIM] = dk[:, :NOPE_DIM].astype(dkv_ref.dtype)
        dkv_ref[:, NOPE_DIM:] = (dva_ref[...] + dvb_ref[...]).astype(dkv_ref.dtype)

        @pl.when(hd == 0)
        def _():
            dkr_ref[...] = dk[:, NOPE_DIM:]

        @pl.when(hd > 0)
        def _():
            dkr_ref[...] += dk[:, NOPE_DIM:]

        @pl.when(hd == heads - 1)
        def _():
            dkr_ref[...] = _rotate(dkr_ref[...], cos_ref[...], sin_ref[...])

    dk_blk = pl.BlockSpec((tr, HEAD_PAD), lambda i, hd: (i, hd))
    dv_blk = pl.BlockSpec((tr, V_DIM), lambda i, hd: (i, hd))
    tab = pl.BlockSpec((tr, LANES), lambda i, hd: (i, 0))
    return pl.pallas_call(
        body,
        name=name,
        grid=(s // tr, heads),
        in_specs=[dk_blk, dk_blk, dv_blk, dv_blk, tab, tab],
        out_specs=[dk_blk, tab],
        out_shape=[jax.ShapeDtypeStruct((s, n), BF16), jax.ShapeDtypeStruct((s, LANES), F32)],
        compiler_params=_params("parallel", "arbitrary"),
    )(dk_a, dk_b, dv_a, dv_b, cos, sin_neg)


def _attn_tile(s):
    return _tile(s, 512, LANES)


def _causal_mask(t, transposed=False):
    rows = lax.broadcasted_iota(jnp.int32, (t, t), 0)
    cols = lax.broadcasted_iota(jnp.int32, (t, t), 1)
    return rows <= cols if transposed else cols <= rows


def _attn_fwd(q, keys, kv, *, name):
    s = q.shape[0]
    heads = q.shape[1] // HEAD_PAD
    t = _attn_tile(s)

    def body(q_ref, k_ref, v_ref, o_ref, lse_ref, m_ref, l_ref, acc_ref):
        qi = pl.program_id(1)
        qv = q_ref[...]
        m_ref[...] = jnp.full_like(m_ref, -jnp.inf)
        l_ref[...] = jnp.zeros_like(l_ref)
        acc_ref[...] = jnp.zeros_like(acc_ref)

        def block(kb, diagonal):
            rows = pl.ds(pl.multiple_of(kb * t, t), t)
            sc = lax.dot_general(qv, k_ref[rows, :], _NT, preferred_element_type=F32) * SM_SCALE
            if diagonal:
                sc = jnp.where(_causal_mask(t), sc, -jnp.inf)
            m_new = jnp.maximum(m_ref[...], jnp.max(sc, axis=1, keepdims=True))
            alpha = jnp.exp(m_ref[...] - m_new)
            p = jnp.exp(sc - m_new)
            l_ref[...] = alpha * l_ref[...] + jnp.sum(p, axis=1, keepdims=True)
            acc_ref[...] = alpha * acc_ref[...] + jnp.dot(p.astype(BF16), v_ref[rows, :], preferred_element_type=F32)
            m_ref[...] = m_new

        def earlier(kb, carry):
            block(kb, False)
            return carry

        lax.fori_loop(0, qi, earlier, 0)
        block(qi, True)
        o_ref[...] = (acc_ref[...] / l_ref[...]).astype(o_ref.dtype)
        lse_ref[0] = m_ref[...] + jnp.log(l_ref[...])

    return pl.pallas_call(
        body,
        name=name,
        grid=(heads, s // t),
        in_specs=[
            pl.BlockSpec((t, HEAD_PAD), lambda hd, qi: (qi, hd)),
            pl.BlockSpec((s, HEAD_PAD), lambda hd, qi: (0, hd)),
            pl.BlockSpec((s, V_DIM), lambda hd, qi: (0, 2 * hd + 1)),
        ],
        out_specs=[pl.BlockSpec((t, V_DIM), lambda hd, qi: (qi, hd)), pl.BlockSpec((1, t, 1), lambda hd, qi: (hd, qi, 0))],
        out_shape=[jax.ShapeDtypeStruct((s, heads * V_DIM), BF16), jax.ShapeDtypeStruct((heads, s, 1), F32)],
        scratch_shapes=[pltpu.VMEM((t, 1), F32), pltpu.VMEM((t, 1), F32), pltpu.VMEM((t, V_DIM), F32)],
        compiler_params=_params("parallel", "parallel"),
    )(q, keys, kv)


def _attn_bwd_q(q, keys, kv, o, do, lse, *, name):
    s = q.shape[0]
    heads = q.shape[1] // HEAD_PAD
    t = _attn_tile(s)

    def body(q_ref, k_ref, v_ref, o_ref, do_ref, lse_ref, dq_ref, delta_ref, acc_ref):
        qi = pl.program_id(1)
        qv, dov, lse = q_ref[...], do_ref[...], lse_ref[0]
        delta = jnp.sum(dov.astype(F32) * o_ref[...].astype(F32), axis=1, keepdims=True)
        delta_ref[0] = delta
        acc_ref[...] = jnp.zeros_like(acc_ref)

        def block(kb, diagonal):
            rows = pl.ds(pl.multiple_of(kb * t, t), t)
            kv_ = k_ref[rows, :]
            sc = lax.dot_general(qv, kv_, _NT, preferred_element_type=F32) * SM_SCALE
            p = jnp.exp(sc - lse)
            if diagonal:
                p = jnp.where(_causal_mask(t), p, 0.0)
            dp = lax.dot_general(dov, v_ref[rows, :], _NT, preferred_element_type=F32)
            ds = p * (dp - delta) * SM_SCALE
            acc_ref[...] += jnp.dot(ds.astype(BF16), kv_, preferred_element_type=F32)

        def earlier(kb, carry):
            block(kb, False)
            return carry

        lax.fori_loop(0, qi, earlier, 0)
        block(qi, True)
        dq_ref[...] = acc_ref[...]

    o_blk = pl.BlockSpec((t, V_DIM), lambda hd, qi: (qi, hd))
    col = pl.BlockSpec((1, t, 1), lambda hd, qi: (hd, qi, 0))
    return pl.pallas_call(
        body,
        name=name,
        grid=(heads, s // t),
        in_specs=[
            pl.BlockSpec((t, HEAD_PAD), lambda hd, qi: (qi, hd)),
            pl.BlockSpec((s, HEAD_PAD), lambda hd, qi: (0, hd)),
            pl.BlockSpec((s, V_DIM), lambda hd, qi: (0, 2 * hd + 1)),
            o_blk,
            o_blk,
            col,
        ],
        out_specs=[pl.BlockSpec((t, HEAD_PAD), lambda hd, qi: (qi, hd)), col],
        out_shape=[jax.ShapeDtypeStruct((s, heads * HEAD_PAD), F32), jax.ShapeDtypeStruct((heads, s, 1), F32)],
        scratch_shapes=[pltpu.VMEM((t, HEAD_PAD), F32)],
        compiler_params=_params("parallel", "parallel"),
    )(q, keys, kv, o, do, lse)


def _attn_bwd_kv(q, keys, kv, do, lse_row, delta_row, *, name):
    s = q.shape[0]
    heads = q.shape[1] // HEAD_PAD
    t = _attn_tile(s)
    nq = s // t

    def body(q_ref, k_ref, v_ref, do_ref, lse_ref, delta_ref, dk_ref, dv_ref, dk_acc, dv_acc):
        ki = pl.program_id(1)
        kv_, vv = k_ref[...], v_ref[...]
        dk_acc[...] = jnp.zeros_like(dk_acc)
        dv_acc[...] = jnp.zeros_like(dv_acc)

        def block(qb, diagonal):
            rows = pl.ds(pl.multiple_of(qb * t, t), t)
            qv, dov = q_ref[rows, :], do_ref[rows, :]
            sc_t = lax.dot_general(kv_, qv, _NT, preferred_element_type=F32) * SM_SCALE
            p_t = jnp.exp(sc_t - lse_ref[0, qb])
            if diagonal:
                p_t = jnp.where(_causal_mask(t, transposed=True), p_t, 0.0)
            dv_acc[...] += jnp.dot(p_t.astype(BF16), dov, preferred_element_type=F32)
            dp_t = lax.dot_general(vv, dov, _NT, preferred_element_type=F32)
            ds_t = p_t * (dp_t - delta_ref[0, qb]) * SM_SCALE
            dk_acc[...] += jnp.dot(ds_t.astype(BF16), qv, preferred_element_type=F32)

        def later(qb, carry):
            block(qb, False)
            return carry

        block(ki, True)
        lax.fori_loop(ki + 1, nq, later, 0)
        dk_ref[...] = dk_acc[...]
        dv_ref[...] = dv_acc[...]

    row = pl.BlockSpec((1, nq, 1, t), lambda hd, ki: (hd, 0, 0, 0))
    return pl.pallas_call(
        body,
        name=name,
        grid=(heads, nq),
        in_specs=[
            pl.BlockSpec((s, HEAD_PAD), lambda hd, ki: (0, hd)),
            pl.BlockSpec((t, HEAD_PAD), lambda hd, ki: (ki, hd)),
            pl.BlockSpec((t, V_DIM), lambda hd, ki: (ki, 2 * hd + 1)),
            pl.BlockSpec((s, V_DIM), lambda hd, ki: (0, hd)),
            row,
            row,
        ],
        out_specs=[pl.BlockSpec((t, HEAD_PAD), lambda hd, ki: (ki, hd)), pl.BlockSpec((t, V_DIM), lambda hd, ki: (ki, hd))],
        out_shape=[jax.ShapeDtypeStruct((s, heads * HEAD_PAD), F32), jax.ShapeDtypeStruct((s, heads * V_DIM), F32)],
        scratch_shapes=[pltpu.VMEM((t, HEAD_PAD), F32), pltpu.VMEM((t, V_DIM), F32)],
        compiler_params=_params("parallel", "parallel"),
    )(q, keys, kv, do, lse_row, delta_row)


def _loss_bwd(x, y, gate, g, target, *, name):
    s, d = x.shape
    tr = _row_tile(s, d)

    def body(x_ref, y_ref, gate_ref, g_ref, t_ref, dx_ref, stats_ref, loss_ref):
        xv = x_ref[...] + gate_ref[...] * y_ref[...]
        r = lax.rsqrt(jnp.mean(xv * xv, axis=-1, keepdims=True) + NORM_EPS)
        xhat = xv * r
        gv = g_ref[...]
        err = xhat * gv - t_ref[...]
        dy = err / d
        dxhat = dy * gv
        dx_ref[...] = r * (dxhat - xhat * jnp.mean(dxhat * xhat, axis=-1, keepdims=True))

        @pl.when(pl.program_id(0) == 0)
        def _():
            stats_ref[...] = jnp.zeros_like(stats_ref)
            loss_ref[...] = jnp.zeros_like(loss_ref)

        stats_ref[0:1, :] += jnp.sum(dy * xhat, axis=0, keepdims=True)
        loss_ref[...] += 0.5 * jnp.sum(jnp.mean(err * err, axis=-1, keepdims=True))

    row = pl.BlockSpec((tr, d), lambda i: (i, 0))
    vec = pl.BlockSpec((1, d), lambda i: (0, 0))
    return pl.pallas_call(
        body,
        name=name,
        grid=(s // tr,),
        in_specs=[row, row, vec, vec, row],
        out_specs=[row, pl.BlockSpec((8, d), lambda i: (0, 0)), pl.BlockSpec((8, LANES), lambda i: (0, 0))],
        out_shape=[jax.ShapeDtypeStruct((s, d), F32), jax.ShapeDtypeStruct((8, d), F32), jax.ShapeDtypeStruct((8, LANES), F32)],
        compiler_params=_params("arbitrary"),
    )(x, y, gate, g, target)


def _adam_math(w, g, m, v):
    new_m = ADAM_B1 * m + (1.0 - ADAM_B1) * g
    new_v = ADAM_B2 * v + (1.0 - ADAM_B2) * (g * g)
    m_hat = new_m / (1.0 - ADAM_B1**ADAM_STEP)
    v_hat = new_v / (1.0 - ADAM_B2**ADAM_STEP)
    return -ADAM_LR * (m_hat / (jnp.sqrt(v_hat) + ADAM_EPS) + ADAM_WD * w), new_m, new_v


def _adamw(w, g, m, v, *, name):
    rows, cols = w.shape
    tr = _tile(rows, max(8, (1 << 18) // cols), 8)

    def body(w_ref, g_ref, m_ref, v_ref, go_ref, d_ref, mo_ref, vo_ref):
        gv = g_ref[...]
        go_ref[...] = gv
        d_ref[...], mo_ref[...], vo_ref[...] = _adam_math(w_ref[...], gv, m_ref[...], v_ref[...])

    spec = pl.BlockSpec((tr, cols), lambda i: (i, 0))
    return pl.pallas_call(
        body,
        name=name,
        grid=(rows // tr,),
        in_specs=[spec] * 4,
        out_specs=[spec] * 4,
        out_shape=[jax.ShapeDtypeStruct((rows, cols), F32)] * 4,
        compiler_params=_params("parallel"),
    )(w, g, m, v)


def _tp_adamw(sc16, dm, w, m, v, *, name):
    nl, d, n = w.shape
    tm = _tile(d, 512, LANES)
    tn = _tile(n, 1024, LANES)

    def body(sc_ref, dm_ref, w_ref, m_ref, v_ref, go_ref, d_ref, mo_ref, vo_ref):
        gv = lax.dot_general(sc_ref[...].astype(BF16), dm_ref[0].astype(BF16), (((0,), (0,)), ((), ())), preferred_element_type=F32)
        go_ref[0] = gv
        d_ref[0], mo_ref[0], vo_ref[0] = _adam_math(w_ref[0], gv, m_ref[0], v_ref[0])

    blk = pl.BlockSpec((1, tm, tn), lambda l, i, j: (l, i, j))
    return pl.pallas_call(
        body,
        name=name,
        grid=(nl, d // tm, n // tn),
        in_specs=[pl.BlockSpec((16, tm), lambda l, i, j: (0, i)), pl.BlockSpec((1, 16, tn), lambda l, i, j: (l, 0, j)), blk, blk, blk],
        out_specs=[blk] * 4,
        out_shape=[jax.ShapeDtypeStruct((nl, d, n), F32)] * 4,
        compiler_params=_params("parallel", "parallel", "parallel"),
    )(sc16, dm, w, m, v)


def _sum_devices(x, *, name):
    def body(x_ref, o_ref):
        acc = x_ref[0]
        for k in range(1, N_DEV):
            acc = acc + x_ref[k]
        o_ref[...] = acc

    return pl.pallas_call(body, name=name, out_shape=jax.ShapeDtypeStruct(x.shape[1:], F32))(x)


def _place():
    mx, my, mc = lax.axis_index("x"), lax.axis_index("y"), lax.axis_index("c")
    chips = [(1 - mx, my), (mx, 1 - my), (1 - mx, 1 - my)]
    return mx, my, mc, chips


def _remote(src, dst, send_sem, recv_sem, device):
    return pltpu.make_async_remote_copy(src_ref=src, dst_ref=dst, send_sem=send_sem, recv_sem=recv_sem, device_id=device, device_id_type=MESH)


def _allgather8(x, *, name):
    def body(x_ref, out_ref, send_sems, recv_sems, local_sem):
        mx, my, mc, chips = _place()
        me, sibling = (mx, my, mc), (mx, my, 1 - mc)

        def slot(px, py, pc):
            return out_ref.at[4 * px + 2 * py + pc]

        def copy(k, block, to, src=None):
            return _remote(slot(*block) if src is None else src, slot(*block), send_sems.at[k], recv_sems.at[k], to)

        mine = pltpu.make_async_copy(x_ref, slot(*me), local_sem)
        mine.start()
        first = [copy(0, me, sibling, src=x_ref)] + [copy(1 + j, me, (*chip, mc), src=x_ref) for j, chip in enumerate(chips)]
        for cp in first:
            cp.start()
        passed = [copy(4 + j, (*chip, mc), sibling) for j, chip in enumerate(chips)]
        for j, chip in enumerate(chips):
            copy(1 + j, (*chip, mc), me).wait_recv()
            passed[j].start()
        copy(0, sibling, me).wait_recv()
        for j, chip in enumerate(chips):
            copy(4 + j, (*chip, 1 - mc), me).wait_recv()
        for cp in first + passed:
            cp.wait_send()
        mine.wait()

    return pl.pallas_call(
        body,
        name=name,
        out_shape=jax.ShapeDtypeStruct((N_DEV,) + x.shape, x.dtype),
        in_specs=[pl.BlockSpec(memory_space=pltpu.VMEM)],
        out_specs=pl.BlockSpec(memory_space=pltpu.VMEM),
        scratch_shapes=[pltpu.SemaphoreType.DMA((7,)), pltpu.SemaphoreType.DMA((7,)), pltpu.SemaphoreType.DMA],
    )(x)


class _Geom:
    def __init__(self, shape3, axis):
        self.shape3, self.axis = shape3, axis
        nl, r, c = shape3
        self.rs, self.cs = (r // N_CHIPS, c) if axis == 1 else (r, c // N_CHIPS)
        self.hl, self.hr = (nl // 2, self.rs) if nl > 1 else (1, self.rs // 2)
        self.shard = (nl, self.rs, self.cs)
        self.half = (self.hl, self.hr, self.cs)

    def in_full(self, ref, chip, core):
        nl = self.shape3[0]
        l0 = core * self.hl if nl > 1 else 0
        r0 = (chip * self.rs if self.axis == 1 else 0) + (0 if nl > 1 else core * self.hr)
        c0 = chip * self.cs if self.axis == 2 else 0
        return ref.at[pl.ds(l0, self.hl), pl.ds(r0, self.hr), pl.ds(c0, self.cs)]


def _place_shard(shard, geom, chip_arr, *, name, layer=None):
    nl, rs, cs = geom.shard
    tr = _tile(rs, max(16, (1 << 18) // cs), 16)
    per = rs // tr
    first = 0 if layer is None else layer

    def body(chip_ref, x_ref, o_ref):
        o_ref[...] = x_ref[...].astype(o_ref.dtype)

    def out_map(l, i, chip_ref):
        return (l, chip_ref[0] * per + i, 0) if geom.axis == 1 else (l, i, chip_ref[0])

    return pl.pallas_call(
        body,
        name=name,
        grid_spec=pltpu.PrefetchScalarGridSpec(
            num_scalar_prefetch=1,
            grid=(nl, per),
            in_specs=[pl.BlockSpec((1, tr, cs), lambda l, i, chip_ref: (first + l, i, 0))],
            out_specs=pl.BlockSpec((1, tr, cs), out_map),
        ),
        out_shape=jax.ShapeDtypeStruct(geom.shape3, BF16),
        compiler_params=_params("parallel", "parallel"),
    )(chip_arr, shard)


def _dma_sems(count, arrays):
    return [pltpu.SemaphoreType.DMA((count,))] * arrays


def _gather_plan(fulls, geoms):
    def ici(w, k, src, dst, sems, device):
        return _remote(src, dst, sems[0].at[3 * w + k], sems[1].at[3 * w + k], device)

    def d2d(w, k, box, sems, device):
        return _remote(box, box, sems[2].at[3 * w + k], sems[3].at[3 * w + k], device)

    def start(given, full, sems):
        mx, my, mc, chips = _place()
        me = 2 * mx + my
        for w, geom in enumerate(geoms):
            for k, chip in enumerate(chips):
                ici(w, k, geom.in_full(given[w], me, mc), geom.in_full(full[w], me, mc), sems, (*chip, mc)).start()

    def finish(given, full, sems):
        mx, my, mc, chips = _place()
        me, sibling = 2 * mx + my, (mx, my, 1 - mc)
        for w, geom in enumerate(geoms):
            for k, (px, py) in enumerate(chips):
                landed = geom.in_full(full[w], 2 * px + py, mc)
                ici(w, k, landed, landed, sems, (px, py, mc)).wait_recv()
                d2d(w, k, landed, sems, sibling).start()
        for w, geom in enumerate(geoms):
            for k, (px, py) in enumerate(chips):
                d2d(w, k, geom.in_full(full[w], 2 * px + py, 1 - mc), sems, sibling).wait_recv()
        for w, geom in enumerate(geoms):
            for k, (px, py) in enumerate(chips):
                ici(w, k, geom.in_full(given[w], me, mc), geom.in_full(full[w], me, mc), sems, (px, py, mc)).wait_send()
                d2d(w, k, geom.in_full(full[w], 2 * px + py, mc), sems, sibling).wait_send()

    n = len(fulls)
    shapes = [jax.ShapeDtypeStruct(f.shape, f.dtype) for f in fulls]
    return _Hosted(fulls, shapes, {w: w for w in range(n)}, _dma_sems(3 * n, 4), start, finish)


def _pair_plan(grads, geoms):
    def copies(grad, theirs, sems):
        mx, my, mc, _ = _place()
        return [
            _remote(geom.in_full(grad[w], chip, 1 - mc), theirs[w].at[chip], sems[0].at[4 * w + chip], sems[1].at[4 * w + chip], (mx, my, 1 - mc))
            for w, geom in enumerate(geoms)
            for chip in range(N_CHIPS)
        ]

    def start(grad, theirs, sems):
        for cp in copies(grad, theirs, sems):
            cp.start()

    def finish(grad, theirs, sems):
        for cp in copies(grad, theirs, sems):
            cp.wait_recv()
        for cp in copies(grad, theirs, sems):
            cp.wait_send()

    shapes = [jax.ShapeDtypeStruct((N_CHIPS,) + g.half, x.dtype) for g, x in zip(geoms, grads)]
    return _Hosted(grads, shapes, {}, _dma_sems(4 * len(grads), 2), start, finish)


def _half_tile(geom):
    return _tile(geom.hr, max(16, (1 << 18) // geom.cs), 16)


def _pair_add(grad, theirs, geom, core_arr, *, name):
    hl, hr, cs = geom.half
    tr = _half_tile(geom)
    stacked = geom.shape3[0] > 1

    def grad_map(chip, l, i, core_ref):
        layer = core_ref[0] * hl + l if stacked else 0
        row = (chip * (geom.rs // tr) if geom.axis == 1 else 0) + (0 if stacked else core_ref[0] * (hr // tr)) + i
        return layer, row, (chip if geom.axis == 2 else 0)

    def body(core_ref, g_ref, t_ref, o_ref):
        o_ref[0] = (g_ref[...].astype(F32) + t_ref[0].astype(F32)).astype(o_ref.dtype)

    blk = pl.BlockSpec((1, 1, tr, cs), lambda chip, l, i, core_ref: (chip, l, i, 0))
    return pl.pallas_call(
        body,
        name=name,
        grid_spec=pltpu.PrefetchScalarGridSpec(
            num_scalar_prefetch=1, grid=(N_CHIPS, hl, hr // tr), in_specs=[pl.BlockSpec((1, tr, cs), grad_map), blk], out_specs=blk
        ),
        out_shape=jax.ShapeDtypeStruct(theirs.shape, BF16),
        compiler_params=_params("parallel", "parallel", "parallel"),
    )(core_arr, grad, theirs)


def _chips_plan(parts):
    def copies(part, slots, sems):
        _, _, mc, chips = _place()
        return [
            _remote(part[w].at[2 * px + py], slots[w].at[k], sems[0].at[3 * w + k], sems[1].at[3 * w + k], (px, py, mc))
            for w in range(len(parts))
            for k, (px, py) in enumerate(chips)
        ]

    def start(part, slots, sems):
        for cp in copies(part, slots, sems):
            cp.start()

    def finish(part, slots, sems):
        for cp in copies(part, slots, sems):
            cp.wait_recv()
        for cp in copies(part, slots, sems):
            cp.wait_send()

    shapes = [jax.ShapeDtypeStruct((N_CHIPS - 1,) + p.shape[1:], p.dtype) for p in parts]
    return _Hosted(parts, shapes, {}, _dma_sems(3 * len(parts), 2), start, finish)


def _chip_sum(part, slots, geom, place_arr, *, name, stack=None):
    hl, hr, cs = geom.half
    tr = _half_tile(geom)

    def body(place_ref, own_ref, s0_ref, s1_ref, s2_ref, *rest):
        o_ref = rest[-1]
        o_ref[...] = ((own_ref[...].astype(F32) + s0_ref[...].astype(F32)) + s1_ref[...].astype(F32)) + s2_ref[...].astype(F32)

    def slot(k):
        return pl.BlockSpec((1, 1, tr, cs), lambda l, i, place_ref: (k, l, i, 0))

    in_specs = [pl.BlockSpec((1, 1, tr, cs), lambda l, i, place_ref: (place_ref[0], l, i, 0)), slot(0), slot(1), slot(2)]
    args = [place_arr, part, slots, slots, slots]
    aliases = {}
    if stack is None:
        out_spec = pl.BlockSpec((1, 1, tr, cs), lambda l, i, place_ref: (place_ref[1], l, i, 0))
        out_shape = jax.ShapeDtypeStruct((2,) + geom.half, F32)
    else:
        layers, layer, prev = stack
        assert hl == 1
        out_spec = pl.BlockSpec((1, 1, tr, cs), lambda l, i, place_ref: (layer, place_ref[1], i, 0))
        out_shape = jax.ShapeDtypeStruct((layers, 2, hr, cs), F32)
        if prev is not None:
            aliases = {len(args): 0}
            in_specs.append(ANY)
            args.append(prev)
    return pl.pallas_call(
        body,
        name=name,
        grid_spec=pltpu.PrefetchScalarGridSpec(num_scalar_prefetch=1, grid=(hl, hr // tr), in_specs=in_specs, out_specs=out_spec),
        out_shape=out_shape,
        input_output_aliases=aliases,
        compiler_params=_params("parallel", "parallel"),
    )(*args)


def _join_plan(boths, prefixes):
    def copies(given, both, sems):
        mx, my, mc, _ = _place()
        out, n = [], 0
        for w in range(len(boths)):
            for p in prefixes[w]:
                out.append(_remote(given[w].at[(*p, mc)], both[w].at[(*p, mc)], sems[0].at[n], sems[1].at[n], (mx, my, 1 - mc)))
                n += 1
        return out

    def arrivals(both, sems):
        mx, my, mc, _ = _place()
        out, n = [], 0
        for w in range(len(boths)):
            for p in prefixes[w]:
                got = both[w].at[(*p, 1 - mc)]
                out.append(_remote(got, got, sems[0].at[n], sems[1].at[n], (mx, my, 1 - mc)))
                n += 1
        return out

    def start(given, both, sems):
        for cp in copies(given, both, sems):
            cp.start()

    def finish(given, both, sems):
        for cp in arrivals(both, sems):
            cp.wait_recv()
        for cp in copies(given, both, sems):
            cp.wait_send()

    count = sum(len(p) for p in prefixes)
    shapes = [jax.ShapeDtypeStruct(b.shape, b.dtype) for b in boths]
    return _Hosted(boths, shapes, {w: w for w in range(len(boths))}, _dma_sems(count, 2), start, finish)


WEIGHTS = ("mod_w", "mod_b", "norm_mix", "norm_ffn", "pool_w", "pool_scale", "kv_mod_w", "kv_mod_b", "kv_in_norm", "w_dkv", "kv_norm",
           "w_uk", "w_uv", "w_kr", "w_dq", "q_norm", "w_uq", "w_o", "ffn_gate", "ffn_up", "ffn_down", "final_norm")
SMALL = ("mod_b", "kv_mod_b", "norm_mix", "norm_ffn", "kv_in_norm", "kv_norm", "q_norm", "final_norm")


def _rows(v):
    return v.reshape(-1, LANES)


def _pad_rows(a):
    return jnp.pad(a, ((0, (-a.shape[0]) % 8), (0, 0)))


def _vec(v):
    return v.reshape(1, -1)


def _step(x, c, positions, target, wts, mom, var):
    _, s, d = x.shape
    depth, n_a, n_b = wts["mod_w"].shape[0], wts["pool_w"].shape[0], wts["w_dq"].shape[0]
    assert n_b == 2 and n_a + n_b == depth
    heads = d // V_DIM
    kvr, qr = wts["w_dkv"].shape[1], wts["w_dq"].shape[2]
    ffn = wts["ffn_gate"].shape[2] * N_CHIPS
    pool_c = d // len(POOL_WINDOWS)
    nmod, nkv = N_MOD * d, 2 * d
    mx, my, mc = lax.axis_index("x"), lax.axis_index("y"), lax.axis_index("c")
    chip, dev = 2 * mx + my, 4 * mx + 2 * my + mc
    xs, tgt = x[0], target[0]

    inv_freq = 1.0 / (ROPE_THETA ** (jnp.arange(0, ROPE_DIM, 2, dtype=F32) / ROPE_DIM))
    ang = positions[0].astype(F32)[:, None] * inv_freq
    cos, sin, zero = jnp.cos(ang), jnp.sin(ang), jnp.zeros((s, LANES - ROPE_DIM), F32)
    cos_t = jnp.concatenate([cos, cos, zero], axis=1)
    sin_fwd = jnp.concatenate([-sin, sin, zero], axis=1)
    sin_bwd = jnp.concatenate([sin, -sin, zero], axis=1)

    c_rows, ps_rows = d // LANES, n_a * (d // N_CHIPS) // LANES
    cond = _allgather8(_pad_rows(jnp.concatenate([_rows(c), _rows(wts["pool_scale"])])), name="gather_cond")
    c_all = cond[:, :c_rows].reshape(N_DEV, d)
    pool_scale = cond[0::2, c_rows : c_rows + ps_rows].reshape(N_CHIPS, n_a, d // N_CHIPS).transpose(1, 0, 2).reshape(n_a, d)
    sc16 = _elementwise(_silu, [jnp.pad(c_all, ((0, 16 - N_DEV), (0, 0)))], [F32], name="silu_cond")[0]

    mod_bias = lax.dynamic_slice_in_dim(wts["mod_b"], chip * (nmod // N_CHIPS), nmod // N_CHIPS, axis=1)[:, None, :]
    kv_bias = lax.dynamic_slice_in_dim(wts["kv_mod_b"], chip * (nkv // N_CHIPS), nkv // N_CHIPS).reshape(1, 1, -1)
    mod_part = _tp_fwd(sc16, wts["mod_w"], mod_bias, name="mod_fwd")
    kv_part = _tp_fwd(sc16, wts["kv_mod_w"][None], kv_bias, name="kv_mod_fwd")
    part = jnp.concatenate([mod_part[i, :N_DEV] for i in range(depth)] + [kv_part[0, :N_DEV]], axis=1)
    ncol = part.shape[1]
    gathered = _allgather8(_pad_rows(_rows(part)), name="gather_mods")
    gathered = gathered[0::2, : N_DEV * ncol // LANES].reshape(N_CHIPS, N_DEV, ncol)
    mine = lax.dynamic_index_in_dim(gathered, dev, axis=1, keepdims=False)
    per = nmod // N_CHIPS
    mods = [mine[:, i * per : (i + 1) * per].reshape(N_MOD, 1, d) for i in range(depth)]
    kv_shift, kv_scale = mine[:, depth * per :].reshape(2, 1, d)

    mixer_names = ("pool_w", "w_dkv", "w_uk", "w_uv", "w_kr", "w_dq", "w_uq", "w_o")
    ffn_names = ("ffn_gate", "ffn_up", "ffn_down")
    geoms = {
        "pool_w": _Geom((n_a * len(POOL_WINDOWS), pool_c, pool_c), 1),
        "w_dkv": _Geom((1, d, kvr), 1),
        "w_uk": _Geom((1, kvr, heads * NOPE_DIM), 2),
        "w_uv": _Geom((1, kvr, heads * V_DIM), 2),
        "w_kr": _Geom((1, d, ROPE_DIM), 1),
        "w_dq": _Geom((n_b, d, qr), 1),
        "w_uq": _Geom((n_b, qr, heads * (NOPE_DIM + ROPE_DIM)), 2),
        "w_o": _Geom((n_b, d, d), 1),
        "ffn_gate": _Geom((1, d, ffn), 2),
        "ffn_up": _Geom((1, d, ffn), 2),
        "ffn_down": _Geom((1, ffn, d), 1),
    }
    mixer_geoms = [geoms[n] for n in mixer_names]
    ffn_geoms = [geoms[n] for n in ffn_names]
    chip_arr, core_arr, place_arr = chip.reshape(1), mc.reshape(1), jnp.stack([chip, mc])
    placed = [_place_shard(wts[n].reshape(geoms[n].shard), geoms[n], chip_arr, name="place_" + n) for n in mixer_names]
    placed_ffn = [[_place_shard(wts[n], geoms[n], chip_arr, layer=i, name="place_" + n) for n in ffn_names] for i in range(depth)]
    first = _run(_gather_plan(placed + placed_ffn[0], mixer_geoms + ffn_geoms), name="gather_first")
    full = dict(zip(mixer_names, first))
    ffn_w = [None] * depth
    ffn_w[0] = first[len(mixer_names) :]

    pool_w = full["pool_w"].reshape(n_a, len(POOL_WINDOWS), pool_c, pool_c)
    w_uq = full["w_uq"].reshape(n_b, qr, heads, NOPE_DIM + ROPE_DIM)
    w_q = jnp.pad(w_uq, ((0, 0), (0, 0), (0, 0), (0, HEAD_PAD - NOPE_DIM - ROPE_DIM))).reshape(n_b, qr, heads * HEAD_PAD)
    w_ukv = jnp.stack([full["w_uk"].reshape(kvr, heads, NOPE_DIM), full["w_uv"].reshape(kvr, heads, V_DIM)], axis=2).reshape(kvr, heads * HEAD_PAD)
    w_dkvkr = jnp.concatenate([full["w_dkv"][0], full["w_kr"][0], jnp.zeros((d, LANES - ROPE_DIM), BF16)], axis=1)
    w_dq, w_o = full["w_dq"], full["w_o"]

    norm_mix, norm_ffn = wts["norm_mix"], wts["norm_ffn"]
    saved = []
    cur, pending = xs, None
    kv_side = None
    for i in range(depth):
        shift_m, scale_m, gate_m, shift_f, scale_f, gate_f = mods[i]
        h1_dtype = F32 if i < n_a else BF16
        if pending is None:
            x0 = cur
            h1 = _norm_fwd(x0, _vec(norm_mix[i]), scale=scale_m, shift=shift_m, out_dtype=h1_dtype, name="norm_mix_first")
        else:
            x0, h1 = _norm_fwd(cur, _vec(norm_mix[i]), scale=scale_m, shift=shift_m, y=pending[0], gate=pending[1], out_dtype=h1_dtype, name="norm_mix")
        lay = {"x0": x0, "h1": h1}
        if i == n_a:
            h_kv = _norm_fwd(x0, _vec(wts["kv_in_norm"]), scale=kv_scale, shift=kv_shift, name="norm_kv_in")
            pre = _mm(h_kv, w_dkvkr, name="kv_down", tn=kvr + LANES)
            ckv_pre, kr_pre = pre[:, :kvr], pre[:, kvr:]
            ckv = _norm_fwd(ckv_pre, _vec(wts["kv_norm"]), name="norm_kv")
            kv = _mm(ckv, w_ukv, out_dtype=BF16, name="kv_up")
            keys = _build_keys(kv, kr_pre, cos_t, sin_fwd, name="build_keys")
            kv_side = {"h_kv": h_kv, "ckv_pre": ckv_pre, "ckv": ckv, "kv": kv, "keys": keys, "x0": x0}
        if i < n_a:
            pooled = _pool(h1, transpose=False, out_dtype=BF16, name="pool_fwd")
            y_pre = _gmm(pooled, pool_w[i], mode="nn", out_dtype=F32, name="pool_mix")
            gate_eff = gate_m * _vec(pool_scale[i])
            lay.update(pooled=pooled)
        else:
            l = i - n_a
            cq_pre = _mm(h1, w_dq[l], name="q_down")
            cq = _norm_fwd(cq_pre, _vec(wts["q_norm"][l]), name="norm_q")
            q = _rope_heads(_mm(cq, w_q[l], name="q_up"), cos_t, sin_fwd, out_dtype=BF16, name="rope_q")
            o, lse = _attn_fwd(q, kv_side["keys"], kv_side["kv"], name="attn_fwd")
            y_pre = _mm(o, w_o[l], name="attn_out")
            gate_eff = gate_m
            lay.update(cq_pre=cq_pre, cq=cq, q=q, o=o, lse=lse)
        x1, h2 = _norm_fwd(x0, _vec(norm_ffn[i]), scale=scale_f, shift=shift_f, y=y_pre, gate=gate_eff, name="norm_ffn")
        w_gate, w_up, w_down = ffn_w[i]
        if i + 1 < depth:
            (a, b, z), next_in = _ffn_in(h2, w_gate, w_up, 0, hosted=_gather_plan(placed_ffn[i + 1][:2], ffn_geoms[:2]), name="ffn_in")
            f, next_down = _mm(z, w_down, b_idx=0, hosted=_gather_plan(placed_ffn[i + 1][2:], ffn_geoms[2:]), name="ffn_down")
            ffn_w[i + 1] = next_in + next_down
        else:
            (a, b, z), _ = _ffn_in(h2, w_gate, w_up, 0, name="ffn_in_last")
            f = _mm(z, w_down, b_idx=0, name="ffn_down_last")
        lay.update(y_pre=y_pre, gate_eff=gate_eff, x1=x1, h2=h2, a=a, b=b, z=z, f=f)
        saved.append(lay)
        cur, pending = x1, (f, gate_f)

    dx, final_stats, loss_tile = _loss_bwd(cur, pending[0], pending[1], _vec(wts["final_norm"]), tgt, name="loss")
    loss = lax.psum(loss_tile[0, 0], ("x", "y", "c"))

    ffn_both = [None] * len(ffn_names)
    in_flight = None

    def sum_chips(layer, parts, slots):
        for w, n in enumerate(ffn_names):
            ffn_both[w] = _chip_sum(parts[w], slots[w], ffn_geoms[w], place_arr, stack=(depth, layer, ffn_both[w]), name="chip_sum_" + n)

    g_wo, g_wq, g_wdq = [None] * n_b, [None] * n_b, [None] * n_b
    g_pool = [None] * n_a
    dmods = [None] * depth
    g_norm_mix, g_norm_ffn, g_q_norm, g_pool_scale = [None] * depth, [None] * depth, [None] * n_b, [None] * n_a
    dk_layers, dv_layers = [None] * n_b, [None] * n_b
    for i in reversed(range(depth)):
        lay = saved[i]
        shift_m, scale_m, gate_m, shift_f, scale_f, gate_f = mods[i]
        df, sums_gf = _gate_bwd(dx, lay["f"], gate_f, name="gate_bwd")
        w_gate, w_up, w_down = ffn_w[i]
        g_down = _mm(lay["z"], df, ta=True, out_dtype=BF16, name="ffn_down_dw")
        if in_flight is None:
            (da, db), _ = _ffn_down_bwd(df, w_down, 0, lay["a"], lay["b"], name="ffn_down_bwd_top")
            (dh2,), _ = _ffn_in_dx(da, db, w_gate, w_up, 0, name="ffn_in_dx_top")
            (g_gate, g_up), _ = _ffn_in_dw(lay["h2"], da, db, name="ffn_in_dw_top")
        else:
            (da, db), got_down = _ffn_down_bwd(df, w_down, 0, lay["a"], lay["b"], hosted=_chips_plan(in_flight[2:]), name="ffn_down_bwd")
            (dh2,), got_gate = _ffn_in_dx(da, db, w_gate, w_up, 0, hosted=_chips_plan(in_flight[:1]), name="ffn_in_dx")
            (g_gate, g_up), got_up = _ffn_in_dw(lay["h2"], da, db, hosted=_chips_plan(in_flight[1:2]), name="ffn_in_dw")
            sum_chips(i + 1, in_flight, got_gate + got_up + got_down)
        ffn_grads = [g_gate[None], g_up[None], g_down[None]]
        theirs = _run(_pair_plan(ffn_grads, ffn_geoms), name="reduce_pair_ffn")
        in_flight = [_pair_add(g, th, geom, core_arr, name="pair_add_" + n) for g, th, geom, n in zip(ffn_grads, theirs, ffn_geoms, ffn_names)]
        dx1, sums_f = _norm_bwd(lay["x1"], _vec(norm_ffn[i]), dh2, scale=scale_f, resid=dx, name="norm_ffn_bwd")
        dyp, sums_gm = _gate_bwd(dx1, lay["y_pre"], lay["gate_eff"], name="gate_bwd")
        if i < n_a:
            g_pool[i] = _gmm(lay["pooled"], dyp, mode="tn", out_dtype=BF16, name="pool_mix_dw")
            dd = _gmm(dyp, pool_w[i], mode="nt", out_dtype=F32, name="pool_mix_dx")
            dh1 = _pool(dd, transpose=True, out_dtype=F32, name="pool_bwd")
            dgate_m = sums_gm[0] * pool_scale[i]
            g_pool_scale[i] = sums_gm[0] * gate_m[0]
        else:
            l = i - n_a
            do = _mm(dyp, w_o[l], tb=True, out_dtype=BF16, name="attn_out_dx")
            g_wo[l] = _mm(lay["o"], dyp, ta=True, out_dtype=BF16, name="attn_out_dw")
            dq, delta = _attn_bwd_q(lay["q"], kv_side["keys"], kv_side["kv"], lay["o"], do, lay["lse"], name="attn_bwd_q")
            dk_layers[l], dv_layers[l] = _attn_bwd_kv(
                lay["q"], kv_side["keys"], kv_side["kv"], do, lay["lse"].reshape(heads, -1, 1, _attn_tile(s)), delta.reshape(heads, -1, 1, _attn_tile(s)), name="attn_bwd_kv"
            )
            dq_pre = _rope_heads(dq, cos_t, sin_bwd, out_dtype=BF16, name="rope_q_bwd")
            dcq = _mm(dq_pre, w_q[l], tb=True, name="q_up_dx")
            g_wq[l] = _mm(lay["cq"], dq_pre, ta=True, out_dtype=BF16, name="q_up_dw")
            dcq_pre, sums_q = _norm_bwd(lay["cq_pre"], _vec(wts["q_norm"][l]), dcq, name="norm_q_bwd")
            g_q_norm[l] = sums_q[2]
            dh1 = _mm(dcq_pre, w_dq[l], tb=True, name="q_down_dx")
            g_wdq[l] = _mm(lay["h1"], dcq_pre, ta=True, out_dtype=BF16, name="q_down_dw")
            dgate_m = sums_gm[0]
        dx, sums_m = _norm_bwd(lay["x0"], _vec(norm_mix[i]), dh1, scale=scale_m, resid=dx1, name="norm_mix_bwd")
        if i == n_a:
            dkv, dkr_pre = _keys_bwd(dk_layers[0], dk_layers[1], dv_layers[0], dv_layers[1], cos_t, sin_bwd, name="keys_bwd")
            dckv = _mm(dkv, w_ukv, tb=True, name="kv_up_dx")
            g_ukv = _mm(kv_side["ckv"], dkv, ta=True, out_dtype=BF16, name="kv_up_dw")
            dckv_pre, sums_kvn = _norm_bwd(kv_side["ckv_pre"], _vec(wts["kv_norm"]), dckv, name="norm_kv_bwd")
            dpre = jnp.concatenate([dckv_pre, dkr_pre], axis=1)
            dh_kv = _mm(dpre, w_dkvkr, tb=True, name="kv_down_dx")
            g_dkvkr = _mm(kv_side["h_kv"], dpre, ta=True, out_dtype=BF16, name="kv_down_dw", tn=kvr + LANES)
            dx, sums_kv = _norm_bwd(lay["x0"], _vec(wts["kv_in_norm"]), dh_kv, scale=kv_scale, resid=dx, name="norm_kv_in_bwd")
        dmods[i] = jnp.concatenate([sums_m[0], sums_m[1], dgate_m, sums_f[0], sums_f[1], sums_gf[0]])
        g_norm_mix[i], g_norm_ffn[i] = sums_m[2], sums_f[2]
    grad_x = dx[None]

    g_ukv = g_ukv.reshape(kvr, heads, 2, NOPE_DIM)
    grad_full = {
        "pool_w": jnp.stack(g_pool).reshape(geoms["pool_w"].shape3),
        "w_dkv": g_dkvkr[None, :, :kvr],
        "w_uk": g_ukv[:, :, 0].reshape(1, kvr, heads * NOPE_DIM),
        "w_uv": g_ukv[:, :, 1].reshape(1, kvr, heads * V_DIM),
        "w_kr": g_dkvkr[None, :, kvr : kvr + ROPE_DIM],
        "w_dq": jnp.stack(g_wdq),
        "w_uq": jnp.stack(g_wq).reshape(n_b, qr, heads, HEAD_PAD)[..., : NOPE_DIM + ROPE_DIM].reshape(geoms["w_uq"].shape3),
        "w_o": jnp.stack(g_wo),
    }

    small_grads = {
        "mod_b": jnp.concatenate(dmods),
        "kv_mod_b": jnp.concatenate([sums_kv[0], sums_kv[1]]),
        "norm_mix": jnp.concatenate(g_norm_mix),
        "norm_ffn": jnp.concatenate(g_norm_ffn),
        "kv_in_norm": sums_kv[2],
        "kv_norm": sums_kvn[2],
        "q_norm": jnp.concatenate(g_q_norm),
        "final_norm": final_stats[0],
    }
    packed = jnp.concatenate([small_grads[n] for n in SMALL] + g_pool_scale)
    small_rows = sum(wts[n].size for n in SMALL) // LANES
    every = _allgather8(_pad_rows(_rows(packed)), name="gather_small_grads")
    summed = _sum_devices(every, name="sum_small_grads")

    mod_rows = depth * nmod // LANES
    dm_all = every[:, :mod_rows].reshape(N_DEV, depth, nmod)
    dm = lax.dynamic_slice_in_dim(dm_all, chip * per, per, axis=2).transpose(1, 0, 2)
    dm = jnp.pad(dm, ((0, 0), (0, 16 - N_DEV), (0, 0)))
    dkvm_all = every[:, mod_rows : mod_rows + nkv // LANES].reshape(N_DEV, nkv)
    dkvm = jnp.pad(lax.dynamic_slice_in_dim(dkvm_all, chip * (nkv // N_CHIPS), nkv // N_CHIPS, axis=1), ((0, 16 - N_DEV), (0, 0)))[None]
    results = {}
    results["mod_w"] = _tp_adamw(sc16, dm, wts["mod_w"], mom["mod_w"], var["mod_w"], name="mod_w_update")
    results["kv_mod_w"] = [
        r[0] for r in _tp_adamw(sc16, dkvm, wts["kv_mod_w"][None], mom["kv_mod_w"][None], var["kv_mod_w"][None], name="kv_mod_w_update")
    ]

    ps_grad = lax.dynamic_slice_in_dim(summed[small_rows : small_rows + n_a * d // LANES].reshape(n_a, d), chip * (d // N_CHIPS), d // N_CHIPS, axis=1)
    small_names = SMALL + ("pool_scale",)

    def pack_small(tree):
        return _pad_rows(jnp.concatenate([_rows(tree[n]) for n in small_names]))

    g_small = _pad_rows(jnp.concatenate([summed[:small_rows], _rows(ps_grad)]))
    small_out = _adamw(pack_small(wts), g_small, pack_small(mom), pack_small(var), name="small_update")
    row = 0
    for n in small_names:
        nrow = wts[n].size // LANES
        results[n] = [r[row : row + nrow].reshape(wts[n].shape) for r in small_out]
        row += nrow

    theirs = _run(_pair_plan([grad_full[n] for n in mixer_names], mixer_geoms), name="reduce_pair")
    pair_sum = [_pair_add(grad_full[n], th, geoms[n], core_arr, name="pair_add_" + n) for n, th in zip(mixer_names, theirs)]
    slots = _run(_chips_plan(pair_sum + in_flight), name="reduce_chips")
    boths = [_chip_sum(ps, sl, geoms[n], place_arr, name="chip_sum_" + n) for n, ps, sl in zip(mixer_names, pair_sum, slots)]
    sum_chips(0, in_flight, slots[len(mixer_names) :])
    prefixes = [[()]] * len(mixer_names) + [[(layer,) for layer in range(depth)]] * len(ffn_names)
    joined = _run(_join_plan(boths + ffn_both, prefixes), name="join_pair")
    for n, both in zip(mixer_names + ffn_names, joined):
        cs = geoms[n].cs
        out = _adamw(wts[n].reshape(-1, cs), both.reshape(-1, cs), mom[n].reshape(-1, cs), var[n].reshape(-1, cs), name="update_" + n)
        results[n] = [r.reshape(wts[n].shape) for r in out]

    outs = [loss, grad_x]
    for k in range(4):
        outs += [results[n][k] for n in WEIGHTS]
    return tuple(outs)


def kernel(x, c, positions, mod_w, mod_b, norm_mix, norm_ffn, pool_w, pool_scale, kv_mod_w, kv_mod_b, kv_in_norm, w_dkv, kv_norm, w_uk, w_uv, w_kr, w_dq, q_norm, w_uq, w_o, ffn_gate, ffn_up, ffn_down, final_norm, loss_target, m_mod_w, m_mod_b, m_norm_mix, m_norm_ffn, m_pool_w, m_pool_scale, m_kv_mod_w, m_kv_mod_b, m_kv_in_norm, m_w_dkv, m_kv_norm, m_w_uk, m_w_uv, m_w_kr, m_w_dq, m_q_norm, m_w_uq, m_w_o, m_ffn_gate, m_ffn_up, m_ffn_down, m_final_norm, v_mod_w, v_mod_b, v_norm_mix, v_norm_ffn, v_pool_w, v_pool_scale, v_kv_mod_w, v_kv_mod_b, v_kv_in_norm, v_w_dkv, v_kv_norm, v_w_uk, v_w_uv, v_w_kr, v_w_dq, v_q_norm, v_w_uq, v_w_o, v_ffn_gate, v_ffn_up, v_ffn_down, v_final_norm):
    given = dict(locals())
    wts = {n: given[n] for n in WEIGHTS}
    mom = {n: given["m_" + n] for n in WEIGHTS}
    var = {n: given["v_" + n] for n in WEIGHTS}
    return _step(x, c, positions, loss_target, wts, mom, var)
```

```python
import functools

import jax
import jax.numpy as jnp
from jax import lax
from jax.experimental import pallas as pl
from jax.experimental.pallas import tpu as pltpu

F32 = jnp.float32
BF16 = jnp.bfloat16
MESH = pl.DeviceIdType.MESH
ANY = pl.BlockSpec(memory_space=pl.ANY)

NORM_EPS = 1e-6
POOL_WINDOWS = (2, 4, 8, 16)
NOPE_DIM = 128
ROPE_DIM = 64
V_DIM = 128
HEAD_PAD = 256
SM_SCALE = (NOPE_DIM + ROPE_DIM) ** -0.5
ROPE_THETA = 10000.0
N_MOD = 6
ADAM_LR, ADAM_B1, ADAM_B2, ADAM_EPS, ADAM_WD, ADAM_STEP = 0.001, 0.9, 0.999, 1e-08, 0.01, 10
N_CHIPS = 4
N_DEV = 8
LANES = 128
HALO = 128
VMEM_LIMIT = 48 * 1024 * 1024


def _tile(dim, pref, align):
    if dim <= pref:
        return dim
    t = (pref // align) * align
    while t >= align:
        if dim % t == 0:
            return t
        t -= align
    return dim


def _params(*sem):
    return pltpu.CompilerParams(dimension_semantics=sem, vmem_limit_bytes=VMEM_LIMIT)


class _Hosted:
    def __init__(self, args, out_shapes, aliases, sem_shapes, start, finish):
        self.args, self.out_shapes, self.aliases, self.sem_shapes = list(args), list(out_shapes), dict(aliases), list(sem_shapes)
        self.start, self.finish = start, finish


def _call(body, *, name, grid, in_specs, out_specs, out_shape, args, sem, scratch_shapes=(), hosted=None):
    n_in, n_out, n_scr = len(args), len(out_shape), len(scratch_shapes)
    if hosted is None:
        outs = pl.pallas_call(
            body, name=name, grid=grid, in_specs=list(in_specs), out_specs=list(out_specs), out_shape=list(out_shape),
            scratch_shapes=list(scratch_shapes), compiler_params=_params(*sem),
        )(*args)
        return list(outs), []
    n_hin, n_hout = len(hosted.args), len(hosted.out_shapes)

    def carrying(*refs):
        own_in, their_in = refs[:n_in], refs[n_in : n_in + n_hin]
        refs = refs[n_in + n_hin :]
        own_out, their_out = refs[:n_out], refs[n_out : n_out + n_hout]
        refs = refs[n_out + n_hout :]
        own_scratch, sems = refs[:n_scr], refs[n_scr:]
        ids = [pl.program_id(axis) for axis in range(len(grid))]
        first = functools.reduce(jnp.logical_and, [i == 0 for i in ids])
        last = functools.reduce(jnp.logical_and, [i == size - 1 for i, size in zip(ids, grid)])

        @pl.when(first)
        def _():
            hosted.start(their_in, their_out, sems)

        body(*own_in, *own_out, *own_scratch)

        @pl.when(last)
        def _():
            hosted.finish(their_in, their_out, sems)

    outs = pl.pallas_call(
        carrying,
        name=name,
        grid=grid,
        in_specs=list(in_specs) + [ANY] * n_hin,
        out_specs=list(out_specs) + [ANY] * n_hout,
        out_shape=list(out_shape) + hosted.out_shapes,
        input_output_aliases={n_in + i: n_out + o for i, o in hosted.aliases.items()},
        scratch_shapes=list(scratch_shapes) + hosted.sem_shapes,
        compiler_params=_params(*["arbitrary"] * len(grid)),
    )(*args, *hosted.args)
    return list(outs[:n_out]), list(outs[n_out:])


def _run(plan, *, name):
    n_in, n_out = len(plan.args), len(plan.out_shapes)

    def body(*refs):
        ins, outs, sems = refs[:n_in], refs[n_in : n_in + n_out], refs[n_in + n_out :]
        plan.start(ins, outs, sems)
        plan.finish(ins, outs, sems)

    return pl.pallas_call(
        body, name=name, in_specs=[ANY] * n_in, out_specs=[ANY] * n_out, out_shape=plan.out_shapes,
        input_output_aliases=plan.aliases, scratch_shapes=plan.sem_shapes,
    )(*plan.args)


def _mm(a, b, *, name, ta=False, tb=False, out_dtype=F32, b_idx=None, hosted=None, tm=1024, tn=1024, tk=None):
    m, k = (a.shape[1], a.shape[0]) if ta else a.shape
    b2 = b.shape if b_idx is None else b.shape[1:]
    kb, n = (b2[1], b2[0]) if tb else b2
    assert k == kb, (a.shape, b.shape, ta, tb)
    tm = _tile(m, tm, LANES)
    tn = _tile(n, tn, LANES)
    tk = _tile(k, 2048 if tk is None else tk, LANES)
    nk = k // tk
    dims = (((0 if ta else 1,), (1 if tb else 0,)), ((), ()))

    def body(a_ref, b_ref, o_ref, *acc):
        part = lax.dot_general(a_ref[...].astype(BF16), b_ref[...].astype(BF16), dims, preferred_element_type=F32)
        if nk == 1:
            o_ref[...] = part.astype(o_ref.dtype)
        else:
            acc_ref = acc[0]
            step = pl.program_id(2)

            @pl.when(step == 0)
            def _():
                acc_ref[...] = part

            @pl.when(step > 0)
            def _():
                acc_ref[...] += part

            @pl.when(step == nk - 1)
            def _():
                o_ref[...] = acc_ref[...].astype(o_ref.dtype)

    a_spec = pl.BlockSpec((tk, tm), lambda i, j, s: (s, i)) if ta else pl.BlockSpec((tm, tk), lambda i, j, s: (i, s))
    if b_idx is None:
        b_spec = pl.BlockSpec((tn, tk), lambda i, j, s: (j, s)) if tb else pl.BlockSpec((tk, tn), lambda i, j, s: (s, j))
    elif tb:
        b_spec = pl.BlockSpec((None, tn, tk), lambda i, j, s: (b_idx, j, s))
    else:
        b_spec = pl.BlockSpec((None, tk, tn), lambda i, j, s: (b_idx, s, j))
    outs, carried = _call(
        body,
        name=name,
        grid=(m // tm, n // tn, nk),
        in_specs=[a_spec, b_spec],
        out_specs=[pl.BlockSpec((tm, tn), lambda i, j, s: (i, j))],
        out_shape=[jax.ShapeDtypeStruct((m, n), out_dtype)],
        args=[a, b],
        sem=("parallel", "parallel", "arbitrary"),
        scratch_shapes=[pltpu.VMEM((tm, tn), F32)] if nk > 1 else [],
        hosted=hosted,
    )
    return outs[0] if hosted is None else (outs[0], carried)


def _tp_fwd(sc16, w, bias, *, name):
    nl, d, n = w.shape
    tn = _tile(n, 512, LANES)

    def body(sc_ref, w_ref, b_ref, o_ref):
        o_ref[0] = jnp.dot(sc_ref[...].astype(BF16), w_ref[0].astype(BF16), preferred_element_type=F32) + b_ref[0]

    return pl.pallas_call(
        body,
        name=name,
        grid=(nl, n // tn),
        in_specs=[
            pl.BlockSpec((16, d), lambda l, j: (0, 0)),
            pl.BlockSpec((1, d, tn), lambda l, j: (l, 0, j)),
            pl.BlockSpec((1, 1, tn), lambda l, j: (l, 0, j)),
        ],
        out_specs=pl.BlockSpec((1, 16, tn), lambda l, j: (l, 0, j)),
        out_shape=jax.ShapeDtypeStruct((nl, 16, n), F32),
        compiler_params=_params("parallel", "parallel"),
    )(sc16, w, bias)


_NT = (((1,), (1,)), ((), ()))
_TN = (((0,), (0,)), ((), ()))


def _silu_parts(a):
    sig = jax.nn.sigmoid(a)
    return a * sig, sig * (1.0 + a * (1.0 - sig))


def _ffn_in(h, w_gate, w_up, layer, *, name, hosted=None):
    s, d = h.shape
    f = w_gate.shape[2]
    tm, tn = _tile(s, 1024, LANES), _tile(f, 512, LANES)

    def body(h_ref, g_ref, u_ref, a_ref, b_ref, z_ref):
        hv = h_ref[...]
        a = jnp.dot(hv, g_ref[...], preferred_element_type=F32)
        b = jnp.dot(hv, u_ref[...], preferred_element_type=F32)
        a_ref[...] = a.astype(a_ref.dtype)
        b_ref[...] = b.astype(b_ref.dtype)
        z_ref[...] = (_silu_parts(a)[0] * b).astype(z_ref.dtype)

    w_spec = pl.BlockSpec((None, d, tn), lambda i, j: (layer, 0, j))
    out = pl.BlockSpec((tm, tn), lambda i, j: (i, j))
    return _call(
        body,
        name=name,
        grid=(s // tm, f // tn),
        in_specs=[pl.BlockSpec((tm, d), lambda i, j: (i, 0)), w_spec, w_spec],
        out_specs=[out] * 3,
        out_shape=[jax.ShapeDtypeStruct((s, f), BF16)] * 3,
        args=[h, w_gate, w_up],
        sem=("parallel", "parallel"),
        hosted=hosted,
    )


def _ffn_down_bwd(df, w_down, layer, a, b, *, name, hosted=None):
    s, d = df.shape
    f = w_down.shape[1]
    tm, tn = _tile(s, 1024, LANES), _tile(f, 512, LANES)

    def body(df_ref, w_ref, a_ref, b_ref, da_ref, db_ref):
        dz = lax.dot_general(df_ref[...], w_ref[...], _NT, preferred_element_type=F32)
        silu, dsilu = _silu_parts(a_ref[...].astype(F32))
        da_ref[...] = (dz * b_ref[...].astype(F32) * dsilu).astype(da_ref.dtype)
        db_ref[...] = (dz * silu).astype(db_ref.dtype)

    blk = pl.BlockSpec((tm, tn), lambda i, j: (i, j))
    return _call(
        body,
        name=name,
        grid=(s // tm, f // tn),
        in_specs=[pl.BlockSpec((tm, d), lambda i, j: (i, 0)), pl.BlockSpec((None, tn, d), lambda i, j: (layer, j, 0)), blk, blk],
        out_specs=[blk, blk],
        out_shape=[jax.ShapeDtypeStruct((s, f), BF16)] * 2,
        args=[df, w_down, a, b],
        sem=("parallel", "parallel"),
        hosted=hosted,
    )


def _ffn_in_dx(da, db, w_gate, w_up, layer, *, name, hosted=None):
    s, f = da.shape
    d = w_gate.shape[1]
    tm, tn, tk = _tile(s, 1024, LANES), _tile(d, 1024, LANES), _tile(f, 2048, LANES)
    nk = f // tk

    def body(da_ref, db_ref, g_ref, u_ref, o_ref, acc_ref):
        part = lax.dot_general(da_ref[...], g_ref[...], _NT, preferred_element_type=F32)
        part = part + lax.dot_general(db_ref[...], u_ref[...], _NT, preferred_element_type=F32)
        step = pl.program_id(2)

        @pl.when(step == 0)
        def _():
            acc_ref[...] = part

        @pl.when(step > 0)
        def _():
            acc_ref[...] += part

        @pl.when(step == nk - 1)
        def _():
            o_ref[...] = acc_ref[...]

    x_spec = pl.BlockSpec((tm, tk), lambda i, j, k: (i, k))
    w_spec = pl.BlockSpec((None, tn, tk), lambda i, j, k: (layer, j, k))
    return _call(
        body,
        name=name,
        grid=(s // tm, d // tn, nk),
        in_specs=[x_spec, x_spec, w_spec, w_spec],
        out_specs=[pl.BlockSpec((tm, tn), lambda i, j, k: (i, j))],
        out_shape=[jax.ShapeDtypeStruct((s, d), F32)],
        args=[da, db, w_gate, w_up],
        sem=("parallel", "parallel", "arbitrary"),
        scratch_shapes=[pltpu.VMEM((tm, tn), F32)],
        hosted=hosted,
    )


def _ffn_in_dw(h, da, db, *, name, hosted=None):
    s, d = h.shape
    f = da.shape[1]
    tm, tn, tk = _tile(d, 1024, LANES), _tile(f, 512, LANES), _tile(s, 2048, LANES)
    nk = s // tk

    def body(h_ref, da_ref, db_ref, g_ref, u_ref, g_acc, u_acc):
        hv = h_ref[...]
        pg = lax.dot_general(hv, da_ref[...], _TN, preferred_element_type=F32)
        pu = lax.dot_general(hv, db_ref[...], _TN, preferred_element_type=F32)
        step = pl.program_id(2)

        @pl.when(step == 0)
        def _():
            g_acc[...] = pg
            u_acc[...] = pu

        @pl.when(step > 0)
        def _():
            g_acc[...] += pg
            u_acc[...] += pu

        @pl.when(step == nk - 1)
        def _():
            g_ref[...] = g_acc[...].astype(g_ref.dtype)
            u_ref[...] = u_acc[...].astype(u_ref.dtype)

    y_spec = pl.BlockSpec((tk, tn), lambda i, j, k: (k, j))
    out = pl.BlockSpec((tm, tn), lambda i, j, k: (i, j))
    return _call(
        body,
        name=name,
        grid=(d // tm, f // tn, nk),
        in_specs=[pl.BlockSpec((tk, tm), lambda i, j, k: (k, i)), y_spec, y_spec],
        out_specs=[out, out],
        out_shape=[jax.ShapeDtypeStruct((d, f), BF16)] * 2,
        args=[h, da, db],
        sem=("parallel", "parallel", "arbitrary"),
        scratch_shapes=[pltpu.VMEM((tm, tn), F32)] * 2,
        hosted=hosted,
    )


def _gmm(a, w, *, name, mode, out_dtype):
    s = a.shape[0]
    g = len(POOL_WINDOWS)
    c = a.shape[1] // g
    tr = _tile(s, 1024, LANES)
    n_row = s // tr

    if mode == "tn":

        def body(a_ref, b_ref, o_ref, acc_ref):
            part = lax.dot_general(a_ref[...].astype(BF16), b_ref[...].astype(BF16), (((0,), (0,)), ((), ())), preferred_element_type=F32)

            @pl.when(pl.program_id(1) == 0)
            def _():
                acc_ref[...] = part

            @pl.when(pl.program_id(1) > 0)
            def _():
                acc_ref[...] += part

            @pl.when(pl.program_id(1) == n_row - 1)
            def _():
                o_ref[0] = acc_ref[...].astype(o_ref.dtype)

        return pl.pallas_call(
            body,
            name=name,
            grid=(g, n_row),
            in_specs=[pl.BlockSpec((tr, c), lambda gi, i: (i, gi)), pl.BlockSpec((tr, c), lambda gi, i: (i, gi))],
            out_specs=pl.BlockSpec((1, c, c), lambda gi, i: (gi, 0, 0)),
            out_shape=jax.ShapeDtypeStruct((g, c, c), out_dtype),
            scratch_shapes=[pltpu.VMEM((c, c), F32)],
            compiler_params=_params("parallel", "arbitrary"),
        )(a, w)

    dims = (((1,), (0 if mode == "nn" else 1,)), ((), ()))

    def body(a_ref, w_ref, o_ref):
        o_ref[...] = lax.dot_general(a_ref[...].astype(BF16), w_ref[0].astype(BF16), dims, preferred_element_type=F32).astype(o_ref.dtype)

    return pl.pallas_call(
        body,
        name=name,
        grid=(g, n_row),
        in_specs=[pl.BlockSpec((tr, c), lambda gi, i: (i, gi)), pl.BlockSpec((1, c, c), lambda gi, i: (gi, 0, 0))],
        out_specs=pl.BlockSpec((tr, c), lambda gi, i: (i, gi)),
        out_shape=jax.ShapeDtypeStruct((s, g * c), out_dtype),
        compiler_params=_params("parallel", "parallel"),
    )(a, w)


def _row_tile(s, d):
    return _tile(s, max(8, (1 << 19) // d), 8)


def _norm_fwd(x, g, *, name, scale=None, shift=None, y=None, gate=None, out_dtype=BF16):
    s, d = x.shape
    tr = _row_tile(s, d)
    has_res, has_mod = y is not None, scale is not None

    def body(*refs):
        refs = list(refs)
        x_ref = refs.pop(0)
        xv = x_ref[...]
        if has_res:
            y_ref, gate_ref = refs.pop(0), refs.pop(0)
            xv = xv + gate_ref[...] * y_ref[...]
        g_ref = refs.pop(0)
        if has_mod:
            scale_ref, shift_ref = refs.pop(0), refs.pop(0)
        if has_res:
            refs.pop(0)[...] = xv
        h = xv * lax.rsqrt(jnp.mean(xv * xv, axis=-1, keepdims=True) + NORM_EPS)
        h = h * g_ref[...]
        if has_mod:
            h = h * (1.0 + scale_ref[...]) + shift_ref[...]
        refs.pop(0)[...] = h.astype(out_dtype)

    row = pl.BlockSpec((tr, d), lambda i: (i, 0))
    vec = pl.BlockSpec((1, d), lambda i: (0, 0))
    args, in_specs = [x], [row]
    if has_res:
        args += [y, gate]
        in_specs += [row, vec]
    args.append(g)
    in_specs.append(vec)
    if has_mod:
        args += [scale, shift]
        in_specs += [vec, vec]
    out_shape, out_specs = [], []
    if has_res:
        out_shape.append(jax.ShapeDtypeStruct((s, d), F32))
        out_specs.append(row)
    out_shape.append(jax.ShapeDtypeStruct((s, d), out_dtype))
    out_specs.append(row)
    res = pl.pallas_call(
        body, name=name, grid=(s // tr,), in_specs=in_specs, out_specs=out_specs, out_shape=out_shape, compiler_params=_params("parallel")
    )(*args)
    return (res[0], res[1]) if has_res else res[0]


def _norm_bwd(x, g, dh, *, name, scale=None, resid=None):
    s, d = x.shape
    tr = _row_tile(s, d)
    has_mod, has_res = scale is not None, resid is not None

    def body(*refs):
        refs = list(refs)
        x_ref, g_ref, dh_ref = refs.pop(0), refs.pop(0), refs.pop(0)
        scale_ref = refs.pop(0) if has_mod else None
        resid_ref = refs.pop(0) if has_res else None
        dx_ref, sums_ref = refs
        xv = x_ref[...]
        r = lax.rsqrt(jnp.mean(xv * xv, axis=-1, keepdims=True) + NORM_EPS)
        xhat = xv * r
        dh32 = dh_ref[...].astype(F32)
        gv = g_ref[...]
        dn = dh32 * (1.0 + scale_ref[...]) if has_mod else dh32
        dxhat = dn * gv
        dx = r * (dxhat - xhat * jnp.mean(dxhat * xhat, axis=-1, keepdims=True))
        if has_res:
            dx = dx + resid_ref[...]
        dx_ref[...] = dx

        @pl.when(pl.program_id(0) == 0)
        def _():
            sums_ref[...] = jnp.zeros_like(sums_ref)

        sums_ref[0:1, :] += jnp.sum(dh32, axis=0, keepdims=True)
        sums_ref[1:2, :] += jnp.sum(dh32 * (xhat * gv), axis=0, keepdims=True)
        sums_ref[2:3, :] += jnp.sum(dn * xhat, axis=0, keepdims=True)

    row = pl.BlockSpec((tr, d), lambda i: (i, 0))
    vec = pl.BlockSpec((1, d), lambda i: (0, 0))
    args, in_specs = [x, g, dh], [row, vec, row]
    if has_mod:
        args.append(scale)
        in_specs.append(vec)
    if has_res:
        args.append(resid)
        in_specs.append(row)
    return pl.pallas_call(
        body,
        name=name,
        grid=(s // tr,),
        in_specs=in_specs,
        out_specs=[row, pl.BlockSpec((8, d), lambda i: (0, 0))],
        out_shape=[jax.ShapeDtypeStruct((s, d), F32), jax.ShapeDtypeStruct((8, d), F32)],
        compiler_params=_params("arbitrary"),
    )(*args)


def _gate_bwd(dx, y, gate, *, name):
    s, d = dx.shape
    tr = _row_tile(s, d)

    def body(dx_ref, y_ref, gate_ref, dy_ref, sums_ref):
        dxv = dx_ref[...]
        dy_ref[...] = (dxv * gate_ref[...]).astype(dy_ref.dtype)

        @pl.when(pl.program_id(0) == 0)
        def _():
            sums_ref[...] = jnp.zeros_like(sums_ref)

        sums_ref[0:1, :] += jnp.sum(dxv * y_ref[...], axis=0, keepdims=True)

    row = pl.BlockSpec((tr, d), lambda i: (i, 0))
    return pl.pallas_call(
        body,
        name=name,
        grid=(s // tr,),
        in_specs=[row, row, pl.BlockSpec((1, d), lambda i: (0, 0))],
        out_specs=[row, pl.BlockSpec((8, d), lambda i: (0, 0))],
        out_shape=[jax.ShapeDtypeStruct((s, d), BF16), jax.ShapeDtypeStruct((8, d), F32)],
        compiler_params=_params("arbitrary"),
    )(dx, y, gate)


def _elementwise(fn, args, out_dtypes, *, name):
    s, d = args[0].shape
    tc = d if d <= 2048 else _tile(d, 1024, LANES)
    tr = _tile(s, max(8, (1 << 18) // tc), 8)
    n_in = len(args)

    def body(*refs):
        outs = fn(*[r[...] for r in refs[:n_in]])
        for o_ref, o in zip(refs[n_in:], outs):
            o_ref[...] = o.astype(o_ref.dtype)

    spec = pl.BlockSpec((tr, tc), lambda i, j: (i, j))
    return pl.pallas_call(
        body,
        name=name,
        grid=(s // tr, d // tc),
        in_specs=[spec] * n_in,
        out_specs=[spec] * len(out_dtypes),
        out_shape=[jax.ShapeDtypeStruct((s, d), dt) for dt in out_dtypes],
        compiler_params=_params("parallel", "parallel"),
    )(*args)


def _silu(v):
    return (v * jax.nn.sigmoid(v),)


def _split3(v):
    hi = v.astype(BF16)
    r1 = v - hi.astype(F32)
    mid = r1.astype(BF16)
    lo = (r1 - mid.astype(F32)).astype(BF16)
    return hi, mid, lo


def _band_dot(band, v):
    return sum(jnp.dot(band, part, preferred_element_type=F32) for part in _split3(v))


def _pool(h, *, name, transpose, out_dtype):
    s, d = h.shape
    c = d // len(POOL_WINDOWS)
    tr = _tile(s, 256, HALO)
    per = tr // HALO
    n_halo = s // HALO

    def body(h_ref, halo_ref, o_ref):
        i = pl.program_id(0)
        out_row = i * tr + lax.broadcasted_iota(jnp.int32, (tr, tr + HALO), 0)
        col = lax.broadcasted_iota(jnp.int32, (tr, tr + HALO), 1)
        if transpose:
            ext = jnp.concatenate([h_ref[...], halo_ref[...]], axis=0)
            src_row = i * tr + col
            ext_row = i * tr + lax.broadcasted_iota(jnp.int32, (tr + HALO, 1), 0)
        else:
            ext = jnp.concatenate([halo_ref[...], h_ref[...]], axis=0)
            src_row = i * tr + col - HALO
            own_row = i * tr + lax.broadcasted_iota(jnp.int32, (tr, 1), 0)
        for gi, w in enumerate(POOL_WINDOWS):
            cols = slice(gi * c, (gi + 1) * c)
            if transpose:
                band = (src_row >= out_row) & (src_row < out_row + w) & (src_row < s)
                scaled = ext[:, cols] / jnp.minimum(ext_row + 1, w).astype(F32)
                res = _band_dot(band.astype(BF16), scaled) - h_ref[:, cols]
            else:
                band = (src_row <= out_row) & (src_row > out_row - w) & (src_row >= 0)
                res = _band_dot(band.astype(BF16), ext[:, cols]) / jnp.minimum(own_row + 1, w).astype(F32) - h_ref[:, cols]
            o_ref[:, cols] = res.astype(o_ref.dtype)

    if transpose:
        halo_map = lambda i: (jnp.minimum((i + 1) * per, n_halo - 1), 0)
    else:
        halo_map = lambda i: (jnp.maximum(i * per - 1, 0), 0)
    return pl.pallas_call(
        body,
        name=name,
        grid=(s // tr,),
        in_specs=[pl.BlockSpec((tr, d), lambda i: (i, 0)), pl.BlockSpec((HALO, d), halo_map)],
        out_specs=pl.BlockSpec((tr, d), lambda i: (i, 0)),
        out_shape=jax.ShapeDtypeStruct((s, d), out_dtype),
        compiler_params=_params("parallel"),
    )(h, h)


def _rotate(v, cos, sin):
    lane = lax.broadcasted_iota(jnp.int32, v.shape, 1)
    swapped = jnp.where(lane % ROPE_DIM < ROPE_DIM // 2, pltpu.roll(v, LANES - ROPE_DIM // 2, 1), pltpu.roll(v, ROPE_DIM // 2, 1))
    return v * cos + swapped * sin


def _rope_heads(x, cos, sin, *, name, out_dtype):
    s, n = x.shape
    tr = _tile(s, max(16, (1 << 18) // n), 16)

    def body(x_ref, cos_ref, sin_ref, o_ref):
        cos_v, sin_v = cos_ref[...], sin_ref[...]
        for j in range(n // LANES):
            lanes = slice(j * LANES, (j + 1) * LANES)
            if j % 2 == 0:
                o_ref[:, lanes] = x_ref[:, lanes].astype(o_ref.dtype)
            else:
                o_ref[:, lanes] = _rotate(x_ref[:, lanes].astype(F32), cos_v, sin_v).astype(o_ref.dtype)

    blk = pl.BlockSpec((tr, n), lambda i: (i, 0))
    tab = pl.BlockSpec((tr, LANES), lambda i: (i, 0))
    return pl.pallas_call(
        body,
        name=name,
        grid=(s // tr,),
        in_specs=[blk, tab, tab],
        out_specs=blk,
        out_shape=jax.ShapeDtypeStruct((s, n), out_dtype),
        compiler_params=_params("parallel"),
    )(x, cos, sin)


def _build_keys(kv, kr_pre, cos, sin, *, name):
    s, n = kv.shape
    tr = _tile(s, max(16, (1 << 18) // n), 16)

    def body(kv_ref, kr_ref, cos_ref, sin_ref, o_ref):
        rope = _rotate(kr_ref[...], cos_ref[...], sin_ref[...]).astype(o_ref.dtype)
        for j in range(n // LANES):
            lanes = slice(j * LANES, (j + 1) * LANES)
            o_ref[:, lanes] = kv_ref[:, lanes] if j % 2 == 0 else rope

    blk = pl.BlockSpec((tr, n), lambda i: (i, 0))
    tab = pl.BlockSpec((tr, LANES), lambda i: (i, 0))
    return pl.pallas_call(
        body,
        name=name,
        grid=(s // tr,),
        in_specs=[blk, tab, tab, tab],
        out_specs=blk,
        out_shape=jax.ShapeDtypeStruct((s, n), BF16),
        compiler_params=_params("parallel"),
    )(kv, kr_pre, cos, sin)


def _keys_bwd(dk_a, dk_b, dv_a, dv_b, cos, sin_neg, *, name):
    s, n = dk_a.shape
    heads = n // HEAD_PAD
    tr = _tile(s, 512, 8)

    def body(dka_ref, dkb_ref, dva_ref, dvb_ref, cos_ref, sin_ref, dkv_ref, dkr_ref):
        hd = pl.program_id(1)
        dk = dka_ref[...] + dkb_ref[...]
        dkv_ref[:, :NOPE_DIM] = dk[:, :NOPE_DIM].astype(dkv_ref.dtype)
        dkv_ref[:, NOPE_DIM:] = (dva_ref[...] + dvb_ref[...]).astype(dkv_ref.dtype)

        @pl.when(hd == 0)
        def _():
            dkr_ref[...] = dk[:, NOPE_DIM:]

        @pl.when(hd > 0)
        def _():
            dkr_ref[...] += dk[:, NOPE_DIM:]

        @pl.when(hd == heads - 1)
        def _():
            dkr_ref[...] = _rotate(dkr_ref[...], cos_ref[...], sin_ref[...])

    dk_blk = pl.BlockSpec((tr, HEAD_PAD), lambda i, hd: (i, hd))
    dv_blk = pl.BlockSpec((tr, V_DIM), lambda i, hd: (i, hd))
    tab = pl.BlockSpec((tr, LANES), lambda i, hd: (i, 0))
    return pl.pallas_call(
        body,
        name=name,
        grid=(s // tr, heads),
        in_specs=[dk_blk, dk_blk, dv_blk, dv_blk, tab, tab],
        out_specs=[dk_blk, tab],
        out_shape=[jax.ShapeDtypeStruct((s, n), BF16), jax.ShapeDtypeStruct((s, LANES), F32)],
        compiler_params=_params("parallel", "arbitrary"),
    )(dk_a, dk_b, dv_a, dv_b, cos, sin_neg)


def _attn_tile(s):
    return _tile(s, 512, LANES)


ATTN_SUB = 512


def _sub_tiles(t):
    step = min(ATTN_SUB, t)
    return [slice(r, r + step) for r in range(0, t, step)]


def _causal_mask(t, sub, transposed=False):
    n = sub.stop - sub.start
    rows = sub.start + lax.broadcasted_iota(jnp.int32, (n, t), 0)
    cols = lax.broadcasted_iota(jnp.int32, (n, t), 1)
    return rows <= cols if transposed else cols <= rows


def _attn_fwd(q, keys, v_t, *, name):
    s = q.shape[0]
    heads = q.shape[1] // HEAD_PAD
    t = _attn_tile(s)
    nq = s // t

    def body(q_ref, k_ref, v_ref, o_ref, lse_ref, m_ref, l_ref, acc_ref):
        qi = pl.program_id(1)
        qv = q_ref[...]
        m_ref[...] = jnp.full_like(m_ref, -jnp.inf)
        l_ref[...] = jnp.zeros_like(l_ref)
        acc_ref[...] = jnp.zeros_like(acc_ref)

        def block(kb, diagonal):
            rows = pl.ds(pl.multiple_of(kb * t, t), t)
            sc_t = lax.dot_general(k_ref[rows, :], qv, _NT, preferred_element_type=F32) * SM_SCALE
            if diagonal:
                sc_t = jnp.where(_causal_mask(t, slice(0, t), transposed=True), sc_t, -jnp.inf)
            m_old = m_ref[...]
            m_new = jnp.maximum(m_old, jnp.max(sc_t, axis=0, keepdims=True))
            alpha = jnp.exp(m_old - m_new)
            p_t = jnp.exp(sc_t - m_new)
            l_ref[...] = alpha * l_ref[...] + jnp.sum(p_t, axis=0, keepdims=True)
            acc_ref[...] = alpha * acc_ref[...] + jnp.dot(v_ref[0, kb], p_t.astype(BF16), preferred_element_type=F32)
            m_ref[...] = m_new

        def earlier(kb, carry):
            block(kb, False)
            return carry

        lax.fori_loop(0, qi, earlier, 0)
        block(qi, True)
        o_ref[...] = (acc_ref[...] / l_ref[...]).astype(o_ref.dtype)
        lse_ref[0, 0] = m_ref[...] + jnp.log(l_ref[...])

    return pl.pallas_call(
        body,
        name=name,
        grid=(heads, nq),
        in_specs=[
            pl.BlockSpec((t, HEAD_PAD), lambda hd, qi: (qi, hd)),
            pl.BlockSpec((s, HEAD_PAD), lambda hd, qi: (0, hd)),
            pl.BlockSpec((1, nq, V_DIM, t), lambda hd, qi: (hd, 0, 0, 0)),
        ],
        out_specs=[pl.BlockSpec((V_DIM, t), lambda hd, qi: (hd, qi)), pl.BlockSpec((1, 1, 1, t), lambda hd, qi: (hd, qi, 0, 0))],
        out_shape=[jax.ShapeDtypeStruct((heads * V_DIM, s), BF16), jax.ShapeDtypeStruct((heads, nq, 1, t), F32)],
        scratch_shapes=[pltpu.VMEM((1, t), F32), pltpu.VMEM((1, t), F32), pltpu.VMEM((V_DIM, t), F32)],
        compiler_params=_params("parallel", "parallel"),
    )(q, keys, v_t)


def _attn_bwd_q(q, keys, kv, o, do, lse, *, name):
    s = q.shape[0]
    heads = q.shape[1] // HEAD_PAD
    t = _attn_tile(s)

    def body(q_ref, k_ref, v_ref, o_ref, do_ref, lse_ref, dq_ref, delta_ref, acc_ref):
        qi = pl.program_id(1)
        qv, dov, lse = q_ref[...], do_ref[...], lse_ref[0]
        delta = jnp.sum(dov.astype(F32) * o_ref[...].astype(F32), axis=1, keepdims=True)
        delta_ref[0] = delta
        acc_ref[...] = jnp.zeros_like(acc_ref)

        def block(kb, diagonal):
            rows = pl.ds(pl.multiple_of(kb * t, t), t)
            kblk, vblk = k_ref[rows, :], v_ref[rows, :]
            for sub in _sub_tiles(t):
                sc = lax.dot_general(qv[sub], kblk, _NT, preferred_element_type=F32) * SM_SCALE
                p = jnp.exp(sc - lse[sub])
                if diagonal:
                    p = jnp.where(_causal_mask(t, sub), p, 0.0)
                dp = lax.dot_general(dov[sub], vblk, _NT, preferred_element_type=F32)
                ds = p * (dp - delta[sub]) * SM_SCALE
                acc_ref[sub, :] += jnp.dot(ds.astype(BF16), kblk, preferred_element_type=F32)

        def earlier(kb, carry):
            block(kb, False)
            return carry

        lax.fori_loop(0, qi, earlier, 0)
        block(qi, True)
        dq_ref[...] = acc_ref[...]

    o_blk = pl.BlockSpec((t, V_DIM), lambda hd, qi: (qi, hd))
    col = pl.BlockSpec((1, t, 1), lambda hd, qi: (hd, qi, 0))
    return pl.pallas_call(
        body,
        name=name,
        grid=(heads, s // t),
        in_specs=[
            pl.BlockSpec((t, HEAD_PAD), lambda hd, qi: (qi, hd)),
            pl.BlockSpec((s, HEAD_PAD), lambda hd, qi: (0, hd)),
            pl.BlockSpec((s, V_DIM), lambda hd, qi: (0, 2 * hd + 1)),
            o_blk,
            o_blk,
            col,
        ],
        out_specs=[pl.BlockSpec((t, HEAD_PAD), lambda hd, qi: (qi, hd)), col],
        out_shape=[jax.ShapeDtypeStruct((s, heads * HEAD_PAD), F32), jax.ShapeDtypeStruct((heads, s, 1), F32)],
        scratch_shapes=[pltpu.VMEM((t, HEAD_PAD), F32)],
        compiler_params=_params("parallel", "parallel"),
    )(q, keys, kv, o, do, lse)


def _attn_bwd_kv(q, keys, kv, do, lse_row, delta_row, *, name):
    s = q.shape[0]
    heads = q.shape[1] // HEAD_PAD
    t = _attn_tile(s)
    nq = s // t

    def body(q_ref, k_ref, v_ref, do_ref, lse_ref, delta_ref, dk_ref, dv_ref, dk_acc, dv_acc):
        ki = pl.program_id(1)
        kv_, vv = k_ref[...], v_ref[...]
        dk_acc[...] = jnp.zeros_like(dk_acc)
        dv_acc[...] = jnp.zeros_like(dv_acc)

        def block(qb, diagonal):
            rows = pl.ds(pl.multiple_of(qb * t, t), t)
            qv, dov = q_ref[rows, :], do_ref[rows, :]
            lse, delta = lse_ref[0, qb], delta_ref[0, qb]
            for sub in _sub_tiles(t):
                sc_t = lax.dot_general(kv_[sub], qv, _NT, preferred_element_type=F32) * SM_SCALE
                p_t = jnp.exp(sc_t - lse)
                if diagonal:
                    p_t = jnp.where(_causal_mask(t, sub, transposed=True), p_t, 0.0)
                dv_acc[sub, :] += jnp.dot(p_t.astype(BF16), dov, preferred_element_type=F32)
                dp_t = lax.dot_general(vv[sub], dov, _NT, preferred_element_type=F32)
                ds_t = p_t * (dp_t - delta) * SM_SCALE
                dk_acc[sub, :] += jnp.dot(ds_t.astype(BF16), qv, preferred_element_type=F32)

        def later(qb, carry):
            block(qb, False)
            return carry

        block(ki, True)
        lax.fori_loop(ki + 1, nq, later, 0)
        dk_ref[...] = dk_acc[...]
        dv_ref[...] = dv_acc[...]

    row = pl.BlockSpec((1, nq, 1, t), lambda hd, ki: (hd, 0, 0, 0))
    return pl.pallas_call(
        body,
        name=name,
        grid=(heads, nq),
        in_specs=[
            pl.BlockSpec((s, HEAD_PAD), lambda hd, ki: (0, hd)),
            pl.BlockSpec((t, HEAD_PAD), lambda hd, ki: (ki, hd)),
            pl.BlockSpec((t, V_DIM), lambda hd, ki: (ki, 2 * hd + 1)),
            pl.BlockSpec((s, V_DIM), lambda hd, ki: (0, hd)),
            row,
            row,
        ],
        out_specs=[pl.BlockSpec((t, HEAD_PAD), lambda hd, ki: (ki, hd)), pl.BlockSpec((t, V_DIM), lambda hd, ki: (ki, hd))],
        out_shape=[jax.ShapeDtypeStruct((s, heads * HEAD_PAD), F32), jax.ShapeDtypeStruct((s, heads * V_DIM), F32)],
        scratch_shapes=[pltpu.VMEM((t, HEAD_PAD), F32), pltpu.VMEM((t, V_DIM), F32)],
        compiler_params=_params("parallel", "parallel"),
    )(q, keys, kv, do, lse_row, delta_row)


def _loss_bwd(x, y, gate, g, target, *, name):
    s, d = x.shape
    tr = _row_tile(s, d)

    def body(x_ref, y_ref, gate_ref, g_ref, t_ref, dx_ref, stats_ref, loss_ref):
        xv = x_ref[...] + gate_ref[...] * y_ref[...]
        r = lax.rsqrt(jnp.mean(xv * xv, axis=-1, keepdims=True) + NORM_EPS)
        xhat = xv * r
        gv = g_ref[...]
        err = xhat * gv - t_ref[...]
        dy = err / d
        dxhat = dy * gv
        dx_ref[...] = r * (dxhat - xhat * jnp.mean(dxhat * xhat, axis=-1, keepdims=True))

        @pl.when(pl.program_id(0) == 0)
        def _():
            stats_ref[...] = jnp.zeros_like(stats_ref)
            loss_ref[...] = jnp.zeros_like(loss_ref)

        stats_ref[0:1, :] += jnp.sum(dy * xhat, axis=0, keepdims=True)
        loss_ref[...] += 0.5 * jnp.sum(jnp.mean(err * err, axis=-1, keepdims=True))

    row = pl.BlockSpec((tr, d), lambda i: (i, 0))
    vec = pl.BlockSpec((1, d), lambda i: (0, 0))
    return pl.pallas_call(
        body,
        name=name,
        grid=(s // tr,),
        in_specs=[row, row, vec, vec, row],
        out_specs=[row, pl.BlockSpec((8, d), lambda i: (0, 0)), pl.BlockSpec((8, LANES), lambda i: (0, 0))],
        out_shape=[jax.ShapeDtypeStruct((s, d), F32), jax.ShapeDtypeStruct((8, d), F32), jax.ShapeDtypeStruct((8, LANES), F32)],
        compiler_params=_params("arbitrary"),
    )(x, y, gate, g, target)


def _adam_math(w, g, m, v):
    new_m = ADAM_B1 * m + (1.0 - ADAM_B1) * g
    new_v = ADAM_B2 * v + (1.0 - ADAM_B2) * (g * g)
    m_hat = new_m / (1.0 - ADAM_B1**ADAM_STEP)
    v_hat = new_v / (1.0 - ADAM_B2**ADAM_STEP)
    return -ADAM_LR * (m_hat / (jnp.sqrt(v_hat) + ADAM_EPS) + ADAM_WD * w), new_m, new_v


def _adamw(w, g, m, v, *, name):
    rows, cols = w.shape
    tr = _tile(rows, max(8, (1 << 18) // cols), 8)

    def body(w_ref, g_ref, m_ref, v_ref, go_ref, d_ref, mo_ref, vo_ref):
        gv = g_ref[...]
        go_ref[...] = gv
        d_ref[...], mo_ref[...], vo_ref[...] = _adam_math(w_ref[...], gv, m_ref[...], v_ref[...])

    spec = pl.BlockSpec((tr, cols), lambda i: (i, 0))
    return pl.pallas_call(
        body,
        name=name,
        grid=(rows // tr,),
        in_specs=[spec] * 4,
        out_specs=[spec] * 4,
        out_shape=[jax.ShapeDtypeStruct((rows, cols), F32)] * 4,
        compiler_params=_params("parallel"),
    )(w, g, m, v)


def _tp_adamw(sc16, dm, w, m, v, *, name):
    nl, d, n = w.shape
    tm = _tile(d, 512, LANES)
    tn = _tile(n, 1024, LANES)

    def body(sc_ref, dm_ref, w_ref, m_ref, v_ref, go_ref, d_ref, mo_ref, vo_ref):
        gv = lax.dot_general(sc_ref[...].astype(BF16), dm_ref[0].astype(BF16), (((0,), (0,)), ((), ())), preferred_element_type=F32)
        go_ref[0] = gv
        d_ref[0], mo_ref[0], vo_ref[0] = _adam_math(w_ref[0], gv, m_ref[0], v_ref[0])

    blk = pl.BlockSpec((1, tm, tn), lambda l, i, j: (l, i, j))
    return pl.pallas_call(
        body,
        name=name,
        grid=(nl, d // tm, n // tn),
        in_specs=[pl.BlockSpec((16, tm), lambda l, i, j: (0, i)), pl.BlockSpec((1, 16, tn), lambda l, i, j: (l, 0, j)), blk, blk, blk],
        out_specs=[blk] * 4,
        out_shape=[jax.ShapeDtypeStruct((nl, d, n), F32)] * 4,
        compiler_params=_params("parallel", "parallel", "parallel"),
    )(sc16, dm, w, m, v)


def _sum_devices(x, *, name):
    def body(x_ref, o_ref):
        acc = x_ref[0]
        for k in range(1, N_DEV):
            acc = acc + x_ref[k]
        o_ref[...] = acc

    return pl.pallas_call(body, name=name, out_shape=jax.ShapeDtypeStruct(x.shape[1:], F32))(x)


def _place():
    mx, my, mc = lax.axis_index("x"), lax.axis_index("y"), lax.axis_index("c")
    chips = [(1 - mx, my), (mx, 1 - my), (1 - mx, 1 - my)]
    return mx, my, mc, chips


def _remote(src, dst, send_sem, recv_sem, device):
    return pltpu.make_async_remote_copy(src_ref=src, dst_ref=dst, send_sem=send_sem, recv_sem=recv_sem, device_id=device, device_id_type=MESH)


def _allgather8(x, *, name):
    def body(x_ref, out_ref, send_sems, recv_sems, local_sem):
        mx, my, mc, chips = _place()
        me, sibling = (mx, my, mc), (mx, my, 1 - mc)

        def slot(px, py, pc):
            return out_ref.at[4 * px + 2 * py + pc]

        def copy(k, block, to, src=None):
            return _remote(slot(*block) if src is None else src, slot(*block), send_sems.at[k], recv_sems.at[k], to)

        mine = pltpu.make_async_copy(x_ref, slot(*me), local_sem)
        mine.start()
        first = [copy(0, me, sibling, src=x_ref)] + [copy(1 + j, me, (*chip, mc), src=x_ref) for j, chip in enumerate(chips)]
        for cp in first:
            cp.start()
        passed = [copy(4 + j, (*chip, mc), sibling) for j, chip in enumerate(chips)]
        for j, chip in enumerate(chips):
            copy(1 + j, (*chip, mc), me).wait_recv()
            passed[j].start()
        copy(0, sibling, me).wait_recv()
        for j, chip in enumerate(chips):
            copy(4 + j, (*chip, 1 - mc), me).wait_recv()
        for cp in first + passed:
            cp.wait_send()
        mine.wait()

    return pl.pallas_call(
        body,
        name=name,
        out_shape=jax.ShapeDtypeStruct((N_DEV,) + x.shape, x.dtype),
        in_specs=[pl.BlockSpec(memory_space=pltpu.VMEM)],
        out_specs=pl.BlockSpec(memory_space=pltpu.VMEM),
        scratch_shapes=[pltpu.SemaphoreType.DMA((7,)), pltpu.SemaphoreType.DMA((7,)), pltpu.SemaphoreType.DMA],
    )(x)


class _Geom:
    def __init__(self, shape3, axis):
        self.shape3, self.axis = shape3, axis
        nl, r, c = shape3
        self.rs, self.cs = (r // N_CHIPS, c) if axis == 1 else (r, c // N_CHIPS)
        self.hl, self.hr = (nl // 2, self.rs) if nl > 1 else (1, self.rs // 2)
        self.shard = (nl, self.rs, self.cs)
        self.half = (self.hl, self.hr, self.cs)

    def in_full(self, ref, chip, core):
        nl = self.shape3[0]
        l0 = core * self.hl if nl > 1 else 0
        r0 = (chip * self.rs if self.axis == 1 else 0) + (0 if nl > 1 else core * self.hr)
        c0 = chip * self.cs if self.axis == 2 else 0
        return ref.at[pl.ds(l0, self.hl), pl.ds(r0, self.hr), pl.ds(c0, self.cs)]


def _place_shard(shard, geom, chip_arr, *, name, layer=None):
    nl, rs, cs = geom.shard
    tr = _tile(rs, max(16, (1 << 18) // cs), 16)
    per = rs // tr
    first = 0 if layer is None else layer

    def body(chip_ref, x_ref, o_ref):
        o_ref[...] = x_ref[...].astype(o_ref.dtype)

    def out_map(l, i, chip_ref):
        return (l, chip_ref[0] * per + i, 0) if geom.axis == 1 else (l, i, chip_ref[0])

    return pl.pallas_call(
        body,
        name=name,
        grid_spec=pltpu.PrefetchScalarGridSpec(
            num_scalar_prefetch=1,
            grid=(nl, per),
            in_specs=[pl.BlockSpec((1, tr, cs), lambda l, i, chip_ref: (first + l, i, 0))],
            out_specs=pl.BlockSpec((1, tr, cs), out_map),
        ),
        out_shape=jax.ShapeDtypeStruct(geom.shape3, BF16),
        compiler_params=_params("parallel", "parallel"),
    )(chip_arr, shard)


def _dma_sems(count, arrays):
    return [pltpu.SemaphoreType.DMA((count,))] * arrays


def _gather_plan(fulls, geoms):
    def ici(w, k, src, dst, sems, device):
        return _remote(src, dst, sems[0].at[3 * w + k], sems[1].at[3 * w + k], device)

    def d2d(w, k, box, sems, device):
        return _remote(box, box, sems[2].at[3 * w + k], sems[3].at[3 * w + k], device)

    def start(given, full, sems):
        mx, my, mc, chips = _place()
        me = 2 * mx + my
        for w, geom in enumerate(geoms):
            for k, chip in enumerate(chips):
                ici(w, k, geom.in_full(given[w], me, mc), geom.in_full(full[w], me, mc), sems, (*chip, mc)).start()

    def finish(given, full, sems):
        mx, my, mc, chips = _place()
        me, sibling = 2 * mx + my, (mx, my, 1 - mc)
        for w, geom in enumerate(geoms):
            for k, (px, py) in enumerate(chips):
                landed = geom.in_full(full[w], 2 * px + py, mc)
                ici(w, k, landed, landed, sems, (px, py, mc)).wait_recv()
                d2d(w, k, landed, sems, sibling).start()
        for w, geom in enumerate(geoms):
            for k, (px, py) in enumerate(chips):
                d2d(w, k, geom.in_full(full[w], 2 * px + py, 1 - mc), sems, sibling).wait_recv()
        for w, geom in enumerate(geoms):
            for k, (px, py) in enumerate(chips):
                ici(w, k, geom.in_full(given[w], me, mc), geom.in_full(full[w], me, mc), sems, (px, py, mc)).wait_send()
                d2d(w, k, geom.in_full(full[w], 2 * px + py, mc), sems, sibling).wait_send()

    n = len(fulls)
    shapes = [jax.ShapeDtypeStruct(f.shape, f.dtype) for f in fulls]
    return _Hosted(fulls, shapes, {w: w for w in range(n)}, _dma_sems(3 * n, 4), start, finish)


def _pair_plan(grads, geoms):
    def copies(grad, theirs, sems):
        mx, my, mc, _ = _place()
        return [
            _remote(geom.in_full(grad[w], chip, 1 - mc), theirs[w].at[chip], sems[0].at[4 * w + chip], sems[1].at[4 * w + chip], (mx, my, 1 - mc))
            for w, geom in enumerate(geoms)
            for chip in range(N_CHIPS)
        ]

    def start(grad, theirs, sems):
        for cp in copies(grad, theirs, sems):
            cp.start()

    def finish(grad, theirs, sems):
        for cp in copies(grad, theirs, sems):
            cp.wait_recv()
        for cp in copies(grad, theirs, sems):
            cp.wait_send()

    shapes = [jax.ShapeDtypeStruct((N_CHIPS,) + g.half, x.dtype) for g, x in zip(geoms, grads)]
    return _Hosted(grads, shapes, {}, _dma_sems(4 * len(grads), 2), start, finish)


def _half_tile(geom):
    return _tile(geom.hr, max(16, (1 << 18) // geom.cs), 16)


def _pair_add(grad, theirs, geom, core_arr, *, name):
    hl, hr, cs = geom.half
    tr = _half_tile(geom)
    stacked = geom.shape3[0] > 1

    def grad_map(chip, l, i, core_ref):
        layer = core_ref[0] * hl + l if stacked else 0
        row = (chip * (geom.rs // tr) if geom.axis == 1 else 0) + (0 if stacked else core_ref[0] * (hr // tr)) + i
        return layer, row, (chip if geom.axis == 2 else 0)

    def body(core_ref, g_ref, t_ref, o_ref):
        o_ref[0] = (g_ref[...].astype(F32) + t_ref[0].astype(F32)).astype(o_ref.dtype)

    blk = pl.BlockSpec((1, 1, tr, cs), lambda chip, l, i, core_ref: (chip, l, i, 0))
    return pl.pallas_call(
        body,
        name=name,
        grid_spec=pltpu.PrefetchScalarGridSpec(
            num_scalar_prefetch=1, grid=(N_CHIPS, hl, hr // tr), in_specs=[pl.BlockSpec((1, tr, cs), grad_map), blk], out_specs=blk
        ),
        out_shape=jax.ShapeDtypeStruct(theirs.shape, BF16),
        compiler_params=_params("parallel", "parallel", "parallel"),
    )(core_arr, grad, theirs)


def _chips_plan(parts):
    def copies(part, slots, sems):
        _, _, mc, chips = _place()
        return [
            _remote(part[w].at[2 * px + py], slots[w].at[k], sems[0].at[3 * w + k], sems[1].at[3 * w + k], (px, py, mc))
            for w in range(len(parts))
            for k, (px, py) in enumerate(chips)
        ]

    def start(part, slots, sems):
        for cp in copies(part, slots, sems):
            cp.start()

    def finish(part, slots, sems):
        for cp in copies(part, slots, sems):
            cp.wait_recv()
        for cp in copies(part, slots, sems):
            cp.wait_send()

    shapes = [jax.ShapeDtypeStruct((N_CHIPS - 1,) + p.shape[1:], p.dtype) for p in parts]
    return _Hosted(parts, shapes, {}, _dma_sems(3 * len(parts), 2), start, finish)


def _chip_sum(part, slots, geom, place_arr, *, name, stack=None):
    hl, hr, cs = geom.half
    tr = _half_tile(geom)

    def body(place_ref, own_ref, s0_ref, s1_ref, s2_ref, *rest):
        o_ref = rest[-1]
        o_ref[...] = ((own_ref[...].astype(F32) + s0_ref[...].astype(F32)) + s1_ref[...].astype(F32)) + s2_ref[...].astype(F32)

    def slot(k):
        return pl.BlockSpec((1, 1, tr, cs), lambda l, i, place_ref: (k, l, i, 0))

    in_specs = [pl.BlockSpec((1, 1, tr, cs), lambda l, i, place_ref: (place_ref[0], l, i, 0)), slot(0), slot(1), slot(2)]
    args = [place_arr, part, slots, slots, slots]
    aliases = {}
    if stack is None:
        out_spec = pl.BlockSpec((1, 1, tr, cs), lambda l, i, place_ref: (place_ref[1], l, i, 0))
        out_shape = jax.ShapeDtypeStruct((2,) + geom.half, F32)
    else:
        layers, layer, prev = stack
        assert hl == 1
        out_spec = pl.BlockSpec((1, 1, tr, cs), lambda l, i, place_ref: (layer, place_ref[1], i, 0))
        out_shape = jax.ShapeDtypeStruct((layers, 2, hr, cs), F32)
        if prev is not None:
            aliases = {len(args): 0}
            in_specs.append(ANY)
            args.append(prev)
    return pl.pallas_call(
        body,
        name=name,
        grid_spec=pltpu.PrefetchScalarGridSpec(num_scalar_prefetch=1, grid=(hl, hr // tr), in_specs=in_specs, out_specs=out_spec),
        out_shape=out_shape,
        input_output_aliases=aliases,
        compiler_params=_params("parallel", "parallel"),
    )(*args)


def _join_plan(boths, prefixes):
    def copies(given, both, sems):
        mx, my, mc, _ = _place()
        out, n = [], 0
        for w in range(len(boths)):
            for p in prefixes[w]:
                out.append(_remote(given[w].at[(*p, mc)], both[w].at[(*p, mc)], sems[0].at[n], sems[1].at[n], (mx, my, 1 - mc)))
                n += 1
        return out

    def arrivals(both, sems):
        mx, my, mc, _ = _place()
        out, n = [], 0
        for w in range(len(boths)):
            for p in prefixes[w]:
                got = both[w].at[(*p, 1 - mc)]
                out.append(_remote(got, got, sems[0].at[n], sems[1].at[n], (mx, my, 1 - mc)))
                n += 1
        return out

    def start(given, both, sems):
        for cp in copies(given, both, sems):
            cp.start()

    def finish(given, both, sems):
        for cp in arrivals(both, sems):
            cp.wait_recv()
        for cp in copies(given, both, sems):
            cp.wait_send()

    count = sum(len(p) for p in prefixes)
    shapes = [jax.ShapeDtypeStruct(b.shape, b.dtype) for b in boths]
    return _Hosted(boths, shapes, {w: w for w in range(len(boths))}, _dma_sems(count, 2), start, finish)


WEIGHTS = ("mod_w", "mod_b", "norm_mix", "norm_ffn", "pool_w", "pool_scale", "kv_mod_w", "kv_mod_b", "kv_in_norm", "w_dkv", "kv_norm",
           "w_uk", "w_uv", "w_kr", "w_dq", "q_norm", "w_uq", "w_o", "ffn_gate", "ffn_up", "ffn_down", "final_norm")
SMALL = ("mod_b", "kv_mod_b", "norm_mix", "norm_ffn", "kv_in_norm", "kv_norm", "q_norm", "final_norm")


def _rows(v):
    return v.reshape(-1, LANES)


def _pad_rows(a):
    return jnp.pad(a, ((0, (-a.shape[0]) % 8), (0, 0)))


def _vec(v):
    return v.reshape(1, -1)


def _step(x, c, positions, target, wts, mom, var):
    _, s, d = x.shape
    depth, n_a, n_b = wts["mod_w"].shape[0], wts["pool_w"].shape[0], wts["w_dq"].shape[0]
    assert n_b == 2 and n_a + n_b == depth
    heads = d // V_DIM
    kvr, qr = wts["w_dkv"].shape[1], wts["w_dq"].shape[2]
    ffn = wts["ffn_gate"].shape[2] * N_CHIPS
    pool_c = d // len(POOL_WINDOWS)
    nmod, nkv = N_MOD * d, 2 * d
    mx, my, mc = lax.axis_index("x"), lax.axis_index("y"), lax.axis_index("c")
    chip, dev = 2 * mx + my, 4 * mx + 2 * my + mc
    xs, tgt = x[0], target[0]

    inv_freq = 1.0 / (ROPE_THETA ** (jnp.arange(0, ROPE_DIM, 2, dtype=F32) / ROPE_DIM))
    ang = positions[0].astype(F32)[:, None] * inv_freq
    cos, sin, zero = jnp.cos(ang), jnp.sin(ang), jnp.zeros((s, LANES - ROPE_DIM), F32)
    cos_t = jnp.concatenate([cos, cos, zero], axis=1)
    sin_fwd = jnp.concatenate([-sin, sin, zero], axis=1)
    sin_bwd = jnp.concatenate([sin, -sin, zero], axis=1)

    c_rows, ps_rows = d // LANES, n_a * (d // N_CHIPS) // LANES
    cond = _allgather8(_pad_rows(jnp.concatenate([_rows(c), _rows(wts["pool_scale"])])), name="gather_cond")
    c_all = cond[:, :c_rows].reshape(N_DEV, d)
    pool_scale = cond[0::2, c_rows : c_rows + ps_rows].reshape(N_CHIPS, n_a, d // N_CHIPS).transpose(1, 0, 2).reshape(n_a, d)
    sc16 = _elementwise(_silu, [jnp.pad(c_all, ((0, 16 - N_DEV), (0, 0)))], [F32], name="silu_cond")[0]

    mod_bias = lax.dynamic_slice_in_dim(wts["mod_b"], chip * (nmod // N_CHIPS), nmod // N_CHIPS, axis=1)[:, None, :]
    kv_bias = lax.dynamic_slice_in_dim(wts["kv_mod_b"], chip * (nkv // N_CHIPS), nkv // N_CHIPS).reshape(1, 1, -1)
    mod_part = _tp_fwd(sc16, wts["mod_w"], mod_bias, name="mod_fwd")
    kv_part = _tp_fwd(sc16, wts["kv_mod_w"][None], kv_bias, name="kv_mod_fwd")
    part = jnp.concatenate([mod_part[i, :N_DEV] for i in range(depth)] + [kv_part[0, :N_DEV]], axis=1)
    ncol = part.shape[1]
    gathered = _allgather8(_pad_rows(_rows(part)), name="gather_mods")
    gathered = gathered[0::2, : N_DEV * ncol // LANES].reshape(N_CHIPS, N_DEV, ncol)
    mine = lax.dynamic_index_in_dim(gathered, dev, axis=1, keepdims=False)
    per = nmod // N_CHIPS
    mods = [mine[:, i * per : (i + 1) * per].reshape(N_MOD, 1, d) for i in range(depth)]
    kv_shift, kv_scale = mine[:, depth * per :].reshape(2, 1, d)

    mixer_names = ("pool_w", "w_dkv", "w_uk", "w_uv", "w_kr", "w_dq", "w_uq", "w_o")
    ffn_names = ("ffn_gate", "ffn_up", "ffn_down")
    geoms = {
        "pool_w": _Geom((n_a * len(POOL_WINDOWS), pool_c, pool_c), 1),
        "w_dkv": _Geom((1, d, kvr), 1),
        "w_uk": _Geom((1, kvr, heads * NOPE_DIM), 2),
        "w_uv": _Geom((1, kvr, heads * V_DIM), 2),
        "w_kr": _Geom((1, d, ROPE_DIM), 1),
        "w_dq": _Geom((n_b, d, qr), 1),
        "w_uq": _Geom((n_b, qr, heads * (NOPE_DIM + ROPE_DIM)), 2),
        "w_o": _Geom((n_b, d, d), 1),
        "ffn_gate": _Geom((1, d, ffn), 2),
        "ffn_up": _Geom((1, d, ffn), 2),
        "ffn_down": _Geom((1, ffn, d), 1),
    }
    mixer_geoms = [geoms[n] for n in mixer_names]
    ffn_geoms = [geoms[n] for n in ffn_names]
    chip_arr, core_arr, place_arr = chip.reshape(1), mc.reshape(1), jnp.stack([chip, mc])
    placed = [_place_shard(wts[n].reshape(geoms[n].shard), geoms[n], chip_arr, name="place_" + n) for n in mixer_names]
    placed_ffn = [[_place_shard(wts[n], geoms[n], chip_arr, layer=i, name="place_" + n) for n in ffn_names] for i in range(depth)]
    first = _run(_gather_plan(placed + placed_ffn[0], mixer_geoms + ffn_geoms), name="gather_first")
    full = dict(zip(mixer_names, first))
    ffn_w = [None] * depth
    ffn_w[0] = first[len(mixer_names) :]

    pool_w = full["pool_w"].reshape(n_a, len(POOL_WINDOWS), pool_c, pool_c)
    w_uq = full["w_uq"].reshape(n_b, qr, heads, NOPE_DIM + ROPE_DIM)
    w_q = jnp.pad(w_uq, ((0, 0), (0, 0), (0, 0), (0, HEAD_PAD - NOPE_DIM - ROPE_DIM))).reshape(n_b, qr, heads * HEAD_PAD)
    w_ukv = jnp.stack([full["w_uk"].reshape(kvr, heads, NOPE_DIM), full["w_uv"].reshape(kvr, heads, V_DIM)], axis=2).reshape(kvr, heads * HEAD_PAD)
    w_dkvkr = jnp.concatenate([full["w_dkv"][0], full["w_kr"][0], jnp.zeros((d, LANES - ROPE_DIM), BF16)], axis=1)
    w_dq, w_o = full["w_dq"], full["w_o"]

    norm_mix, norm_ffn = wts["norm_mix"], wts["norm_ffn"]
    saved = []
    cur, pending = xs, None
    kv_side = None
    for i in range(depth):
        shift_m, scale_m, gate_m, shift_f, scale_f, gate_f = mods[i]
        h1_dtype = F32 if i < n_a else BF16
        if pending is None:
            x0 = cur
            h1 = _norm_fwd(x0, _vec(norm_mix[i]), scale=scale_m, shift=shift_m, out_dtype=h1_dtype, name="norm_mix_first")
        else:
            x0, h1 = _norm_fwd(cur, _vec(norm_mix[i]), scale=scale_m, shift=shift_m, y=pending[0], gate=pending[1], out_dtype=h1_dtype, name="norm_mix")
        lay = {"x0": x0, "h1": h1}
        if i == n_a:
            h_kv = _norm_fwd(x0, _vec(wts["kv_in_norm"]), scale=kv_scale, shift=kv_shift, name="norm_kv_in")
            pre = _mm(h_kv, w_dkvkr, name="kv_down", tn=kvr + LANES)
            ckv_pre, kr_pre = pre[:, :kvr], pre[:, kvr:]
            ckv = _norm_fwd(ckv_pre, _vec(wts["kv_norm"]), name="norm_kv")
            kv = _mm(ckv, w_ukv, out_dtype=BF16, name="kv_up")
            keys = _build_keys(kv, kr_pre, cos_t, sin_fwd, name="build_keys")
            t_attn = _attn_tile(s)
            v_t = kv.reshape(s // t_attn, t_attn, heads, 2, V_DIM)[:, :, :, 1].transpose(2, 0, 3, 1)
            kv_side = {"h_kv": h_kv, "ckv_pre": ckv_pre, "ckv": ckv, "kv": kv, "keys": keys, "v_t": v_t, "x0": x0}
        if i < n_a:
            pooled = _pool(h1, transpose=False, out_dtype=BF16, name="pool_fwd")
            y_pre = _gmm(pooled, pool_w[i], mode="nn", out_dtype=F32, name="pool_mix")
            gate_eff = gate_m * _vec(pool_scale[i])
            lay.update(pooled=pooled)
        else:
            l = i - n_a
            cq_pre = _mm(h1, w_dq[l], name="q_down")
            cq = _norm_fwd(cq_pre, _vec(wts["q_norm"][l]), name="norm_q")
            q = _rope_heads(_mm(cq, w_q[l], name="q_up"), cos_t, sin_fwd, out_dtype=BF16, name="rope_q")
            o_t, lse = _attn_fwd(q, kv_side["keys"], kv_side["v_t"], name="attn_fwd")
            o = o_t.T
            y_pre = _mm(o, w_o[l], name="attn_out")
            gate_eff = gate_m
            lay.update(cq_pre=cq_pre, cq=cq, q=q, o=o, lse=lse)
        x1, h2 = _norm_fwd(x0, _vec(norm_ffn[i]), scale=scale_f, shift=shift_f, y=y_pre, gate=gate_eff, name="norm_ffn")
        w_gate, w_up, w_down = ffn_w[i]
        if i + 1 < depth:
            (a, b, z), next_in = _ffn_in(h2, w_gate, w_up, 0, hosted=_gather_plan(placed_ffn[i + 1][:2], ffn_geoms[:2]), name="ffn_in")
            f, next_down = _mm(z, w_down, b_idx=0, hosted=_gather_plan(placed_ffn[i + 1][2:], ffn_geoms[2:]), name="ffn_down")
            ffn_w[i + 1] = next_in + next_down
        else:
            (a, b, z), _ = _ffn_in(h2, w_gate, w_up, 0, name="ffn_in_last")
            f = _mm(z, w_down, b_idx=0, name="ffn_down_last")
        lay.update(y_pre=y_pre, gate_eff=gate_eff, x1=x1, h2=h2, a=a, b=b, z=z, f=f)
        saved.append(lay)
        cur, pending = x1, (f, gate_f)

    dx, final_stats, loss_tile = _loss_bwd(cur, pending[0], pending[1], _vec(wts["final_norm"]), tgt, name="loss")
    loss = lax.psum(loss_tile[0, 0], ("x", "y", "c"))

    ffn_both = [None] * len(ffn_names)
    in_flight = None

    def sum_chips(layer, parts, slots):
        for w, n in enumerate(ffn_names):
            ffn_both[w] = _chip_sum(parts[w], slots[w], ffn_geoms[w], place_arr, stack=(depth, layer, ffn_both[w]), name="chip_sum_" + n)

    g_wo, g_wq, g_wdq = [None] * n_b, [None] * n_b, [None] * n_b
    g_pool = [None] * n_a
    dmods = [None] * depth
    g_norm_mix, g_norm_ffn, g_q_norm, g_pool_scale = [None] * depth, [None] * depth, [None] * n_b, [None] * n_a
    dk_layers, dv_layers = [None] * n_b, [None] * n_b
    for i in reversed(range(depth)):
        lay = saved[i]
        shift_m, scale_m, gate_m, shift_f, scale_f, gate_f = mods[i]
        df, sums_gf = _gate_bwd(dx, lay["f"], gate_f, name="gate_bwd")
        w_gate, w_up, w_down = ffn_w[i]
        g_down = _mm(lay["z"], df, ta=True, out_dtype=BF16, name="ffn_down_dw")
        if in_flight is None:
            (da, db), _ = _ffn_down_bwd(df, w_down, 0, lay["a"], lay["b"], name="ffn_down_bwd_top")
            (dh2,), _ = _ffn_in_dx(da, db, w_gate, w_up, 0, name="ffn_in_dx_top")
            (g_gate, g_up), _ = _ffn_in_dw(lay["h2"], da, db, name="ffn_in_dw_top")
        else:
            (da, db), got_down = _ffn_down_bwd(df, w_down, 0, lay["a"], lay["b"], hosted=_chips_plan(in_flight[2:]), name="ffn_down_bwd")
            (dh2,), got_gate = _ffn_in_dx(da, db, w_gate, w_up, 0, hosted=_chips_plan(in_flight[:1]), name="ffn_in_dx")
            (g_gate, g_up), got_up = _ffn_in_dw(lay["h2"], da, db, hosted=_chips_plan(in_flight[1:2]), name="ffn_in_dw")
            sum_chips(i + 1, in_flight, got_gate + got_up + got_down)
        ffn_grads = [g_gate[None], g_up[None], g_down[None]]
        theirs = _run(_pair_plan(ffn_grads, ffn_geoms), name="reduce_pair_ffn")
        in_flight = [_pair_add(g, th, geom, core_arr, name="pair_add_" + n) for g, th, geom, n in zip(ffn_grads, theirs, ffn_geoms, ffn_names)]
        dx1, sums_f = _norm_bwd(lay["x1"], _vec(norm_ffn[i]), dh2, scale=scale_f, resid=dx, name="norm_ffn_bwd")
        dyp, sums_gm = _gate_bwd(dx1, lay["y_pre"], lay["gate_eff"], name="gate_bwd")
        if i < n_a:
            g_pool[i] = _gmm(lay["pooled"], dyp, mode="tn", out_dtype=BF16, name="pool_mix_dw")
            dd = _gmm(dyp, pool_w[i], mode="nt", out_dtype=F32, name="pool_mix_dx")
            dh1 = _pool(dd, transpose=True, out_dtype=F32, name="pool_bwd")
            dgate_m = sums_gm[0] * pool_scale[i]
            g_pool_scale[i] = sums_gm[0] * gate_m[0]
        else:
            l = i - n_a
            do = _mm(dyp, w_o[l], tb=True, out_dtype=BF16, name="attn_out_dx")
            g_wo[l] = _mm(lay["o"], dyp, ta=True, out_dtype=BF16, name="attn_out_dw")
            dq, delta = _attn_bwd_q(lay["q"], kv_side["keys"], kv_side["kv"], lay["o"], do, lay["lse"].reshape(heads, s, 1), name="attn_bwd_q")
            dk_layers[l], dv_layers[l] = _attn_bwd_kv(
                lay["q"], kv_side["keys"], kv_side["kv"], do, lay["lse"], delta.reshape(lay["lse"].shape), name="attn_bwd_kv"
            )
            dq_pre = _rope_heads(dq, cos_t, sin_bwd, out_dtype=BF16, name="rope_q_bwd")
            dcq = _mm(dq_pre, w_q[l], tb=True, name="q_up_dx")
            g_wq[l] = _mm(lay["cq"], dq_pre, ta=True, out_dtype=BF16, name="q_up_dw")
            dcq_pre, sums_q = _norm_bwd(lay["cq_pre"], _vec(wts["q_norm"][l]), dcq, name="norm_q_bwd")
            g_q_norm[l] = sums_q[2]
            dh1 = _mm(dcq_pre, w_dq[l], tb=True, name="q_down_dx")
            g_wdq[l] = _mm(lay["h1"], dcq_pre, ta=True, out_dtype=BF16, name="q_down_dw")
            dgate_m = sums_gm[0]
        dx, sums_m = _norm_bwd(lay["x0"], _vec(norm_mix[i]), dh1, scale=scale_m, resid=dx1, name="norm_mix_bwd")
        if i == n_a:
            dkv, dkr_pre = _keys_bwd(dk_layers[0], dk_layers[1], dv_layers[0], dv_layers[1], cos_t, sin_bwd, name="keys_bwd")
            dckv = _mm(dkv, w_ukv, tb=True, name="kv_up_dx")
            g_ukv = _mm(kv_side["ckv"], dkv, ta=True, out_dtype=BF16, name="kv_up_dw")
            dckv_pre, sums_kvn = _norm_bwd(kv_side["ckv_pre"], _vec(wts["kv_norm"]), dckv, name="norm_kv_bwd")
            dpre = jnp.concatenate([dckv_pre, dkr_pre], axis=1)
            dh_kv = _mm(dpre, w_dkvkr, tb=True, name="kv_down_dx")
            g_dkvkr = _mm(kv_side["h_kv"], dpre, ta=True, out_dtype=BF16, name="kv_down_dw", tn=kvr + LANES)
            dx, sums_kv = _norm_bwd(lay["x0"], _vec(wts["kv_in_norm"]), dh_kv, scale=kv_scale, resid=dx, name="norm_kv_in_bwd")
        dmods[i] = jnp.concatenate([sums_m[0], sums_m[1], dgate_m, sums_f[0], sums_f[1], sums_gf[0]])
        g_norm_mix[i], g_norm_ffn[i] = sums_m[2], sums_f[2]
    grad_x = dx[None]

    g_ukv = g_ukv.reshape(kvr, heads, 2, NOPE_DIM)
    grad_full = {
        "pool_w": jnp.stack(g_pool).reshape(geoms["pool_w"].shape3),
        "w_dkv": g_dkvkr[None, :, :kvr],
        "w_uk": g_ukv[:, :, 0].reshape(1, kvr, heads * NOPE_DIM),
        "w_uv": g_ukv[:, :, 1].reshape(1, kvr, heads * V_DIM),
        "w_kr": g_dkvkr[None, :, kvr : kvr + ROPE_DIM],
        "w_dq": jnp.stack(g_wdq),
        "w_uq": jnp.stack(g_wq).reshape(n_b, qr, heads, HEAD_PAD)[..., : NOPE_DIM + ROPE_DIM].reshape(geoms["w_uq"].shape3),
        "w_o": jnp.stack(g_wo),
    }

    small_grads = {
        "mod_b": jnp.concatenate(dmods),
        "kv_mod_b": jnp.concatenate([sums_kv[0], sums_kv[1]]),
        "norm_mix": jnp.concatenate(g_norm_mix),
        "norm_ffn": jnp.concatenate(g_norm_ffn),
        "kv_in_norm": sums_kv[2],
        "kv_norm": sums_kvn[2],
        "q_norm": jnp.concatenate(g_q_norm),
        "final_norm": final_stats[0],
    }
    packed = jnp.concatenate([small_grads[n] for n in SMALL] + g_pool_scale)
    small_rows = sum(wts[n].size for n in SMALL) // LANES
    every = _allgather8(_pad_rows(_rows(packed)), name="gather_small_grads")
    summed = _sum_devices(every, name="sum_small_grads")

    mod_rows = depth * nmod // LANES
    dm_all = every[:, :mod_rows].reshape(N_DEV, depth, nmod)
    dm = lax.dynamic_slice_in_dim(dm_all, chip * per, per, axis=2).transpose(1, 0, 2)
    dm = jnp.pad(dm, ((0, 0), (0, 16 - N_DEV), (0, 0)))
    dkvm_all = every[:, mod_rows : mod_rows + nkv // LANES].reshape(N_DEV, nkv)
    dkvm = jnp.pad(lax.dynamic_slice_in_dim(dkvm_all, chip * (nkv // N_CHIPS), nkv // N_CHIPS, axis=1), ((0, 16 - N_DEV), (0, 0)))[None]
    results = {}
    results["mod_w"] = _tp_adamw(sc16, dm, wts["mod_w"], mom["mod_w"], var["mod_w"], name="mod_w_update")
    results["kv_mod_w"] = [
        r[0] for r in _tp_adamw(sc16, dkvm, wts["kv_mod_w"][None], mom["kv_mod_w"][None], var["kv_mod_w"][None], name="kv_mod_w_update")
    ]

    ps_grad = lax.dynamic_slice_in_dim(summed[small_rows : small_rows + n_a * d // LANES].reshape(n_a, d), chip * (d // N_CHIPS), d // N_CHIPS, axis=1)
    small_names = SMALL + ("pool_scale",)

    def pack_small(tree):
        return _pad_rows(jnp.concatenate([_rows(tree[n]) for n in small_names]))

    g_small = _pad_rows(jnp.concatenate([summed[:small_rows], _rows(ps_grad)]))
    small_out = _adamw(pack_small(wts), g_small, pack_small(mom), pack_small(var), name="small_update")
    row = 0
    for n in small_names:
        nrow = wts[n].size // LANES
        results[n] = [r[row : row + nrow].reshape(wts[n].shape) for r in small_out]
        row += nrow

    theirs = _run(_pair_plan([grad_full[n] for n in mixer_names], mixer_geoms), name="reduce_pair")
    pair_sum = [_pair_add(grad_full[n], th, geoms[n], core_arr, name="pair_add_" + n) for n, th in zip(mixer_names, theirs)]
    slots = _run(_chips_plan(pair_sum + in_flight), name="reduce_chips")
    boths = [_chip_sum(ps, sl, geoms[n], place_arr, name="chip_sum_" + n) for n, ps, sl in zip(mixer_names, pair_sum, slots)]
    sum_chips(0, in_flight, slots[len(mixer_names) :])
    prefixes = [[()]] * len(mixer_names) + [[(layer,) for layer in range(depth)]] * len(ffn_names)
    joined = _run(_join_plan(boths + ffn_both, prefixes), name="join_pair")
    for n, both in zip(mixer_names + ffn_names, joined):
        cs = geoms[n].cs
        out = _adamw(wts[n].reshape(-1, cs), both.reshape(-1, cs), mom[n].reshape(-1, cs), var[n].reshape(-1, cs), name="update_" + n)
        results[n] = [r.reshape(wts[n].shape) for r in out]

    outs = [loss, grad_x]
    for k in range(4):
        outs += [results[n][k] for n in WEIGHTS]
    return tuple(outs)


def kernel(x, c, positions, mod_w, mod_b, norm_mix, norm_ffn, pool_w, pool_scale, kv_mod_w, kv_mod_b, kv_in_norm, w_dkv, kv_norm, w_uk, w_uv, w_kr, w_dq, q_norm, w_uq, w_o, ffn_gate, ffn_up, ffn_down, final_norm, loss_target, m_mod_w, m_mod_b, m_norm_mix, m_norm_ffn, m_pool_w, m_pool_scale, m_kv_mod_w, m_kv_mod_b, m_kv_in_norm, m_w_dkv, m_kv_norm, m_w_uk, m_w_uv, m_w_kr, m_w_dq, m_q_norm, m_w_uq, m_w_o, m_ffn_gate, m_ffn_up, m_ffn_down, m_final_norm, v_mod_w, v_mod_b, v_norm_mix, v_norm_ffn, v_pool_w, v_pool_scale, v_kv_mod_w, v_kv_mod_b, v_kv_in_norm, v_w_dkv, v_kv_norm, v_w_uk, v_w_uv, v_w_kr, v_w_dq, v_q_norm, v_w_uq, v_w_o, v_ffn_gate, v_ffn_up, v_ffn_down, v_final_norm):
    given = dict(locals())
    wts = {n: given[n] for n in WEIGHTS}
    mom = {n: given["m_" + n] for n in WEIGHTS}
    var = {n: given["v_" + n] for n in WEIGHTS}
    return _step(x, c, positions, loss_target, wts, mom, var)
```

```python
import functools

import jax
import jax.numpy as jnp
from jax import lax
from jax.experimental import pallas as pl
from jax.experimental.pallas import tpu as pltpu

F32 = jnp.float32
BF16 = jnp.bfloat16
MESH = pl.DeviceIdType.MESH
ANY = pl.BlockSpec(memory_space=pl.ANY)

NORM_EPS = 1e-6
POOL_WINDOWS = (2, 4, 8, 16)
NOPE_DIM = 128
ROPE_DIM = 64
V_DIM = 128
HEAD_PAD = 256
SM_SCALE = (NOPE_DIM + ROPE_DIM) ** -0.5
ROPE_THETA = 10000.0
N_MOD = 6
ADAM_LR, ADAM_B1, ADAM_B2, ADAM_EPS, ADAM_WD, ADAM_STEP = 0.001, 0.9, 0.999, 1e-08, 0.01, 10
N_CHIPS = 4
N_DEV = 8
LANES = 128
HALO = 128
VMEM_LIMIT = 48 * 1024 * 1024


def _tile(dim, pref, align):
    if dim <= pref:
        return dim
    t = (pref // align) * align
    while t >= align:
        if dim % t == 0:
            return t
        t -= align
    return dim


def _params(*sem):
    return pltpu.CompilerParams(dimension_semantics=sem, vmem_limit_bytes=VMEM_LIMIT)


class _Hosted:
    def __init__(self, args, out_shapes, aliases, sem_shapes, start, finish):
        self.args, self.out_shapes, self.aliases, self.sem_shapes = list(args), list(out_shapes), dict(aliases), list(sem_shapes)
        self.start, self.finish = start, finish


def _call(body, *, name, grid, in_specs, out_specs, out_shape, args, sem, scratch_shapes=(), hosted=None):
    n_in, n_out, n_scr = len(args), len(out_shape), len(scratch_shapes)
    if hosted is None:
        outs = pl.pallas_call(
            body, name=name, grid=grid, in_specs=list(in_specs), out_specs=list(out_specs), out_shape=list(out_shape),
            scratch_shapes=list(scratch_shapes), compiler_params=_params(*sem),
        )(*args)
        return list(outs), []
    n_hin, n_hout = len(hosted.args), len(hosted.out_shapes)

    def carrying(*refs):
        own_in, their_in = refs[:n_in], refs[n_in : n_in + n_hin]
        refs = refs[n_in + n_hin :]
        own_out, their_out = refs[:n_out], refs[n_out : n_out + n_hout]
        refs = refs[n_out + n_hout :]
        own_scratch, sems = refs[:n_scr], refs[n_scr:]
        ids = [pl.program_id(axis) for axis in range(len(grid))]
        first = functools.reduce(jnp.logical_and, [i == 0 for i in ids])
        last = functools.reduce(jnp.logical_and, [i == size - 1 for i, size in zip(ids, grid)])

        @pl.when(first)
        def _():
            hosted.start(their_in, their_out, sems)

        body(*own_in, *own_out, *own_scratch)

        @pl.when(last)
        def _():
            hosted.finish(their_in, their_out, sems)

    outs = pl.pallas_call(
        carrying,
        name=name,
        grid=grid,
        in_specs=list(in_specs) + [ANY] * n_hin,
        out_specs=list(out_specs) + [ANY] * n_hout,
        out_shape=list(out_shape) + hosted.out_shapes,
        input_output_aliases={n_in + i: n_out + o for i, o in hosted.aliases.items()},
        scratch_shapes=list(scratch_shapes) + hosted.sem_shapes,
        compiler_params=_params(*["arbitrary"] * len(grid)),
    )(*args, *hosted.args)
    return list(outs[:n_out]), list(outs[n_out:])


def _merge(plans):
    if len(plans) == 1:
        return plans[0]
    args, out_shapes, aliases, sem_shapes, spans = [], [], {}, [], []
    for p in plans:
        spans.append((len(args), len(out_shapes), len(sem_shapes)))
        aliases.update({len(args) + i: len(out_shapes) + o for i, o in p.aliases.items()})
        args, out_shapes, sem_shapes = args + p.args, out_shapes + p.out_shapes, sem_shapes + p.sem_shapes

    def each(method, ins, outs, sems):
        for p, (a0, o0, s0) in zip(plans, spans):
            getattr(p, method)(ins[a0 : a0 + len(p.args)], outs[o0 : o0 + len(p.out_shapes)], sems[s0 : s0 + len(p.sem_shapes)])

    return _Hosted(args, out_shapes, aliases, sem_shapes, functools.partial(each, "start"), functools.partial(each, "finish"))


def _run(plan, *, name):
    n_in, n_out = len(plan.args), len(plan.out_shapes)

    def body(*refs):
        ins, outs, sems = refs[:n_in], refs[n_in : n_in + n_out], refs[n_in + n_out :]
        plan.start(ins, outs, sems)
        plan.finish(ins, outs, sems)

    return pl.pallas_call(
        body, name=name, in_specs=[ANY] * n_in, out_specs=[ANY] * n_out, out_shape=plan.out_shapes,
        input_output_aliases=plan.aliases, scratch_shapes=plan.sem_shapes,
    )(*plan.args)


def _mm(a, b, *, name, ta=False, tb=False, out_dtype=F32, b_idx=None, hosted=None, tm=1024, tn=1024, tk=None):
    m, k = (a.shape[1], a.shape[0]) if ta else a.shape
    b2 = b.shape if b_idx is None else b.shape[1:]
    kb, n = (b2[1], b2[0]) if tb else b2
    assert k == kb, (a.shape, b.shape, ta, tb)
    tm = _tile(m, tm, LANES)
    tn = _tile(n, tn, LANES)
    tk = _tile(k, 2048 if tk is None else tk, LANES)
    nk = k // tk
    dims = (((0 if ta else 1,), (1 if tb else 0,)), ((), ()))

    def body(a_ref, b_ref, o_ref, *acc):
        part = lax.dot_general(a_ref[...].astype(BF16), b_ref[...].astype(BF16), dims, preferred_element_type=F32)
        if nk == 1:
            o_ref[...] = part.astype(o_ref.dtype)
        else:
            acc_ref = acc[0]
            step = pl.program_id(2)

            @pl.when(step == 0)
            def _():
                acc_ref[...] = part

            @pl.when(step > 0)
            def _():
                acc_ref[...] += part

            @pl.when(step == nk - 1)
            def _():
                o_ref[...] = acc_ref[...].astype(o_ref.dtype)

    a_spec = pl.BlockSpec((tk, tm), lambda i, j, s: (s, i)) if ta else pl.BlockSpec((tm, tk), lambda i, j, s: (i, s))
    if b_idx is None:
        b_spec = pl.BlockSpec((tn, tk), lambda i, j, s: (j, s)) if tb else pl.BlockSpec((tk, tn), lambda i, j, s: (s, j))
    elif tb:
        b_spec = pl.BlockSpec((None, tn, tk), lambda i, j, s: (b_idx, j, s))
    else:
        b_spec = pl.BlockSpec((None, tk, tn), lambda i, j, s: (b_idx, s, j))
    outs, carried = _call(
        body,
        name=name,
        grid=(m // tm, n // tn, nk),
        in_specs=[a_spec, b_spec],
        out_specs=[pl.BlockSpec((tm, tn), lambda i, j, s: (i, j))],
        out_shape=[jax.ShapeDtypeStruct((m, n), out_dtype)],
        args=[a, b],
        sem=("parallel", "parallel", "arbitrary"),
        scratch_shapes=[pltpu.VMEM((tm, tn), F32)] if nk > 1 else [],
        hosted=hosted,
    )
    return outs[0] if hosted is None else (outs[0], carried)


def _tp_fwd(sc16, w, bias, *, name):
    nl, d, n = w.shape
    tn = _tile(n, 512, LANES)

    def body(sc_ref, w_ref, b_ref, o_ref):
        o_ref[0] = jnp.dot(sc_ref[...].astype(BF16), w_ref[0].astype(BF16), preferred_element_type=F32) + b_ref[0]

    return pl.pallas_call(
        body,
        name=name,
        grid=(nl, n // tn),
        in_specs=[
            pl.BlockSpec((16, d), lambda l, j: (0, 0)),
            pl.BlockSpec((1, d, tn), lambda l, j: (l, 0, j)),
            pl.BlockSpec((1, 1, tn), lambda l, j: (l, 0, j)),
        ],
        out_specs=pl.BlockSpec((1, 16, tn), lambda l, j: (l, 0, j)),
        out_shape=jax.ShapeDtypeStruct((nl, 16, n), F32),
        compiler_params=_params("parallel", "parallel"),
    )(sc16, w, bias)


_NT = (((1,), (1,)), ((), ()))
_TN = (((0,), (0,)), ((), ()))


def _silu_parts(a):
    sig = jax.nn.sigmoid(a)
    return a * sig, sig * (1.0 + a * (1.0 - sig))


def _ffn_in(h, w_gate, w_up, layer, *, name, hosted=None):
    s, d = h.shape
    f = w_gate.shape[2]
    tm, tn = _tile(s, 1024, LANES), _tile(f, 512, LANES)

    def body(h_ref, g_ref, u_ref, a_ref, b_ref, z_ref):
        hv = h_ref[...]
        a = jnp.dot(hv, g_ref[...], preferred_element_type=F32)
        b = jnp.dot(hv, u_ref[...], preferred_element_type=F32)
        a_ref[...] = a.astype(a_ref.dtype)
        b_ref[...] = b.astype(b_ref.dtype)
        z_ref[...] = (_silu_parts(a)[0] * b).astype(z_ref.dtype)

    w_spec = pl.BlockSpec((None, d, tn), lambda i, j: (layer, 0, j))
    out = pl.BlockSpec((tm, tn), lambda i, j: (i, j))
    return _call(
        body,
        name=name,
        grid=(s // tm, f // tn),
        in_specs=[pl.BlockSpec((tm, d), lambda i, j: (i, 0)), w_spec, w_spec],
        out_specs=[out] * 3,
        out_shape=[jax.ShapeDtypeStruct((s, f), BF16)] * 3,
        args=[h, w_gate, w_up],
        sem=("parallel", "parallel"),
        hosted=hosted,
    )


def _ffn_down_bwd(df, w_down, layer, a, b, *, name, hosted=None):
    s, d = df.shape
    f = w_down.shape[1]
    tm, tn = _tile(s, 1024, LANES), _tile(f, 512, LANES)

    def body(df_ref, w_ref, a_ref, b_ref, da_ref, db_ref):
        dz = lax.dot_general(df_ref[...], w_ref[...], _NT, preferred_element_type=F32)
        silu, dsilu = _silu_parts(a_ref[...].astype(F32))
        da_ref[...] = (dz * b_ref[...].astype(F32) * dsilu).astype(da_ref.dtype)
        db_ref[...] = (dz * silu).astype(db_ref.dtype)

    blk = pl.BlockSpec((tm, tn), lambda i, j: (i, j))
    return _call(
        body,
        name=name,
        grid=(s // tm, f // tn),
        in_specs=[pl.BlockSpec((tm, d), lambda i, j: (i, 0)), pl.BlockSpec((None, tn, d), lambda i, j: (layer, j, 0)), blk, blk],
        out_specs=[blk, blk],
        out_shape=[jax.ShapeDtypeStruct((s, f), BF16)] * 2,
        args=[df, w_down, a, b],
        sem=("parallel", "parallel"),
        hosted=hosted,
    )


def _ffn_in_dx(da, db, w_gate, w_up, layer, *, name, hosted=None):
    s, f = da.shape
    d = w_gate.shape[1]
    tm, tn, tk = _tile(s, 512, LANES), _tile(d, 1024, LANES), _tile(f, 3072, LANES)
    nk = f // tk

    def body(da_ref, db_ref, g_ref, u_ref, o_ref, acc_ref):
        part = lax.dot_general(da_ref[...], g_ref[...], _NT, preferred_element_type=F32)
        part = part + lax.dot_general(db_ref[...], u_ref[...], _NT, preferred_element_type=F32)
        step = pl.program_id(2)

        @pl.when(step == 0)
        def _():
            acc_ref[...] = part

        @pl.when(step > 0)
        def _():
            acc_ref[...] += part

        @pl.when(step == nk - 1)
        def _():
            o_ref[...] = acc_ref[...]

    x_spec = pl.BlockSpec((tm, tk), lambda i, j, k: (i, k))
    w_spec = pl.BlockSpec((None, tn, tk), lambda i, j, k: (layer, j, k))
    return _call(
        body,
        name=name,
        grid=(s // tm, d // tn, nk),
        in_specs=[x_spec, x_spec, w_spec, w_spec],
        out_specs=[pl.BlockSpec((tm, tn), lambda i, j, k: (i, j))],
        out_shape=[jax.ShapeDtypeStruct((s, d), F32)],
        args=[da, db, w_gate, w_up],
        sem=("parallel", "parallel", "arbitrary"),
        scratch_shapes=[pltpu.VMEM((tm, tn), F32)],
        hosted=hosted,
    )


def _ffn_in_dw(h, da, db, *, name, hosted=None):
    s, d = h.shape
    f = da.shape[1]
    tm, tn, tk = _tile(d, 1024, LANES), _tile(f, 512, LANES), _tile(s, 4096, LANES)
    nk = s // tk

    def body(h_ref, da_ref, db_ref, g_ref, u_ref, *acc):
        hv = h_ref[...]
        pg = lax.dot_general(hv, da_ref[...], _TN, preferred_element_type=F32)
        pu = lax.dot_general(hv, db_ref[...], _TN, preferred_element_type=F32)
        if nk == 1:
            g_ref[...] = pg.astype(g_ref.dtype)
            u_ref[...] = pu.astype(u_ref.dtype)
            return
        g_acc, u_acc = acc
        step = pl.program_id(2)

        @pl.when(step == 0)
        def _():
            g_acc[...] = pg
            u_acc[...] = pu

        @pl.when(step > 0)
        def _():
            g_acc[...] += pg
            u_acc[...] += pu

        @pl.when(step == nk - 1)
        def _():
            g_ref[...] = g_acc[...].astype(g_ref.dtype)
            u_ref[...] = u_acc[...].astype(u_ref.dtype)

    y_spec = pl.BlockSpec((tk, tn), lambda i, j, k: (k, j))
    out = pl.BlockSpec((tm, tn), lambda i, j, k: (i, j))
    return _call(
        body,
        name=name,
        grid=(d // tm, f // tn, nk),
        in_specs=[pl.BlockSpec((tk, tm), lambda i, j, k: (k, i)), y_spec, y_spec],
        out_specs=[out, out],
        out_shape=[jax.ShapeDtypeStruct((d, f), BF16)] * 2,
        args=[h, da, db],
        sem=("parallel", "parallel", "arbitrary"),
        scratch_shapes=[pltpu.VMEM((tm, tn), F32)] * 2 if nk > 1 else [],
        hosted=hosted,
    )


def _gmm(a, w, *, name, mode, out_dtype):
    s = a.shape[0]
    g = len(POOL_WINDOWS)
    c = a.shape[1] // g
    tr = _tile(s, 1024, LANES)
    n_row = s // tr

    if mode == "tn":

        def body(a_ref, b_ref, o_ref, acc_ref):
            part = lax.dot_general(a_ref[...].astype(BF16), b_ref[...].astype(BF16), (((0,), (0,)), ((), ())), preferred_element_type=F32)

            @pl.when(pl.program_id(1) == 0)
            def _():
                acc_ref[...] = part

            @pl.when(pl.program_id(1) > 0)
            def _():
                acc_ref[...] += part

            @pl.when(pl.program_id(1) == n_row - 1)
            def _():
                o_ref[0] = acc_ref[...].astype(o_ref.dtype)

        return pl.pallas_call(
            body,
            name=name,
            grid=(g, n_row),
            in_specs=[pl.BlockSpec((tr, c), lambda gi, i: (i, gi)), pl.BlockSpec((tr, c), lambda gi, i: (i, gi))],
            out_specs=pl.BlockSpec((1, c, c), lambda gi, i: (gi, 0, 0)),
            out_shape=jax.ShapeDtypeStruct((g, c, c), out_dtype),
            scratch_shapes=[pltpu.VMEM((c, c), F32)],
            compiler_params=_params("parallel", "arbitrary"),
        )(a, w)

    dims = (((1,), (0 if mode == "nn" else 1,)), ((), ()))

    def body(a_ref, w_ref, o_ref):
        o_ref[...] = lax.dot_general(a_ref[...].astype(BF16), w_ref[0].astype(BF16), dims, preferred_element_type=F32).astype(o_ref.dtype)

    return pl.pallas_call(
        body,
        name=name,
        grid=(g, n_row),
        in_specs=[pl.BlockSpec((tr, c), lambda gi, i: (i, gi)), pl.BlockSpec((1, c, c), lambda gi, i: (gi, 0, 0))],
        out_specs=pl.BlockSpec((tr, c), lambda gi, i: (i, gi)),
        out_shape=jax.ShapeDtypeStruct((s, g * c), out_dtype),
        compiler_params=_params("parallel", "parallel"),
    )(a, w)


def _row_tile(s, d):
    return _tile(s, max(8, (1 << 19) // d), 8)


def _norm_fwd(x, g, *, name, scale=None, shift=None, y=None, gate=None, out_dtype=BF16):
    s, d = x.shape
    tr = _row_tile(s, d)
    has_res, has_mod = y is not None, scale is not None

    def body(*refs):
        refs = list(refs)
        x_ref = refs.pop(0)
        xv = x_ref[...]
        if has_res:
            y_ref, gate_ref = refs.pop(0), refs.pop(0)
            xv = xv + gate_ref[...] * y_ref[...]
        g_ref = refs.pop(0)
        if has_mod:
            scale_ref, shift_ref = refs.pop(0), refs.pop(0)
        if has_res:
            refs.pop(0)[...] = xv
        h = xv * lax.rsqrt(jnp.mean(xv * xv, axis=-1, keepdims=True) + NORM_EPS)
        h = h * g_ref[...]
        if has_mod:
            h = h * (1.0 + scale_ref[...]) + shift_ref[...]
        refs.pop(0)[...] = h.astype(out_dtype)

    row = pl.BlockSpec((tr, d), lambda i: (i, 0))
    vec = pl.BlockSpec((1, d), lambda i: (0, 0))
    args, in_specs = [x], [row]
    if has_res:
        args += [y, gate]
        in_specs += [row, vec]
    args.append(g)
    in_specs.append(vec)
    if has_mod:
        args += [scale, shift]
        in_specs += [vec, vec]
    out_shape, out_specs = [], []
    if has_res:
        out_shape.append(jax.ShapeDtypeStruct((s, d), F32))
        out_specs.append(row)
    out_shape.append(jax.ShapeDtypeStruct((s, d), out_dtype))
    out_specs.append(row)
    res = pl.pallas_call(
        body, name=name, grid=(s // tr,), in_specs=in_specs, out_specs=out_specs, out_shape=out_shape, compiler_params=_params("parallel")
    )(*args)
    return (res[0], res[1]) if has_res else res[0]


def _norm_bwd(x, g, dh, *, name, scale=None, resid=None, hosted=None):
    s, d = x.shape
    tr = _row_tile(s, d)
    has_mod, has_res = scale is not None, resid is not None

    def body(*refs):
        refs = list(refs)
        x_ref, g_ref, dh_ref = refs.pop(0), refs.pop(0), refs.pop(0)
        scale_ref = refs.pop(0) if has_mod else None
        resid_ref = refs.pop(0) if has_res else None
        dx_ref, sums_ref = refs
        xv = x_ref[...]
        r = lax.rsqrt(jnp.mean(xv * xv, axis=-1, keepdims=True) + NORM_EPS)
        xhat = xv * r
        dh32 = dh_ref[...].astype(F32)
        gv = g_ref[...]
        dn = dh32 * (1.0 + scale_ref[...]) if has_mod else dh32
        dxhat = dn * gv
        dx = r * (dxhat - xhat * jnp.mean(dxhat * xhat, axis=-1, keepdims=True))
        if has_res:
            dx = dx + resid_ref[...]
        dx_ref[...] = dx

        @pl.when(pl.program_id(0) == 0)
        def _():
            sums_ref[...] = jnp.zeros_like(sums_ref)

        sums_ref[0:1, :] += jnp.sum(dh32, axis=0, keepdims=True)
        sums_ref[1:2, :] += jnp.sum(dh32 * (xhat * gv), axis=0, keepdims=True)
        sums_ref[2:3, :] += jnp.sum(dn * xhat, axis=0, keepdims=True)

    row = pl.BlockSpec((tr, d), lambda i: (i, 0))
    vec = pl.BlockSpec((1, d), lambda i: (0, 0))
    args, in_specs = [x, g, dh], [row, vec, row]
    if has_mod:
        args.append(scale)
        in_specs.append(vec)
    if has_res:
        args.append(resid)
        in_specs.append(row)
    outs, carried = _call(
        body,
        name=name,
        grid=(s // tr,),
        in_specs=in_specs,
        out_specs=[row, pl.BlockSpec((8, d), lambda i: (0, 0))],
        out_shape=[jax.ShapeDtypeStruct((s, d), F32), jax.ShapeDtypeStruct((8, d), F32)],
        args=args,
        sem=("arbitrary",),
        hosted=hosted,
    )
    return outs if hosted is None else (outs, carried)


def _gate_bwd(dx, y, gate, *, name):
    s, d = dx.shape
    tr = _row_tile(s, d)

    def body(dx_ref, y_ref, gate_ref, dy_ref, sums_ref):
        dxv = dx_ref[...]
        dy_ref[...] = (dxv * gate_ref[...]).astype(dy_ref.dtype)

        @pl.when(pl.program_id(0) == 0)
        def _():
            sums_ref[...] = jnp.zeros_like(sums_ref)

        sums_ref[0:1, :] += jnp.sum(dxv * y_ref[...], axis=0, keepdims=True)

    row = pl.BlockSpec((tr, d), lambda i: (i, 0))
    return pl.pallas_call(
        body,
        name=name,
        grid=(s // tr,),
        in_specs=[row, row, pl.BlockSpec((1, d), lambda i: (0, 0))],
        out_specs=[row, pl.BlockSpec((8, d), lambda i: (0, 0))],
        out_shape=[jax.ShapeDtypeStruct((s, d), BF16), jax.ShapeDtypeStruct((8, d), F32)],
        compiler_params=_params("arbitrary"),
    )(dx, y, gate)


def _elementwise(fn, args, out_dtypes, *, name):
    s, d = args[0].shape
    tc = d if d <= 2048 else _tile(d, 1024, LANES)
    tr = _tile(s, max(8, (1 << 18) // tc), 8)
    n_in = len(args)

    def body(*refs):
        outs = fn(*[r[...] for r in refs[:n_in]])
        for o_ref, o in zip(refs[n_in:], outs):
            o_ref[...] = o.astype(o_ref.dtype)

    spec = pl.BlockSpec((tr, tc), lambda i, j: (i, j))
    return pl.pallas_call(
        body,
        name=name,
        grid=(s // tr, d // tc),
        in_specs=[spec] * n_in,
        out_specs=[spec] * len(out_dtypes),
        out_shape=[jax.ShapeDtypeStruct((s, d), dt) for dt in out_dtypes],
        compiler_params=_params("parallel", "parallel"),
    )(*args)


def _silu(v):
    return (v * jax.nn.sigmoid(v),)


def _split3(v):
    hi = v.astype(BF16)
    r1 = v - hi.astype(F32)
    mid = r1.astype(BF16)
    lo = (r1 - mid.astype(F32)).astype(BF16)
    return hi, mid, lo


def _band_dot(band, v):
    return sum(jnp.dot(band, part, preferred_element_type=F32) for part in _split3(v))


def _pool(h, *, name, transpose, out_dtype):
    s, d = h.shape
    c = d // len(POOL_WINDOWS)
    tr = _tile(s, 256, HALO)
    per = tr // HALO
    n_halo = s // HALO

    def body(h_ref, halo_ref, o_ref):
        i = pl.program_id(0)
        out_row = i * tr + lax.broadcasted_iota(jnp.int32, (tr, tr + HALO), 0)
        col = lax.broadcasted_iota(jnp.int32, (tr, tr + HALO), 1)
        if transpose:
            ext = jnp.concatenate([h_ref[...], halo_ref[...]], axis=0)
            src_row = i * tr + col
            ext_row = i * tr + lax.broadcasted_iota(jnp.int32, (tr + HALO, 1), 0)
        else:
            ext = jnp.concatenate([halo_ref[...], h_ref[...]], axis=0)
            src_row = i * tr + col - HALO
            own_row = i * tr + lax.broadcasted_iota(jnp.int32, (tr, 1), 0)
        for gi, w in enumerate(POOL_WINDOWS):
            cols = slice(gi * c, (gi + 1) * c)
            if transpose:
                band = (src_row >= out_row) & (src_row < out_row + w) & (src_row < s)
                scaled = ext[:, cols] / jnp.minimum(ext_row + 1, w).astype(F32)
                res = _band_dot(band.astype(BF16), scaled) - h_ref[:, cols]
            else:
                band = (src_row <= out_row) & (src_row > out_row - w) & (src_row >= 0)
                res = _band_dot(band.astype(BF16), ext[:, cols]) / jnp.minimum(own_row + 1, w).astype(F32) - h_ref[:, cols]
            o_ref[:, cols] = res.astype(o_ref.dtype)

    if transpose:
        halo_map = lambda i: (jnp.minimum((i + 1) * per, n_halo - 1), 0)
    else:
        halo_map = lambda i: (jnp.maximum(i * per - 1, 0), 0)
    return pl.pallas_call(
        body,
        name=name,
        grid=(s // tr,),
        in_specs=[pl.BlockSpec((tr, d), lambda i: (i, 0)), pl.BlockSpec((HALO, d), halo_map)],
        out_specs=pl.BlockSpec((tr, d), lambda i: (i, 0)),
        out_shape=jax.ShapeDtypeStruct((s, d), out_dtype),
        compiler_params=_params("parallel"),
    )(h, h)


def _rotate(v, cos, sin):
    lane = lax.broadcasted_iota(jnp.int32, v.shape, 1)
    swapped = jnp.where(lane % ROPE_DIM < ROPE_DIM // 2, pltpu.roll(v, LANES - ROPE_DIM // 2, 1), pltpu.roll(v, ROPE_DIM // 2, 1))
    return v * cos + swapped * sin


def _rope_heads(x, cos, sin, *, name, out_dtype):
    s, n = x.shape
    tr = _tile(s, max(16, (1 << 18) // n), 16)

    def body(x_ref, cos_ref, sin_ref, o_ref):
        cos_v, sin_v = cos_ref[...], sin_ref[...]
        for j in range(n // LANES):
            lanes = slice(j * LANES, (j + 1) * LANES)
            if j % 2 == 0:
                o_ref[:, lanes] = x_ref[:, lanes].astype(o_ref.dtype)
            else:
                o_ref[:, lanes] = _rotate(x_ref[:, lanes].astype(F32), cos_v, sin_v).astype(o_ref.dtype)

    blk = pl.BlockSpec((tr, n), lambda i: (i, 0))
    tab = pl.BlockSpec((tr, LANES), lambda i: (i, 0))
    return pl.pallas_call(
        body,
        name=name,
        grid=(s // tr,),
        in_specs=[blk, tab, tab],
        out_specs=blk,
        out_shape=jax.ShapeDtypeStruct((s, n), out_dtype),
        compiler_params=_params("parallel"),
    )(x, cos, sin)


def _build_keys(kv, kr_pre, cos, sin, *, name):
    s, n = kv.shape
    tr = _tile(s, max(16, (1 << 18) // n), 16)

    def body(kv_ref, kr_ref, cos_ref, sin_ref, o_ref):
        rope = _rotate(kr_ref[...], cos_ref[...], sin_ref[...]).astype(o_ref.dtype)
        for j in range(n // LANES):
            lanes = slice(j * LANES, (j + 1) * LANES)
            o_ref[:, lanes] = kv_ref[:, lanes] if j % 2 == 0 else rope

    blk = pl.BlockSpec((tr, n), lambda i: (i, 0))
    tab = pl.BlockSpec((tr, LANES), lambda i: (i, 0))
    return pl.pallas_call(
        body,
        name=name,
        grid=(s // tr,),
        in_specs=[blk, tab, tab, tab],
        out_specs=blk,
        out_shape=jax.ShapeDtypeStruct((s, n), BF16),
        compiler_params=_params("parallel"),
    )(kv, kr_pre, cos, sin)


def _keys_bwd(dk_a, dk_b, dv_a, dv_b, cos, sin_neg, *, name):
    s, n = dk_a.shape
    heads = n // HEAD_PAD
    tr = _tile(s, 512, 8)

    def body(dka_ref, dkb_ref, dva_ref, dvb_ref, cos_ref, sin_ref, dkv_ref, dkr_ref):
        hd = pl.program_id(1)
        dk = dka_ref[...] + dkb_ref[...]
        dkv_ref[:, :NOPE_DIM] = dk[:, :NOPE_DIM].astype(dkv_ref.dtype)
        dkv_ref[:, NOPE_DIM:] = (dva_ref[...] + dvb_ref[...]).astype(dkv_ref.dtype)

        @pl.when(hd == 0)
        def _():
            dkr_ref[...] = dk[:, NOPE_DIM:]

        @pl.when(hd > 0)
        def _():
            dkr_ref[...] += dk[:, NOPE_DIM:]

        @pl.when(hd == heads - 1)
        def _():
            dkr_ref[...] = _rotate(dkr_ref[...], cos_ref[...], sin_ref[...])

    dk_blk = pl.BlockSpec((tr, HEAD_PAD), lambda i, hd: (i, hd))
    dv_blk = pl.BlockSpec((tr, V_DIM), lambda i, hd: (i, hd))
    tab = pl.BlockSpec((tr, LANES), lambda i, hd: (i, 0))
    return pl.pallas_call(
        body,
        name=name,
        grid=(s // tr, heads),
        in_specs=[dk_blk, dk_blk, dv_blk, dv_blk, tab, tab],
        out_specs=[dk_blk, tab],
        out_shape=[jax.ShapeDtypeStruct((s, n), BF16), jax.ShapeDtypeStruct((s, LANES), F32)],
        compiler_params=_params("parallel", "arbitrary"),
    )(dk_a, dk_b, dv_a, dv_b, cos, sin_neg)


def _attn_tile(s):
    return _tile(s, 512, LANES)


ATTN_SUB = 512


def _sub_tiles(t):
    step = min(ATTN_SUB, t)
    return [slice(r, r + step) for r in range(0, t, step)]


def _causal_mask(t, sub, transposed=False):
    n = sub.stop - sub.start
    rows = sub.start + lax.broadcasted_iota(jnp.int32, (n, t), 0)
    cols = lax.broadcasted_iota(jnp.int32, (n, t), 1)
    return rows <= cols if transposed else cols <= rows


def _attn_fwd(q, keys, v_t, *, name):
    s = q.shape[0]
    heads = q.shape[1] // HEAD_PAD
    t = _attn_tile(s)
    nq = s // t

    def body(q_ref, k_ref, v_ref, o_ref, lse_ref, m_ref, l_ref, acc_ref):
        qi = pl.program_id(1)
        qv = q_ref[...]
        m_ref[...] = jnp.full_like(m_ref, -jnp.inf)
        l_ref[...] = jnp.zeros_like(l_ref)
        acc_ref[...] = jnp.zeros_like(acc_ref)

        def block(kb, diagonal):
            rows = pl.ds(pl.multiple_of(kb * t, t), t)
            sc_t = lax.dot_general(k_ref[rows, :], qv, _NT, preferred_element_type=F32) * SM_SCALE
            if diagonal:
                sc_t = jnp.where(_causal_mask(t, slice(0, t), transposed=True), sc_t, -jnp.inf)
            m_old = m_ref[...]
            m_new = jnp.maximum(m_old, jnp.max(sc_t, axis=0, keepdims=True))
            alpha = jnp.exp(m_old - m_new)
            p_t = jnp.exp(sc_t - m_new)
            l_ref[...] = alpha * l_ref[...] + jnp.sum(p_t, axis=0, keepdims=True)
            acc_ref[...] = alpha * acc_ref[...] + jnp.dot(v_ref[0, kb], p_t.astype(BF16), preferred_element_type=F32)
            m_ref[...] = m_new

        def earlier(kb, carry):
            block(kb, False)
            return carry

        lax.fori_loop(0, qi, earlier, 0)
        block(qi, True)
        o_ref[...] = (acc_ref[...] / l_ref[...]).astype(o_ref.dtype)
        lse_ref[0, 0] = m_ref[...] + jnp.log(l_ref[...])

    return pl.pallas_call(
        body,
        name=name,
        grid=(heads, nq),
        in_specs=[
            pl.BlockSpec((t, HEAD_PAD), lambda hd, qi: (qi, hd)),
            pl.BlockSpec((s, HEAD_PAD), lambda hd, qi: (0, hd)),
            pl.BlockSpec((1, nq, V_DIM, t), lambda hd, qi: (hd, 0, 0, 0)),
        ],
        out_specs=[pl.BlockSpec((V_DIM, t), lambda hd, qi: (hd, qi)), pl.BlockSpec((1, 1, 1, t), lambda hd, qi: (hd, qi, 0, 0))],
        out_shape=[jax.ShapeDtypeStruct((heads * V_DIM, s), BF16), jax.ShapeDtypeStruct((heads, nq, 1, t), F32)],
        scratch_shapes=[pltpu.VMEM((1, t), F32), pltpu.VMEM((1, t), F32), pltpu.VMEM((V_DIM, t), F32)],
        compiler_params=_params("parallel", "parallel"),
    )(q, keys, v_t)


def _attn_bwd_q(q, keys, kv, o, do, lse, *, name):
    s = q.shape[0]
    heads = q.shape[1] // HEAD_PAD
    t = _attn_tile(s)

    def body(q_ref, k_ref, v_ref, o_ref, do_ref, lse_ref, dq_ref, delta_ref, acc_ref):
        qi = pl.program_id(1)
        qv, dov, lse = q_ref[...], do_ref[...], lse_ref[0]
        delta = jnp.sum(dov.astype(F32) * o_ref[...].astype(F32), axis=1, keepdims=True)
        delta_ref[0] = delta
        acc_ref[...] = jnp.zeros_like(acc_ref)

        def block(kb, diagonal):
            rows = pl.ds(pl.multiple_of(kb * t, t), t)
            kblk, vblk = k_ref[rows, :], v_ref[rows, :]
            for sub in _sub_tiles(t):
                sc = lax.dot_general(qv[sub], kblk, _NT, preferred_element_type=F32) * SM_SCALE
                p = jnp.exp(sc - lse[sub])
                if diagonal:
                    p = jnp.where(_causal_mask(t, sub), p, 0.0)
                dp = lax.dot_general(dov[sub], vblk, _NT, preferred_element_type=F32)
                ds = p * (dp - delta[sub]) * SM_SCALE
                acc_ref[sub, :] += jnp.dot(ds.astype(BF16), kblk, preferred_element_type=F32)

        def earlier(kb, carry):
            block(kb, False)
            return carry

        lax.fori_loop(0, qi, earlier, 0)
        block(qi, True)
        dq_ref[...] = acc_ref[...]

    o_blk = pl.BlockSpec((t, V_DIM), lambda hd, qi: (qi, hd))
    col = pl.BlockSpec((1, t, 1), lambda hd, qi: (hd, qi, 0))
    return pl.pallas_call(
        body,
        name=name,
        grid=(heads, s // t),
        in_specs=[
            pl.BlockSpec((t, HEAD_PAD), lambda hd, qi: (qi, hd)),
            pl.BlockSpec((s, HEAD_PAD), lambda hd, qi: (0, hd)),
            pl.BlockSpec((s, V_DIM), lambda hd, qi: (0, 2 * hd + 1)),
            o_blk,
            o_blk,
            col,
        ],
        out_specs=[pl.BlockSpec((t, HEAD_PAD), lambda hd, qi: (qi, hd)), col],
        out_shape=[jax.ShapeDtypeStruct((s, heads * HEAD_PAD), F32), jax.ShapeDtypeStruct((heads, s, 1), F32)],
        scratch_shapes=[pltpu.VMEM((t, HEAD_PAD), F32)],
        compiler_params=_params("parallel", "parallel"),
    )(q, keys, kv, o, do, lse)


def _attn_bwd_kv(q, keys, kv, do, lse_row, delta_row, *, name):
    s = q.shape[0]
    heads = q.shape[1] // HEAD_PAD
    t = _attn_tile(s)
    nq = s // t

    def body(q_ref, k_ref, v_ref, do_ref, lse_ref, delta_ref, dk_ref, dv_ref, dk_acc, dv_acc):
        ki = pl.program_id(1)
        kv_, vv = k_ref[...], v_ref[...]
        dk_acc[...] = jnp.zeros_like(dk_acc)
        dv_acc[...] = jnp.zeros_like(dv_acc)

        def block(qb, diagonal):
            rows = pl.ds(pl.multiple_of(qb * t, t), t)
            qv, dov = q_ref[rows, :], do_ref[rows, :]
            lse, delta = lse_ref[0, qb], delta_ref[0, qb]
            for sub in _sub_tiles(t):
                sc_t = lax.dot_general(kv_[sub], qv, _NT, preferred_element_type=F32) * SM_SCALE
                p_t = jnp.exp(sc_t - lse)
                if diagonal:
                    p_t = jnp.where(_causal_mask(t, sub, transposed=True), p_t, 0.0)
                dv_acc[sub, :] += jnp.dot(p_t.astype(BF16), dov, preferred_element_type=F32)
                dp_t = lax.dot_general(vv[sub], dov, _NT, preferred_element_type=F32)
                ds_t = p_t * (dp_t - delta) * SM_SCALE
                dk_acc[sub, :] += jnp.dot(ds_t.astype(BF16), qv, preferred_element_type=F32)

        def later(qb, carry):
            block(qb, False)
            return carry

        block(ki, True)
        lax.fori_loop(ki + 1, nq, later, 0)
        dk_ref[...] = dk_acc[...]
        dv_ref[...] = dv_acc[...]

    row = pl.BlockSpec((1, nq, 1, t), lambda hd, ki: (hd, 0, 0, 0))
    return pl.pallas_call(
        body,
        name=name,
        grid=(heads, nq),
        in_specs=[
            pl.BlockSpec((s, HEAD_PAD), lambda hd, ki: (0, hd)),
            pl.BlockSpec((t, HEAD_PAD), lambda hd, ki: (ki, hd)),
            pl.BlockSpec((t, V_DIM), lambda hd, ki: (ki, 2 * hd + 1)),
            pl.BlockSpec((s, V_DIM), lambda hd, ki: (0, hd)),
            row,
            row,
        ],
        out_specs=[pl.BlockSpec((t, HEAD_PAD), lambda hd, ki: (ki, hd)), pl.BlockSpec((t, V_DIM), lambda hd, ki: (ki, hd))],
        out_shape=[jax.ShapeDtypeStruct((s, heads * HEAD_PAD), F32), jax.ShapeDtypeStruct((s, heads * V_DIM), F32)],
        scratch_shapes=[pltpu.VMEM((t, HEAD_PAD), F32), pltpu.VMEM((t, V_DIM), F32)],
        compiler_params=_params("parallel", "parallel"),
    )(q, keys, kv, do, lse_row, delta_row)


def _loss_bwd(x, y, gate, g, target, *, name):
    s, d = x.shape
    tr = _row_tile(s, d)

    def body(x_ref, y_ref, gate_ref, g_ref, t_ref, dx_ref, stats_ref, loss_ref):
        xv = x_ref[...] + gate_ref[...] * y_ref[...]
        r = lax.rsqrt(jnp.mean(xv * xv, axis=-1, keepdims=True) + NORM_EPS)
        xhat = xv * r
        gv = g_ref[...]
        err = xhat * gv - t_ref[...]
        dy = err / d
        dxhat = dy * gv
        dx_ref[...] = r * (dxhat - xhat * jnp.mean(dxhat * xhat, axis=-1, keepdims=True))

        @pl.when(pl.program_id(0) == 0)
        def _():
            stats_ref[...] = jnp.zeros_like(stats_ref)
            loss_ref[...] = jnp.zeros_like(loss_ref)

        stats_ref[0:1, :] += jnp.sum(dy * xhat, axis=0, keepdims=True)
        loss_ref[...] += 0.5 * jnp.sum(jnp.mean(err * err, axis=-1, keepdims=True))

    row = pl.BlockSpec((tr, d), lambda i: (i, 0))
    vec = pl.BlockSpec((1, d), lambda i: (0, 0))
    return pl.pallas_call(
        body,
        name=name,
        grid=(s // tr,),
        in_specs=[row, row, vec, vec, row],
        out_specs=[row, pl.BlockSpec((8, d), lambda i: (0, 0)), pl.BlockSpec((8, LANES), lambda i: (0, 0))],
        out_shape=[jax.ShapeDtypeStruct((s, d), F32), jax.ShapeDtypeStruct((8, d), F32), jax.ShapeDtypeStruct((8, LANES), F32)],
        compiler_params=_params("arbitrary"),
    )(x, y, gate, g, target)


def _adam_math(w, g, m, v):
    new_m = ADAM_B1 * m + (1.0 - ADAM_B1) * g
    new_v = ADAM_B2 * v + (1.0 - ADAM_B2) * (g * g)
    m_hat = new_m / (1.0 - ADAM_B1**ADAM_STEP)
    v_hat = new_v / (1.0 - ADAM_B2**ADAM_STEP)
    return -ADAM_LR * (m_hat / (jnp.sqrt(v_hat) + ADAM_EPS) + ADAM_WD * w), new_m, new_v


def _adamw(w, g, m, v, *, name):
    rows, cols = w.shape
    tr = _tile(rows, max(8, (1 << 18) // cols), 8)

    def body(w_ref, g_ref, m_ref, v_ref, go_ref, d_ref, mo_ref, vo_ref):
        gv = g_ref[...]
        go_ref[...] = gv
        d_ref[...], mo_ref[...], vo_ref[...] = _adam_math(w_ref[...], gv, m_ref[...], v_ref[...])

    spec = pl.BlockSpec((tr, cols), lambda i: (i, 0))
    return pl.pallas_call(
        body,
        name=name,
        grid=(rows // tr,),
        in_specs=[spec] * 4,
        out_specs=[spec] * 4,
        out_shape=[jax.ShapeDtypeStruct((rows, cols), F32)] * 4,
        compiler_params=_params("parallel"),
    )(w, g, m, v)


def _tp_adamw(sc16, dm, w, m, v, *, name):
    nl, d, n = w.shape
    tm = _tile(d, 512, LANES)
    tn = _tile(n, 1024, LANES)

    def body(sc_ref, dm_ref, w_ref, m_ref, v_ref, go_ref, d_ref, mo_ref, vo_ref):
        gv = lax.dot_general(sc_ref[...].astype(BF16), dm_ref[0].astype(BF16), (((0,), (0,)), ((), ())), preferred_element_type=F32)
        go_ref[0] = gv
        d_ref[0], mo_ref[0], vo_ref[0] = _adam_math(w_ref[0], gv, m_ref[0], v_ref[0])

    blk = pl.BlockSpec((1, tm, tn), lambda l, i, j: (l, i, j))
    return pl.pallas_call(
        body,
        name=name,
        grid=(nl, d // tm, n // tn),
        in_specs=[pl.BlockSpec((16, tm), lambda l, i, j: (0, i)), pl.BlockSpec((1, 16, tn), lambda l, i, j: (l, 0, j)), blk, blk, blk],
        out_specs=[blk] * 4,
        out_shape=[jax.ShapeDtypeStruct((nl, d, n), F32)] * 4,
        compiler_params=_params("parallel", "parallel", "parallel"),
    )(sc16, dm, w, m, v)


def _sum_devices(x, *, name):
    def body(x_ref, o_ref):
        acc = x_ref[0]
        for k in range(1, N_DEV):
            acc = acc + x_ref[k]
        o_ref[...] = acc

    return pl.pallas_call(body, name=name, out_shape=jax.ShapeDtypeStruct(x.shape[1:], F32))(x)


def _place():
    mx, my, mc = lax.axis_index("x"), lax.axis_index("y"), lax.axis_index("c")
    chips = [(1 - mx, my), (mx, 1 - my), (1 - mx, 1 - my)]
    return mx, my, mc, chips


def _remote(src, dst, send_sem, recv_sem, device):
    return pltpu.make_async_remote_copy(src_ref=src, dst_ref=dst, send_sem=send_sem, recv_sem=recv_sem, device_id=device, device_id_type=MESH)


def _allgather8(x, *, name):
    def body(x_ref, out_ref, send_sems, recv_sems, local_sem):
        mx, my, mc, chips = _place()
        me, sibling = (mx, my, mc), (mx, my, 1 - mc)

        def slot(px, py, pc):
            return out_ref.at[4 * px + 2 * py + pc]

        def copy(k, block, to, src=None):
            return _remote(slot(*block) if src is None else src, slot(*block), send_sems.at[k], recv_sems.at[k], to)

        mine = pltpu.make_async_copy(x_ref, slot(*me), local_sem)
        mine.start()
        first = [copy(0, me, sibling, src=x_ref)] + [copy(1 + j, me, (*chip, mc), src=x_ref) for j, chip in enumerate(chips)]
        for cp in first:
            cp.start()
        passed = [copy(4 + j, (*chip, mc), sibling) for j, chip in enumerate(chips)]
        for j, chip in enumerate(chips):
            copy(1 + j, (*chip, mc), me).wait_recv()
            passed[j].start()
        copy(0, sibling, me).wait_recv()
        for j, chip in enumerate(chips):
            copy(4 + j, (*chip, 1 - mc), me).wait_recv()
        for cp in first + passed:
            cp.wait_send()
        mine.wait()

    return pl.pallas_call(
        body,
        name=name,
        out_shape=jax.ShapeDtypeStruct((N_DEV,) + x.shape, x.dtype),
        in_specs=[pl.BlockSpec(memory_space=pltpu.VMEM)],
        out_specs=pl.BlockSpec(memory_space=pltpu.VMEM),
        scratch_shapes=[pltpu.SemaphoreType.DMA((7,)), pltpu.SemaphoreType.DMA((7,)), pltpu.SemaphoreType.DMA],
    )(x)


class _Geom:
    def __init__(self, shape3, axis):
        self.shape3, self.axis = shape3, axis
        nl, r, c = shape3
        self.rs, self.cs = (r // N_CHIPS, c) if axis == 1 else (r, c // N_CHIPS)
        self.hl, self.hr = (nl // 2, self.rs) if nl > 1 else (1, self.rs // 2)
        self.shard = (nl, self.rs, self.cs)
        self.half = (self.hl, self.hr, self.cs)

    def in_full(self, ref, chip, core):
        nl = self.shape3[0]
        l0 = core * self.hl if nl > 1 else 0
        r0 = (chip * self.rs if self.axis == 1 else 0) + (0 if nl > 1 else core * self.hr)
        c0 = chip * self.cs if self.axis == 2 else 0
        return ref.at[pl.ds(l0, self.hl), pl.ds(r0, self.hr), pl.ds(c0, self.cs)]


def _place_shard(shard, geom, chip_arr, *, name, layer=None):
    nl, rs, cs = geom.shard
    tr = _tile(rs, max(16, (1 << 18) // cs), 16)
    per = rs // tr
    first = 0 if layer is None else layer

    def body(chip_ref, x_ref, o_ref):
        o_ref[...] = x_ref[...].astype(o_ref.dtype)

    def out_map(l, i, chip_ref):
        return (l, chip_ref[0] * per + i, 0) if geom.axis == 1 else (l, i, chip_ref[0])

    return pl.pallas_call(
        body,
        name=name,
        grid_spec=pltpu.PrefetchScalarGridSpec(
            num_scalar_prefetch=1,
            grid=(nl, per),
            in_specs=[pl.BlockSpec((1, tr, cs), lambda l, i, chip_ref: (first + l, i, 0))],
            out_specs=pl.BlockSpec((1, tr, cs), out_map),
        ),
        out_shape=jax.ShapeDtypeStruct(geom.shape3, BF16),
        compiler_params=_params("parallel", "parallel"),
    )(chip_arr, shard)


def _dma_sems(count, arrays):
    return [pltpu.SemaphoreType.DMA((count,))] * arrays


def _gather_plan(fulls, geoms):
    def ici(w, k, src, dst, sems, device):
        return _remote(src, dst, sems[0].at[3 * w + k], sems[1].at[3 * w + k], device)

    def d2d(w, k, box, sems, device):
        return _remote(box, box, sems[2].at[3 * w + k], sems[3].at[3 * w + k], device)

    def start(given, full, sems):
        mx, my, mc, chips = _place()
        me = 2 * mx + my
        for w, geom in enumerate(geoms):
            for k, chip in enumerate(chips):
                ici(w, k, geom.in_full(given[w], me, mc), geom.in_full(full[w], me, mc), sems, (*chip, mc)).start()

    def finish(given, full, sems):
        mx, my, mc, chips = _place()
        me, sibling = 2 * mx + my, (mx, my, 1 - mc)
        for w, geom in enumerate(geoms):
            for k, (px, py) in enumerate(chips):
                landed = geom.in_full(full[w], 2 * px + py, mc)
                ici(w, k, landed, landed, sems, (px, py, mc)).wait_recv()
                d2d(w, k, landed, sems, sibling).start()
        for w, geom in enumerate(geoms):
            for k, (px, py) in enumerate(chips):
                d2d(w, k, geom.in_full(full[w], 2 * px + py, 1 - mc), sems, sibling).wait_recv()
        for w, geom in enumerate(geoms):
            for k, (px, py) in enumerate(chips):
                ici(w, k, geom.in_full(given[w], me, mc), geom.in_full(full[w], me, mc), sems, (px, py, mc)).wait_send()
                d2d(w, k, geom.in_full(full[w], 2 * px + py, mc), sems, sibling).wait_send()

    n = len(fulls)
    shapes = [jax.ShapeDtypeStruct(f.shape, f.dtype) for f in fulls]
    return _Hosted(fulls, shapes, {w: w for w in range(n)}, _dma_sems(3 * n, 4), start, finish)


def _pair_plan(grads, geoms):
    def copies(grad, theirs, sems):
        mx, my, mc, _ = _place()
        return [
            _remote(geom.in_full(grad[w], chip, 1 - mc), theirs[w].at[chip], sems[0].at[4 * w + chip], sems[1].at[4 * w + chip], (mx, my, 1 - mc))
            for w, geom in enumerate(geoms)
            for chip in range(N_CHIPS)
        ]

    def start(grad, theirs, sems):
        for cp in copies(grad, theirs, sems):
            cp.start()

    def finish(grad, theirs, sems):
        for cp in copies(grad, theirs, sems):
            cp.wait_recv()
        for cp in copies(grad, theirs, sems):
            cp.wait_send()

    shapes = [jax.ShapeDtypeStruct((N_CHIPS,) + g.half, x.dtype) for g, x in zip(geoms, grads)]
    return _Hosted(grads, shapes, {}, _dma_sems(4 * len(grads), 2), start, finish)


def _half_tile(geom):
    return _tile(geom.hr, max(16, (1 << 18) // geom.cs), 16)


def _pair_add(grad, theirs, geom, core_arr, *, name):
    hl, hr, cs = geom.half
    tr = _half_tile(geom)
    stacked = geom.shape3[0] > 1

    def grad_map(chip, l, i, core_ref):
        layer = core_ref[0] * hl + l if stacked else 0
        row = (chip * (geom.rs // tr) if geom.axis == 1 else 0) + (0 if stacked else core_ref[0] * (hr // tr)) + i
        return layer, row, (chip if geom.axis == 2 else 0)

    def body(core_ref, g_ref, t_ref, o_ref):
        o_ref[0] = (g_ref[...].astype(F32) + t_ref[0].astype(F32)).astype(o_ref.dtype)

    blk = pl.BlockSpec((1, 1, tr, cs), lambda chip, l, i, core_ref: (chip, l, i, 0))
    return pl.pallas_call(
        body,
        name=name,
        grid_spec=pltpu.PrefetchScalarGridSpec(
            num_scalar_prefetch=1, grid=(N_CHIPS, hl, hr // tr), in_specs=[pl.BlockSpec((1, tr, cs), grad_map), blk], out_specs=blk
        ),
        out_shape=jax.ShapeDtypeStruct(theirs.shape, BF16),
        compiler_params=_params("parallel", "parallel", "parallel"),
    )(core_arr, grad, theirs)


def _chips_plan(parts):
    def copies(part, slots, sems):
        _, _, mc, chips = _place()
        return [
            _remote(part[w].at[2 * px + py], slots[w].at[k], sems[0].at[3 * w + k], sems[1].at[3 * w + k], (px, py, mc))
            for w in range(len(parts))
            for k, (px, py) in enumerate(chips)
        ]

    def start(part, slots, sems):
        for cp in copies(part, slots, sems):
            cp.start()

    def finish(part, slots, sems):
        for cp in copies(part, slots, sems):
            cp.wait_recv()
        for cp in copies(part, slots, sems):
            cp.wait_send()

    shapes = [jax.ShapeDtypeStruct((N_CHIPS - 1,) + p.shape[1:], p.dtype) for p in parts]
    return _Hosted(parts, shapes, {}, _dma_sems(3 * len(parts), 2), start, finish)


def _chip_sum(part, slots, geom, place_arr, *, name, stack=None):
    hl, hr, cs = geom.half
    tr = _half_tile(geom)

    def body(place_ref, own_ref, s0_ref, s1_ref, s2_ref, *rest):
        o_ref = rest[-1]
        o_ref[...] = ((own_ref[...].astype(F32) + s0_ref[...].astype(F32)) + s1_ref[...].astype(F32)) + s2_ref[...].astype(F32)

    def slot(k):
        return pl.BlockSpec((1, 1, tr, cs), lambda l, i, place_ref: (k, l, i, 0))

    in_specs = [pl.BlockSpec((1, 1, tr, cs), lambda l, i, place_ref: (place_ref[0], l, i, 0)), slot(0), slot(1), slot(2)]
    args = [place_arr, part, slots, slots, slots]
    aliases = {}
    if stack is None:
        out_spec = pl.BlockSpec((1, 1, tr, cs), lambda l, i, place_ref: (place_ref[1], l, i, 0))
        out_shape = jax.ShapeDtypeStruct((2,) + geom.half, F32)
    else:
        layers, layer, prev = stack
        assert hl == 1
        out_spec = pl.BlockSpec((1, 1, tr, cs), lambda l, i, place_ref: (layer, place_ref[1], i, 0))
        out_shape = jax.ShapeDtypeStruct((layers, 2, hr, cs), F32)
        if prev is not None:
            aliases = {len(args): 0}
            in_specs.append(ANY)
            args.append(prev)
    return pl.pallas_call(
        body,
        name=name,
        grid_spec=pltpu.PrefetchScalarGridSpec(num_scalar_prefetch=1, grid=(hl, hr // tr), in_specs=in_specs, out_specs=out_spec),
        out_shape=out_shape,
        input_output_aliases=aliases,
        compiler_params=_params("parallel", "parallel"),
    )(*args)


def _join_plan(boths, prefixes):
    def copies(given, both, sems):
        mx, my, mc, _ = _place()
        out, n = [], 0
        for w in range(len(boths)):
            for p in prefixes[w]:
                out.append(_remote(given[w].at[(*p, mc)], both[w].at[(*p, mc)], sems[0].at[n], sems[1].at[n], (mx, my, 1 - mc)))
                n += 1
        return out

    def arrivals(both, sems):
        mx, my, mc, _ = _place()
        out, n = [], 0
        for w in range(len(boths)):
            for p in prefixes[w]:
                got = both[w].at[(*p, 1 - mc)]
                out.append(_remote(got, got, sems[0].at[n], sems[1].at[n], (mx, my, 1 - mc)))
                n += 1
        return out

    def start(given, both, sems):
        for cp in copies(given, both, sems):
            cp.start()

    def finish(given, both, sems):
        for cp in arrivals(both, sems):
            cp.wait_recv()
        for cp in copies(given, both, sems):
            cp.wait_send()

    count = sum(len(p) for p in prefixes)
    shapes = [jax.ShapeDtypeStruct(b.shape, b.dtype) for b in boths]
    return _Hosted(boths, shapes, {w: w for w in range(len(boths))}, _dma_sems(count, 2), start, finish)


WEIGHTS = ("mod_w", "mod_b", "norm_mix", "norm_ffn", "pool_w", "pool_scale", "kv_mod_w", "kv_mod_b", "kv_in_norm", "w_dkv", "kv_norm",
           "w_uk", "w_uv", "w_kr", "w_dq", "q_norm", "w_uq", "w_o", "ffn_gate", "ffn_up", "ffn_down", "final_norm")
SMALL = ("mod_b", "kv_mod_b", "norm_mix", "norm_ffn", "kv_in_norm", "kv_norm", "q_norm", "final_norm")


def _rows(v):
    return v.reshape(-1, LANES)


def _pad_rows(a):
    return jnp.pad(a, ((0, (-a.shape[0]) % 8), (0, 0)))


def _vec(v):
    return v.reshape(1, -1)


def _step(x, c, positions, target, wts, mom, var):
    _, s, d = x.shape
    depth, n_a, n_b = wts["mod_w"].shape[0], wts["pool_w"].shape[0], wts["w_dq"].shape[0]
    assert n_b == 2 and n_a + n_b == depth
    heads = d // V_DIM
    kvr, qr = wts["w_dkv"].shape[1], wts["w_dq"].shape[2]
    ffn = wts["ffn_gate"].shape[2] * N_CHIPS
    pool_c = d // len(POOL_WINDOWS)
    nmod, nkv = N_MOD * d, 2 * d
    mx, my, mc = lax.axis_index("x"), lax.axis_index("y"), lax.axis_index("c")
    chip, dev = 2 * mx + my, 4 * mx + 2 * my + mc
    xs, tgt = x[0], target[0]

    inv_freq = 1.0 / (ROPE_THETA ** (jnp.arange(0, ROPE_DIM, 2, dtype=F32) / ROPE_DIM))
    ang = positions[0].astype(F32)[:, None] * inv_freq
    cos, sin, zero = jnp.cos(ang), jnp.sin(ang), jnp.zeros((s, LANES - ROPE_DIM), F32)
    cos_t = jnp.concatenate([cos, cos, zero], axis=1)
    sin_fwd = jnp.concatenate([-sin, sin, zero], axis=1)
    sin_bwd = jnp.concatenate([sin, -sin, zero], axis=1)

    c_rows, ps_rows = d // LANES, n_a * (d // N_CHIPS) // LANES
    cond = _allgather8(_pad_rows(jnp.concatenate([_rows(c), _rows(wts["pool_scale"])])), name="gather_cond")
    c_all = cond[:, :c_rows].reshape(N_DEV, d)
    pool_scale = cond[0::2, c_rows : c_rows + ps_rows].reshape(N_CHIPS, n_a, d // N_CHIPS).transpose(1, 0, 2).reshape(n_a, d)
    sc16 = _elementwise(_silu, [jnp.pad(c_all, ((0, 16 - N_DEV), (0, 0)))], [F32], name="silu_cond")[0]

    mod_bias = lax.dynamic_slice_in_dim(wts["mod_b"], chip * (nmod // N_CHIPS), nmod // N_CHIPS, axis=1)[:, None, :]
    kv_bias = lax.dynamic_slice_in_dim(wts["kv_mod_b"], chip * (nkv // N_CHIPS), nkv // N_CHIPS).reshape(1, 1, -1)
    mod_part = _tp_fwd(sc16, wts["mod_w"], mod_bias, name="mod_fwd")
    kv_part = _tp_fwd(sc16, wts["kv_mod_w"][None], kv_bias, name="kv_mod_fwd")
    part = jnp.concatenate([mod_part[i, :N_DEV] for i in range(depth)] + [kv_part[0, :N_DEV]], axis=1)
    ncol = part.shape[1]
    gathered = _allgather8(_pad_rows(_rows(part)), name="gather_mods")
    gathered = gathered[0::2, : N_DEV * ncol // LANES].reshape(N_CHIPS, N_DEV, ncol)
    mine = lax.dynamic_index_in_dim(gathered, dev, axis=1, keepdims=False)
    per = nmod // N_CHIPS
    mods = [mine[:, i * per : (i + 1) * per].reshape(N_MOD, 1, d) for i in range(depth)]
    kv_shift, kv_scale = mine[:, depth * per :].reshape(2, 1, d)

    mixer_names = ("pool_w", "w_dkv", "w_uk", "w_uv", "w_kr", "w_dq", "w_uq", "w_o")
    ffn_names = ("ffn_gate", "ffn_up", "ffn_down")
    geoms = {
        "pool_w": _Geom((n_a * len(POOL_WINDOWS), pool_c, pool_c), 1),
        "w_dkv": _Geom((1, d, kvr), 1),
        "w_uk": _Geom((1, kvr, heads * NOPE_DIM), 2),
        "w_uv": _Geom((1, kvr, heads * V_DIM), 2),
        "w_kr": _Geom((1, d, ROPE_DIM), 1),
        "w_dq": _Geom((n_b, d, qr), 1),
        "w_uq": _Geom((n_b, qr, heads * (NOPE_DIM + ROPE_DIM)), 2),
        "w_o": _Geom((n_b, d, d), 1),
        "ffn_gate": _Geom((1, d, ffn), 2),
        "ffn_up": _Geom((1, d, ffn), 2),
        "ffn_down": _Geom((1, ffn, d), 1),
    }
    mixer_geoms = [geoms[n] for n in mixer_names]
    ffn_geoms = [geoms[n] for n in ffn_names]
    chip_arr, core_arr, place_arr = chip.reshape(1), mc.reshape(1), jnp.stack([chip, mc])
    placed = [_place_shard(wts[n].reshape(geoms[n].shard), geoms[n], chip_arr, name="place_" + n) for n in mixer_names]
    placed_ffn = [[_place_shard(wts[n], geoms[n], chip_arr, layer=i, name="place_" + n) for n in ffn_names] for i in range(depth)]
    first = _run(_gather_plan(placed + placed_ffn[0], mixer_geoms + ffn_geoms), name="gather_first")
    full = dict(zip(mixer_names, first))
    whole_k = dict(tm=512, tn=1024, tk=max(s, ffn))
    ffn_w = [None] * depth
    ffn_w[0] = first[len(mixer_names) :]

    pool_w = full["pool_w"].reshape(n_a, len(POOL_WINDOWS), pool_c, pool_c)
    w_uq = full["w_uq"].reshape(n_b, qr, heads, NOPE_DIM + ROPE_DIM)
    w_q = jnp.pad(w_uq, ((0, 0), (0, 0), (0, 0), (0, HEAD_PAD - NOPE_DIM - ROPE_DIM))).reshape(n_b, qr, heads * HEAD_PAD)
    w_ukv = jnp.stack([full["w_uk"].reshape(kvr, heads, NOPE_DIM), full["w_uv"].reshape(kvr, heads, V_DIM)], axis=2).reshape(kvr, heads * HEAD_PAD)
    w_dkvkr = jnp.concatenate([full["w_dkv"][0], full["w_kr"][0], jnp.zeros((d, LANES - ROPE_DIM), BF16)], axis=1)
    w_dq, w_o = full["w_dq"], full["w_o"]

    norm_mix, norm_ffn = wts["norm_mix"], wts["norm_ffn"]
    saved = []
    cur, pending = xs, None
    kv_side = None
    for i in range(depth):
        shift_m, scale_m, gate_m, shift_f, scale_f, gate_f = mods[i]
        h1_dtype = F32 if i < n_a else BF16
        if pending is None:
            x0 = cur
            h1 = _norm_fwd(x0, _vec(norm_mix[i]), scale=scale_m, shift=shift_m, out_dtype=h1_dtype, name="norm_mix_first")
        else:
            x0, h1 = _norm_fwd(cur, _vec(norm_mix[i]), scale=scale_m, shift=shift_m, y=pending[0], gate=pending[1], out_dtype=h1_dtype, name="norm_mix")
        lay = {"x0": x0, "h1": h1}
        if i == n_a:
            h_kv = _norm_fwd(x0, _vec(wts["kv_in_norm"]), scale=kv_scale, shift=kv_shift, name="norm_kv_in")
            pre = _mm(h_kv, w_dkvkr, name="kv_down", tn=kvr + LANES)
            ckv_pre, kr_pre = pre[:, :kvr], pre[:, kvr:]
            ckv = _norm_fwd(ckv_pre, _vec(wts["kv_norm"]), name="norm_kv")
            kv = _mm(ckv, w_ukv, out_dtype=BF16, name="kv_up")
            keys = _build_keys(kv, kr_pre, cos_t, sin_fwd, name="build_keys")
            t_attn = _attn_tile(s)
            v_t = kv.reshape(s // t_attn, t_attn, heads, 2, V_DIM)[:, :, :, 1].transpose(2, 0, 3, 1)
            kv_side = {"h_kv": h_kv, "ckv_pre": ckv_pre, "ckv": ckv, "kv": kv, "keys": keys, "v_t": v_t, "x0": x0}
        if i < n_a:
            pooled = _pool(h1, transpose=False, out_dtype=BF16, name="pool_fwd")
            y_pre = _gmm(pooled, pool_w[i], mode="nn", out_dtype=F32, name="pool_mix")
            gate_eff = gate_m * _vec(pool_scale[i])
            lay.update(pooled=pooled)
        else:
            l = i - n_a
            cq_pre = _mm(h1, w_dq[l], name="q_down")
            cq = _norm_fwd(cq_pre, _vec(wts["q_norm"][l]), name="norm_q")
            q = _rope_heads(_mm(cq, w_q[l], name="q_up"), cos_t, sin_fwd, out_dtype=BF16, name="rope_q")
            o_t, lse = _attn_fwd(q, kv_side["keys"], kv_side["v_t"], name="attn_fwd")
            o = o_t.T
            y_pre = _mm(o, w_o[l], name="attn_out")
            gate_eff = gate_m
            lay.update(cq_pre=cq_pre, cq=cq, q=q, o=o, lse=lse)
        x1, h2 = _norm_fwd(x0, _vec(norm_ffn[i]), scale=scale_f, shift=shift_f, y=y_pre, gate=gate_eff, name="norm_ffn")
        w_gate, w_up, w_down = ffn_w[i]
        if i + 1 < depth:
            (a, b, z), next_in = _ffn_in(h2, w_gate, w_up, 0, hosted=_gather_plan(placed_ffn[i + 1][:2], ffn_geoms[:2]), name="ffn_in")
            f, next_down = _mm(z, w_down, b_idx=0, hosted=_gather_plan(placed_ffn[i + 1][2:], ffn_geoms[2:]), name="ffn_down", **whole_k)
            ffn_w[i + 1] = next_in + next_down
        else:
            (a, b, z), _ = _ffn_in(h2, w_gate, w_up, 0, name="ffn_in_last")
            f = _mm(z, w_down, b_idx=0, name="ffn_down_last", **whole_k)
        lay.update(y_pre=y_pre, gate_eff=gate_eff, x1=x1, h2=h2, a=a, b=b, z=z, f=f)
        saved.append(lay)
        cur, pending = x1, (f, gate_f)

    dx, final_stats, loss_tile = _loss_bwd(cur, pending[0], pending[1], _vec(wts["final_norm"]), tgt, name="loss")
    loss = lax.psum(loss_tile[0, 0], ("x", "y", "c"))

    ffn_both = [None] * len(ffn_names)
    in_flight = None
    attn_names = tuple(n for n in mixer_names if n != "pool_w")
    grad_full = {}
    attn_parts = attn_slots = None

    def sum_chips(layer, parts, slots):
        for w, n in enumerate(ffn_names):
            ffn_both[w] = _chip_sum(parts[w], slots[w], ffn_geoms[w], place_arr, stack=(depth, layer, ffn_both[w]), name="chip_sum_" + n)

    g_wo, g_wq, g_wdq = [None] * n_b, [None] * n_b, [None] * n_b
    g_pool = [None] * n_a
    dmods = [None] * depth
    g_norm_mix, g_norm_ffn, g_q_norm, g_pool_scale = [None] * depth, [None] * depth, [None] * n_b, [None] * n_a
    dk_layers, dv_layers = [None] * n_b, [None] * n_b
    for i in reversed(range(depth)):
        lay = saved[i]
        shift_m, scale_m, gate_m, shift_f, scale_f, gate_f = mods[i]
        df, sums_gf = _gate_bwd(dx, lay["f"], gate_f, name="gate_bwd")
        w_gate, w_up, w_down = ffn_w[i]
        g_down = _mm(lay["z"], df, ta=True, out_dtype=BF16, name="ffn_down_dw", **whole_k)
        pair_down = _pair_plan([g_down[None]], ffn_geoms[2:])
        if in_flight is None:
            (da, db), _ = _ffn_down_bwd(df, w_down, 0, lay["a"], lay["b"], name="ffn_down_bwd_top")
            (dh2,), their_down = _ffn_in_dx(da, db, w_gate, w_up, 0, hosted=pair_down, name="ffn_in_dx_top")
            (g_gate, g_up), _ = _ffn_in_dw(lay["h2"], da, db, name="ffn_in_dw_top")
        else:
            carry_attn = attn_parts is not None and attn_slots is None
            plans = [_chips_plan(in_flight[:1]), pair_down] + ([_chips_plan(attn_parts)] if carry_attn else [])
            (da, db), got_down = _ffn_down_bwd(df, w_down, 0, lay["a"], lay["b"], hosted=_chips_plan(in_flight[2:]), name="ffn_down_bwd")
            (dh2,), got = _ffn_in_dx(da, db, w_gate, w_up, 0, hosted=_merge(plans), name="ffn_in_dx_attn" if carry_attn else "ffn_in_dx")
            got_gate, their_down = got[:1], got[1:2]
            if carry_attn:
                attn_slots = got[2:]
            (g_gate, g_up), got_up = _ffn_in_dw(lay["h2"], da, db, hosted=_chips_plan(in_flight[1:2]), name="ffn_in_dw")
            sum_chips(i + 1, in_flight, got_gate + got_up + got_down)
        (dx1, sums_f), their_in = _norm_bwd(
            lay["x1"], _vec(norm_ffn[i]), dh2, scale=scale_f, resid=dx, hosted=_pair_plan([g_gate[None], g_up[None]], ffn_geoms[:2]), name="norm_ffn_bwd"
        )
        ffn_grads = [g_gate[None], g_up[None], g_down[None]]
        in_flight = [
            _pair_add(g, th, geom, core_arr, name="pair_add_" + n) for g, th, geom, n in zip(ffn_grads, their_in + their_down, ffn_geoms, ffn_names)
        ]
        dyp, sums_gm = _gate_bwd(dx1, lay["y_pre"], lay["gate_eff"], name="gate_bwd")
        if i < n_a:
            g_pool[i] = _gmm(lay["pooled"], dyp, mode="tn", out_dtype=BF16, name="pool_mix_dw")
            dd = _gmm(dyp, pool_w[i], mode="nt", out_dtype=F32, name="pool_mix_dx")
            dh1 = _pool(dd, transpose=True, out_dtype=F32, name="pool_bwd")
            dgate_m = sums_gm[0] * pool_scale[i]
            g_pool_scale[i] = sums_gm[0] * gate_m[0]
        else:
            l = i - n_a
            do = _mm(dyp, w_o[l], tb=True, out_dtype=BF16, name="attn_out_dx")
            g_wo[l] = _mm(lay["o"], dyp, ta=True, out_dtype=BF16, name="attn_out_dw")
            dq, delta = _attn_bwd_q(lay["q"], kv_side["keys"], kv_side["kv"], lay["o"], do, lay["lse"].reshape(heads, s, 1), name="attn_bwd_q")
            dk_layers[l], dv_layers[l] = _attn_bwd_kv(
                lay["q"], kv_side["keys"], kv_side["kv"], do, lay["lse"], delta.reshape(lay["lse"].shape), name="attn_bwd_kv"
            )
            dq_pre = _rope_heads(dq, cos_t, sin_bwd, out_dtype=BF16, name="rope_q_bwd")
            dcq = _mm(dq_pre, w_q[l], tb=True, name="q_up_dx")
            g_wq[l] = _mm(lay["cq"], dq_pre, ta=True, out_dtype=BF16, name="q_up_dw")
            dcq_pre, sums_q = _norm_bwd(lay["cq_pre"], _vec(wts["q_norm"][l]), dcq, name="norm_q_bwd")
            g_q_norm[l] = sums_q[2]
            dh1 = _mm(dcq_pre, w_dq[l], tb=True, name="q_down_dx")
            g_wdq[l] = _mm(lay["h1"], dcq_pre, ta=True, out_dtype=BF16, name="q_down_dw")
            dgate_m = sums_gm[0]
        dx, sums_m = _norm_bwd(lay["x0"], _vec(norm_mix[i]), dh1, scale=scale_m, resid=dx1, name="norm_mix_bwd")
        if i == n_a:
            dkv, dkr_pre = _keys_bwd(dk_layers[0], dk_layers[1], dv_layers[0], dv_layers[1], cos_t, sin_bwd, name="keys_bwd")
            dckv = _mm(dkv, w_ukv, tb=True, name="kv_up_dx")
            g_ukv = _mm(kv_side["ckv"], dkv, ta=True, out_dtype=BF16, name="kv_up_dw")
            dckv_pre, sums_kvn = _norm_bwd(kv_side["ckv_pre"], _vec(wts["kv_norm"]), dckv, name="norm_kv_bwd")
            dpre = jnp.concatenate([dckv_pre, dkr_pre], axis=1)
            dh_kv = _mm(dpre, w_dkvkr, tb=True, name="kv_down_dx")
            g_dkvkr = _mm(kv_side["h_kv"], dpre, ta=True, out_dtype=BF16, name="kv_down_dw", tn=kvr + LANES)
            dx, sums_kv = _norm_bwd(lay["x0"], _vec(wts["kv_in_norm"]), dh_kv, scale=kv_scale, resid=dx, name="norm_kv_in_bwd")
            g_ukv = g_ukv.reshape(kvr, heads, 2, NOPE_DIM)
            grad_full.update(
                w_dkv=g_dkvkr[None, :, :kvr],
                w_uk=g_ukv[:, :, 0].reshape(1, kvr, heads * NOPE_DIM),
                w_uv=g_ukv[:, :, 1].reshape(1, kvr, heads * V_DIM),
                w_kr=g_dkvkr[None, :, kvr : kvr + ROPE_DIM],
                w_dq=jnp.stack(g_wdq),
                w_uq=jnp.stack(g_wq).reshape(n_b, qr, heads, HEAD_PAD)[..., : NOPE_DIM + ROPE_DIM].reshape(geoms["w_uq"].shape3),
                w_o=jnp.stack(g_wo),
            )
            attn_theirs = _run(_pair_plan([grad_full[n] for n in attn_names], [geoms[n] for n in attn_names]), name="reduce_pair_attn")
            attn_parts = [_pair_add(grad_full[n], th, geoms[n], core_arr, name="pair_add_" + n) for n, th in zip(attn_names, attn_theirs)]
        dmods[i] = jnp.concatenate([sums_m[0], sums_m[1], dgate_m, sums_f[0], sums_f[1], sums_gf[0]])
        g_norm_mix[i], g_norm_ffn[i] = sums_m[2], sums_f[2]
    grad_x = dx[None]
    grad_full["pool_w"] = jnp.stack(g_pool).reshape(geoms["pool_w"].shape3)

    small_grads = {
        "mod_b": jnp.concatenate(dmods),
        "kv_mod_b": jnp.concatenate([sums_kv[0], sums_kv[1]]),
        "norm_mix": jnp.concatenate(g_norm_mix),
        "norm_ffn": jnp.concatenate(g_norm_ffn),
        "kv_in_norm": sums_kv[2],
        "kv_norm": sums_kvn[2],
        "q_norm": jnp.concatenate(g_q_norm),
        "final_norm": final_stats[0],
    }
    packed = jnp.concatenate([small_grads[n] for n in SMALL] + g_pool_scale)
    small_rows = sum(wts[n].size for n in SMALL) // LANES
    every = _allgather8(_pad_rows(_rows(packed)), name="gather_small_grads")
    summed = _sum_devices(every, name="sum_small_grads")

    mod_rows = depth * nmod // LANES
    dm_all = every[:, :mod_rows].reshape(N_DEV, depth, nmod)
    dm = lax.dynamic_slice_in_dim(dm_all, chip * per, per, axis=2).transpose(1, 0, 2)
    dm = jnp.pad(dm, ((0, 0), (0, 16 - N_DEV), (0, 0)))
    dkvm_all = every[:, mod_rows : mod_rows + nkv // LANES].reshape(N_DEV, nkv)
    dkvm = jnp.pad(lax.dynamic_slice_in_dim(dkvm_all, chip * (nkv // N_CHIPS), nkv // N_CHIPS, axis=1), ((0, 16 - N_DEV), (0, 0)))[None]
    results = {}
    results["mod_w"] = _tp_adamw(sc16, dm, wts["mod_w"], mom["mod_w"], var["mod_w"], name="mod_w_update")
    results["kv_mod_w"] = [
        r[0] for r in _tp_adamw(sc16, dkvm, wts["kv_mod_w"][None], mom["kv_mod_w"][None], var["kv_mod_w"][None], name="kv_mod_w_update")
    ]

    ps_grad = lax.dynamic_slice_in_dim(summed[small_rows : small_rows + n_a * d // LANES].reshape(n_a, d), chip * (d // N_CHIPS), d // N_CHIPS, axis=1)
    small_names = SMALL + ("pool_scale",)

    def pack_small(tree):
        return _pad_rows(jnp.concatenate([_rows(tree[n]) for n in small_names]))

    g_small = _pad_rows(jnp.concatenate([summed[:small_rows], _rows(ps_grad)]))
    small_out = _adamw(pack_small(wts), g_small, pack_small(mom), pack_small(var), name="small_update")
    row = 0
    for n in small_names:
        nrow = wts[n].size // LANES
        results[n] = [r[row : row + nrow].reshape(wts[n].shape) for r in small_out]
        row += nrow

    assert attn_slots is not None
    theirs = _run(_pair_plan([grad_full["pool_w"]], [geoms["pool_w"]]), name="reduce_pair")
    pool_part = _pair_add(grad_full["pool_w"], theirs[0], geoms["pool_w"], core_arr, name="pair_add_pool_w")
    got = _run(_chips_plan([pool_part] + in_flight), name="reduce_chips")
    parts = dict(zip(attn_names, attn_parts), pool_w=pool_part)
    slots = dict(zip(attn_names, attn_slots), pool_w=got[0])
    boths = [_chip_sum(parts[n], slots[n], geoms[n], place_arr, name="chip_sum_" + n) for n in mixer_names]
    sum_chips(0, in_flight, got[1:])
    prefixes = [[()]] * len(mixer_names) + [[(layer,) for layer in range(depth)]] * len(ffn_names)
    joined = _run(_join_plan(boths + ffn_both, prefixes), name="join_pair")
    for n, both in zip(mixer_names + ffn_names, joined):
        cs = geoms[n].cs
        out = _adamw(wts[n].reshape(-1, cs), both.reshape(-1, cs), mom[n].reshape(-1, cs), var[n].reshape(-1, cs), name="update_" + n)
        results[n] = [r.reshape(wts[n].shape) for r in out]

    outs = [loss, grad_x]
    for k in range(4):
        outs += [results[n][k] for n in WEIGHTS]
    return tuple(outs)


def kernel(x, c, positions, mod_w, mod_b, norm_mix, norm_ffn, pool_w, pool_scale, kv_mod_w, kv_mod_b, kv_in_norm, w_dkv, kv_norm, w_uk, w_uv, w_kr, w_dq, q_norm, w_uq, w_o, ffn_gate, ffn_up, ffn_down, final_norm, loss_target, m_mod_w, m_mod_b, m_norm_mix, m_norm_ffn, m_pool_w, m_pool_scale, m_kv_mod_w, m_kv_mod_b, m_kv_in_norm, m_w_dkv, m_kv_norm, m_w_uk, m_w_uv, m_w_kr, m_w_dq, m_q_norm, m_w_uq, m_w_o, m_ffn_gate, m_ffn_up, m_ffn_down, m_final_norm, v_mod_w, v_mod_b, v_norm_mix, v_norm_ffn, v_pool_w, v_pool_scale, v_kv_mod_w, v_kv_mod_b, v_kv_in_norm, v_w_dkv, v_kv_norm, v_w_uk, v_w_uv, v_w_kr, v_w_dq, v_q_norm, v_w_uq, v_w_o, v_ffn_gate, v_ffn_up, v_ffn_down, v_final_norm):
    given = dict(locals())
    wts = {n: given[n] for n in WEIGHTS}
    mom = {n: given["m_" + n] for n in WEIGHTS}
    var = {n: given["v_" + n] for n in WEIGHTS}
    return _step(x, c, positions, loss_target, wts, mom, var)
```

```python
import functools

import jax
import jax.numpy as jnp
from jax import lax
from jax.experimental import pallas as pl
from jax.experimental.pallas import tpu as pltpu

F32 = jnp.float32
BF16 = jnp.bfloat16
MESH = pl.DeviceIdType.MESH
ANY = pl.BlockSpec(memory_space=pl.ANY)

NORM_EPS = 1e-6
POOL_WINDOWS = (2, 4, 8, 16)
NOPE_DIM = 128
ROPE_DIM = 64
V_DIM = 128
HEAD_PAD = 256
SM_SCALE = (NOPE_DIM + ROPE_DIM) ** -0.5
ROPE_THETA = 10000.0
N_MOD = 6
ADAM_LR, ADAM_B1, ADAM_B2, ADAM_EPS, ADAM_WD, ADAM_STEP = 0.001, 0.9, 0.999, 1e-08, 0.01, 10
N_CHIPS = 4
N_DEV = 8
LANES = 128
HALO = 128
VMEM_LIMIT = 48 * 1024 * 1024


def _tile(dim, pref, align):
    if dim <= pref:
        return dim
    t = (pref // align) * align
    while t >= align:
        if dim % t == 0:
            return t
        t -= align
    return dim


def _params(*sem):
    return pltpu.CompilerParams(dimension_semantics=sem, vmem_limit_bytes=VMEM_LIMIT)


class _Hosted:
    def __init__(self, args, out_shapes, aliases, sem_shapes, start, finish):
        self.args, self.out_shapes, self.aliases, self.sem_shapes = list(args), list(out_shapes), dict(aliases), list(sem_shapes)
        self.start, self.finish = start, finish


def _call(body, *, name, grid, in_specs, out_specs, out_shape, args, sem, scratch_shapes=(), hosted=None):
    n_in, n_out, n_scr = len(args), len(out_shape), len(scratch_shapes)
    if hosted is None:
        outs = pl.pallas_call(
            body, name=name, grid=grid, in_specs=list(in_specs), out_specs=list(out_specs), out_shape=list(out_shape),
            scratch_shapes=list(scratch_shapes), compiler_params=_params(*sem),
        )(*args)
        return list(outs), []
    n_hin, n_hout = len(hosted.args), len(hosted.out_shapes)

    def carrying(*refs):
        own_in, their_in = refs[:n_in], refs[n_in : n_in + n_hin]
        refs = refs[n_in + n_hin :]
        own_out, their_out = refs[:n_out], refs[n_out : n_out + n_hout]
        refs = refs[n_out + n_hout :]
        own_scratch, sems = refs[:n_scr], refs[n_scr:]
        ids = [pl.program_id(axis) for axis in range(len(grid))]
        first = functools.reduce(jnp.logical_and, [i == 0 for i in ids])
        last = functools.reduce(jnp.logical_and, [i == size - 1 for i, size in zip(ids, grid)])

        @pl.when(first)
        def _():
            hosted.start(their_in, their_out, sems)

        body(*own_in, *own_out, *own_scratch)

        @pl.when(last)
        def _():
            hosted.finish(their_in, their_out, sems)

    outs = pl.pallas_call(
        carrying,
        name=name,
        grid=grid,
        in_specs=list(in_specs) + [ANY] * n_hin,
        out_specs=list(out_specs) + [ANY] * n_hout,
        out_shape=list(out_shape) + hosted.out_shapes,
        input_output_aliases={n_in + i: n_out + o for i, o in hosted.aliases.items()},
        scratch_shapes=list(scratch_shapes) + hosted.sem_shapes,
        compiler_params=_params(*["arbitrary"] * len(grid)),
    )(*args, *hosted.args)
    return list(outs[:n_out]), list(outs[n_out:])


def _merge(plans):
    if len(plans) == 1:
        return plans[0]
    args, out_shapes, aliases, sem_shapes, spans = [], [], {}, [], []
    for p in plans:
        spans.append((len(args), len(out_shapes), len(sem_shapes)))
        aliases.update({len(args) + i: len(out_shapes) + o for i, o in p.aliases.items()})
        args, out_shapes, sem_shapes = args + p.args, out_shapes + p.out_shapes, sem_shapes + p.sem_shapes

    def each(method, ins, outs, sems):
        for p, (a0, o0, s0) in zip(plans, spans):
            getattr(p, method)(ins[a0 : a0 + len(p.args)], outs[o0 : o0 + len(p.out_shapes)], sems[s0 : s0 + len(p.sem_shapes)])

    return _Hosted(args, out_shapes, aliases, sem_shapes, functools.partial(each, "start"), functools.partial(each, "finish"))


def _run(plan, *, name):
    n_in, n_out = len(plan.args), len(plan.out_shapes)

    def body(*refs):
        ins, outs, sems = refs[:n_in], refs[n_in : n_in + n_out], refs[n_in + n_out :]
        plan.start(ins, outs, sems)
        plan.finish(ins, outs, sems)

    return pl.pallas_call(
        body, name=name, in_specs=[ANY] * n_in, out_specs=[ANY] * n_out, out_shape=plan.out_shapes,
        input_output_aliases=plan.aliases, scratch_shapes=plan.sem_shapes,
    )(*plan.args)


def _mm(a, b, *, name, ta=False, tb=False, out_dtype=F32, b_idx=None, hosted=None, tm=1024, tn=1024, tk=None):
    m, k = (a.shape[1], a.shape[0]) if ta else a.shape
    b2 = b.shape if b_idx is None else b.shape[1:]
    kb, n = (b2[1], b2[0]) if tb else b2
    assert k == kb, (a.shape, b.shape, ta, tb)
    tm = _tile(m, tm, LANES)
    tn = _tile(n, tn, LANES)
    tk = _tile(k, 2048 if tk is None else tk, LANES)
    nk = k // tk
    dims = (((0 if ta else 1,), (1 if tb else 0,)), ((), ()))

    def body(a_ref, b_ref, o_ref, *acc):
        part = lax.dot_general(a_ref[...].astype(BF16), b_ref[...].astype(BF16), dims, preferred_element_type=F32)
        if nk == 1:
            o_ref[...] = part.astype(o_ref.dtype)
        else:
            acc_ref = acc[0]
            step = pl.program_id(2)

            @pl.when(step == 0)
            def _():
                acc_ref[...] = part

            @pl.when(step > 0)
            def _():
                acc_ref[...] += part

            @pl.when(step == nk - 1)
            def _():
                o_ref[...] = acc_ref[...].astype(o_ref.dtype)

    a_spec = pl.BlockSpec((tk, tm), lambda i, j, s: (s, i)) if ta else pl.BlockSpec((tm, tk), lambda i, j, s: (i, s))
    if b_idx is None:
        b_spec = pl.BlockSpec((tn, tk), lambda i, j, s: (j, s)) if tb else pl.BlockSpec((tk, tn), lambda i, j, s: (s, j))
    elif tb:
        b_spec = pl.BlockSpec((None, tn, tk), lambda i, j, s: (b_idx, j, s))
    else:
        b_spec = pl.BlockSpec((None, tk, tn), lambda i, j, s: (b_idx, s, j))
    outs, carried = _call(
        body,
        name=name,
        grid=(m // tm, n // tn, nk),
        in_specs=[a_spec, b_spec],
        out_specs=[pl.BlockSpec((tm, tn), lambda i, j, s: (i, j))],
        out_shape=[jax.ShapeDtypeStruct((m, n), out_dtype)],
        args=[a, b],
        sem=("parallel", "parallel", "arbitrary"),
        scratch_shapes=[pltpu.VMEM((tm, tn), F32)] if nk > 1 else [],
        hosted=hosted,
    )
    return outs[0] if hosted is None else (outs[0], carried)


def _tp_fwd(sc16, w, bias, *, name):
    nl, d, n = w.shape
    tn = _tile(n, 512, LANES)

    def body(sc_ref, w_ref, b_ref, o_ref):
        o_ref[0] = jnp.dot(sc_ref[...].astype(BF16), w_ref[0].astype(BF16), preferred_element_type=F32) + b_ref[0]

    return pl.pallas_call(
        body,
        name=name,
        grid=(nl, n // tn),
        in_specs=[
            pl.BlockSpec((16, d), lambda l, j: (0, 0)),
            pl.BlockSpec((1, d, tn), lambda l, j: (l, 0, j)),
            pl.BlockSpec((1, 1, tn), lambda l, j: (l, 0, j)),
        ],
        out_specs=pl.BlockSpec((1, 16, tn), lambda l, j: (l, 0, j)),
        out_shape=jax.ShapeDtypeStruct((nl, 16, n), F32),
        compiler_params=_params("parallel", "parallel"),
    )(sc16, w, bias)


_NT = (((1,), (1,)), ((), ()))
_TN = (((0,), (0,)), ((), ()))


def _silu_parts(a):
    sig = jax.nn.sigmoid(a)
    return a * sig, sig * (1.0 + a * (1.0 - sig))


def _ffn_in(h, w_gate, w_up, layer, *, name, hosted=None):
    s, d = h.shape
    f = w_gate.shape[2]
    tm, tn = _tile(s, 1024, LANES), _tile(f, 512, LANES)

    def body(h_ref, g_ref, u_ref, a_ref, b_ref, z_ref):
        hv = h_ref[...]
        a = jnp.dot(hv, g_ref[...], preferred_element_type=F32)
        b = jnp.dot(hv, u_ref[...], preferred_element_type=F32)
        a_ref[...] = a.astype(a_ref.dtype)
        b_ref[...] = b.astype(b_ref.dtype)
        z_ref[...] = (_silu_parts(a)[0] * b).astype(z_ref.dtype)

    w_spec = pl.BlockSpec((None, d, tn), lambda i, j: (layer, 0, j))
    out = pl.BlockSpec((tm, tn), lambda i, j: (i, j))
    return _call(
        body,
        name=name,
        grid=(s // tm, f // tn),
        in_specs=[pl.BlockSpec((tm, d), lambda i, j: (i, 0)), w_spec, w_spec],
        out_specs=[out] * 3,
        out_shape=[jax.ShapeDtypeStruct((s, f), BF16)] * 3,
        args=[h, w_gate, w_up],
        sem=("parallel", "parallel"),
        hosted=hosted,
    )


def _ffn_down_bwd(df, w_down, layer, a, b, *, name, hosted=None):
    s, d = df.shape
    f = w_down.shape[1]
    tm, tn = _tile(s, 1024, LANES), _tile(f, 512, LANES)

    def body(df_ref, w_ref, a_ref, b_ref, da_ref, db_ref):
        dz = lax.dot_general(df_ref[...], w_ref[...], _NT, preferred_element_type=F32)
        silu, dsilu = _silu_parts(a_ref[...].astype(F32))
        da_ref[...] = (dz * b_ref[...].astype(F32) * dsilu).astype(da_ref.dtype)
        db_ref[...] = (dz * silu).astype(db_ref.dtype)

    blk = pl.BlockSpec((tm, tn), lambda i, j: (i, j))
    return _call(
        body,
        name=name,
        grid=(s // tm, f // tn),
        in_specs=[pl.BlockSpec((tm, d), lambda i, j: (i, 0)), pl.BlockSpec((None, tn, d), lambda i, j: (layer, j, 0)), blk, blk],
        out_specs=[blk, blk],
        out_shape=[jax.ShapeDtypeStruct((s, f), BF16)] * 2,
        args=[df, w_down, a, b],
        sem=("parallel", "parallel"),
        hosted=hosted,
    )


def _ffn_in_dx(da, db, w_gate, w_up, layer, *, name, hosted=None):
    s, f = da.shape
    d = w_gate.shape[1]
    tm, tn, tk = _tile(s, 512, LANES), _tile(d, 1024, LANES), _tile(f, 3072, LANES)
    nk = f // tk

    def body(da_ref, db_ref, g_ref, u_ref, o_ref, acc_ref):
        part = lax.dot_general(da_ref[...], g_ref[...], _NT, preferred_element_type=F32)
        part = part + lax.dot_general(db_ref[...], u_ref[...], _NT, preferred_element_type=F32)
        step = pl.program_id(2)

        @pl.when(step == 0)
        def _():
            acc_ref[...] = part

        @pl.when(step > 0)
        def _():
            acc_ref[...] += part

        @pl.when(step == nk - 1)
        def _():
            o_ref[...] = acc_ref[...]

    x_spec = pl.BlockSpec((tm, tk), lambda i, j, k: (i, k))
    w_spec = pl.BlockSpec((None, tn, tk), lambda i, j, k: (layer, j, k))
    return _call(
        body,
        name=name,
        grid=(s // tm, d // tn, nk),
        in_specs=[x_spec, x_spec, w_spec, w_spec],
        out_specs=[pl.BlockSpec((tm, tn), lambda i, j, k: (i, j))],
        out_shape=[jax.ShapeDtypeStruct((s, d), F32)],
        args=[da, db, w_gate, w_up],
        sem=("parallel", "parallel", "arbitrary"),
        scratch_shapes=[pltpu.VMEM((tm, tn), F32)],
        hosted=hosted,
    )


def _ffn_in_dw(h, da, db, *, name, hosted=None):
    s, d = h.shape
    f = da.shape[1]
    tm, tn, tk = _tile(d, 1024, LANES), _tile(f, 512, LANES), _tile(s, 4096, LANES)
    nk = s // tk

    def body(h_ref, da_ref, db_ref, g_ref, u_ref, *acc):
        hv = h_ref[...]
        pg = lax.dot_general(hv, da_ref[...], _TN, preferred_element_type=F32)
        pu = lax.dot_general(hv, db_ref[...], _TN, preferred_element_type=F32)
        if nk == 1:
            g_ref[...] = pg.astype(g_ref.dtype)
            u_ref[...] = pu.astype(u_ref.dtype)
            return
        g_acc, u_acc = acc
        step = pl.program_id(2)

        @pl.when(step == 0)
        def _():
            g_acc[...] = pg
            u_acc[...] = pu

        @pl.when(step > 0)
        def _():
            g_acc[...] += pg
            u_acc[...] += pu

        @pl.when(step == nk - 1)
        def _():
            g_ref[...] = g_acc[...].astype(g_ref.dtype)
            u_ref[...] = u_acc[...].astype(u_ref.dtype)

    y_spec = pl.BlockSpec((tk, tn), lambda i, j, k: (k, j))
    out = pl.BlockSpec((tm, tn), lambda i, j, k: (i, j))
    return _call(
        body,
        name=name,
        grid=(d // tm, f // tn, nk),
        in_specs=[pl.BlockSpec((tk, tm), lambda i, j, k: (k, i)), y_spec, y_spec],
        out_specs=[out, out],
        out_shape=[jax.ShapeDtypeStruct((d, f), BF16)] * 2,
        args=[h, da, db],
        sem=("parallel", "parallel", "arbitrary"),
        scratch_shapes=[pltpu.VMEM((tm, tn), F32)] * 2 if nk > 1 else [],
        hosted=hosted,
    )


def _gmm(a, w, *, name, mode, out_dtype):
    s = a.shape[0]
    g = len(POOL_WINDOWS)
    c = a.shape[1] // g
    tr = _tile(s, 1024, LANES)
    n_row = s // tr

    if mode == "tn":

        def body(a_ref, b_ref, o_ref, acc_ref):
            part = lax.dot_general(a_ref[...].astype(BF16), b_ref[...].astype(BF16), (((0,), (0,)), ((), ())), preferred_element_type=F32)

            @pl.when(pl.program_id(1) == 0)
            def _():
                acc_ref[...] = part

            @pl.when(pl.program_id(1) > 0)
            def _():
                acc_ref[...] += part

            @pl.when(pl.program_id(1) == n_row - 1)
            def _():
                o_ref[0] = acc_ref[...].astype(o_ref.dtype)

        return pl.pallas_call(
            body,
            name=name,
            grid=(g, n_row),
            in_specs=[pl.BlockSpec((tr, c), lambda gi, i: (i, gi)), pl.BlockSpec((tr, c), lambda gi, i: (i, gi))],
            out_specs=pl.BlockSpec((1, c, c), lambda gi, i: (gi, 0, 0)),
            out_shape=jax.ShapeDtypeStruct((g, c, c), out_dtype),
            scratch_shapes=[pltpu.VMEM((c, c), F32)],
            compiler_params=_params("parallel", "arbitrary"),
        )(a, w)

    dims = (((1,), (0 if mode == "nn" else 1,)), ((), ()))

    def body(a_ref, w_ref, o_ref):
        o_ref[...] = lax.dot_general(a_ref[...].astype(BF16), w_ref[0].astype(BF16), dims, preferred_element_type=F32).astype(o_ref.dtype)

    return pl.pallas_call(
        body,
        name=name,
        grid=(g, n_row),
        in_specs=[pl.BlockSpec((tr, c), lambda gi, i: (i, gi)), pl.BlockSpec((1, c, c), lambda gi, i: (gi, 0, 0))],
        out_specs=pl.BlockSpec((tr, c), lambda gi, i: (i, gi)),
        out_shape=jax.ShapeDtypeStruct((s, g * c), out_dtype),
        compiler_params=_params("parallel", "parallel"),
    )(a, w)


def _row_tile(s, d):
    return _tile(s, max(8, (1 << 19) // d), 8)


def _norm_fwd(x, g, *, name, scale=None, shift=None, y=None, gate=None, out_dtype=BF16):
    s, d = x.shape
    tr = _row_tile(s, d)
    has_res, has_mod = y is not None, scale is not None

    def body(*refs):
        refs = list(refs)
        x_ref = refs.pop(0)
        xv = x_ref[...]
        if has_res:
            y_ref, gate_ref = refs.pop(0), refs.pop(0)
            xv = xv + gate_ref[...] * y_ref[...]
        g_ref = refs.pop(0)
        if has_mod:
            scale_ref, shift_ref = refs.pop(0), refs.pop(0)
        if has_res:
            refs.pop(0)[...] = xv
        h = xv * lax.rsqrt(jnp.mean(xv * xv, axis=-1, keepdims=True) + NORM_EPS)
        h = h * g_ref[...]
        if has_mod:
            h = h * (1.0 + scale_ref[...]) + shift_ref[...]
        refs.pop(0)[...] = h.astype(out_dtype)

    row = pl.BlockSpec((tr, d), lambda i: (i, 0))
    vec = pl.BlockSpec((1, d), lambda i: (0, 0))
    args, in_specs = [x], [row]
    if has_res:
        args += [y, gate]
        in_specs += [row, vec]
    args.append(g)
    in_specs.append(vec)
    if has_mod:
        args += [scale, shift]
        in_specs += [vec, vec]
    out_shape, out_specs = [], []
    if has_res:
        out_shape.append(jax.ShapeDtypeStruct((s, d), F32))
        out_specs.append(row)
    out_shape.append(jax.ShapeDtypeStruct((s, d), out_dtype))
    out_specs.append(row)
    res = pl.pallas_call(
        body, name=name, grid=(s // tr,), in_specs=in_specs, out_specs=out_specs, out_shape=out_shape, compiler_params=_params("parallel")
    )(*args)
    return (res[0], res[1]) if has_res else res[0]


def _norm_bwd(x, g, dh, *, name, scale=None, resid=None, hosted=None):
    s, d = x.shape
    tr = _row_tile(s, d)
    has_mod, has_res = scale is not None, resid is not None

    def body(*refs):
        refs = list(refs)
        x_ref, g_ref, dh_ref = refs.pop(0), refs.pop(0), refs.pop(0)
        scale_ref = refs.pop(0) if has_mod else None
        resid_ref = refs.pop(0) if has_res else None
        dx_ref, sums_ref = refs
        xv = x_ref[...]
        r = lax.rsqrt(jnp.mean(xv * xv, axis=-1, keepdims=True) + NORM_EPS)
        xhat = xv * r
        dh32 = dh_ref[...].astype(F32)
        gv = g_ref[...]
        dn = dh32 * (1.0 + scale_ref[...]) if has_mod else dh32
        dxhat = dn * gv
        dx = r * (dxhat - xhat * jnp.mean(dxhat * xhat, axis=-1, keepdims=True))
        if has_res:
            dx = dx + resid_ref[...]
        dx_ref[...] = dx

        @pl.when(pl.program_id(0) == 0)
        def _():
            sums_ref[...] = jnp.zeros_like(sums_ref)

        sums_ref[0:1, :] += jnp.sum(dh32, axis=0, keepdims=True)
        sums_ref[1:2, :] += jnp.sum(dh32 * (xhat * gv), axis=0, keepdims=True)
        sums_ref[2:3, :] += jnp.sum(dn * xhat, axis=0, keepdims=True)

    row = pl.BlockSpec((tr, d), lambda i: (i, 0))
    vec = pl.BlockSpec((1, d), lambda i: (0, 0))
    args, in_specs = [x, g, dh], [row, vec, row]
    if has_mod:
        args.append(scale)
        in_specs.append(vec)
    if has_res:
        args.append(resid)
        in_specs.append(row)
    outs, carried = _call(
        body,
        name=name,
        grid=(s // tr,),
        in_specs=in_specs,
        out_specs=[row, pl.BlockSpec((8, d), lambda i: (0, 0))],
        out_shape=[jax.ShapeDtypeStruct((s, d), F32), jax.ShapeDtypeStruct((8, d), F32)],
        args=args,
        sem=("arbitrary",),
        hosted=hosted,
    )
    return outs if hosted is None else (outs, carried)


def _gate_bwd(dx, y, gate, *, name):
    s, d = dx.shape
    tr = _row_tile(s, d)

    def body(dx_ref, y_ref, gate_ref, dy_ref, sums_ref):
        dxv = dx_ref[...]
        dy_ref[...] = (dxv * gate_ref[...]).astype(dy_ref.dtype)

        @pl.when(pl.program_id(0) == 0)
        def _():
            sums_ref[...] = jnp.zeros_like(sums_ref)

        sums_ref[0:1, :] += jnp.sum(dxv * y_ref[...], axis=0, keepdims=True)

    row = pl.BlockSpec((tr, d), lambda i: (i, 0))
    return pl.pallas_call(
        body,
        name=name,
        grid=(s // tr,),
        in_specs=[row, row, pl.BlockSpec((1, d), lambda i: (0, 0))],
        out_specs=[row, pl.BlockSpec((8, d), lambda i: (0, 0))],
        out_shape=[jax.ShapeDtypeStruct((s, d), BF16), jax.ShapeDtypeStruct((8, d), F32)],
        compiler_params=_params("arbitrary"),
    )(dx, y, gate)


def _elementwise(fn, args, out_dtypes, *, name):
    s, d = args[0].shape
    tc = d if d <= 2048 else _tile(d, 1024, LANES)
    tr = _tile(s, max(8, (1 << 18) // tc), 8)
    n_in = len(args)

    def body(*refs):
        outs = fn(*[r[...] for r in refs[:n_in]])
        for o_ref, o in zip(refs[n_in:], outs):
            o_ref[...] = o.astype(o_ref.dtype)

    spec = pl.BlockSpec((tr, tc), lambda i, j: (i, j))
    return pl.pallas_call(
        body,
        name=name,
        grid=(s // tr, d // tc),
        in_specs=[spec] * n_in,
        out_specs=[spec] * len(out_dtypes),
        out_shape=[jax.ShapeDtypeStruct((s, d), dt) for dt in out_dtypes],
        compiler_params=_params("parallel", "parallel"),
    )(*args)


def _silu(v):
    return (v * jax.nn.sigmoid(v),)


def _split3(v):
    hi = v.astype(BF16)
    r1 = v - hi.astype(F32)
    mid = r1.astype(BF16)
    lo = (r1 - mid.astype(F32)).astype(BF16)
    return hi, mid, lo


def _band_dot(band, v):
    return sum(jnp.dot(band, part, preferred_element_type=F32) for part in _split3(v))


def _pool(h, *, name, transpose, out_dtype):
    s, d = h.shape
    c = d // len(POOL_WINDOWS)
    tr = _tile(s, 256, HALO)
    per = tr // HALO
    n_halo = s // HALO

    def body(h_ref, halo_ref, o_ref):
        i = pl.program_id(0)
        out_row = i * tr + lax.broadcasted_iota(jnp.int32, (tr, tr + HALO), 0)
        col = lax.broadcasted_iota(jnp.int32, (tr, tr + HALO), 1)
        if transpose:
            ext = jnp.concatenate([h_ref[...], halo_ref[...]], axis=0)
            src_row = i * tr + col
            ext_row = i * tr + lax.broadcasted_iota(jnp.int32, (tr + HALO, 1), 0)
        else:
            ext = jnp.concatenate([halo_ref[...], h_ref[...]], axis=0)
            src_row = i * tr + col - HALO
            own_row = i * tr + lax.broadcasted_iota(jnp.int32, (tr, 1), 0)
        for gi, w in enumerate(POOL_WINDOWS):
            cols = slice(gi * c, (gi + 1) * c)
            if transpose:
                band = (src_row >= out_row) & (src_row < out_row + w) & (src_row < s)
                scaled = ext[:, cols] / jnp.minimum(ext_row + 1, w).astype(F32)
                res = _band_dot(band.astype(BF16), scaled) - h_ref[:, cols]
            else:
                band = (src_row <= out_row) & (src_row > out_row - w) & (src_row >= 0)
                res = _band_dot(band.astype(BF16), ext[:, cols]) / jnp.minimum(own_row + 1, w).astype(F32) - h_ref[:, cols]
            o_ref[:, cols] = res.astype(o_ref.dtype)

    if transpose:
        halo_map = lambda i: (jnp.minimum((i + 1) * per, n_halo - 1), 0)
    else:
        halo_map = lambda i: (jnp.maximum(i * per - 1, 0), 0)
    return pl.pallas_call(
        body,
        name=name,
        grid=(s // tr,),
        in_specs=[pl.BlockSpec((tr, d), lambda i: (i, 0)), pl.BlockSpec((HALO, d), halo_map)],
        out_specs=pl.BlockSpec((tr, d), lambda i: (i, 0)),
        out_shape=jax.ShapeDtypeStruct((s, d), out_dtype),
        compiler_params=_params("parallel"),
    )(h, h)


def _rotate(v, cos, sin):
    lane = lax.broadcasted_iota(jnp.int32, v.shape, 1)
    swapped = jnp.where(lane % ROPE_DIM < ROPE_DIM // 2, pltpu.roll(v, LANES - ROPE_DIM // 2, 1), pltpu.roll(v, ROPE_DIM // 2, 1))
    return v * cos + swapped * sin


def _rope_heads(x, cos, sin, *, name, out_dtype):
    s, n = x.shape
    tr = _tile(s, max(16, (1 << 18) // n), 16)

    def body(x_ref, cos_ref, sin_ref, o_ref):
        cos_v, sin_v = cos_ref[...], sin_ref[...]
        for j in range(n // LANES):
            lanes = slice(j * LANES, (j + 1) * LANES)
            if j % 2 == 0:
                o_ref[:, lanes] = x_ref[:, lanes].astype(o_ref.dtype)
            else:
                o_ref[:, lanes] = _rotate(x_ref[:, lanes].astype(F32), cos_v, sin_v).astype(o_ref.dtype)

    blk = pl.BlockSpec((tr, n), lambda i: (i, 0))
    tab = pl.BlockSpec((tr, LANES), lambda i: (i, 0))
    return pl.pallas_call(
        body,
        name=name,
        grid=(s // tr,),
        in_specs=[blk, tab, tab],
        out_specs=blk,
        out_shape=jax.ShapeDtypeStruct((s, n), out_dtype),
        compiler_params=_params("parallel"),
    )(x, cos, sin)


def _build_keys(kv, kr_pre, cos, sin, *, name):
    s, n = kv.shape
    heads = n // HEAD_PAD
    t = _attn_tile(s)

    def body(kv_ref, kr_ref, cos_ref, sin_ref, keys_ref, kt_ref, vt_ref):
        rope = _rotate(kr_ref[...], cos_ref[...], sin_ref[...])
        nope, val = kv_ref[:, :NOPE_DIM], kv_ref[:, NOPE_DIM:]
        keys_ref[:, :NOPE_DIM] = nope
        keys_ref[:, NOPE_DIM:] = rope.astype(keys_ref.dtype)
        kt_ref[0, 0, :NOPE_DIM, :] = nope.astype(F32).T.astype(kt_ref.dtype)
        kt_ref[0, 0, NOPE_DIM:, :] = rope.T.astype(kt_ref.dtype)
        vt_ref[0, 0] = val.astype(F32).T.astype(vt_ref.dtype)

    tab = pl.BlockSpec((t, LANES), lambda hd, kb: (kb, 0))
    return pl.pallas_call(
        body,
        name=name,
        grid=(heads, s // t),
        in_specs=[pl.BlockSpec((t, HEAD_PAD), lambda hd, kb: (kb, hd)), tab, tab, tab],
        out_specs=[
            pl.BlockSpec((t, HEAD_PAD), lambda hd, kb: (kb, hd)),
            pl.BlockSpec((1, 1, HEAD_PAD, t), lambda hd, kb: (hd, kb, 0, 0)),
            pl.BlockSpec((1, 1, V_DIM, t), lambda hd, kb: (hd, kb, 0, 0)),
        ],
        out_shape=[
            jax.ShapeDtypeStruct((s, n), BF16),
            jax.ShapeDtypeStruct((heads, s // t, HEAD_PAD, t), BF16),
            jax.ShapeDtypeStruct((heads, s // t, V_DIM, t), BF16),
        ],
        compiler_params=_params("parallel", "parallel"),
    )(kv, kr_pre, cos, sin)


def _rope_back(dq_t, cos, sin, *, name):
    heads, nq, _, t = dq_t.shape

    def body(x_ref, cos_ref, sin_ref, o_ref):
        x = x_ref[0, 0].T
        o_ref[:, :NOPE_DIM] = x[:, :NOPE_DIM].astype(o_ref.dtype)
        o_ref[:, NOPE_DIM:] = _rotate(x[:, NOPE_DIM:], cos_ref[...], sin_ref[...]).astype(o_ref.dtype)

    tab = pl.BlockSpec((t, LANES), lambda hd, qb: (qb, 0))
    return pl.pallas_call(
        body,
        name=name,
        grid=(heads, nq),
        in_specs=[pl.BlockSpec((1, 1, HEAD_PAD, t), lambda hd, qb: (hd, qb, 0, 0)), tab, tab],
        out_specs=pl.BlockSpec((t, HEAD_PAD), lambda hd, qb: (qb, hd)),
        out_shape=jax.ShapeDtypeStruct((nq * t, heads * HEAD_PAD), BF16),
        compiler_params=_params("parallel", "parallel"),
    )(dq_t, cos, sin)


def _keys_bwd(dk_a, dk_b, dv_a, dv_b, cos, sin_neg, *, name):
    s, n = dk_a.shape
    heads = n // HEAD_PAD
    tr = _tile(s, 512, 8)

    def body(dka_ref, dkb_ref, dva_ref, dvb_ref, cos_ref, sin_ref, dkv_ref, dkr_ref):
        hd = pl.program_id(1)
        dk = dka_ref[...] + dkb_ref[...]
        dkv_ref[:, :NOPE_DIM] = dk[:, :NOPE_DIM].astype(dkv_ref.dtype)
        dkv_ref[:, NOPE_DIM:] = (dva_ref[...] + dvb_ref[...]).astype(dkv_ref.dtype)

        @pl.when(hd == 0)
        def _():
            dkr_ref[...] = dk[:, NOPE_DIM:]

        @pl.when(hd > 0)
        def _():
            dkr_ref[...] += dk[:, NOPE_DIM:]

        @pl.when(hd == heads - 1)
        def _():
            dkr_ref[...] = _rotate(dkr_ref[...], cos_ref[...], sin_ref[...])

    dk_blk = pl.BlockSpec((tr, HEAD_PAD), lambda i, hd: (i, hd))
    dv_blk = pl.BlockSpec((tr, V_DIM), lambda i, hd: (i, hd))
    tab = pl.BlockSpec((tr, LANES), lambda i, hd: (i, 0))
    return pl.pallas_call(
        body,
        name=name,
        grid=(s // tr, heads),
        in_specs=[dk_blk, dk_blk, dv_blk, dv_blk, tab, tab],
        out_specs=[dk_blk, tab],
        out_shape=[jax.ShapeDtypeStruct((s, n), BF16), jax.ShapeDtypeStruct((s, LANES), F32)],
        compiler_params=_params("parallel", "arbitrary"),
    )(dk_a, dk_b, dv_a, dv_b, cos, sin_neg)


def _attn_tile(s):
    return _tile(s, 512, LANES)


def _causal_mask(t):
    return lax.broadcasted_iota(jnp.int32, (t, t), 0) <= lax.broadcasted_iota(jnp.int32, (t, t), 1)


def _attn_fwd(q, keys, v_t, *, name):
    s = q.shape[0]
    heads = q.shape[1] // HEAD_PAD
    t = _attn_tile(s)
    nq = s // t

    def body(q_ref, k_ref, v_ref, o_ref, lse_ref, m_ref, l_ref, acc_ref):
        qi = pl.program_id(1)
        qv = q_ref[...]
        m_ref[...] = jnp.full_like(m_ref, -jnp.inf)
        l_ref[...] = jnp.zeros_like(l_ref)
        acc_ref[...] = jnp.zeros_like(acc_ref)

        def block(kb, diagonal):
            rows = pl.ds(pl.multiple_of(kb * t, t), t)
            sc_t = lax.dot_general(k_ref[rows, :], qv, _NT, preferred_element_type=F32) * SM_SCALE
            if diagonal:
                sc_t = jnp.where(_causal_mask(t), sc_t, -jnp.inf)
            m_old = m_ref[...]
            m_new = jnp.maximum(m_old, jnp.max(sc_t, axis=0, keepdims=True))
            alpha = jnp.exp(m_old - m_new)
            p_t = jnp.exp(sc_t - m_new)
            l_ref[...] = alpha * l_ref[...] + jnp.sum(p_t, axis=0, keepdims=True)
            acc_ref[...] = alpha * acc_ref[...] + jnp.dot(v_ref[0, kb], p_t.astype(BF16), preferred_element_type=F32)
            m_ref[...] = m_new

        def earlier(kb, carry):
            block(kb, False)
            return carry

        lax.fori_loop(0, qi, earlier, 0)
        block(qi, True)
        o_ref[...] = (acc_ref[...] / l_ref[...]).T.astype(o_ref.dtype)
        lse_ref[0, 0] = m_ref[...] + jnp.log(l_ref[...])

    return pl.pallas_call(
        body,
        name=name,
        grid=(heads, nq),
        in_specs=[
            pl.BlockSpec((t, HEAD_PAD), lambda hd, qi: (qi, hd)),
            pl.BlockSpec((s, HEAD_PAD), lambda hd, qi: (0, hd)),
            pl.BlockSpec((1, nq, V_DIM, t), lambda hd, qi: (hd, 0, 0, 0)),
        ],
        out_specs=[pl.BlockSpec((t, V_DIM), lambda hd, qi: (qi, hd)), pl.BlockSpec((1, 1, 1, t), lambda hd, qi: (hd, qi, 0, 0))],
        out_shape=[jax.ShapeDtypeStruct((s, heads * V_DIM), BF16), jax.ShapeDtypeStruct((heads, nq, 1, t), F32)],
        scratch_shapes=[pltpu.VMEM((1, t), F32), pltpu.VMEM((1, t), F32), pltpu.VMEM((V_DIM, t), F32)],
        compiler_params=_params("parallel", "parallel"),
    )(q, keys, v_t)


def _attn_bwd(q, keys, keys_t, kv, o, do, lse, *, name):
    s = q.shape[0]
    heads = q.shape[1] // HEAD_PAD
    t = _attn_tile(s)
    nq = s // t

    def body(q_ref, k_ref, kt_ref, v_ref, o_ref, do_ref, lse_ref, dq_ref, dk_ref, dv_ref, dk_acc, dv_acc, delta_ref):
        ki = pl.program_id(1)

        @pl.when(ki == 0)
        def _():
            dq_ref[...] = jnp.zeros_like(dq_ref)
            ones = jnp.ones((8, V_DIM), BF16)

            def row_sums(qb, carry):
                rows = pl.ds(pl.multiple_of(qb * t, t), t)
                prod = do_ref[rows, :].astype(F32) * o_ref[rows, :].astype(F32)
                sums = sum(lax.dot_general(ones, part, _NT, preferred_element_type=F32) for part in _split3(prod))
                delta_ref[qb] = sums[0:1]
                return carry

            lax.fori_loop(0, nq, row_sums, 0)

        kv_, vv, kt = k_ref[...], v_ref[...], kt_ref[0, 0]
        dk_acc[...] = jnp.zeros_like(dk_acc)
        dv_acc[...] = jnp.zeros_like(dv_acc)

        def block(qb, diagonal):
            rows = pl.ds(pl.multiple_of(qb * t, t), t)
            qv, dov = q_ref[rows, :], do_ref[rows, :]
            sc_t = lax.dot_general(kv_, qv, _NT, preferred_element_type=F32) * SM_SCALE
            p_t = jnp.exp(sc_t - lse_ref[0, qb])
            if diagonal:
                p_t = jnp.where(_causal_mask(t), p_t, 0.0)
            dv_acc[...] += jnp.dot(p_t.astype(BF16), dov, preferred_element_type=F32)
            dp_t = lax.dot_general(vv, dov, _NT, preferred_element_type=F32)
            ds_t = (p_t * (dp_t - delta_ref[qb]) * SM_SCALE).astype(BF16)
            dk_acc[...] += jnp.dot(ds_t, qv, preferred_element_type=F32)
            dq_ref[0, qb] += jnp.dot(kt, ds_t, preferred_element_type=F32)

        def later(qb, carry):
            block(qb, False)
            return carry

        block(ki, True)
        lax.fori_loop(ki + 1, nq, later, 0)
        dk_ref[...] = dk_acc[...]
        dv_ref[...] = dv_acc[...]

    whole = pl.BlockSpec((s, V_DIM), lambda hd, ki: (0, hd))
    return pl.pallas_call(
        body,
        name=name,
        grid=(heads, nq),
        in_specs=[
            pl.BlockSpec((s, HEAD_PAD), lambda hd, ki: (0, hd)),
            pl.BlockSpec((t, HEAD_PAD), lambda hd, ki: (ki, hd)),
            pl.BlockSpec((1, 1, HEAD_PAD, t), lambda hd, ki: (hd, ki, 0, 0)),
            pl.BlockSpec((t, V_DIM), lambda hd, ki: (ki, 2 * hd + 1)),
            whole,
            whole,
            pl.BlockSpec((1, nq, 1, t), lambda hd, ki: (hd, 0, 0, 0)),
        ],
        out_specs=[
            pl.BlockSpec((1, nq, HEAD_PAD, t), lambda hd, ki: (hd, 0, 0, 0)),
            pl.BlockSpec((t, HEAD_PAD), lambda hd, ki: (ki, hd)),
            pl.BlockSpec((t, V_DIM), lambda hd, ki: (ki, hd)),
        ],
        out_shape=[
            jax.ShapeDtypeStruct((heads, nq, HEAD_PAD, t), F32),
            jax.ShapeDtypeStruct((s, heads * HEAD_PAD), F32),
            jax.ShapeDtypeStruct((s, heads * V_DIM), F32),
        ],
        scratch_shapes=[pltpu.VMEM((t, HEAD_PAD), F32), pltpu.VMEM((t, V_DIM), F32), pltpu.VMEM((nq, 1, t), F32)],
        compiler_params=_params("parallel", "arbitrary"),
    )(q, keys, keys_t, kv, o, do, lse)


def _loss_bwd(x, y, gate, g, target, *, name):
    s, d = x.shape
    tr = _row_tile(s, d)

    def body(x_ref, y_ref, gate_ref, g_ref, t_ref, dx_ref, stats_ref, loss_ref):
        xv = x_ref[...] + gate_ref[...] * y_ref[...]
        r = lax.rsqrt(jnp.mean(xv * xv, axis=-1, keepdims=True) + NORM_EPS)
        xhat = xv * r
        gv = g_ref[...]
        err = xhat * gv - t_ref[...]
        dy = err / d
        dxhat = dy * gv
        dx_ref[...] = r * (dxhat - xhat * jnp.mean(dxhat * xhat, axis=-1, keepdims=True))

        @pl.when(pl.program_id(0) == 0)
        def _():
            stats_ref[...] = jnp.zeros_like(stats_ref)
            loss_ref[...] = jnp.zeros_like(loss_ref)

        stats_ref[0:1, :] += jnp.sum(dy * xhat, axis=0, keepdims=True)
        loss_ref[...] += 0.5 * jnp.sum(jnp.mean(err * err, axis=-1, keepdims=True))

    row = pl.BlockSpec((tr, d), lambda i: (i, 0))
    vec = pl.BlockSpec((1, d), lambda i: (0, 0))
    return pl.pallas_call(
        body,
        name=name,
        grid=(s // tr,),
        in_specs=[row, row, vec, vec, row],
        out_specs=[row, pl.BlockSpec((8, d), lambda i: (0, 0)), pl.BlockSpec((8, LANES), lambda i: (0, 0))],
        out_shape=[jax.ShapeDtypeStruct((s, d), F32), jax.ShapeDtypeStruct((8, d), F32), jax.ShapeDtypeStruct((8, LANES), F32)],
        compiler_params=_params("arbitrary"),
    )(x, y, gate, g, target)


def _adam_math(w, g, m, v):
    new_m = ADAM_B1 * m + (1.0 - ADAM_B1) * g
    new_v = ADAM_B2 * v + (1.0 - ADAM_B2) * (g * g)
    m_hat = new_m / (1.0 - ADAM_B1**ADAM_STEP)
    v_hat = new_v / (1.0 - ADAM_B2**ADAM_STEP)
    return -ADAM_LR * (m_hat / (jnp.sqrt(v_hat) + ADAM_EPS) + ADAM_WD * w), new_m, new_v


def _adamw(w, g, m, v, *, name):
    rows, cols = w.shape
    tr = _tile(rows, max(8, (1 << 18) // cols), 8)

    def body(w_ref, g_ref, m_ref, v_ref, go_ref, d_ref, mo_ref, vo_ref):
        gv = g_ref[...]
        go_ref[...] = gv
        d_ref[...], mo_ref[...], vo_ref[...] = _adam_math(w_ref[...], gv, m_ref[...], v_ref[...])

    spec = pl.BlockSpec((tr, cols), lambda i: (i, 0))
    return pl.pallas_call(
        body,
        name=name,
        grid=(rows // tr,),
        in_specs=[spec] * 4,
        out_specs=[spec] * 4,
        out_shape=[jax.ShapeDtypeStruct((rows, cols), F32)] * 4,
        compiler_params=_params("parallel"),
    )(w, g, m, v)


def _tp_adamw(sc16, dm, w, m, v, *, name):
    nl, d, n = w.shape
    tm = _tile(d, 512, LANES)
    tn = _tile(n, 1024, LANES)

    def body(sc_ref, dm_ref, w_ref, m_ref, v_ref, go_ref, d_ref, mo_ref, vo_ref):
        gv = lax.dot_general(sc_ref[...].astype(BF16), dm_ref[0].astype(BF16), (((0,), (0,)), ((), ())), preferred_element_type=F32)
        go_ref[0] = gv
        d_ref[0], mo_ref[0], vo_ref[0] = _adam_math(w_ref[0], gv, m_ref[0], v_ref[0])

    blk = pl.BlockSpec((1, tm, tn), lambda l, i, j: (l, i, j))
    return pl.pallas_call(
        body,
        name=name,
        grid=(nl, d // tm, n // tn),
        in_specs=[pl.BlockSpec((16, tm), lambda l, i, j: (0, i)), pl.BlockSpec((1, 16, tn), lambda l, i, j: (l, 0, j)), blk, blk, blk],
        out_specs=[blk] * 4,
        out_shape=[jax.ShapeDtypeStruct((nl, d, n), F32)] * 4,
        compiler_params=_params("parallel", "parallel", "parallel"),
    )(sc16, dm, w, m, v)


def _sum_devices(x, *, name):
    def body(x_ref, o_ref):
        acc = x_ref[0]
        for k in range(1, N_DEV):
            acc = acc + x_ref[k]
        o_ref[...] = acc

    return pl.pallas_call(body, name=name, out_shape=jax.ShapeDtypeStruct(x.shape[1:], F32))(x)


def _place():
    mx, my, mc = lax.axis_index("x"), lax.axis_index("y"), lax.axis_index("c")
    chips = [(1 - mx, my), (mx, 1 - my), (1 - mx, 1 - my)]
    return mx, my, mc, chips


def _remote(src, dst, send_sem, recv_sem, device):
    return pltpu.make_async_remote_copy(src_ref=src, dst_ref=dst, send_sem=send_sem, recv_sem=recv_sem, device_id=device, device_id_type=MESH)


def _allgather8(x, *, name):
    def body(x_ref, out_ref, send_sems, recv_sems, local_sem):
        mx, my, mc, chips = _place()
        me, sibling = (mx, my, mc), (mx, my, 1 - mc)

        def slot(px, py, pc):
            return out_ref.at[4 * px + 2 * py + pc]

        def copy(k, block, to, src=None):
            return _remote(slot(*block) if src is None else src, slot(*block), send_sems.at[k], recv_sems.at[k], to)

        mine = pltpu.make_async_copy(x_ref, slot(*me), local_sem)
        mine.start()
        first = [copy(0, me, sibling, src=x_ref)] + [copy(1 + j, me, (*chip, mc), src=x_ref) for j, chip in enumerate(chips)]
        for cp in first:
            cp.start()
        passed = [copy(4 + j, (*chip, mc), sibling) for j, chip in enumerate(chips)]
        for j, chip in enumerate(chips):
            copy(1 + j, (*chip, mc), me).wait_recv()
            passed[j].start()
        copy(0, sibling, me).wait_recv()
        for j, chip in enumerate(chips):
            copy(4 + j, (*chip, 1 - mc), me).wait_recv()
        for cp in first + passed:
            cp.wait_send()
        mine.wait()

    return pl.pallas_call(
        body,
        name=name,
        out_shape=jax.ShapeDtypeStruct((N_DEV,) + x.shape, x.dtype),
        in_specs=[pl.BlockSpec(memory_space=pltpu.VMEM)],
        out_specs=pl.BlockSpec(memory_space=pltpu.VMEM),
        scratch_shapes=[pltpu.SemaphoreType.DMA((7,)), pltpu.SemaphoreType.DMA((7,)), pltpu.SemaphoreType.DMA],
    )(x)


class _Geom:
    def __init__(self, shape3, axis):
        self.shape3, self.axis = shape3, axis
        nl, r, c = shape3
        self.rs, self.cs = (r // N_CHIPS, c) if axis == 1 else (r, c // N_CHIPS)
        self.hl, self.hr = (nl // 2, self.rs) if nl > 1 else (1, self.rs // 2)
        self.shard = (nl, self.rs, self.cs)
        self.half = (self.hl, self.hr, self.cs)

    def in_full(self, ref, chip, core):
        nl = self.shape3[0]
        l0 = core * self.hl if nl > 1 else 0
        r0 = (chip * self.rs if self.axis == 1 else 0) + (0 if nl > 1 else core * self.hr)
        c0 = chip * self.cs if self.axis == 2 else 0
        return ref.at[pl.ds(l0, self.hl), pl.ds(r0, self.hr), pl.ds(c0, self.cs)]


def _place_shard(shard, geom, chip_arr, *, name, layer=None):
    nl, rs, cs = geom.shard
    tr = _tile(rs, max(16, (1 << 18) // cs), 16)
    per = rs // tr
    first = 0 if layer is None else layer

    def body(chip_ref, x_ref, o_ref):
        o_ref[...] = x_ref[...].astype(o_ref.dtype)

    def out_map(l, i, chip_ref):
        return (l, chip_ref[0] * per + i, 0) if geom.axis == 1 else (l, i, chip_ref[0])

    return pl.pallas_call(
        body,
        name=name,
        grid_spec=pltpu.PrefetchScalarGridSpec(
            num_scalar_prefetch=1,
            grid=(nl, per),
            in_specs=[pl.BlockSpec((1, tr, cs), lambda l, i, chip_ref: (first + l, i, 0))],
            out_specs=pl.BlockSpec((1, tr, cs), out_map),
        ),
        out_shape=jax.ShapeDtypeStruct(geom.shape3, BF16),
        compiler_params=_params("parallel", "parallel"),
    )(chip_arr, shard)


def _dma_sems(count, arrays):
    return [pltpu.SemaphoreType.DMA((count,))] * arrays


def _gather_plan(fulls, geoms):
    def ici(w, k, src, dst, sems, device):
        return _remote(src, dst, sems[0].at[3 * w + k], sems[1].at[3 * w + k], device)

    def d2d(w, k, box, sems, device):
        return _remote(box, box, sems[2].at[3 * w + k], sems[3].at[3 * w + k], device)

    def start(given, full, sems):
        mx, my, mc, chips = _place()
        me = 2 * mx + my
        for w, geom in enumerate(geoms):
            for k, chip in enumerate(chips):
                ici(w, k, geom.in_full(given[w], me, mc), geom.in_full(full[w], me, mc), sems, (*chip, mc)).start()

    def finish(given, full, sems):
        mx, my, mc, chips = _place()
        me, sibling = 2 * mx + my, (mx, my, 1 - mc)
        for w, geom in enumerate(geoms):
            for k, (px, py) in enumerate(chips):
                landed = geom.in_full(full[w], 2 * px + py, mc)
                ici(w, k, landed, landed, sems, (px, py, mc)).wait_recv()
                d2d(w, k, landed, sems, sibling).start()
        for w, geom in enumerate(geoms):
            for k, (px, py) in enumerate(chips):
                d2d(w, k, geom.in_full(full[w], 2 * px + py, 1 - mc), sems, sibling).wait_recv()
        for w, geom in enumerate(geoms):
            for k, (px, py) in enumerate(chips):
                ici(w, k, geom.in_full(given[w], me, mc), geom.in_full(full[w], me, mc), sems, (px, py, mc)).wait_send()
                d2d(w, k, geom.in_full(full[w], 2 * px + py, mc), sems, sibling).wait_send()

    n = len(fulls)
    shapes = [jax.ShapeDtypeStruct(f.shape, f.dtype) for f in fulls]
    return _Hosted(fulls, shapes, {w: w for w in range(n)}, _dma_sems(3 * n, 4), start, finish)


def _pair_plan(grads, geoms):
    def copies(grad, theirs, sems):
        mx, my, mc, _ = _place()
        return [
            _remote(geom.in_full(grad[w], chip, 1 - mc), theirs[w].at[chip], sems[0].at[4 * w + chip], sems[1].at[4 * w + chip], (mx, my, 1 - mc))
            for w, geom in enumerate(geoms)
            for chip in range(N_CHIPS)
        ]

    def start(grad, theirs, sems):
        for cp in copies(grad, theirs, sems):
            cp.start()

    def finish(grad, theirs, sems):
        for cp in copies(grad, theirs, sems):
            cp.wait_recv()
        for cp in copies(grad, theirs, sems):
            cp.wait_send()

    shapes = [jax.ShapeDtypeStruct((N_CHIPS,) + g.half, x.dtype) for g, x in zip(geoms, grads)]
    return _Hosted(grads, shapes, {}, _dma_sems(4 * len(grads), 2), start, finish)


def _half_tile(geom):
    return _tile(geom.hr, max(16, (1 << 18) // geom.cs), 16)


def _pair_add(grad, theirs, geom, core_arr, *, name):
    hl, hr, cs = geom.half
    tr = _half_tile(geom)
    stacked = geom.shape3[0] > 1

    def grad_map(chip, l, i, core_ref):
        layer = core_ref[0] * hl + l if stacked else 0
        row = (chip * (geom.rs // tr) if geom.axis == 1 else 0) + (0 if stacked else core_ref[0] * (hr // tr)) + i
        return layer, row, (chip if geom.axis == 2 else 0)

    def body(core_ref, g_ref, t_ref, o_ref):
        o_ref[0] = (g_ref[...].astype(F32) + t_ref[0].astype(F32)).astype(o_ref.dtype)

    blk = pl.BlockSpec((1, 1, tr, cs), lambda chip, l, i, core_ref: (chip, l, i, 0))
    return pl.pallas_call(
        body,
        name=name,
        grid_spec=pltpu.PrefetchScalarGridSpec(
            num_scalar_prefetch=1, grid=(N_CHIPS, hl, hr // tr), in_specs=[pl.BlockSpec((1, tr, cs), grad_map), blk], out_specs=blk
        ),
        out_shape=jax.ShapeDtypeStruct(theirs.shape, BF16),
        compiler_params=_params("parallel", "parallel", "parallel"),
    )(core_arr, grad, theirs)


def _chips_plan(parts):
    def copies(part, slots, sems):
        _, _, mc, chips = _place()
        return [
            _remote(part[w].at[2 * px + py], slots[w].at[k], sems[0].at[3 * w + k], sems[1].at[3 * w + k], (px, py, mc))
            for w in range(len(parts))
            for k, (px, py) in enumerate(chips)
        ]

    def start(part, slots, sems):
        for cp in copies(part, slots, sems):
            cp.start()

    def finish(part, slots, sems):
        for cp in copies(part, slots, sems):
            cp.wait_recv()
        for cp in copies(part, slots, sems):
            cp.wait_send()

    shapes = [jax.ShapeDtypeStruct((N_CHIPS - 1,) + p.shape[1:], p.dtype) for p in parts]
    return _Hosted(parts, shapes, {}, _dma_sems(3 * len(parts), 2), start, finish)


def _chip_sum(part, slots, geom, place_arr, *, name, stack=None):
    hl, hr, cs = geom.half
    tr = _half_tile(geom)

    def body(place_ref, own_ref, s0_ref, s1_ref, s2_ref, *rest):
        o_ref = rest[-1]
        o_ref[...] = ((own_ref[...].astype(F32) + s0_ref[...].astype(F32)) + s1_ref[...].astype(F32)) + s2_ref[...].astype(F32)

    def slot(k):
        return pl.BlockSpec((1, 1, tr, cs), lambda l, i, place_ref: (k, l, i, 0))

    in_specs = [pl.BlockSpec((1, 1, tr, cs), lambda l, i, place_ref: (place_ref[0], l, i, 0)), slot(0), slot(1), slot(2)]
    args = [place_arr, part, slots, slots, slots]
    aliases = {}
    if stack is None:
        out_spec = pl.BlockSpec((1, 1, tr, cs), lambda l, i, place_ref: (place_ref[1], l, i, 0))
        out_shape = jax.ShapeDtypeStruct((2,) + geom.half, F32)
    else:
        layers, layer, prev = stack
        assert hl == 1
        out_spec = pl.BlockSpec((1, 1, tr, cs), lambda l, i, place_ref: (layer, place_ref[1], i, 0))
        out_shape = jax.ShapeDtypeStruct((layers, 2, hr, cs), F32)
        if prev is not None:
            aliases = {len(args): 0}
            in_specs.append(ANY)
            args.append(prev)
    return pl.pallas_call(
        body,
        name=name,
        grid_spec=pltpu.PrefetchScalarGridSpec(num_scalar_prefetch=1, grid=(hl, hr // tr), in_specs=in_specs, out_specs=out_spec),
        out_shape=out_shape,
        input_output_aliases=aliases,
        compiler_params=_params("parallel", "parallel"),
    )(*args)


def _join_plan(boths, prefixes):
    def copies(given, both, sems):
        mx, my, mc, _ = _place()
        out, n = [], 0
        for w in range(len(boths)):
            for p in prefixes[w]:
                out.append(_remote(given[w].at[(*p, mc)], both[w].at[(*p, mc)], sems[0].at[n], sems[1].at[n], (mx, my, 1 - mc)))
                n += 1
        return out

    def arrivals(both, sems):
        mx, my, mc, _ = _place()
        out, n = [], 0
        for w in range(len(boths)):
            for p in prefixes[w]:
                got = both[w].at[(*p, 1 - mc)]
                out.append(_remote(got, got, sems[0].at[n], sems[1].at[n], (mx, my, 1 - mc)))
                n += 1
        return out

    def start(given, both, sems):
        for cp in copies(given, both, sems):
            cp.start()

    def finish(given, both, sems):
        for cp in arrivals(both, sems):
            cp.wait_recv()
        for cp in copies(given, both, sems):
            cp.wait_send()

    count = sum(len(p) for p in prefixes)
    shapes = [jax.ShapeDtypeStruct(b.shape, b.dtype) for b in boths]
    return _Hosted(boths, shapes, {w: w for w in range(len(boths))}, _dma_sems(count, 2), start, finish)


WEIGHTS = ("mod_w", "mod_b", "norm_mix", "norm_ffn", "pool_w", "pool_scale", "kv_mod_w", "kv_mod_b", "kv_in_norm", "w_dkv", "kv_norm",
           "w_uk", "w_uv", "w_kr", "w_dq", "q_norm", "w_uq", "w_o", "ffn_gate", "ffn_up", "ffn_down", "final_norm")
SMALL = ("mod_b", "kv_mod_b", "norm_mix", "norm_ffn", "kv_in_norm", "kv_norm", "q_norm", "final_norm")


def _rows(v):
    return v.reshape(-1, LANES)


def _pad_rows(a):
    return jnp.pad(a, ((0, (-a.shape[0]) % 8), (0, 0)))


def _vec(v):
    return v.reshape(1, -1)


def _step(x, c, positions, target, wts, mom, var):
    _, s, d = x.shape
    depth, n_a, n_b = wts["mod_w"].shape[0], wts["pool_w"].shape[0], wts["w_dq"].shape[0]
    assert n_b == 2 and n_a + n_b == depth
    heads = d // V_DIM
    kvr, qr = wts["w_dkv"].shape[1], wts["w_dq"].shape[2]
    ffn = wts["ffn_gate"].shape[2] * N_CHIPS
    pool_c = d // len(POOL_WINDOWS)
    nmod, nkv = N_MOD * d, 2 * d
    mx, my, mc = lax.axis_index("x"), lax.axis_index("y"), lax.axis_index("c")
    chip, dev = 2 * mx + my, 4 * mx + 2 * my + mc
    xs, tgt = x[0], target[0]

    inv_freq = 1.0 / (ROPE_THETA ** (jnp.arange(0, ROPE_DIM, 2, dtype=F32) / ROPE_DIM))
    ang = positions[0].astype(F32)[:, None] * inv_freq
    cos, sin, zero = jnp.cos(ang), jnp.sin(ang), jnp.zeros((s, LANES - ROPE_DIM), F32)
    cos_t = jnp.concatenate([cos, cos, zero], axis=1)
    sin_fwd = jnp.concatenate([-sin, sin, zero], axis=1)
    sin_bwd = jnp.concatenate([sin, -sin, zero], axis=1)

    c_rows, ps_rows = d // LANES, n_a * (d // N_CHIPS) // LANES
    cond = _allgather8(_pad_rows(jnp.concatenate([_rows(c), _rows(wts["pool_scale"])])), name="gather_cond")
    c_all = cond[:, :c_rows].reshape(N_DEV, d)
    pool_scale = cond[0::2, c_rows : c_rows + ps_rows].reshape(N_CHIPS, n_a, d // N_CHIPS).transpose(1, 0, 2).reshape(n_a, d)
    sc16 = _elementwise(_silu, [jnp.pad(c_all, ((0, 16 - N_DEV), (0, 0)))], [F32], name="silu_cond")[0]

    mod_bias = lax.dynamic_slice_in_dim(wts["mod_b"], chip * (nmod // N_CHIPS), nmod // N_CHIPS, axis=1)[:, None, :]
    kv_bias = lax.dynamic_slice_in_dim(wts["kv_mod_b"], chip * (nkv // N_CHIPS), nkv // N_CHIPS).reshape(1, 1, -1)
    mod_part = _tp_fwd(sc16, wts["mod_w"], mod_bias, name="mod_fwd")
    kv_part = _tp_fwd(sc16, wts["kv_mod_w"][None], kv_bias, name="kv_mod_fwd")
    part = jnp.concatenate([mod_part[i, :N_DEV] for i in range(depth)] + [kv_part[0, :N_DEV]], axis=1)
    ncol = part.shape[1]
    gathered = _allgather8(_pad_rows(_rows(part)), name="gather_mods")
    gathered = gathered[0::2, : N_DEV * ncol // LANES].reshape(N_CHIPS, N_DEV, ncol)
    mine = lax.dynamic_index_in_dim(gathered, dev, axis=1, keepdims=False)
    per = nmod // N_CHIPS
    mods = [mine[:, i * per : (i + 1) * per].reshape(N_MOD, 1, d) for i in range(depth)]
    kv_shift, kv_scale = mine[:, depth * per :].reshape(2, 1, d)

    mixer_names = ("pool_w", "w_dkv", "w_uk", "w_uv", "w_kr", "w_dq", "w_uq", "w_o")
    ffn_names = ("ffn_gate", "ffn_up", "ffn_down")
    geoms = {
        "pool_w": _Geom((n_a * len(POOL_WINDOWS), pool_c, pool_c), 1),
        "w_dkv": _Geom((1, d, kvr), 1),
        "w_uk": _Geom((1, kvr, heads * NOPE_DIM), 2),
        "w_uv": _Geom((1, kvr, heads * V_DIM), 2),
        "w_kr": _Geom((1, d, ROPE_DIM), 1),
        "w_dq": _Geom((n_b, d, qr), 1),
        "w_uq": _Geom((n_b, qr, heads * (NOPE_DIM + ROPE_DIM)), 2),
        "w_o": _Geom((n_b, d, d), 1),
        "ffn_gate": _Geom((1, d, ffn), 2),
        "ffn_up": _Geom((1, d, ffn), 2),
        "ffn_down": _Geom((1, ffn, d), 1),
    }
    mixer_geoms = [geoms[n] for n in mixer_names]
    ffn_geoms = [geoms[n] for n in ffn_names]
    chip_arr, core_arr, place_arr = chip.reshape(1), mc.reshape(1), jnp.stack([chip, mc])
    placed = [_place_shard(wts[n].reshape(geoms[n].shard), geoms[n], chip_arr, name="place_" + n) for n in mixer_names]
    placed_ffn = [[_place_shard(wts[n], geoms[n], chip_arr, layer=i, name="place_" + n) for n in ffn_names] for i in range(depth)]
    first = _run(_gather_plan(placed + placed_ffn[0], mixer_geoms + ffn_geoms), name="gather_first")
    full = dict(zip(mixer_names, first))
    whole_k = dict(tm=512, tn=1024, tk=max(s, ffn))
    ffn_w = [None] * depth
    ffn_w[0] = first[len(mixer_names) :]

    pool_w = full["pool_w"].reshape(n_a, len(POOL_WINDOWS), pool_c, pool_c)
    w_uq = full["w_uq"].reshape(n_b, qr, heads, NOPE_DIM + ROPE_DIM)
    w_q = jnp.pad(w_uq, ((0, 0), (0, 0), (0, 0), (0, HEAD_PAD - NOPE_DIM - ROPE_DIM))).reshape(n_b, qr, heads * HEAD_PAD)
    w_ukv = jnp.stack([full["w_uk"].reshape(kvr, heads, NOPE_DIM), full["w_uv"].reshape(kvr, heads, V_DIM)], axis=2).reshape(kvr, heads * HEAD_PAD)
    w_dkvkr = jnp.concatenate([full["w_dkv"][0], full["w_kr"][0], jnp.zeros((d, LANES - ROPE_DIM), BF16)], axis=1)
    w_dq, w_o = full["w_dq"], full["w_o"]

    norm_mix, norm_ffn = wts["norm_mix"], wts["norm_ffn"]
    saved = []
    cur, pending = xs, None
    kv_side = None
    for i in range(depth):
        shift_m, scale_m, gate_m, shift_f, scale_f, gate_f = mods[i]
        h1_dtype = F32 if i < n_a else BF16
        if pending is None:
            x0 = cur
            h1 = _norm_fwd(x0, _vec(norm_mix[i]), scale=scale_m, shift=shift_m, out_dtype=h1_dtype, name="norm_mix_first")
        else:
            x0, h1 = _norm_fwd(cur, _vec(norm_mix[i]), scale=scale_m, shift=shift_m, y=pending[0], gate=pending[1], out_dtype=h1_dtype, name="norm_mix")
        lay = {"x0": x0, "h1": h1}
        if i == n_a:
            h_kv = _norm_fwd(x0, _vec(wts["kv_in_norm"]), scale=kv_scale, shift=kv_shift, name="norm_kv_in")
            pre = _mm(h_kv, w_dkvkr, name="kv_down", tn=kvr + LANES)
            ckv_pre, kr_pre = pre[:, :kvr], pre[:, kvr:]
            ckv = _norm_fwd(ckv_pre, _vec(wts["kv_norm"]), name="norm_kv")
            kv = _mm(ckv, w_ukv, out_dtype=BF16, name="kv_up")
            keys, keys_t, v_t = _build_keys(kv, kr_pre, cos_t, sin_fwd, name="build_keys")
            kv_side = {"h_kv": h_kv, "ckv_pre": ckv_pre, "ckv": ckv, "kv": kv, "keys": keys, "keys_t": keys_t, "v_t": v_t, "x0": x0}
        if i < n_a:
            pooled = _pool(h1, transpose=False, out_dtype=BF16, name="pool_fwd")
            y_pre = _gmm(pooled, pool_w[i], mode="nn", out_dtype=F32, name="pool_mix")
            gate_eff = gate_m * _vec(pool_scale[i])
            lay.update(pooled=pooled)
        else:
            l = i - n_a
            cq_pre = _mm(h1, w_dq[l], name="q_down")
            cq = _norm_fwd(cq_pre, _vec(wts["q_norm"][l]), name="norm_q")
            q = _rope_heads(_mm(cq, w_q[l], name="q_up"), cos_t, sin_fwd, out_dtype=BF16, name="rope_q")
            o, lse = _attn_fwd(q, kv_side["keys"], kv_side["v_t"], name="attn_fwd")
            y_pre = _mm(o, w_o[l], name="attn_out")
            gate_eff = gate_m
            lay.update(cq_pre=cq_pre, cq=cq, q=q, o=o, lse=lse)
        x1, h2 = _norm_fwd(x0, _vec(norm_ffn[i]), scale=scale_f, shift=shift_f, y=y_pre, gate=gate_eff, name="norm_ffn")
        w_gate, w_up, w_down = ffn_w[i]
        if i + 1 < depth:
            (a, b, z), next_in = _ffn_in(h2, w_gate, w_up, 0, hosted=_gather_plan(placed_ffn[i + 1][:2], ffn_geoms[:2]), name="ffn_in")
            f, next_down = _mm(z, w_down, b_idx=0, hosted=_gather_plan(placed_ffn[i + 1][2:], ffn_geoms[2:]), name="ffn_down", **whole_k)
            ffn_w[i + 1] = next_in + next_down
        else:
            (a, b, z), _ = _ffn_in(h2, w_gate, w_up, 0, name="ffn_in_last")
            f = _mm(z, w_down, b_idx=0, name="ffn_down_last", **whole_k)
        lay.update(y_pre=y_pre, gate_eff=gate_eff, x1=x1, h2=h2, a=a, b=b, z=z, f=f)
        saved.append(lay)
        cur, pending = x1, (f, gate_f)

    dx, final_stats, loss_tile = _loss_bwd(cur, pending[0], pending[1], _vec(wts["final_norm"]), tgt, name="loss")
    loss = lax.psum(loss_tile[0, 0], ("x", "y", "c"))

    ffn_both = [None] * len(ffn_names)
    in_flight = None
    attn_names = tuple(n for n in mixer_names if n != "pool_w")
    grad_full = {}
    attn_parts = attn_slots = None

    def sum_chips(layer, parts, slots):
        for w, n in enumerate(ffn_names):
            ffn_both[w] = _chip_sum(parts[w], slots[w], ffn_geoms[w], place_arr, stack=(depth, layer, ffn_both[w]), name="chip_sum_" + n)

    g_wo, g_wq, g_wdq = [None] * n_b, [None] * n_b, [None] * n_b
    g_pool = [None] * n_a
    dmods = [None] * depth
    g_norm_mix, g_norm_ffn, g_q_norm, g_pool_scale = [None] * depth, [None] * depth, [None] * n_b, [None] * n_a
    dk_layers, dv_layers = [None] * n_b, [None] * n_b
    for i in reversed(range(depth)):
        lay = saved[i]
        shift_m, scale_m, gate_m, shift_f, scale_f, gate_f = mods[i]
        df, sums_gf = _gate_bwd(dx, lay["f"], gate_f, name="gate_bwd")
        w_gate, w_up, w_down = ffn_w[i]
        g_down = _mm(lay["z"], df, ta=True, out_dtype=BF16, name="ffn_down_dw", **whole_k)
        pair_down = _pair_plan([g_down[None]], ffn_geoms[2:])
        if in_flight is None:
            (da, db), _ = _ffn_down_bwd(df, w_down, 0, lay["a"], lay["b"], name="ffn_down_bwd_top")
            (dh2,), their_down = _ffn_in_dx(da, db, w_gate, w_up, 0, hosted=pair_down, name="ffn_in_dx_top")
            (g_gate, g_up), _ = _ffn_in_dw(lay["h2"], da, db, name="ffn_in_dw_top")
        else:
            carry_attn = attn_parts is not None and attn_slots is None
            plans = [_chips_plan(in_flight[:1]), pair_down] + ([_chips_plan(attn_parts)] if carry_attn else [])
            (da, db), got_down = _ffn_down_bwd(df, w_down, 0, lay["a"], lay["b"], hosted=_chips_plan(in_flight[2:]), name="ffn_down_bwd")
            (dh2,), got = _ffn_in_dx(da, db, w_gate, w_up, 0, hosted=_merge(plans), name="ffn_in_dx_attn" if carry_attn else "ffn_in_dx")
            got_gate, their_down = got[:1], got[1:2]
            if carry_attn:
                attn_slots = got[2:]
            (g_gate, g_up), got_up = _ffn_in_dw(lay["h2"], da, db, hosted=_chips_plan(in_flight[1:2]), name="ffn_in_dw")
            sum_chips(i + 1, in_flight, got_gate + got_up + got_down)
        (dx1, sums_f), their_in = _norm_bwd(
            lay["x1"], _vec(norm_ffn[i]), dh2, scale=scale_f, resid=dx, hosted=_pair_plan([g_gate[None], g_up[None]], ffn_geoms[:2]), name="norm_ffn_bwd"
        )
        ffn_grads = [g_gate[None], g_up[None], g_down[None]]
        in_flight = [
            _pair_add(g, th, geom, core_arr, name="pair_add_" + n) for g, th, geom, n in zip(ffn_grads, their_in + their_down, ffn_geoms, ffn_names)
        ]
        dyp, sums_gm = _gate_bwd(dx1, lay["y_pre"], lay["gate_eff"], name="gate_bwd")
        if i < n_a:
            g_pool[i] = _gmm(lay["pooled"], dyp, mode="tn", out_dtype=BF16, name="pool_mix_dw")
            dd = _gmm(dyp, pool_w[i], mode="nt", out_dtype=F32, name="pool_mix_dx")
            dh1 = _pool(dd, transpose=True, out_dtype=F32, name="pool_bwd")
            dgate_m = sums_gm[0] * pool_scale[i]
            g_pool_scale[i] = sums_gm[0] * gate_m[0]
        else:
            l = i - n_a
            do = _mm(dyp, w_o[l], tb=True, out_dtype=BF16, name="attn_out_dx")
            g_wo[l] = _mm(lay["o"], dyp, ta=True, out_dtype=BF16, name="attn_out_dw")
            dq_t, dk_layers[l], dv_layers[l] = _attn_bwd(
                lay["q"], kv_side["keys"], kv_side["keys_t"], kv_side["kv"], lay["o"], do, lay["lse"], name="attn_bwd"
            )
            dq_pre = _rope_back(dq_t, cos_t, sin_bwd, name="rope_q_bwd")
            dcq = _mm(dq_pre, w_q[l], tb=True, name="q_up_dx")
            g_wq[l] = _mm(lay["cq"], dq_pre, ta=True, out_dtype=BF16, name="q_up_dw")
            dcq_pre, sums_q = _norm_bwd(lay["cq_pre"], _vec(wts["q_norm"][l]), dcq, name="norm_q_bwd")
            g_q_norm[l] = sums_q[2]
            dh1 = _mm(dcq_pre, w_dq[l], tb=True, name="q_down_dx")
            g_wdq[l] = _mm(lay["h1"], dcq_pre, ta=True, out_dtype=BF16, name="q_down_dw")
            dgate_m = sums_gm[0]
        dx, sums_m = _norm_bwd(lay["x0"], _vec(norm_mix[i]), dh1, scale=scale_m, resid=dx1, name="norm_mix_bwd")
        if i == n_a:
            dkv, dkr_pre = _keys_bwd(dk_layers[0], dk_layers[1], dv_layers[0], dv_layers[1], cos_t, sin_bwd, name="keys_bwd")
            dckv = _mm(dkv, w_ukv, tb=True, name="kv_up_dx")
            g_ukv = _mm(kv_side["ckv"], dkv, ta=True, out_dtype=BF16, name="kv_up_dw")
            dckv_pre, sums_kvn = _norm_bwd(kv_side["ckv_pre"], _vec(wts["kv_norm"]), dckv, name="norm_kv_bwd")
            dpre = jnp.concatenate([dckv_pre, dkr_pre], axis=1)
            dh_kv = _mm(dpre, w_dkvkr, tb=True, name="kv_down_dx")
            g_dkvkr = _mm(kv_side["h_kv"], dpre, ta=True, out_dtype=BF16, name="kv_down_dw", tn=kvr + LANES)
            dx, sums_kv = _norm_bwd(lay["x0"], _vec(wts["kv_in_norm"]), dh_kv, scale=kv_scale, resid=dx, name="norm_kv_in_bwd")
            g_ukv = g_ukv.reshape(kvr, heads, 2, NOPE_DIM)
            grad_full.update(
                w_dkv=g_dkvkr[None, :, :kvr],
                w_uk=g_ukv[:, :, 0].reshape(1, kvr, heads * NOPE_DIM),
                w_uv=g_ukv[:, :, 1].reshape(1, kvr, heads * V_DIM),
                w_kr=g_dkvkr[None, :, kvr : kvr + ROPE_DIM],
                w_dq=jnp.stack(g_wdq),
                w_uq=jnp.stack(g_wq).reshape(n_b, qr, heads, HEAD_PAD)[..., : NOPE_DIM + ROPE_DIM].reshape(geoms["w_uq"].shape3),
                w_o=jnp.stack(g_wo),
            )
            attn_theirs = _run(_pair_plan([grad_full[n] for n in attn_names], [geoms[n] for n in attn_names]), name="reduce_pair_attn")
            attn_parts = [_pair_add(grad_full[n], th, geoms[n], core_arr, name="pair_add_" + n) for n, th in zip(attn_names, attn_theirs)]
        dmods[i] = jnp.concatenate([sums_m[0], sums_m[1], dgate_m, sums_f[0], sums_f[1], sums_gf[0]])
        g_norm_mix[i], g_norm_ffn[i] = sums_m[2], sums_f[2]
    grad_x = dx[None]
    grad_full["pool_w"] = jnp.stack(g_pool).reshape(geoms["pool_w"].shape3)

    small_grads = {
        "mod_b": jnp.concatenate(dmods),
        "kv_mod_b": jnp.concatenate([sums_kv[0], sums_kv[1]]),
        "norm_mix": jnp.concatenate(g_norm_mix),
        "norm_ffn": jnp.concatenate(g_norm_ffn),
        "kv_in_norm": sums_kv[2],
        "kv_norm": sums_kvn[2],
        "q_norm": jnp.concatenate(g_q_norm),
        "final_norm": final_stats[0],
    }
    packed = jnp.concatenate([small_grads[n] for n in SMALL] + g_pool_scale)
    small_rows = sum(wts[n].size for n in SMALL) // LANES
    every = _allgather8(_pad_rows(_rows(packed)), name="gather_small_grads")
    summed = _sum_devices(every, name="sum_small_grads")

    mod_rows = depth * nmod // LANES
    dm_all = every[:, :mod_rows].reshape(N_DEV, depth, nmod)
    dm = lax.dynamic_slice_in_dim(dm_all, chip * per, per, axis=2).transpose(1, 0, 2)
    dm = jnp.pad(dm, ((0, 0), (0, 16 - N_DEV), (0, 0)))
    dkvm_all = every[:, mod_rows : mod_rows + nkv // LANES].reshape(N_DEV, nkv)
    dkvm = jnp.pad(lax.dynamic_slice_in_dim(dkvm_all, chip * (nkv // N_CHIPS), nkv // N_CHIPS, axis=1), ((0, 16 - N_DEV), (0, 0)))[None]
    results = {}
    results["mod_w"] = _tp_adamw(sc16, dm, wts["mod_w"], mom["mod_w"], var["mod_w"], name="mod_w_update")
    results["kv_mod_w"] = [
        r[0] for r in _tp_adamw(sc16, dkvm, wts["kv_mod_w"][None], mom["kv_mod_w"][None], var["kv_mod_w"][None], name="kv_mod_w_update")
    ]

    ps_grad = lax.dynamic_slice_in_dim(summed[small_rows : small_rows + n_a * d // LANES].reshape(n_a, d), chip * (d // N_CHIPS), d // N_CHIPS, axis=1)
    small_names = SMALL + ("pool_scale",)

    def pack_small(tree):
        return _pad_rows(jnp.concatenate([_rows(tree[n]) for n in small_names]))

    g_small = _pad_rows(jnp.concatenate([summed[:small_rows], _rows(ps_grad)]))
    small_out = _adamw(pack_small(wts), g_small, pack_small(mom), pack_small(var), name="small_update")
    row = 0
    for n in small_names:
        nrow = wts[n].size // LANES
        results[n] = [r[row : row + nrow].reshape(wts[n].shape) for r in small_out]
        row += nrow

    assert attn_slots is not None
    theirs = _run(_pair_plan([grad_full["pool_w"]], [geoms["pool_w"]]), name="reduce_pair")
    pool_part = _pair_add(grad_full["pool_w"], theirs[0], geoms["pool_w"], core_arr, name="pair_add_pool_w")
    got = _run(_chips_plan([pool_part] + in_flight), name="reduce_chips")
    parts = dict(zip(attn_names, attn_parts), pool_w=pool_part)
    slots = dict(zip(attn_names, attn_slots), pool_w=got[0])
    boths = [_chip_sum(parts[n], slots[n], geoms[n], place_arr, name="chip_sum_" + n) for n in mixer_names]
    sum_chips(0, in_flight, got[1:])
    prefixes = [[()]] * len(mixer_names) + [[(layer,) for layer in range(depth)]] * len(ffn_names)
    joined = _run(_join_plan(boths + ffn_both, prefixes), name="join_pair")
    for n, both in zip(mixer_names + ffn_names, joined):
        cs = geoms[n].cs
        out = _adamw(wts[n].reshape(-1, cs), both.reshape(-1, cs), mom[n].reshape(-1, cs), var[n].reshape(-1, cs), name="update_" + n)
        results[n] = [r.reshape(wts[n].shape) for r in out]

    outs = [loss, grad_x]
    for k in range(4):
        outs += [results[n][k] for n in WEIGHTS]
    return tuple(outs)


def kernel(x, c, positions, mod_w, mod_b, norm_mix, norm_ffn, pool_w, pool_scale, kv_mod_w, kv_mod_b, kv_in_norm, w_dkv, kv_norm, w_uk, w_uv, w_kr, w_dq, q_norm, w_uq, w_o, ffn_gate, ffn_up, ffn_down, final_norm, loss_target, m_mod_w, m_mod_b, m_norm_mix, m_norm_ffn, m_pool_w, m_pool_scale, m_kv_mod_w, m_kv_mod_b, m_kv_in_norm, m_w_dkv, m_kv_norm, m_w_uk, m_w_uv, m_w_kr, m_w_dq, m_q_norm, m_w_uq, m_w_o, m_ffn_gate, m_ffn_up, m_ffn_down, m_final_norm, v_mod_w, v_mod_b, v_norm_mix, v_norm_ffn, v_pool_w, v_pool_scale, v_kv_mod_w, v_kv_mod_b, v_kv_in_norm, v_w_dkv, v_kv_norm, v_w_uk, v_w_uv, v_w_kr, v_w_dq, v_q_norm, v_w_uq, v_w_o, v_ffn_gate, v_ffn_up, v_ffn_down, v_final_norm):
    given = dict(locals())
    wts = {n: given[n] for n in WEIGHTS}
    mom = {n: given["m_" + n] for n in WEIGHTS}
    var = {n: given["v_" + n] for n in WEIGHTS}
    return _step(x, c, positions, loss_target, wts, mom, var)
```

```python
import functools

import jax
import jax.numpy as jnp
from jax import lax
from jax.experimental import pallas as pl
from jax.experimental.pallas import tpu as pltpu

F32 = jnp.float32
BF16 = jnp.bfloat16
MESH = pl.DeviceIdType.MESH
ANY = pl.BlockSpec(memory_space=pl.ANY)

NORM_EPS = 1e-6
POOL_WINDOWS = (2, 4, 8, 16)
NOPE_DIM = 128
ROPE_DIM = 64
V_DIM = 128
HEAD_PAD = 256
SM_SCALE = (NOPE_DIM + ROPE_DIM) ** -0.5
ROPE_THETA = 10000.0
N_MOD = 6
ADAM_LR, ADAM_B1, ADAM_B2, ADAM_EPS, ADAM_WD, ADAM_STEP = 0.001, 0.9, 0.999, 1e-08, 0.01, 10
N_CHIPS = 4
N_DEV = 8
LANES = 128
HALO = 128
VMEM_LIMIT = 48 * 1024 * 1024


def _tile(dim, pref, align):
    if dim <= pref:
        return dim
    t = (pref // align) * align
    while t >= align:
        if dim % t == 0:
            return t
        t -= align
    return dim


def _params(*sem):
    return pltpu.CompilerParams(dimension_semantics=sem, vmem_limit_bytes=VMEM_LIMIT)


class _Hosted:
    def __init__(self, args, out_shapes, aliases, sem_shapes, start, finish):
        self.args, self.out_shapes, self.aliases, self.sem_shapes = list(args), list(out_shapes), dict(aliases), list(sem_shapes)
        self.start, self.finish = start, finish


def _call(body, *, name, grid, in_specs, out_specs, out_shape, args, sem, scratch_shapes=(), hosted=None):
    n_in, n_out, n_scr = len(args), len(out_shape), len(scratch_shapes)
    if hosted is None:
        outs = pl.pallas_call(
            body, name=name, grid=grid, in_specs=list(in_specs), out_specs=list(out_specs), out_shape=list(out_shape),
            scratch_shapes=list(scratch_shapes), compiler_params=_params(*sem),
        )(*args)
        return list(outs), []
    n_hin, n_hout = len(hosted.args), len(hosted.out_shapes)

    def carrying(*refs):
        own_in, their_in = refs[:n_in], refs[n_in : n_in + n_hin]
        refs = refs[n_in + n_hin :]
        own_out, their_out = refs[:n_out], refs[n_out : n_out + n_hout]
        refs = refs[n_out + n_hout :]
        own_scratch, sems = refs[:n_scr], refs[n_scr:]
        ids = [pl.program_id(axis) for axis in range(len(grid))]
        first = functools.reduce(jnp.logical_and, [i == 0 for i in ids])
        last = functools.reduce(jnp.logical_and, [i == size - 1 for i, size in zip(ids, grid)])

        @pl.when(first)
        def _():
            hosted.start(their_in, their_out, sems)

        body(*own_in, *own_out, *own_scratch)

        @pl.when(last)
        def _():
            hosted.finish(their_in, their_out, sems)

    outs = pl.pallas_call(
        carrying,
        name=name,
        grid=grid,
        in_specs=list(in_specs) + [ANY] * n_hin,
        out_specs=list(out_specs) + [ANY] * n_hout,
        out_shape=list(out_shape) + hosted.out_shapes,
        input_output_aliases={n_in + i: n_out + o for i, o in hosted.aliases.items()},
        scratch_shapes=list(scratch_shapes) + hosted.sem_shapes,
        compiler_params=_params(*["arbitrary"] * len(grid)),
    )(*args, *hosted.args)
    return list(outs[:n_out]), list(outs[n_out:])


def _merge(plans):
    if len(plans) == 1:
        return plans[0]
    args, out_shapes, aliases, sem_shapes, spans = [], [], {}, [], []
    for p in plans:
        spans.append((len(args), len(out_shapes), len(sem_shapes)))
        aliases.update({len(args) + i: len(out_shapes) + o for i, o in p.aliases.items()})
        args, out_shapes, sem_shapes = args + p.args, out_shapes + p.out_shapes, sem_shapes + p.sem_shapes

    def each(method, ins, outs, sems):
        for p, (a0, o0, s0) in zip(plans, spans):
            getattr(p, method)(ins[a0 : a0 + len(p.args)], outs[o0 : o0 + len(p.out_shapes)], sems[s0 : s0 + len(p.sem_shapes)])

    return _Hosted(args, out_shapes, aliases, sem_shapes, functools.partial(each, "start"), functools.partial(each, "finish"))


def _run(plan, *, name):
    n_in, n_out = len(plan.args), len(plan.out_shapes)

    def body(*refs):
        ins, outs, sems = refs[:n_in], refs[n_in : n_in + n_out], refs[n_in + n_out :]
        plan.start(ins, outs, sems)
        plan.finish(ins, outs, sems)

    return pl.pallas_call(
        body, name=name, in_specs=[ANY] * n_in, out_specs=[ANY] * n_out, out_shape=plan.out_shapes,
        input_output_aliases=plan.aliases, scratch_shapes=plan.sem_shapes,
    )(*plan.args)


def _mm(a, b, *, name, ta=False, tb=False, out_dtype=F32, b_idx=None, hosted=None, tm=1024, tn=1024, tk=None):
    m, k = (a.shape[1], a.shape[0]) if ta else a.shape
    b2 = b.shape if b_idx is None else b.shape[1:]
    kb, n = (b2[1], b2[0]) if tb else b2
    assert k == kb, (a.shape, b.shape, ta, tb)
    tm = _tile(m, tm, LANES)
    tn = _tile(n, tn, LANES)
    tk = _tile(k, 2048 if tk is None else tk, LANES)
    nk = k // tk
    dims = (((0 if ta else 1,), (1 if tb else 0,)), ((), ()))

    def body(a_ref, b_ref, o_ref, *acc):
        part = lax.dot_general(a_ref[...].astype(BF16), b_ref[...].astype(BF16), dims, preferred_element_type=F32)
        if nk == 1:
            o_ref[...] = part.astype(o_ref.dtype)
        else:
            acc_ref = acc[0]
            step = pl.program_id(2)

            @pl.when(step == 0)
            def _():
                acc_ref[...] = part

            @pl.when(step > 0)
            def _():
                acc_ref[...] += part

            @pl.when(step == nk - 1)
            def _():
                o_ref[...] = acc_ref[...].astype(o_ref.dtype)

    a_spec = pl.BlockSpec((tk, tm), lambda i, j, s: (s, i)) if ta else pl.BlockSpec((tm, tk), lambda i, j, s: (i, s))
    if b_idx is None:
        b_spec = pl.BlockSpec((tn, tk), lambda i, j, s: (j, s)) if tb else pl.BlockSpec((tk, tn), lambda i, j, s: (s, j))
    elif tb:
        b_spec = pl.BlockSpec((None, tn, tk), lambda i, j, s: (b_idx, j, s))
    else:
        b_spec = pl.BlockSpec((None, tk, tn), lambda i, j, s: (b_idx, s, j))
    outs, carried = _call(
        body,
        name=name,
        grid=(m // tm, n // tn, nk),
        in_specs=[a_spec, b_spec],
        out_specs=[pl.BlockSpec((tm, tn), lambda i, j, s: (i, j))],
        out_shape=[jax.ShapeDtypeStruct((m, n), out_dtype)],
        args=[a, b],
        sem=("parallel", "parallel", "arbitrary"),
        scratch_shapes=[pltpu.VMEM((tm, tn), F32)] if nk > 1 else [],
        hosted=hosted,
    )
    return outs[0] if hosted is None else (outs[0], carried)


def _tp_fwd(sc16, w, bias, *, name):
    nl, d, n = w.shape
    tn = _tile(n, 512, LANES)

    def body(sc_ref, w_ref, b_ref, o_ref):
        o_ref[0] = jnp.dot(sc_ref[...].astype(BF16), w_ref[0].astype(BF16), preferred_element_type=F32) + b_ref[0]

    return pl.pallas_call(
        body,
        name=name,
        grid=(nl, n // tn),
        in_specs=[
            pl.BlockSpec((16, d), lambda l, j: (0, 0)),
            pl.BlockSpec((1, d, tn), lambda l, j: (l, 0, j)),
            pl.BlockSpec((1, 1, tn), lambda l, j: (l, 0, j)),
        ],
        out_specs=pl.BlockSpec((1, 16, tn), lambda l, j: (l, 0, j)),
        out_shape=jax.ShapeDtypeStruct((nl, 16, n), F32),
        compiler_params=_params("parallel", "parallel"),
    )(sc16, w, bias)


_NT = (((1,), (1,)), ((), ()))
_TN = (((0,), (0,)), ((), ()))


def _silu_parts(a):
    sig = jax.nn.sigmoid(a)
    return a * sig, sig * (1.0 + a * (1.0 - sig))


def _ffn_in(h, w_gate, w_up, layer, *, name, hosted=None):
    s, d = h.shape
    f = w_gate.shape[2]
    tm, tn = _tile(s, 1024, LANES), _tile(f, 512, LANES)

    def body(h_ref, g_ref, u_ref, ga_ref, gb_ref, z_ref):
        hv = h_ref[...]
        a = jnp.dot(hv, g_ref[...], preferred_element_type=F32)
        b = jnp.dot(hv, u_ref[...], preferred_element_type=F32)
        silu, dsilu = _silu_parts(a)
        ga_ref[...] = (b * dsilu).astype(ga_ref.dtype)
        gb_ref[...] = silu.astype(gb_ref.dtype)
        z_ref[...] = (silu * b).astype(z_ref.dtype)

    w_spec = pl.BlockSpec((None, d, tn), lambda i, j: (layer, 0, j))
    out = pl.BlockSpec((tm, tn), lambda i, j: (i, j))
    return _call(
        body,
        name=name,
        grid=(s // tm, f // tn),
        in_specs=[pl.BlockSpec((tm, d), lambda i, j: (i, 0)), w_spec, w_spec],
        out_specs=[out] * 3,
        out_shape=[jax.ShapeDtypeStruct((s, f), BF16)] * 3,
        args=[h, w_gate, w_up],
        sem=("parallel", "parallel"),
        hosted=hosted,
    )


def _ffn_down_bwd(df, w_down, layer, dz_da, dz_db, *, name, hosted=None):
    s, d = df.shape
    f = w_down.shape[1]
    tm, tn = _tile(s, 1024, LANES), _tile(f, 512, LANES)

    def body(df_ref, w_ref, ga_ref, gb_ref, da_ref, db_ref):
        dz = lax.dot_general(df_ref[...], w_ref[...], _NT, preferred_element_type=F32)
        da_ref[...] = (dz * ga_ref[...].astype(F32)).astype(da_ref.dtype)
        db_ref[...] = (dz * gb_ref[...].astype(F32)).astype(db_ref.dtype)

    blk = pl.BlockSpec((tm, tn), lambda i, j: (i, j))
    return _call(
        body,
        name=name,
        grid=(s // tm, f // tn),
        in_specs=[pl.BlockSpec((tm, d), lambda i, j: (i, 0)), pl.BlockSpec((None, tn, d), lambda i, j: (layer, j, 0)), blk, blk],
        out_specs=[blk, blk],
        out_shape=[jax.ShapeDtypeStruct((s, f), BF16)] * 2,
        args=[df, w_down, dz_da, dz_db],
        sem=("parallel", "parallel"),
        hosted=hosted,
    )


def _ffn_in_dx(da, db, w_gate, w_up, layer, *, name, hosted=None):
    s, f = da.shape
    d = w_gate.shape[1]
    tm, tn, tk = _tile(s, 512, LANES), _tile(d, 1024, LANES), _tile(f, 3072, LANES)
    nk = f // tk

    def body(da_ref, db_ref, g_ref, u_ref, o_ref, acc_ref):
        part = lax.dot_general(da_ref[...], g_ref[...], _NT, preferred_element_type=F32)
        part = part + lax.dot_general(db_ref[...], u_ref[...], _NT, preferred_element_type=F32)
        step = pl.program_id(2)

        @pl.when(step == 0)
        def _():
            acc_ref[...] = part

        @pl.when(step > 0)
        def _():
            acc_ref[...] += part

        @pl.when(step == nk - 1)
        def _():
            o_ref[...] = acc_ref[...]

    x_spec = pl.BlockSpec((tm, tk), lambda i, j, k: (i, k))
    w_spec = pl.BlockSpec((None, tn, tk), lambda i, j, k: (layer, j, k))
    return _call(
        body,
        name=name,
        grid=(s // tm, d // tn, nk),
        in_specs=[x_spec, x_spec, w_spec, w_spec],
        out_specs=[pl.BlockSpec((tm, tn), lambda i, j, k: (i, j))],
        out_shape=[jax.ShapeDtypeStruct((s, d), F32)],
        args=[da, db, w_gate, w_up],
        sem=("parallel", "parallel", "arbitrary"),
        scratch_shapes=[pltpu.VMEM((tm, tn), F32)],
        hosted=hosted,
    )


def _ffn_in_dw(h, da, db, *, name, hosted=None):
    s, d = h.shape
    f = da.shape[1]
    tm, tn, tk = _tile(d, 1024, LANES), _tile(f, 512, LANES), _tile(s, 4096, LANES)
    nk = s // tk

    def body(h_ref, da_ref, db_ref, g_ref, u_ref, *acc):
        hv = h_ref[...]
        pg = lax.dot_general(hv, da_ref[...], _TN, preferred_element_type=F32)
        pu = lax.dot_general(hv, db_ref[...], _TN, preferred_element_type=F32)
        if nk == 1:
            g_ref[...] = pg.astype(g_ref.dtype)
            u_ref[...] = pu.astype(u_ref.dtype)
            return
        g_acc, u_acc = acc
        step = pl.program_id(2)

        @pl.when(step == 0)
        def _():
            g_acc[...] = pg
            u_acc[...] = pu

        @pl.when(step > 0)
        def _():
            g_acc[...] += pg
            u_acc[...] += pu

        @pl.when(step == nk - 1)
        def _():
            g_ref[...] = g_acc[...].astype(g_ref.dtype)
            u_ref[...] = u_acc[...].astype(u_ref.dtype)

    y_spec = pl.BlockSpec((tk, tn), lambda i, j, k: (k, j))
    out = pl.BlockSpec((tm, tn), lambda i, j, k: (i, j))
    return _call(
        body,
        name=name,
        grid=(d // tm, f // tn, nk),
        in_specs=[pl.BlockSpec((tk, tm), lambda i, j, k: (k, i)), y_spec, y_spec],
        out_specs=[out, out],
        out_shape=[jax.ShapeDtypeStruct((d, f), BF16)] * 2,
        args=[h, da, db],
        sem=("parallel", "parallel", "arbitrary"),
        scratch_shapes=[pltpu.VMEM((tm, tn), F32)] * 2 if nk > 1 else [],
        hosted=hosted,
    )


def _gmm(a, w, *, name, mode, out_dtype):
    s = a.shape[0]
    g = len(POOL_WINDOWS)
    c = a.shape[1] // g
    tr = _tile(s, 1024, LANES)
    n_row = s // tr

    if mode == "tn":

        def body(a_ref, b_ref, o_ref, acc_ref):
            part = lax.dot_general(a_ref[...].astype(BF16), b_ref[...].astype(BF16), (((0,), (0,)), ((), ())), preferred_element_type=F32)

            @pl.when(pl.program_id(1) == 0)
            def _():
                acc_ref[...] = part

            @pl.when(pl.program_id(1) > 0)
            def _():
                acc_ref[...] += part

            @pl.when(pl.program_id(1) == n_row - 1)
            def _():
                o_ref[0] = acc_ref[...].astype(o_ref.dtype)

        return pl.pallas_call(
            body,
            name=name,
            grid=(g, n_row),
            in_specs=[pl.BlockSpec((tr, c), lambda gi, i: (i, gi)), pl.BlockSpec((tr, c), lambda gi, i: (i, gi))],
            out_specs=pl.BlockSpec((1, c, c), lambda gi, i: (gi, 0, 0)),
            out_shape=jax.ShapeDtypeStruct((g, c, c), out_dtype),
            scratch_shapes=[pltpu.VMEM((c, c), F32)],
            compiler_params=_params("parallel", "arbitrary"),
        )(a, w)

    dims = (((1,), (0 if mode == "nn" else 1,)), ((), ()))

    def body(a_ref, w_ref, o_ref):
        o_ref[...] = lax.dot_general(a_ref[...].astype(BF16), w_ref[0].astype(BF16), dims, preferred_element_type=F32).astype(o_ref.dtype)

    return pl.pallas_call(
        body,
        name=name,
        grid=(g, n_row),
        in_specs=[pl.BlockSpec((tr, c), lambda gi, i: (i, gi)), pl.BlockSpec((1, c, c), lambda gi, i: (gi, 0, 0))],
        out_specs=pl.BlockSpec((tr, c), lambda gi, i: (i, gi)),
        out_shape=jax.ShapeDtypeStruct((s, g * c), out_dtype),
        compiler_params=_params("parallel", "parallel"),
    )(a, w)


def _row_tile(s, d):
    return _tile(s, max(8, (1 << 19) // d), 8)


def _norm_fwd(x, g, *, name, scale=None, shift=None, y=None, gate=None, out_dtype=BF16):
    s, d = x.shape
    tr = _row_tile(s, d)
    has_res, has_mod = y is not None, scale is not None

    def body(*refs):
        refs = list(refs)
        x_ref = refs.pop(0)
        xv = x_ref[...]
        if has_res:
            y_ref, gate_ref = refs.pop(0), refs.pop(0)
            xv = xv + gate_ref[...] * y_ref[...]
        g_ref = refs.pop(0)
        if has_mod:
            scale_ref, shift_ref = refs.pop(0), refs.pop(0)
        if has_res:
            refs.pop(0)[...] = xv
        h = xv * lax.rsqrt(jnp.mean(xv * xv, axis=-1, keepdims=True) + NORM_EPS)
        h = h * g_ref[...]
        if has_mod:
            h = h * (1.0 + scale_ref[...]) + shift_ref[...]
        refs.pop(0)[...] = h.astype(out_dtype)

    row = pl.BlockSpec((tr, d), lambda i: (i, 0))
    vec = pl.BlockSpec((1, d), lambda i: (0, 0))
    args, in_specs = [x], [row]
    if has_res:
        args += [y, gate]
        in_specs += [row, vec]
    args.append(g)
    in_specs.append(vec)
    if has_mod:
        args += [scale, shift]
        in_specs += [vec, vec]
    out_shape, out_specs = [], []
    if has_res:
        out_shape.append(jax.ShapeDtypeStruct((s, d), F32))
        out_specs.append(row)
    out_shape.append(jax.ShapeDtypeStruct((s, d), out_dtype))
    out_specs.append(row)
    res = pl.pallas_call(
        body, name=name, grid=(s // tr,), in_specs=in_specs, out_specs=out_specs, out_shape=out_shape, compiler_params=_params("parallel")
    )(*args)
    return (res[0], res[1]) if has_res else res[0]


def _norm_bwd(x, g, dh, *, name, scale=None, resid=None, hosted=None):
    s, d = x.shape
    tr = _row_tile(s, d)
    has_mod, has_res = scale is not None, resid is not None

    def body(*refs):
        refs = list(refs)
        x_ref, g_ref, dh_ref = refs.pop(0), refs.pop(0), refs.pop(0)
        scale_ref = refs.pop(0) if has_mod else None
        resid_ref = refs.pop(0) if has_res else None
        dx_ref, sums_ref = refs
        xv = x_ref[...]
        r = lax.rsqrt(jnp.mean(xv * xv, axis=-1, keepdims=True) + NORM_EPS)
        xhat = xv * r
        dh32 = dh_ref[...].astype(F32)
        gv = g_ref[...]
        dn = dh32 * (1.0 + scale_ref[...]) if has_mod else dh32
        dxhat = dn * gv
        dx = r * (dxhat - xhat * jnp.mean(dxhat * xhat, axis=-1, keepdims=True))
        if has_res:
            dx = dx + resid_ref[...]
        dx_ref[...] = dx

        @pl.when(pl.program_id(0) == 0)
        def _():
            sums_ref[...] = jnp.zeros_like(sums_ref)

        sums_ref[0:1, :] += jnp.sum(dh32, axis=0, keepdims=True)
        sums_ref[1:2, :] += jnp.sum(dh32 * (xhat * gv), axis=0, keepdims=True)
        sums_ref[2:3, :] += jnp.sum(dn * xhat, axis=0, keepdims=True)

    row = pl.BlockSpec((tr, d), lambda i: (i, 0))
    vec = pl.BlockSpec((1, d), lambda i: (0, 0))
    args, in_specs = [x, g, dh], [row, vec, row]
    if has_mod:
        args.append(scale)
        in_specs.append(vec)
    if has_res:
        args.append(resid)
        in_specs.append(row)
    outs, carried = _call(
        body,
        name=name,
        grid=(s // tr,),
        in_specs=in_specs,
        out_specs=[row, pl.BlockSpec((8, d), lambda i: (0, 0))],
        out_shape=[jax.ShapeDtypeStruct((s, d), F32), jax.ShapeDtypeStruct((8, d), F32)],
        args=args,
        sem=("arbitrary",),
        hosted=hosted,
    )
    return outs if hosted is None else (outs, carried)


def _gate_bwd(dx, y, gate, *, name):
    s, d = dx.shape
    tr = _row_tile(s, d)

    def body(dx_ref, y_ref, gate_ref, dy_ref, sums_ref):
        dxv = dx_ref[...]
        dy_ref[...] = (dxv * gate_ref[...]).astype(dy_ref.dtype)

        @pl.when(pl.program_id(0) == 0)
        def _():
            sums_ref[...] = jnp.zeros_like(sums_ref)

        sums_ref[0:1, :] += jnp.sum(dxv * y_ref[...], axis=0, keepdims=True)

    row = pl.BlockSpec((tr, d), lambda i: (i, 0))
    return pl.pallas_call(
        body,
        name=name,
        grid=(s // tr,),
        in_specs=[row, row, pl.BlockSpec((1, d), lambda i: (0, 0))],
        out_specs=[row, pl.BlockSpec((8, d), lambda i: (0, 0))],
        out_shape=[jax.ShapeDtypeStruct((s, d), BF16), jax.ShapeDtypeStruct((8, d), F32)],
        compiler_params=_params("arbitrary"),
    )(dx, y, gate)


def _elementwise(fn, args, out_dtypes, *, name):
    s, d = args[0].shape
    tc = d if d <= 2048 else _tile(d, 1024, LANES)
    tr = _tile(s, max(8, (1 << 18) // tc), 8)
    n_in = len(args)

    def body(*refs):
        outs = fn(*[r[...] for r in refs[:n_in]])
        for o_ref, o in zip(refs[n_in:], outs):
            o_ref[...] = o.astype(o_ref.dtype)

    spec = pl.BlockSpec((tr, tc), lambda i, j: (i, j))
    return pl.pallas_call(
        body,
        name=name,
        grid=(s // tr, d // tc),
        in_specs=[spec] * n_in,
        out_specs=[spec] * len(out_dtypes),
        out_shape=[jax.ShapeDtypeStruct((s, d), dt) for dt in out_dtypes],
        compiler_params=_params("parallel", "parallel"),
    )(*args)


def _silu(v):
    return (v * jax.nn.sigmoid(v),)


def _split3(v):
    hi = v.astype(BF16)
    r1 = v - hi.astype(F32)
    mid = r1.astype(BF16)
    lo = (r1 - mid.astype(F32)).astype(BF16)
    return hi, mid, lo


def _band_dot(band, v):
    return sum(jnp.dot(band, part, preferred_element_type=F32) for part in _split3(v))


def _pool(h, *, name, transpose, out_dtype):
    s, d = h.shape
    c = d // len(POOL_WINDOWS)
    tr = _tile(s, 256, HALO)
    per = tr // HALO
    n_halo = s // HALO

    def body(h_ref, halo_ref, o_ref):
        i = pl.program_id(0)
        out_row = i * tr + lax.broadcasted_iota(jnp.int32, (tr, tr + HALO), 0)
        col = lax.broadcasted_iota(jnp.int32, (tr, tr + HALO), 1)
        if transpose:
            ext = jnp.concatenate([h_ref[...], halo_ref[...]], axis=0)
            src_row = i * tr + col
            ext_row = i * tr + lax.broadcasted_iota(jnp.int32, (tr + HALO, 1), 0)
        else:
            ext = jnp.concatenate([halo_ref[...], h_ref[...]], axis=0)
            src_row = i * tr + col - HALO
            own_row = i * tr + lax.broadcasted_iota(jnp.int32, (tr, 1), 0)
        for gi, w in enumerate(POOL_WINDOWS):
            cols = slice(gi * c, (gi + 1) * c)
            if transpose:
                band = (src_row >= out_row) & (src_row < out_row + w) & (src_row < s)
                scaled = ext[:, cols] / jnp.minimum(ext_row + 1, w).astype(F32)
                res = _band_dot(band.astype(BF16), scaled) - h_ref[:, cols]
            else:
                band = (src_row <= out_row) & (src_row > out_row - w) & (src_row >= 0)
                res = _band_dot(band.astype(BF16), ext[:, cols]) / jnp.minimum(own_row + 1, w).astype(F32) - h_ref[:, cols]
            o_ref[:, cols] = res.astype(o_ref.dtype)

    if transpose:
        halo_map = lambda i: (jnp.minimum((i + 1) * per, n_halo - 1), 0)
    else:
        halo_map = lambda i: (jnp.maximum(i * per - 1, 0), 0)
    return pl.pallas_call(
        body,
        name=name,
        grid=(s // tr,),
        in_specs=[pl.BlockSpec((tr, d), lambda i: (i, 0)), pl.BlockSpec((HALO, d), halo_map)],
        out_specs=pl.BlockSpec((tr, d), lambda i: (i, 0)),
        out_shape=jax.ShapeDtypeStruct((s, d), out_dtype),
        compiler_params=_params("parallel"),
    )(h, h)


def _rotate(v, cos, sin):
    lane = lax.broadcasted_iota(jnp.int32, v.shape, 1)
    swapped = jnp.where(lane % ROPE_DIM < ROPE_DIM // 2, pltpu.roll(v, LANES - ROPE_DIM // 2, 1), pltpu.roll(v, ROPE_DIM // 2, 1))
    return v * cos + swapped * sin


def _rope_heads(x, cos, sin, *, name, out_dtype):
    s, n = x.shape
    tr = _tile(s, max(16, (1 << 18) // n), 16)

    def body(x_ref, cos_ref, sin_ref, o_ref):
        cos_v, sin_v = cos_ref[...], sin_ref[...]
        for j in range(n // LANES):
            lanes = slice(j * LANES, (j + 1) * LANES)
            if j % 2 == 0:
                o_ref[:, lanes] = x_ref[:, lanes].astype(o_ref.dtype)
            else:
                o_ref[:, lanes] = _rotate(x_ref[:, lanes].astype(F32), cos_v, sin_v).astype(o_ref.dtype)

    blk = pl.BlockSpec((tr, n), lambda i: (i, 0))
    tab = pl.BlockSpec((tr, LANES), lambda i: (i, 0))
    return pl.pallas_call(
        body,
        name=name,
        grid=(s // tr,),
        in_specs=[blk, tab, tab],
        out_specs=blk,
        out_shape=jax.ShapeDtypeStruct((s, n), out_dtype),
        compiler_params=_params("parallel"),
    )(x, cos, sin)


def _build_keys(kv, kr_pre, cos, sin, *, name):
    s, n = kv.shape
    heads = n // HEAD_PAD
    t = _attn_tile(s)

    def body(kv_ref, kr_ref, cos_ref, sin_ref, keys_ref, kt_ref, vt_ref):
        rope = _rotate(kr_ref[...], cos_ref[...], sin_ref[...])
        nope, val = kv_ref[:, :NOPE_DIM], kv_ref[:, NOPE_DIM:]
        keys_ref[:, :NOPE_DIM] = nope
        keys_ref[:, NOPE_DIM:] = rope.astype(keys_ref.dtype)
        kt_ref[0, 0, :NOPE_DIM, :] = nope.astype(F32).T.astype(kt_ref.dtype)
        kt_ref[0, 0, NOPE_DIM:, :] = rope.T.astype(kt_ref.dtype)
        vt_ref[0, 0] = val.astype(F32).T.astype(vt_ref.dtype)

    tab = pl.BlockSpec((t, LANES), lambda hd, kb: (kb, 0))
    return pl.pallas_call(
        body,
        name=name,
        grid=(heads, s // t),
        in_specs=[pl.BlockSpec((t, HEAD_PAD), lambda hd, kb: (kb, hd)), tab, tab, tab],
        out_specs=[
            pl.BlockSpec((t, HEAD_PAD), lambda hd, kb: (kb, hd)),
            pl.BlockSpec((1, 1, HEAD_PAD, t), lambda hd, kb: (hd, kb, 0, 0)),
            pl.BlockSpec((1, 1, V_DIM, t), lambda hd, kb: (hd, kb, 0, 0)),
        ],
        out_shape=[
            jax.ShapeDtypeStruct((s, n), BF16),
            jax.ShapeDtypeStruct((heads, s // t, HEAD_PAD, t), BF16),
            jax.ShapeDtypeStruct((heads, s // t, V_DIM, t), BF16),
        ],
        compiler_params=_params("parallel", "parallel"),
    )(kv, kr_pre, cos, sin)


def _rope_back(dq_t, cos, sin, *, name):
    heads, nq, _, t = dq_t.shape

    def body(x_ref, cos_ref, sin_ref, o_ref):
        x = x_ref[0, 0].T
        o_ref[:, :NOPE_DIM] = x[:, :NOPE_DIM].astype(o_ref.dtype)
        o_ref[:, NOPE_DIM:] = _rotate(x[:, NOPE_DIM:], cos_ref[...], sin_ref[...]).astype(o_ref.dtype)

    tab = pl.BlockSpec((t, LANES), lambda hd, qb: (qb, 0))
    return pl.pallas_call(
        body,
        name=name,
        grid=(heads, nq),
        in_specs=[pl.BlockSpec((1, 1, HEAD_PAD, t), lambda hd, qb: (hd, qb, 0, 0)), tab, tab],
        out_specs=pl.BlockSpec((t, HEAD_PAD), lambda hd, qb: (qb, hd)),
        out_shape=jax.ShapeDtypeStruct((nq * t, heads * HEAD_PAD), BF16),
        compiler_params=_params("parallel", "parallel"),
    )(dq_t, cos, sin)


def _keys_bwd(dk_a, dk_b, dv_a, dv_b, cos, sin_neg, *, name):
    s, n = dk_a.shape
    heads = n // HEAD_PAD
    tr = _tile(s, 512, 8)

    def body(dka_ref, dkb_ref, dva_ref, dvb_ref, cos_ref, sin_ref, dkv_ref, dkr_ref):
        hd = pl.program_id(1)
        dk = dka_ref[...] + dkb_ref[...]
        dkv_ref[:, :NOPE_DIM] = dk[:, :NOPE_DIM].astype(dkv_ref.dtype)
        dkv_ref[:, NOPE_DIM:] = (dva_ref[...] + dvb_ref[...]).astype(dkv_ref.dtype)

        @pl.when(hd == 0)
        def _():
            dkr_ref[...] = dk[:, NOPE_DIM:]

        @pl.when(hd > 0)
        def _():
            dkr_ref[...] += dk[:, NOPE_DIM:]

        @pl.when(hd == heads - 1)
        def _():
            dkr_ref[...] = _rotate(dkr_ref[...], cos_ref[...], sin_ref[...])

    dk_blk = pl.BlockSpec((tr, HEAD_PAD), lambda i, hd: (i, hd))
    dv_blk = pl.BlockSpec((tr, V_DIM), lambda i, hd: (i, hd))
    tab = pl.BlockSpec((tr, LANES), lambda i, hd: (i, 0))
    return pl.pallas_call(
        body,
        name=name,
        grid=(s // tr, heads),
        in_specs=[dk_blk, dk_blk, dv_blk, dv_blk, tab, tab],
        out_specs=[dk_blk, tab],
        out_shape=[jax.ShapeDtypeStruct((s, n), BF16), jax.ShapeDtypeStruct((s, LANES), F32)],
        compiler_params=_params("parallel", "arbitrary"),
    )(dk_a, dk_b, dv_a, dv_b, cos, sin_neg)


def _attn_tile(s):
    return _tile(s, 512, LANES)


def _causal_mask(t):
    return lax.broadcasted_iota(jnp.int32, (t, t), 0) <= lax.broadcasted_iota(jnp.int32, (t, t), 1)


def _attn_fwd(q, keys, v_t, *, name, hosted=None):
    s = q.shape[0]
    heads = q.shape[1] // HEAD_PAD
    t = _attn_tile(s)
    nq = s // t

    def body(q_ref, k_ref, v_ref, o_ref, lse_ref, m_ref, l_ref, acc_ref):
        qi = pl.program_id(1)
        qv = q_ref[...]
        m_ref[...] = jnp.full_like(m_ref, -jnp.inf)
        l_ref[...] = jnp.zeros_like(l_ref)
        acc_ref[...] = jnp.zeros_like(acc_ref)

        def block(kb, diagonal):
            rows = pl.ds(pl.multiple_of(kb * t, t), t)
            sc_t = lax.dot_general(k_ref[rows, :], qv, _NT, preferred_element_type=F32) * SM_SCALE
            if diagonal:
                sc_t = jnp.where(_causal_mask(t), sc_t, -jnp.inf)
            m_old = m_ref[...]
            m_new = jnp.maximum(m_old, jnp.max(sc_t, axis=0, keepdims=True))
            alpha = jnp.exp(m_old - m_new)
            p_t = jnp.exp(sc_t - m_new)
            l_ref[...] = alpha * l_ref[...] + jnp.sum(p_t, axis=0, keepdims=True)
            acc_ref[...] = alpha * acc_ref[...] + jnp.dot(v_ref[0, kb], p_t.astype(BF16), preferred_element_type=F32)
            m_ref[...] = m_new

        def earlier(kb, carry):
            block(kb, False)
            return carry

        lax.fori_loop(0, qi, earlier, 0)
        block(qi, True)
        o_ref[...] = (acc_ref[...] / l_ref[...]).T.astype(o_ref.dtype)
        lse_ref[0, 0] = m_ref[...] + jnp.log(l_ref[...])

    return _call(
        body,
        name=name,
        grid=(heads, nq),
        in_specs=[
            pl.BlockSpec((t, HEAD_PAD), lambda hd, qi: (qi, hd)),
            pl.BlockSpec((s, HEAD_PAD), lambda hd, qi: (0, hd)),
            pl.BlockSpec((1, nq, V_DIM, t), lambda hd, qi: (hd, 0, 0, 0)),
        ],
        out_specs=[pl.BlockSpec((t, V_DIM), lambda hd, qi: (qi, hd)), pl.BlockSpec((1, 1, 1, t), lambda hd, qi: (hd, qi, 0, 0))],
        out_shape=[jax.ShapeDtypeStruct((s, heads * V_DIM), BF16), jax.ShapeDtypeStruct((heads, nq, 1, t), F32)],
        args=[q, keys, v_t],
        sem=("parallel", "parallel"),
        scratch_shapes=[pltpu.VMEM((1, t), F32), pltpu.VMEM((1, t), F32), pltpu.VMEM((V_DIM, t), F32)],
        hosted=hosted,
    )


def _attn_bwd(q, keys, keys_t, kv, o, do, lse, *, name, hosted=None):
    s = q.shape[0]
    heads = q.shape[1] // HEAD_PAD
    t = _attn_tile(s)
    nq = s // t

    def body(q_ref, k_ref, kt_ref, v_ref, o_ref, do_ref, lse_ref, dq_ref, dk_ref, dv_ref, dk_acc, dv_acc, delta_ref):
        ki = pl.program_id(1)

        @pl.when(ki == 0)
        def _():
            dq_ref[...] = jnp.zeros_like(dq_ref)
            ones = jnp.ones((8, V_DIM), BF16)

            def row_sums(qb, carry):
                rows = pl.ds(pl.multiple_of(qb * t, t), t)
                prod = do_ref[rows, :].astype(F32) * o_ref[rows, :].astype(F32)
                sums = sum(lax.dot_general(ones, part, _NT, preferred_element_type=F32) for part in _split3(prod))
                delta_ref[qb] = sums[0:1]
                return carry

            lax.fori_loop(0, nq, row_sums, 0)

        kv_, vv, kt = k_ref[...], v_ref[...], kt_ref[0, 0]
        dk_acc[...] = jnp.zeros_like(dk_acc)
        dv_acc[...] = jnp.zeros_like(dv_acc)

        def block(qb, diagonal):
            rows = pl.ds(pl.multiple_of(qb * t, t), t)
            qv, dov = q_ref[rows, :], do_ref[rows, :]
            sc_t = lax.dot_general(kv_, qv, _NT, preferred_element_type=F32) * SM_SCALE
            p_t = jnp.exp(sc_t - lse_ref[0, qb])
            if diagonal:
                p_t = jnp.where(_causal_mask(t), p_t, 0.0)
            dv_acc[...] += jnp.dot(p_t.astype(BF16), dov, preferred_element_type=F32)
            dp_t = lax.dot_general(vv, dov, _NT, preferred_element_type=F32)
            ds_t = (p_t * (dp_t - delta_ref[qb]) * SM_SCALE).astype(BF16)
            dk_acc[...] += jnp.dot(ds_t, qv, preferred_element_type=F32)
            dq_ref[0, qb] += jnp.dot(kt, ds_t, preferred_element_type=F32)

        def later(qb, carry):
            block(qb, False)
            return carry

        block(ki, True)
        lax.fori_loop(ki + 1, nq, later, 0)
        dk_ref[...] = dk_acc[...]
        dv_ref[...] = dv_acc[...]

    whole = pl.BlockSpec((s, V_DIM), lambda hd, ki: (0, hd))
    return _call(
        body,
        name=name,
        grid=(heads, nq),
        in_specs=[
            pl.BlockSpec((s, HEAD_PAD), lambda hd, ki: (0, hd)),
            pl.BlockSpec((t, HEAD_PAD), lambda hd, ki: (ki, hd)),
            pl.BlockSpec((1, 1, HEAD_PAD, t), lambda hd, ki: (hd, ki, 0, 0)),
            pl.BlockSpec((t, V_DIM), lambda hd, ki: (ki, 2 * hd + 1)),
            whole,
            whole,
            pl.BlockSpec((1, nq, 1, t), lambda hd, ki: (hd, 0, 0, 0)),
        ],
        out_specs=[
            pl.BlockSpec((1, nq, HEAD_PAD, t), lambda hd, ki: (hd, 0, 0, 0)),
            pl.BlockSpec((t, HEAD_PAD), lambda hd, ki: (ki, hd)),
            pl.BlockSpec((t, V_DIM), lambda hd, ki: (ki, hd)),
        ],
        out_shape=[
            jax.ShapeDtypeStruct((heads, nq, HEAD_PAD, t), F32),
            jax.ShapeDtypeStruct((s, heads * HEAD_PAD), F32),
            jax.ShapeDtypeStruct((s, heads * V_DIM), F32),
        ],
        args=[q, keys, keys_t, kv, o, do, lse],
        sem=("parallel", "arbitrary"),
        scratch_shapes=[pltpu.VMEM((t, HEAD_PAD), F32), pltpu.VMEM((t, V_DIM), F32), pltpu.VMEM((nq, 1, t), F32)],
        hosted=hosted,
    )


def _loss_bwd(x, y, gate, g, target, *, name):
    s, d = x.shape
    tr = _row_tile(s, d)

    def body(x_ref, y_ref, gate_ref, g_ref, t_ref, dx_ref, stats_ref, loss_ref):
        xv = x_ref[...] + gate_ref[...] * y_ref[...]
        r = lax.rsqrt(jnp.mean(xv * xv, axis=-1, keepdims=True) + NORM_EPS)
        xhat = xv * r
        gv = g_ref[...]
        err = xhat * gv - t_ref[...]
        dy = err / d
        dxhat = dy * gv
        dx_ref[...] = r * (dxhat - xhat * jnp.mean(dxhat * xhat, axis=-1, keepdims=True))

        @pl.when(pl.program_id(0) == 0)
        def _():
            stats_ref[...] = jnp.zeros_like(stats_ref)
            loss_ref[...] = jnp.zeros_like(loss_ref)

        stats_ref[0:1, :] += jnp.sum(dy * xhat, axis=0, keepdims=True)
        loss_ref[...] += 0.5 * jnp.sum(jnp.mean(err * err, axis=-1, keepdims=True))

    row = pl.BlockSpec((tr, d), lambda i: (i, 0))
    vec = pl.BlockSpec((1, d), lambda i: (0, 0))
    return pl.pallas_call(
        body,
        name=name,
        grid=(s // tr,),
        in_specs=[row, row, vec, vec, row],
        out_specs=[row, pl.BlockSpec((8, d), lambda i: (0, 0)), pl.BlockSpec((8, LANES), lambda i: (0, 0))],
        out_shape=[jax.ShapeDtypeStruct((s, d), F32), jax.ShapeDtypeStruct((8, d), F32), jax.ShapeDtypeStruct((8, LANES), F32)],
        compiler_params=_params("arbitrary"),
    )(x, y, gate, g, target)


def _adam_math(w, g, m, v):
    new_m = ADAM_B1 * m + (1.0 - ADAM_B1) * g
    new_v = ADAM_B2 * v + (1.0 - ADAM_B2) * (g * g)
    m_hat = new_m / (1.0 - ADAM_B1**ADAM_STEP)
    v_hat = new_v / (1.0 - ADAM_B2**ADAM_STEP)
    return -ADAM_LR * (m_hat / (jnp.sqrt(v_hat) + ADAM_EPS) + ADAM_WD * w), new_m, new_v


def _adamw(w, g, m, v, *, name):
    rows, cols = w.shape
    tr = _tile(rows, max(8, (1 << 18) // cols), 8)

    def body(w_ref, g_ref, m_ref, v_ref, go_ref, d_ref, mo_ref, vo_ref):
        gv = g_ref[...]
        go_ref[...] = gv
        d_ref[...], mo_ref[...], vo_ref[...] = _adam_math(w_ref[...], gv, m_ref[...], v_ref[...])

    spec = pl.BlockSpec((tr, cols), lambda i: (i, 0))
    return pl.pallas_call(
        body,
        name=name,
        grid=(rows // tr,),
        in_specs=[spec] * 4,
        out_specs=[spec] * 4,
        out_shape=[jax.ShapeDtypeStruct((rows, cols), F32)] * 4,
        compiler_params=_params("parallel"),
    )(w, g, m, v)


def _tp_adamw(sc16, dm, w, m, v, *, name):
    nl, d, n = w.shape
    tm = _tile(d, 512, LANES)
    tn = _tile(n, 1024, LANES)

    def body(sc_ref, dm_ref, w_ref, m_ref, v_ref, go_ref, d_ref, mo_ref, vo_ref):
        gv = lax.dot_general(sc_ref[...].astype(BF16), dm_ref[0].astype(BF16), (((0,), (0,)), ((), ())), preferred_element_type=F32)
        go_ref[0] = gv
        d_ref[0], mo_ref[0], vo_ref[0] = _adam_math(w_ref[0], gv, m_ref[0], v_ref[0])

    blk = pl.BlockSpec((1, tm, tn), lambda l, i, j: (l, i, j))
    return pl.pallas_call(
        body,
        name=name,
        grid=(nl, d // tm, n // tn),
        in_specs=[pl.BlockSpec((16, tm), lambda l, i, j: (0, i)), pl.BlockSpec((1, 16, tn), lambda l, i, j: (l, 0, j)), blk, blk, blk],
        out_specs=[blk] * 4,
        out_shape=[jax.ShapeDtypeStruct((nl, d, n), F32)] * 4,
        compiler_params=_params("parallel", "parallel", "parallel"),
    )(sc16, dm, w, m, v)


def _sum_devices(x, *, name):
    def body(x_ref, o_ref):
        acc = x_ref[0]
        for k in range(1, N_DEV):
            acc = acc + x_ref[k]
        o_ref[...] = acc

    return pl.pallas_call(body, name=name, out_shape=jax.ShapeDtypeStruct(x.shape[1:], F32))(x)


def _place():
    mx, my, mc = lax.axis_index("x"), lax.axis_index("y"), lax.axis_index("c")
    chips = [(1 - mx, my), (mx, 1 - my), (1 - mx, 1 - my)]
    return mx, my, mc, chips


def _remote(src, dst, send_sem, recv_sem, device):
    return pltpu.make_async_remote_copy(src_ref=src, dst_ref=dst, send_sem=send_sem, recv_sem=recv_sem, device_id=device, device_id_type=MESH)


def _allgather8(x, *, name):
    def body(x_ref, out_ref, send_sems, recv_sems, local_sem):
        mx, my, mc, chips = _place()
        me, sibling = (mx, my, mc), (mx, my, 1 - mc)

        def slot(px, py, pc):
            return out_ref.at[4 * px + 2 * py + pc]

        def copy(k, block, to, src=None):
            return _remote(slot(*block) if src is None else src, slot(*block), send_sems.at[k], recv_sems.at[k], to)

        mine = pltpu.make_async_copy(x_ref, slot(*me), local_sem)
        mine.start()
        first = [copy(0, me, sibling, src=x_ref)] + [copy(1 + j, me, (*chip, mc), src=x_ref) for j, chip in enumerate(chips)]
        for cp in first:
            cp.start()
        passed = [copy(4 + j, (*chip, mc), sibling) for j, chip in enumerate(chips)]
        for j, chip in enumerate(chips):
            copy(1 + j, (*chip, mc), me).wait_recv()
            passed[j].start()
        copy(0, sibling, me).wait_recv()
        for j, chip in enumerate(chips):
            copy(4 + j, (*chip, 1 - mc), me).wait_recv()
        for cp in first + passed:
            cp.wait_send()
        mine.wait()

    return pl.pallas_call(
        body,
        name=name,
        out_shape=jax.ShapeDtypeStruct((N_DEV,) + x.shape, x.dtype),
        in_specs=[pl.BlockSpec(memory_space=pltpu.VMEM)],
        out_specs=pl.BlockSpec(memory_space=pltpu.VMEM),
        scratch_shapes=[pltpu.SemaphoreType.DMA((7,)), pltpu.SemaphoreType.DMA((7,)), pltpu.SemaphoreType.DMA],
    )(x)


class _Geom:
    def __init__(self, shape3, axis):
        self.shape3, self.axis = shape3, axis
        nl, r, c = shape3
        self.rs, self.cs = (r // N_CHIPS, c) if axis == 1 else (r, c // N_CHIPS)
        self.hl, self.hr = (nl // 2, self.rs) if nl > 1 else (1, self.rs // 2)
        self.shard = (nl, self.rs, self.cs)
        self.half = (self.hl, self.hr, self.cs)

    def in_full(self, ref, chip, core):
        nl = self.shape3[0]
        l0 = core * self.hl if nl > 1 else 0
        r0 = (chip * self.rs if self.axis == 1 else 0) + (0 if nl > 1 else core * self.hr)
        c0 = chip * self.cs if self.axis == 2 else 0
        return ref.at[pl.ds(l0, self.hl), pl.ds(r0, self.hr), pl.ds(c0, self.cs)]


def _place_shard(shard, geom, chip_arr, *, name, layer=None):
    nl, rs, cs = geom.shard
    tr = _tile(rs, max(16, (1 << 20) // cs), 16)
    per = rs // tr
    first = 0 if layer is None else layer

    def body(chip_ref, x_ref, o_ref):
        o_ref[...] = x_ref[...].astype(o_ref.dtype)

    def out_map(l, i, chip_ref):
        return (l, chip_ref[0] * per + i, 0) if geom.axis == 1 else (l, i, chip_ref[0])

    return pl.pallas_call(
        body,
        name=name,
        grid_spec=pltpu.PrefetchScalarGridSpec(
            num_scalar_prefetch=1,
            grid=(nl, per),
            in_specs=[pl.BlockSpec((1, tr, cs), lambda l, i, chip_ref: (first + l, i, 0))],
            out_specs=pl.BlockSpec((1, tr, cs), out_map),
        ),
        out_shape=jax.ShapeDtypeStruct(geom.shape3, BF16),
        compiler_params=_params("parallel", "parallel"),
    )(chip_arr, shard)


def _dma_sems(count, arrays):
    return [pltpu.SemaphoreType.DMA((count,))] * arrays


def _gather_plan(fulls, geoms):
    def ici(w, k, src, dst, sems, device):
        return _remote(src, dst, sems[0].at[3 * w + k], sems[1].at[3 * w + k], device)

    def d2d(w, k, box, sems, device):
        return _remote(box, box, sems[2].at[3 * w + k], sems[3].at[3 * w + k], device)

    def start(given, full, sems):
        mx, my, mc, chips = _place()
        me = 2 * mx + my
        for w, geom in enumerate(geoms):
            for k, chip in enumerate(chips):
                ici(w, k, geom.in_full(given[w], me, mc), geom.in_full(full[w], me, mc), sems, (*chip, mc)).start()

    def finish(given, full, sems):
        mx, my, mc, chips = _place()
        me, sibling = 2 * mx + my, (mx, my, 1 - mc)
        for w, geom in enumerate(geoms):
            for k, (px, py) in enumerate(chips):
                landed = geom.in_full(full[w], 2 * px + py, mc)
                ici(w, k, landed, landed, sems, (px, py, mc)).wait_recv()
                d2d(w, k, landed, sems, sibling).start()
        for w, geom in enumerate(geoms):
            for k, (px, py) in enumerate(chips):
                d2d(w, k, geom.in_full(full[w], 2 * px + py, 1 - mc), sems, sibling).wait_recv()
        for w, geom in enumerate(geoms):
            for k, (px, py) in enumerate(chips):
                ici(w, k, geom.in_full(given[w], me, mc), geom.in_full(full[w], me, mc), sems, (px, py, mc)).wait_send()
                d2d(w, k, geom.in_full(full[w], 2 * px + py, mc), sems, sibling).wait_send()

    n = len(fulls)
    shapes = [jax.ShapeDtypeStruct(f.shape, f.dtype) for f in fulls]
    return _Hosted(fulls, shapes, {w: w for w in range(n)}, _dma_sems(3 * n, 4), start, finish)


def _pair_plan(grads, geoms):
    def copies(grad, theirs, sems):
        mx, my, mc, _ = _place()
        return [
            _remote(geom.in_full(grad[w], chip, 1 - mc), theirs[w].at[chip], sems[0].at[4 * w + chip], sems[1].at[4 * w + chip], (mx, my, 1 - mc))
            for w, geom in enumerate(geoms)
            for chip in range(N_CHIPS)
        ]

    def start(grad, theirs, sems):
        for cp in copies(grad, theirs, sems):
            cp.start()

    def finish(grad, theirs, sems):
        for cp in copies(grad, theirs, sems):
            cp.wait_recv()
        for cp in copies(grad, theirs, sems):
            cp.wait_send()

    shapes = [jax.ShapeDtypeStruct((N_CHIPS,) + g.half, x.dtype) for g, x in zip(geoms, grads)]
    return _Hosted(grads, shapes, {}, _dma_sems(4 * len(grads), 2), start, finish)


def _half_tile(geom):
    return _tile(geom.hr, max(16, (1 << 18) // geom.cs), 16)


def _pair_add(grad, theirs, geom, core_arr, *, name):
    hl, hr, cs = geom.half
    tr = _half_tile(geom)
    stacked = geom.shape3[0] > 1

    def grad_map(chip, l, i, core_ref):
        layer = core_ref[0] * hl + l if stacked else 0
        row = (chip * (geom.rs // tr) if geom.axis == 1 else 0) + (0 if stacked else core_ref[0] * (hr // tr)) + i
        return layer, row, (chip if geom.axis == 2 else 0)

    def body(core_ref, g_ref, t_ref, o_ref):
        o_ref[0] = (g_ref[...].astype(F32) + t_ref[0].astype(F32)).astype(o_ref.dtype)

    blk = pl.BlockSpec((1, 1, tr, cs), lambda chip, l, i, core_ref: (chip, l, i, 0))
    return pl.pallas_call(
        body,
        name=name,
        grid_spec=pltpu.PrefetchScalarGridSpec(
            num_scalar_prefetch=1, grid=(N_CHIPS, hl, hr // tr), in_specs=[pl.BlockSpec((1, tr, cs), grad_map), blk], out_specs=blk
        ),
        out_shape=jax.ShapeDtypeStruct(theirs.shape, BF16),
        compiler_params=_params("parallel", "parallel", "parallel"),
    )(core_arr, grad, theirs)


def _chips_plan(parts):
    def copies(part, slots, sems):
        _, _, mc, chips = _place()
        return [
            _remote(part[w].at[2 * px + py], slots[w].at[k], sems[0].at[3 * w + k], sems[1].at[3 * w + k], (px, py, mc))
            for w in range(len(parts))
            for k, (px, py) in enumerate(chips)
        ]

    def start(part, slots, sems):
        for cp in copies(part, slots, sems):
            cp.start()

    def finish(part, slots, sems):
        for cp in copies(part, slots, sems):
            cp.wait_recv()
        for cp in copies(part, slots, sems):
            cp.wait_send()

    shapes = [jax.ShapeDtypeStruct((N_CHIPS - 1,) + p.shape[1:], p.dtype) for p in parts]
    return _Hosted(parts, shapes, {}, _dma_sems(3 * len(parts), 2), start, finish)


def _chip_sum(part, slots, geom, place_arr, *, name, stack=None):
    hl, hr, cs = geom.half
    tr = _half_tile(geom)

    def body(place_ref, own_ref, s0_ref, s1_ref, s2_ref, *rest):
        o_ref = rest[-1]
        o_ref[...] = ((own_ref[...].astype(F32) + s0_ref[...].astype(F32)) + s1_ref[...].astype(F32)) + s2_ref[...].astype(F32)

    def slot(k):
        return pl.BlockSpec((1, 1, tr, cs), lambda l, i, place_ref: (k, l, i, 0))

    in_specs = [pl.BlockSpec((1, 1, tr, cs), lambda l, i, place_ref: (place_ref[0], l, i, 0)), slot(0), slot(1), slot(2)]
    args = [place_arr, part, slots, slots, slots]
    aliases = {}
    if stack is None:
        out_spec = pl.BlockSpec((1, 1, tr, cs), lambda l, i, place_ref: (place_ref[1], l, i, 0))
        out_shape = jax.ShapeDtypeStruct((2,) + geom.half, F32)
    else:
        layers, layer, prev = stack
        assert hl == 1
        out_spec = pl.BlockSpec((1, 1, tr, cs), lambda l, i, place_ref: (layer, place_ref[1], i, 0))
        out_shape = jax.ShapeDtypeStruct((layers, 2, hr, cs), F32)
        if prev is not None:
            aliases = {len(args): 0}
            in_specs.append(ANY)
            args.append(prev)
    return pl.pallas_call(
        body,
        name=name,
        grid_spec=pltpu.PrefetchScalarGridSpec(num_scalar_prefetch=1, grid=(hl, hr // tr), in_specs=in_specs, out_specs=out_spec),
        out_shape=out_shape,
        input_output_aliases=aliases,
        compiler_params=_params("parallel", "parallel"),
    )(*args)


def _join_plan(boths, prefixes):
    def copies(given, both, sems):
        mx, my, mc, _ = _place()
        out, n = [], 0
        for w in range(len(boths)):
            for p in prefixes[w]:
                out.append(_remote(given[w].at[(*p, mc)], both[w].at[(*p, mc)], sems[0].at[n], sems[1].at[n], (mx, my, 1 - mc)))
                n += 1
        return out

    def arrivals(both, sems):
        mx, my, mc, _ = _place()
        out, n = [], 0
        for w in range(len(boths)):
            for p in prefixes[w]:
                got = both[w].at[(*p, 1 - mc)]
                out.append(_remote(got, got, sems[0].at[n], sems[1].at[n], (mx, my, 1 - mc)))
                n += 1
        return out

    def start(given, both, sems):
        for cp in copies(given, both, sems):
            cp.start()

    def finish(given, both, sems):
        for cp in arrivals(both, sems):
            cp.wait_recv()
        for cp in copies(given, both, sems):
            cp.wait_send()

    count = sum(len(p) for p in prefixes)
    shapes = [jax.ShapeDtypeStruct(b.shape, b.dtype) for b in boths]
    return _Hosted(boths, shapes, {w: w for w in range(len(boths))}, _dma_sems(count, 2), start, finish)


WEIGHTS = ("mod_w", "mod_b", "norm_mix", "norm_ffn", "pool_w", "pool_scale", "kv_mod_w", "kv_mod_b", "kv_in_norm", "w_dkv", "kv_norm",
           "w_uk", "w_uv", "w_kr", "w_dq", "q_norm", "w_uq", "w_o", "ffn_gate", "ffn_up", "ffn_down", "final_norm")
SMALL = ("mod_b", "kv_mod_b", "norm_mix", "norm_ffn", "kv_in_norm", "kv_norm", "q_norm", "final_norm")


def _rows(v):
    return v.reshape(-1, LANES)


def _pad_rows(a):
    return jnp.pad(a, ((0, (-a.shape[0]) % 8), (0, 0)))


def _vec(v):
    return v.reshape(1, -1)


def _step(x, c, positions, target, wts, mom, var):
    _, s, d = x.shape
    depth, n_a, n_b = wts["mod_w"].shape[0], wts["pool_w"].shape[0], wts["w_dq"].shape[0]
    assert n_b == 2 and n_a + n_b == depth
    heads = d // V_DIM
    kvr, qr = wts["w_dkv"].shape[1], wts["w_dq"].shape[2]
    ffn = wts["ffn_gate"].shape[2] * N_CHIPS
    pool_c = d // len(POOL_WINDOWS)
    nmod, nkv = N_MOD * d, 2 * d
    mx, my, mc = lax.axis_index("x"), lax.axis_index("y"), lax.axis_index("c")
    chip, dev = 2 * mx + my, 4 * mx + 2 * my + mc
    xs, tgt = x[0], target[0]

    inv_freq = 1.0 / (ROPE_THETA ** (jnp.arange(0, ROPE_DIM, 2, dtype=F32) / ROPE_DIM))
    ang = positions[0].astype(F32)[:, None] * inv_freq
    cos, sin, zero = jnp.cos(ang), jnp.sin(ang), jnp.zeros((s, LANES - ROPE_DIM), F32)
    cos_t = jnp.concatenate([cos, cos, zero], axis=1)
    sin_fwd = jnp.concatenate([-sin, sin, zero], axis=1)
    sin_bwd = jnp.concatenate([sin, -sin, zero], axis=1)

    c_rows, ps_rows = d // LANES, n_a * (d // N_CHIPS) // LANES
    cond = _allgather8(_pad_rows(jnp.concatenate([_rows(c), _rows(wts["pool_scale"])])), name="gather_cond")
    c_all = cond[:, :c_rows].reshape(N_DEV, d)
    pool_scale = cond[0::2, c_rows : c_rows + ps_rows].reshape(N_CHIPS, n_a, d // N_CHIPS).transpose(1, 0, 2).reshape(n_a, d)
    sc16 = _elementwise(_silu, [jnp.pad(c_all, ((0, 16 - N_DEV), (0, 0)))], [F32], name="silu_cond")[0]

    mod_bias = lax.dynamic_slice_in_dim(wts["mod_b"], chip * (nmod // N_CHIPS), nmod // N_CHIPS, axis=1)[:, None, :]
    kv_bias = lax.dynamic_slice_in_dim(wts["kv_mod_b"], chip * (nkv // N_CHIPS), nkv // N_CHIPS).reshape(1, 1, -1)
    mod_part = _tp_fwd(sc16, wts["mod_w"], mod_bias, name="mod_fwd")
    kv_part = _tp_fwd(sc16, wts["kv_mod_w"][None], kv_bias, name="kv_mod_fwd")
    part = jnp.concatenate([mod_part[i, :N_DEV] for i in range(depth)] + [kv_part[0, :N_DEV]], axis=1)
    ncol = part.shape[1]
    gathered = _allgather8(_pad_rows(_rows(part)), name="gather_mods")
    gathered = gathered[0::2, : N_DEV * ncol // LANES].reshape(N_CHIPS, N_DEV, ncol)
    mine = lax.dynamic_index_in_dim(gathered, dev, axis=1, keepdims=False)
    per = nmod // N_CHIPS
    mods = [mine[:, i * per : (i + 1) * per].reshape(N_MOD, 1, d) for i in range(depth)]
    kv_shift, kv_scale = mine[:, depth * per :].reshape(2, 1, d)

    mixer_names = ("pool_w", "w_dkv", "w_uk", "w_uv", "w_kr", "w_dq", "w_uq", "w_o")
    ffn_names = ("ffn_gate", "ffn_up", "ffn_down")
    geoms = {
        "pool_w": _Geom((n_a * len(POOL_WINDOWS), pool_c, pool_c), 1),
        "w_dkv": _Geom((1, d, kvr), 1),
        "w_uk": _Geom((1, kvr, heads * NOPE_DIM), 2),
        "w_uv": _Geom((1, kvr, heads * V_DIM), 2),
        "w_kr": _Geom((1, d, ROPE_DIM), 1),
        "w_dq": _Geom((n_b, d, qr), 1),
        "w_uq": _Geom((n_b, qr, heads * (NOPE_DIM + ROPE_DIM)), 2),
        "w_o": _Geom((n_b, d, d), 1),
        "ffn_gate": _Geom((1, d, ffn), 2),
        "ffn_up": _Geom((1, d, ffn), 2),
        "ffn_down": _Geom((1, ffn, d), 1),
    }
    mixer_geoms = [geoms[n] for n in mixer_names]
    ffn_geoms = [geoms[n] for n in ffn_names]
    chip_arr, core_arr, place_arr = chip.reshape(1), mc.reshape(1), jnp.stack([chip, mc])
    placed = [_place_shard(wts[n].reshape(geoms[n].shard), geoms[n], chip_arr, name="place_" + n) for n in mixer_names]
    placed_ffn = [[_place_shard(wts[n], geoms[n], chip_arr, layer=i, name="place_" + n) for n in ffn_names] for i in range(depth)]
    first = _run(_gather_plan(placed + placed_ffn[0], mixer_geoms + ffn_geoms), name="gather_first")
    full = dict(zip(mixer_names, first))
    whole_k = dict(tm=512, tn=1024, tk=max(s, ffn))
    ffn_w = [None] * depth
    ffn_w[0] = first[len(mixer_names) :]

    pool_w = full["pool_w"].reshape(n_a, len(POOL_WINDOWS), pool_c, pool_c)
    w_uq = full["w_uq"].reshape(n_b, qr, heads, NOPE_DIM + ROPE_DIM)
    w_q = jnp.pad(w_uq, ((0, 0), (0, 0), (0, 0), (0, HEAD_PAD - NOPE_DIM - ROPE_DIM))).reshape(n_b, qr, heads * HEAD_PAD)
    w_ukv = jnp.stack([full["w_uk"].reshape(kvr, heads, NOPE_DIM), full["w_uv"].reshape(kvr, heads, V_DIM)], axis=2).reshape(kvr, heads * HEAD_PAD)
    w_dkvkr = jnp.concatenate([full["w_dkv"][0], full["w_kr"][0], jnp.zeros((d, LANES - ROPE_DIM), BF16)], axis=1)
    w_dq, w_o = full["w_dq"], full["w_o"]

    norm_mix, norm_ffn = wts["norm_mix"], wts["norm_ffn"]
    saved = []
    cur, pending = xs, None
    kv_side = None
    for i in range(depth):
        shift_m, scale_m, gate_m, shift_f, scale_f, gate_f = mods[i]
        h1_dtype = F32 if i < n_a else BF16
        if pending is None:
            x0 = cur
            h1 = _norm_fwd(x0, _vec(norm_mix[i]), scale=scale_m, shift=shift_m, out_dtype=h1_dtype, name="norm_mix_first")
        else:
            x0, h1 = _norm_fwd(cur, _vec(norm_mix[i]), scale=scale_m, shift=shift_m, y=pending[0], gate=pending[1], out_dtype=h1_dtype, name="norm_mix")
        lay = {"x0": x0, "h1": h1}
        if i == n_a:
            h_kv = _norm_fwd(x0, _vec(wts["kv_in_norm"]), scale=kv_scale, shift=kv_shift, name="norm_kv_in")
            pre = _mm(h_kv, w_dkvkr, name="kv_down", tn=kvr + LANES)
            ckv_pre, kr_pre = pre[:, :kvr], pre[:, kvr:]
            ckv = _norm_fwd(ckv_pre, _vec(wts["kv_norm"]), name="norm_kv")
            kv = _mm(ckv, w_ukv, out_dtype=BF16, name="kv_up")
            keys, keys_t, v_t = _build_keys(kv, kr_pre, cos_t, sin_fwd, name="build_keys")
            kv_side = {"h_kv": h_kv, "ckv_pre": ckv_pre, "ckv": ckv, "kv": kv, "keys": keys, "keys_t": keys_t, "v_t": v_t, "x0": x0}
        if i < n_a:
            pooled = _pool(h1, transpose=False, out_dtype=BF16, name="pool_fwd")
            y_pre = _gmm(pooled, pool_w[i], mode="nn", out_dtype=F32, name="pool_mix")
            gate_eff = gate_m * _vec(pool_scale[i])
            lay.update(pooled=pooled)
        else:
            l = i - n_a
            cq_pre = _mm(h1, w_dq[l], name="q_down")
            cq = _norm_fwd(cq_pre, _vec(wts["q_norm"][l]), name="norm_q")
            q = _rope_heads(_mm(cq, w_q[l], name="q_up"), cos_t, sin_fwd, out_dtype=BF16, name="rope_q")
            if i + 1 < depth:
                (o, lse), ffn_w[i + 1] = _attn_fwd(q, kv_side["keys"], kv_side["v_t"], hosted=_gather_plan(placed_ffn[i + 1], ffn_geoms), name="attn_fwd")
            else:
                (o, lse), _ = _attn_fwd(q, kv_side["keys"], kv_side["v_t"], name="attn_fwd_last")
            y_pre = _mm(o, w_o[l], name="attn_out")
            gate_eff = gate_m
            lay.update(cq_pre=cq_pre, cq=cq, q=q, o=o, lse=lse)
        x1, h2 = _norm_fwd(x0, _vec(norm_ffn[i]), scale=scale_f, shift=shift_f, y=y_pre, gate=gate_eff, name="norm_ffn")
        w_gate, w_up, w_down = ffn_w[i]
        if i + 1 < depth and ffn_w[i + 1] is None:
            (a, b, z), next_in = _ffn_in(h2, w_gate, w_up, 0, hosted=_gather_plan(placed_ffn[i + 1][:2], ffn_geoms[:2]), name="ffn_in")
            f, next_down = _mm(z, w_down, b_idx=0, hosted=_gather_plan(placed_ffn[i + 1][2:], ffn_geoms[2:]), name="ffn_down", **whole_k)
            ffn_w[i + 1] = next_in + next_down
        else:
            (a, b, z), _ = _ffn_in(h2, w_gate, w_up, 0, name="ffn_in_last")
            f = _mm(z, w_down, b_idx=0, name="ffn_down_last", **whole_k)
        lay.update(y_pre=y_pre, gate_eff=gate_eff, x1=x1, h2=h2, a=a, b=b, z=z, f=f)
        saved.append(lay)
        cur, pending = x1, (f, gate_f)

    dx, final_stats, loss_tile = _loss_bwd(cur, pending[0], pending[1], _vec(wts["final_norm"]), tgt, name="loss")
    loss = lax.psum(loss_tile[0, 0], ("x", "y", "c"))

    ffn_both = [None] * len(ffn_names)
    in_flight = None
    attn_names = tuple(n for n in mixer_names if n != "pool_w")
    grad_full = {}
    attn_parts = attn_slots = None

    def sum_chips(layer, parts, slots):
        for w, n in enumerate(ffn_names):
            ffn_both[w] = _chip_sum(parts[w], slots[w], ffn_geoms[w], place_arr, stack=(depth, layer, ffn_both[w]), name="chip_sum_" + n)

    g_wo, g_wq, g_wdq = [None] * n_b, [None] * n_b, [None] * n_b
    g_pool = [None] * n_a
    dmods = [None] * depth
    g_norm_mix, g_norm_ffn, g_q_norm, g_pool_scale = [None] * depth, [None] * depth, [None] * n_b, [None] * n_a
    dk_layers, dv_layers = [None] * n_b, [None] * n_b
    for i in reversed(range(depth)):
        lay = saved[i]
        shift_m, scale_m, gate_m, shift_f, scale_f, gate_f = mods[i]
        df, sums_gf = _gate_bwd(dx, lay["f"], gate_f, name="gate_bwd")
        w_gate, w_up, w_down = ffn_w[i]
        g_down = _mm(lay["z"], df, ta=True, out_dtype=BF16, name="ffn_down_dw", **whole_k)
        pair_down = _pair_plan([g_down[None]], ffn_geoms[2:])
        carry_attn = attn_parts is not None and attn_slots is None
        riding = [] if in_flight is None else in_flight
        plans = [pair_down] + [_chips_plan(part) for part in (riding[:1], attn_parts if carry_attn else None) if part]
        (da, db), got_down = _ffn_down_bwd(df, w_down, 0, lay["a"], lay["b"], hosted=_chips_plan(riding[2:]) if riding else None, name="ffn_down_bwd")
        (dh2,), got = _ffn_in_dx(da, db, w_gate, w_up, 0, hosted=_merge(plans), name="ffn_in_dx")
        their_down, got_gate, got = got[:1], got[1 : 1 + len(riding[:1])], got[1 + len(riding[:1]) :]
        if carry_attn:
            attn_slots = got
        (g_gate, g_up), got_up = _ffn_in_dw(lay["h2"], da, db, hosted=_chips_plan(riding[1:2]) if riding else None, name="ffn_in_dw")
        if riding:
            sum_chips(i + 1, riding, got_gate + got_up + got_down)
        (dx1, sums_f), their_in = _norm_bwd(
            lay["x1"], _vec(norm_ffn[i]), dh2, scale=scale_f, resid=dx, hosted=_pair_plan([g_gate[None], g_up[None]], ffn_geoms[:2]), name="norm_ffn_bwd"
        )
        ffn_grads = [g_gate[None], g_up[None], g_down[None]]
        in_flight = [
            _pair_add(g, th, geom, core_arr, name="pair_add_" + n) for g, th, geom, n in zip(ffn_grads, their_in + their_down, ffn_geoms, ffn_names)
        ]
        dyp, sums_gm = _gate_bwd(dx1, lay["y_pre"], lay["gate_eff"], name="gate_bwd")
        if i < n_a:
            g_pool[i] = _gmm(lay["pooled"], dyp, mode="tn", out_dtype=BF16, name="pool_mix_dw")
            dd = _gmm(dyp, pool_w[i], mode="nt", out_dtype=F32, name="pool_mix_dx")
            dh1 = _pool(dd, transpose=True, out_dtype=F32, name="pool_bwd")
            dgate_m = sums_gm[0] * pool_scale[i]
            g_pool_scale[i] = sums_gm[0] * gate_m[0]
        else:
            l = i - n_a
            do = _mm(dyp, w_o[l], tb=True, out_dtype=BF16, name="attn_out_dx")
            g_wo[l] = _mm(lay["o"], dyp, ta=True, out_dtype=BF16, name="attn_out_dw")
            (dq_t, dk_layers[l], dv_layers[l]), got = _attn_bwd(
                lay["q"], kv_side["keys"], kv_side["keys_t"], kv_side["kv"], lay["o"], do, lay["lse"], hosted=_chips_plan(in_flight), name="attn_bwd"
            )
            sum_chips(i, in_flight, got)
            in_flight = None
            dq_pre = _rope_back(dq_t, cos_t, sin_bwd, name="rope_q_bwd")
            dcq = _mm(dq_pre, w_q[l], tb=True, name="q_up_dx")
            g_wq[l] = _mm(lay["cq"], dq_pre, ta=True, out_dtype=BF16, name="q_up_dw")
            dcq_pre, sums_q = _norm_bwd(lay["cq_pre"], _vec(wts["q_norm"][l]), dcq, name="norm_q_bwd")
            g_q_norm[l] = sums_q[2]
            dh1 = _mm(dcq_pre, w_dq[l], tb=True, name="q_down_dx")
            g_wdq[l] = _mm(lay["h1"], dcq_pre, ta=True, out_dtype=BF16, name="q_down_dw")
            dgate_m = sums_gm[0]
        dx, sums_m = _norm_bwd(lay["x0"], _vec(norm_mix[i]), dh1, scale=scale_m, resid=dx1, name="norm_mix_bwd")
        if i == n_a:
            dkv, dkr_pre = _keys_bwd(dk_layers[0], dk_layers[1], dv_layers[0], dv_layers[1], cos_t, sin_bwd, name="keys_bwd")
            dckv = _mm(dkv, w_ukv, tb=True, name="kv_up_dx")
            g_ukv = _mm(kv_side["ckv"], dkv, ta=True, out_dtype=BF16, name="kv_up_dw")
            dckv_pre, sums_kvn = _norm_bwd(kv_side["ckv_pre"], _vec(wts["kv_norm"]), dckv, name="norm_kv_bwd")
            dpre = jnp.concatenate([dckv_pre, dkr_pre], axis=1)
            dh_kv = _mm(dpre, w_dkvkr, tb=True, name="kv_down_dx")
            g_dkvkr = _mm(kv_side["h_kv"], dpre, ta=True, out_dtype=BF16, name="kv_down_dw", tn=kvr + LANES)
            dx, sums_kv = _norm_bwd(lay["x0"], _vec(wts["kv_in_norm"]), dh_kv, scale=kv_scale, resid=dx, name="norm_kv_in_bwd")
            g_ukv = g_ukv.reshape(kvr, heads, 2, NOPE_DIM)
            grad_full.update(
                w_dkv=g_dkvkr[None, :, :kvr],
                w_uk=g_ukv[:, :, 0].reshape(1, kvr, heads * NOPE_DIM),
                w_uv=g_ukv[:, :, 1].reshape(1, kvr, heads * V_DIM),
                w_kr=g_dkvkr[None, :, kvr : kvr + ROPE_DIM],
                w_dq=jnp.stack(g_wdq),
                w_uq=jnp.stack(g_wq).reshape(n_b, qr, heads, HEAD_PAD)[..., : NOPE_DIM + ROPE_DIM].reshape(geoms["w_uq"].shape3),
                w_o=jnp.stack(g_wo),
            )
            attn_theirs = _run(_pair_plan([grad_full[n] for n in attn_names], [geoms[n] for n in attn_names]), name="reduce_pair_attn")
            attn_parts = [_pair_add(grad_full[n], th, geoms[n], core_arr, name="pair_add_" + n) for n, th in zip(attn_names, attn_theirs)]
        dmods[i] = jnp.concatenate([sums_m[0], sums_m[1], dgate_m, sums_f[0], sums_f[1], sums_gf[0]])
        g_norm_mix[i], g_norm_ffn[i] = sums_m[2], sums_f[2]
    grad_x = dx[None]
    grad_full["pool_w"] = jnp.stack(g_pool).reshape(geoms["pool_w"].shape3)

    small_grads = {
        "mod_b": jnp.concatenate(dmods),
        "kv_mod_b": jnp.concatenate([sums_kv[0], sums_kv[1]]),
        "norm_mix": jnp.concatenate(g_norm_mix),
        "norm_ffn": jnp.concatenate(g_norm_ffn),
        "kv_in_norm": sums_kv[2],
        "kv_norm": sums_kvn[2],
        "q_norm": jnp.concatenate(g_q_norm),
        "final_norm": final_stats[0],
    }
    packed = jnp.concatenate([small_grads[n] for n in SMALL] + g_pool_scale)
    small_rows = sum(wts[n].size for n in SMALL) // LANES
    every = _allgather8(_pad_rows(_rows(packed)), name="gather_small_grads")
    summed = _sum_devices(every, name="sum_small_grads")

    mod_rows = depth * nmod // LANES
    dm_all = every[:, :mod_rows].reshape(N_DEV, depth, nmod)
    dm = lax.dynamic_slice_in_dim(dm_all, chip * per, per, axis=2).transpose(1, 0, 2)
    dm = jnp.pad(dm, ((0, 0), (0, 16 - N_DEV), (0, 0)))
    dkvm_all = every[:, mod_rows : mod_rows + nkv // LANES].reshape(N_DEV, nkv)
    dkvm = jnp.pad(lax.dynamic_slice_in_dim(dkvm_all, chip * (nkv // N_CHIPS), nkv // N_CHIPS, axis=1), ((0, 16 - N_DEV), (0, 0)))[None]
    results = {}
    results["mod_w"] = _tp_adamw(sc16, dm, wts["mod_w"], mom["mod_w"], var["mod_w"], name="mod_w_update")
    results["kv_mod_w"] = [
        r[0] for r in _tp_adamw(sc16, dkvm, wts["kv_mod_w"][None], mom["kv_mod_w"][None], var["kv_mod_w"][None], name="kv_mod_w_update")
    ]

    ps_grad = lax.dynamic_slice_in_dim(summed[small_rows : small_rows + n_a * d // LANES].reshape(n_a, d), chip * (d // N_CHIPS), d // N_CHIPS, axis=1)
    small_names = SMALL + ("pool_scale",)

    def pack_small(tree):
        return _pad_rows(jnp.concatenate([_rows(tree[n]) for n in small_names]))

    g_small = _pad_rows(jnp.concatenate([summed[:small_rows], _rows(ps_grad)]))
    small_out = _adamw(pack_small(wts), g_small, pack_small(mom), pack_small(var), name="small_update")
    row = 0
    for n in small_names:
        nrow = wts[n].size // LANES
        results[n] = [r[row : row + nrow].reshape(wts[n].shape) for r in small_out]
        row += nrow

    assert attn_slots is not None
    theirs = _run(_pair_plan([grad_full["pool_w"]], [geoms["pool_w"]]), name="reduce_pair")
    pool_part = _pair_add(grad_full["pool_w"], theirs[0], geoms["pool_w"], core_arr, name="pair_add_pool_w")
    got = _run(_chips_plan([pool_part] + in_flight), name="reduce_chips")
    parts = dict(zip(attn_names, attn_parts), pool_w=pool_part)
    slots = dict(zip(attn_names, attn_slots), pool_w=got[0])
    boths = [_chip_sum(parts[n], slots[n], geoms[n], place_arr, name="chip_sum_" + n) for n in mixer_names]
    sum_chips(0, in_flight, got[1:])
    prefixes = [[()]] * len(mixer_names) + [[(layer,) for layer in range(depth)]] * len(ffn_names)
    joined = _run(_join_plan(boths + ffn_both, prefixes), name="join_pair")
    for n, both in zip(mixer_names + ffn_names, joined):
        cs = geoms[n].cs
        out = _adamw(wts[n].reshape(-1, cs), both.reshape(-1, cs), mom[n].reshape(-1, cs), var[n].reshape(-1, cs), name="update_" + n)
        results[n] = [r.reshape(wts[n].shape) for r in out]

    outs = [loss, grad_x]
    for k in range(4):
        outs += [results[n][k] for n in WEIGHTS]
    return tuple(outs)


def kernel(x, c, positions, mod_w, mod_b, norm_mix, norm_ffn, pool_w, pool_scale, kv_mod_w, kv_mod_b, kv_in_norm, w_dkv, kv_norm, w_uk, w_uv, w_kr, w_dq, q_norm, w_uq, w_o, ffn_gate, ffn_up, ffn_down, final_norm, loss_target, m_mod_w, m_mod_b, m_norm_mix, m_norm_ffn, m_pool_w, m_pool_scale, m_kv_mod_w, m_kv_mod_b, m_kv_in_norm, m_w_dkv, m_kv_norm, m_w_uk, m_w_uv, m_w_kr, m_w_dq, m_q_norm, m_w_uq, m_w_o, m_ffn_gate, m_ffn_up, m_ffn_down, m_final_norm, v_mod_w, v_mod_b, v_norm_mix, v_norm_ffn, v_pool_w, v_pool_scale, v_kv_mod_w, v_kv_mod_b, v_kv_in_norm, v_w_dkv, v_kv_norm, v_w_uk, v_w_uv, v_w_kr, v_w_dq, v_q_norm, v_w_uq, v_w_o, v_ffn_gate, v_ffn_up, v_ffn_down, v_final_norm):
    given = dict(locals())
    wts = {n: given[n] for n in WEIGHTS}
    mom = {n: given["m_" + n] for n in WEIGHTS}
    var = {n: given["v_" + n] for n in WEIGHTS}
    return _step(x, c, positions, loss_target, wts, mom, var)
```

```python
import functools

import jax
import jax.numpy as jnp
from jax import lax
from jax.experimental import pallas as pl
from jax.experimental.pallas import tpu as pltpu

F32 = jnp.float32
BF16 = jnp.bfloat16
MESH = pl.DeviceIdType.MESH
ANY = pl.BlockSpec(memory_space=pl.ANY)

NORM_EPS = 1e-6
POOL_WINDOWS = (2, 4, 8, 16)
NOPE_DIM = 128
ROPE_DIM = 64
V_DIM = 128
HEAD_PAD = 256
SM_SCALE = (NOPE_DIM + ROPE_DIM) ** -0.5
ROPE_THETA = 10000.0
N_MOD = 6
ADAM_LR, ADAM_B1, ADAM_B2, ADAM_EPS, ADAM_WD, ADAM_STEP = 0.001, 0.9, 0.999, 1e-08, 0.01, 10
N_CHIPS = 4
N_DEV = 8
LANES = 128
HALO = 128
VMEM_LIMIT = 48 * 1024 * 1024


def _tile(dim, pref, align):
    if dim <= pref:
        return dim
    t = (pref // align) * align
    while t >= align:
        if dim % t == 0:
            return t
        t -= align
    return dim


def _params(*sem):
    return pltpu.CompilerParams(dimension_semantics=sem, vmem_limit_bytes=VMEM_LIMIT)


class _Hosted:
    def __init__(self, args, out_shapes, aliases, sem_shapes, start, finish):
        self.args, self.out_shapes, self.aliases, self.sem_shapes = list(args), list(out_shapes), dict(aliases), list(sem_shapes)
        self.start, self.finish = start, finish


def _call(body, *, name, grid, in_specs, out_specs, out_shape, args, sem, scratch_shapes=(), hosted=None):
    n_in, n_out, n_scr = len(args), len(out_shape), len(scratch_shapes)
    if hosted is None:
        outs = pl.pallas_call(
            body, name=name, grid=grid, in_specs=list(in_specs), out_specs=list(out_specs), out_shape=list(out_shape),
            scratch_shapes=list(scratch_shapes), compiler_params=_params(*sem),
        )(*args)
        return list(outs), []
    n_hin, n_hout = len(hosted.args), len(hosted.out_shapes)

    def carrying(*refs):
        own_in, their_in = refs[:n_in], refs[n_in : n_in + n_hin]
        refs = refs[n_in + n_hin :]
        own_out, their_out = refs[:n_out], refs[n_out : n_out + n_hout]
        refs = refs[n_out + n_hout :]
        own_scratch, sems = refs[:n_scr], refs[n_scr:]
        ids = [pl.program_id(axis) for axis in range(len(grid))]
        first = functools.reduce(jnp.logical_and, [i == 0 for i in ids])
        last = functools.reduce(jnp.logical_and, [i == size - 1 for i, size in zip(ids, grid)])

        @pl.when(first)
        def _():
            hosted.start(their_in, their_out, sems)

        body(*own_in, *own_out, *own_scratch)

        @pl.when(last)
        def _():
            hosted.finish(their_in, their_out, sems)

    outs = pl.pallas_call(
        carrying,
        name=name,
        grid=grid,
        in_specs=list(in_specs) + [ANY] * n_hin,
        out_specs=list(out_specs) + [ANY] * n_hout,
        out_shape=list(out_shape) + hosted.out_shapes,
        input_output_aliases={n_in + i: n_out + o for i, o in hosted.aliases.items()},
        scratch_shapes=list(scratch_shapes) + hosted.sem_shapes,
        compiler_params=_params(*["arbitrary"] * len(grid)),
    )(*args, *hosted.args)
    return list(outs[:n_out]), list(outs[n_out:])


def _merge(plans):
    if len(plans) == 1:
        return plans[0]
    args, out_shapes, aliases, sem_shapes, spans = [], [], {}, [], []
    for p in plans:
        spans.append((len(args), len(out_shapes), len(sem_shapes)))
        aliases.update({len(args) + i: len(out_shapes) + o for i, o in p.aliases.items()})
        args, out_shapes, sem_shapes = args + p.args, out_shapes + p.out_shapes, sem_shapes + p.sem_shapes

    def each(method, ins, outs, sems):
        for p, (a0, o0, s0) in zip(plans, spans):
            getattr(p, method)(ins[a0 : a0 + len(p.args)], outs[o0 : o0 + len(p.out_shapes)], sems[s0 : s0 + len(p.sem_shapes)])

    return _Hosted(args, out_shapes, aliases, sem_shapes, functools.partial(each, "start"), functools.partial(each, "finish"))


def _run(plan, *, name):
    n_in, n_out = len(plan.args), len(plan.out_shapes)

    def body(*refs):
        ins, outs, sems = refs[:n_in], refs[n_in : n_in + n_out], refs[n_in + n_out :]
        plan.start(ins, outs, sems)
        plan.finish(ins, outs, sems)

    return pl.pallas_call(
        body, name=name, in_specs=[ANY] * n_in, out_specs=[ANY] * n_out, out_shape=plan.out_shapes,
        input_output_aliases=plan.aliases, scratch_shapes=plan.sem_shapes,
    )(*plan.args)


def _mm(a, b, *, name, ta=False, tb=False, out_dtype=F32, b_idx=None, hosted=None, tm=1024, tn=1024, tk=None):
    m, k = (a.shape[1], a.shape[0]) if ta else a.shape
    b2 = b.shape if b_idx is None else b.shape[1:]
    kb, n = (b2[1], b2[0]) if tb else b2
    assert k == kb, (a.shape, b.shape, ta, tb)
    tm = _tile(m, tm, LANES)
    tn = _tile(n, tn, LANES)
    tk = _tile(k, 2048 if tk is None else tk, LANES)
    nk = k // tk
    dims = (((0 if ta else 1,), (1 if tb else 0,)), ((), ()))

    def body(a_ref, b_ref, o_ref, *acc):
        part = lax.dot_general(a_ref[...].astype(BF16), b_ref[...].astype(BF16), dims, preferred_element_type=F32)
        if nk == 1:
            o_ref[...] = part.astype(o_ref.dtype)
        else:
            acc_ref = acc[0]
            step = pl.program_id(2)

            @pl.when(step == 0)
            def _():
                acc_ref[...] = part

            @pl.when(step > 0)
            def _():
                acc_ref[...] += part

            @pl.when(step == nk - 1)
            def _():
                o_ref[...] = acc_ref[...].astype(o_ref.dtype)

    a_spec = pl.BlockSpec((tk, tm), lambda i, j, s: (s, i)) if ta else pl.BlockSpec((tm, tk), lambda i, j, s: (i, s))
    if b_idx is None:
        b_spec = pl.BlockSpec((tn, tk), lambda i, j, s: (j, s)) if tb else pl.BlockSpec((tk, tn), lambda i, j, s: (s, j))
    elif tb:
        b_spec = pl.BlockSpec((None, tn, tk), lambda i, j, s: (b_idx, j, s))
    else:
        b_spec = pl.BlockSpec((None, tk, tn), lambda i, j, s: (b_idx, s, j))
    outs, carried = _call(
        body,
        name=name,
        grid=(m // tm, n // tn, nk),
        in_specs=[a_spec, b_spec],
        out_specs=[pl.BlockSpec((tm, tn), lambda i, j, s: (i, j))],
        out_shape=[jax.ShapeDtypeStruct((m, n), out_dtype)],
        args=[a, b],
        sem=("parallel", "parallel", "arbitrary"),
        scratch_shapes=[pltpu.VMEM((tm, tn), F32)] if nk > 1 else [],
        hosted=hosted,
    )
    return outs[0] if hosted is None else (outs[0], carried)


def _tp_fwd(sc16, w, bias, *, name):
    nl, d, n = w.shape
    tn = _tile(n, 512, LANES)

    def body(sc_ref, w_ref, b_ref, o_ref):
        o_ref[0] = jnp.dot(sc_ref[...].astype(BF16), w_ref[0].astype(BF16), preferred_element_type=F32) + b_ref[0]

    return pl.pallas_call(
        body,
        name=name,
        grid=(nl, n // tn),
        in_specs=[
            pl.BlockSpec((16, d), lambda l, j: (0, 0)),
            pl.BlockSpec((1, d, tn), lambda l, j: (l, 0, j)),
            pl.BlockSpec((1, 1, tn), lambda l, j: (l, 0, j)),
        ],
        out_specs=pl.BlockSpec((1, 16, tn), lambda l, j: (l, 0, j)),
        out_shape=jax.ShapeDtypeStruct((nl, 16, n), F32),
        compiler_params=_params("parallel", "parallel"),
    )(sc16, w, bias)


_NT = (((1,), (1,)), ((), ()))
_TN = (((0,), (0,)), ((), ()))


def _silu_parts(a):
    sig = jax.nn.sigmoid(a)
    return a * sig, sig * (1.0 + a * (1.0 - sig))


def _ffn_in(h, w_gate, w_up, layer, *, name, hosted=None):
    s, d = h.shape
    f = w_gate.shape[2]
    tm, tn = _tile(s, 1024, LANES), _tile(f, 512, LANES)

    def body(h_ref, g_ref, u_ref, ga_ref, gb_ref, z_ref):
        hv = h_ref[...]
        a = jnp.dot(hv, g_ref[...], preferred_element_type=F32)
        b = jnp.dot(hv, u_ref[...], preferred_element_type=F32)
        silu, dsilu = _silu_parts(a)
        ga_ref[...] = (b * dsilu).astype(ga_ref.dtype)
        gb_ref[...] = silu.astype(gb_ref.dtype)
        z_ref[...] = (silu * b).astype(z_ref.dtype)

    w_spec = pl.BlockSpec((None, d, tn), lambda i, j: (layer, 0, j))
    out = pl.BlockSpec((tm, tn), lambda i, j: (i, j))
    return _call(
        body,
        name=name,
        grid=(s // tm, f // tn),
        in_specs=[pl.BlockSpec((tm, d), lambda i, j: (i, 0)), w_spec, w_spec],
        out_specs=[out] * 3,
        out_shape=[jax.ShapeDtypeStruct((s, f), BF16)] * 3,
        args=[h, w_gate, w_up],
        sem=("parallel", "parallel"),
        hosted=hosted,
    )


def _ffn_down_bwd(df, w_down, layer, dz_da, dz_db, *, name, hosted=None):
    s, d = df.shape
    f = w_down.shape[1]
    tm, tn = _tile(s, 1024, LANES), _tile(f, 512, LANES)

    def body(df_ref, w_ref, ga_ref, gb_ref, da_ref, db_ref):
        dz = lax.dot_general(df_ref[...], w_ref[...], _NT, preferred_element_type=F32)
        da_ref[...] = (dz * ga_ref[...].astype(F32)).astype(da_ref.dtype)
        db_ref[...] = (dz * gb_ref[...].astype(F32)).astype(db_ref.dtype)

    blk = pl.BlockSpec((tm, tn), lambda i, j: (i, j))
    return _call(
        body,
        name=name,
        grid=(s // tm, f // tn),
        in_specs=[pl.BlockSpec((tm, d), lambda i, j: (i, 0)), pl.BlockSpec((None, tn, d), lambda i, j: (layer, j, 0)), blk, blk],
        out_specs=[blk, blk],
        out_shape=[jax.ShapeDtypeStruct((s, f), BF16)] * 2,
        args=[df, w_down, dz_da, dz_db],
        sem=("parallel", "parallel"),
        hosted=hosted,
    )


def _ffn_in_dx(da, db, w_gate, w_up, layer, *, name, hosted=None):
    s, f = da.shape
    d = w_gate.shape[1]
    tm, tn, tk = _tile(s, 512, LANES), _tile(d, 1024, LANES), _tile(f, 3072, LANES)
    nk = f // tk

    def body(da_ref, db_ref, g_ref, u_ref, o_ref, acc_ref):
        part = lax.dot_general(da_ref[...], g_ref[...], _NT, preferred_element_type=F32)
        part = part + lax.dot_general(db_ref[...], u_ref[...], _NT, preferred_element_type=F32)
        step = pl.program_id(2)

        @pl.when(step == 0)
        def _():
            acc_ref[...] = part

        @pl.when(step > 0)
        def _():
            acc_ref[...] += part

        @pl.when(step == nk - 1)
        def _():
            o_ref[...] = acc_ref[...]

    x_spec = pl.BlockSpec((tm, tk), lambda i, j, k: (i, k))
    w_spec = pl.BlockSpec((None, tn, tk), lambda i, j, k: (layer, j, k))
    return _call(
        body,
        name=name,
        grid=(s // tm, d // tn, nk),
        in_specs=[x_spec, x_spec, w_spec, w_spec],
        out_specs=[pl.BlockSpec((tm, tn), lambda i, j, k: (i, j))],
        out_shape=[jax.ShapeDtypeStruct((s, d), F32)],
        args=[da, db, w_gate, w_up],
        sem=("parallel", "parallel", "arbitrary"),
        scratch_shapes=[pltpu.VMEM((tm, tn), F32)],
        hosted=hosted,
    )


def _ffn_in_dw(h, da, db, *, name, hosted=None):
    s, d = h.shape
    f = da.shape[1]
    tm, tn, tk = _tile(d, 1024, LANES), _tile(f, 512, LANES), _tile(s, 4096, LANES)
    nk = s // tk

    def body(h_ref, da_ref, db_ref, g_ref, u_ref, *acc):
        hv = h_ref[...]
        pg = lax.dot_general(hv, da_ref[...], _TN, preferred_element_type=F32)
        pu = lax.dot_general(hv, db_ref[...], _TN, preferred_element_type=F32)
        if nk == 1:
            g_ref[...] = pg.astype(g_ref.dtype)
            u_ref[...] = pu.astype(u_ref.dtype)
            return
        g_acc, u_acc = acc
        step = pl.program_id(2)

        @pl.when(step == 0)
        def _():
            g_acc[...] = pg
            u_acc[...] = pu

        @pl.when(step > 0)
        def _():
            g_acc[...] += pg
            u_acc[...] += pu

        @pl.when(step == nk - 1)
        def _():
            g_ref[...] = g_acc[...].astype(g_ref.dtype)
            u_ref[...] = u_acc[...].astype(u_ref.dtype)

    y_spec = pl.BlockSpec((tk, tn), lambda i, j, k: (k, j))
    out = pl.BlockSpec((tm, tn), lambda i, j, k: (i, j))
    return _call(
        body,
        name=name,
        grid=(d // tm, f // tn, nk),
        in_specs=[pl.BlockSpec((tk, tm), lambda i, j, k: (k, i)), y_spec, y_spec],
        out_specs=[out, out],
        out_shape=[jax.ShapeDtypeStruct((d, f), BF16)] * 2,
        args=[h, da, db],
        sem=("parallel", "parallel", "arbitrary"),
        scratch_shapes=[pltpu.VMEM((tm, tn), F32)] * 2 if nk > 1 else [],
        hosted=hosted,
    )


def _gmm(a, w, *, name, mode, out_dtype):
    s = a.shape[0]
    g = len(POOL_WINDOWS)
    c = a.shape[1] // g
    tr = _tile(s, 1024, LANES)
    n_row = s // tr

    if mode == "tn":

        def body(a_ref, b_ref, o_ref, acc_ref):
            part = lax.dot_general(a_ref[...].astype(BF16), b_ref[...].astype(BF16), (((0,), (0,)), ((), ())), preferred_element_type=F32)

            @pl.when(pl.program_id(1) == 0)
            def _():
                acc_ref[...] = part

            @pl.when(pl.program_id(1) > 0)
            def _():
                acc_ref[...] += part

            @pl.when(pl.program_id(1) == n_row - 1)
            def _():
                o_ref[0] = acc_ref[...].astype(o_ref.dtype)

        return pl.pallas_call(
            body,
            name=name,
            grid=(g, n_row),
            in_specs=[pl.BlockSpec((tr, c), lambda gi, i: (i, gi)), pl.BlockSpec((tr, c), lambda gi, i: (i, gi))],
            out_specs=pl.BlockSpec((1, c, c), lambda gi, i: (gi, 0, 0)),
            out_shape=jax.ShapeDtypeStruct((g, c, c), out_dtype),
            scratch_shapes=[pltpu.VMEM((c, c), F32)],
            compiler_params=_params("parallel", "arbitrary"),
        )(a, w)

    dims = (((1,), (0 if mode == "nn" else 1,)), ((), ()))

    def body(a_ref, w_ref, o_ref):
        o_ref[...] = lax.dot_general(a_ref[...].astype(BF16), w_ref[0].astype(BF16), dims, preferred_element_type=F32).astype(o_ref.dtype)

    return pl.pallas_call(
        body,
        name=name,
        grid=(g, n_row),
        in_specs=[pl.BlockSpec((tr, c), lambda gi, i: (i, gi)), pl.BlockSpec((1, c, c), lambda gi, i: (gi, 0, 0))],
        out_specs=pl.BlockSpec((tr, c), lambda gi, i: (i, gi)),
        out_shape=jax.ShapeDtypeStruct((s, g * c), out_dtype),
        compiler_params=_params("parallel", "parallel"),
    )(a, w)


def _row_tile(s, d):
    return _tile(s, max(8, (1 << 19) // d), 8)


def _norm_fwd(x, g, *, name, scale=None, shift=None, y=None, gate=None, out_dtype=BF16):
    s, d = x.shape
    tr = _row_tile(s, d)
    has_res, has_mod = y is not None, scale is not None

    def body(*refs):
        refs = list(refs)
        x_ref = refs.pop(0)
        xv = x_ref[...]
        if has_res:
            y_ref, gate_ref = refs.pop(0), refs.pop(0)
            xv = xv + gate_ref[...] * y_ref[...]
        g_ref = refs.pop(0)
        if has_mod:
            scale_ref, shift_ref = refs.pop(0), refs.pop(0)
        if has_res:
            refs.pop(0)[...] = xv
        h = xv * lax.rsqrt(jnp.mean(xv * xv, axis=-1, keepdims=True) + NORM_EPS)
        h = h * g_ref[...]
        if has_mod:
            h = h * (1.0 + scale_ref[...]) + shift_ref[...]
        refs.pop(0)[...] = h.astype(out_dtype)

    row = pl.BlockSpec((tr, d), lambda i: (i, 0))
    vec = pl.BlockSpec((1, d), lambda i: (0, 0))
    args, in_specs = [x], [row]
    if has_res:
        args += [y, gate]
        in_specs += [row, vec]
    args.append(g)
    in_specs.append(vec)
    if has_mod:
        args += [scale, shift]
        in_specs += [vec, vec]
    out_shape, out_specs = [], []
    if has_res:
        out_shape.append(jax.ShapeDtypeStruct((s, d), F32))
        out_specs.append(row)
    out_shape.append(jax.ShapeDtypeStruct((s, d), out_dtype))
    out_specs.append(row)
    res = pl.pallas_call(
        body, name=name, grid=(s // tr,), in_specs=in_specs, out_specs=out_specs, out_shape=out_shape, compiler_params=_params("parallel")
    )(*args)
    return (res[0], res[1]) if has_res else res[0]


def _norm_bwd(x, g, dh, *, name, scale=None, resid=None, hosted=None):
    s, d = x.shape
    tr = _row_tile(s, d)
    has_mod, has_res = scale is not None, resid is not None

    def body(*refs):
        refs = list(refs)
        x_ref, g_ref, dh_ref = refs.pop(0), refs.pop(0), refs.pop(0)
        scale_ref = refs.pop(0) if has_mod else None
        resid_ref = refs.pop(0) if has_res else None
        dx_ref, sums_ref = refs
        xv = x_ref[...]
        r = lax.rsqrt(jnp.mean(xv * xv, axis=-1, keepdims=True) + NORM_EPS)
        xhat = xv * r
        dh32 = dh_ref[...].astype(F32)
        gv = g_ref[...]
        dn = dh32 * (1.0 + scale_ref[...]) if has_mod else dh32
        dxhat = dn * gv
        dx = r * (dxhat - xhat * jnp.mean(dxhat * xhat, axis=-1, keepdims=True))
        if has_res:
            dx = dx + resid_ref[...]
        dx_ref[...] = dx

        @pl.when(pl.program_id(0) == 0)
        def _():
            sums_ref[...] = jnp.zeros_like(sums_ref)

        sums_ref[0:1, :] += jnp.sum(dh32, axis=0, keepdims=True)
        sums_ref[1:2, :] += jnp.sum(dh32 * (xhat * gv), axis=0, keepdims=True)
        sums_ref[2:3, :] += jnp.sum(dn * xhat, axis=0, keepdims=True)

    row = pl.BlockSpec((tr, d), lambda i: (i, 0))
    vec = pl.BlockSpec((1, d), lambda i: (0, 0))
    args, in_specs = [x, g, dh], [row, vec, row]
    if has_mod:
        args.append(scale)
        in_specs.append(vec)
    if has_res:
        args.append(resid)
        in_specs.append(row)
    outs, carried = _call(
        body,
        name=name,
        grid=(s // tr,),
        in_specs=in_specs,
        out_specs=[row, pl.BlockSpec((8, d), lambda i: (0, 0))],
        out_shape=[jax.ShapeDtypeStruct((s, d), F32), jax.ShapeDtypeStruct((8, d), F32)],
        args=args,
        sem=("arbitrary",),
        hosted=hosted,
    )
    return outs if hosted is None else (outs, carried)


def _gate_bwd(dx, y, gate, *, name):
    s, d = dx.shape
    tr = _row_tile(s, d)

    def body(dx_ref, y_ref, gate_ref, dy_ref, sums_ref):
        dxv = dx_ref[...]
        dy_ref[...] = (dxv * gate_ref[...]).astype(dy_ref.dtype)

        @pl.when(pl.program_id(0) == 0)
        def _():
            sums_ref[...] = jnp.zeros_like(sums_ref)

        sums_ref[0:1, :] += jnp.sum(dxv * y_ref[...], axis=0, keepdims=True)

    row = pl.BlockSpec((tr, d), lambda i: (i, 0))
    return pl.pallas_call(
        body,
        name=name,
        grid=(s // tr,),
        in_specs=[row, row, pl.BlockSpec((1, d), lambda i: (0, 0))],
        out_specs=[row, pl.BlockSpec((8, d), lambda i: (0, 0))],
        out_shape=[jax.ShapeDtypeStruct((s, d), BF16), jax.ShapeDtypeStruct((8, d), F32)],
        compiler_params=_params("arbitrary"),
    )(dx, y, gate)


def _elementwise(fn, args, out_dtypes, *, name):
    s, d = args[0].shape
    tc = d if d <= 2048 else _tile(d, 1024, LANES)
    tr = _tile(s, max(8, (1 << 18) // tc), 8)
    n_in = len(args)

    def body(*refs):
        outs = fn(*[r[...] for r in refs[:n_in]])
        for o_ref, o in zip(refs[n_in:], outs):
            o_ref[...] = o.astype(o_ref.dtype)

    spec = pl.BlockSpec((tr, tc), lambda i, j: (i, j))
    return pl.pallas_call(
        body,
        name=name,
        grid=(s // tr, d // tc),
        in_specs=[spec] * n_in,
        out_specs=[spec] * len(out_dtypes),
        out_shape=[jax.ShapeDtypeStruct((s, d), dt) for dt in out_dtypes],
        compiler_params=_params("parallel", "parallel"),
    )(*args)


def _silu(v):
    return (v * jax.nn.sigmoid(v),)


def _split3(v):
    hi = v.astype(BF16)
    r1 = v - hi.astype(F32)
    mid = r1.astype(BF16)
    lo = (r1 - mid.astype(F32)).astype(BF16)
    return hi, mid, lo


def _band_dot(band, v):
    return sum(jnp.dot(band, part, preferred_element_type=F32) for part in _split3(v))


def _pool(h, *, name, transpose, out_dtype):
    s, d = h.shape
    c = d // len(POOL_WINDOWS)
    tr = _tile(s, 256, HALO)
    per = tr // HALO
    n_halo = s // HALO

    def body(h_ref, halo_ref, o_ref):
        i = pl.program_id(0)
        out_row = i * tr + lax.broadcasted_iota(jnp.int32, (tr, tr + HALO), 0)
        col = lax.broadcasted_iota(jnp.int32, (tr, tr + HALO), 1)
        if transpose:
            ext = jnp.concatenate([h_ref[...], halo_ref[...]], axis=0)
            src_row = i * tr + col
            ext_row = i * tr + lax.broadcasted_iota(jnp.int32, (tr + HALO, 1), 0)
        else:
            ext = jnp.concatenate([halo_ref[...], h_ref[...]], axis=0)
            src_row = i * tr + col - HALO
            own_row = i * tr + lax.broadcasted_iota(jnp.int32, (tr, 1), 0)
        for gi, w in enumerate(POOL_WINDOWS):
            cols = slice(gi * c, (gi + 1) * c)
            if transpose:
                band = (src_row >= out_row) & (src_row < out_row + w) & (src_row < s)
                scaled = ext[:, cols] / jnp.minimum(ext_row + 1, w).astype(F32)
                res = _band_dot(band.astype(BF16), scaled) - h_ref[:, cols]
            else:
                band = (src_row <= out_row) & (src_row > out_row - w) & (src_row >= 0)
                res = _band_dot(band.astype(BF16), ext[:, cols]) / jnp.minimum(own_row + 1, w).astype(F32) - h_ref[:, cols]
            o_ref[:, cols] = res.astype(o_ref.dtype)

    if transpose:
        halo_map = lambda i: (jnp.minimum((i + 1) * per, n_halo - 1), 0)
    else:
        halo_map = lambda i: (jnp.maximum(i * per - 1, 0), 0)
    return pl.pallas_call(
        body,
        name=name,
        grid=(s // tr,),
        in_specs=[pl.BlockSpec((tr, d), lambda i: (i, 0)), pl.BlockSpec((HALO, d), halo_map)],
        out_specs=pl.BlockSpec((tr, d), lambda i: (i, 0)),
        out_shape=jax.ShapeDtypeStruct((s, d), out_dtype),
        compiler_params=_params("parallel"),
    )(h, h)


def _rotate(v, cos, sin):
    lane = lax.broadcasted_iota(jnp.int32, v.shape, 1)
    swapped = jnp.where(lane % ROPE_DIM < ROPE_DIM // 2, pltpu.roll(v, LANES - ROPE_DIM // 2, 1), pltpu.roll(v, ROPE_DIM // 2, 1))
    return v * cos + swapped * sin


def _rope_heads(x, cos, sin, *, name, out_dtype):
    s, n = x.shape
    tr = _tile(s, max(16, (1 << 18) // n), 16)

    def body(x_ref, cos_ref, sin_ref, o_ref):
        cos_v, sin_v = cos_ref[...], sin_ref[...]
        for j in range(n // LANES):
            lanes = slice(j * LANES, (j + 1) * LANES)
            if j % 2 == 0:
                o_ref[:, lanes] = x_ref[:, lanes].astype(o_ref.dtype)
            else:
                o_ref[:, lanes] = _rotate(x_ref[:, lanes].astype(F32), cos_v, sin_v).astype(o_ref.dtype)

    blk = pl.BlockSpec((tr, n), lambda i: (i, 0))
    tab = pl.BlockSpec((tr, LANES), lambda i: (i, 0))
    return pl.pallas_call(
        body,
        name=name,
        grid=(s // tr,),
        in_specs=[blk, tab, tab],
        out_specs=blk,
        out_shape=jax.ShapeDtypeStruct((s, n), out_dtype),
        compiler_params=_params("parallel"),
    )(x, cos, sin)


def _build_keys(kv, kr_pre, cos, sin, *, name):
    s, n = kv.shape
    heads = n // HEAD_PAD
    t = _attn_tile(s)

    def body(kv_ref, kr_ref, cos_ref, sin_ref, keys_ref, kt_ref, vt_ref):
        rope = _rotate(kr_ref[...], cos_ref[...], sin_ref[...])
        nope, val = kv_ref[:, :NOPE_DIM], kv_ref[:, NOPE_DIM:]
        keys_ref[:, :NOPE_DIM] = nope
        keys_ref[:, NOPE_DIM:] = rope.astype(keys_ref.dtype)
        kt_ref[0, 0, :NOPE_DIM, :] = nope.astype(F32).T.astype(kt_ref.dtype)
        kt_ref[0, 0, NOPE_DIM:, :] = rope.T.astype(kt_ref.dtype)
        vt_ref[0, 0] = val.astype(F32).T.astype(vt_ref.dtype)

    tab = pl.BlockSpec((t, LANES), lambda hd, kb: (kb, 0))
    return pl.pallas_call(
        body,
        name=name,
        grid=(heads, s // t),
        in_specs=[pl.BlockSpec((t, HEAD_PAD), lambda hd, kb: (kb, hd)), tab, tab, tab],
        out_specs=[
            pl.BlockSpec((t, HEAD_PAD), lambda hd, kb: (kb, hd)),
            pl.BlockSpec((1, 1, HEAD_PAD, t), lambda hd, kb: (hd, kb, 0, 0)),
            pl.BlockSpec((1, 1, V_DIM, t), lambda hd, kb: (hd, kb, 0, 0)),
        ],
        out_shape=[
            jax.ShapeDtypeStruct((s, n), BF16),
            jax.ShapeDtypeStruct((heads, s // t, HEAD_PAD, t), BF16),
            jax.ShapeDtypeStruct((heads, s // t, V_DIM, t), BF16),
        ],
        compiler_params=_params("parallel", "parallel"),
    )(kv, kr_pre, cos, sin)


def _rope_back(dq_t, cos, sin, *, name):
    heads, nq, _, t = dq_t.shape

    def body(x_ref, cos_ref, sin_ref, o_ref):
        x = x_ref[0, 0].T
        o_ref[:, :NOPE_DIM] = x[:, :NOPE_DIM].astype(o_ref.dtype)
        o_ref[:, NOPE_DIM:] = _rotate(x[:, NOPE_DIM:], cos_ref[...], sin_ref[...]).astype(o_ref.dtype)

    tab = pl.BlockSpec((t, LANES), lambda hd, qb: (qb, 0))
    return pl.pallas_call(
        body,
        name=name,
        grid=(heads, nq),
        in_specs=[pl.BlockSpec((1, 1, HEAD_PAD, t), lambda hd, qb: (hd, qb, 0, 0)), tab, tab],
        out_specs=pl.BlockSpec((t, HEAD_PAD), lambda hd, qb: (qb, hd)),
        out_shape=jax.ShapeDtypeStruct((nq * t, heads * HEAD_PAD), BF16),
        compiler_params=_params("parallel", "parallel"),
    )(dq_t, cos, sin)


def _keys_bwd(dk_a, dk_b, dv_a, dv_b, cos, sin_neg, *, name):
    s, n = dk_a.shape
    heads = n // HEAD_PAD
    tr = _tile(s, 512, 8)

    def body(dka_ref, dkb_ref, dva_ref, dvb_ref, cos_ref, sin_ref, dkv_ref, dkr_ref):
        hd = pl.program_id(1)
        dk = dka_ref[...] + dkb_ref[...]
        dkv_ref[:, :NOPE_DIM] = dk[:, :NOPE_DIM].astype(dkv_ref.dtype)
        dkv_ref[:, NOPE_DIM:] = (dva_ref[...] + dvb_ref[...]).astype(dkv_ref.dtype)

        @pl.when(hd == 0)
        def _():
            dkr_ref[...] = dk[:, NOPE_DIM:]

        @pl.when(hd > 0)
        def _():
            dkr_ref[...] += dk[:, NOPE_DIM:]

        @pl.when(hd == heads - 1)
        def _():
            dkr_ref[...] = _rotate(dkr_ref[...], cos_ref[...], sin_ref[...])

    dk_blk = pl.BlockSpec((tr, HEAD_PAD), lambda i, hd: (i, hd))
    dv_blk = pl.BlockSpec((tr, V_DIM), lambda i, hd: (i, hd))
    tab = pl.BlockSpec((tr, LANES), lambda i, hd: (i, 0))
    return pl.pallas_call(
        body,
        name=name,
        grid=(s // tr, heads),
        in_specs=[dk_blk, dk_blk, dv_blk, dv_blk, tab, tab],
        out_specs=[dk_blk, tab],
        out_shape=[jax.ShapeDtypeStruct((s, n), BF16), jax.ShapeDtypeStruct((s, LANES), F32)],
        compiler_params=_params("parallel", "arbitrary"),
    )(dk_a, dk_b, dv_a, dv_b, cos, sin_neg)


def _attn_tile(s):
    return _tile(s, 512, LANES)


def _causal_mask(t):
    return lax.broadcasted_iota(jnp.int32, (t, t), 0) <= lax.broadcasted_iota(jnp.int32, (t, t), 1)


def _attn_fwd(q, keys, v_t, *, name, hosted=None):
    s = q.shape[0]
    heads = q.shape[1] // HEAD_PAD
    t = _attn_tile(s)
    nq = s // t

    def body(q_ref, k_ref, v_ref, o_ref, lse_ref, m_ref, l_ref, acc_ref):
        qi = pl.program_id(1)
        qv = q_ref[...]
        m_ref[...] = jnp.full_like(m_ref, -jnp.inf)
        l_ref[...] = jnp.zeros_like(l_ref)
        acc_ref[...] = jnp.zeros_like(acc_ref)

        def block(kb, diagonal):
            rows = pl.ds(pl.multiple_of(kb * t, t), t)
            sc_t = lax.dot_general(k_ref[rows, :], qv, _NT, preferred_element_type=F32) * SM_SCALE
            if diagonal:
                sc_t = jnp.where(_causal_mask(t), sc_t, -jnp.inf)
            m_old = m_ref[...]
            m_new = jnp.maximum(m_old, jnp.max(sc_t, axis=0, keepdims=True))
            alpha = jnp.exp(m_old - m_new)
            p_t = jnp.exp(sc_t - m_new)
            l_ref[...] = alpha * l_ref[...] + jnp.sum(p_t, axis=0, keepdims=True)
            acc_ref[...] = alpha * acc_ref[...] + jnp.dot(v_ref[0, kb], p_t.astype(BF16), preferred_element_type=F32)
            m_ref[...] = m_new

        def earlier(kb, carry):
            block(kb, False)
            return carry

        lax.fori_loop(0, qi, earlier, 0)
        block(qi, True)
        o_ref[...] = (acc_ref[...] / l_ref[...]).T.astype(o_ref.dtype)
        lse_ref[0, 0] = m_ref[...] + jnp.log(l_ref[...])

    return _call(
        body,
        name=name,
        grid=(heads, nq),
        in_specs=[
            pl.BlockSpec((t, HEAD_PAD), lambda hd, qi: (qi, hd)),
            pl.BlockSpec((s, HEAD_PAD), lambda hd, qi: (0, hd)),
            pl.BlockSpec((1, nq, V_DIM, t), lambda hd, qi: (hd, 0, 0, 0)),
        ],
        out_specs=[pl.BlockSpec((t, V_DIM), lambda hd, qi: (qi, hd)), pl.BlockSpec((1, 1, 1, t), lambda hd, qi: (hd, qi, 0, 0))],
        out_shape=[jax.ShapeDtypeStruct((s, heads * V_DIM), BF16), jax.ShapeDtypeStruct((heads, nq, 1, t), F32)],
        args=[q, keys, v_t],
        sem=("parallel", "parallel"),
        scratch_shapes=[pltpu.VMEM((1, t), F32), pltpu.VMEM((1, t), F32), pltpu.VMEM((V_DIM, t), F32)],
        hosted=hosted,
    )


def _attn_bwd(q, keys, keys_t, kv, o, do, lse, *, name, hosted=None):
    s = q.shape[0]
    heads = q.shape[1] // HEAD_PAD
    t = _attn_tile(s)
    nq = s // t

    def body(q_ref, k_ref, kt_ref, v_ref, o_ref, do_ref, lse_ref, dq_ref, dk_ref, dv_ref, dk_acc, dv_acc, delta_ref):
        ki = pl.program_id(1)

        @pl.when(ki == 0)
        def _():
            dq_ref[...] = jnp.zeros_like(dq_ref)
            ones = jnp.ones((8, V_DIM), BF16)

            def row_sums(qb, carry):
                rows = pl.ds(pl.multiple_of(qb * t, t), t)
                prod = do_ref[rows, :].astype(F32) * o_ref[rows, :].astype(F32)
                sums = sum(lax.dot_general(ones, part, _NT, preferred_element_type=F32) for part in _split3(prod))
                delta_ref[qb] = sums[0:1]
                return carry

            lax.fori_loop(0, nq, row_sums, 0)

        kv_, vv, kt = k_ref[...], v_ref[...], kt_ref[0, 0]
        dk_acc[...] = jnp.zeros_like(dk_acc)
        dv_acc[...] = jnp.zeros_like(dv_acc)

        def block(qb, diagonal):
            rows = pl.ds(pl.multiple_of(qb * t, t), t)
            qv, dov = q_ref[rows, :], do_ref[rows, :]
            sc_t = lax.dot_general(kv_, qv, _NT, preferred_element_type=F32) * SM_SCALE
            p_t = jnp.exp(sc_t - lse_ref[0, qb])
            if diagonal:
                p_t = jnp.where(_causal_mask(t), p_t, 0.0)
            dv_acc[...] += jnp.dot(p_t.astype(BF16), dov, preferred_element_type=F32)
            dp_t = lax.dot_general(vv, dov, _NT, preferred_element_type=F32)
            ds_t = (p_t * (dp_t - delta_ref[qb]) * SM_SCALE).astype(BF16)
            dk_acc[...] += jnp.dot(ds_t, qv, preferred_element_type=F32)
            dq_ref[0, qb] += jnp.dot(kt, ds_t, preferred_element_type=F32)

        def later(qb, carry):
            block(qb, False)
            return carry

        block(ki, True)
        lax.fori_loop(ki + 1, nq, later, 0)
        dk_ref[...] = dk_acc[...]
        dv_ref[...] = dv_acc[...]

    whole = pl.BlockSpec((s, V_DIM), lambda hd, ki: (0, hd))
    return _call(
        body,
        name=name,
        grid=(heads, nq),
        in_specs=[
            pl.BlockSpec((s, HEAD_PAD), lambda hd, ki: (0, hd)),
            pl.BlockSpec((t, HEAD_PAD), lambda hd, ki: (ki, hd)),
            pl.BlockSpec((1, 1, HEAD_PAD, t), lambda hd, ki: (hd, ki, 0, 0)),
            pl.BlockSpec((t, V_DIM), lambda hd, ki: (ki, 2 * hd + 1)),
            whole,
            whole,
            pl.BlockSpec((1, nq, 1, t), lambda hd, ki: (hd, 0, 0, 0)),
        ],
        out_specs=[
            pl.BlockSpec((1, nq, HEAD_PAD, t), lambda hd, ki: (hd, 0, 0, 0)),
            pl.BlockSpec((t, HEAD_PAD), lambda hd, ki: (ki, hd)),
            pl.BlockSpec((t, V_DIM), lambda hd, ki: (ki, hd)),
        ],
        out_shape=[
            jax.ShapeDtypeStruct((heads, nq, HEAD_PAD, t), F32),
            jax.ShapeDtypeStruct((s, heads * HEAD_PAD), F32),
            jax.ShapeDtypeStruct((s, heads * V_DIM), F32),
        ],
        args=[q, keys, keys_t, kv, o, do, lse],
        sem=("parallel", "arbitrary"),
        scratch_shapes=[pltpu.VMEM((t, HEAD_PAD), F32), pltpu.VMEM((t, V_DIM), F32), pltpu.VMEM((nq, 1, t), F32)],
        hosted=hosted,
    )


def _loss_bwd(x, y, gate, g, target, *, name):
    s, d = x.shape
    tr = _row_tile(s, d)

    def body(x_ref, y_ref, gate_ref, g_ref, t_ref, dx_ref, stats_ref, loss_ref):
        xv = x_ref[...] + gate_ref[...] * y_ref[...]
        r = lax.rsqrt(jnp.mean(xv * xv, axis=-1, keepdims=True) + NORM_EPS)
        xhat = xv * r
        gv = g_ref[...]
        err = xhat * gv - t_ref[...]
        dy = err / d
        dxhat = dy * gv
        dx_ref[...] = r * (dxhat - xhat * jnp.mean(dxhat * xhat, axis=-1, keepdims=True))

        @pl.when(pl.program_id(0) == 0)
        def _():
            stats_ref[...] = jnp.zeros_like(stats_ref)
            loss_ref[...] = jnp.zeros_like(loss_ref)

        stats_ref[0:1, :] += jnp.sum(dy * xhat, axis=0, keepdims=True)
        loss_ref[...] += 0.5 * jnp.sum(jnp.mean(err * err, axis=-1, keepdims=True))

    row = pl.BlockSpec((tr, d), lambda i: (i, 0))
    vec = pl.BlockSpec((1, d), lambda i: (0, 0))
    return pl.pallas_call(
        body,
        name=name,
        grid=(s // tr,),
        in_specs=[row, row, vec, vec, row],
        out_specs=[row, pl.BlockSpec((8, d), lambda i: (0, 0)), pl.BlockSpec((8, LANES), lambda i: (0, 0))],
        out_shape=[jax.ShapeDtypeStruct((s, d), F32), jax.ShapeDtypeStruct((8, d), F32), jax.ShapeDtypeStruct((8, LANES), F32)],
        compiler_params=_params("arbitrary"),
    )(x, y, gate, g, target)


def _adam_math(w, g, m, v):
    new_m = ADAM_B1 * m + (1.0 - ADAM_B1) * g
    new_v = ADAM_B2 * v + (1.0 - ADAM_B2) * (g * g)
    m_hat = new_m / (1.0 - ADAM_B1**ADAM_STEP)
    v_hat = new_v / (1.0 - ADAM_B2**ADAM_STEP)
    return -ADAM_LR * (m_hat / (jnp.sqrt(v_hat) + ADAM_EPS) + ADAM_WD * w), new_m, new_v


def _adamw(w, g, m, v, *, name):
    rows, cols = w.shape
    tr = _tile(rows, max(8, (1 << 18) // cols), 8)

    def body(w_ref, g_ref, m_ref, v_ref, go_ref, d_ref, mo_ref, vo_ref):
        gv = g_ref[...]
        go_ref[...] = gv
        d_ref[...], mo_ref[...], vo_ref[...] = _adam_math(w_ref[...], gv, m_ref[...], v_ref[...])

    spec = pl.BlockSpec((tr, cols), lambda i: (i, 0))
    return pl.pallas_call(
        body,
        name=name,
        grid=(rows // tr,),
        in_specs=[spec] * 4,
        out_specs=[spec] * 4,
        out_shape=[jax.ShapeDtypeStruct((rows, cols), F32)] * 4,
        compiler_params=_params("parallel"),
    )(w, g, m, v)


def _tp_adamw(sc16, dm, w, m, v, *, name):
    nl, d, n = w.shape
    tm = _tile(d, 512, LANES)
    tn = _tile(n, 1024, LANES)

    def body(sc_ref, dm_ref, w_ref, m_ref, v_ref, go_ref, d_ref, mo_ref, vo_ref):
        gv = lax.dot_general(sc_ref[...].astype(BF16), dm_ref[0].astype(BF16), (((0,), (0,)), ((), ())), preferred_element_type=F32)
        go_ref[0] = gv
        d_ref[0], mo_ref[0], vo_ref[0] = _adam_math(w_ref[0], gv, m_ref[0], v_ref[0])

    blk = pl.BlockSpec((1, tm, tn), lambda l, i, j: (l, i, j))
    return pl.pallas_call(
        body,
        name=name,
        grid=(nl, d // tm, n // tn),
        in_specs=[pl.BlockSpec((16, tm), lambda l, i, j: (0, i)), pl.BlockSpec((1, 16, tn), lambda l, i, j: (l, 0, j)), blk, blk, blk],
        out_specs=[blk] * 4,
        out_shape=[jax.ShapeDtypeStruct((nl, d, n), F32)] * 4,
        compiler_params=_params("parallel", "parallel", "parallel"),
    )(sc16, dm, w, m, v)


def _sum_devices(x, *, name):
    def body(x_ref, o_ref):
        acc = x_ref[0]
        for k in range(1, N_DEV):
            acc = acc + x_ref[k]
        o_ref[...] = acc

    return pl.pallas_call(body, name=name, out_shape=jax.ShapeDtypeStruct(x.shape[1:], F32))(x)


def _place():
    mx, my, mc = lax.axis_index("x"), lax.axis_index("y"), lax.axis_index("c")
    chips = [(1 - mx, my), (mx, 1 - my), (1 - mx, 1 - my)]
    return mx, my, mc, chips


def _remote(src, dst, send_sem, recv_sem, device):
    return pltpu.make_async_remote_copy(src_ref=src, dst_ref=dst, send_sem=send_sem, recv_sem=recv_sem, device_id=device, device_id_type=MESH)


def _allgather8(x, *, name):
    def body(x_ref, out_ref, send_sems, recv_sems, local_sem):
        mx, my, mc, chips = _place()
        me, sibling = (mx, my, mc), (mx, my, 1 - mc)

        def slot(px, py, pc):
            return out_ref.at[4 * px + 2 * py + pc]

        def copy(k, block, to, src=None):
            return _remote(slot(*block) if src is None else src, slot(*block), send_sems.at[k], recv_sems.at[k], to)

        mine = pltpu.make_async_copy(x_ref, slot(*me), local_sem)
        mine.start()
        first = [copy(0, me, sibling, src=x_ref)] + [copy(1 + j, me, (*chip, mc), src=x_ref) for j, chip in enumerate(chips)]
        for cp in first:
            cp.start()
        passed = [copy(4 + j, (*chip, mc), sibling) for j, chip in enumerate(chips)]
        for j, chip in enumerate(chips):
            copy(1 + j, (*chip, mc), me).wait_recv()
            passed[j].start()
        copy(0, sibling, me).wait_recv()
        for j, chip in enumerate(chips):
            copy(4 + j, (*chip, 1 - mc), me).wait_recv()
        for cp in first + passed:
            cp.wait_send()
        mine.wait()

    return pl.pallas_call(
        body,
        name=name,
        out_shape=jax.ShapeDtypeStruct((N_DEV,) + x.shape, x.dtype),
        in_specs=[pl.BlockSpec(memory_space=pltpu.VMEM)],
        out_specs=pl.BlockSpec(memory_space=pltpu.VMEM),
        scratch_shapes=[pltpu.SemaphoreType.DMA((7,)), pltpu.SemaphoreType.DMA((7,)), pltpu.SemaphoreType.DMA],
    )(x)


class _Geom:
    def __init__(self, shape3, axis):
        self.shape3, self.axis = shape3, axis
        nl, r, c = shape3
        self.rs, self.cs = (r // N_CHIPS, c) if axis == 1 else (r, c // N_CHIPS)
        self.hl, self.hr = (nl // 2, self.rs) if nl > 1 else (1, self.rs // 2)
        self.shard = (nl, self.rs, self.cs)
        self.half = (self.hl, self.hr, self.cs)

    def in_full(self, ref, chip, core):
        nl = self.shape3[0]
        l0 = core * self.hl if nl > 1 else 0
        r0 = (chip * self.rs if self.axis == 1 else 0) + (0 if nl > 1 else core * self.hr)
        c0 = chip * self.cs if self.axis == 2 else 0
        return ref.at[pl.ds(l0, self.hl), pl.ds(r0, self.hr), pl.ds(c0, self.cs)]


def _place_shard(shard, geom, chip_arr, *, name, layer=None):
    nl, rs, cs = geom.shard
    tr = _tile(rs, max(16, (1 << 20) // cs), 16)
    per = rs // tr
    first = 0 if layer is None else layer

    def body(chip_ref, x_ref, o_ref):
        o_ref[...] = x_ref[...].astype(o_ref.dtype)

    def out_map(l, i, chip_ref):
        return (l, chip_ref[0] * per + i, 0) if geom.axis == 1 else (l, i, chip_ref[0])

    return pl.pallas_call(
        body,
        name=name,
        grid_spec=pltpu.PrefetchScalarGridSpec(
            num_scalar_prefetch=1,
            grid=(nl, per),
            in_specs=[pl.BlockSpec((1, tr, cs), lambda l, i, chip_ref: (first + l, i, 0))],
            out_specs=pl.BlockSpec((1, tr, cs), out_map),
        ),
        out_shape=jax.ShapeDtypeStruct(geom.shape3, BF16),
        compiler_params=_params("parallel", "parallel"),
    )(chip_arr, shard)


def _dma_sems(count, arrays):
    return [pltpu.SemaphoreType.DMA((count,))] * arrays


def _gather_plan(fulls, geoms):
    def ici(w, k, src, dst, sems, device):
        return _remote(src, dst, sems[0].at[3 * w + k], sems[1].at[3 * w + k], device)

    def d2d(w, k, box, sems, device):
        return _remote(box, box, sems[2].at[3 * w + k], sems[3].at[3 * w + k], device)

    def start(given, full, sems):
        mx, my, mc, chips = _place()
        me = 2 * mx + my
        for w, geom in enumerate(geoms):
            for k, chip in enumerate(chips):
                ici(w, k, geom.in_full(given[w], me, mc), geom.in_full(full[w], me, mc), sems, (*chip, mc)).start()

    def finish(given, full, sems):
        mx, my, mc, chips = _place()
        me, sibling = 2 * mx + my, (mx, my, 1 - mc)
        for w, geom in enumerate(geoms):
            for k, (px, py) in enumerate(chips):
                landed = geom.in_full(full[w], 2 * px + py, mc)
                ici(w, k, landed, landed, sems, (px, py, mc)).wait_recv()
                d2d(w, k, landed, sems, sibling).start()
        for w, geom in enumerate(geoms):
            for k, (px, py) in enumerate(chips):
                d2d(w, k, geom.in_full(full[w], 2 * px + py, 1 - mc), sems, sibling).wait_recv()
        for w, geom in enumerate(geoms):
            for k, (px, py) in enumerate(chips):
                ici(w, k, geom.in_full(given[w], me, mc), geom.in_full(full[w], me, mc), sems, (px, py, mc)).wait_send()
                d2d(w, k, geom.in_full(full[w], 2 * px + py, mc), sems, sibling).wait_send()

    n = len(fulls)
    shapes = [jax.ShapeDtypeStruct(f.shape, f.dtype) for f in fulls]
    return _Hosted(fulls, shapes, {w: w for w in range(n)}, _dma_sems(3 * n, 4), start, finish)


def _pair_plan(grads, geoms):
    def copies(grad, theirs, sems):
        mx, my, mc, _ = _place()
        return [
            _remote(geom.in_full(grad[w], chip, 1 - mc), theirs[w].at[chip], sems[0].at[4 * w + chip], sems[1].at[4 * w + chip], (mx, my, 1 - mc))
            for w, geom in enumerate(geoms)
            for chip in range(N_CHIPS)
        ]

    def start(grad, theirs, sems):
        for cp in copies(grad, theirs, sems):
            cp.start()

    def finish(grad, theirs, sems):
        for cp in copies(grad, theirs, sems):
            cp.wait_recv()
        for cp in copies(grad, theirs, sems):
            cp.wait_send()

    shapes = [jax.ShapeDtypeStruct((N_CHIPS,) + g.half, x.dtype) for g, x in zip(geoms, grads)]
    return _Hosted(grads, shapes, {}, _dma_sems(4 * len(grads), 2), start, finish)


def _half_tile(geom):
    return _tile(geom.hr, max(16, (1 << 18) // geom.cs), 16)


def _pair_add(grad, theirs, geom, core_arr, *, name):
    hl, hr, cs = geom.half
    tr = _half_tile(geom)
    stacked = geom.shape3[0] > 1

    def grad_map(chip, l, i, core_ref):
        layer = core_ref[0] * hl + l if stacked else 0
        row = (chip * (geom.rs // tr) if geom.axis == 1 else 0) + (0 if stacked else core_ref[0] * (hr // tr)) + i
        return layer, row, (chip if geom.axis == 2 else 0)

    def body(core_ref, g_ref, t_ref, o_ref):
        o_ref[0] = (g_ref[...].astype(F32) + t_ref[0].astype(F32)).astype(o_ref.dtype)

    blk = pl.BlockSpec((1, 1, tr, cs), lambda chip, l, i, core_ref: (chip, l, i, 0))
    return pl.pallas_call(
        body,
        name=name,
        grid_spec=pltpu.PrefetchScalarGridSpec(
            num_scalar_prefetch=1, grid=(N_CHIPS, hl, hr // tr), in_specs=[pl.BlockSpec((1, tr, cs), grad_map), blk], out_specs=blk
        ),
        out_shape=jax.ShapeDtypeStruct(theirs.shape, BF16),
        compiler_params=_params("parallel", "parallel", "parallel"),
    )(core_arr, grad, theirs)


def _chips_plan(parts):
    def copies(part, slots, sems):
        _, _, mc, chips = _place()
        return [
            _remote(part[w].at[2 * px + py], slots[w].at[k], sems[0].at[3 * w + k], sems[1].at[3 * w + k], (px, py, mc))
            for w in range(len(parts))
            for k, (px, py) in enumerate(chips)
        ]

    def start(part, slots, sems):
        for cp in copies(part, slots, sems):
            cp.start()

    def finish(part, slots, sems):
        for cp in copies(part, slots, sems):
            cp.wait_recv()
        for cp in copies(part, slots, sems):
            cp.wait_send()

    shapes = [jax.ShapeDtypeStruct((N_CHIPS - 1,) + p.shape[1:], p.dtype) for p in parts]
    return _Hosted(parts, shapes, {}, _dma_sems(3 * len(parts), 2), start, finish)


def _chip_sum(part, slots, geom, place_arr, *, name, stack=None):
    hl, hr, cs = geom.half
    tr = _half_tile(geom)

    def body(place_ref, own_ref, s0_ref, s1_ref, s2_ref, *rest):
        o_ref = rest[-1]
        o_ref[...] = ((own_ref[...].astype(F32) + s0_ref[...].astype(F32)) + s1_ref[...].astype(F32)) + s2_ref[...].astype(F32)

    def slot(k):
        return pl.BlockSpec((1, 1, tr, cs), lambda l, i, place_ref: (k, l, i, 0))

    in_specs = [pl.BlockSpec((1, 1, tr, cs), lambda l, i, place_ref: (place_ref[0], l, i, 0)), slot(0), slot(1), slot(2)]
    args = [place_arr, part, slots, slots, slots]
    aliases = {}
    if stack is None:
        out_spec = pl.BlockSpec((1, 1, tr, cs), lambda l, i, place_ref: (place_ref[1], l, i, 0))
        out_shape = jax.ShapeDtypeStruct((2,) + geom.half, F32)
    else:
        layers, layer, prev = stack
        assert hl == 1
        out_spec = pl.BlockSpec((1, 1, tr, cs), lambda l, i, place_ref: (layer, place_ref[1], i, 0))
        out_shape = jax.ShapeDtypeStruct((layers, 2, hr, cs), F32)
        if prev is not None:
            aliases = {len(args): 0}
            in_specs.append(ANY)
            args.append(prev)
    return pl.pallas_call(
        body,
        name=name,
        grid_spec=pltpu.PrefetchScalarGridSpec(num_scalar_prefetch=1, grid=(hl, hr // tr), in_specs=in_specs, out_specs=out_spec),
        out_shape=out_shape,
        input_output_aliases=aliases,
        compiler_params=_params("parallel", "parallel"),
    )(*args)


def _join_plan(boths, prefixes):
    def copies(given, both, sems):
        mx, my, mc, _ = _place()
        out, n = [], 0
        for w in range(len(boths)):
            for p in prefixes[w]:
                out.append(_remote(given[w].at[(*p, mc)], both[w].at[(*p, mc)], sems[0].at[n], sems[1].at[n], (mx, my, 1 - mc)))
                n += 1
        return out

    def arrivals(both, sems):
        mx, my, mc, _ = _place()
        out, n = [], 0
        for w in range(len(boths)):
            for p in prefixes[w]:
                got = both[w].at[(*p, 1 - mc)]
                out.append(_remote(got, got, sems[0].at[n], sems[1].at[n], (mx, my, 1 - mc)))
                n += 1
        return out

    def start(given, both, sems):
        for cp in copies(given, both, sems):
            cp.start()

    def finish(given, both, sems):
        for cp in arrivals(both, sems):
            cp.wait_recv()
        for cp in copies(given, both, sems):
            cp.wait_send()

    count = sum(len(p) for p in prefixes)
    shapes = [jax.ShapeDtypeStruct(b.shape, b.dtype) for b in boths]
    return _Hosted(boths, shapes, {w: w for w in range(len(boths))}, _dma_sems(count, 2), start, finish)


WEIGHTS = ("mod_w", "mod_b", "norm_mix", "norm_ffn", "pool_w", "pool_scale", "kv_mod_w", "kv_mod_b", "kv_in_norm", "w_dkv", "kv_norm",
           "w_uk", "w_uv", "w_kr", "w_dq", "q_norm", "w_uq", "w_o", "ffn_gate", "ffn_up", "ffn_down", "final_norm")
SMALL = ("mod_b", "kv_mod_b", "norm_mix", "norm_ffn", "kv_in_norm", "kv_norm", "q_norm", "final_norm")


def _rows(v):
    return v.reshape(-1, LANES)


def _pad_rows(a):
    return jnp.pad(a, ((0, (-a.shape[0]) % 8), (0, 0)))


def _vec(v):
    return v.reshape(1, -1)


def _step(x, c, positions, target, wts, mom, var):
    _, s, d = x.shape
    depth, n_a, n_b = wts["mod_w"].shape[0], wts["pool_w"].shape[0], wts["w_dq"].shape[0]
    assert n_b == 2 and n_a + n_b == depth
    heads = d // V_DIM
    kvr, qr = wts["w_dkv"].shape[1], wts["w_dq"].shape[2]
    ffn = wts["ffn_gate"].shape[2] * N_CHIPS
    pool_c = d // len(POOL_WINDOWS)
    nmod, nkv = N_MOD * d, 2 * d
    mx, my, mc = lax.axis_index("x"), lax.axis_index("y"), lax.axis_index("c")
    chip, dev = 2 * mx + my, 4 * mx + 2 * my + mc
    xs, tgt = x[0], target[0]

    inv_freq = 1.0 / (ROPE_THETA ** (jnp.arange(0, ROPE_DIM, 2, dtype=F32) / ROPE_DIM))
    ang = positions[0].astype(F32)[:, None] * inv_freq
    cos, sin, zero = jnp.cos(ang), jnp.sin(ang), jnp.zeros((s, LANES - ROPE_DIM), F32)
    cos_t = jnp.concatenate([cos, cos, zero], axis=1)
    sin_fwd = jnp.concatenate([-sin, sin, zero], axis=1)
    sin_bwd = jnp.concatenate([sin, -sin, zero], axis=1)

    c_rows, ps_rows = d // LANES, n_a * (d // N_CHIPS) // LANES
    cond = _allgather8(_pad_rows(jnp.concatenate([_rows(c), _rows(wts["pool_scale"])])), name="gather_cond")
    c_all = cond[:, :c_rows].reshape(N_DEV, d)
    pool_scale = cond[0::2, c_rows : c_rows + ps_rows].reshape(N_CHIPS, n_a, d // N_CHIPS).transpose(1, 0, 2).reshape(n_a, d)
    sc16 = _elementwise(_silu, [jnp.pad(c_all, ((0, 16 - N_DEV), (0, 0)))], [F32], name="silu_cond")[0]

    mod_bias = lax.dynamic_slice_in_dim(wts["mod_b"], chip * (nmod // N_CHIPS), nmod // N_CHIPS, axis=1)[:, None, :]
    kv_bias = lax.dynamic_slice_in_dim(wts["kv_mod_b"], chip * (nkv // N_CHIPS), nkv // N_CHIPS).reshape(1, 1, -1)
    mod_part = _tp_fwd(sc16, wts["mod_w"], mod_bias, name="mod_fwd")
    kv_part = _tp_fwd(sc16, wts["kv_mod_w"][None], kv_bias, name="kv_mod_fwd")
    part = jnp.concatenate([mod_part[i, :N_DEV] for i in range(depth)] + [kv_part[0, :N_DEV]], axis=1)
    ncol = part.shape[1]
    gathered = _allgather8(_pad_rows(_rows(part)), name="gather_mods")
    gathered = gathered[0::2, : N_DEV * ncol // LANES].reshape(N_CHIPS, N_DEV, ncol)
    mine = lax.dynamic_index_in_dim(gathered, dev, axis=1, keepdims=False)
    per = nmod // N_CHIPS
    mods = [mine[:, i * per : (i + 1) * per].reshape(N_MOD, 1, d) for i in range(depth)]
    kv_shift, kv_scale = mine[:, depth * per :].reshape(2, 1, d)

    mixer_names = ("pool_w", "w_dkv", "w_uk", "w_uv", "w_kr", "w_dq", "w_uq", "w_o")
    ffn_names = ("ffn_gate", "ffn_up", "ffn_down")
    geoms = {
        "pool_w": _Geom((n_a * len(POOL_WINDOWS), pool_c, pool_c), 1),
        "w_dkv": _Geom((1, d, kvr), 1),
        "w_uk": _Geom((1, kvr, heads * NOPE_DIM), 2),
        "w_uv": _Geom((1, kvr, heads * V_DIM), 2),
        "w_kr": _Geom((1, d, ROPE_DIM), 1),
        "w_dq": _Geom((n_b, d, qr), 1),
        "w_uq": _Geom((n_b, qr, heads * (NOPE_DIM + ROPE_DIM)), 2),
        "w_o": _Geom((n_b, d, d), 1),
        "ffn_gate": _Geom((1, d, ffn), 2),
        "ffn_up": _Geom((1, d, ffn), 2),
        "ffn_down": _Geom((1, ffn, d), 1),
    }
    mixer_geoms = [geoms[n] for n in mixer_names]
    ffn_geoms = [geoms[n] for n in ffn_names]
    chip_arr, core_arr, place_arr = chip.reshape(1), mc.reshape(1), jnp.stack([chip, mc])
    placed = [_place_shard(wts[n].reshape(geoms[n].shard), geoms[n], chip_arr, name="place_" + n) for n in mixer_names]
    placed_ffn = [[_place_shard(wts[n], geoms[n], chip_arr, layer=i, name="place_" + n) for n in ffn_names] for i in range(depth)]
    first = _run(_gather_plan(placed + placed_ffn[0], mixer_geoms + ffn_geoms), name="gather_first")
    full = dict(zip(mixer_names, first))
    whole_k = dict(tm=512, tn=1024, tk=max(s, ffn))
    ffn_w = [None] * depth
    ffn_w[0] = first[len(mixer_names) :]

    pool_w = full["pool_w"].reshape(n_a, len(POOL_WINDOWS), pool_c, pool_c)
    w_uq = full["w_uq"].reshape(n_b, qr, heads, NOPE_DIM + ROPE_DIM)
    w_q = jnp.pad(w_uq, ((0, 0), (0, 0), (0, 0), (0, HEAD_PAD - NOPE_DIM - ROPE_DIM))).reshape(n_b, qr, heads * HEAD_PAD)
    w_ukv = jnp.stack([full["w_uk"].reshape(kvr, heads, NOPE_DIM), full["w_uv"].reshape(kvr, heads, V_DIM)], axis=2).reshape(kvr, heads * HEAD_PAD)
    w_dkvkr = jnp.concatenate([full["w_dkv"][0], full["w_kr"][0], jnp.zeros((d, LANES - ROPE_DIM), BF16)], axis=1)
    w_dq, w_o = full["w_dq"], full["w_o"]

    norm_mix, norm_ffn = wts["norm_mix"], wts["norm_ffn"]
    saved = []
    cur, pending = xs, None
    kv_side = None
    for i in range(depth):
        shift_m, scale_m, gate_m, shift_f, scale_f, gate_f = mods[i]
        h1_dtype = F32 if i < n_a else BF16
        if pending is None:
            x0 = cur
            h1 = _norm_fwd(x0, _vec(norm_mix[i]), scale=scale_m, shift=shift_m, out_dtype=h1_dtype, name="norm_mix_first")
        else:
            x0, h1 = _norm_fwd(cur, _vec(norm_mix[i]), scale=scale_m, shift=shift_m, y=pending[0], gate=pending[1], out_dtype=h1_dtype, name="norm_mix")
        lay = {"x0": x0, "h1": h1}
        if i == n_a:
            h_kv = _norm_fwd(x0, _vec(wts["kv_in_norm"]), scale=kv_scale, shift=kv_shift, name="norm_kv_in")
            pre = _mm(h_kv, w_dkvkr, name="kv_down", tn=kvr + LANES)
            ckv_pre, kr_pre = pre[:, :kvr], pre[:, kvr:]
            ckv = _norm_fwd(ckv_pre, _vec(wts["kv_norm"]), name="norm_kv")
            kv = _mm(ckv, w_ukv, out_dtype=BF16, name="kv_up")
            keys, keys_t, v_t = _build_keys(kv, kr_pre, cos_t, sin_fwd, name="build_keys")
            kv_side = {"h_kv": h_kv, "ckv_pre": ckv_pre, "ckv": ckv, "kv": kv, "keys": keys, "keys_t": keys_t, "v_t": v_t, "x0": x0}
        if i < n_a:
            pooled = _pool(h1, transpose=False, out_dtype=BF16, name="pool_fwd")
            y_pre = _gmm(pooled, pool_w[i], mode="nn", out_dtype=F32, name="pool_mix")
            gate_eff = gate_m * _vec(pool_scale[i])
            lay.update(pooled=pooled)
        else:
            l = i - n_a
            cq_pre = _mm(h1, w_dq[l], name="q_down")
            cq = _norm_fwd(cq_pre, _vec(wts["q_norm"][l]), name="norm_q")
            q = _rope_heads(_mm(cq, w_q[l], name="q_up"), cos_t, sin_fwd, out_dtype=BF16, name="rope_q")
            if i + 1 < depth:
                (o, lse), ffn_w[i + 1] = _attn_fwd(q, kv_side["keys"], kv_side["v_t"], hosted=_gather_plan(placed_ffn[i + 1], ffn_geoms), name="attn_fwd")
            else:
                (o, lse), _ = _attn_fwd(q, kv_side["keys"], kv_side["v_t"], name="attn_fwd_last")
            y_pre = _mm(o, w_o[l], name="attn_out")
            gate_eff = gate_m
            lay.update(cq_pre=cq_pre, cq=cq, q=q, o=o, lse=lse)
        x1, h2 = _norm_fwd(x0, _vec(norm_ffn[i]), scale=scale_f, shift=shift_f, y=y_pre, gate=gate_eff, name="norm_ffn")
        w_gate, w_up, w_down = ffn_w[i]
        if i + 1 < depth and ffn_w[i + 1] is None:
            (a, b, z), next_in = _ffn_in(h2, w_gate, w_up, 0, hosted=_gather_plan(placed_ffn[i + 1][:2], ffn_geoms[:2]), name="ffn_in")
            f, next_down = _mm(z, w_down, b_idx=0, hosted=_gather_plan(placed_ffn[i + 1][2:], ffn_geoms[2:]), name="ffn_down", **whole_k)
            ffn_w[i + 1] = next_in + next_down
        else:
            (a, b, z), _ = _ffn_in(h2, w_gate, w_up, 0, name="ffn_in_last")
            f = _mm(z, w_down, b_idx=0, name="ffn_down_last", **whole_k)
        lay.update(y_pre=y_pre, gate_eff=gate_eff, x1=x1, h2=h2, a=a, b=b, z=z, f=f)
        saved.append(lay)
        cur, pending = x1, (f, gate_f)

    dx, final_stats, loss_tile = _loss_bwd(cur, pending[0], pending[1], _vec(wts["final_norm"]), tgt, name="loss")
    loss = lax.psum(loss_tile[0, 0], ("x", "y", "c"))

    ffn_both = [None] * len(ffn_names)
    in_flight = None
    attn_names = tuple(n for n in mixer_names if n != "pool_w")
    grad_full = {}
    attn_parts = attn_slots = None

    def sum_chips(layer, which, parts, slots):
        for w, part, slot in zip(which, parts, slots):
            ffn_both[w] = _chip_sum(part, slot, ffn_geoms[w], place_arr, stack=(depth, layer, ffn_both[w]), name="chip_sum_" + ffn_names[w])

    g_wo, g_wq, g_wdq = [None] * n_b, [None] * n_b, [None] * n_b
    g_pool = [None] * n_a
    dmods = [None] * depth
    g_norm_mix, g_norm_ffn, g_q_norm, g_pool_scale = [None] * depth, [None] * depth, [None] * n_b, [None] * n_a
    dk_layers, dv_layers = [None] * n_b, [None] * n_b
    for i in reversed(range(depth)):
        lay = saved[i]
        shift_m, scale_m, gate_m, shift_f, scale_f, gate_f = mods[i]
        df, sums_gf = _gate_bwd(dx, lay["f"], gate_f, name="gate_bwd")
        w_gate, w_up, w_down = ffn_w[i]
        g_down = _mm(lay["z"], df, ta=True, out_dtype=BF16, name="ffn_down_dw", **whole_k)
        carry_attn = attn_parts is not None and attn_slots is None
        riding = [] if in_flight is None else in_flight
        plans = [_pair_plan([g_down[None]], ffn_geoms[2:])] + ([_chips_plan(riding[:1])] if riding else [])
        (da, db), got = _ffn_down_bwd(df, w_down, 0, lay["a"], lay["b"], hosted=_merge(plans), name="ffn_down_bwd")
        their_down, got_gate = got[0], got[1:]
        down_part = _pair_add(g_down[None], their_down, ffn_geoms[2], core_arr, name="pair_add_ffn_down")
        plans = [_chips_plan(part) for part in (riding[1:], attn_parts if carry_attn else None) if part]
        (dh2,), got = _ffn_in_dx(da, db, w_gate, w_up, 0, hosted=_merge(plans) if plans else None, name="ffn_in_dx")
        got_up, got = got[: len(riding[1:])], got[len(riding[1:]) :]
        if carry_attn:
            attn_slots = got
        (g_gate, g_up), got_down = _ffn_in_dw(lay["h2"], da, db, hosted=_chips_plan([down_part]), name="ffn_in_dw")
        sum_chips(i, [2], [down_part], got_down)
        if riding:
            sum_chips(i + 1, [0, 1], riding, got_gate + got_up)
        (dx1, sums_f), their_in = _norm_bwd(
            lay["x1"], _vec(norm_ffn[i]), dh2, scale=scale_f, resid=dx, hosted=_pair_plan([g_gate[None], g_up[None]], ffn_geoms[:2]), name="norm_ffn_bwd"
        )
        in_flight = [
            _pair_add(g[None], th, geom, core_arr, name="pair_add_" + n) for g, th, geom, n in zip((g_gate, g_up), their_in, ffn_geoms, ffn_names)
        ]
        dyp, sums_gm = _gate_bwd(dx1, lay["y_pre"], lay["gate_eff"], name="gate_bwd")
        if i < n_a:
            g_pool[i] = _gmm(lay["pooled"], dyp, mode="tn", out_dtype=BF16, name="pool_mix_dw")
            dd = _gmm(dyp, pool_w[i], mode="nt", out_dtype=F32, name="pool_mix_dx")
            dh1 = _pool(dd, transpose=True, out_dtype=F32, name="pool_bwd")
            dgate_m = sums_gm[0] * pool_scale[i]
            g_pool_scale[i] = sums_gm[0] * gate_m[0]
        else:
            l = i - n_a
            do = _mm(dyp, w_o[l], tb=True, out_dtype=BF16, name="attn_out_dx")
            g_wo[l] = _mm(lay["o"], dyp, ta=True, out_dtype=BF16, name="attn_out_dw")
            (dq_t, dk_layers[l], dv_layers[l]), got = _attn_bwd(
                lay["q"], kv_side["keys"], kv_side["keys_t"], kv_side["kv"], lay["o"], do, lay["lse"], hosted=_chips_plan(in_flight), name="attn_bwd"
            )
            sum_chips(i, [0, 1], in_flight, got)
            in_flight = None
            dq_pre = _rope_back(dq_t, cos_t, sin_bwd, name="rope_q_bwd")
            dcq = _mm(dq_pre, w_q[l], tb=True, name="q_up_dx")
            g_wq[l] = _mm(lay["cq"], dq_pre, ta=True, out_dtype=BF16, name="q_up_dw")
            dcq_pre, sums_q = _norm_bwd(lay["cq_pre"], _vec(wts["q_norm"][l]), dcq, name="norm_q_bwd")
            g_q_norm[l] = sums_q[2]
            dh1 = _mm(dcq_pre, w_dq[l], tb=True, name="q_down_dx")
            g_wdq[l] = _mm(lay["h1"], dcq_pre, ta=True, out_dtype=BF16, name="q_down_dw")
            dgate_m = sums_gm[0]
        dx, sums_m = _norm_bwd(lay["x0"], _vec(norm_mix[i]), dh1, scale=scale_m, resid=dx1, name="norm_mix_bwd")
        if i == n_a:
            dkv, dkr_pre = _keys_bwd(dk_layers[0], dk_layers[1], dv_layers[0], dv_layers[1], cos_t, sin_bwd, name="keys_bwd")
            dckv = _mm(dkv, w_ukv, tb=True, name="kv_up_dx")
            g_ukv = _mm(kv_side["ckv"], dkv, ta=True, out_dtype=BF16, name="kv_up_dw")
            dckv_pre, sums_kvn = _norm_bwd(kv_side["ckv_pre"], _vec(wts["kv_norm"]), dckv, name="norm_kv_bwd")
            dpre = jnp.concatenate([dckv_pre, dkr_pre], axis=1)
            dh_kv = _mm(dpre, w_dkvkr, tb=True, name="kv_down_dx")
            g_dkvkr = _mm(kv_side["h_kv"], dpre, ta=True, out_dtype=BF16, name="kv_down_dw", tn=kvr + LANES)
            dx, sums_kv = _norm_bwd(lay["x0"], _vec(wts["kv_in_norm"]), dh_kv, scale=kv_scale, resid=dx, name="norm_kv_in_bwd")
            g_ukv = g_ukv.reshape(kvr, heads, 2, NOPE_DIM)
            grad_full.update(
                w_dkv=g_dkvkr[None, :, :kvr],
                w_uk=g_ukv[:, :, 0].reshape(1, kvr, heads * NOPE_DIM),
                w_uv=g_ukv[:, :, 1].reshape(1, kvr, heads * V_DIM),
                w_kr=g_dkvkr[None, :, kvr : kvr + ROPE_DIM],
                w_dq=jnp.stack(g_wdq),
                w_uq=jnp.stack(g_wq).reshape(n_b, qr, heads, HEAD_PAD)[..., : NOPE_DIM + ROPE_DIM].reshape(geoms["w_uq"].shape3),
                w_o=jnp.stack(g_wo),
            )
            attn_theirs = _run(_pair_plan([grad_full[n] for n in attn_names], [geoms[n] for n in attn_names]), name="reduce_pair_attn")
            attn_parts = [_pair_add(grad_full[n], th, geoms[n], core_arr, name="pair_add_" + n) for n, th in zip(attn_names, attn_theirs)]
        dmods[i] = jnp.concatenate([sums_m[0], sums_m[1], dgate_m, sums_f[0], sums_f[1], sums_gf[0]])
        g_norm_mix[i], g_norm_ffn[i] = sums_m[2], sums_f[2]
    grad_x = dx[None]
    grad_full["pool_w"] = jnp.stack(g_pool).reshape(geoms["pool_w"].shape3)

    small_grads = {
        "mod_b": jnp.concatenate(dmods),
        "kv_mod_b": jnp.concatenate([sums_kv[0], sums_kv[1]]),
        "norm_mix": jnp.concatenate(g_norm_mix),
        "norm_ffn": jnp.concatenate(g_norm_ffn),
        "kv_in_norm": sums_kv[2],
        "kv_norm": sums_kvn[2],
        "q_norm": jnp.concatenate(g_q_norm),
        "final_norm": final_stats[0],
    }
    packed = jnp.concatenate([small_grads[n] for n in SMALL] + g_pool_scale)
    small_rows = sum(wts[n].size for n in SMALL) // LANES
    every = _allgather8(_pad_rows(_rows(packed)), name="gather_small_grads")
    summed = _sum_devices(every, name="sum_small_grads")

    mod_rows = depth * nmod // LANES
    dm_all = every[:, :mod_rows].reshape(N_DEV, depth, nmod)
    dm = lax.dynamic_slice_in_dim(dm_all, chip * per, per, axis=2).transpose(1, 0, 2)
    dm = jnp.pad(dm, ((0, 0), (0, 16 - N_DEV), (0, 0)))
    dkvm_all = every[:, mod_rows : mod_rows + nkv // LANES].reshape(N_DEV, nkv)
    dkvm = jnp.pad(lax.dynamic_slice_in_dim(dkvm_all, chip * (nkv // N_CHIPS), nkv // N_CHIPS, axis=1), ((0, 16 - N_DEV), (0, 0)))[None]
    results = {}
    results["mod_w"] = _tp_adamw(sc16, dm, wts["mod_w"], mom["mod_w"], var["mod_w"], name="mod_w_update")
    results["kv_mod_w"] = [
        r[0] for r in _tp_adamw(sc16, dkvm, wts["kv_mod_w"][None], mom["kv_mod_w"][None], var["kv_mod_w"][None], name="kv_mod_w_update")
    ]

    ps_grad = lax.dynamic_slice_in_dim(summed[small_rows : small_rows + n_a * d // LANES].reshape(n_a, d), chip * (d // N_CHIPS), d // N_CHIPS, axis=1)
    small_names = SMALL + ("pool_scale",)

    def pack_small(tree):
        return _pad_rows(jnp.concatenate([_rows(tree[n]) for n in small_names]))

    g_small = _pad_rows(jnp.concatenate([summed[:small_rows], _rows(ps_grad)]))
    small_out = _adamw(pack_small(wts), g_small, pack_small(mom), pack_small(var), name="small_update")
    row = 0
    for n in small_names:
        nrow = wts[n].size // LANES
        results[n] = [r[row : row + nrow].reshape(wts[n].shape) for r in small_out]
        row += nrow

    assert attn_slots is not None
    theirs = _run(_pair_plan([grad_full["pool_w"]], [geoms["pool_w"]]), name="reduce_pair")
    pool_part = _pair_add(grad_full["pool_w"], theirs[0], geoms["pool_w"], core_arr, name="pair_add_pool_w")
    got = _run(_chips_plan([pool_part] + in_flight), name="reduce_chips")
    parts = dict(zip(attn_names, attn_parts), pool_w=pool_part)
    slots = dict(zip(attn_names, attn_slots), pool_w=got[0])
    boths = [_chip_sum(parts[n], slots[n], geoms[n], place_arr, name="chip_sum_" + n) for n in mixer_names]
    sum_chips(0, [0, 1], in_flight, got[1:])
    prefixes = [[()]] * len(mixer_names) + [[(layer,) for layer in range(depth)]] * len(ffn_names)
    joined = _run(_join_plan(boths + ffn_both, prefixes), name="join_pair")
    for n, both in zip(mixer_names + ffn_names, joined):
        cs = geoms[n].cs
        out = _adamw(wts[n].reshape(-1, cs), both.reshape(-1, cs), mom[n].reshape(-1, cs), var[n].reshape(-1, cs), name="update_" + n)
        results[n] = [r.reshape(wts[n].shape) for r in out]

    outs = [loss, grad_x]
    for k in range(4):
        outs += [results[n][k] for n in WEIGHTS]
    return tuple(outs)


def kernel(x, c, positions, mod_w, mod_b, norm_mix, norm_ffn, pool_w, pool_scale, kv_mod_w, kv_mod_b, kv_in_norm, w_dkv, kv_norm, w_uk, w_uv, w_kr, w_dq, q_norm, w_uq, w_o, ffn_gate, ffn_up, ffn_down, final_norm, loss_target, m_mod_w, m_mod_b, m_norm_mix, m_norm_ffn, m_pool_w, m_pool_scale, m_kv_mod_w, m_kv_mod_b, m_kv_in_norm, m_w_dkv, m_kv_norm, m_w_uk, m_w_uv, m_w_kr, m_w_dq, m_q_norm, m_w_uq, m_w_o, m_ffn_gate, m_ffn_up, m_ffn_down, m_final_norm, v_mod_w, v_mod_b, v_norm_mix, v_norm_ffn, v_pool_w, v_pool_scale, v_kv_mod_w, v_kv_mod_b, v_kv_in_norm, v_w_dkv, v_kv_norm, v_w_uk, v_w_uv, v_w_kr, v_w_dq, v_q_norm, v_w_uq, v_w_o, v_ffn_gate, v_ffn_up, v_ffn_down, v_final_norm):
    given = dict(locals())
    wts = {n: given[n] for n in WEIGHTS}
    mom = {n: given["m_" + n] for n in WEIGHTS}
    var = {n: given["v_" + n] for n in WEIGHTS}
    return _step(x, c, positions, loss_target, wts, mom, var)
```

```python
import functools

import jax
import jax.numpy as jnp
from jax import lax
from jax.experimental import pallas as pl
from jax.experimental.pallas import tpu as pltpu

F32 = jnp.float32
BF16 = jnp.bfloat16
MESH = pl.DeviceIdType.MESH
ANY = pl.BlockSpec(memory_space=pl.ANY)

NORM_EPS = 1e-6
POOL_WINDOWS = (2, 4, 8, 16)
NOPE_DIM = 128
ROPE_DIM = 64
V_DIM = 128
HEAD_PAD = 256
SM_SCALE = (NOPE_DIM + ROPE_DIM) ** -0.5
ROPE_THETA = 10000.0
N_MOD = 6
ADAM_LR, ADAM_B1, ADAM_B2, ADAM_EPS, ADAM_WD, ADAM_STEP = 0.001, 0.9, 0.999, 1e-08, 0.01, 10
N_CHIPS = 4
N_DEV = 8
LANES = 128
HALO = 128
VMEM_LIMIT = 48 * 1024 * 1024


def _tile(dim, pref, align):
    if dim <= pref:
        return dim
    t = (pref // align) * align
    while t >= align:
        if dim % t == 0:
            return t
        t -= align
    return dim


def _params(*sem):
    return pltpu.CompilerParams(dimension_semantics=sem, vmem_limit_bytes=VMEM_LIMIT)


class _Hosted:
    def __init__(self, args, out_shapes, aliases, sem_shapes, start, finish):
        self.args, self.out_shapes, self.aliases, self.sem_shapes = list(args), list(out_shapes), dict(aliases), list(sem_shapes)
        self.start, self.finish = start, finish


def _call(body, *, name, grid, in_specs, out_specs, out_shape, args, sem, scratch_shapes=(), hosted=None):
    n_in, n_out, n_scr = len(args), len(out_shape), len(scratch_shapes)
    if hosted is None:
        outs = pl.pallas_call(
            body, name=name, grid=grid, in_specs=list(in_specs), out_specs=list(out_specs), out_shape=list(out_shape),
            scratch_shapes=list(scratch_shapes), compiler_params=_params(*sem),
        )(*args)
        return list(outs), []
    n_hin, n_hout = len(hosted.args), len(hosted.out_shapes)

    def carrying(*refs):
        own_in, their_in = refs[:n_in], refs[n_in : n_in + n_hin]
        refs = refs[n_in + n_hin :]
        own_out, their_out = refs[:n_out], refs[n_out : n_out + n_hout]
        refs = refs[n_out + n_hout :]
        own_scratch, sems = refs[:n_scr], refs[n_scr:]
        ids = [pl.program_id(axis) for axis in range(len(grid))]
        first = functools.reduce(jnp.logical_and, [i == 0 for i in ids])
        last = functools.reduce(jnp.logical_and, [i == size - 1 for i, size in zip(ids, grid)])

        @pl.when(first)
        def _():
            hosted.start(their_in, their_out, sems)

        body(*own_in, *own_out, *own_scratch)

        @pl.when(last)
        def _():
            hosted.finish(their_in, their_out, sems)

    outs = pl.pallas_call(
        carrying,
        name=name,
        grid=grid,
        in_specs=list(in_specs) + [ANY] * n_hin,
        out_specs=list(out_specs) + [ANY] * n_hout,
        out_shape=list(out_shape) + hosted.out_shapes,
        input_output_aliases={n_in + i: n_out + o for i, o in hosted.aliases.items()},
        scratch_shapes=list(scratch_shapes) + hosted.sem_shapes,
        compiler_params=_params(*["arbitrary"] * len(grid)),
    )(*args, *hosted.args)
    return list(outs[:n_out]), list(outs[n_out:])


def _merge(plans):
    if len(plans) == 1:
        return plans[0]
    args, out_shapes, aliases, sem_shapes, spans = [], [], {}, [], []
    for p in plans:
        spans.append((len(args), len(out_shapes), len(sem_shapes)))
        aliases.update({len(args) + i: len(out_shapes) + o for i, o in p.aliases.items()})
        args, out_shapes, sem_shapes = args + p.args, out_shapes + p.out_shapes, sem_shapes + p.sem_shapes

    def each(method, ins, outs, sems):
        for p, (a0, o0, s0) in zip(plans, spans):
            getattr(p, method)(ins[a0 : a0 + len(p.args)], outs[o0 : o0 + len(p.out_shapes)], sems[s0 : s0 + len(p.sem_shapes)])

    return _Hosted(args, out_shapes, aliases, sem_shapes, functools.partial(each, "start"), functools.partial(each, "finish"))


def _run(plan, *, name):
    n_in, n_out = len(plan.args), len(plan.out_shapes)

    def body(*refs):
        ins, outs, sems = refs[:n_in], refs[n_in : n_in + n_out], refs[n_in + n_out :]
        plan.start(ins, outs, sems)
        plan.finish(ins, outs, sems)

    return pl.pallas_call(
        body, name=name, in_specs=[ANY] * n_in, out_specs=[ANY] * n_out, out_shape=plan.out_shapes,
        input_output_aliases=plan.aliases, scratch_shapes=plan.sem_shapes,
    )(*plan.args)


def _mm(a, b, *, name, ta=False, tb=False, out_dtype=F32, b_idx=None, hosted=None, tm=1024, tn=1024, tk=None):
    m, k = (a.shape[1], a.shape[0]) if ta else a.shape
    b2 = b.shape if b_idx is None else b.shape[1:]
    kb, n = (b2[1], b2[0]) if tb else b2
    assert k == kb, (a.shape, b.shape, ta, tb)
    tm = _tile(m, tm, LANES)
    tn = _tile(n, tn, LANES)
    tk = _tile(k, 2048 if tk is None else tk, LANES)
    nk = k // tk
    dims = (((0 if ta else 1,), (1 if tb else 0,)), ((), ()))

    def body(a_ref, b_ref, o_ref, *acc):
        part = lax.dot_general(a_ref[...].astype(BF16), b_ref[...].astype(BF16), dims, preferred_element_type=F32)
        if nk == 1:
            o_ref[...] = part.astype(o_ref.dtype)
        else:
            acc_ref = acc[0]
            step = pl.program_id(2)

            @pl.when(step == 0)
            def _():
                acc_ref[...] = part

            @pl.when(step > 0)
            def _():
                acc_ref[...] += part

            @pl.when(step == nk - 1)
            def _():
                o_ref[...] = acc_ref[...].astype(o_ref.dtype)

    a_spec = pl.BlockSpec((tk, tm), lambda i, j, s: (s, i)) if ta else pl.BlockSpec((tm, tk), lambda i, j, s: (i, s))
    if b_idx is None:
        b_spec = pl.BlockSpec((tn, tk), lambda i, j, s: (j, s)) if tb else pl.BlockSpec((tk, tn), lambda i, j, s: (s, j))
    elif tb:
        b_spec = pl.BlockSpec((None, tn, tk), lambda i, j, s: (b_idx, j, s))
    else:
        b_spec = pl.BlockSpec((None, tk, tn), lambda i, j, s: (b_idx, s, j))
    outs, carried = _call(
        body,
        name=name,
        grid=(m // tm, n // tn, nk),
        in_specs=[a_spec, b_spec],
        out_specs=[pl.BlockSpec((tm, tn), lambda i, j, s: (i, j))],
        out_shape=[jax.ShapeDtypeStruct((m, n), out_dtype)],
        args=[a, b],
        sem=("parallel", "parallel", "arbitrary"),
        scratch_shapes=[pltpu.VMEM((tm, tn), F32)] if nk > 1 else [],
        hosted=hosted,
    )
    return outs[0] if hosted is None else (outs[0], carried)


def _tp_fwd(sc16, w, bias, *, name):
    nl, d, n = w.shape
    tn = _tile(n, 512, LANES)

    def body(sc_ref, w_ref, b_ref, o_ref):
        o_ref[0] = jnp.dot(sc_ref[...].astype(BF16), w_ref[0].astype(BF16), preferred_element_type=F32) + b_ref[0]

    return pl.pallas_call(
        body,
        name=name,
        grid=(nl, n // tn),
        in_specs=[
            pl.BlockSpec((16, d), lambda l, j: (0, 0)),
            pl.BlockSpec((1, d, tn), lambda l, j: (l, 0, j)),
            pl.BlockSpec((1, 1, tn), lambda l, j: (l, 0, j)),
        ],
        out_specs=pl.BlockSpec((1, 16, tn), lambda l, j: (l, 0, j)),
        out_shape=jax.ShapeDtypeStruct((nl, 16, n), F32),
        compiler_params=_params("parallel", "parallel"),
    )(sc16, w, bias)


_NT = (((1,), (1,)), ((), ()))
_TN = (((0,), (0,)), ((), ()))


def _silu_parts(a):
    sig = jax.nn.sigmoid(a)
    return a * sig, sig * (1.0 + a * (1.0 - sig))


def _ffn_in(h, w_gate, w_up, layer, *, name, hosted=None):
    s, d = h.shape
    f = w_gate.shape[2]
    tm, tn = _tile(s, 1024, LANES), _tile(f, 512, LANES)

    def body(h_ref, g_ref, u_ref, ga_ref, gb_ref, z_ref):
        hv = h_ref[...]
        a = jnp.dot(hv, g_ref[...], preferred_element_type=F32)
        b = jnp.dot(hv, u_ref[...], preferred_element_type=F32)
        silu, dsilu = _silu_parts(a)
        ga_ref[...] = (b * dsilu).astype(ga_ref.dtype)
        gb_ref[...] = silu.astype(gb_ref.dtype)
        z_ref[...] = (silu * b).astype(z_ref.dtype)

    w_spec = pl.BlockSpec((None, d, tn), lambda i, j: (layer, 0, j))
    out = pl.BlockSpec((tm, tn), lambda i, j: (i, j))
    return _call(
        body,
        name=name,
        grid=(s // tm, f // tn),
        in_specs=[pl.BlockSpec((tm, d), lambda i, j: (i, 0)), w_spec, w_spec],
        out_specs=[out] * 3,
        out_shape=[jax.ShapeDtypeStruct((s, f), BF16)] * 3,
        args=[h, w_gate, w_up],
        sem=("parallel", "parallel"),
        hosted=hosted,
    )


def _ffn_down_bwd(df, w_down, layer, dz_da, dz_db, *, name, hosted=None):
    s, d = df.shape
    f = w_down.shape[1]
    tm, tn = _tile(s, 1024, LANES), _tile(f, 512, LANES)

    def body(df_ref, w_ref, ga_ref, gb_ref, da_ref, db_ref):
        dz = lax.dot_general(df_ref[...], w_ref[...], _NT, preferred_element_type=F32)
        da_ref[...] = (dz * ga_ref[...].astype(F32)).astype(da_ref.dtype)
        db_ref[...] = (dz * gb_ref[...].astype(F32)).astype(db_ref.dtype)

    blk = pl.BlockSpec((tm, tn), lambda i, j: (i, j))
    return _call(
        body,
        name=name,
        grid=(s // tm, f // tn),
        in_specs=[pl.BlockSpec((tm, d), lambda i, j: (i, 0)), pl.BlockSpec((None, tn, d), lambda i, j: (layer, j, 0)), blk, blk],
        out_specs=[blk, blk],
        out_shape=[jax.ShapeDtypeStruct((s, f), BF16)] * 2,
        args=[df, w_down, dz_da, dz_db],
        sem=("parallel", "parallel"),
        hosted=hosted,
    )


def _ffn_in_dx(da, db, w_gate, w_up, layer, *, name, hosted=None):
    s, f = da.shape
    d = w_gate.shape[1]
    tm, tn, tk = _tile(s, 512, LANES), _tile(d, 1024, LANES), _tile(f, 3072, LANES)
    nk = f // tk

    def body(da_ref, db_ref, g_ref, u_ref, o_ref, acc_ref):
        part = lax.dot_general(da_ref[...], g_ref[...], _NT, preferred_element_type=F32)
        part = part + lax.dot_general(db_ref[...], u_ref[...], _NT, preferred_element_type=F32)
        step = pl.program_id(2)

        @pl.when(step == 0)
        def _():
            acc_ref[...] = part

        @pl.when(step > 0)
        def _():
            acc_ref[...] += part

        @pl.when(step == nk - 1)
        def _():
            o_ref[...] = acc_ref[...]

    x_spec = pl.BlockSpec((tm, tk), lambda i, j, k: (i, k))
    w_spec = pl.BlockSpec((None, tn, tk), lambda i, j, k: (layer, j, k))
    return _call(
        body,
        name=name,
        grid=(s // tm, d // tn, nk),
        in_specs=[x_spec, x_spec, w_spec, w_spec],
        out_specs=[pl.BlockSpec((tm, tn), lambda i, j, k: (i, j))],
        out_shape=[jax.ShapeDtypeStruct((s, d), F32)],
        args=[da, db, w_gate, w_up],
        sem=("parallel", "parallel", "arbitrary"),
        scratch_shapes=[pltpu.VMEM((tm, tn), F32)],
        hosted=hosted,
    )


def _ffn_in_dw(h, da, db, *, name, hosted=None):
    s, d = h.shape
    f = da.shape[1]
    tm, tn, tk = _tile(d, 1024, LANES), _tile(f, 512, LANES), _tile(s, 4096, LANES)
    nk = s // tk

    def body(h_ref, da_ref, db_ref, g_ref, u_ref, *acc):
        hv = h_ref[...]
        pg = lax.dot_general(hv, da_ref[...], _TN, preferred_element_type=F32)
        pu = lax.dot_general(hv, db_ref[...], _TN, preferred_element_type=F32)
        if nk == 1:
            g_ref[...] = pg.astype(g_ref.dtype)
            u_ref[...] = pu.astype(u_ref.dtype)
            return
        g_acc, u_acc = acc
        step = pl.program_id(2)

        @pl.when(step == 0)
        def _():
            g_acc[...] = pg
            u_acc[...] = pu

        @pl.when(step > 0)
        def _():
            g_acc[...] += pg
            u_acc[...] += pu

        @pl.when(step == nk - 1)
        def _():
            g_ref[...] = g_acc[...].astype(g_ref.dtype)
            u_ref[...] = u_acc[...].astype(u_ref.dtype)

    y_spec = pl.BlockSpec((tk, tn), lambda i, j, k: (k, j))
    out = pl.BlockSpec((tm, tn), lambda i, j, k: (i, j))
    return _call(
        body,
        name=name,
        grid=(d // tm, f // tn, nk),
        in_specs=[pl.BlockSpec((tk, tm), lambda i, j, k: (k, i)), y_spec, y_spec],
        out_specs=[out, out],
        out_shape=[jax.ShapeDtypeStruct((d, f), BF16)] * 2,
        args=[h, da, db],
        sem=("parallel", "parallel", "arbitrary"),
        scratch_shapes=[pltpu.VMEM((tm, tn), F32)] * 2 if nk > 1 else [],
        hosted=hosted,
    )


def _gmm(a, w, *, name, mode, out_dtype):
    s = a.shape[0]
    g = len(POOL_WINDOWS)
    c = a.shape[1] // g
    tr = _tile(s, 1024, LANES)
    n_row = s // tr

    if mode == "tn":

        def body(a_ref, b_ref, o_ref, acc_ref):
            part = lax.dot_general(a_ref[...].astype(BF16), b_ref[...].astype(BF16), (((0,), (0,)), ((), ())), preferred_element_type=F32)

            @pl.when(pl.program_id(1) == 0)
            def _():
                acc_ref[...] = part

            @pl.when(pl.program_id(1) > 0)
            def _():
                acc_ref[...] += part

            @pl.when(pl.program_id(1) == n_row - 1)
            def _():
                o_ref[0] = acc_ref[...].astype(o_ref.dtype)

        return pl.pallas_call(
            body,
            name=name,
            grid=(g, n_row),
            in_specs=[pl.BlockSpec((tr, c), lambda gi, i: (i, gi)), pl.BlockSpec((tr, c), lambda gi, i: (i, gi))],
            out_specs=pl.BlockSpec((1, c, c), lambda gi, i: (gi, 0, 0)),
            out_shape=jax.ShapeDtypeStruct((g, c, c), out_dtype),
            scratch_shapes=[pltpu.VMEM((c, c), F32)],
            compiler_params=_params("parallel", "arbitrary"),
        )(a, w)

    dims = (((1,), (0 if mode == "nn" else 1,)), ((), ()))

    def body(a_ref, w_ref, o_ref):
        o_ref[...] = lax.dot_general(a_ref[...].astype(BF16), w_ref[0].astype(BF16), dims, preferred_element_type=F32).astype(o_ref.dtype)

    return pl.pallas_call(
        body,
        name=name,
        grid=(g, n_row),
        in_specs=[pl.BlockSpec((tr, c), lambda gi, i: (i, gi)), pl.BlockSpec((1, c, c), lambda gi, i: (gi, 0, 0))],
        out_specs=pl.BlockSpec((tr, c), lambda gi, i: (i, gi)),
        out_shape=jax.ShapeDtypeStruct((s, g * c), out_dtype),
        compiler_params=_params("parallel", "parallel"),
    )(a, w)


def _row_tile(s, d):
    return _tile(s, max(8, (1 << 19) // d), 8)


def _norm_fwd(x, g, *, name, scale=None, shift=None, y=None, gate=None, out_dtype=BF16):
    s, d = x.shape
    tr = _row_tile(s, d)
    has_res, has_mod = y is not None, scale is not None

    def body(*refs):
        refs = list(refs)
        x_ref = refs.pop(0)
        xv = x_ref[...]
        if has_res:
            y_ref, gate_ref = refs.pop(0), refs.pop(0)
            xv = xv + gate_ref[...] * y_ref[...]
        g_ref = refs.pop(0)
        if has_mod:
            scale_ref, shift_ref = refs.pop(0), refs.pop(0)
        if has_res:
            refs.pop(0)[...] = xv
        h = xv * lax.rsqrt(jnp.mean(xv * xv, axis=-1, keepdims=True) + NORM_EPS)
        h = h * g_ref[...]
        if has_mod:
            h = h * (1.0 + scale_ref[...]) + shift_ref[...]
        refs.pop(0)[...] = h.astype(out_dtype)

    row = pl.BlockSpec((tr, d), lambda i: (i, 0))
    vec = pl.BlockSpec((1, d), lambda i: (0, 0))
    args, in_specs = [x], [row]
    if has_res:
        args += [y, gate]
        in_specs += [row, vec]
    args.append(g)
    in_specs.append(vec)
    if has_mod:
        args += [scale, shift]
        in_specs += [vec, vec]
    out_shape, out_specs = [], []
    if has_res:
        out_shape.append(jax.ShapeDtypeStruct((s, d), F32))
        out_specs.append(row)
    out_shape.append(jax.ShapeDtypeStruct((s, d), out_dtype))
    out_specs.append(row)
    res = pl.pallas_call(
        body, name=name, grid=(s // tr,), in_specs=in_specs, out_specs=out_specs, out_shape=out_shape, compiler_params=_params("parallel")
    )(*args)
    return (res[0], res[1]) if has_res else res[0]


def _norm_bwd(x, g, dh, *, name, scale=None, resid=None, hosted=None):
    s, d = x.shape
    tr = _row_tile(s, d)
    has_mod, has_res = scale is not None, resid is not None

    def body(*refs):
        refs = list(refs)
        x_ref, g_ref, dh_ref = refs.pop(0), refs.pop(0), refs.pop(0)
        scale_ref = refs.pop(0) if has_mod else None
        resid_ref = refs.pop(0) if has_res else None
        dx_ref, sums_ref = refs
        xv = x_ref[...]
        r = lax.rsqrt(jnp.mean(xv * xv, axis=-1, keepdims=True) + NORM_EPS)
        xhat = xv * r
        dh32 = dh_ref[...].astype(F32)
        gv = g_ref[...]
        dn = dh32 * (1.0 + scale_ref[...]) if has_mod else dh32
        dxhat = dn * gv
        dx = r * (dxhat - xhat * jnp.mean(dxhat * xhat, axis=-1, keepdims=True))
        if has_res:
            dx = dx + resid_ref[...]
        dx_ref[...] = dx

        @pl.when(pl.program_id(0) == 0)
        def _():
            sums_ref[...] = jnp.zeros_like(sums_ref)

        sums_ref[0:1, :] += jnp.sum(dh32, axis=0, keepdims=True)
        sums_ref[1:2, :] += jnp.sum(dh32 * (xhat * gv), axis=0, keepdims=True)
        sums_ref[2:3, :] += jnp.sum(dn * xhat, axis=0, keepdims=True)

    row = pl.BlockSpec((tr, d), lambda i: (i, 0))
    vec = pl.BlockSpec((1, d), lambda i: (0, 0))
    args, in_specs = [x, g, dh], [row, vec, row]
    if has_mod:
        args.append(scale)
        in_specs.append(vec)
    if has_res:
        args.append(resid)
        in_specs.append(row)
    outs, carried = _call(
        body,
        name=name,
        grid=(s // tr,),
        in_specs=in_specs,
        out_specs=[row, pl.BlockSpec((8, d), lambda i: (0, 0))],
        out_shape=[jax.ShapeDtypeStruct((s, d), F32), jax.ShapeDtypeStruct((8, d), F32)],
        args=args,
        sem=("arbitrary",),
        hosted=hosted,
    )
    return outs if hosted is None else (outs, carried)


def _gate_bwd(dx, y, gate, *, name):
    s, d = dx.shape
    tr = _row_tile(s, d)

    def body(dx_ref, y_ref, gate_ref, dy_ref, sums_ref):
        dxv = dx_ref[...]
        dy_ref[...] = (dxv * gate_ref[...]).astype(dy_ref.dtype)

        @pl.when(pl.program_id(0) == 0)
        def _():
            sums_ref[...] = jnp.zeros_like(sums_ref)

        sums_ref[0:1, :] += jnp.sum(dxv * y_ref[...], axis=0, keepdims=True)

    row = pl.BlockSpec((tr, d), lambda i: (i, 0))
    return pl.pallas_call(
        body,
        name=name,
        grid=(s // tr,),
        in_specs=[row, row, pl.BlockSpec((1, d), lambda i: (0, 0))],
        out_specs=[row, pl.BlockSpec((8, d), lambda i: (0, 0))],
        out_shape=[jax.ShapeDtypeStruct((s, d), BF16), jax.ShapeDtypeStruct((8, d), F32)],
        compiler_params=_params("arbitrary"),
    )(dx, y, gate)


def _elementwise(fn, args, out_dtypes, *, name):
    s, d = args[0].shape
    tc = d if d <= 2048 else _tile(d, 1024, LANES)
    tr = _tile(s, max(8, (1 << 18) // tc), 8)
    n_in = len(args)

    def body(*refs):
        outs = fn(*[r[...] for r in refs[:n_in]])
        for o_ref, o in zip(refs[n_in:], outs):
            o_ref[...] = o.astype(o_ref.dtype)

    spec = pl.BlockSpec((tr, tc), lambda i, j: (i, j))
    return pl.pallas_call(
        body,
        name=name,
        grid=(s // tr, d // tc),
        in_specs=[spec] * n_in,
        out_specs=[spec] * len(out_dtypes),
        out_shape=[jax.ShapeDtypeStruct((s, d), dt) for dt in out_dtypes],
        compiler_params=_params("parallel", "parallel"),
    )(*args)


def _silu(v):
    return (v * jax.nn.sigmoid(v),)


def _split3(v):
    hi = v.astype(BF16)
    r1 = v - hi.astype(F32)
    mid = r1.astype(BF16)
    lo = (r1 - mid.astype(F32)).astype(BF16)
    return hi, mid, lo


def _band_dot(band, v):
    return sum(jnp.dot(band, part, preferred_element_type=F32) for part in _split3(v))


def _pool(h, *, name, transpose, out_dtype):
    s, d = h.shape
    c = d // len(POOL_WINDOWS)
    tr = _tile(s, 256, HALO)
    per = tr // HALO
    n_halo = s // HALO

    def body(h_ref, halo_ref, o_ref):
        i = pl.program_id(0)
        out_row = i * tr + lax.broadcasted_iota(jnp.int32, (tr, tr + HALO), 0)
        col = lax.broadcasted_iota(jnp.int32, (tr, tr + HALO), 1)
        if transpose:
            ext = jnp.concatenate([h_ref[...], halo_ref[...]], axis=0)
            src_row = i * tr + col
            ext_row = i * tr + lax.broadcasted_iota(jnp.int32, (tr + HALO, 1), 0)
        else:
            ext = jnp.concatenate([halo_ref[...], h_ref[...]], axis=0)
            src_row = i * tr + col - HALO
            own_row = i * tr + lax.broadcasted_iota(jnp.int32, (tr, 1), 0)
        for gi, w in enumerate(POOL_WINDOWS):
            cols = slice(gi * c, (gi + 1) * c)
            if transpose:
                band = (src_row >= out_row) & (src_row < out_row + w) & (src_row < s)
                scaled = ext[:, cols] / jnp.minimum(ext_row + 1, w).astype(F32)
                res = _band_dot(band.astype(BF16), scaled) - h_ref[:, cols]
            else:
                band = (src_row <= out_row) & (src_row > out_row - w) & (src_row >= 0)
                res = _band_dot(band.astype(BF16), ext[:, cols]) / jnp.minimum(own_row + 1, w).astype(F32) - h_ref[:, cols]
            o_ref[:, cols] = res.astype(o_ref.dtype)

    if transpose:
        halo_map = lambda i: (jnp.minimum((i + 1) * per, n_halo - 1), 0)
    else:
        halo_map = lambda i: (jnp.maximum(i * per - 1, 0), 0)
    return pl.pallas_call(
        body,
        name=name,
        grid=(s // tr,),
        in_specs=[pl.BlockSpec((tr, d), lambda i: (i, 0)), pl.BlockSpec((HALO, d), halo_map)],
        out_specs=pl.BlockSpec((tr, d), lambda i: (i, 0)),
        out_shape=jax.ShapeDtypeStruct((s, d), out_dtype),
        compiler_params=_params("parallel"),
    )(h, h)


def _rotate(v, cos, sin):
    lane = lax.broadcasted_iota(jnp.int32, v.shape, 1)
    swapped = jnp.where(lane % ROPE_DIM < ROPE_DIM // 2, pltpu.roll(v, LANES - ROPE_DIM // 2, 1), pltpu.roll(v, ROPE_DIM // 2, 1))
    return v * cos + swapped * sin


def _rope_heads(x, cos, sin, *, name, out_dtype):
    s, n = x.shape
    tr = _tile(s, max(16, (1 << 18) // n), 16)

    def body(x_ref, cos_ref, sin_ref, o_ref):
        cos_v, sin_v = cos_ref[...], sin_ref[...]
        for j in range(n // LANES):
            lanes = slice(j * LANES, (j + 1) * LANES)
            if j % 2 == 0:
                o_ref[:, lanes] = x_ref[:, lanes].astype(o_ref.dtype)
            else:
                o_ref[:, lanes] = _rotate(x_ref[:, lanes].astype(F32), cos_v, sin_v).astype(o_ref.dtype)

    blk = pl.BlockSpec((tr, n), lambda i: (i, 0))
    tab = pl.BlockSpec((tr, LANES), lambda i: (i, 0))
    return pl.pallas_call(
        body,
        name=name,
        grid=(s // tr,),
        in_specs=[blk, tab, tab],
        out_specs=blk,
        out_shape=jax.ShapeDtypeStruct((s, n), out_dtype),
        compiler_params=_params("parallel"),
    )(x, cos, sin)


def _build_keys(kv, kr_pre, cos, sin, *, name):
    s, n = kv.shape
    heads = n // HEAD_PAD
    t = _attn_tile(s)

    def body(kv_ref, kr_ref, cos_ref, sin_ref, keys_ref, kt_ref, vt_ref):
        rope = _rotate(kr_ref[...], cos_ref[...], sin_ref[...])
        nope, val = kv_ref[:, :NOPE_DIM], kv_ref[:, NOPE_DIM:]
        keys_ref[:, :NOPE_DIM] = nope
        keys_ref[:, NOPE_DIM:] = rope.astype(keys_ref.dtype)
        kt_ref[0, 0, :NOPE_DIM, :] = nope.astype(F32).T.astype(kt_ref.dtype)
        kt_ref[0, 0, NOPE_DIM:, :] = rope.T.astype(kt_ref.dtype)
        vt_ref[0, 0] = val.astype(F32).T.astype(vt_ref.dtype)

    tab = pl.BlockSpec((t, LANES), lambda hd, kb: (kb, 0))
    return pl.pallas_call(
        body,
        name=name,
        grid=(heads, s // t),
        in_specs=[pl.BlockSpec((t, HEAD_PAD), lambda hd, kb: (kb, hd)), tab, tab, tab],
        out_specs=[
            pl.BlockSpec((t, HEAD_PAD), lambda hd, kb: (kb, hd)),
            pl.BlockSpec((1, 1, HEAD_PAD, t), lambda hd, kb: (hd, kb, 0, 0)),
            pl.BlockSpec((1, 1, V_DIM, t), lambda hd, kb: (hd, kb, 0, 0)),
        ],
        out_shape=[
            jax.ShapeDtypeStruct((s, n), BF16),
            jax.ShapeDtypeStruct((heads, s // t, HEAD_PAD, t), BF16),
            jax.ShapeDtypeStruct((heads, s // t, V_DIM, t), BF16),
        ],
        compiler_params=_params("parallel", "parallel"),
    )(kv, kr_pre, cos, sin)


def _rope_back(dq_t, cos, sin, *, name):
    heads, nq, _, t = dq_t.shape

    def body(x_ref, cos_ref, sin_ref, o_ref):
        x = x_ref[0, 0].T
        o_ref[:, :NOPE_DIM] = x[:, :NOPE_DIM].astype(o_ref.dtype)
        o_ref[:, NOPE_DIM:] = _rotate(x[:, NOPE_DIM:], cos_ref[...], sin_ref[...]).astype(o_ref.dtype)

    tab = pl.BlockSpec((t, LANES), lambda hd, qb: (qb, 0))
    return pl.pallas_call(
        body,
        name=name,
        grid=(heads, nq),
        in_specs=[pl.BlockSpec((1, 1, HEAD_PAD, t), lambda hd, qb: (hd, qb, 0, 0)), tab, tab],
        out_specs=pl.BlockSpec((t, HEAD_PAD), lambda hd, qb: (qb, hd)),
        out_shape=jax.ShapeDtypeStruct((nq * t, heads * HEAD_PAD), BF16),
        compiler_params=_params("parallel", "parallel"),
    )(dq_t, cos, sin)


def _keys_bwd(dk_a, dk_b, dv_a, dv_b, cos, sin_neg, *, name):
    s, n = dk_a.shape
    heads = n // HEAD_PAD
    tr = _tile(s, 512, 8)

    def body(dka_ref, dkb_ref, dva_ref, dvb_ref, cos_ref, sin_ref, dkv_ref, dkr_ref):
        hd = pl.program_id(1)
        dk = dka_ref[...] + dkb_ref[...]
        dkv_ref[:, :NOPE_DIM] = dk[:, :NOPE_DIM].astype(dkv_ref.dtype)
        dkv_ref[:, NOPE_DIM:] = (dva_ref[...] + dvb_ref[...]).astype(dkv_ref.dtype)

        @pl.when(hd == 0)
        def _():
            dkr_ref[...] = dk[:, NOPE_DIM:]

        @pl.when(hd > 0)
        def _():
            dkr_ref[...] += dk[:, NOPE_DIM:]

        @pl.when(hd == heads - 1)
        def _():
            dkr_ref[...] = _rotate(dkr_ref[...], cos_ref[...], sin_ref[...])

    dk_blk = pl.BlockSpec((tr, HEAD_PAD), lambda i, hd: (i, hd))
    dv_blk = pl.BlockSpec((tr, V_DIM), lambda i, hd: (i, hd))
    tab = pl.BlockSpec((tr, LANES), lambda i, hd: (i, 0))
    return pl.pallas_call(
        body,
        name=name,
        grid=(s // tr, heads),
        in_specs=[dk_blk, dk_blk, dv_blk, dv_blk, tab, tab],
        out_specs=[dk_blk, tab],
        out_shape=[jax.ShapeDtypeStruct((s, n), BF16), jax.ShapeDtypeStruct((s, LANES), F32)],
        compiler_params=_params("parallel", "arbitrary"),
    )(dk_a, dk_b, dv_a, dv_b, cos, sin_neg)


def _attn_tile(s):
    return _tile(s, 512, LANES)


def _causal_mask(t):
    return lax.broadcasted_iota(jnp.int32, (t, t), 0) <= lax.broadcasted_iota(jnp.int32, (t, t), 1)


def _attn_fwd(q, keys, v_t, *, name, hosted=None):
    s = q.shape[0]
    heads = q.shape[1] // HEAD_PAD
    t = _attn_tile(s)
    nq = s // t

    def body(q_ref, k_ref, v_ref, o_ref, lse_ref, m_ref, l_ref, acc_ref):
        qi = pl.program_id(1)
        qv = q_ref[...]
        m_ref[...] = jnp.full_like(m_ref, -jnp.inf)
        l_ref[...] = jnp.zeros_like(l_ref)
        acc_ref[...] = jnp.zeros_like(acc_ref)

        def block(kb, diagonal):
            rows = pl.ds(pl.multiple_of(kb * t, t), t)
            sc_t = lax.dot_general(k_ref[rows, :], qv, _NT, preferred_element_type=F32) * SM_SCALE
            if diagonal:
                sc_t = jnp.where(_causal_mask(t), sc_t, -jnp.inf)
            m_old = m_ref[...]
            m_new = jnp.maximum(m_old, jnp.max(sc_t, axis=0, keepdims=True))
            alpha = jnp.exp(m_old - m_new)
            p_t = jnp.exp(sc_t - m_new)
            l_ref[...] = alpha * l_ref[...] + jnp.sum(p_t, axis=0, keepdims=True)
            acc_ref[...] = alpha * acc_ref[...] + jnp.dot(v_ref[0, kb], p_t.astype(BF16), preferred_element_type=F32)
            m_ref[...] = m_new

        def earlier(kb, carry):
            block(kb, False)
            return carry

        lax.fori_loop(0, qi, earlier, 0)
        block(qi, True)
        o_ref[...] = (acc_ref[...] / l_ref[...]).T.astype(o_ref.dtype)
        lse_ref[0, 0] = m_ref[...] + jnp.log(l_ref[...])

    return _call(
        body,
        name=name,
        grid=(heads, nq),
        in_specs=[
            pl.BlockSpec((t, HEAD_PAD), lambda hd, qi: (qi, hd)),
            pl.BlockSpec((s, HEAD_PAD), lambda hd, qi: (0, hd)),
            pl.BlockSpec((1, nq, V_DIM, t), lambda hd, qi: (hd, 0, 0, 0)),
        ],
        out_specs=[pl.BlockSpec((t, V_DIM), lambda hd, qi: (qi, hd)), pl.BlockSpec((1, 1, 1, t), lambda hd, qi: (hd, qi, 0, 0))],
        out_shape=[jax.ShapeDtypeStruct((s, heads * V_DIM), BF16), jax.ShapeDtypeStruct((heads, nq, 1, t), F32)],
        args=[q, keys, v_t],
        sem=("parallel", "parallel"),
        scratch_shapes=[pltpu.VMEM((1, t), F32), pltpu.VMEM((1, t), F32), pltpu.VMEM((V_DIM, t), F32)],
        hosted=hosted,
    )


def _attn_bwd(q, keys, keys_t, kv, o, do, lse, *, name, hosted=None):
    s = q.shape[0]
    heads = q.shape[1] // HEAD_PAD
    t = _attn_tile(s)
    nq = s // t

    def body(q_ref, k_ref, kt_ref, v_ref, o_ref, do_ref, lse_ref, dq_ref, dk_ref, dv_ref, dk_acc, dv_acc, delta_ref):
        ki = pl.program_id(1)

        @pl.when(ki == 0)
        def _():
            dq_ref[...] = jnp.zeros_like(dq_ref)
            ones = jnp.ones((8, V_DIM), BF16)

            def row_sums(qb, carry):
                rows = pl.ds(pl.multiple_of(qb * t, t), t)
                prod = do_ref[rows, :].astype(F32) * o_ref[rows, :].astype(F32)
                sums = sum(lax.dot_general(ones, part, _NT, preferred_element_type=F32) for part in _split3(prod))
                delta_ref[qb] = sums[0:1]
                return carry

            lax.fori_loop(0, nq, row_sums, 0)

        kv_, vv, kt = k_ref[...], v_ref[...], kt_ref[0, 0]
        dk_acc[...] = jnp.zeros_like(dk_acc)
        dv_acc[...] = jnp.zeros_like(dv_acc)

        def block(qb, diagonal):
            rows = pl.ds(pl.multiple_of(qb * t, t), t)
            qv, dov = q_ref[rows, :], do_ref[rows, :]
            sc_t = lax.dot_general(kv_, qv, _NT, preferred_element_type=F32) * SM_SCALE
            p_t = jnp.exp(sc_t - lse_ref[0, qb])
            if diagonal:
                p_t = jnp.where(_causal_mask(t), p_t, 0.0)
            dv_acc[...] += jnp.dot(p_t.astype(BF16), dov, preferred_element_type=F32)
            dp_t = lax.dot_general(vv, dov, _NT, preferred_element_type=F32)
            ds_t = (p_t * (dp_t - delta_ref[qb]) * SM_SCALE).astype(BF16)
            dk_acc[...] += jnp.dot(ds_t, qv, preferred_element_type=F32)
            dq_ref[0, qb] += jnp.dot(kt, ds_t, preferred_element_type=F32)

        def later(qb, carry):
            block(qb, False)
            return carry

        block(ki, True)
        lax.fori_loop(ki + 1, nq, later, 0)
        dk_ref[...] = dk_acc[...]
        dv_ref[...] = dv_acc[...]

    whole = pl.BlockSpec((s, V_DIM), lambda hd, ki: (0, hd))
    return _call(
        body,
        name=name,
        grid=(heads, nq),
        in_specs=[
            pl.BlockSpec((s, HEAD_PAD), lambda hd, ki: (0, hd)),
            pl.BlockSpec((t, HEAD_PAD), lambda hd, ki: (ki, hd)),
            pl.BlockSpec((1, 1, HEAD_PAD, t), lambda hd, ki: (hd, ki, 0, 0)),
            pl.BlockSpec((t, V_DIM), lambda hd, ki: (ki, 2 * hd + 1)),
            whole,
            whole,
            pl.BlockSpec((1, nq, 1, t), lambda hd, ki: (hd, 0, 0, 0)),
        ],
        out_specs=[
            pl.BlockSpec((1, nq, HEAD_PAD, t), lambda hd, ki: (hd, 0, 0, 0)),
            pl.BlockSpec((t, HEAD_PAD), lambda hd, ki: (ki, hd)),
            pl.BlockSpec((t, V_DIM), lambda hd, ki: (ki, hd)),
        ],
        out_shape=[
            jax.ShapeDtypeStruct((heads, nq, HEAD_PAD, t), F32),
            jax.ShapeDtypeStruct((s, heads * HEAD_PAD), F32),
            jax.ShapeDtypeStruct((s, heads * V_DIM), F32),
        ],
        args=[q, keys, keys_t, kv, o, do, lse],
        sem=("parallel", "arbitrary"),
        scratch_shapes=[pltpu.VMEM((t, HEAD_PAD), F32), pltpu.VMEM((t, V_DIM), F32), pltpu.VMEM((nq, 1, t), F32)],
        hosted=hosted,
    )


def _loss_bwd(x, y, gate, g, target, *, name):
    s, d = x.shape
    tr = _row_tile(s, d)

    def body(x_ref, y_ref, gate_ref, g_ref, t_ref, dx_ref, stats_ref, loss_ref):
        xv = x_ref[...] + gate_ref[...] * y_ref[...]
        r = lax.rsqrt(jnp.mean(xv * xv, axis=-1, keepdims=True) + NORM_EPS)
        xhat = xv * r
        gv = g_ref[...]
        err = xhat * gv - t_ref[...]
        dy = err / d
        dxhat = dy * gv
        dx_ref[...] = r * (dxhat - xhat * jnp.mean(dxhat * xhat, axis=-1, keepdims=True))

        @pl.when(pl.program_id(0) == 0)
        def _():
            stats_ref[...] = jnp.zeros_like(stats_ref)
            loss_ref[...] = jnp.zeros_like(loss_ref)

        stats_ref[0:1, :] += jnp.sum(dy * xhat, axis=0, keepdims=True)
        loss_ref[...] += 0.5 * jnp.sum(jnp.mean(err * err, axis=-1, keepdims=True))

    row = pl.BlockSpec((tr, d), lambda i: (i, 0))
    vec = pl.BlockSpec((1, d), lambda i: (0, 0))
    return pl.pallas_call(
        body,
        name=name,
        grid=(s // tr,),
        in_specs=[row, row, vec, vec, row],
        out_specs=[row, pl.BlockSpec((8, d), lambda i: (0, 0)), pl.BlockSpec((8, LANES), lambda i: (0, 0))],
        out_shape=[jax.ShapeDtypeStruct((s, d), F32), jax.ShapeDtypeStruct((8, d), F32), jax.ShapeDtypeStruct((8, LANES), F32)],
        compiler_params=_params("arbitrary"),
    )(x, y, gate, g, target)


def _adam_math(w, g, m, v):
    new_m = ADAM_B1 * m + (1.0 - ADAM_B1) * g
    new_v = ADAM_B2 * v + (1.0 - ADAM_B2) * (g * g)
    m_hat = new_m / (1.0 - ADAM_B1**ADAM_STEP)
    v_hat = new_v / (1.0 - ADAM_B2**ADAM_STEP)
    return -ADAM_LR * (m_hat / (jnp.sqrt(v_hat) + ADAM_EPS) + ADAM_WD * w), new_m, new_v


def _adamw(w, g, m, v, *, name):
    rows, cols = w.shape
    tr = _tile(rows, max(8, (1 << 18) // cols), 8)

    def body(w_ref, g_ref, m_ref, v_ref, go_ref, d_ref, mo_ref, vo_ref):
        gv = g_ref[...]
        go_ref[...] = gv
        d_ref[...], mo_ref[...], vo_ref[...] = _adam_math(w_ref[...], gv, m_ref[...], v_ref[...])

    spec = pl.BlockSpec((tr, cols), lambda i: (i, 0))
    return pl.pallas_call(
        body,
        name=name,
        grid=(rows // tr,),
        in_specs=[spec] * 4,
        out_specs=[spec] * 4,
        out_shape=[jax.ShapeDtypeStruct((rows, cols), F32)] * 4,
        compiler_params=_params("parallel"),
    )(w, g, m, v)


def _tp_adamw(sc16, dm, w, m, v, *, name):
    nl, d, n = w.shape
    tm = _tile(d, 512, LANES)
    tn = _tile(n, 1024, LANES)

    def body(sc_ref, dm_ref, w_ref, m_ref, v_ref, go_ref, d_ref, mo_ref, vo_ref):
        gv = lax.dot_general(sc_ref[...].astype(BF16), dm_ref[0].astype(BF16), (((0,), (0,)), ((), ())), preferred_element_type=F32)
        go_ref[0] = gv
        d_ref[0], mo_ref[0], vo_ref[0] = _adam_math(w_ref[0], gv, m_ref[0], v_ref[0])

    blk = pl.BlockSpec((1, tm, tn), lambda l, i, j: (l, i, j))
    return pl.pallas_call(
        body,
        name=name,
        grid=(nl, d // tm, n // tn),
        in_specs=[pl.BlockSpec((16, tm), lambda l, i, j: (0, i)), pl.BlockSpec((1, 16, tn), lambda l, i, j: (l, 0, j)), blk, blk, blk],
        out_specs=[blk] * 4,
        out_shape=[jax.ShapeDtypeStruct((nl, d, n), F32)] * 4,
        compiler_params=_params("parallel", "parallel", "parallel"),
    )(sc16, dm, w, m, v)


def _sum_devices(x, *, name):
    def body(x_ref, o_ref):
        acc = x_ref[0]
        for k in range(1, N_DEV):
            acc = acc + x_ref[k]
        o_ref[...] = acc

    return pl.pallas_call(body, name=name, out_shape=jax.ShapeDtypeStruct(x.shape[1:], F32))(x)


def _place():
    mx, my, mc = lax.axis_index("x"), lax.axis_index("y"), lax.axis_index("c")
    chips = [(1 - mx, my), (mx, 1 - my), (1 - mx, 1 - my)]
    return mx, my, mc, chips


def _remote(src, dst, send_sem, recv_sem, device):
    return pltpu.make_async_remote_copy(src_ref=src, dst_ref=dst, send_sem=send_sem, recv_sem=recv_sem, device_id=device, device_id_type=MESH)


def _allgather8(x, *, name):
    def body(x_ref, out_ref, send_sems, recv_sems, local_sem):
        mx, my, mc, chips = _place()
        me, sibling = (mx, my, mc), (mx, my, 1 - mc)

        def slot(px, py, pc):
            return out_ref.at[4 * px + 2 * py + pc]

        def copy(k, block, to, src=None):
            return _remote(slot(*block) if src is None else src, slot(*block), send_sems.at[k], recv_sems.at[k], to)

        mine = pltpu.make_async_copy(x_ref, slot(*me), local_sem)
        mine.start()
        first = [copy(0, me, sibling, src=x_ref)] + [copy(1 + j, me, (*chip, mc), src=x_ref) for j, chip in enumerate(chips)]
        for cp in first:
            cp.start()
        passed = [copy(4 + j, (*chip, mc), sibling) for j, chip in enumerate(chips)]
        for j, chip in enumerate(chips):
            copy(1 + j, (*chip, mc), me).wait_recv()
            passed[j].start()
        copy(0, sibling, me).wait_recv()
        for j, chip in enumerate(chips):
            copy(4 + j, (*chip, 1 - mc), me).wait_recv()
        for cp in first + passed:
            cp.wait_send()
        mine.wait()

    return pl.pallas_call(
        body,
        name=name,
        out_shape=jax.ShapeDtypeStruct((N_DEV,) + x.shape, x.dtype),
        in_specs=[pl.BlockSpec(memory_space=pltpu.VMEM)],
        out_specs=pl.BlockSpec(memory_space=pltpu.VMEM),
        scratch_shapes=[pltpu.SemaphoreType.DMA((7,)), pltpu.SemaphoreType.DMA((7,)), pltpu.SemaphoreType.DMA],
    )(x)


class _Geom:
    def __init__(self, shape3, axis):
        self.shape3, self.axis = shape3, axis
        nl, r, c = shape3
        self.rs, self.cs = (r // N_CHIPS, c) if axis == 1 else (r, c // N_CHIPS)
        self.hl, self.hr = (nl // 2, self.rs) if nl > 1 else (1, self.rs // 2)
        self.shard = (nl, self.rs, self.cs)
        self.half = (self.hl, self.hr, self.cs)

    def in_full(self, ref, chip, core):
        nl = self.shape3[0]
        l0 = core * self.hl if nl > 1 else 0
        r0 = (chip * self.rs if self.axis == 1 else 0) + (0 if nl > 1 else core * self.hr)
        c0 = chip * self.cs if self.axis == 2 else 0
        return ref.at[pl.ds(l0, self.hl), pl.ds(r0, self.hr), pl.ds(c0, self.cs)]


def _place_shard(shard, geom, chip_arr, *, name, layer=None):
    nl, rs, cs = geom.shard
    tr = _tile(rs, max(16, (1 << 20) // cs), 16)
    per = rs // tr
    first = 0 if layer is None else layer

    def body(chip_ref, x_ref, o_ref):
        o_ref[...] = x_ref[...].astype(o_ref.dtype)

    def out_map(l, i, chip_ref):
        return (l, chip_ref[0] * per + i, 0) if geom.axis == 1 else (l, i, chip_ref[0])

    return pl.pallas_call(
        body,
        name=name,
        grid_spec=pltpu.PrefetchScalarGridSpec(
            num_scalar_prefetch=1,
            grid=(nl, per),
            in_specs=[pl.BlockSpec((1, tr, cs), lambda l, i, chip_ref: (first + l, i, 0))],
            out_specs=pl.BlockSpec((1, tr, cs), out_map),
        ),
        out_shape=jax.ShapeDtypeStruct(geom.shape3, BF16),
        compiler_params=_params("parallel", "parallel"),
    )(chip_arr, shard)


def _dma_sems(count, arrays):
    return [pltpu.SemaphoreType.DMA((count,))] * arrays


def _gather_plan(fulls, geoms):
    def ici(w, k, src, dst, sems, device):
        return _remote(src, dst, sems[0].at[3 * w + k], sems[1].at[3 * w + k], device)

    def d2d(w, k, box, sems, device):
        return _remote(box, box, sems[2].at[3 * w + k], sems[3].at[3 * w + k], device)

    def start(given, full, sems):
        mx, my, mc, chips = _place()
        me = 2 * mx + my
        for w, geom in enumerate(geoms):
            for k, chip in enumerate(chips):
                ici(w, k, geom.in_full(given[w], me, mc), geom.in_full(full[w], me, mc), sems, (*chip, mc)).start()

    def finish(given, full, sems):
        mx, my, mc, chips = _place()
        me, sibling = 2 * mx + my, (mx, my, 1 - mc)
        for w, geom in enumerate(geoms):
            for k, (px, py) in enumerate(chips):
                landed = geom.in_full(full[w], 2 * px + py, mc)
                ici(w, k, landed, landed, sems, (px, py, mc)).wait_recv()
                d2d(w, k, landed, sems, sibling).start()
        for w, geom in enumerate(geoms):
            for k, (px, py) in enumerate(chips):
                d2d(w, k, geom.in_full(full[w], 2 * px + py, 1 - mc), sems, sibling).wait_recv()
        for w, geom in enumerate(geoms):
            for k, (px, py) in enumerate(chips):
                ici(w, k, geom.in_full(given[w], me, mc), geom.in_full(full[w], me, mc), sems, (px, py, mc)).wait_send()
                d2d(w, k, geom.in_full(full[w], 2 * px + py, mc), sems, sibling).wait_send()

    n = len(fulls)
    shapes = [jax.ShapeDtypeStruct(f.shape, f.dtype) for f in fulls]
    return _Hosted(fulls, shapes, {w: w for w in range(n)}, _dma_sems(3 * n, 4), start, finish)


def _gather_chips_plan(fulls, geoms):
    def copies(given, full, sems):
        mx, my, mc, chips = _place()
        me = 2 * mx + my
        return [
            _remote(geom.in_full(given[w], me, mc), geom.in_full(full[w], me, mc), sems[0].at[3 * w + k], sems[1].at[3 * w + k], (*chip, mc))
            for w, geom in enumerate(geoms)
            for k, chip in enumerate(chips)
        ]

    def arrivals(full, sems):
        _, _, mc, chips = _place()
        out = []
        for w, geom in enumerate(geoms):
            for k, (px, py) in enumerate(chips):
                landed = geom.in_full(full[w], 2 * px + py, mc)
                out.append(_remote(landed, landed, sems[0].at[3 * w + k], sems[1].at[3 * w + k], (px, py, mc)))
        return out

    def start(given, full, sems):
        for cp in copies(given, full, sems):
            cp.start()

    def finish(given, full, sems):
        for cp in arrivals(full, sems):
            cp.wait_recv()
        for cp in copies(given, full, sems):
            cp.wait_send()

    n = len(fulls)
    shapes = [jax.ShapeDtypeStruct(f.shape, f.dtype) for f in fulls]
    return _Hosted(fulls, shapes, {w: w for w in range(n)}, _dma_sems(3 * n, 2), start, finish)


def _gather_pass_plan(fulls, geoms):
    def copies(given, full, sems, core):
        mx, my, mc, chips = _place()
        half = mc if core == "mine" else 1 - mc
        out = []
        for w, geom in enumerate(geoms):
            for k, (px, py) in enumerate(chips):
                src = geom.in_full(given[w], 2 * px + py, half)
                out.append(_remote(src, geom.in_full(full[w], 2 * px + py, half), sems[0].at[3 * w + k], sems[1].at[3 * w + k], (mx, my, 1 - mc)))
        return out

    def start(given, full, sems):
        for cp in copies(given, full, sems, "mine"):
            cp.start()

    def finish(given, full, sems):
        for cp in copies(full, full, sems, "theirs"):
            cp.wait_recv()
        for cp in copies(given, full, sems, "mine"):
            cp.wait_send()

    n = len(fulls)
    shapes = [jax.ShapeDtypeStruct(f.shape, f.dtype) for f in fulls]
    return _Hosted(fulls, shapes, {w: w for w in range(n)}, _dma_sems(3 * n, 2), start, finish)


def _pair_plan(grads, geoms):
    def copies(grad, theirs, sems):
        mx, my, mc, _ = _place()
        return [
            _remote(geom.in_full(grad[w], chip, 1 - mc), theirs[w].at[chip], sems[0].at[4 * w + chip], sems[1].at[4 * w + chip], (mx, my, 1 - mc))
            for w, geom in enumerate(geoms)
            for chip in range(N_CHIPS)
        ]

    def start(grad, theirs, sems):
        for cp in copies(grad, theirs, sems):
            cp.start()

    def finish(grad, theirs, sems):
        for cp in copies(grad, theirs, sems):
            cp.wait_recv()
        for cp in copies(grad, theirs, sems):
            cp.wait_send()

    shapes = [jax.ShapeDtypeStruct((N_CHIPS,) + g.half, x.dtype) for g, x in zip(geoms, grads)]
    return _Hosted(grads, shapes, {}, _dma_sems(4 * len(grads), 2), start, finish)


def _half_tile(geom):
    return _tile(geom.hr, max(16, (1 << 18) // geom.cs), 16)


def _pair_add(grad, theirs, geom, core_arr, *, name):
    hl, hr, cs = geom.half
    tr = _half_tile(geom)
    stacked = geom.shape3[0] > 1

    def grad_map(chip, l, i, core_ref):
        layer = core_ref[0] * hl + l if stacked else 0
        row = (chip * (geom.rs // tr) if geom.axis == 1 else 0) + (0 if stacked else core_ref[0] * (hr // tr)) + i
        return layer, row, (chip if geom.axis == 2 else 0)

    def body(core_ref, g_ref, t_ref, o_ref):
        o_ref[0] = (g_ref[...].astype(F32) + t_ref[0].astype(F32)).astype(o_ref.dtype)

    blk = pl.BlockSpec((1, 1, tr, cs), lambda chip, l, i, core_ref: (chip, l, i, 0))
    return pl.pallas_call(
        body,
        name=name,
        grid_spec=pltpu.PrefetchScalarGridSpec(
            num_scalar_prefetch=1, grid=(N_CHIPS, hl, hr // tr), in_specs=[pl.BlockSpec((1, tr, cs), grad_map), blk], out_specs=blk
        ),
        out_shape=jax.ShapeDtypeStruct(theirs.shape, BF16),
        compiler_params=_params("parallel", "parallel", "parallel"),
    )(core_arr, grad, theirs)


def _chips_plan(parts):
    def copies(part, slots, sems):
        _, _, mc, chips = _place()
        return [
            _remote(part[w].at[2 * px + py], slots[w].at[k], sems[0].at[3 * w + k], sems[1].at[3 * w + k], (px, py, mc))
            for w in range(len(parts))
            for k, (px, py) in enumerate(chips)
        ]

    def start(part, slots, sems):
        for cp in copies(part, slots, sems):
            cp.start()

    def finish(part, slots, sems):
        for cp in copies(part, slots, sems):
            cp.wait_recv()
        for cp in copies(part, slots, sems):
            cp.wait_send()

    shapes = [jax.ShapeDtypeStruct((N_CHIPS - 1,) + p.shape[1:], p.dtype) for p in parts]
    return _Hosted(parts, shapes, {}, _dma_sems(3 * len(parts), 2), start, finish)


def _chip_sum(part, slots, geom, place_arr, *, name, stack=None):
    hl, hr, cs = geom.half
    tr = _half_tile(geom)

    def body(place_ref, own_ref, s0_ref, s1_ref, s2_ref, *rest):
        o_ref = rest[-1]
        o_ref[...] = ((own_ref[...].astype(F32) + s0_ref[...].astype(F32)) + s1_ref[...].astype(F32)) + s2_ref[...].astype(F32)

    def slot(k):
        return pl.BlockSpec((1, 1, tr, cs), lambda l, i, place_ref: (k, l, i, 0))

    in_specs = [pl.BlockSpec((1, 1, tr, cs), lambda l, i, place_ref: (place_ref[0], l, i, 0)), slot(0), slot(1), slot(2)]
    args = [place_arr, part, slots, slots, slots]
    aliases = {}
    if stack is None:
        out_spec = pl.BlockSpec((1, 1, tr, cs), lambda l, i, place_ref: (place_ref[1], l, i, 0))
        out_shape = jax.ShapeDtypeStruct((2,) + geom.half, F32)
    else:
        layers, layer, prev = stack
        assert hl == 1
        out_spec = pl.BlockSpec((1, 1, tr, cs), lambda l, i, place_ref: (layer, place_ref[1], i, 0))
        out_shape = jax.ShapeDtypeStruct((layers, 2, hr, cs), F32)
        if prev is not None:
            aliases = {len(args): 0}
            in_specs.append(ANY)
            args.append(prev)
    return pl.pallas_call(
        body,
        name=name,
        grid_spec=pltpu.PrefetchScalarGridSpec(num_scalar_prefetch=1, grid=(hl, hr // tr), in_specs=in_specs, out_specs=out_spec),
        out_shape=out_shape,
        input_output_aliases=aliases,
        compiler_params=_params("parallel", "parallel"),
    )(*args)


def _join_plan(boths, prefixes):
    def copies(given, both, sems):
        mx, my, mc, _ = _place()
        out, n = [], 0
        for w in range(len(boths)):
            for p in prefixes[w]:
                out.append(_remote(given[w].at[(*p, mc)], both[w].at[(*p, mc)], sems[0].at[n], sems[1].at[n], (mx, my, 1 - mc)))
                n += 1
        return out

    def arrivals(both, sems):
        mx, my, mc, _ = _place()
        out, n = [], 0
        for w in range(len(boths)):
            for p in prefixes[w]:
                got = both[w].at[(*p, 1 - mc)]
                out.append(_remote(got, got, sems[0].at[n], sems[1].at[n], (mx, my, 1 - mc)))
                n += 1
        return out

    def start(given, both, sems):
        for cp in copies(given, both, sems):
            cp.start()

    def finish(given, both, sems):
        for cp in arrivals(both, sems):
            cp.wait_recv()
        for cp in copies(given, both, sems):
            cp.wait_send()

    count = sum(len(p) for p in prefixes)
    shapes = [jax.ShapeDtypeStruct(b.shape, b.dtype) for b in boths]
    return _Hosted(boths, shapes, {w: w for w in range(len(boths))}, _dma_sems(count, 2), start, finish)


WEIGHTS = ("mod_w", "mod_b", "norm_mix", "norm_ffn", "pool_w", "pool_scale", "kv_mod_w", "kv_mod_b", "kv_in_norm", "w_dkv", "kv_norm",
           "w_uk", "w_uv", "w_kr", "w_dq", "q_norm", "w_uq", "w_o", "ffn_gate", "ffn_up", "ffn_down", "final_norm")
SMALL = ("mod_b", "kv_mod_b", "norm_mix", "norm_ffn", "kv_in_norm", "kv_norm", "q_norm", "final_norm")


def _rows(v):
    return v.reshape(-1, LANES)


def _pad_rows(a):
    return jnp.pad(a, ((0, (-a.shape[0]) % 8), (0, 0)))


def _vec(v):
    return v.reshape(1, -1)


def _step(x, c, positions, target, wts, mom, var):
    _, s, d = x.shape
    depth, n_a, n_b = wts["mod_w"].shape[0], wts["pool_w"].shape[0], wts["w_dq"].shape[0]
    assert n_b == 2 and n_a + n_b == depth
    heads = d // V_DIM
    kvr, qr = wts["w_dkv"].shape[1], wts["w_dq"].shape[2]
    ffn = wts["ffn_gate"].shape[2] * N_CHIPS
    pool_c = d // len(POOL_WINDOWS)
    nmod, nkv = N_MOD * d, 2 * d
    mx, my, mc = lax.axis_index("x"), lax.axis_index("y"), lax.axis_index("c")
    chip, dev = 2 * mx + my, 4 * mx + 2 * my + mc
    xs, tgt = x[0], target[0]

    inv_freq = 1.0 / (ROPE_THETA ** (jnp.arange(0, ROPE_DIM, 2, dtype=F32) / ROPE_DIM))
    ang = positions[0].astype(F32)[:, None] * inv_freq
    cos, sin, zero = jnp.cos(ang), jnp.sin(ang), jnp.zeros((s, LANES - ROPE_DIM), F32)
    cos_t = jnp.concatenate([cos, cos, zero], axis=1)
    sin_fwd = jnp.concatenate([-sin, sin, zero], axis=1)
    sin_bwd = jnp.concatenate([sin, -sin, zero], axis=1)

    c_rows, ps_rows = d // LANES, n_a * (d // N_CHIPS) // LANES
    cond = _allgather8(_pad_rows(jnp.concatenate([_rows(c), _rows(wts["pool_scale"])])), name="gather_cond")
    c_all = cond[:, :c_rows].reshape(N_DEV, d)
    pool_scale = cond[0::2, c_rows : c_rows + ps_rows].reshape(N_CHIPS, n_a, d // N_CHIPS).transpose(1, 0, 2).reshape(n_a, d)
    sc16 = _elementwise(_silu, [jnp.pad(c_all, ((0, 16 - N_DEV), (0, 0)))], [F32], name="silu_cond")[0]

    mod_bias = lax.dynamic_slice_in_dim(wts["mod_b"], chip * (nmod // N_CHIPS), nmod // N_CHIPS, axis=1)[:, None, :]
    kv_bias = lax.dynamic_slice_in_dim(wts["kv_mod_b"], chip * (nkv // N_CHIPS), nkv // N_CHIPS).reshape(1, 1, -1)
    mod_part = _tp_fwd(sc16, wts["mod_w"], mod_bias, name="mod_fwd")
    kv_part = _tp_fwd(sc16, wts["kv_mod_w"][None], kv_bias, name="kv_mod_fwd")
    part = jnp.concatenate([mod_part[i, :N_DEV] for i in range(depth)] + [kv_part[0, :N_DEV]], axis=1)
    ncol = part.shape[1]
    gathered = _allgather8(_pad_rows(_rows(part)), name="gather_mods")
    gathered = gathered[0::2, : N_DEV * ncol // LANES].reshape(N_CHIPS, N_DEV, ncol)
    mine = lax.dynamic_index_in_dim(gathered, dev, axis=1, keepdims=False)
    per = nmod // N_CHIPS
    mods = [mine[:, i * per : (i + 1) * per].reshape(N_MOD, 1, d) for i in range(depth)]
    kv_shift, kv_scale = mine[:, depth * per :].reshape(2, 1, d)

    mixer_names = ("pool_w", "w_dkv", "w_uk", "w_uv", "w_kr", "w_dq", "w_uq", "w_o")
    ffn_names = ("ffn_gate", "ffn_up", "ffn_down")
    geoms = {
        "pool_w": _Geom((n_a * len(POOL_WINDOWS), pool_c, pool_c), 1),
        "w_dkv": _Geom((1, d, kvr), 1),
        "w_uk": _Geom((1, kvr, heads * NOPE_DIM), 2),
        "w_uv": _Geom((1, kvr, heads * V_DIM), 2),
        "w_kr": _Geom((1, d, ROPE_DIM), 1),
        "w_dq": _Geom((n_b, d, qr), 1),
        "w_uq": _Geom((n_b, qr, heads * (NOPE_DIM + ROPE_DIM)), 2),
        "w_o": _Geom((n_b, d, d), 1),
        "ffn_gate": _Geom((1, d, ffn), 2),
        "ffn_up": _Geom((1, d, ffn), 2),
        "ffn_down": _Geom((1, ffn, d), 1),
    }
    mixer_geoms = [geoms[n] for n in mixer_names]
    ffn_geoms = [geoms[n] for n in ffn_names]
    chip_arr, core_arr, place_arr = chip.reshape(1), mc.reshape(1), jnp.stack([chip, mc])
    placed = [_place_shard(wts[n].reshape(geoms[n].shard), geoms[n], chip_arr, name="place_" + n) for n in mixer_names]
    placed_ffn = [[_place_shard(wts[n], geoms[n], chip_arr, layer=i, name="place_" + n) for n in ffn_names] for i in range(depth)]
    first = _run(_gather_plan(placed + placed_ffn[0], mixer_geoms + ffn_geoms), name="gather_first")
    full = dict(zip(mixer_names, first))
    whole_k = dict(tm=512, tn=1024, tk=max(s, ffn))
    ffn_w = [None] * depth
    ffn_w[0] = first[len(mixer_names) :]

    pool_w = full["pool_w"].reshape(n_a, len(POOL_WINDOWS), pool_c, pool_c)
    w_uq = full["w_uq"].reshape(n_b, qr, heads, NOPE_DIM + ROPE_DIM)
    w_q = jnp.pad(w_uq, ((0, 0), (0, 0), (0, 0), (0, HEAD_PAD - NOPE_DIM - ROPE_DIM))).reshape(n_b, qr, heads * HEAD_PAD)
    w_ukv = jnp.stack([full["w_uk"].reshape(kvr, heads, NOPE_DIM), full["w_uv"].reshape(kvr, heads, V_DIM)], axis=2).reshape(kvr, heads * HEAD_PAD)
    w_dkvkr = jnp.concatenate([full["w_dkv"][0], full["w_kr"][0], jnp.zeros((d, LANES - ROPE_DIM), BF16)], axis=1)
    w_dq, w_o = full["w_dq"], full["w_o"]

    norm_mix, norm_ffn = wts["norm_mix"], wts["norm_ffn"]
    saved = []
    cur, pending = xs, None
    kv_side = None
    for i in range(depth):
        shift_m, scale_m, gate_m, shift_f, scale_f, gate_f = mods[i]
        h1_dtype = F32 if i < n_a else BF16
        if pending is None:
            x0 = cur
            h1 = _norm_fwd(x0, _vec(norm_mix[i]), scale=scale_m, shift=shift_m, out_dtype=h1_dtype, name="norm_mix_first")
        else:
            x0, h1 = _norm_fwd(cur, _vec(norm_mix[i]), scale=scale_m, shift=shift_m, y=pending[0], gate=pending[1], out_dtype=h1_dtype, name="norm_mix")
        lay = {"x0": x0, "h1": h1}
        if i == n_a:
            h_kv = _norm_fwd(x0, _vec(wts["kv_in_norm"]), scale=kv_scale, shift=kv_shift, name="norm_kv_in")
            pre = _mm(h_kv, w_dkvkr, name="kv_down", tn=kvr + LANES)
            ckv_pre, kr_pre = pre[:, :kvr], pre[:, kvr:]
            ckv = _norm_fwd(ckv_pre, _vec(wts["kv_norm"]), name="norm_kv")
            kv = _mm(ckv, w_ukv, out_dtype=BF16, name="kv_up")
            keys, keys_t, v_t = _build_keys(kv, kr_pre, cos_t, sin_fwd, name="build_keys")
            kv_side = {"h_kv": h_kv, "ckv_pre": ckv_pre, "ckv": ckv, "kv": kv, "keys": keys, "keys_t": keys_t, "v_t": v_t, "x0": x0}
        if i < n_a:
            pooled = _pool(h1, transpose=False, out_dtype=BF16, name="pool_fwd")
            y_pre = _gmm(pooled, pool_w[i], mode="nn", out_dtype=F32, name="pool_mix")
            gate_eff = gate_m * _vec(pool_scale[i])
            lay.update(pooled=pooled)
        else:
            l = i - n_a
            cq_pre = _mm(h1, w_dq[l], name="q_down")
            cq = _norm_fwd(cq_pre, _vec(wts["q_norm"][l]), name="norm_q")
            q = _rope_heads(_mm(cq, w_q[l], name="q_up"), cos_t, sin_fwd, out_dtype=BF16, name="rope_q")
            if i + 1 < depth:
                (o, lse), ffn_w[i + 1] = _attn_fwd(q, kv_side["keys"], kv_side["v_t"], hosted=_gather_plan(placed_ffn[i + 1], ffn_geoms), name="attn_fwd")
            else:
                (o, lse), _ = _attn_fwd(q, kv_side["keys"], kv_side["v_t"], name="attn_fwd_last")
            y_pre = _mm(o, w_o[l], name="attn_out")
            gate_eff = gate_m
            lay.update(cq_pre=cq_pre, cq=cq, q=q, o=o, lse=lse)
        x1, h2 = _norm_fwd(x0, _vec(norm_ffn[i]), scale=scale_f, shift=shift_f, y=y_pre, gate=gate_eff, name="norm_ffn")
        w_gate, w_up, w_down = ffn_w[i]
        if i + 1 < depth and ffn_w[i + 1] is None:
            (a, b, z), landed = _ffn_in(h2, w_gate, w_up, 0, hosted=_gather_chips_plan(placed_ffn[i + 1][:2], ffn_geoms[:2]), name="ffn_in")
            rest = _merge([_gather_plan(placed_ffn[i + 1][2:], ffn_geoms[2:]), _gather_pass_plan(landed, ffn_geoms[:2])])
            f, got = _mm(z, w_down, b_idx=0, hosted=rest, name="ffn_down", **whole_k)
            ffn_w[i + 1] = got[1:] + got[:1]
        else:
            (a, b, z), _ = _ffn_in(h2, w_gate, w_up, 0, name="ffn_in_last")
            f = _mm(z, w_down, b_idx=0, name="ffn_down_last", **whole_k)
        lay.update(y_pre=y_pre, gate_eff=gate_eff, x1=x1, h2=h2, a=a, b=b, z=z, f=f)
        saved.append(lay)
        cur, pending = x1, (f, gate_f)

    dx, final_stats, loss_tile = _loss_bwd(cur, pending[0], pending[1], _vec(wts["final_norm"]), tgt, name="loss")
    loss = lax.psum(loss_tile[0, 0], ("x", "y", "c"))

    ffn_both = [None] * len(ffn_names)
    in_flight = None
    attn_names = tuple(n for n in mixer_names if n != "pool_w")
    grad_full = {}
    attn_parts = attn_slots = None

    def sum_chips(layer, which, parts, slots):
        for w, part, slot in zip(which, parts, slots):
            ffn_both[w] = _chip_sum(part, slot, ffn_geoms[w], place_arr, stack=(depth, layer, ffn_both[w]), name="chip_sum_" + ffn_names[w])

    g_wo, g_wq, g_wdq = [None] * n_b, [None] * n_b, [None] * n_b
    g_pool = [None] * n_a
    dmods = [None] * depth
    g_norm_mix, g_norm_ffn, g_q_norm, g_pool_scale = [None] * depth, [None] * depth, [None] * n_b, [None] * n_a
    dk_layers, dv_layers = [None] * n_b, [None] * n_b
    for i in reversed(range(depth)):
        lay = saved[i]
        shift_m, scale_m, gate_m, shift_f, scale_f, gate_f = mods[i]
        df, sums_gf = _gate_bwd(dx, lay["f"], gate_f, name="gate_bwd")
        w_gate, w_up, w_down = ffn_w[i]
        g_down = _mm(lay["z"], df, ta=True, out_dtype=BF16, name="ffn_down_dw", **whole_k)
        carry_attn = attn_parts is not None and attn_slots is None
        riding = [] if in_flight is None else in_flight
        plans = [_pair_plan([g_down[None]], ffn_geoms[2:])] + ([_chips_plan(riding[:1])] if riding else [])
        (da, db), got = _ffn_down_bwd(df, w_down, 0, lay["a"], lay["b"], hosted=_merge(plans), name="ffn_down_bwd")
        their_down, got_gate = got[0], got[1:]
        down_part = _pair_add(g_down[None], their_down, ffn_geoms[2], core_arr, name="pair_add_ffn_down")
        plans = [_chips_plan(part) for part in (riding[1:], attn_parts if carry_attn else None) if part]
        (dh2,), got = _ffn_in_dx(da, db, w_gate, w_up, 0, hosted=_merge(plans) if plans else None, name="ffn_in_dx")
        got_up, got = got[: len(riding[1:])], got[len(riding[1:]) :]
        if carry_attn:
            attn_slots = got
        (g_gate, g_up), got_down = _ffn_in_dw(lay["h2"], da, db, hosted=_chips_plan([down_part]), name="ffn_in_dw")
        sum_chips(i, [2], [down_part], got_down)
        if riding:
            sum_chips(i + 1, [0, 1], riding, got_gate + got_up)
        (dx1, sums_f), their_in = _norm_bwd(
            lay["x1"], _vec(norm_ffn[i]), dh2, scale=scale_f, resid=dx, hosted=_pair_plan([g_gate[None], g_up[None]], ffn_geoms[:2]), name="norm_ffn_bwd"
        )
        in_flight = [
            _pair_add(g[None], th, geom, core_arr, name="pair_add_" + n) for g, th, geom, n in zip((g_gate, g_up), their_in, ffn_geoms, ffn_names)
        ]
        dyp, sums_gm = _gate_bwd(dx1, lay["y_pre"], lay["gate_eff"], name="gate_bwd")
        if i < n_a:
            g_pool[i] = _gmm(lay["pooled"], dyp, mode="tn", out_dtype=BF16, name="pool_mix_dw")
            dd = _gmm(dyp, pool_w[i], mode="nt", out_dtype=F32, name="pool_mix_dx")
            dh1 = _pool(dd, transpose=True, out_dtype=F32, name="pool_bwd")
            dgate_m = sums_gm[0] * pool_scale[i]
            g_pool_scale[i] = sums_gm[0] * gate_m[0]
        else:
            l = i - n_a
            do = _mm(dyp, w_o[l], tb=True, out_dtype=BF16, name="attn_out_dx")
            g_wo[l] = _mm(lay["o"], dyp, ta=True, out_dtype=BF16, name="attn_out_dw")
            (dq_t, dk_layers[l], dv_layers[l]), got = _attn_bwd(
                lay["q"], kv_side["keys"], kv_side["keys_t"], kv_side["kv"], lay["o"], do, lay["lse"], hosted=_chips_plan(in_flight), name="attn_bwd"
            )
            sum_chips(i, [0, 1], in_flight, got)
            in_flight = None
            dq_pre = _rope_back(dq_t, cos_t, sin_bwd, name="rope_q_bwd")
            dcq = _mm(dq_pre, w_q[l], tb=True, name="q_up_dx")
            g_wq[l] = _mm(lay["cq"], dq_pre, ta=True, out_dtype=BF16, name="q_up_dw")
            dcq_pre, sums_q = _norm_bwd(lay["cq_pre"], _vec(wts["q_norm"][l]), dcq, name="norm_q_bwd")
            g_q_norm[l] = sums_q[2]
            dh1 = _mm(dcq_pre, w_dq[l], tb=True, name="q_down_dx")
            g_wdq[l] = _mm(lay["h1"], dcq_pre, ta=True, out_dtype=BF16, name="q_down_dw")
            dgate_m = sums_gm[0]
        dx, sums_m = _norm_bwd(lay["x0"], _vec(norm_mix[i]), dh1, scale=scale_m, resid=dx1, name="norm_mix_bwd")
        if i == n_a:
            dkv, dkr_pre = _keys_bwd(dk_layers[0], dk_layers[1], dv_layers[0], dv_layers[1], cos_t, sin_bwd, name="keys_bwd")
            dckv = _mm(dkv, w_ukv, tb=True, name="kv_up_dx")
            g_ukv = _mm(kv_side["ckv"], dkv, ta=True, out_dtype=BF16, name="kv_up_dw")
            dckv_pre, sums_kvn = _norm_bwd(kv_side["ckv_pre"], _vec(wts["kv_norm"]), dckv, name="norm_kv_bwd")
            dpre = jnp.concatenate([dckv_pre, dkr_pre], axis=1)
            dh_kv = _mm(dpre, w_dkvkr, tb=True, name="kv_down_dx")
            g_dkvkr = _mm(kv_side["h_kv"], dpre, ta=True, out_dtype=BF16, name="kv_down_dw", tn=kvr + LANES)
            dx, sums_kv = _norm_bwd(lay["x0"], _vec(wts["kv_in_norm"]), dh_kv, scale=kv_scale, resid=dx, name="norm_kv_in_bwd")
            g_ukv = g_ukv.reshape(kvr, heads, 2, NOPE_DIM)
            grad_full.update(
                w_dkv=g_dkvkr[None, :, :kvr],
                w_uk=g_ukv[:, :, 0].reshape(1, kvr, heads * NOPE_DIM),
                w_uv=g_ukv[:, :, 1].reshape(1, kvr, heads * V_DIM),
                w_kr=g_dkvkr[None, :, kvr : kvr + ROPE_DIM],
                w_dq=jnp.stack(g_wdq),
                w_uq=jnp.stack(g_wq).reshape(n_b, qr, heads, HEAD_PAD)[..., : NOPE_DIM + ROPE_DIM].reshape(geoms["w_uq"].shape3),
                w_o=jnp.stack(g_wo),
            )
            attn_theirs = _run(_pair_plan([grad_full[n] for n in attn_names], [geoms[n] for n in attn_names]), name="reduce_pair_attn")
            attn_parts = [_pair_add(grad_full[n], th, geoms[n], core_arr, name="pair_add_" + n) for n, th in zip(attn_names, attn_theirs)]
        dmods[i] = jnp.concatenate([sums_m[0], sums_m[1], dgate_m, sums_f[0], sums_f[1], sums_gf[0]])
        g_norm_mix[i], g_norm_ffn[i] = sums_m[2], sums_f[2]
    grad_x = dx[None]
    grad_full["pool_w"] = jnp.stack(g_pool).reshape(geoms["pool_w"].shape3)

    small_grads = {
        "mod_b": jnp.concatenate(dmods),
        "kv_mod_b": jnp.concatenate([sums_kv[0], sums_kv[1]]),
        "norm_mix": jnp.concatenate(g_norm_mix),
        "norm_ffn": jnp.concatenate(g_norm_ffn),
        "kv_in_norm": sums_kv[2],
        "kv_norm": sums_kvn[2],
        "q_norm": jnp.concatenate(g_q_norm),
        "final_norm": final_stats[0],
    }
    packed = jnp.concatenate([small_grads[n] for n in SMALL] + g_pool_scale)
    small_rows = sum(wts[n].size for n in SMALL) // LANES
    every = _allgather8(_pad_rows(_rows(packed)), name="gather_small_grads")
    summed = _sum_devices(every, name="sum_small_grads")

    mod_rows = depth * nmod // LANES
    dm_all = every[:, :mod_rows].reshape(N_DEV, depth, nmod)
    dm = lax.dynamic_slice_in_dim(dm_all, chip * per, per, axis=2).transpose(1, 0, 2)
    dm = jnp.pad(dm, ((0, 0), (0, 16 - N_DEV), (0, 0)))
    dkvm_all = every[:, mod_rows : mod_rows + nkv // LANES].reshape(N_DEV, nkv)
    dkvm = jnp.pad(lax.dynamic_slice_in_dim(dkvm_all, chip * (nkv // N_CHIPS), nkv // N_CHIPS, axis=1), ((0, 16 - N_DEV), (0, 0)))[None]
    results = {}
    results["mod_w"] = _tp_adamw(sc16, dm, wts["mod_w"], mom["mod_w"], var["mod_w"], name="mod_w_update")
    results["kv_mod_w"] = [
        r[0] for r in _tp_adamw(sc16, dkvm, wts["kv_mod_w"][None], mom["kv_mod_w"][None], var["kv_mod_w"][None], name="kv_mod_w_update")
    ]

    ps_grad = lax.dynamic_slice_in_dim(summed[small_rows : small_rows + n_a * d // LANES].reshape(n_a, d), chip * (d // N_CHIPS), d // N_CHIPS, axis=1)
    small_names = SMALL + ("pool_scale",)

    def pack_small(tree):
        return _pad_rows(jnp.concatenate([_rows(tree[n]) for n in small_names]))

    g_small = _pad_rows(jnp.concatenate([summed[:small_rows], _rows(ps_grad)]))
    small_out = _adamw(pack_small(wts), g_small, pack_small(mom), pack_small(var), name="small_update")
    row = 0
    for n in small_names:
        nrow = wts[n].size // LANES
        results[n] = [r[row : row + nrow].reshape(wts[n].shape) for r in small_out]
        row += nrow

    assert attn_slots is not None
    theirs = _run(_pair_plan([grad_full["pool_w"]], [geoms["pool_w"]]), name="reduce_pair")
    pool_part = _pair_add(grad_full["pool_w"], theirs[0], geoms["pool_w"], core_arr, name="pair_add_pool_w")
    got = _run(_chips_plan([pool_part] + in_flight), name="reduce_chips")
    parts = dict(zip(attn_names, attn_parts), pool_w=pool_part)
    slots = dict(zip(attn_names, attn_slots), pool_w=got[0])
    boths = [_chip_sum(parts[n], slots[n], geoms[n], place_arr, name="chip_sum_" + n) for n in mixer_names]
    sum_chips(0, [0, 1], in_flight, got[1:])
    prefixes = [[()]] * len(mixer_names) + [[(layer,) for layer in range(depth)]] * len(ffn_names)
    joined = _run(_join_plan(boths + ffn_both, prefixes), name="join_pair")
    for n, both in zip(mixer_names + ffn_names, joined):
        cs = geoms[n].cs
        out = _adamw(wts[n].reshape(-1, cs), both.reshape(-1, cs), mom[n].reshape(-1, cs), var[n].reshape(-1, cs), name="update_" + n)
        results[n] = [r.reshape(wts[n].shape) for r in out]

    outs = [loss, grad_x]
    for k in range(4):
        outs += [results[n][k] for n in WEIGHTS]
    return tuple(outs)


def kernel(x, c, positions, mod_w, mod_b, norm_mix, norm_ffn, pool_w, pool_scale, kv_mod_w, kv_mod_b, kv_in_norm, w_dkv, kv_norm, w_uk, w_uv, w_kr, w_dq, q_norm, w_uq, w_o, ffn_gate, ffn_up, ffn_down, final_norm, loss_target, m_mod_w, m_mod_b, m_norm_mix, m_norm_ffn, m_pool_w, m_pool_scale, m_kv_mod_w, m_kv_mod_b, m_kv_in_norm, m_w_dkv, m_kv_norm, m_w_uk, m_w_uv, m_w_kr, m_w_dq, m_q_norm, m_w_uq, m_w_o, m_ffn_gate, m_ffn_up, m_ffn_down, m_final_norm, v_mod_w, v_mod_b, v_norm_mix, v_norm_ffn, v_pool_w, v_pool_scale, v_kv_mod_w, v_kv_mod_b, v_kv_in_norm, v_w_dkv, v_kv_norm, v_w_uk, v_w_uv, v_w_kr, v_w_dq, v_q_norm, v_w_uq, v_w_o, v_ffn_gate, v_ffn_up, v_ffn_down, v_final_norm):
    given = dict(locals())
    wts = {n: given[n] for n in WEIGHTS}
    mom = {n: given["m_" + n] for n in WEIGHTS}
    var = {n: given["v_" + n] for n in WEIGHTS}
    return _step(x, c, positions, loss_target, wts, mom, var)
```

```python
import functools

import jax
import jax.numpy as jnp
from jax import lax
from jax.experimental import pallas as pl
from jax.experimental.pallas import tpu as pltpu

F32 = jnp.float32
BF16 = jnp.bfloat16
MESH = pl.DeviceIdType.MESH
ANY = pl.BlockSpec(memory_space=pl.ANY)

NORM_EPS = 1e-6
POOL_WINDOWS = (2, 4, 8, 16)
NOPE_DIM = 128
ROPE_DIM = 64
V_DIM = 128
HEAD_PAD = 256
SM_SCALE = (NOPE_DIM + ROPE_DIM) ** -0.5
ROPE_THETA = 10000.0
N_MOD = 6
ADAM_LR, ADAM_B1, ADAM_B2, ADAM_EPS, ADAM_WD, ADAM_STEP = 0.001, 0.9, 0.999, 1e-08, 0.01, 10
N_CHIPS = 4
N_DEV = 8
LANES = 128
HALO = 128
VMEM_LIMIT = 48 * 1024 * 1024


def _tile(dim, pref, align):
    if dim <= pref:
        return dim
    t = (pref // align) * align
    while t >= align:
        if dim % t == 0:
            return t
        t -= align
    return dim


def _params(*sem):
    return pltpu.CompilerParams(dimension_semantics=sem, vmem_limit_bytes=VMEM_LIMIT)


class _Hosted:
    def __init__(self, args, out_shapes, aliases, sem_shapes, start, finish):
        self.args, self.out_shapes, self.aliases, self.sem_shapes = list(args), list(out_shapes), dict(aliases), list(sem_shapes)
        self.start, self.finish = start, finish


def _call(body, *, name, grid, in_specs, out_specs, out_shape, args, sem, scratch_shapes=(), hosted=None):
    n_in, n_out, n_scr = len(args), len(out_shape), len(scratch_shapes)
    if hosted is None:
        outs = pl.pallas_call(
            body, name=name, grid=grid, in_specs=list(in_specs), out_specs=list(out_specs), out_shape=list(out_shape),
            scratch_shapes=list(scratch_shapes), compiler_params=_params(*sem),
        )(*args)
        return list(outs), []
    n_hin, n_hout = len(hosted.args), len(hosted.out_shapes)

    def carrying(*refs):
        own_in, their_in = refs[:n_in], refs[n_in : n_in + n_hin]
        refs = refs[n_in + n_hin :]
        own_out, their_out = refs[:n_out], refs[n_out : n_out + n_hout]
        refs = refs[n_out + n_hout :]
        own_scratch, sems = refs[:n_scr], refs[n_scr:]
        ids = [pl.program_id(axis) for axis in range(len(grid))]
        first = functools.reduce(jnp.logical_and, [i == 0 for i in ids])
        last = functools.reduce(jnp.logical_and, [i == size - 1 for i, size in zip(ids, grid)])

        @pl.when(first)
        def _():
            hosted.start(their_in, their_out, sems)

        body(*own_in, *own_out, *own_scratch)

        @pl.when(last)
        def _():
            hosted.finish(their_in, their_out, sems)

    outs = pl.pallas_call(
        carrying,
        name=name,
        grid=grid,
        in_specs=list(in_specs) + [ANY] * n_hin,
        out_specs=list(out_specs) + [ANY] * n_hout,
        out_shape=list(out_shape) + hosted.out_shapes,
        input_output_aliases={n_in + i: n_out + o for i, o in hosted.aliases.items()},
        scratch_shapes=list(scratch_shapes) + hosted.sem_shapes,
        compiler_params=_params(*["arbitrary"] * len(grid)),
    )(*args, *hosted.args)
    return list(outs[:n_out]), list(outs[n_out:])


def _merge(plans):
    if len(plans) == 1:
        return plans[0]
    args, out_shapes, aliases, sem_shapes, spans = [], [], {}, [], []
    for p in plans:
        spans.append((len(args), len(out_shapes), len(sem_shapes)))
        aliases.update({len(args) + i: len(out_shapes) + o for i, o in p.aliases.items()})
        args, out_shapes, sem_shapes = args + p.args, out_shapes + p.out_shapes, sem_shapes + p.sem_shapes

    def each(method, ins, outs, sems):
        for p, (a0, o0, s0) in zip(plans, spans):
            getattr(p, method)(ins[a0 : a0 + len(p.args)], outs[o0 : o0 + len(p.out_shapes)], sems[s0 : s0 + len(p.sem_shapes)])

    return _Hosted(args, out_shapes, aliases, sem_shapes, functools.partial(each, "start"), functools.partial(each, "finish"))


def _run(plan, *, name):
    n_in, n_out = len(plan.args), len(plan.out_shapes)

    def body(*refs):
        ins, outs, sems = refs[:n_in], refs[n_in : n_in + n_out], refs[n_in + n_out :]
        plan.start(ins, outs, sems)
        plan.finish(ins, outs, sems)

    return pl.pallas_call(
        body, name=name, in_specs=[ANY] * n_in, out_specs=[ANY] * n_out, out_shape=plan.out_shapes,
        input_output_aliases=plan.aliases, scratch_shapes=plan.sem_shapes,
    )(*plan.args)


def _mm(a, b, *, name, ta=False, tb=False, out_dtype=F32, b_idx=None, hosted=None, tm=1024, tn=1024, tk=None):
    m, k = (a.shape[1], a.shape[0]) if ta else a.shape
    b2 = b.shape if b_idx is None else b.shape[1:]
    kb, n = (b2[1], b2[0]) if tb else b2
    assert k == kb, (a.shape, b.shape, ta, tb)
    tm = _tile(m, tm, LANES)
    tn = _tile(n, tn, LANES)
    tk = _tile(k, 2048 if tk is None else tk, LANES)
    nk = k // tk
    dims = (((0 if ta else 1,), (1 if tb else 0,)), ((), ()))

    def body(a_ref, b_ref, o_ref, *acc):
        part = lax.dot_general(a_ref[...].astype(BF16), b_ref[...].astype(BF16), dims, preferred_element_type=F32)
        if nk == 1:
            o_ref[...] = part.astype(o_ref.dtype)
        else:
            acc_ref = acc[0]
            step = pl.program_id(2)

            @pl.when(step == 0)
            def _():
                acc_ref[...] = part

            @pl.when(step > 0)
            def _():
                acc_ref[...] += part

            @pl.when(step == nk - 1)
            def _():
                o_ref[...] = acc_ref[...].astype(o_ref.dtype)

    a_spec = pl.BlockSpec((tk, tm), lambda i, j, s: (s, i)) if ta else pl.BlockSpec((tm, tk), lambda i, j, s: (i, s))
    if b_idx is None:
        b_spec = pl.BlockSpec((tn, tk), lambda i, j, s: (j, s)) if tb else pl.BlockSpec((tk, tn), lambda i, j, s: (s, j))
    elif tb:
        b_spec = pl.BlockSpec((None, tn, tk), lambda i, j, s: (b_idx, j, s))
    else:
        b_spec = pl.BlockSpec((None, tk, tn), lambda i, j, s: (b_idx, s, j))
    outs, carried = _call(
        body,
        name=name,
        grid=(m // tm, n // tn, nk),
        in_specs=[a_spec, b_spec],
        out_specs=[pl.BlockSpec((tm, tn), lambda i, j, s: (i, j))],
        out_shape=[jax.ShapeDtypeStruct((m, n), out_dtype)],
        args=[a, b],
        sem=("parallel", "parallel", "arbitrary"),
        scratch_shapes=[pltpu.VMEM((tm, tn), F32)] if nk > 1 else [],
        hosted=hosted,
    )
    return outs[0] if hosted is None else (outs[0], carried)


def _tp_fwd(sc16, w, bias, *, name):
    nl, d, n = w.shape
    tn = _tile(n, 512, LANES)

    def body(sc_ref, w_ref, b_ref, o_ref):
        o_ref[0] = jnp.dot(sc_ref[...].astype(BF16), w_ref[0].astype(BF16), preferred_element_type=F32) + b_ref[0]

    return pl.pallas_call(
        body,
        name=name,
        grid=(nl, n // tn),
        in_specs=[
            pl.BlockSpec((16, d), lambda l, j: (0, 0)),
            pl.BlockSpec((1, d, tn), lambda l, j: (l, 0, j)),
            pl.BlockSpec((1, 1, tn), lambda l, j: (l, 0, j)),
        ],
        out_specs=pl.BlockSpec((1, 16, tn), lambda l, j: (l, 0, j)),
        out_shape=jax.ShapeDtypeStruct((nl, 16, n), F32),
        compiler_params=_params("parallel", "parallel"),
    )(sc16, w, bias)


_NT = (((1,), (1,)), ((), ()))
_TN = (((0,), (0,)), ((), ()))


def _silu_parts(a):
    sig = jax.nn.sigmoid(a)
    return a * sig, sig * (1.0 + a * (1.0 - sig))


def _ffn_in(h, w_gate, w_up, layer, *, name, hosted=None):
    s, d = h.shape
    f = w_gate.shape[2]
    tm, tn = _tile(s, 1024, LANES), _tile(f, 512, LANES)

    def body(h_ref, g_ref, u_ref, ga_ref, gb_ref, z_ref):
        hv = h_ref[...]
        a = jnp.dot(hv, g_ref[...], preferred_element_type=F32)
        b = jnp.dot(hv, u_ref[...], preferred_element_type=F32)
        silu, dsilu = _silu_parts(a)
        ga_ref[...] = (b * dsilu).astype(ga_ref.dtype)
        gb_ref[...] = silu.astype(gb_ref.dtype)
        z_ref[...] = (silu * b).astype(z_ref.dtype)

    w_spec = pl.BlockSpec((None, d, tn), lambda i, j: (layer, 0, j))
    out = pl.BlockSpec((tm, tn), lambda i, j: (i, j))
    return _call(
        body,
        name=name,
        grid=(s // tm, f // tn),
        in_specs=[pl.BlockSpec((tm, d), lambda i, j: (i, 0)), w_spec, w_spec],
        out_specs=[out] * 3,
        out_shape=[jax.ShapeDtypeStruct((s, f), BF16)] * 3,
        args=[h, w_gate, w_up],
        sem=("parallel", "parallel"),
        hosted=hosted,
    )


def _ffn_down_bwd(df, w_down, layer, dz_da, dz_db, *, name, hosted=None):
    s, d = df.shape
    f = w_down.shape[1]
    tm, tn = _tile(s, 1024, LANES), _tile(f, 512, LANES)

    def body(df_ref, w_ref, ga_ref, gb_ref, da_ref, db_ref):
        dz = lax.dot_general(df_ref[...], w_ref[...], _NT, preferred_element_type=F32)
        da_ref[...] = (dz * ga_ref[...].astype(F32)).astype(da_ref.dtype)
        db_ref[...] = (dz * gb_ref[...].astype(F32)).astype(db_ref.dtype)

    blk = pl.BlockSpec((tm, tn), lambda i, j: (i, j))
    return _call(
        body,
        name=name,
        grid=(s // tm, f // tn),
        in_specs=[pl.BlockSpec((tm, d), lambda i, j: (i, 0)), pl.BlockSpec((None, tn, d), lambda i, j: (layer, j, 0)), blk, blk],
        out_specs=[blk, blk],
        out_shape=[jax.ShapeDtypeStruct((s, f), BF16)] * 2,
        args=[df, w_down, dz_da, dz_db],
        sem=("parallel", "parallel"),
        hosted=hosted,
    )


def _ffn_in_dx(da, db, w_gate, w_up, layer, *, name, hosted=None):
    s, f = da.shape
    d = w_gate.shape[1]
    tm, tn, tk = _tile(s, 512, LANES), _tile(d, 1024, LANES), _tile(f, 3072, LANES)
    nk = f // tk

    def body(da_ref, db_ref, g_ref, u_ref, o_ref, acc_ref):
        part = lax.dot_general(da_ref[...], g_ref[...], _NT, preferred_element_type=F32)
        part = part + lax.dot_general(db_ref[...], u_ref[...], _NT, preferred_element_type=F32)
        step = pl.program_id(2)

        @pl.when(step == 0)
        def _():
            acc_ref[...] = part

        @pl.when(step > 0)
        def _():
            acc_ref[...] += part

        @pl.when(step == nk - 1)
        def _():
            o_ref[...] = acc_ref[...]

    x_spec = pl.BlockSpec((tm, tk), lambda i, j, k: (i, k))
    w_spec = pl.BlockSpec((None, tn, tk), lambda i, j, k: (layer, j, k))
    return _call(
        body,
        name=name,
        grid=(s // tm, d // tn, nk),
        in_specs=[x_spec, x_spec, w_spec, w_spec],
        out_specs=[pl.BlockSpec((tm, tn), lambda i, j, k: (i, j))],
        out_shape=[jax.ShapeDtypeStruct((s, d), F32)],
        args=[da, db, w_gate, w_up],
        sem=("parallel", "parallel", "arbitrary"),
        scratch_shapes=[pltpu.VMEM((tm, tn), F32)],
        hosted=hosted,
    )


def _ffn_in_dw(h, da, db, *, name, hosted=None):
    s, d = h.shape
    f = da.shape[1]
    tm, tn, tk = _tile(d, 1024, LANES), _tile(f, 512, LANES), _tile(s, 4096, LANES)
    nk = s // tk

    def body(h_ref, da_ref, db_ref, g_ref, u_ref, *acc):
        hv = h_ref[...]
        pg = lax.dot_general(hv, da_ref[...], _TN, preferred_element_type=F32)
        pu = lax.dot_general(hv, db_ref[...], _TN, preferred_element_type=F32)
        if nk == 1:
            g_ref[...] = pg.astype(g_ref.dtype)
            u_ref[...] = pu.astype(u_ref.dtype)
            return
        g_acc, u_acc = acc
        step = pl.program_id(2)

        @pl.when(step == 0)
        def _():
            g_acc[...] = pg
            u_acc[...] = pu

        @pl.when(step > 0)
        def _():
            g_acc[...] += pg
            u_acc[...] += pu

        @pl.when(step == nk - 1)
        def _():
            g_ref[...] = g_acc[...].astype(g_ref.dtype)
            u_ref[...] = u_acc[...].astype(u_ref.dtype)

    y_spec = pl.BlockSpec((tk, tn), lambda i, j, k: (k, j))
    out = pl.BlockSpec((tm, tn), lambda i, j, k: (i, j))
    return _call(
        body,
        name=name,
        grid=(d // tm, f // tn, nk),
        in_specs=[pl.BlockSpec((tk, tm), lambda i, j, k: (k, i)), y_spec, y_spec],
        out_specs=[out, out],
        out_shape=[jax.ShapeDtypeStruct((d, f), BF16)] * 2,
        args=[h, da, db],
        sem=("parallel", "parallel", "arbitrary"),
        scratch_shapes=[pltpu.VMEM((tm, tn), F32)] * 2 if nk > 1 else [],
        hosted=hosted,
    )


def _gmm(a, w, *, name, mode, out_dtype):
    s = a.shape[0]
    g = len(POOL_WINDOWS)
    c = a.shape[1] // g
    tr = _tile(s, 1024, LANES)
    n_row = s // tr

    if mode == "tn":

        def body(a_ref, b_ref, o_ref, acc_ref):
            part = lax.dot_general(a_ref[...].astype(BF16), b_ref[...].astype(BF16), (((0,), (0,)), ((), ())), preferred_element_type=F32)

            @pl.when(pl.program_id(1) == 0)
            def _():
                acc_ref[...] = part

            @pl.when(pl.program_id(1) > 0)
            def _():
                acc_ref[...] += part

            @pl.when(pl.program_id(1) == n_row - 1)
            def _():
                o_ref[0] = acc_ref[...].astype(o_ref.dtype)

        return pl.pallas_call(
            body,
            name=name,
            grid=(g, n_row),
            in_specs=[pl.BlockSpec((tr, c), lambda gi, i: (i, gi)), pl.BlockSpec((tr, c), lambda gi, i: (i, gi))],
            out_specs=pl.BlockSpec((1, c, c), lambda gi, i: (gi, 0, 0)),
            out_shape=jax.ShapeDtypeStruct((g, c, c), out_dtype),
            scratch_shapes=[pltpu.VMEM((c, c), F32)],
            compiler_params=_params("parallel", "arbitrary"),
        )(a, w)

    dims = (((1,), (0 if mode == "nn" else 1,)), ((), ()))

    def body(a_ref, w_ref, o_ref):
        o_ref[...] = lax.dot_general(a_ref[...].astype(BF16), w_ref[0].astype(BF16), dims, preferred_element_type=F32).astype(o_ref.dtype)

    return pl.pallas_call(
        body,
        name=name,
        grid=(g, n_row),
        in_specs=[pl.BlockSpec((tr, c), lambda gi, i: (i, gi)), pl.BlockSpec((1, c, c), lambda gi, i: (gi, 0, 0))],
        out_specs=pl.BlockSpec((tr, c), lambda gi, i: (i, gi)),
        out_shape=jax.ShapeDtypeStruct((s, g * c), out_dtype),
        compiler_params=_params("parallel", "parallel"),
    )(a, w)


def _row_tile(s, d):
    return _tile(s, max(8, (1 << 19) // d), 8)


def _norm_fwd(x, g, *, name, scale=None, shift=None, y=None, gate=None, out_dtype=BF16):
    s, d = x.shape
    tr = _row_tile(s, d)
    has_res, has_mod = y is not None, scale is not None

    def body(*refs):
        refs = list(refs)
        x_ref = refs.pop(0)
        xv = x_ref[...]
        if has_res:
            y_ref, gate_ref = refs.pop(0), refs.pop(0)
            xv = xv + gate_ref[...] * y_ref[...]
        g_ref = refs.pop(0)
        if has_mod:
            scale_ref, shift_ref = refs.pop(0), refs.pop(0)
        if has_res:
            refs.pop(0)[...] = xv
        h = xv * lax.rsqrt(jnp.mean(xv * xv, axis=-1, keepdims=True) + NORM_EPS)
        h = h * g_ref[...]
        if has_mod:
            h = h * (1.0 + scale_ref[...]) + shift_ref[...]
        refs.pop(0)[...] = h.astype(out_dtype)

    row = pl.BlockSpec((tr, d), lambda i: (i, 0))
    vec = pl.BlockSpec((1, d), lambda i: (0, 0))
    args, in_specs = [x], [row]
    if has_res:
        args += [y, gate]
        in_specs += [row, vec]
    args.append(g)
    in_specs.append(vec)
    if has_mod:
        args += [scale, shift]
        in_specs += [vec, vec]
    out_shape, out_specs = [], []
    if has_res:
        out_shape.append(jax.ShapeDtypeStruct((s, d), F32))
        out_specs.append(row)
    out_shape.append(jax.ShapeDtypeStruct((s, d), out_dtype))
    out_specs.append(row)
    res = pl.pallas_call(
        body, name=name, grid=(s // tr,), in_specs=in_specs, out_specs=out_specs, out_shape=out_shape, compiler_params=_params("parallel")
    )(*args)
    return (res[0], res[1]) if has_res else res[0]


def _norm_bwd(x, g, dh, *, name, scale=None, resid=None, hosted=None):
    s, d = x.shape
    tr = _row_tile(s, d)
    has_mod, has_res = scale is not None, resid is not None

    def body(*refs):
        refs = list(refs)
        x_ref, g_ref, dh_ref = refs.pop(0), refs.pop(0), refs.pop(0)
        scale_ref = refs.pop(0) if has_mod else None
        resid_ref = refs.pop(0) if has_res else None
        dx_ref, sums_ref = refs
        xv = x_ref[...]
        r = lax.rsqrt(jnp.mean(xv * xv, axis=-1, keepdims=True) + NORM_EPS)
        xhat = xv * r
        dh32 = dh_ref[...].astype(F32)
        gv = g_ref[...]
        dn = dh32 * (1.0 + scale_ref[...]) if has_mod else dh32
        dxhat = dn * gv
        dx = r * (dxhat - xhat * jnp.mean(dxhat * xhat, axis=-1, keepdims=True))
        if has_res:
            dx = dx + resid_ref[...]
        dx_ref[...] = dx

        @pl.when(pl.program_id(0) == 0)
        def _():
            sums_ref[...] = jnp.zeros_like(sums_ref)

        sums_ref[0:1, :] += jnp.sum(dh32, axis=0, keepdims=True)
        sums_ref[1:2, :] += jnp.sum(dh32 * (xhat * gv), axis=0, keepdims=True)
        sums_ref[2:3, :] += jnp.sum(dn * xhat, axis=0, keepdims=True)

    row = pl.BlockSpec((tr, d), lambda i: (i, 0))
    vec = pl.BlockSpec((1, d), lambda i: (0, 0))
    args, in_specs = [x, g, dh], [row, vec, row]
    if has_mod:
        args.append(scale)
        in_specs.append(vec)
    if has_res:
        args.append(resid)
        in_specs.append(row)
    outs, carried = _call(
        body,
        name=name,
        grid=(s // tr,),
        in_specs=in_specs,
        out_specs=[row, pl.BlockSpec((8, d), lambda i: (0, 0))],
        out_shape=[jax.ShapeDtypeStruct((s, d), F32), jax.ShapeDtypeStruct((8, d), F32)],
        args=args,
        sem=("arbitrary",),
        hosted=hosted,
    )
    return outs if hosted is None else (outs, carried)


def _gate_bwd(dx, y, gate, *, name):
    s, d = dx.shape
    tr = _row_tile(s, d)

    def body(dx_ref, y_ref, gate_ref, dy_ref, sums_ref):
        dxv = dx_ref[...]
        dy_ref[...] = (dxv * gate_ref[...]).astype(dy_ref.dtype)

        @pl.when(pl.program_id(0) == 0)
        def _():
            sums_ref[...] = jnp.zeros_like(sums_ref)

        sums_ref[0:1, :] += jnp.sum(dxv * y_ref[...], axis=0, keepdims=True)

    row = pl.BlockSpec((tr, d), lambda i: (i, 0))
    return pl.pallas_call(
        body,
        name=name,
        grid=(s // tr,),
        in_specs=[row, row, pl.BlockSpec((1, d), lambda i: (0, 0))],
        out_specs=[row, pl.BlockSpec((8, d), lambda i: (0, 0))],
        out_shape=[jax.ShapeDtypeStruct((s, d), BF16), jax.ShapeDtypeStruct((8, d), F32)],
        compiler_params=_params("arbitrary"),
    )(dx, y, gate)


def _elementwise(fn, args, out_dtypes, *, name):
    s, d = args[0].shape
    tc = d if d <= 2048 else _tile(d, 1024, LANES)
    tr = _tile(s, max(8, (1 << 18) // tc), 8)
    n_in = len(args)

    def body(*refs):
        outs = fn(*[r[...] for r in refs[:n_in]])
        for o_ref, o in zip(refs[n_in:], outs):
            o_ref[...] = o.astype(o_ref.dtype)

    spec = pl.BlockSpec((tr, tc), lambda i, j: (i, j))
    return pl.pallas_call(
        body,
        name=name,
        grid=(s // tr, d // tc),
        in_specs=[spec] * n_in,
        out_specs=[spec] * len(out_dtypes),
        out_shape=[jax.ShapeDtypeStruct((s, d), dt) for dt in out_dtypes],
        compiler_params=_params("parallel", "parallel"),
    )(*args)


def _silu(v):
    return (v * jax.nn.sigmoid(v),)


def _split3(v):
    hi = v.astype(BF16)
    r1 = v - hi.astype(F32)
    mid = r1.astype(BF16)
    lo = (r1 - mid.astype(F32)).astype(BF16)
    return hi, mid, lo


def _band_dot(band, v):
    return sum(jnp.dot(band, part, preferred_element_type=F32) for part in _split3(v))


def _pool(h, *, name, transpose, out_dtype):
    s, d = h.shape
    c = d // len(POOL_WINDOWS)
    tr = _tile(s, 256, HALO)
    per = tr // HALO
    n_halo = s // HALO

    def body(h_ref, halo_ref, o_ref):
        i = pl.program_id(0)
        out_row = i * tr + lax.broadcasted_iota(jnp.int32, (tr, tr + HALO), 0)
        col = lax.broadcasted_iota(jnp.int32, (tr, tr + HALO), 1)
        if transpose:
            ext = jnp.concatenate([h_ref[...], halo_ref[...]], axis=0)
            src_row = i * tr + col
            ext_row = i * tr + lax.broadcasted_iota(jnp.int32, (tr + HALO, 1), 0)
        else:
            ext = jnp.concatenate([halo_ref[...], h_ref[...]], axis=0)
            src_row = i * tr + col - HALO
            own_row = i * tr + lax.broadcasted_iota(jnp.int32, (tr, 1), 0)
        for gi, w in enumerate(POOL_WINDOWS):
            cols = slice(gi * c, (gi + 1) * c)
            if transpose:
                band = (src_row >= out_row) & (src_row < out_row + w) & (src_row < s)
                scaled = ext[:, cols] / jnp.minimum(ext_row + 1, w).astype(F32)
                res = _band_dot(band.astype(BF16), scaled) - h_ref[:, cols]
            else:
                band = (src_row <= out_row) & (src_row > out_row - w) & (src_row >= 0)
                res = _band_dot(band.astype(BF16), ext[:, cols]) / jnp.minimum(own_row + 1, w).astype(F32) - h_ref[:, cols]
            o_ref[:, cols] = res.astype(o_ref.dtype)

    if transpose:
        halo_map = lambda i: (jnp.minimum((i + 1) * per, n_halo - 1), 0)
    else:
        halo_map = lambda i: (jnp.maximum(i * per - 1, 0), 0)
    return pl.pallas_call(
        body,
        name=name,
        grid=(s // tr,),
        in_specs=[pl.BlockSpec((tr, d), lambda i: (i, 0)), pl.BlockSpec((HALO, d), halo_map)],
        out_specs=pl.BlockSpec((tr, d), lambda i: (i, 0)),
        out_shape=jax.ShapeDtypeStruct((s, d), out_dtype),
        compiler_params=_params("parallel"),
    )(h, h)


def _rotate(v, cos, sin):
    lane = lax.broadcasted_iota(jnp.int32, v.shape, 1)
    swapped = jnp.where(lane % ROPE_DIM < ROPE_DIM // 2, pltpu.roll(v, LANES - ROPE_DIM // 2, 1), pltpu.roll(v, ROPE_DIM // 2, 1))
    return v * cos + swapped * sin


def _rope_heads(x, cos, sin, *, name, out_dtype):
    s, n = x.shape
    tr = _tile(s, max(16, (1 << 18) // n), 16)

    def body(x_ref, cos_ref, sin_ref, o_ref):
        cos_v, sin_v = cos_ref[...], sin_ref[...]
        for j in range(n // LANES):
            lanes = slice(j * LANES, (j + 1) * LANES)
            if j % 2 == 0:
                o_ref[:, lanes] = x_ref[:, lanes].astype(o_ref.dtype)
            else:
                o_ref[:, lanes] = _rotate(x_ref[:, lanes].astype(F32), cos_v, sin_v).astype(o_ref.dtype)

    blk = pl.BlockSpec((tr, n), lambda i: (i, 0))
    tab = pl.BlockSpec((tr, LANES), lambda i: (i, 0))
    return pl.pallas_call(
        body,
        name=name,
        grid=(s // tr,),
        in_specs=[blk, tab, tab],
        out_specs=blk,
        out_shape=jax.ShapeDtypeStruct((s, n), out_dtype),
        compiler_params=_params("parallel"),
    )(x, cos, sin)


def _build_keys(kv, kr_pre, cos, sin, *, name):
    s, n = kv.shape
    heads = n // HEAD_PAD
    t = _attn_tile(s)

    def body(kv_ref, kr_ref, cos_ref, sin_ref, keys_ref, kt_ref, vt_ref):
        rope = _rotate(kr_ref[...], cos_ref[...], sin_ref[...])
        nope, val = kv_ref[:, :NOPE_DIM], kv_ref[:, NOPE_DIM:]
        keys_ref[:, :NOPE_DIM] = nope
        keys_ref[:, NOPE_DIM:] = rope.astype(keys_ref.dtype)
        kt_ref[0, 0, :NOPE_DIM, :] = nope.astype(F32).T.astype(kt_ref.dtype)
        kt_ref[0, 0, NOPE_DIM:, :] = rope.T.astype(kt_ref.dtype)
        vt_ref[0, 0] = val.astype(F32).T.astype(vt_ref.dtype)

    tab = pl.BlockSpec((t, LANES), lambda hd, kb: (kb, 0))
    return pl.pallas_call(
        body,
        name=name,
        grid=(heads, s // t),
        in_specs=[pl.BlockSpec((t, HEAD_PAD), lambda hd, kb: (kb, hd)), tab, tab, tab],
        out_specs=[
            pl.BlockSpec((t, HEAD_PAD), lambda hd, kb: (kb, hd)),
            pl.BlockSpec((1, 1, HEAD_PAD, t), lambda hd, kb: (hd, kb, 0, 0)),
            pl.BlockSpec((1, 1, V_DIM, t), lambda hd, kb: (hd, kb, 0, 0)),
        ],
        out_shape=[
            jax.ShapeDtypeStruct((s, n), BF16),
            jax.ShapeDtypeStruct((heads, s // t, HEAD_PAD, t), BF16),
            jax.ShapeDtypeStruct((heads, s // t, V_DIM, t), BF16),
        ],
        compiler_params=_params("parallel", "parallel"),
    )(kv, kr_pre, cos, sin)


def _rope_back(dq_t, cos, sin, *, name):
    heads, nq, _, t = dq_t.shape

    def body(x_ref, cos_ref, sin_ref, o_ref):
        x = x_ref[0, 0].T
        o_ref[:, :NOPE_DIM] = x[:, :NOPE_DIM].astype(o_ref.dtype)
        o_ref[:, NOPE_DIM:] = _rotate(x[:, NOPE_DIM:], cos_ref[...], sin_ref[...]).astype(o_ref.dtype)

    tab = pl.BlockSpec((t, LANES), lambda hd, qb: (qb, 0))
    return pl.pallas_call(
        body,
        name=name,
        grid=(heads, nq),
        in_specs=[pl.BlockSpec((1, 1, HEAD_PAD, t), lambda hd, qb: (hd, qb, 0, 0)), tab, tab],
        out_specs=pl.BlockSpec((t, HEAD_PAD), lambda hd, qb: (qb, hd)),
        out_shape=jax.ShapeDtypeStruct((nq * t, heads * HEAD_PAD), BF16),
        compiler_params=_params("parallel", "parallel"),
    )(dq_t, cos, sin)


def _keys_bwd(dk_a, dk_b, dv_a, dv_b, cos, sin_neg, *, name):
    s, n = dk_a.shape
    heads = n // HEAD_PAD
    tr = _tile(s, 512, 8)

    def body(dka_ref, dkb_ref, dva_ref, dvb_ref, cos_ref, sin_ref, dkv_ref, dkr_ref):
        hd = pl.program_id(1)
        dk = dka_ref[...] + dkb_ref[...]
        dkv_ref[:, :NOPE_DIM] = dk[:, :NOPE_DIM].astype(dkv_ref.dtype)
        dkv_ref[:, NOPE_DIM:] = (dva_ref[...] + dvb_ref[...]).astype(dkv_ref.dtype)

        @pl.when(hd == 0)
        def _():
            dkr_ref[...] = dk[:, NOPE_DIM:]

        @pl.when(hd > 0)
        def _():
            dkr_ref[...] += dk[:, NOPE_DIM:]

        @pl.when(hd == heads - 1)
        def _():
            dkr_ref[...] = _rotate(dkr_ref[...], cos_ref[...], sin_ref[...])

    dk_blk = pl.BlockSpec((tr, HEAD_PAD), lambda i, hd: (i, hd))
    dv_blk = pl.BlockSpec((tr, V_DIM), lambda i, hd: (i, hd))
    tab = pl.BlockSpec((tr, LANES), lambda i, hd: (i, 0))
    return pl.pallas_call(
        body,
        name=name,
        grid=(s // tr, heads),
        in_specs=[dk_blk, dk_blk, dv_blk, dv_blk, tab, tab],
        out_specs=[dk_blk, tab],
        out_shape=[jax.ShapeDtypeStruct((s, n), BF16), jax.ShapeDtypeStruct((s, LANES), F32)],
        compiler_params=_params("parallel", "arbitrary"),
    )(dk_a, dk_b, dv_a, dv_b, cos, sin_neg)


def _attn_tile(s):
    return _tile(s, 512, LANES)


def _causal_mask(t):
    return lax.broadcasted_iota(jnp.int32, (t, t), 0) <= lax.broadcasted_iota(jnp.int32, (t, t), 1)


def _attn_fwd(q, keys, v_t, *, name, hosted=None):
    s = q.shape[0]
    heads = q.shape[1] // HEAD_PAD
    t = _attn_tile(s)
    nq = s // t

    def body(q_ref, k_ref, v_ref, o_ref, lse_ref, m_ref, l_ref, acc_ref):
        qi = pl.program_id(1)
        qv = q_ref[...]
        m_ref[...] = jnp.full_like(m_ref, -jnp.inf)
        l_ref[...] = jnp.zeros_like(l_ref)
        acc_ref[...] = jnp.zeros_like(acc_ref)

        def block(kb, diagonal):
            rows = pl.ds(pl.multiple_of(kb * t, t), t)
            sc_t = lax.dot_general(k_ref[rows, :], qv, _NT, preferred_element_type=F32) * SM_SCALE
            if diagonal:
                sc_t = jnp.where(_causal_mask(t), sc_t, -jnp.inf)
            m_old = m_ref[...]
            m_new = jnp.maximum(m_old, jnp.max(sc_t, axis=0, keepdims=True))
            alpha = jnp.exp(m_old - m_new)
            p_t = jnp.exp(sc_t - m_new)
            l_ref[...] = alpha * l_ref[...] + jnp.sum(p_t, axis=0, keepdims=True)
            acc_ref[...] = alpha * acc_ref[...] + jnp.dot(v_ref[0, kb], p_t.astype(BF16), preferred_element_type=F32)
            m_ref[...] = m_new

        def earlier(kb, carry):
            block(kb, False)
            return carry

        lax.fori_loop(0, qi, earlier, 0)
        block(qi, True)
        o_ref[...] = (acc_ref[...] / l_ref[...]).T.astype(o_ref.dtype)
        lse_ref[0, 0] = m_ref[...] + jnp.log(l_ref[...])

    return _call(
        body,
        name=name,
        grid=(heads, nq),
        in_specs=[
            pl.BlockSpec((t, HEAD_PAD), lambda hd, qi: (qi, hd)),
            pl.BlockSpec((s, HEAD_PAD), lambda hd, qi: (0, hd)),
            pl.BlockSpec((1, nq, V_DIM, t), lambda hd, qi: (hd, 0, 0, 0)),
        ],
        out_specs=[pl.BlockSpec((t, V_DIM), lambda hd, qi: (qi, hd)), pl.BlockSpec((1, 1, 1, t), lambda hd, qi: (hd, qi, 0, 0))],
        out_shape=[jax.ShapeDtypeStruct((s, heads * V_DIM), BF16), jax.ShapeDtypeStruct((heads, nq, 1, t), F32)],
        args=[q, keys, v_t],
        sem=("parallel", "parallel"),
        scratch_shapes=[pltpu.VMEM((1, t), F32), pltpu.VMEM((1, t), F32), pltpu.VMEM((V_DIM, t), F32)],
        hosted=hosted,
    )


def _attn_bwd(q, keys, keys_t, kv, o, do, lse, *, name, hosted=None):
    s = q.shape[0]
    heads = q.shape[1] // HEAD_PAD
    t = _attn_tile(s)
    nq = s // t

    def body(q_ref, k_ref, kt_ref, v_ref, o_ref, do_ref, lse_ref, dq_ref, dk_ref, dv_ref, dk_acc, dv_acc, delta_ref):
        ki = pl.program_id(1)

        @pl.when(ki == 0)
        def _():
            dq_ref[...] = jnp.zeros_like(dq_ref)
            ones = jnp.ones((8, V_DIM), BF16)

            def row_sums(qb, carry):
                rows = pl.ds(pl.multiple_of(qb * t, t), t)
                prod = do_ref[rows, :].astype(F32) * o_ref[rows, :].astype(F32)
                sums = sum(lax.dot_general(ones, part, _NT, preferred_element_type=F32) for part in _split3(prod))
                delta_ref[qb] = sums[0:1]
                return carry

            lax.fori_loop(0, nq, row_sums, 0)

        kv_, vv, kt = k_ref[...], v_ref[...], kt_ref[0, 0]
        dk_acc[...] = jnp.zeros_like(dk_acc)
        dv_acc[...] = jnp.zeros_like(dv_acc)

        def block(qb, diagonal):
            rows = pl.ds(pl.multiple_of(qb * t, t), t)
            qv, dov = q_ref[rows, :], do_ref[rows, :]
            sc_t = lax.dot_general(kv_, qv, _NT, preferred_element_type=F32) * SM_SCALE
            p_t = jnp.exp(sc_t - lse_ref[0, qb])
            if diagonal:
                p_t = jnp.where(_causal_mask(t), p_t, 0.0)
            dv_acc[...] += jnp.dot(p_t.astype(BF16), dov, preferred_element_type=F32)
            dp_t = lax.dot_general(vv, dov, _NT, preferred_element_type=F32)
            ds_t = (p_t * (dp_t - delta_ref[qb]) * SM_SCALE).astype(BF16)
            dk_acc[...] += jnp.dot(ds_t, qv, preferred_element_type=F32)
            dq_ref[0, qb] += jnp.dot(kt, ds_t, preferred_element_type=F32)

        def later(qb, carry):
            block(qb, False)
            return carry

        block(ki, True)
        lax.fori_loop(ki + 1, nq, later, 0)
        dk_ref[...] = dk_acc[...]
        dv_ref[...] = dv_acc[...]

    whole = pl.BlockSpec((s, V_DIM), lambda hd, ki: (0, hd))
    return _call(
        body,
        name=name,
        grid=(heads, nq),
        in_specs=[
            pl.BlockSpec((s, HEAD_PAD), lambda hd, ki: (0, hd)),
            pl.BlockSpec((t, HEAD_PAD), lambda hd, ki: (ki, hd)),
            pl.BlockSpec((1, 1, HEAD_PAD, t), lambda hd, ki: (hd, ki, 0, 0)),
            pl.BlockSpec((t, V_DIM), lambda hd, ki: (ki, 2 * hd + 1)),
            whole,
            whole,
            pl.BlockSpec((1, nq, 1, t), lambda hd, ki: (hd, 0, 0, 0)),
        ],
        out_specs=[
            pl.BlockSpec((1, nq, HEAD_PAD, t), lambda hd, ki: (hd, 0, 0, 0)),
            pl.BlockSpec((t, HEAD_PAD), lambda hd, ki: (ki, hd)),
            pl.BlockSpec((t, V_DIM), lambda hd, ki: (ki, hd)),
        ],
        out_shape=[
            jax.ShapeDtypeStruct((heads, nq, HEAD_PAD, t), F32),
            jax.ShapeDtypeStruct((s, heads * HEAD_PAD), F32),
            jax.ShapeDtypeStruct((s, heads * V_DIM), F32),
        ],
        args=[q, keys, keys_t, kv, o, do, lse],
        sem=("parallel", "arbitrary"),
        scratch_shapes=[pltpu.VMEM((t, HEAD_PAD), F32), pltpu.VMEM((t, V_DIM), F32), pltpu.VMEM((nq, 1, t), F32)],
        hosted=hosted,
    )


def _loss_bwd(x, y, gate, g, target, *, name):
    s, d = x.shape
    tr = _row_tile(s, d)

    def body(x_ref, y_ref, gate_ref, g_ref, t_ref, dx_ref, stats_ref, loss_ref):
        xv = x_ref[...] + gate_ref[...] * y_ref[...]
        r = lax.rsqrt(jnp.mean(xv * xv, axis=-1, keepdims=True) + NORM_EPS)
        xhat = xv * r
        gv = g_ref[...]
        err = xhat * gv - t_ref[...]
        dy = err / d
        dxhat = dy * gv
        dx_ref[...] = r * (dxhat - xhat * jnp.mean(dxhat * xhat, axis=-1, keepdims=True))

        @pl.when(pl.program_id(0) == 0)
        def _():
            stats_ref[...] = jnp.zeros_like(stats_ref)
            loss_ref[...] = jnp.zeros_like(loss_ref)

        stats_ref[0:1, :] += jnp.sum(dy * xhat, axis=0, keepdims=True)
        loss_ref[...] += 0.5 * jnp.sum(jnp.mean(err * err, axis=-1, keepdims=True))

    row = pl.BlockSpec((tr, d), lambda i: (i, 0))
    vec = pl.BlockSpec((1, d), lambda i: (0, 0))
    return pl.pallas_call(
        body,
        name=name,
        grid=(s // tr,),
        in_specs=[row, row, vec, vec, row],
        out_specs=[row, pl.BlockSpec((8, d), lambda i: (0, 0)), pl.BlockSpec((8, LANES), lambda i: (0, 0))],
        out_shape=[jax.ShapeDtypeStruct((s, d), F32), jax.ShapeDtypeStruct((8, d), F32), jax.ShapeDtypeStruct((8, LANES), F32)],
        compiler_params=_params("arbitrary"),
    )(x, y, gate, g, target)


def _adam_math(w, g, m, v):
    new_m = ADAM_B1 * m + (1.0 - ADAM_B1) * g
    new_v = ADAM_B2 * v + (1.0 - ADAM_B2) * (g * g)
    m_hat = new_m / (1.0 - ADAM_B1**ADAM_STEP)
    v_hat = new_v / (1.0 - ADAM_B2**ADAM_STEP)
    return -ADAM_LR * (m_hat / (jnp.sqrt(v_hat) + ADAM_EPS) + ADAM_WD * w), new_m, new_v


def _adamw(w, g, m, v, *, name):
    rows, cols = w.shape
    tr = _tile(rows, max(8, (1 << 19) // cols), 8)

    def body(w_ref, g_ref, m_ref, v_ref, go_ref, d_ref, mo_ref, vo_ref):
        gv = g_ref[...]
        go_ref[...] = gv
        d_ref[...], mo_ref[...], vo_ref[...] = _adam_math(w_ref[...], gv, m_ref[...], v_ref[...])

    spec = pl.BlockSpec((tr, cols), lambda i: (i, 0))
    return pl.pallas_call(
        body,
        name=name,
        grid=(rows // tr,),
        in_specs=[spec] * 4,
        out_specs=[spec] * 4,
        out_shape=[jax.ShapeDtypeStruct((rows, cols), F32)] * 4,
        compiler_params=_params("parallel"),
    )(w, g, m, v)


def _tp_adamw(sc16, dm, w, m, v, *, name):
    nl, d, n = w.shape
    tm = _tile(d, 512, LANES)
    tn = _tile(n, 1024, LANES)

    def body(sc_ref, dm_ref, w_ref, m_ref, v_ref, go_ref, d_ref, mo_ref, vo_ref):
        gv = lax.dot_general(sc_ref[...].astype(BF16), dm_ref[0].astype(BF16), (((0,), (0,)), ((), ())), preferred_element_type=F32)
        go_ref[0] = gv
        d_ref[0], mo_ref[0], vo_ref[0] = _adam_math(w_ref[0], gv, m_ref[0], v_ref[0])

    blk = pl.BlockSpec((1, tm, tn), lambda l, i, j: (l, i, j))
    return pl.pallas_call(
        body,
        name=name,
        grid=(nl, d // tm, n // tn),
        in_specs=[pl.BlockSpec((16, tm), lambda l, i, j: (0, i)), pl.BlockSpec((1, 16, tn), lambda l, i, j: (l, 0, j)), blk, blk, blk],
        out_specs=[blk] * 4,
        out_shape=[jax.ShapeDtypeStruct((nl, d, n), F32)] * 4,
        compiler_params=_params("parallel", "parallel", "parallel"),
    )(sc16, dm, w, m, v)


def _sum_devices(x, *, name):
    def body(x_ref, o_ref):
        acc = x_ref[0]
        for k in range(1, N_DEV):
            acc = acc + x_ref[k]
        o_ref[...] = acc

    return pl.pallas_call(body, name=name, out_shape=jax.ShapeDtypeStruct(x.shape[1:], F32))(x)


def _place():
    mx, my, mc = lax.axis_index("x"), lax.axis_index("y"), lax.axis_index("c")
    chips = [(1 - mx, my), (mx, 1 - my), (1 - mx, 1 - my)]
    return mx, my, mc, chips


def _remote(src, dst, send_sem, recv_sem, device):
    return pltpu.make_async_remote_copy(src_ref=src, dst_ref=dst, send_sem=send_sem, recv_sem=recv_sem, device_id=device, device_id_type=MESH)


def _allgather8(x, *, name):
    def body(x_ref, out_ref, send_sems, recv_sems, local_sem):
        mx, my, mc, chips = _place()
        me, sibling = (mx, my, mc), (mx, my, 1 - mc)

        def slot(px, py, pc):
            return out_ref.at[4 * px + 2 * py + pc]

        def copy(k, block, to, src=None):
            return _remote(slot(*block) if src is None else src, slot(*block), send_sems.at[k], recv_sems.at[k], to)

        mine = pltpu.make_async_copy(x_ref, slot(*me), local_sem)
        mine.start()
        first = [copy(0, me, sibling, src=x_ref)] + [copy(1 + j, me, (*chip, mc), src=x_ref) for j, chip in enumerate(chips)]
        for cp in first:
            cp.start()
        passed = [copy(4 + j, (*chip, mc), sibling) for j, chip in enumerate(chips)]
        for j, chip in enumerate(chips):
            copy(1 + j, (*chip, mc), me).wait_recv()
            passed[j].start()
        copy(0, sibling, me).wait_recv()
        for j, chip in enumerate(chips):
            copy(4 + j, (*chip, 1 - mc), me).wait_recv()
        for cp in first + passed:
            cp.wait_send()
        mine.wait()

    return pl.pallas_call(
        body,
        name=name,
        out_shape=jax.ShapeDtypeStruct((N_DEV,) + x.shape, x.dtype),
        in_specs=[pl.BlockSpec(memory_space=pltpu.VMEM)],
        out_specs=pl.BlockSpec(memory_space=pltpu.VMEM),
        scratch_shapes=[pltpu.SemaphoreType.DMA((7,)), pltpu.SemaphoreType.DMA((7,)), pltpu.SemaphoreType.DMA],
    )(x)


class _Geom:
    def __init__(self, shape3, axis):
        self.shape3, self.axis = shape3, axis
        nl, r, c = shape3
        self.rs, self.cs = (r // N_CHIPS, c) if axis == 1 else (r, c // N_CHIPS)
        self.hl, self.hr = (nl // 2, self.rs) if nl > 1 else (1, self.rs // 2)
        self.shard = (nl, self.rs, self.cs)
        self.half = (self.hl, self.hr, self.cs)

    def in_full(self, ref, chip, core):
        nl = self.shape3[0]
        l0 = core * self.hl if nl > 1 else 0
        r0 = (chip * self.rs if self.axis == 1 else 0) + (0 if nl > 1 else core * self.hr)
        c0 = chip * self.cs if self.axis == 2 else 0
        return ref.at[pl.ds(l0, self.hl), pl.ds(r0, self.hr), pl.ds(c0, self.cs)]


def _place_shard(shard, geom, chip_arr, *, name, layer=None):
    nl, rs, cs = geom.shard
    tr = _tile(rs, max(16, (1 << 20) // cs), 16)
    per = rs // tr
    first = 0 if layer is None else layer

    def body(chip_ref, x_ref, o_ref):
        o_ref[...] = x_ref[...].astype(o_ref.dtype)

    def out_map(l, i, chip_ref):
        return (l, chip_ref[0] * per + i, 0) if geom.axis == 1 else (l, i, chip_ref[0])

    return pl.pallas_call(
        body,
        name=name,
        grid_spec=pltpu.PrefetchScalarGridSpec(
            num_scalar_prefetch=1,
            grid=(nl, per),
            in_specs=[pl.BlockSpec((1, tr, cs), lambda l, i, chip_ref: (first + l, i, 0))],
            out_specs=pl.BlockSpec((1, tr, cs), out_map),
        ),
        out_shape=jax.ShapeDtypeStruct(geom.shape3, BF16),
        compiler_params=_params("parallel", "parallel"),
    )(chip_arr, shard)


def _dma_sems(count, arrays):
    return [pltpu.SemaphoreType.DMA((count,))] * arrays


def _gather_plan(fulls, geoms):
    def ici(w, k, src, dst, sems, device):
        return _remote(src, dst, sems[0].at[3 * w + k], sems[1].at[3 * w + k], device)

    def d2d(w, k, box, sems, device):
        return _remote(box, box, sems[2].at[3 * w + k], sems[3].at[3 * w + k], device)

    def start(given, full, sems):
        mx, my, mc, chips = _place()
        me = 2 * mx + my
        for w, geom in enumerate(geoms):
            for k, chip in enumerate(chips):
                ici(w, k, geom.in_full(given[w], me, mc), geom.in_full(full[w], me, mc), sems, (*chip, mc)).start()

    def finish(given, full, sems):
        mx, my, mc, chips = _place()
        me, sibling = 2 * mx + my, (mx, my, 1 - mc)
        for w, geom in enumerate(geoms):
            for k, (px, py) in enumerate(chips):
                landed = geom.in_full(full[w], 2 * px + py, mc)
                ici(w, k, landed, landed, sems, (px, py, mc)).wait_recv()
                d2d(w, k, landed, sems, sibling).start()
        for w, geom in enumerate(geoms):
            for k, (px, py) in enumerate(chips):
                d2d(w, k, geom.in_full(full[w], 2 * px + py, 1 - mc), sems, sibling).wait_recv()
        for w, geom in enumerate(geoms):
            for k, (px, py) in enumerate(chips):
                ici(w, k, geom.in_full(given[w], me, mc), geom.in_full(full[w], me, mc), sems, (px, py, mc)).wait_send()
                d2d(w, k, geom.in_full(full[w], 2 * px + py, mc), sems, sibling).wait_send()

    n = len(fulls)
    shapes = [jax.ShapeDtypeStruct(f.shape, f.dtype) for f in fulls]
    return _Hosted(fulls, shapes, {w: w for w in range(n)}, _dma_sems(3 * n, 4), start, finish)


def _pair_plan(grads, geoms):
    def copies(grad, theirs, sems):
        mx, my, mc, _ = _place()
        return [
            _remote(geom.in_full(grad[w], chip, 1 - mc), theirs[w].at[chip], sems[0].at[4 * w + chip], sems[1].at[4 * w + chip], (mx, my, 1 - mc))
            for w, geom in enumerate(geoms)
            for chip in range(N_CHIPS)
        ]

    def start(grad, theirs, sems):
        for cp in copies(grad, theirs, sems):
            cp.start()

    def finish(grad, theirs, sems):
        for cp in copies(grad, theirs, sems):
            cp.wait_recv()
        for cp in copies(grad, theirs, sems):
            cp.wait_send()

    shapes = [jax.ShapeDtypeStruct((N_CHIPS,) + g.half, x.dtype) for g, x in zip(geoms, grads)]
    return _Hosted(grads, shapes, {}, _dma_sems(4 * len(grads), 2), start, finish)


def _half_tile(geom):
    return _tile(geom.hr, max(16, (1 << 20) // geom.cs), 16)


def _pair_add(grad, theirs, geom, core_arr, *, name):
    hl, hr, cs = geom.half
    tr = _half_tile(geom)
    stacked = geom.shape3[0] > 1

    def grad_map(chip, l, i, core_ref):
        layer = core_ref[0] * hl + l if stacked else 0
        row = (chip * (geom.rs // tr) if geom.axis == 1 else 0) + (0 if stacked else core_ref[0] * (hr // tr)) + i
        return layer, row, (chip if geom.axis == 2 else 0)

    def body(core_ref, g_ref, t_ref, o_ref):
        o_ref[0] = (g_ref[...].astype(F32) + t_ref[0].astype(F32)).astype(o_ref.dtype)

    blk = pl.BlockSpec((1, 1, tr, cs), lambda chip, l, i, core_ref: (chip, l, i, 0))
    return pl.pallas_call(
        body,
        name=name,
        grid_spec=pltpu.PrefetchScalarGridSpec(
            num_scalar_prefetch=1, grid=(N_CHIPS, hl, hr // tr), in_specs=[pl.BlockSpec((1, tr, cs), grad_map), blk], out_specs=blk
        ),
        out_shape=jax.ShapeDtypeStruct(theirs.shape, BF16),
        compiler_params=_params("parallel", "parallel", "parallel"),
    )(core_arr, grad, theirs)


def _chips_plan(parts):
    def copies(part, slots, sems):
        _, _, mc, chips = _place()
        return [
            _remote(part[w].at[2 * px + py], slots[w].at[k], sems[0].at[3 * w + k], sems[1].at[3 * w + k], (px, py, mc))
            for w in range(len(parts))
            for k, (px, py) in enumerate(chips)
        ]

    def start(part, slots, sems):
        for cp in copies(part, slots, sems):
            cp.start()

    def finish(part, slots, sems):
        for cp in copies(part, slots, sems):
            cp.wait_recv()
        for cp in copies(part, slots, sems):
            cp.wait_send()

    shapes = [jax.ShapeDtypeStruct((N_CHIPS - 1,) + p.shape[1:], p.dtype) for p in parts]
    return _Hosted(parts, shapes, {}, _dma_sems(3 * len(parts), 2), start, finish)


def _chip_sum(part, slots, geom, place_arr, *, name, stack=None):
    hl, hr, cs = geom.half
    tr = _half_tile(geom)

    def body(place_ref, own_ref, s0_ref, s1_ref, s2_ref, *rest):
        o_ref = rest[-1]
        o_ref[...] = ((own_ref[...].astype(F32) + s0_ref[...].astype(F32)) + s1_ref[...].astype(F32)) + s2_ref[...].astype(F32)

    def slot(k):
        return pl.BlockSpec((1, 1, tr, cs), lambda l, i, place_ref: (k, l, i, 0))

    in_specs = [pl.BlockSpec((1, 1, tr, cs), lambda l, i, place_ref: (place_ref[0], l, i, 0)), slot(0), slot(1), slot(2)]
    args = [place_arr, part, slots, slots, slots]
    aliases = {}
    if stack is None:
        out_spec = pl.BlockSpec((1, 1, tr, cs), lambda l, i, place_ref: (place_ref[1], l, i, 0))
        out_shape = jax.ShapeDtypeStruct((2,) + geom.half, F32)
    else:
        layers, layer, prev = stack
        assert hl == 1
        out_spec = pl.BlockSpec((1, 1, tr, cs), lambda l, i, place_ref: (layer, place_ref[1], i, 0))
        out_shape = jax.ShapeDtypeStruct((layers, 2, hr, cs), F32)
        if prev is not None:
            aliases = {len(args): 0}
            in_specs.append(ANY)
            args.append(prev)
    return pl.pallas_call(
        body,
        name=name,
        grid_spec=pltpu.PrefetchScalarGridSpec(num_scalar_prefetch=1, grid=(hl, hr // tr), in_specs=in_specs, out_specs=out_spec),
        out_shape=out_shape,
        input_output_aliases=aliases,
        compiler_params=_params("parallel", "parallel"),
    )(*args)


def _join_plan(boths, prefixes):
    def copies(given, both, sems):
        mx, my, mc, _ = _place()
        out, n = [], 0
        for w in range(len(boths)):
            for p in prefixes[w]:
                out.append(_remote(given[w].at[(*p, mc)], both[w].at[(*p, mc)], sems[0].at[n], sems[1].at[n], (mx, my, 1 - mc)))
                n += 1
        return out

    def arrivals(both, sems):
        mx, my, mc, _ = _place()
        out, n = [], 0
        for w in range(len(boths)):
            for p in prefixes[w]:
                got = both[w].at[(*p, 1 - mc)]
                out.append(_remote(got, got, sems[0].at[n], sems[1].at[n], (mx, my, 1 - mc)))
                n += 1
        return out

    def start(given, both, sems):
        for cp in copies(given, both, sems):
            cp.start()

    def finish(given, both, sems):
        for cp in arrivals(both, sems):
            cp.wait_recv()
        for cp in copies(given, both, sems):
            cp.wait_send()

    count = sum(len(p) for p in prefixes)
    shapes = [jax.ShapeDtypeStruct(b.shape, b.dtype) for b in boths]
    return _Hosted(boths, shapes, {w: w for w in range(len(boths))}, _dma_sems(count, 2), start, finish)


WEIGHTS = ("mod_w", "mod_b", "norm_mix", "norm_ffn", "pool_w", "pool_scale", "kv_mod_w", "kv_mod_b", "kv_in_norm", "w_dkv", "kv_norm",
           "w_uk", "w_uv", "w_kr", "w_dq", "q_norm", "w_uq", "w_o", "ffn_gate", "ffn_up", "ffn_down", "final_norm")
SMALL = ("mod_b", "kv_mod_b", "norm_mix", "norm_ffn", "kv_in_norm", "kv_norm", "q_norm", "final_norm")


def _rows(v):
    return v.reshape(-1, LANES)


def _pad_rows(a):
    return jnp.pad(a, ((0, (-a.shape[0]) % 8), (0, 0)))


def _vec(v):
    return v.reshape(1, -1)


def _step(x, c, positions, target, wts, mom, var):
    _, s, d = x.shape
    depth, n_a, n_b = wts["mod_w"].shape[0], wts["pool_w"].shape[0], wts["w_dq"].shape[0]
    assert n_b == 2 and n_a + n_b == depth
    heads = d // V_DIM
    kvr, qr = wts["w_dkv"].shape[1], wts["w_dq"].shape[2]
    ffn = wts["ffn_gate"].shape[2] * N_CHIPS
    pool_c = d // len(POOL_WINDOWS)
    nmod, nkv = N_MOD * d, 2 * d
    mx, my, mc = lax.axis_index("x"), lax.axis_index("y"), lax.axis_index("c")
    chip, dev = 2 * mx + my, 4 * mx + 2 * my + mc
    xs, tgt = x[0], target[0]

    inv_freq = 1.0 / (ROPE_THETA ** (jnp.arange(0, ROPE_DIM, 2, dtype=F32) / ROPE_DIM))
    ang = positions[0].astype(F32)[:, None] * inv_freq
    cos, sin, zero = jnp.cos(ang), jnp.sin(ang), jnp.zeros((s, LANES - ROPE_DIM), F32)
    cos_t = jnp.concatenate([cos, cos, zero], axis=1)
    sin_fwd = jnp.concatenate([-sin, sin, zero], axis=1)
    sin_bwd = jnp.concatenate([sin, -sin, zero], axis=1)

    c_rows, ps_rows = d // LANES, n_a * (d // N_CHIPS) // LANES
    cond = _allgather8(_pad_rows(jnp.concatenate([_rows(c), _rows(wts["pool_scale"])])), name="gather_cond")
    c_all = cond[:, :c_rows].reshape(N_DEV, d)
    pool_scale = cond[0::2, c_rows : c_rows + ps_rows].reshape(N_CHIPS, n_a, d // N_CHIPS).transpose(1, 0, 2).reshape(n_a, d)
    sc16 = _elementwise(_silu, [jnp.pad(c_all, ((0, 16 - N_DEV), (0, 0)))], [F32], name="silu_cond")[0]

    mod_bias = lax.dynamic_slice_in_dim(wts["mod_b"], chip * (nmod // N_CHIPS), nmod // N_CHIPS, axis=1)[:, None, :]
    kv_bias = lax.dynamic_slice_in_dim(wts["kv_mod_b"], chip * (nkv // N_CHIPS), nkv // N_CHIPS).reshape(1, 1, -1)
    mod_part = _tp_fwd(sc16, wts["mod_w"], mod_bias, name="mod_fwd")
    kv_part = _tp_fwd(sc16, wts["kv_mod_w"][None], kv_bias, name="kv_mod_fwd")
    part = jnp.concatenate([mod_part[i, :N_DEV] for i in range(depth)] + [kv_part[0, :N_DEV]], axis=1)
    ncol = part.shape[1]
    gathered = _allgather8(_pad_rows(_rows(part)), name="gather_mods")
    gathered = gathered[0::2, : N_DEV * ncol // LANES].reshape(N_CHIPS, N_DEV, ncol)
    mine = lax.dynamic_index_in_dim(gathered, dev, axis=1, keepdims=False)
    per = nmod // N_CHIPS
    mods = [mine[:, i * per : (i + 1) * per].reshape(N_MOD, 1, d) for i in range(depth)]
    kv_shift, kv_scale = mine[:, depth * per :].reshape(2, 1, d)

    mixer_names = ("pool_w", "w_dkv", "w_uk", "w_uv", "w_kr", "w_dq", "w_uq", "w_o")
    ffn_names = ("ffn_gate", "ffn_up", "ffn_down")
    geoms = {
        "pool_w": _Geom((n_a * len(POOL_WINDOWS), pool_c, pool_c), 1),
        "w_dkv": _Geom((1, d, kvr), 1),
        "w_uk": _Geom((1, kvr, heads * NOPE_DIM), 2),
        "w_uv": _Geom((1, kvr, heads * V_DIM), 2),
        "w_kr": _Geom((1, d, ROPE_DIM), 1),
        "w_dq": _Geom((n_b, d, qr), 1),
        "w_uq": _Geom((n_b, qr, heads * (NOPE_DIM + ROPE_DIM)), 2),
        "w_o": _Geom((n_b, d, d), 1),
        "ffn_gate": _Geom((1, d, ffn), 2),
        "ffn_up": _Geom((1, d, ffn), 2),
        "ffn_down": _Geom((1, ffn, d), 1),
    }
    mixer_geoms = [geoms[n] for n in mixer_names]
    ffn_geoms = [geoms[n] for n in ffn_names]
    chip_arr, core_arr, place_arr = chip.reshape(1), mc.reshape(1), jnp.stack([chip, mc])
    placed = [_place_shard(wts[n].reshape(geoms[n].shard), geoms[n], chip_arr, name="place_" + n) for n in mixer_names]
    placed_ffn = [[_place_shard(wts[n], geoms[n], chip_arr, layer=i, name="place_" + n) for n in ffn_names] for i in range(depth)]
    first = _run(_gather_plan(placed + placed_ffn[0], mixer_geoms + ffn_geoms), name="gather_first")
    full = dict(zip(mixer_names, first))
    whole_k = dict(tm=512, tn=1024, tk=max(s, ffn))
    ffn_w = [None] * depth
    ffn_w[0] = first[len(mixer_names) :]

    pool_w = full["pool_w"].reshape(n_a, len(POOL_WINDOWS), pool_c, pool_c)
    w_uq = full["w_uq"].reshape(n_b, qr, heads, NOPE_DIM + ROPE_DIM)
    w_q = jnp.pad(w_uq, ((0, 0), (0, 0), (0, 0), (0, HEAD_PAD - NOPE_DIM - ROPE_DIM))).reshape(n_b, qr, heads * HEAD_PAD)
    w_ukv = jnp.stack([full["w_uk"].reshape(kvr, heads, NOPE_DIM), full["w_uv"].reshape(kvr, heads, V_DIM)], axis=2).reshape(kvr, heads * HEAD_PAD)
    w_dkvkr = jnp.concatenate([full["w_dkv"][0], full["w_kr"][0], jnp.zeros((d, LANES - ROPE_DIM), BF16)], axis=1)
    w_dq, w_o = full["w_dq"], full["w_o"]

    norm_mix, norm_ffn = wts["norm_mix"], wts["norm_ffn"]
    saved = []
    cur, pending = xs, None
    kv_side = None
    for i in range(depth):
        shift_m, scale_m, gate_m, shift_f, scale_f, gate_f = mods[i]
        h1_dtype = F32 if i < n_a else BF16
        if pending is None:
            x0 = cur
            h1 = _norm_fwd(x0, _vec(norm_mix[i]), scale=scale_m, shift=shift_m, out_dtype=h1_dtype, name="norm_mix_first")
        else:
            x0, h1 = _norm_fwd(cur, _vec(norm_mix[i]), scale=scale_m, shift=shift_m, y=pending[0], gate=pending[1], out_dtype=h1_dtype, name="norm_mix")
        lay = {"x0": x0, "h1": h1}
        if i == n_a:
            h_kv = _norm_fwd(x0, _vec(wts["kv_in_norm"]), scale=kv_scale, shift=kv_shift, name="norm_kv_in")
            pre = _mm(h_kv, w_dkvkr, name="kv_down", tn=kvr + LANES)
            ckv_pre, kr_pre = pre[:, :kvr], pre[:, kvr:]
            ckv = _norm_fwd(ckv_pre, _vec(wts["kv_norm"]), name="norm_kv")
            kv = _mm(ckv, w_ukv, out_dtype=BF16, name="kv_up")
            keys, keys_t, v_t = _build_keys(kv, kr_pre, cos_t, sin_fwd, name="build_keys")
            kv_side = {"h_kv": h_kv, "ckv_pre": ckv_pre, "ckv": ckv, "kv": kv, "keys": keys, "keys_t": keys_t, "v_t": v_t, "x0": x0}
        if i < n_a:
            pooled = _pool(h1, transpose=False, out_dtype=BF16, name="pool_fwd")
            y_pre = _gmm(pooled, pool_w[i], mode="nn", out_dtype=F32, name="pool_mix")
            gate_eff = gate_m * _vec(pool_scale[i])
            lay.update(pooled=pooled)
        else:
            l = i - n_a
            cq_pre = _mm(h1, w_dq[l], name="q_down")
            cq = _norm_fwd(cq_pre, _vec(wts["q_norm"][l]), name="norm_q")
            q = _rope_heads(_mm(cq, w_q[l], name="q_up"), cos_t, sin_fwd, out_dtype=BF16, name="rope_q")
            if i + 1 < depth:
                (o, lse), ffn_w[i + 1] = _attn_fwd(q, kv_side["keys"], kv_side["v_t"], hosted=_gather_plan(placed_ffn[i + 1], ffn_geoms), name="attn_fwd")
            else:
                (o, lse), _ = _attn_fwd(q, kv_side["keys"], kv_side["v_t"], name="attn_fwd_last")
            y_pre = _mm(o, w_o[l], name="attn_out")
            gate_eff = gate_m
            lay.update(cq_pre=cq_pre, cq=cq, q=q, o=o, lse=lse)
        x1, h2 = _norm_fwd(x0, _vec(norm_ffn[i]), scale=scale_f, shift=shift_f, y=y_pre, gate=gate_eff, name="norm_ffn")
        w_gate, w_up, w_down = ffn_w[i]
        if i + 1 < depth and ffn_w[i + 1] is None:
            (a, b, z), next_in = _ffn_in(h2, w_gate, w_up, 0, hosted=_gather_plan(placed_ffn[i + 1][:2], ffn_geoms[:2]), name="ffn_in")
            f, next_down = _mm(z, w_down, b_idx=0, hosted=_gather_plan(placed_ffn[i + 1][2:], ffn_geoms[2:]), name="ffn_down", **whole_k)
            ffn_w[i + 1] = next_in + next_down
        else:
            (a, b, z), _ = _ffn_in(h2, w_gate, w_up, 0, name="ffn_in_last")
            f = _mm(z, w_down, b_idx=0, name="ffn_down_last", **whole_k)
        lay.update(y_pre=y_pre, gate_eff=gate_eff, x1=x1, h2=h2, a=a, b=b, z=z, f=f)
        saved.append(lay)
        cur, pending = x1, (f, gate_f)

    dx, final_stats, loss_tile = _loss_bwd(cur, pending[0], pending[1], _vec(wts["final_norm"]), tgt, name="loss")
    loss = lax.psum(loss_tile[0, 0], ("x", "y", "c"))

    ffn_both = [None] * len(ffn_names)
    in_flight = None
    attn_names = tuple(n for n in mixer_names if n != "pool_w")
    grad_full = {}
    attn_parts = attn_slots = None

    def sum_chips(layer, which, parts, slots):
        for w, part, slot in zip(which, parts, slots):
            ffn_both[w] = _chip_sum(part, slot, ffn_geoms[w], place_arr, stack=(depth, layer, ffn_both[w]), name="chip_sum_" + ffn_names[w])

    g_wo, g_wq, g_wdq = [None] * n_b, [None] * n_b, [None] * n_b
    g_pool = [None] * n_a
    dmods = [None] * depth
    g_norm_mix, g_norm_ffn, g_q_norm, g_pool_scale = [None] * depth, [None] * depth, [None] * n_b, [None] * n_a
    dk_layers, dv_layers = [None] * n_b, [None] * n_b
    for i in reversed(range(depth)):
        lay = saved[i]
        shift_m, scale_m, gate_m, shift_f, scale_f, gate_f = mods[i]
        df, sums_gf = _gate_bwd(dx, lay["f"], gate_f, name="gate_bwd")
        w_gate, w_up, w_down = ffn_w[i]
        g_down = _mm(lay["z"], df, ta=True, out_dtype=BF16, name="ffn_down_dw", **whole_k)
        carry_attn = attn_parts is not None and attn_slots is None
        riding = [] if in_flight is None else in_flight
        plans = [_pair_plan([g_down[None]], ffn_geoms[2:])] + ([_chips_plan(riding[:1])] if riding else [])
        (da, db), got = _ffn_down_bwd(df, w_down, 0, lay["a"], lay["b"], hosted=_merge(plans), name="ffn_down_bwd")
        their_down, got_gate = got[0], got[1:]
        down_part = _pair_add(g_down[None], their_down, ffn_geoms[2], core_arr, name="pair_add_ffn_down")
        plans = [_chips_plan(part) for part in (riding[1:], attn_parts if carry_attn else None) if part]
        (dh2,), got = _ffn_in_dx(da, db, w_gate, w_up, 0, hosted=_merge(plans) if plans else None, name="ffn_in_dx")
        got_up, got = got[: len(riding[1:])], got[len(riding[1:]) :]
        if carry_attn:
            attn_slots = got
        (g_gate, g_up), got_down = _ffn_in_dw(lay["h2"], da, db, hosted=_chips_plan([down_part]), name="ffn_in_dw")
        sum_chips(i, [2], [down_part], got_down)
        if riding:
            sum_chips(i + 1, [0, 1], riding, got_gate + got_up)
        (dx1, sums_f), their_in = _norm_bwd(
            lay["x1"], _vec(norm_ffn[i]), dh2, scale=scale_f, resid=dx, hosted=_pair_plan([g_gate[None], g_up[None]], ffn_geoms[:2]), name="norm_ffn_bwd"
        )
        in_flight = [
            _pair_add(g[None], th, geom, core_arr, name="pair_add_" + n) for g, th, geom, n in zip((g_gate, g_up), their_in, ffn_geoms, ffn_names)
        ]
        dyp, sums_gm = _gate_bwd(dx1, lay["y_pre"], lay["gate_eff"], name="gate_bwd")
        if i < n_a:
            g_pool[i] = _gmm(lay["pooled"], dyp, mode="tn", out_dtype=BF16, name="pool_mix_dw")
            dd = _gmm(dyp, pool_w[i], mode="nt", out_dtype=F32, name="pool_mix_dx")
            dh1 = _pool(dd, transpose=True, out_dtype=F32, name="pool_bwd")
            dgate_m = sums_gm[0] * pool_scale[i]
            g_pool_scale[i] = sums_gm[0] * gate_m[0]
        else:
            l = i - n_a
            do = _mm(dyp, w_o[l], tb=True, out_dtype=BF16, name="attn_out_dx")
            g_wo[l] = _mm(lay["o"], dyp, ta=True, out_dtype=BF16, name="attn_out_dw")
            (dq_t, dk_layers[l], dv_layers[l]), got = _attn_bwd(
                lay["q"], kv_side["keys"], kv_side["keys_t"], kv_side["kv"], lay["o"], do, lay["lse"], hosted=_chips_plan(in_flight), name="attn_bwd"
            )
            sum_chips(i, [0, 1], in_flight, got)
            in_flight = None
            dq_pre = _rope_back(dq_t, cos_t, sin_bwd, name="rope_q_bwd")
            dcq = _mm(dq_pre, w_q[l], tb=True, name="q_up_dx")
            g_wq[l] = _mm(lay["cq"], dq_pre, ta=True, out_dtype=BF16, name="q_up_dw")
            dcq_pre, sums_q = _norm_bwd(lay["cq_pre"], _vec(wts["q_norm"][l]), dcq, name="norm_q_bwd")
            g_q_norm[l] = sums_q[2]
            dh1 = _mm(dcq_pre, w_dq[l], tb=True, name="q_down_dx")
            g_wdq[l] = _mm(lay["h1"], dcq_pre, ta=True, out_dtype=BF16, name="q_down_dw")
            dgate_m = sums_gm[0]
        dx, sums_m = _norm_bwd(lay["x0"], _vec(norm_mix[i]), dh1, scale=scale_m, resid=dx1, name="norm_mix_bwd")
        if i == n_a:
            dkv, dkr_pre = _keys_bwd(dk_layers[0], dk_layers[1], dv_layers[0], dv_layers[1], cos_t, sin_bwd, name="keys_bwd")
            dckv = _mm(dkv, w_ukv, tb=True, name="kv_up_dx")
            g_ukv = _mm(kv_side["ckv"], dkv, ta=True, out_dtype=BF16, name="kv_up_dw")
            dckv_pre, sums_kvn = _norm_bwd(kv_side["ckv_pre"], _vec(wts["kv_norm"]), dckv, name="norm_kv_bwd")
            dpre = jnp.concatenate([dckv_pre, dkr_pre], axis=1)
            dh_kv = _mm(dpre, w_dkvkr, tb=True, name="kv_down_dx")
            g_dkvkr = _mm(kv_side["h_kv"], dpre, ta=True, out_dtype=BF16, name="kv_down_dw", tn=kvr + LANES)
            dx, sums_kv = _norm_bwd(lay["x0"], _vec(wts["kv_in_norm"]), dh_kv, scale=kv_scale, resid=dx, name="norm_kv_in_bwd")
            g_ukv = g_ukv.reshape(kvr, heads, 2, NOPE_DIM)
            grad_full.update(
                w_dkv=g_dkvkr[None, :, :kvr],
                w_uk=g_ukv[:, :, 0].reshape(1, kvr, heads * NOPE_DIM),
                w_uv=g_ukv[:, :, 1].reshape(1, kvr, heads * V_DIM),
                w_kr=g_dkvkr[None, :, kvr : kvr + ROPE_DIM],
                w_dq=jnp.stack(g_wdq),
                w_uq=jnp.stack(g_wq).reshape(n_b, qr, heads, HEAD_PAD)[..., : NOPE_DIM + ROPE_DIM].reshape(geoms["w_uq"].shape3),
                w_o=jnp.stack(g_wo),
            )
            attn_theirs = _run(_pair_plan([grad_full[n] for n in attn_names], [geoms[n] for n in attn_names]), name="reduce_pair_attn")
            attn_parts = [_pair_add(grad_full[n], th, geoms[n], core_arr, name="pair_add_" + n) for n, th in zip(attn_names, attn_theirs)]
        dmods[i] = jnp.concatenate([sums_m[0], sums_m[1], dgate_m, sums_f[0], sums_f[1], sums_gf[0]])
        g_norm_mix[i], g_norm_ffn[i] = sums_m[2], sums_f[2]
    grad_x = dx[None]
    grad_full["pool_w"] = jnp.stack(g_pool).reshape(geoms["pool_w"].shape3)

    small_grads = {
        "mod_b": jnp.concatenate(dmods),
        "kv_mod_b": jnp.concatenate([sums_kv[0], sums_kv[1]]),
        "norm_mix": jnp.concatenate(g_norm_mix),
        "norm_ffn": jnp.concatenate(g_norm_ffn),
        "kv_in_norm": sums_kv[2],
        "kv_norm": sums_kvn[2],
        "q_norm": jnp.concatenate(g_q_norm),
        "final_norm": final_stats[0],
    }
    packed = jnp.concatenate([small_grads[n] for n in SMALL] + g_pool_scale)
    small_rows = sum(wts[n].size for n in SMALL) // LANES
    every = _allgather8(_pad_rows(_rows(packed)), name="gather_small_grads")
    summed = _sum_devices(every, name="sum_small_grads")

    mod_rows = depth * nmod // LANES
    dm_all = every[:, :mod_rows].reshape(N_DEV, depth, nmod)
    dm = lax.dynamic_slice_in_dim(dm_all, chip * per, per, axis=2).transpose(1, 0, 2)
    dm = jnp.pad(dm, ((0, 0), (0, 16 - N_DEV), (0, 0)))
    dkvm_all = every[:, mod_rows : mod_rows + nkv // LANES].reshape(N_DEV, nkv)
    dkvm = jnp.pad(lax.dynamic_slice_in_dim(dkvm_all, chip * (nkv // N_CHIPS), nkv // N_CHIPS, axis=1), ((0, 16 - N_DEV), (0, 0)))[None]
    results = {}
    results["mod_w"] = _tp_adamw(sc16, dm, wts["mod_w"], mom["mod_w"], var["mod_w"], name="mod_w_update")
    results["kv_mod_w"] = [
        r[0] for r in _tp_adamw(sc16, dkvm, wts["kv_mod_w"][None], mom["kv_mod_w"][None], var["kv_mod_w"][None], name="kv_mod_w_update")
    ]

    ps_grad = lax.dynamic_slice_in_dim(summed[small_rows : small_rows + n_a * d // LANES].reshape(n_a, d), chip * (d // N_CHIPS), d // N_CHIPS, axis=1)
    small_names = SMALL + ("pool_scale",)

    def pack_small(tree):
        return _pad_rows(jnp.concatenate([_rows(tree[n]) for n in small_names]))

    g_small = _pad_rows(jnp.concatenate([summed[:small_rows], _rows(ps_grad)]))
    small_out = _adamw(pack_small(wts), g_small, pack_small(mom), pack_small(var), name="small_update")
    row = 0
    for n in small_names:
        nrow = wts[n].size // LANES
        results[n] = [r[row : row + nrow].reshape(wts[n].shape) for r in small_out]
        row += nrow

    assert attn_slots is not None
    theirs = _run(_pair_plan([grad_full["pool_w"]], [geoms["pool_w"]]), name="reduce_pair")
    pool_part = _pair_add(grad_full["pool_w"], theirs[0], geoms["pool_w"], core_arr, name="pair_add_pool_w")
    got = _run(_chips_plan([pool_part] + in_flight), name="reduce_chips")
    parts = dict(zip(attn_names, attn_parts), pool_w=pool_part)
    slots = dict(zip(attn_names, attn_slots), pool_w=got[0])
    boths = [_chip_sum(parts[n], slots[n], geoms[n], place_arr, name="chip_sum_" + n) for n in mixer_names]
    sum_chips(0, [0, 1], in_flight, got[1:])
    prefixes = [[()]] * len(mixer_names) + [[(layer,) for layer in range(depth)]] * len(ffn_names)
    joined = _run(_join_plan(boths + ffn_both, prefixes), name="join_pair")
    for n, both in zip(mixer_names + ffn_names, joined):
        cs = geoms[n].cs
        out = _adamw(wts[n].reshape(-1, cs), both.reshape(-1, cs), mom[n].reshape(-1, cs), var[n].reshape(-1, cs), name="update_" + n)
        results[n] = [r.reshape(wts[n].shape) for r in out]

    outs = [loss, grad_x]
    for k in range(4):
        outs += [results[n][k] for n in WEIGHTS]
    return tuple(outs)


def kernel(x, c, positions, mod_w, mod_b, norm_mix, norm_ffn, pool_w, pool_scale, kv_mod_w, kv_mod_b, kv_in_norm, w_dkv, kv_norm, w_uk, w_uv, w_kr, w_dq, q_norm, w_uq, w_o, ffn_gate, ffn_up, ffn_down, final_norm, loss_target, m_mod_w, m_mod_b, m_norm_mix, m_norm_ffn, m_pool_w, m_pool_scale, m_kv_mod_w, m_kv_mod_b, m_kv_in_norm, m_w_dkv, m_kv_norm, m_w_uk, m_w_uv, m_w_kr, m_w_dq, m_q_norm, m_w_uq, m_w_o, m_ffn_gate, m_ffn_up, m_ffn_down, m_final_norm, v_mod_w, v_mod_b, v_norm_mix, v_norm_ffn, v_pool_w, v_pool_scale, v_kv_mod_w, v_kv_mod_b, v_kv_in_norm, v_w_dkv, v_kv_norm, v_w_uk, v_w_uv, v_w_kr, v_w_dq, v_q_norm, v_w_uq, v_w_o, v_ffn_gate, v_ffn_up, v_ffn_down, v_final_norm):
    given = dict(locals())
    wts = {n: given[n] for n in WEIGHTS}
    mom = {n: given["m_" + n] for n in WEIGHTS}
    var = {n: given["v_" + n] for n in WEIGHTS}
    return _step(x, c, positions, loss_target, wts, mom, var)
```

```python
import functools

import jax
import jax.numpy as jnp
from jax import lax
from jax.experimental import pallas as pl
from jax.experimental.pallas import tpu as pltpu

F32 = jnp.float32
BF16 = jnp.bfloat16
MESH = pl.DeviceIdType.MESH
ANY = pl.BlockSpec(memory_space=pl.ANY)

NORM_EPS = 1e-6
POOL_WINDOWS = (2, 4, 8, 16)
NOPE_DIM = 128
ROPE_DIM = 64
V_DIM = 128
HEAD_PAD = 256
SM_SCALE = (NOPE_DIM + ROPE_DIM) ** -0.5
ROPE_THETA = 10000.0
N_MOD = 6
ADAM_LR, ADAM_B1, ADAM_B2, ADAM_EPS, ADAM_WD, ADAM_STEP = 0.001, 0.9, 0.999, 1e-08, 0.01, 10
N_CHIPS = 4
N_DEV = 8
LANES = 128
HALO = 128
VMEM_LIMIT = 48 * 1024 * 1024


def _tile(dim, pref, align):
    if dim <= pref:
        return dim
    t = (pref // align) * align
    while t >= align:
        if dim % t == 0:
            return t
        t -= align
    return dim


def _params(*sem):
    return pltpu.CompilerParams(dimension_semantics=sem, vmem_limit_bytes=VMEM_LIMIT)


class _Hosted:
    def __init__(self, args, out_shapes, aliases, sem_shapes, start, finish):
        self.args, self.out_shapes, self.aliases, self.sem_shapes = list(args), list(out_shapes), dict(aliases), list(sem_shapes)
        self.start, self.finish = start, finish


def _call(body, *, name, grid, in_specs, out_specs, out_shape, args, sem, scratch_shapes=(), hosted=None):
    n_in, n_out, n_scr = len(args), len(out_shape), len(scratch_shapes)
    if hosted is None:
        outs = pl.pallas_call(
            body, name=name, grid=grid, in_specs=list(in_specs), out_specs=list(out_specs), out_shape=list(out_shape),
            scratch_shapes=list(scratch_shapes), compiler_params=_params(*sem),
        )(*args)
        return list(outs), []
    n_hin, n_hout = len(hosted.args), len(hosted.out_shapes)

    def carrying(*refs):
        own_in, their_in = refs[:n_in], refs[n_in : n_in + n_hin]
        refs = refs[n_in + n_hin :]
        own_out, their_out = refs[:n_out], refs[n_out : n_out + n_hout]
        refs = refs[n_out + n_hout :]
        own_scratch, sems = refs[:n_scr], refs[n_scr:]
        ids = [pl.program_id(axis) for axis in range(len(grid))]
        first = functools.reduce(jnp.logical_and, [i == 0 for i in ids])
        last = functools.reduce(jnp.logical_and, [i == size - 1 for i, size in zip(ids, grid)])

        @pl.when(first)
        def _():
            hosted.start(their_in, their_out, sems)

        body(*own_in, *own_out, *own_scratch)

        @pl.when(last)
        def _():
            hosted.finish(their_in, their_out, sems)

    outs = pl.pallas_call(
        carrying,
        name=name,
        grid=grid,
        in_specs=list(in_specs) + [ANY] * n_hin,
        out_specs=list(out_specs) + [ANY] * n_hout,
        out_shape=list(out_shape) + hosted.out_shapes,
        input_output_aliases={n_in + i: n_out + o for i, o in hosted.aliases.items()},
        scratch_shapes=list(scratch_shapes) + hosted.sem_shapes,
        compiler_params=_params(*["arbitrary"] * len(grid)),
    )(*args, *hosted.args)
    return list(outs[:n_out]), list(outs[n_out:])


def _merge(plans):
    if len(plans) == 1:
        return plans[0]
    args, out_shapes, aliases, sem_shapes, spans = [], [], {}, [], []
    for p in plans:
        spans.append((len(args), len(out_shapes), len(sem_shapes)))
        aliases.update({len(args) + i: len(out_shapes) + o for i, o in p.aliases.items()})
        args, out_shapes, sem_shapes = args + p.args, out_shapes + p.out_shapes, sem_shapes + p.sem_shapes

    def each(method, ins, outs, sems):
        for p, (a0, o0, s0) in zip(plans, spans):
            getattr(p, method)(ins[a0 : a0 + len(p.args)], outs[o0 : o0 + len(p.out_shapes)], sems[s0 : s0 + len(p.sem_shapes)])

    return _Hosted(args, out_shapes, aliases, sem_shapes, functools.partial(each, "start"), functools.partial(each, "finish"))


def _run(plan, *, name):
    n_in, n_out = len(plan.args), len(plan.out_shapes)

    def body(*refs):
        ins, outs, sems = refs[:n_in], refs[n_in : n_in + n_out], refs[n_in + n_out :]
        plan.start(ins, outs, sems)
        plan.finish(ins, outs, sems)

    return pl.pallas_call(
        body, name=name, in_specs=[ANY] * n_in, out_specs=[ANY] * n_out, out_shape=plan.out_shapes,
        input_output_aliases=plan.aliases, scratch_shapes=plan.sem_shapes,
    )(*plan.args)


def _mm(a, b, *, name, ta=False, tb=False, out_dtype=F32, b_idx=None, hosted=None, tm=1024, tn=1024, tk=None):
    m, k = (a.shape[1], a.shape[0]) if ta else a.shape
    b2 = b.shape if b_idx is None else b.shape[1:]
    kb, n = (b2[1], b2[0]) if tb else b2
    assert k == kb, (a.shape, b.shape, ta, tb)
    tm = _tile(m, tm, LANES)
    tn = _tile(n, tn, LANES)
    tk = _tile(k, 2048 if tk is None else tk, LANES)
    nk = k // tk
    dims = (((0 if ta else 1,), (1 if tb else 0,)), ((), ()))

    def body(a_ref, b_ref, o_ref, *acc):
        part = lax.dot_general(a_ref[...].astype(BF16), b_ref[...].astype(BF16), dims, preferred_element_type=F32)
        if nk == 1:
            o_ref[...] = part.astype(o_ref.dtype)
        else:
            acc_ref = acc[0]
            step = pl.program_id(2)

            @pl.when(step == 0)
            def _():
                acc_ref[...] = part

            @pl.when(step > 0)
            def _():
                acc_ref[...] += part

            @pl.when(step == nk - 1)
            def _():
                o_ref[...] = acc_ref[...].astype(o_ref.dtype)

    a_spec = pl.BlockSpec((tk, tm), lambda i, j, s: (s, i)) if ta else pl.BlockSpec((tm, tk), lambda i, j, s: (i, s))
    if b_idx is None:
        b_spec = pl.BlockSpec((tn, tk), lambda i, j, s: (j, s)) if tb else pl.BlockSpec((tk, tn), lambda i, j, s: (s, j))
    elif tb:
        b_spec = pl.BlockSpec((None, tn, tk), lambda i, j, s: (b_idx, j, s))
    else:
        b_spec = pl.BlockSpec((None, tk, tn), lambda i, j, s: (b_idx, s, j))
    outs, carried = _call(
        body,
        name=name,
        grid=(m // tm, n // tn, nk),
        in_specs=[a_spec, b_spec],
        out_specs=[pl.BlockSpec((tm, tn), lambda i, j, s: (i, j))],
        out_shape=[jax.ShapeDtypeStruct((m, n), out_dtype)],
        args=[a, b],
        sem=("parallel", "parallel", "arbitrary"),
        scratch_shapes=[pltpu.VMEM((tm, tn), F32)] if nk > 1 else [],
        hosted=hosted,
    )
    return outs[0] if hosted is None else (outs[0], carried)


def _tp_fwd(sc16, w, bias, *, name):
    nl, d, n = w.shape
    tn = _tile(n, 512, LANES)

    def body(sc_ref, w_ref, b_ref, o_ref):
        o_ref[0] = jnp.dot(sc_ref[...].astype(BF16), w_ref[0].astype(BF16), preferred_element_type=F32) + b_ref[0]

    return pl.pallas_call(
        body,
        name=name,
        grid=(nl, n // tn),
        in_specs=[
            pl.BlockSpec((16, d), lambda l, j: (0, 0)),
            pl.BlockSpec((1, d, tn), lambda l, j: (l, 0, j)),
            pl.BlockSpec((1, 1, tn), lambda l, j: (l, 0, j)),
        ],
        out_specs=pl.BlockSpec((1, 16, tn), lambda l, j: (l, 0, j)),
        out_shape=jax.ShapeDtypeStruct((nl, 16, n), F32),
        compiler_params=_params("parallel", "parallel"),
    )(sc16, w, bias)


_NT = (((1,), (1,)), ((), ()))
_TN = (((0,), (0,)), ((), ()))


def _silu_parts(a):
    sig = jax.nn.sigmoid(a)
    return a * sig, sig * (1.0 + a * (1.0 - sig))


def _ffn_in(h, w_gate, w_up, layer, *, name, hosted=None):
    s, d = h.shape
    f = w_gate.shape[2]
    tm, tn = _tile(s, 1024, LANES), _tile(f, 512, LANES)

    def body(h_ref, g_ref, u_ref, ga_ref, gb_ref, z_ref):
        hv = h_ref[...]
        a = jnp.dot(hv, g_ref[...], preferred_element_type=F32)
        b = jnp.dot(hv, u_ref[...], preferred_element_type=F32)
        silu, dsilu = _silu_parts(a)
        ga_ref[...] = (b * dsilu).astype(ga_ref.dtype)
        gb_ref[...] = silu.astype(gb_ref.dtype)
        z_ref[...] = (silu * b).astype(z_ref.dtype)

    w_spec = pl.BlockSpec((None, d, tn), lambda i, j: (layer, 0, j))
    out = pl.BlockSpec((tm, tn), lambda i, j: (i, j))
    return _call(
        body,
        name=name,
        grid=(s // tm, f // tn),
        in_specs=[pl.BlockSpec((tm, d), lambda i, j: (i, 0)), w_spec, w_spec],
        out_specs=[out] * 3,
        out_shape=[jax.ShapeDtypeStruct((s, f), BF16)] * 3,
        args=[h, w_gate, w_up],
        sem=("parallel", "parallel"),
        hosted=hosted,
    )


def _ffn_down_bwd(df, w_down, layer, dz_da, dz_db, *, name, hosted=None):
    s, d = df.shape
    f = w_down.shape[1]
    tm, tn = _tile(s, 1024, LANES), _tile(f, 512, LANES)

    def body(df_ref, w_ref, ga_ref, gb_ref, da_ref, db_ref):
        dz = lax.dot_general(df_ref[...], w_ref[...], _NT, preferred_element_type=F32)
        da_ref[...] = (dz * ga_ref[...].astype(F32)).astype(da_ref.dtype)
        db_ref[...] = (dz * gb_ref[...].astype(F32)).astype(db_ref.dtype)

    blk = pl.BlockSpec((tm, tn), lambda i, j: (i, j))
    return _call(
        body,
        name=name,
        grid=(s // tm, f // tn),
        in_specs=[pl.BlockSpec((tm, d), lambda i, j: (i, 0)), pl.BlockSpec((None, tn, d), lambda i, j: (layer, j, 0)), blk, blk],
        out_specs=[blk, blk],
        out_shape=[jax.ShapeDtypeStruct((s, f), BF16)] * 2,
        args=[df, w_down, dz_da, dz_db],
        sem=("parallel", "parallel"),
        hosted=hosted,
    )


def _ffn_in_dx(da, db, w_gate, w_up, layer, *, name, hosted=None):
    s, f = da.shape
    d = w_gate.shape[1]
    tm, tn, tk = _tile(s, 512, LANES), _tile(d, 1024, LANES), _tile(f, 3072, LANES)
    nk = f // tk

    def body(da_ref, db_ref, g_ref, u_ref, o_ref, acc_ref):
        part = lax.dot_general(da_ref[...], g_ref[...], _NT, preferred_element_type=F32)
        part = part + lax.dot_general(db_ref[...], u_ref[...], _NT, preferred_element_type=F32)
        step = pl.program_id(2)

        @pl.when(step == 0)
        def _():
            acc_ref[...] = part

        @pl.when(step > 0)
        def _():
            acc_ref[...] += part

        @pl.when(step == nk - 1)
        def _():
            o_ref[...] = acc_ref[...]

    x_spec = pl.BlockSpec((tm, tk), lambda i, j, k: (i, k))
    w_spec = pl.BlockSpec((None, tn, tk), lambda i, j, k: (layer, j, k))
    return _call(
        body,
        name=name,
        grid=(s // tm, d // tn, nk),
        in_specs=[x_spec, x_spec, w_spec, w_spec],
        out_specs=[pl.BlockSpec((tm, tn), lambda i, j, k: (i, j))],
        out_shape=[jax.ShapeDtypeStruct((s, d), F32)],
        args=[da, db, w_gate, w_up],
        sem=("parallel", "parallel", "arbitrary"),
        scratch_shapes=[pltpu.VMEM((tm, tn), F32)],
        hosted=hosted,
    )


def _ffn_in_dw(h, da, db, *, name, hosted=None):
    s, d = h.shape
    f = da.shape[1]
    tm, tn, tk = _tile(d, 1024, LANES), _tile(f, 512, LANES), _tile(s, 4096, LANES)
    nk = s // tk

    def body(h_ref, da_ref, db_ref, g_ref, u_ref, *acc):
        hv = h_ref[...]
        pg = lax.dot_general(hv, da_ref[...], _TN, preferred_element_type=F32)
        pu = lax.dot_general(hv, db_ref[...], _TN, preferred_element_type=F32)
        if nk == 1:
            g_ref[...] = pg.astype(g_ref.dtype)
            u_ref[...] = pu.astype(u_ref.dtype)
            return
        g_acc, u_acc = acc
        step = pl.program_id(2)

        @pl.when(step == 0)
        def _():
            g_acc[...] = pg
            u_acc[...] = pu

        @pl.when(step > 0)
        def _():
            g_acc[...] += pg
            u_acc[...] += pu

        @pl.when(step == nk - 1)
        def _():
            g_ref[...] = g_acc[...].astype(g_ref.dtype)
            u_ref[...] = u_acc[...].astype(u_ref.dtype)

    y_spec = pl.BlockSpec((tk, tn), lambda i, j, k: (k, j))
    out = pl.BlockSpec((tm, tn), lambda i, j, k: (i, j))
    return _call(
        body,
        name=name,
        grid=(d // tm, f // tn, nk),
        in_specs=[pl.BlockSpec((tk, tm), lambda i, j, k: (k, i)), y_spec, y_spec],
        out_specs=[out, out],
        out_shape=[jax.ShapeDtypeStruct((d, f), BF16)] * 2,
        args=[h, da, db],
        sem=("parallel", "parallel", "arbitrary"),
        scratch_shapes=[pltpu.VMEM((tm, tn), F32)] * 2 if nk > 1 else [],
        hosted=hosted,
    )


def _gmm(a, w, *, name, mode, out_dtype):
    s = a.shape[0]
    g = len(POOL_WINDOWS)
    c = a.shape[1] // g
    tr = _tile(s, 1024, LANES)
    n_row = s // tr

    if mode == "tn":

        def body(a_ref, b_ref, o_ref, acc_ref):
            part = lax.dot_general(a_ref[...].astype(BF16), b_ref[...].astype(BF16), (((0,), (0,)), ((), ())), preferred_element_type=F32)

            @pl.when(pl.program_id(1) == 0)
            def _():
                acc_ref[...] = part

            @pl.when(pl.program_id(1) > 0)
            def _():
                acc_ref[...] += part

            @pl.when(pl.program_id(1) == n_row - 1)
            def _():
                o_ref[0] = acc_ref[...].astype(o_ref.dtype)

        return pl.pallas_call(
            body,
            name=name,
            grid=(g, n_row),
            in_specs=[pl.BlockSpec((tr, c), lambda gi, i: (i, gi)), pl.BlockSpec((tr, c), lambda gi, i: (i, gi))],
            out_specs=pl.BlockSpec((1, c, c), lambda gi, i: (gi, 0, 0)),
            out_shape=jax.ShapeDtypeStruct((g, c, c), out_dtype),
            scratch_shapes=[pltpu.VMEM((c, c), F32)],
            compiler_params=_params("parallel", "arbitrary"),
        )(a, w)

    dims = (((1,), (0 if mode == "nn" else 1,)), ((), ()))

    def body(a_ref, w_ref, o_ref):
        o_ref[...] = lax.dot_general(a_ref[...].astype(BF16), w_ref[0].astype(BF16), dims, preferred_element_type=F32).astype(o_ref.dtype)

    return pl.pallas_call(
        body,
        name=name,
        grid=(g, n_row),
        in_specs=[pl.BlockSpec((tr, c), lambda gi, i: (i, gi)), pl.BlockSpec((1, c, c), lambda gi, i: (gi, 0, 0))],
        out_specs=pl.BlockSpec((tr, c), lambda gi, i: (i, gi)),
        out_shape=jax.ShapeDtypeStruct((s, g * c), out_dtype),
        compiler_params=_params("parallel", "parallel"),
    )(a, w)


def _row_tile(s, d):
    return _tile(s, max(8, (1 << 19) // d), 8)


def _norm_fwd(x, g, *, name, scale=None, shift=None, y=None, gate=None, out_dtype=BF16):
    s, d = x.shape
    tr = _row_tile(s, d)
    has_res, has_mod = y is not None, scale is not None

    def body(*refs):
        refs = list(refs)
        x_ref = refs.pop(0)
        xv = x_ref[...]
        if has_res:
            y_ref, gate_ref = refs.pop(0), refs.pop(0)
            xv = xv + gate_ref[...] * y_ref[...]
        g_ref = refs.pop(0)
        if has_mod:
            scale_ref, shift_ref = refs.pop(0), refs.pop(0)
        if has_res:
            refs.pop(0)[...] = xv
        h = xv * lax.rsqrt(jnp.mean(xv * xv, axis=-1, keepdims=True) + NORM_EPS)
        h = h * g_ref[...]
        if has_mod:
            h = h * (1.0 + scale_ref[...]) + shift_ref[...]
        refs.pop(0)[...] = h.astype(out_dtype)

    row = pl.BlockSpec((tr, d), lambda i: (i, 0))
    vec = pl.BlockSpec((1, d), lambda i: (0, 0))
    args, in_specs = [x], [row]
    if has_res:
        args += [y, gate]
        in_specs += [row, vec]
    args.append(g)
    in_specs.append(vec)
    if has_mod:
        args += [scale, shift]
        in_specs += [vec, vec]
    out_shape, out_specs = [], []
    if has_res:
        out_shape.append(jax.ShapeDtypeStruct((s, d), F32))
        out_specs.append(row)
    out_shape.append(jax.ShapeDtypeStruct((s, d), out_dtype))
    out_specs.append(row)
    res = pl.pallas_call(
        body, name=name, grid=(s // tr,), in_specs=in_specs, out_specs=out_specs, out_shape=out_shape, compiler_params=_params("parallel")
    )(*args)
    return (res[0], res[1]) if has_res else res[0]


def _norm_bwd(x, g, dh, *, name, scale=None, resid=None, hosted=None):
    s, d = x.shape
    tr = _row_tile(s, d)
    has_mod, has_res = scale is not None, resid is not None

    def body(*refs):
        refs = list(refs)
        x_ref, g_ref, dh_ref = refs.pop(0), refs.pop(0), refs.pop(0)
        scale_ref = refs.pop(0) if has_mod else None
        resid_ref = refs.pop(0) if has_res else None
        dx_ref, sums_ref = refs
        xv = x_ref[...]
        r = lax.rsqrt(jnp.mean(xv * xv, axis=-1, keepdims=True) + NORM_EPS)
        xhat = xv * r
        dh32 = dh_ref[...].astype(F32)
        gv = g_ref[...]
        dn = dh32 * (1.0 + scale_ref[...]) if has_mod else dh32
        dxhat = dn * gv
        dx = r * (dxhat - xhat * jnp.mean(dxhat * xhat, axis=-1, keepdims=True))
        if has_res:
            dx = dx + resid_ref[...]
        dx_ref[...] = dx

        @pl.when(pl.program_id(0) == 0)
        def _():
            sums_ref[...] = jnp.zeros_like(sums_ref)

        sums_ref[0:1, :] += jnp.sum(dh32, axis=0, keepdims=True)
        sums_ref[1:2, :] += jnp.sum(dh32 * (xhat * gv), axis=0, keepdims=True)
        sums_ref[2:3, :] += jnp.sum(dn * xhat, axis=0, keepdims=True)

    row = pl.BlockSpec((tr, d), lambda i: (i, 0))
    vec = pl.BlockSpec((1, d), lambda i: (0, 0))
    args, in_specs = [x, g, dh], [row, vec, row]
    if has_mod:
        args.append(scale)
        in_specs.append(vec)
    if has_res:
        args.append(resid)
        in_specs.append(row)
    outs, carried = _call(
        body,
        name=name,
        grid=(s // tr,),
        in_specs=in_specs,
        out_specs=[row, pl.BlockSpec((8, d), lambda i: (0, 0))],
        out_shape=[jax.ShapeDtypeStruct((s, d), F32), jax.ShapeDtypeStruct((8, d), F32)],
        args=args,
        sem=("arbitrary",),
        hosted=hosted,
    )
    return outs if hosted is None else (outs, carried)


def _gate_bwd(dx, y, gate, *, name):
    s, d = dx.shape
    tr = _row_tile(s, d)

    def body(dx_ref, y_ref, gate_ref, dy_ref, sums_ref):
        dxv = dx_ref[...]
        dy_ref[...] = (dxv * gate_ref[...]).astype(dy_ref.dtype)

        @pl.when(pl.program_id(0) == 0)
        def _():
            sums_ref[...] = jnp.zeros_like(sums_ref)

        sums_ref[0:1, :] += jnp.sum(dxv * y_ref[...], axis=0, keepdims=True)

    row = pl.BlockSpec((tr, d), lambda i: (i, 0))
    return pl.pallas_call(
        body,
        name=name,
        grid=(s // tr,),
        in_specs=[row, row, pl.BlockSpec((1, d), lambda i: (0, 0))],
        out_specs=[row, pl.BlockSpec((8, d), lambda i: (0, 0))],
        out_shape=[jax.ShapeDtypeStruct((s, d), BF16), jax.ShapeDtypeStruct((8, d), F32)],
        compiler_params=_params("arbitrary"),
    )(dx, y, gate)


def _elementwise(fn, args, out_dtypes, *, name):
    s, d = args[0].shape
    tc = d if d <= 2048 else _tile(d, 1024, LANES)
    tr = _tile(s, max(8, (1 << 18) // tc), 8)
    n_in = len(args)

    def body(*refs):
        outs = fn(*[r[...] for r in refs[:n_in]])
        for o_ref, o in zip(refs[n_in:], outs):
            o_ref[...] = o.astype(o_ref.dtype)

    spec = pl.BlockSpec((tr, tc), lambda i, j: (i, j))
    return pl.pallas_call(
        body,
        name=name,
        grid=(s // tr, d // tc),
        in_specs=[spec] * n_in,
        out_specs=[spec] * len(out_dtypes),
        out_shape=[jax.ShapeDtypeStruct((s, d), dt) for dt in out_dtypes],
        compiler_params=_params("parallel", "parallel"),
    )(*args)


def _silu(v):
    return (v * jax.nn.sigmoid(v),)


def _split3(v):
    hi = v.astype(BF16)
    r1 = v - hi.astype(F32)
    mid = r1.astype(BF16)
    lo = (r1 - mid.astype(F32)).astype(BF16)
    return hi, mid, lo


def _band_dot(band, v):
    return sum(jnp.dot(band, part, preferred_element_type=F32) for part in _split3(v))


def _pool(h, *, name, transpose, out_dtype):
    s, d = h.shape
    c = d // len(POOL_WINDOWS)
    tr = _tile(s, 256, HALO)
    per = tr // HALO
    n_halo = s // HALO

    def body(h_ref, halo_ref, o_ref):
        i = pl.program_id(0)
        out_row = i * tr + lax.broadcasted_iota(jnp.int32, (tr, tr + HALO), 0)
        col = lax.broadcasted_iota(jnp.int32, (tr, tr + HALO), 1)
        if transpose:
            ext = jnp.concatenate([h_ref[...], halo_ref[...]], axis=0)
            src_row = i * tr + col
            ext_row = i * tr + lax.broadcasted_iota(jnp.int32, (tr + HALO, 1), 0)
        else:
            ext = jnp.concatenate([halo_ref[...], h_ref[...]], axis=0)
            src_row = i * tr + col - HALO
            own_row = i * tr + lax.broadcasted_iota(jnp.int32, (tr, 1), 0)
        for gi, w in enumerate(POOL_WINDOWS):
            cols = slice(gi * c, (gi + 1) * c)
            if transpose:
                band = (src_row >= out_row) & (src_row < out_row + w) & (src_row < s)
                scaled = ext[:, cols] / jnp.minimum(ext_row + 1, w).astype(F32)
                res = _band_dot(band.astype(BF16), scaled) - h_ref[:, cols]
            else:
                band = (src_row <= out_row) & (src_row > out_row - w) & (src_row >= 0)
                res = _band_dot(band.astype(BF16), ext[:, cols]) / jnp.minimum(own_row + 1, w).astype(F32) - h_ref[:, cols]
            o_ref[:, cols] = res.astype(o_ref.dtype)

    if transpose:
        halo_map = lambda i: (jnp.minimum((i + 1) * per, n_halo - 1), 0)
    else:
        halo_map = lambda i: (jnp.maximum(i * per - 1, 0), 0)
    return pl.pallas_call(
        body,
        name=name,
        grid=(s // tr,),
        in_specs=[pl.BlockSpec((tr, d), lambda i: (i, 0)), pl.BlockSpec((HALO, d), halo_map)],
        out_specs=pl.BlockSpec((tr, d), lambda i: (i, 0)),
        out_shape=jax.ShapeDtypeStruct((s, d), out_dtype),
        compiler_params=_params("parallel"),
    )(h, h)


def _rotate(v, cos, sin):
    lane = lax.broadcasted_iota(jnp.int32, v.shape, 1)
    swapped = jnp.where(lane % ROPE_DIM < ROPE_DIM // 2, pltpu.roll(v, LANES - ROPE_DIM // 2, 1), pltpu.roll(v, ROPE_DIM // 2, 1))
    return v * cos + swapped * sin


def _rope_heads(x, cos, sin, *, name, out_dtype):
    s, n = x.shape
    tr = _tile(s, max(16, (1 << 18) // n), 16)

    def body(x_ref, cos_ref, sin_ref, o_ref):
        cos_v, sin_v = cos_ref[...], sin_ref[...]
        for j in range(n // LANES):
            lanes = slice(j * LANES, (j + 1) * LANES)
            if j % 2 == 0:
                o_ref[:, lanes] = x_ref[:, lanes].astype(o_ref.dtype)
            else:
                o_ref[:, lanes] = _rotate(x_ref[:, lanes].astype(F32), cos_v, sin_v).astype(o_ref.dtype)

    blk = pl.BlockSpec((tr, n), lambda i: (i, 0))
    tab = pl.BlockSpec((tr, LANES), lambda i: (i, 0))
    return pl.pallas_call(
        body,
        name=name,
        grid=(s // tr,),
        in_specs=[blk, tab, tab],
        out_specs=blk,
        out_shape=jax.ShapeDtypeStruct((s, n), out_dtype),
        compiler_params=_params("parallel"),
    )(x, cos, sin)


def _build_keys(kv, kr_pre, cos, sin, *, name):
    s, n = kv.shape
    heads = n // HEAD_PAD
    t = _attn_tile(s)

    def body(kv_ref, kr_ref, cos_ref, sin_ref, keys_ref, kt_ref, vt_ref):
        rope = _rotate(kr_ref[...], cos_ref[...], sin_ref[...])
        nope, val = kv_ref[:, :NOPE_DIM], kv_ref[:, NOPE_DIM:]
        keys_ref[:, :NOPE_DIM] = nope
        keys_ref[:, NOPE_DIM:] = rope.astype(keys_ref.dtype)
        kt_ref[0, 0, :NOPE_DIM, :] = nope.astype(F32).T.astype(kt_ref.dtype)
        kt_ref[0, 0, NOPE_DIM:, :] = rope.T.astype(kt_ref.dtype)
        vt_ref[0, 0] = val.astype(F32).T.astype(vt_ref.dtype)

    tab = pl.BlockSpec((t, LANES), lambda hd, kb: (kb, 0))
    return pl.pallas_call(
        body,
        name=name,
        grid=(heads, s // t),
        in_specs=[pl.BlockSpec((t, HEAD_PAD), lambda hd, kb: (kb, hd)), tab, tab, tab],
        out_specs=[
            pl.BlockSpec((t, HEAD_PAD), lambda hd, kb: (kb, hd)),
            pl.BlockSpec((1, 1, HEAD_PAD, t), lambda hd, kb: (hd, kb, 0, 0)),
            pl.BlockSpec((1, 1, V_DIM, t), lambda hd, kb: (hd, kb, 0, 0)),
        ],
        out_shape=[
            jax.ShapeDtypeStruct((s, n), BF16),
            jax.ShapeDtypeStruct((heads, s // t, HEAD_PAD, t), BF16),
            jax.ShapeDtypeStruct((heads, s // t, V_DIM, t), BF16),
        ],
        compiler_params=_params("parallel", "parallel"),
    )(kv, kr_pre, cos, sin)


def _rope_back(dq_t, cos, sin, *, name):
    heads, nq, _, t = dq_t.shape

    def body(x_ref, cos_ref, sin_ref, o_ref):
        x = x_ref[0, 0].T * SM_SCALE
        o_ref[:, :NOPE_DIM] = x[:, :NOPE_DIM].astype(o_ref.dtype)
        o_ref[:, NOPE_DIM:] = _rotate(x[:, NOPE_DIM:], cos_ref[...], sin_ref[...]).astype(o_ref.dtype)

    tab = pl.BlockSpec((t, LANES), lambda hd, qb: (qb, 0))
    return pl.pallas_call(
        body,
        name=name,
        grid=(heads, nq),
        in_specs=[pl.BlockSpec((1, 1, HEAD_PAD, t), lambda hd, qb: (hd, qb, 0, 0)), tab, tab],
        out_specs=pl.BlockSpec((t, HEAD_PAD), lambda hd, qb: (qb, hd)),
        out_shape=jax.ShapeDtypeStruct((nq * t, heads * HEAD_PAD), BF16),
        compiler_params=_params("parallel", "parallel"),
    )(dq_t, cos, sin)


def _keys_bwd(dk_a, dk_b, dv_a, dv_b, cos, sin_neg, *, name):
    s, n = dk_a.shape
    heads = n // HEAD_PAD
    tr = _tile(s, 512, 8)

    def body(dka_ref, dkb_ref, dva_ref, dvb_ref, cos_ref, sin_ref, dkv_ref, dkr_ref):
        hd = pl.program_id(1)
        dk = dka_ref[...] + dkb_ref[...]
        dkv_ref[:, :NOPE_DIM] = dk[:, :NOPE_DIM].astype(dkv_ref.dtype)
        dkv_ref[:, NOPE_DIM:] = (dva_ref[...] + dvb_ref[...]).astype(dkv_ref.dtype)

        @pl.when(hd == 0)
        def _():
            dkr_ref[...] = dk[:, NOPE_DIM:]

        @pl.when(hd > 0)
        def _():
            dkr_ref[...] += dk[:, NOPE_DIM:]

        @pl.when(hd == heads - 1)
        def _():
            dkr_ref[...] = _rotate(dkr_ref[...], cos_ref[...], sin_ref[...])

    dk_blk = pl.BlockSpec((tr, HEAD_PAD), lambda i, hd: (i, hd))
    dv_blk = pl.BlockSpec((tr, V_DIM), lambda i, hd: (i, hd))
    tab = pl.BlockSpec((tr, LANES), lambda i, hd: (i, 0))
    return pl.pallas_call(
        body,
        name=name,
        grid=(s // tr, heads),
        in_specs=[dk_blk, dk_blk, dv_blk, dv_blk, tab, tab],
        out_specs=[dk_blk, tab],
        out_shape=[jax.ShapeDtypeStruct((s, n), BF16), jax.ShapeDtypeStruct((s, LANES), F32)],
        compiler_params=_params("parallel", "arbitrary"),
    )(dk_a, dk_b, dv_a, dv_b, cos, sin_neg)


def _attn_tile(s):
    return _tile(s, 512, LANES)


def _causal_mask(t):
    return lax.broadcasted_iota(jnp.int32, (t, t), 0) <= lax.broadcasted_iota(jnp.int32, (t, t), 1)


def _attn_fwd(q, keys, v_t, *, name, hosted=None):
    s = q.shape[0]
    heads = q.shape[1] // HEAD_PAD
    t = _attn_tile(s)
    nq = s // t

    def body(q_ref, k_ref, v_ref, o_ref, lse_ref, m_ref, l_ref, acc_ref):
        qi = pl.program_id(1)
        qv = q_ref[...]
        m_ref[...] = jnp.full_like(m_ref, -jnp.inf)
        l_ref[...] = jnp.zeros_like(l_ref)
        acc_ref[...] = jnp.zeros_like(acc_ref)

        def block(kb, diagonal):
            rows = pl.ds(pl.multiple_of(kb * t, t), t)
            sc_t = lax.dot_general(k_ref[rows, :], qv, _NT, preferred_element_type=F32) * SM_SCALE
            if diagonal:
                sc_t = jnp.where(_causal_mask(t), sc_t, -jnp.inf)
            m_old = m_ref[...]
            m_new = jnp.maximum(m_old, jnp.max(sc_t, axis=0, keepdims=True))
            alpha = jnp.exp(m_old - m_new)
            p_t = jnp.exp(sc_t - m_new)
            l_ref[...] = alpha * l_ref[...] + jnp.sum(p_t, axis=0, keepdims=True)
            acc_ref[...] = alpha * acc_ref[...] + jnp.dot(v_ref[0, kb], p_t.astype(BF16), preferred_element_type=F32)
            m_ref[...] = m_new

        def earlier(kb, carry):
            block(kb, False)
            return carry

        lax.fori_loop(0, qi, earlier, 0)
        block(qi, True)
        o_ref[...] = (acc_ref[...] / l_ref[...]).T.astype(o_ref.dtype)
        lse_ref[0, 0] = m_ref[...] + jnp.log(l_ref[...])

    return _call(
        body,
        name=name,
        grid=(heads, nq),
        in_specs=[
            pl.BlockSpec((t, HEAD_PAD), lambda hd, qi: (qi, hd)),
            pl.BlockSpec((s, HEAD_PAD), lambda hd, qi: (0, hd)),
            pl.BlockSpec((1, nq, V_DIM, t), lambda hd, qi: (hd, 0, 0, 0)),
        ],
        out_specs=[pl.BlockSpec((t, V_DIM), lambda hd, qi: (qi, hd)), pl.BlockSpec((1, 1, 1, t), lambda hd, qi: (hd, qi, 0, 0))],
        out_shape=[jax.ShapeDtypeStruct((s, heads * V_DIM), BF16), jax.ShapeDtypeStruct((heads, nq, 1, t), F32)],
        args=[q, keys, v_t],
        sem=("parallel", "parallel"),
        scratch_shapes=[pltpu.VMEM((1, t), F32), pltpu.VMEM((1, t), F32), pltpu.VMEM((V_DIM, t), F32)],
        hosted=hosted,
    )


def _attn_bwd(q, keys, keys_t, kv, o, do, lse, *, name, hosted=None):
    s = q.shape[0]
    heads = q.shape[1] // HEAD_PAD
    t = _attn_tile(s)
    nq = s // t

    def body(q_ref, k_ref, kt_ref, v_ref, o_ref, do_ref, lse_ref, dq_ref, dk_ref, dv_ref, dk_acc, dv_acc, delta_ref):
        ki = pl.program_id(1)

        @pl.when(ki == 0)
        def _():
            dq_ref[...] = jnp.zeros_like(dq_ref)
            ones = jnp.ones((8, V_DIM), BF16)

            def row_sums(qb, carry):
                rows = pl.ds(pl.multiple_of(qb * t, t), t)
                prod = do_ref[rows, :].astype(F32) * o_ref[rows, :].astype(F32)
                sums = sum(lax.dot_general(ones, part, _NT, preferred_element_type=F32) for part in _split3(prod))
                delta_ref[qb] = sums[0:1]
                return carry

            lax.fori_loop(0, nq, row_sums, 0)

        kv_, vv, kt = k_ref[...], v_ref[...], kt_ref[0, 0]
        dk_acc[...] = jnp.zeros_like(dk_acc)
        dv_acc[...] = jnp.zeros_like(dv_acc)

        def block(qb, diagonal):
            rows = pl.ds(pl.multiple_of(qb * t, t), t)
            qv, dov = q_ref[rows, :], do_ref[rows, :]
            sc_t = lax.dot_general(kv_, qv, _NT, preferred_element_type=F32) * SM_SCALE
            p_t = jnp.exp(sc_t - lse_ref[0, qb])
            if diagonal:
                p_t = jnp.where(_causal_mask(t), p_t, 0.0)
            dv_acc[...] += jnp.dot(p_t.astype(BF16), dov, preferred_element_type=F32)
            dp_t = lax.dot_general(vv, dov, _NT, preferred_element_type=F32)
            ds_t = (p_t * (dp_t - delta_ref[qb])).astype(BF16)
            dk_acc[...] += jnp.dot(ds_t, qv, preferred_element_type=F32)
            dq_ref[0, qb] += jnp.dot(kt, ds_t, preferred_element_type=F32)

        def later(qb, carry):
            block(qb, False)
            return carry

        block(ki, True)
        lax.fori_loop(ki + 1, nq, later, 0)
        dk_ref[...] = dk_acc[...] * SM_SCALE
        dv_ref[...] = dv_acc[...]

    whole = pl.BlockSpec((s, V_DIM), lambda hd, ki: (0, hd))
    return _call(
        body,
        name=name,
        grid=(heads, nq),
        in_specs=[
            pl.BlockSpec((s, HEAD_PAD), lambda hd, ki: (0, hd)),
            pl.BlockSpec((t, HEAD_PAD), lambda hd, ki: (ki, hd)),
            pl.BlockSpec((1, 1, HEAD_PAD, t), lambda hd, ki: (hd, ki, 0, 0)),
            pl.BlockSpec((t, V_DIM), lambda hd, ki: (ki, 2 * hd + 1)),
            whole,
            whole,
            pl.BlockSpec((1, nq, 1, t), lambda hd, ki: (hd, 0, 0, 0)),
        ],
        out_specs=[
            pl.BlockSpec((1, nq, HEAD_PAD, t), lambda hd, ki: (hd, 0, 0, 0)),
            pl.BlockSpec((t, HEAD_PAD), lambda hd, ki: (ki, hd)),
            pl.BlockSpec((t, V_DIM), lambda hd, ki: (ki, hd)),
        ],
        out_shape=[
            jax.ShapeDtypeStruct((heads, nq, HEAD_PAD, t), F32),
            jax.ShapeDtypeStruct((s, heads * HEAD_PAD), F32),
            jax.ShapeDtypeStruct((s, heads * V_DIM), F32),
        ],
        args=[q, keys, keys_t, kv, o, do, lse],
        sem=("parallel", "arbitrary"),
        scratch_shapes=[pltpu.VMEM((t, HEAD_PAD), F32), pltpu.VMEM((t, V_DIM), F32), pltpu.VMEM((nq, 1, t), F32)],
        hosted=hosted,
    )


def _loss_bwd(x, y, gate, g, target, *, name):
    s, d = x.shape
    tr = _row_tile(s, d)

    def body(x_ref, y_ref, gate_ref, g_ref, t_ref, dx_ref, stats_ref, loss_ref):
        xv = x_ref[...] + gate_ref[...] * y_ref[...]
        r = lax.rsqrt(jnp.mean(xv * xv, axis=-1, keepdims=True) + NORM_EPS)
        xhat = xv * r
        gv = g_ref[...]
        err = xhat * gv - t_ref[...]
        dy = err / d
        dxhat = dy * gv
        dx_ref[...] = r * (dxhat - xhat * jnp.mean(dxhat * xhat, axis=-1, keepdims=True))

        @pl.when(pl.program_id(0) == 0)
        def _():
            stats_ref[...] = jnp.zeros_like(stats_ref)
            loss_ref[...] = jnp.zeros_like(loss_ref)

        stats_ref[0:1, :] += jnp.sum(dy * xhat, axis=0, keepdims=True)
        loss_ref[...] += 0.5 * jnp.sum(jnp.mean(err * err, axis=-1, keepdims=True))

    row = pl.BlockSpec((tr, d), lambda i: (i, 0))
    vec = pl.BlockSpec((1, d), lambda i: (0, 0))
    return pl.pallas_call(
        body,
        name=name,
        grid=(s // tr,),
        in_specs=[row, row, vec, vec, row],
        out_specs=[row, pl.BlockSpec((8, d), lambda i: (0, 0)), pl.BlockSpec((8, LANES), lambda i: (0, 0))],
        out_shape=[jax.ShapeDtypeStruct((s, d), F32), jax.ShapeDtypeStruct((8, d), F32), jax.ShapeDtypeStruct((8, LANES), F32)],
        compiler_params=_params("arbitrary"),
    )(x, y, gate, g, target)


def _adam_math(w, g, m, v):
    new_m = ADAM_B1 * m + (1.0 - ADAM_B1) * g
    new_v = ADAM_B2 * v + (1.0 - ADAM_B2) * (g * g)
    m_hat = new_m / (1.0 - ADAM_B1**ADAM_STEP)
    v_hat = new_v / (1.0 - ADAM_B2**ADAM_STEP)
    return -ADAM_LR * (m_hat / (jnp.sqrt(v_hat) + ADAM_EPS) + ADAM_WD * w), new_m, new_v


def _adamw(w, g, m, v, *, name):
    rows, cols = w.shape
    tr = _tile(rows, max(8, (1 << 19) // cols), 8)

    def body(w_ref, g_ref, m_ref, v_ref, go_ref, d_ref, mo_ref, vo_ref):
        gv = g_ref[...]
        go_ref[...] = gv
        d_ref[...], mo_ref[...], vo_ref[...] = _adam_math(w_ref[...], gv, m_ref[...], v_ref[...])

    spec = pl.BlockSpec((tr, cols), lambda i: (i, 0))
    return pl.pallas_call(
        body,
        name=name,
        grid=(rows // tr,),
        in_specs=[spec] * 4,
        out_specs=[spec] * 4,
        out_shape=[jax.ShapeDtypeStruct((rows, cols), F32)] * 4,
        compiler_params=_params("parallel"),
    )(w, g, m, v)


def _tp_adamw(sc16, dm, w, m, v, *, name):
    nl, d, n = w.shape
    tm = _tile(d, 512, LANES)
    tn = _tile(n, 1024, LANES)

    def body(sc_ref, dm_ref, w_ref, m_ref, v_ref, go_ref, d_ref, mo_ref, vo_ref):
        gv = lax.dot_general(sc_ref[...].astype(BF16), dm_ref[0].astype(BF16), (((0,), (0,)), ((), ())), preferred_element_type=F32)
        go_ref[0] = gv
        d_ref[0], mo_ref[0], vo_ref[0] = _adam_math(w_ref[0], gv, m_ref[0], v_ref[0])

    blk = pl.BlockSpec((1, tm, tn), lambda l, i, j: (l, i, j))
    return pl.pallas_call(
        body,
        name=name,
        grid=(nl, d // tm, n // tn),
        in_specs=[pl.BlockSpec((16, tm), lambda l, i, j: (0, i)), pl.BlockSpec((1, 16, tn), lambda l, i, j: (l, 0, j)), blk, blk, blk],
        out_specs=[blk] * 4,
        out_shape=[jax.ShapeDtypeStruct((nl, d, n), F32)] * 4,
        compiler_params=_params("parallel", "parallel", "parallel"),
    )(sc16, dm, w, m, v)


def _sum_devices(x, *, name):
    def body(x_ref, o_ref):
        acc = x_ref[0]
        for k in range(1, N_DEV):
            acc = acc + x_ref[k]
        o_ref[...] = acc

    return pl.pallas_call(body, name=name, out_shape=jax.ShapeDtypeStruct(x.shape[1:], F32))(x)


def _place():
    mx, my, mc = lax.axis_index("x"), lax.axis_index("y"), lax.axis_index("c")
    chips = [(1 - mx, my), (mx, 1 - my), (1 - mx, 1 - my)]
    return mx, my, mc, chips


def _remote(src, dst, send_sem, recv_sem, device):
    return pltpu.make_async_remote_copy(src_ref=src, dst_ref=dst, send_sem=send_sem, recv_sem=recv_sem, device_id=device, device_id_type=MESH)


def _allgather8(x, *, name):
    def body(x_ref, out_ref, send_sems, recv_sems, local_sem):
        mx, my, mc, chips = _place()
        me, sibling = (mx, my, mc), (mx, my, 1 - mc)

        def slot(px, py, pc):
            return out_ref.at[4 * px + 2 * py + pc]

        def copy(k, block, to, src=None):
            return _remote(slot(*block) if src is None else src, slot(*block), send_sems.at[k], recv_sems.at[k], to)

        mine = pltpu.make_async_copy(x_ref, slot(*me), local_sem)
        mine.start()
        first = [copy(0, me, sibling, src=x_ref)] + [copy(1 + j, me, (*chip, mc), src=x_ref) for j, chip in enumerate(chips)]
        for cp in first:
            cp.start()
        passed = [copy(4 + j, (*chip, mc), sibling) for j, chip in enumerate(chips)]
        for j, chip in enumerate(chips):
            copy(1 + j, (*chip, mc), me).wait_recv()
            passed[j].start()
        copy(0, sibling, me).wait_recv()
        for j, chip in enumerate(chips):
            copy(4 + j, (*chip, 1 - mc), me).wait_recv()
        for cp in first + passed:
            cp.wait_send()
        mine.wait()

    return pl.pallas_call(
        body,
        name=name,
        out_shape=jax.ShapeDtypeStruct((N_DEV,) + x.shape, x.dtype),
        in_specs=[pl.BlockSpec(memory_space=pltpu.VMEM)],
        out_specs=pl.BlockSpec(memory_space=pltpu.VMEM),
        scratch_shapes=[pltpu.SemaphoreType.DMA((7,)), pltpu.SemaphoreType.DMA((7,)), pltpu.SemaphoreType.DMA],
    )(x)


class _Geom:
    def __init__(self, shape3, axis):
        self.shape3, self.axis = shape3, axis
        nl, r, c = shape3
        self.rs, self.cs = (r // N_CHIPS, c) if axis == 1 else (r, c // N_CHIPS)
        self.hl, self.hr = (nl // 2, self.rs) if nl > 1 else (1, self.rs // 2)
        self.shard = (nl, self.rs, self.cs)
        self.half = (self.hl, self.hr, self.cs)

    def in_full(self, ref, chip, core):
        nl = self.shape3[0]
        l0 = core * self.hl if nl > 1 else 0
        r0 = (chip * self.rs if self.axis == 1 else 0) + (0 if nl > 1 else core * self.hr)
        c0 = chip * self.cs if self.axis == 2 else 0
        return ref.at[pl.ds(l0, self.hl), pl.ds(r0, self.hr), pl.ds(c0, self.cs)]


def _place_shard(shard, geom, chip_arr, *, name, layer=None):
    nl, rs, cs = geom.shard
    tr = _tile(rs, max(16, (1 << 20) // cs), 16)
    per = rs // tr
    first = 0 if layer is None else layer

    def body(chip_ref, x_ref, o_ref):
        o_ref[...] = x_ref[...].astype(o_ref.dtype)

    def out_map(l, i, chip_ref):
        return (l, chip_ref[0] * per + i, 0) if geom.axis == 1 else (l, i, chip_ref[0])

    return pl.pallas_call(
        body,
        name=name,
        grid_spec=pltpu.PrefetchScalarGridSpec(
            num_scalar_prefetch=1,
            grid=(nl, per),
            in_specs=[pl.BlockSpec((1, tr, cs), lambda l, i, chip_ref: (first + l, i, 0))],
            out_specs=pl.BlockSpec((1, tr, cs), out_map),
        ),
        out_shape=jax.ShapeDtypeStruct(geom.shape3, BF16),
        compiler_params=_params("parallel", "parallel"),
    )(chip_arr, shard)


def _dma_sems(count, arrays):
    return [pltpu.SemaphoreType.DMA((count,))] * arrays


def _gather_plan(fulls, geoms):
    def ici(w, k, src, dst, sems, device):
        return _remote(src, dst, sems[0].at[3 * w + k], sems[1].at[3 * w + k], device)

    def d2d(w, k, box, sems, device):
        return _remote(box, box, sems[2].at[3 * w + k], sems[3].at[3 * w + k], device)

    def start(given, full, sems):
        mx, my, mc, chips = _place()
        me = 2 * mx + my
        for w, geom in enumerate(geoms):
            for k, chip in enumerate(chips):
                ici(w, k, geom.in_full(given[w], me, mc), geom.in_full(full[w], me, mc), sems, (*chip, mc)).start()

    def finish(given, full, sems):
        mx, my, mc, chips = _place()
        me, sibling = 2 * mx + my, (mx, my, 1 - mc)
        for w, geom in enumerate(geoms):
            for k, (px, py) in enumerate(chips):
                landed = geom.in_full(full[w], 2 * px + py, mc)
                ici(w, k, landed, landed, sems, (px, py, mc)).wait_recv()
                d2d(w, k, landed, sems, sibling).start()
        for w, geom in enumerate(geoms):
            for k, (px, py) in enumerate(chips):
                d2d(w, k, geom.in_full(full[w], 2 * px + py, 1 - mc), sems, sibling).wait_recv()
        for w, geom in enumerate(geoms):
            for k, (px, py) in enumerate(chips):
                ici(w, k, geom.in_full(given[w], me, mc), geom.in_full(full[w], me, mc), sems, (px, py, mc)).wait_send()
                d2d(w, k, geom.in_full(full[w], 2 * px + py, mc), sems, sibling).wait_send()

    n = len(fulls)
    shapes = [jax.ShapeDtypeStruct(f.shape, f.dtype) for f in fulls]
    return _Hosted(fulls, shapes, {w: w for w in range(n)}, _dma_sems(3 * n, 4), start, finish)


def _pair_plan(grads, geoms):
    def copies(grad, theirs, sems):
        mx, my, mc, _ = _place()
        return [
            _remote(geom.in_full(grad[w], chip, 1 - mc), theirs[w].at[chip], sems[0].at[4 * w + chip], sems[1].at[4 * w + chip], (mx, my, 1 - mc))
            for w, geom in enumerate(geoms)
            for chip in range(N_CHIPS)
        ]

    def start(grad, theirs, sems):
        for cp in copies(grad, theirs, sems):
            cp.start()

    def finish(grad, theirs, sems):
        for cp in copies(grad, theirs, sems):
            cp.wait_recv()
        for cp in copies(grad, theirs, sems):
            cp.wait_send()

    shapes = [jax.ShapeDtypeStruct((N_CHIPS,) + g.half, x.dtype) for g, x in zip(geoms, grads)]
    return _Hosted(grads, shapes, {}, _dma_sems(4 * len(grads), 2), start, finish)


def _half_tile(geom):
    return _tile(geom.hr, max(16, (1 << 20) // geom.cs), 16)


def _pair_add(grad, theirs, geom, core_arr, *, name):
    hl, hr, cs = geom.half
    tr = _half_tile(geom)
    stacked = geom.shape3[0] > 1

    def grad_map(chip, l, i, core_ref):
        layer = core_ref[0] * hl + l if stacked else 0
        row = (chip * (geom.rs // tr) if geom.axis == 1 else 0) + (0 if stacked else core_ref[0] * (hr // tr)) + i
        return layer, row, (chip if geom.axis == 2 else 0)

    def body(core_ref, g_ref, t_ref, o_ref):
        o_ref[0] = (g_ref[...].astype(F32) + t_ref[0].astype(F32)).astype(o_ref.dtype)

    blk = pl.BlockSpec((1, 1, tr, cs), lambda chip, l, i, core_ref: (chip, l, i, 0))
    return pl.pallas_call(
        body,
        name=name,
        grid_spec=pltpu.PrefetchScalarGridSpec(
            num_scalar_prefetch=1, grid=(N_CHIPS, hl, hr // tr), in_specs=[pl.BlockSpec((1, tr, cs), grad_map), blk], out_specs=blk
        ),
        out_shape=jax.ShapeDtypeStruct(theirs.shape, BF16),
        compiler_params=_params("parallel", "parallel", "parallel"),
    )(core_arr, grad, theirs)


def _chips_plan(parts):
    def copies(part, slots, sems):
        _, _, mc, chips = _place()
        return [
            _remote(part[w].at[2 * px + py], slots[w].at[k], sems[0].at[3 * w + k], sems[1].at[3 * w + k], (px, py, mc))
            for w in range(len(parts))
            for k, (px, py) in enumerate(chips)
        ]

    def start(part, slots, sems):
        for cp in copies(part, slots, sems):
            cp.start()

    def finish(part, slots, sems):
        for cp in copies(part, slots, sems):
            cp.wait_recv()
        for cp in copies(part, slots, sems):
            cp.wait_send()

    shapes = [jax.ShapeDtypeStruct((N_CHIPS - 1,) + p.shape[1:], p.dtype) for p in parts]
    return _Hosted(parts, shapes, {}, _dma_sems(3 * len(parts), 2), start, finish)


def _chip_sum(part, slots, geom, place_arr, *, name, stack=None):
    hl, hr, cs = geom.half
    tr = _half_tile(geom)

    def body(place_ref, own_ref, s0_ref, s1_ref, s2_ref, *rest):
        o_ref = rest[-1]
        o_ref[...] = ((own_ref[...].astype(F32) + s0_ref[...].astype(F32)) + s1_ref[...].astype(F32)) + s2_ref[...].astype(F32)

    def slot(k):
        return pl.BlockSpec((1, 1, tr, cs), lambda l, i, place_ref: (k, l, i, 0))

    in_specs = [pl.BlockSpec((1, 1, tr, cs), lambda l, i, place_ref: (place_ref[0], l, i, 0)), slot(0), slot(1), slot(2)]
    args = [place_arr, part, slots, slots, slots]
    aliases = {}
    if stack is None:
        out_spec = pl.BlockSpec((1, 1, tr, cs), lambda l, i, place_ref: (place_ref[1], l, i, 0))
        out_shape = jax.ShapeDtypeStruct((2,) + geom.half, F32)
    else:
        layers, layer, prev = stack
        assert hl == 1
        out_spec = pl.BlockSpec((1, 1, tr, cs), lambda l, i, place_ref: (layer, place_ref[1], i, 0))
        out_shape = jax.ShapeDtypeStruct((layers, 2, hr, cs), F32)
        if prev is not None:
            aliases = {len(args): 0}
            in_specs.append(ANY)
            args.append(prev)
    return pl.pallas_call(
        body,
        name=name,
        grid_spec=pltpu.PrefetchScalarGridSpec(num_scalar_prefetch=1, grid=(hl, hr // tr), in_specs=in_specs, out_specs=out_spec),
        out_shape=out_shape,
        input_output_aliases=aliases,
        compiler_params=_params("parallel", "parallel"),
    )(*args)


def _join_plan(boths, prefixes):
    def copies(given, both, sems):
        mx, my, mc, _ = _place()
        out, n = [], 0
        for w in range(len(boths)):
            for p in prefixes[w]:
                out.append(_remote(given[w].at[(*p, mc)], both[w].at[(*p, mc)], sems[0].at[n], sems[1].at[n], (mx, my, 1 - mc)))
                n += 1
        return out

    def arrivals(both, sems):
        mx, my, mc, _ = _place()
        out, n = [], 0
        for w in range(len(boths)):
            for p in prefixes[w]:
                got = both[w].at[(*p, 1 - mc)]
                out.append(_remote(got, got, sems[0].at[n], sems[1].at[n], (mx, my, 1 - mc)))
                n += 1
        return out

    def start(given, both, sems):
        for cp in copies(given, both, sems):
            cp.start()

    def finish(given, both, sems):
        for cp in arrivals(both, sems):
            cp.wait_recv()
        for cp in copies(given, both, sems):
            cp.wait_send()

    count = sum(len(p) for p in prefixes)
    shapes = [jax.ShapeDtypeStruct(b.shape, b.dtype) for b in boths]
    return _Hosted(boths, shapes, {w: w for w in range(len(boths))}, _dma_sems(count, 2), start, finish)


WEIGHTS = ("mod_w", "mod_b", "norm_mix", "norm_ffn", "pool_w", "pool_scale", "kv_mod_w", "kv_mod_b", "kv_in_norm", "w_dkv", "kv_norm",
           "w_uk", "w_uv", "w_kr", "w_dq", "q_norm", "w_uq", "w_o", "ffn_gate", "ffn_up", "ffn_down", "final_norm")
SMALL = ("mod_b", "kv_mod_b", "norm_mix", "norm_ffn", "kv_in_norm", "kv_norm", "q_norm", "final_norm")


def _rows(v):
    return v.reshape(-1, LANES)


def _pad_rows(a):
    return jnp.pad(a, ((0, (-a.shape[0]) % 8), (0, 0)))


def _vec(v):
    return v.reshape(1, -1)


def _step(x, c, positions, target, wts, mom, var):
    _, s, d = x.shape
    depth, n_a, n_b = wts["mod_w"].shape[0], wts["pool_w"].shape[0], wts["w_dq"].shape[0]
    assert n_b == 2 and n_a + n_b == depth
    heads = d // V_DIM
    kvr, qr = wts["w_dkv"].shape[1], wts["w_dq"].shape[2]
    ffn = wts["ffn_gate"].shape[2] * N_CHIPS
    pool_c = d // len(POOL_WINDOWS)
    nmod, nkv = N_MOD * d, 2 * d
    mx, my, mc = lax.axis_index("x"), lax.axis_index("y"), lax.axis_index("c")
    chip, dev = 2 * mx + my, 4 * mx + 2 * my + mc
    xs, tgt = x[0], target[0]

    inv_freq = 1.0 / (ROPE_THETA ** (jnp.arange(0, ROPE_DIM, 2, dtype=F32) / ROPE_DIM))
    ang = positions[0].astype(F32)[:, None] * inv_freq
    cos, sin, zero = jnp.cos(ang), jnp.sin(ang), jnp.zeros((s, LANES - ROPE_DIM), F32)
    cos_t = jnp.concatenate([cos, cos, zero], axis=1)
    sin_fwd = jnp.concatenate([-sin, sin, zero], axis=1)
    sin_bwd = jnp.concatenate([sin, -sin, zero], axis=1)

    c_rows, ps_rows = d // LANES, n_a * (d // N_CHIPS) // LANES
    cond = _allgather8(_pad_rows(jnp.concatenate([_rows(c), _rows(wts["pool_scale"])])), name="gather_cond")
    c_all = cond[:, :c_rows].reshape(N_DEV, d)
    pool_scale = cond[0::2, c_rows : c_rows + ps_rows].reshape(N_CHIPS, n_a, d // N_CHIPS).transpose(1, 0, 2).reshape(n_a, d)
    sc16 = _elementwise(_silu, [jnp.pad(c_all, ((0, 16 - N_DEV), (0, 0)))], [F32], name="silu_cond")[0]

    mod_bias = lax.dynamic_slice_in_dim(wts["mod_b"], chip * (nmod // N_CHIPS), nmod // N_CHIPS, axis=1)[:, None, :]
    kv_bias = lax.dynamic_slice_in_dim(wts["kv_mod_b"], chip * (nkv // N_CHIPS), nkv // N_CHIPS).reshape(1, 1, -1)
    mod_part = _tp_fwd(sc16, wts["mod_w"], mod_bias, name="mod_fwd")
    kv_part = _tp_fwd(sc16, wts["kv_mod_w"][None], kv_bias, name="kv_mod_fwd")
    part = jnp.concatenate([mod_part[i, :N_DEV] for i in range(depth)] + [kv_part[0, :N_DEV]], axis=1)
    ncol = part.shape[1]
    gathered = _allgather8(_pad_rows(_rows(part)), name="gather_mods")
    gathered = gathered[0::2, : N_DEV * ncol // LANES].reshape(N_CHIPS, N_DEV, ncol)
    mine = lax.dynamic_index_in_dim(gathered, dev, axis=1, keepdims=False)
    per = nmod // N_CHIPS
    mods = [mine[:, i * per : (i + 1) * per].reshape(N_MOD, 1, d) for i in range(depth)]
    kv_shift, kv_scale = mine[:, depth * per :].reshape(2, 1, d)

    mixer_names = ("pool_w", "w_dkv", "w_uk", "w_uv", "w_kr", "w_dq", "w_uq", "w_o")
    ffn_names = ("ffn_gate", "ffn_up", "ffn_down")
    geoms = {
        "pool_w": _Geom((n_a * len(POOL_WINDOWS), pool_c, pool_c), 1),
        "w_dkv": _Geom((1, d, kvr), 1),
        "w_uk": _Geom((1, kvr, heads * NOPE_DIM), 2),
        "w_uv": _Geom((1, kvr, heads * V_DIM), 2),
        "w_kr": _Geom((1, d, ROPE_DIM), 1),
        "w_dq": _Geom((n_b, d, qr), 1),
        "w_uq": _Geom((n_b, qr, heads * (NOPE_DIM + ROPE_DIM)), 2),
        "w_o": _Geom((n_b, d, d), 1),
        "ffn_gate": _Geom((1, d, ffn), 2),
        "ffn_up": _Geom((1, d, ffn), 2),
        "ffn_down": _Geom((1, ffn, d), 1),
    }
    mixer_geoms = [geoms[n] for n in mixer_names]
    ffn_geoms = [geoms[n] for n in ffn_names]
    chip_arr, core_arr, place_arr = chip.reshape(1), mc.reshape(1), jnp.stack([chip, mc])
    placed = [_place_shard(wts[n].reshape(geoms[n].shard), geoms[n], chip_arr, name="place_" + n) for n in mixer_names]
    placed_ffn = [[_place_shard(wts[n], geoms[n], chip_arr, layer=i, name="place_" + n) for n in ffn_names] for i in range(depth)]
    first = _run(_gather_plan(placed + placed_ffn[0], mixer_geoms + ffn_geoms), name="gather_first")
    full = dict(zip(mixer_names, first))
    whole_k = dict(tm=512, tn=1024, tk=max(s, ffn))
    ffn_w = [None] * depth
    ffn_w[0] = first[len(mixer_names) :]

    pool_w = full["pool_w"].reshape(n_a, len(POOL_WINDOWS), pool_c, pool_c)
    w_uq = full["w_uq"].reshape(n_b, qr, heads, NOPE_DIM + ROPE_DIM)
    w_q = jnp.pad(w_uq, ((0, 0), (0, 0), (0, 0), (0, HEAD_PAD - NOPE_DIM - ROPE_DIM))).reshape(n_b, qr, heads * HEAD_PAD)
    w_ukv = jnp.stack([full["w_uk"].reshape(kvr, heads, NOPE_DIM), full["w_uv"].reshape(kvr, heads, V_DIM)], axis=2).reshape(kvr, heads * HEAD_PAD)
    w_dkvkr = jnp.concatenate([full["w_dkv"][0], full["w_kr"][0], jnp.zeros((d, LANES - ROPE_DIM), BF16)], axis=1)
    w_dq, w_o = full["w_dq"], full["w_o"]

    norm_mix, norm_ffn = wts["norm_mix"], wts["norm_ffn"]
    saved = []
    cur, pending = xs, None
    kv_side = None
    for i in range(depth):
        shift_m, scale_m, gate_m, shift_f, scale_f, gate_f = mods[i]
        h1_dtype = F32 if i < n_a else BF16
        if pending is None:
            x0 = cur
            h1 = _norm_fwd(x0, _vec(norm_mix[i]), scale=scale_m, shift=shift_m, out_dtype=h1_dtype, name="norm_mix_first")
        else:
            x0, h1 = _norm_fwd(cur, _vec(norm_mix[i]), scale=scale_m, shift=shift_m, y=pending[0], gate=pending[1], out_dtype=h1_dtype, name="norm_mix")
        lay = {"x0": x0, "h1": h1}
        if i == n_a:
            h_kv = _norm_fwd(x0, _vec(wts["kv_in_norm"]), scale=kv_scale, shift=kv_shift, name="norm_kv_in")
            pre = _mm(h_kv, w_dkvkr, name="kv_down", tn=kvr + LANES)
            ckv_pre, kr_pre = pre[:, :kvr], pre[:, kvr:]
            ckv = _norm_fwd(ckv_pre, _vec(wts["kv_norm"]), name="norm_kv")
            kv = _mm(ckv, w_ukv, out_dtype=BF16, name="kv_up")
            keys, keys_t, v_t = _build_keys(kv, kr_pre, cos_t, sin_fwd, name="build_keys")
            kv_side = {"h_kv": h_kv, "ckv_pre": ckv_pre, "ckv": ckv, "kv": kv, "keys": keys, "keys_t": keys_t, "v_t": v_t, "x0": x0}
        if i < n_a:
            pooled = _pool(h1, transpose=False, out_dtype=BF16, name="pool_fwd")
            y_pre = _gmm(pooled, pool_w[i], mode="nn", out_dtype=F32, name="pool_mix")
            gate_eff = gate_m * _vec(pool_scale[i])
            lay.update(pooled=pooled)
        else:
            l = i - n_a
            cq_pre = _mm(h1, w_dq[l], name="q_down")
            cq = _norm_fwd(cq_pre, _vec(wts["q_norm"][l]), name="norm_q")
            q = _rope_heads(_mm(cq, w_q[l], name="q_up"), cos_t, sin_fwd, out_dtype=BF16, name="rope_q")
            if i + 1 < depth:
                (o, lse), ffn_w[i + 1] = _attn_fwd(q, kv_side["keys"], kv_side["v_t"], hosted=_gather_plan(placed_ffn[i + 1], ffn_geoms), name="attn_fwd")
            else:
                (o, lse), _ = _attn_fwd(q, kv_side["keys"], kv_side["v_t"], name="attn_fwd_last")
            y_pre = _mm(o, w_o[l], name="attn_out")
            gate_eff = gate_m
            lay.update(cq_pre=cq_pre, cq=cq, q=q, o=o, lse=lse)
        x1, h2 = _norm_fwd(x0, _vec(norm_ffn[i]), scale=scale_f, shift=shift_f, y=y_pre, gate=gate_eff, name="norm_ffn")
        w_gate, w_up, w_down = ffn_w[i]
        if i + 1 < depth and ffn_w[i + 1] is None:
            (a, b, z), next_in = _ffn_in(h2, w_gate, w_up, 0, hosted=_gather_plan(placed_ffn[i + 1][:2], ffn_geoms[:2]), name="ffn_in")
            f, next_down = _mm(z, w_down, b_idx=0, hosted=_gather_plan(placed_ffn[i + 1][2:], ffn_geoms[2:]), name="ffn_down", **whole_k)
            ffn_w[i + 1] = next_in + next_down
        else:
            (a, b, z), _ = _ffn_in(h2, w_gate, w_up, 0, name="ffn_in_last")
            f = _mm(z, w_down, b_idx=0, name="ffn_down_last", **whole_k)
        lay.update(y_pre=y_pre, gate_eff=gate_eff, x1=x1, h2=h2, a=a, b=b, z=z, f=f)
        saved.append(lay)
        cur, pending = x1, (f, gate_f)

    dx, final_stats, loss_tile = _loss_bwd(cur, pending[0], pending[1], _vec(wts["final_norm"]), tgt, name="loss")
    loss = lax.psum(loss_tile[0, 0], ("x", "y", "c"))

    ffn_both = [None] * len(ffn_names)
    in_flight = None
    attn_names = tuple(n for n in mixer_names if n != "pool_w")
    grad_full = {}
    attn_parts = attn_slots = None

    def sum_chips(layer, which, parts, slots):
        for w, part, slot in zip(which, parts, slots):
            ffn_both[w] = _chip_sum(part, slot, ffn_geoms[w], place_arr, stack=(depth, layer, ffn_both[w]), name="chip_sum_" + ffn_names[w])

    g_wo, g_wq, g_wdq = [None] * n_b, [None] * n_b, [None] * n_b
    g_pool = [None] * n_a
    dmods = [None] * depth
    g_norm_mix, g_norm_ffn, g_q_norm, g_pool_scale = [None] * depth, [None] * depth, [None] * n_b, [None] * n_a
    dk_layers, dv_layers = [None] * n_b, [None] * n_b
    for i in reversed(range(depth)):
        lay = saved[i]
        shift_m, scale_m, gate_m, shift_f, scale_f, gate_f = mods[i]
        df, sums_gf = _gate_bwd(dx, lay["f"], gate_f, name="gate_bwd")
        w_gate, w_up, w_down = ffn_w[i]
        g_down = _mm(lay["z"], df, ta=True, out_dtype=BF16, name="ffn_down_dw", **whole_k)
        carry_attn = attn_parts is not None and attn_slots is None
        riding = [] if in_flight is None else in_flight
        plans = [_pair_plan([g_down[None]], ffn_geoms[2:])] + ([_chips_plan(riding[:1])] if riding else [])
        (da, db), got = _ffn_down_bwd(df, w_down, 0, lay["a"], lay["b"], hosted=_merge(plans), name="ffn_down_bwd")
        their_down, got_gate = got[0], got[1:]
        down_part = _pair_add(g_down[None], their_down, ffn_geoms[2], core_arr, name="pair_add_ffn_down")
        plans = [_chips_plan(part) for part in (riding[1:], attn_parts if carry_attn else None) if part]
        (dh2,), got = _ffn_in_dx(da, db, w_gate, w_up, 0, hosted=_merge(plans) if plans else None, name="ffn_in_dx")
        got_up, got = got[: len(riding[1:])], got[len(riding[1:]) :]
        if carry_attn:
            attn_slots = got
        (g_gate, g_up), got_down = _ffn_in_dw(lay["h2"], da, db, hosted=_chips_plan([down_part]), name="ffn_in_dw")
        sum_chips(i, [2], [down_part], got_down)
        if riding:
            sum_chips(i + 1, [0, 1], riding, got_gate + got_up)
        (dx1, sums_f), their_in = _norm_bwd(
            lay["x1"], _vec(norm_ffn[i]), dh2, scale=scale_f, resid=dx, hosted=_pair_plan([g_gate[None], g_up[None]], ffn_geoms[:2]), name="norm_ffn_bwd"
        )
        in_flight = [
            _pair_add(g[None], th, geom, core_arr, name="pair_add_" + n) for g, th, geom, n in zip((g_gate, g_up), their_in, ffn_geoms, ffn_names)
        ]
        dyp, sums_gm = _gate_bwd(dx1, lay["y_pre"], lay["gate_eff"], name="gate_bwd")
        if i < n_a:
            g_pool[i] = _gmm(lay["pooled"], dyp, mode="tn", out_dtype=BF16, name="pool_mix_dw")
            dd = _gmm(dyp, pool_w[i], mode="nt", out_dtype=F32, name="pool_mix_dx")
            dh1 = _pool(dd, transpose=True, out_dtype=F32, name="pool_bwd")
            dgate_m = sums_gm[0] * pool_scale[i]
            g_pool_scale[i] = sums_gm[0] * gate_m[0]
        else:
            l = i - n_a
            do = _mm(dyp, w_o[l], tb=True, out_dtype=BF16, name="attn_out_dx")
            g_wo[l] = _mm(lay["o"], dyp, ta=True, out_dtype=BF16, name="attn_out_dw")
            (dq_t, dk_layers[l], dv_layers[l]), got = _attn_bwd(
                lay["q"], kv_side["keys"], kv_side["keys_t"], kv_side["kv"], lay["o"], do, lay["lse"], hosted=_chips_plan(in_flight), name="attn_bwd"
            )
            sum_chips(i, [0, 1], in_flight, got)
            in_flight = None
            dq_pre = _rope_back(dq_t, cos_t, sin_bwd, name="rope_q_bwd")
            dcq = _mm(dq_pre, w_q[l], tb=True, name="q_up_dx")
            g_wq[l] = _mm(lay["cq"], dq_pre, ta=True, out_dtype=BF16, name="q_up_dw")
            dcq_pre, sums_q = _norm_bwd(lay["cq_pre"], _vec(wts["q_norm"][l]), dcq, name="norm_q_bwd")
            g_q_norm[l] = sums_q[2]
            dh1 = _mm(dcq_pre, w_dq[l], tb=True, name="q_down_dx")
            g_wdq[l] = _mm(lay["h1"], dcq_pre, ta=True, out_dtype=BF16, name="q_down_dw")
            dgate_m = sums_gm[0]
        dx, sums_m = _norm_bwd(lay["x0"], _vec(norm_mix[i]), dh1, scale=scale_m, resid=dx1, name="norm_mix_bwd")
        if i == n_a:
            dkv, dkr_pre = _keys_bwd(dk_layers[0], dk_layers[1], dv_layers[0], dv_layers[1], cos_t, sin_bwd, name="keys_bwd")
            dckv = _mm(dkv, w_ukv, tb=True, name="kv_up_dx")
            g_ukv = _mm(kv_side["ckv"], dkv, ta=True, out_dtype=BF16, name="kv_up_dw")
            dckv_pre, sums_kvn = _norm_bwd(kv_side["ckv_pre"], _vec(wts["kv_norm"]), dckv, name="norm_kv_bwd")
            dpre = jnp.concatenate([dckv_pre, dkr_pre], axis=1)
            dh_kv = _mm(dpre, w_dkvkr, tb=True, name="kv_down_dx")
            g_dkvkr = _mm(kv_side["h_kv"], dpre, ta=True, out_dtype=BF16, name="kv_down_dw", tn=kvr + LANES)
            dx, sums_kv = _norm_bwd(lay["x0"], _vec(wts["kv_in_norm"]), dh_kv, scale=kv_scale, resid=dx, name="norm_kv_in_bwd")
            g_ukv = g_ukv.reshape(kvr, heads, 2, NOPE_DIM)
            grad_full.update(
                w_dkv=g_dkvkr[None, :, :kvr],
                w_uk=g_ukv[:, :, 0].reshape(1, kvr, heads * NOPE_DIM),
                w_uv=g_ukv[:, :, 1].reshape(1, kvr, heads * V_DIM),
                w_kr=g_dkvkr[None, :, kvr : kvr + ROPE_DIM],
                w_dq=jnp.stack(g_wdq),
                w_uq=jnp.stack(g_wq).reshape(n_b, qr, heads, HEAD_PAD)[..., : NOPE_DIM + ROPE_DIM].reshape(geoms["w_uq"].shape3),
                w_o=jnp.stack(g_wo),
            )
            attn_theirs = _run(_pair_plan([grad_full[n] for n in attn_names], [geoms[n] for n in attn_names]), name="reduce_pair_attn")
            attn_parts = [_pair_add(grad_full[n], th, geoms[n], core_arr, name="pair_add_" + n) for n, th in zip(attn_names, attn_theirs)]
        dmods[i] = jnp.concatenate([sums_m[0], sums_m[1], dgate_m, sums_f[0], sums_f[1], sums_gf[0]])
        g_norm_mix[i], g_norm_ffn[i] = sums_m[2], sums_f[2]
    grad_x = dx[None]
    grad_full["pool_w"] = jnp.stack(g_pool).reshape(geoms["pool_w"].shape3)

    small_grads = {
        "mod_b": jnp.concatenate(dmods),
        "kv_mod_b": jnp.concatenate([sums_kv[0], sums_kv[1]]),
        "norm_mix": jnp.concatenate(g_norm_mix),
        "norm_ffn": jnp.concatenate(g_norm_ffn),
        "kv_in_norm": sums_kv[2],
        "kv_norm": sums_kvn[2],
        "q_norm": jnp.concatenate(g_q_norm),
        "final_norm": final_stats[0],
    }
    packed = jnp.concatenate([small_grads[n] for n in SMALL] + g_pool_scale)
    small_rows = sum(wts[n].size for n in SMALL) // LANES
    every = _allgather8(_pad_rows(_rows(packed)), name="gather_small_grads")
    summed = _sum_devices(every, name="sum_small_grads")

    mod_rows = depth * nmod // LANES
    dm_all = every[:, :mod_rows].reshape(N_DEV, depth, nmod)
    dm = lax.dynamic_slice_in_dim(dm_all, chip * per, per, axis=2).transpose(1, 0, 2)
    dm = jnp.pad(dm, ((0, 0), (0, 16 - N_DEV), (0, 0)))
    dkvm_all = every[:, mod_rows : mod_rows + nkv // LANES].reshape(N_DEV, nkv)
    dkvm = jnp.pad(lax.dynamic_slice_in_dim(dkvm_all, chip * (nkv // N_CHIPS), nkv // N_CHIPS, axis=1), ((0, 16 - N_DEV), (0, 0)))[None]
    results = {}
    results["mod_w"] = _tp_adamw(sc16, dm, wts["mod_w"], mom["mod_w"], var["mod_w"], name="mod_w_update")
    results["kv_mod_w"] = [
        r[0] for r in _tp_adamw(sc16, dkvm, wts["kv_mod_w"][None], mom["kv_mod_w"][None], var["kv_mod_w"][None], name="kv_mod_w_update")
    ]

    ps_grad = lax.dynamic_slice_in_dim(summed[small_rows : small_rows + n_a * d // LANES].reshape(n_a, d), chip * (d // N_CHIPS), d // N_CHIPS, axis=1)
    small_names = SMALL + ("pool_scale",)

    def pack_small(tree):
        return _pad_rows(jnp.concatenate([_rows(tree[n]) for n in small_names]))

    g_small = _pad_rows(jnp.concatenate([summed[:small_rows], _rows(ps_grad)]))
    small_out = _adamw(pack_small(wts), g_small, pack_small(mom), pack_small(var), name="small_update")
    row = 0
    for n in small_names:
        nrow = wts[n].size // LANES
        results[n] = [r[row : row + nrow].reshape(wts[n].shape) for r in small_out]
        row += nrow

    assert attn_slots is not None
    theirs = _run(_pair_plan([grad_full["pool_w"]], [geoms["pool_w"]]), name="reduce_pair")
    pool_part = _pair_add(grad_full["pool_w"], theirs[0], geoms["pool_w"], core_arr, name="pair_add_pool_w")
    got = _run(_chips_plan([pool_part] + in_flight), name="reduce_chips")
    parts = dict(zip(attn_names, attn_parts), pool_w=pool_part)
    slots = dict(zip(attn_names, attn_slots), pool_w=got[0])
    boths = [_chip_sum(parts[n], slots[n], geoms[n], place_arr, name="chip_sum_" + n) for n in mixer_names]
    sum_chips(0, [0, 1], in_flight, got[1:])
    prefixes = [[()]] * len(mixer_names) + [[(layer,) for layer in range(depth)]] * len(ffn_names)
    joined = _run(_join_plan(boths + ffn_both, prefixes), name="join_pair")
    for n, both in zip(mixer_names + ffn_names, joined):
        cs = geoms[n].cs
        out = _adamw(wts[n].reshape(-1, cs), both.reshape(-1, cs), mom[n].reshape(-1, cs), var[n].reshape(-1, cs), name="update_" + n)
        results[n] = [r.reshape(wts[n].shape) for r in out]

    outs = [loss, grad_x]
    for k in range(4):
        outs += [results[n][k] for n in WEIGHTS]
    return tuple(outs)


def kernel(x, c, positions, mod_w, mod_b, norm_mix, norm_ffn, pool_w, pool_scale, kv_mod_w, kv_mod_b, kv_in_norm, w_dkv, kv_norm, w_uk, w_uv, w_kr, w_dq, q_norm, w_uq, w_o, ffn_gate, ffn_up, ffn_down, final_norm, loss_target, m_mod_w, m_mod_b, m_norm_mix, m_norm_ffn, m_pool_w, m_pool_scale, m_kv_mod_w, m_kv_mod_b, m_kv_in_norm, m_w_dkv, m_kv_norm, m_w_uk, m_w_uv, m_w_kr, m_w_dq, m_q_norm, m_w_uq, m_w_o, m_ffn_gate, m_ffn_up, m_ffn_down, m_final_norm, v_mod_w, v_mod_b, v_norm_mix, v_norm_ffn, v_pool_w, v_pool_scale, v_kv_mod_w, v_kv_mod_b, v_kv_in_norm, v_w_dkv, v_kv_norm, v_w_uk, v_w_uv, v_w_kr, v_w_dq, v_q_norm, v_w_uq, v_w_o, v_ffn_gate, v_ffn_up, v_ffn_down, v_final_norm):
    given = dict(locals())
    wts = {n: given[n] for n in WEIGHTS}
    mom = {n: given["m_" + n] for n in WEIGHTS}
    var = {n: given["v_" + n] for n in WEIGHTS}
    return _step(x, c, positions, loss_target, wts, mom, var)
```
